```python
import jax, jax.numpy as jnp
from jax import lax
import numpy as np

D_MODEL = 2048
BATCH = 8
SEQ = 8192
DEPTH = 1

N_META = 16
MIX = D_MODEL
RET_WIDTH = MIX // 2
RET_HEADS = 4
RET_HEAD_DIM = RET_WIDTH // RET_HEADS
CONV_WIDTH = MIX - RET_WIDTH
CONV_GROUPS = 4
CONV_K = 31
CHUNK = 128
ROPE_BASE = 10000.0
EPS = 1e-6
IN_SPLITS = (RET_WIDTH, RET_WIDTH, RET_WIDTH, RET_WIDTH, CONV_WIDTH, CONV_WIDTH, CONV_WIDTH)
IN_WIDTH = sum(IN_SPLITS)

kernel_name = "hybrid_retention_conformer_block"


def rms_norm(x, g):
    xf = x.astype(jnp.float32)
    y = xf * lax.rsqrt(jnp.mean(xf * xf, axis=-1, keepdims=True) + EPS)
    return (y * g.astype(jnp.float32)).astype(x.dtype)


def rotary(x, pos):
    half = x.shape[-1] // 2
    inv_freq = ROPE_BASE ** (-jnp.arange(half, dtype=jnp.float32) / half)
    ang = pos[:, None] * inv_freq[None, :]
    cos = jnp.cos(ang)[None, :, None, :]
    sin = jnp.sin(ang)[None, :, None, :]
    x1, x2 = x[..., :half], x[..., half:]
    return jnp.concatenate([x1 * cos - x2 * sin, x1 * sin + x2 * cos], axis=-1)


def chunkwise_retention(q, k, v):
    b, l, h, d = q.shape
    n = l // CHUNK
    to_chunks = lambda t: t.reshape(b, n, CHUNK, h, t.shape[-1]).transpose(1, 0, 3, 2, 4)
    qc, kc, vc = to_chunks(q), to_chunks(k), to_chunks(v)
    gamma = 1.0 - jnp.exp2(-5.0 - jnp.arange(h, dtype=jnp.float32))
    log_g = jnp.log(gamma)
    idx = jnp.arange(CHUNK, dtype=jnp.float32)
    rel = idx[:, None] - idx[None, :]
    decay_mask = jnp.where(rel[None] >= 0,
                           jnp.exp(jnp.maximum(rel, 0.0)[None] * log_g[:, None, None]),
                           0.0)
    q_decay = jnp.exp((idx[None, :] + 1.0) * log_g[:, None])
    k_decay = jnp.exp((CHUNK - 1.0 - idx[None, :]) * log_g[:, None])
    chunk_decay = jnp.exp(CHUNK * log_g)

    def step(state, inp):
        qi, ki, vi = inp
        scores = jnp.einsum('bhqd,bhkd->bhqk', qi, ki) * decay_mask[None]
        inner = jnp.einsum('bhqk,bhkv->bhqv', scores, vi)
        cross = jnp.einsum('bhqd,bhdv->bhqv', qi * q_decay[None, :, :, None], state)
        new_state = state * chunk_decay[None, :, None, None] + jnp.einsum(
            'bhkd,bhkv->bhdv', ki * k_decay[None, :, :, None], vi)
        return new_state, inner + cross

    state0 = jnp.zeros((b, h, d, v.shape[-1]), jnp.float32)
    _, out = lax.scan(step, state0, (qc, kc, vc))
    return out.transpose(1, 0, 3, 2, 4).reshape(b, l, h, v.shape[-1])


def retention_group(q_in, k_in, v_in, g_in, gn_g):
    b, l, _ = q_in.shape
    pos = jnp.arange(l, dtype=jnp.float32)
    shp = (b, l, RET_HEADS, RET_HEAD_DIM)
    q = rotary(q_in.astype(jnp.float32).reshape(shp), pos)
    k = rotary(k_in.astype(jnp.float32).reshape(shp), pos) * (RET_HEAD_DIM ** -0.5)
    v = v_in.astype(jnp.float32).reshape(shp)
    lead = (-N_META) % CHUNK
    padw = ((0, 0), (lead, 0), (0, 0), (0, 0))
    y = chunkwise_retention(jnp.pad(q, padw), jnp.pad(k, padw), jnp.pad(v, padw))[:, lead:]
    mu = jnp.mean(y, axis=-1, keepdims=True)
    var = jnp.mean(jnp.square(y - mu), axis=-1, keepdims=True)
    y = ((y - mu) * lax.rsqrt(var + EPS)).reshape(b, l, RET_WIDTH) * gn_g.astype(jnp.float32)
    return (y * jax.nn.silu(g_in.astype(jnp.float32))).astype(q_in.dtype)


def conformer_conv_group(a_in, b_in, g_in, dw_w, dw_b, ln_g, ln_b, pw_w, pw_b):
    u = a_in * jax.nn.sigmoid(b_in)
    u = lax.conv_general_dilated(u, dw_w[:, None, :].astype(u.dtype), window_strides=(1,),
                                 padding=[(CONV_K - 1, 0)],
                                 dimension_numbers=('NWC', 'WIO', 'NWC'),
                                 feature_group_count=CONV_WIDTH) + dw_b
    uf = u.astype(jnp.float32)
    mu = jnp.mean(uf, axis=-1, keepdims=True)
    var = jnp.mean(jnp.square(uf - mu), axis=-1, keepdims=True)
    uf = (uf - mu) * lax.rsqrt(var + EPS) * ln_g.astype(jnp.float32) + ln_b.astype(jnp.float32)
    u = jax.nn.silu(uf).astype(a_in.dtype)
    u = jnp.einsum('blc,cd->bld', u, pw_w) + pw_b
    return u * jax.nn.silu(g_in)


def _fwd_setup_inputs(seed: int = 0) -> dict:
    key = jax.random.key(seed)
    ks = jax.random.split(key, 14)
    f32 = jnp.float32
    nrm = lambda k, s, sc: jax.random.normal(k, s, f32) * sc
    return {
        "x": nrm(ks[0], (BATCH, SEQ, D_MODEL), 1.0),
        "meta_tokens": nrm(ks[1], (N_META, D_MODEL), 1.0),
        "ln_g": 1.0 + nrm(ks[2], (DEPTH, D_MODEL), 0.02),
        "w_in": nrm(ks[3], (DEPTH, D_MODEL, IN_WIDTH), D_MODEL ** -0.5),
        "ret_gn_g": 1.0 + nrm(ks[4], (DEPTH, RET_WIDTH), 0.02),
        "conv_dw_w": nrm(ks[5], (DEPTH, CONV_K, CONV_WIDTH), CONV_K ** -0.5),
        "conv_dw_b": nrm(ks[6], (DEPTH, CONV_WIDTH), 0.01),
        "conv_ln_g": 1.0 + nrm(ks[7], (DEPTH, CONV_WIDTH), 0.02),
        "conv_ln_b": nrm(ks[8], (DEPTH, CONV_WIDTH), 0.01),
        "conv_pw_w": nrm(ks[9], (DEPTH, CONV_WIDTH, CONV_WIDTH), CONV_WIDTH ** -0.5),
        "conv_pw_b": nrm(ks[10], (DEPTH, CONV_WIDTH), 0.01),
        "w_out": nrm(ks[11], (DEPTH, MIX, D_MODEL), MIX ** -0.5),
        "final_g": 1.0 + nrm(ks[12], (D_MODEL,), 0.02),
    }


def _fwd_reference(x, meta_tokens, ln_g, w_in, ret_gn_g, conv_dw_w, conv_dw_b, conv_ln_g,
              conv_ln_b, conv_pw_w, conv_pw_b, w_out, final_g):
    b = x.shape[0]
    meta = jnp.broadcast_to(meta_tokens[None].astype(x.dtype), (b, N_META, D_MODEL))
    h = jnp.concatenate([meta, x], axis=1)
    offs = np.cumsum(IN_SPLITS)[:-1].tolist()
    for l in range(DEPTH):
        hn = rms_norm(h, ln_g[l])
        proj = jnp.einsum('bld,de->ble', hn, w_in[l])
        q_in, k_in, v_in, gr_in, a_in, b_in, gc_in = jnp.split(proj, offs, axis=-1)
        y_ret = retention_group(q_in, k_in, v_in, gr_in, ret_gn_g[l])
        y_conv = conformer_conv_group(a_in, b_in, gc_in, conv_dw_w[l], conv_dw_b[l],
                                      conv_ln_g[l], conv_ln_b[l], conv_pw_w[l], conv_pw_b[l])
        y = jnp.concatenate([y_ret, y_conv], axis=-1)
        h = h + jnp.einsum('ble,ed->bld', y, w_out[l])
    h = rms_norm(h, final_g)
    return h[:, N_META:]


import jax as _jax
import jax.numpy as _jnp

TWIN_FORMAT = 'train_step'
FWD_PARAMS = ['x', 'meta_tokens', 'ln_g', 'w_in', 'ret_gn_g', 'conv_dw_w', 'conv_dw_b', 'conv_ln_g', 'conv_ln_b', 'conv_pw_w', 'conv_pw_b', 'w_out', 'final_g']
TWIN_WEIGHTS = ['meta_tokens', 'ln_g', 'w_in', 'ret_gn_g', 'conv_dw_w', 'conv_dw_b', 'conv_ln_g', 'conv_ln_b', 'conv_pw_w', 'conv_pw_b', 'w_out', 'final_g']
TWIN_DIFF_INPUT = 'x'
TWIN_INPUTS = ['x', 'meta_tokens', 'ln_g', 'w_in', 'ret_gn_g', 'conv_dw_w', 'conv_dw_b', 'conv_ln_g', 'conv_ln_b', 'conv_pw_w', 'conv_pw_b', 'w_out', 'final_g', 'loss_target', 'm_meta_tokens', 'm_ln_g', 'm_w_in', 'm_ret_gn_g', 'm_conv_dw_w', 'm_conv_dw_b', 'm_conv_ln_g', 'm_conv_ln_b', 'm_conv_pw_w', 'm_conv_pw_b', 'm_w_out', 'm_final_g', 'v_meta_tokens', 'v_ln_g', 'v_w_in', 'v_ret_gn_g', 'v_conv_dw_w', 'v_conv_dw_b', 'v_conv_ln_g', 'v_conv_ln_b', 'v_conv_pw_w', 'v_conv_pw_b', 'v_w_out', 'v_final_g']
TWIN_OUTPUTS = ['loss', 'grad_x', 'grad_meta_tokens', 'grad_ln_g', 'grad_w_in', 'grad_ret_gn_g', 'grad_conv_dw_w', 'grad_conv_dw_b', 'grad_conv_ln_g', 'grad_conv_ln_b', 'grad_conv_pw_w', 'grad_conv_pw_b', 'grad_w_out', 'grad_final_g', 'delta_meta_tokens', 'delta_ln_g', 'delta_w_in', 'delta_ret_gn_g', 'delta_conv_dw_w', 'delta_conv_dw_b', 'delta_conv_ln_g', 'delta_conv_ln_b', 'delta_conv_pw_w', 'delta_conv_pw_b', 'delta_w_out', 'delta_final_g', 'new_m_meta_tokens', 'new_m_ln_g', 'new_m_w_in', 'new_m_ret_gn_g', 'new_m_conv_dw_w', 'new_m_conv_dw_b', 'new_m_conv_ln_g', 'new_m_conv_ln_b', 'new_m_conv_pw_w', 'new_m_conv_pw_b', 'new_m_w_out', 'new_m_final_g', 'new_v_meta_tokens', 'new_v_ln_g', 'new_v_w_in', 'new_v_ret_gn_g', 'new_v_conv_dw_w', 'new_v_conv_dw_b', 'new_v_conv_ln_g', 'new_v_conv_ln_b', 'new_v_conv_pw_w', 'new_v_conv_pw_b', 'new_v_w_out', 'new_v_final_g']
TWIN_LEAF_KINDS = {'loss': 'loss', 'grad_x': 'grad_x', 'grad_meta_tokens': 'grad_w', 'grad_ln_g': 'grad_w', 'grad_w_in': 'grad_w', 'grad_ret_gn_g': 'grad_w', 'grad_conv_dw_w': 'grad_w', 'grad_conv_dw_b': 'grad_w', 'grad_conv_ln_g': 'grad_w', 'grad_conv_ln_b': 'grad_w', 'grad_conv_pw_w': 'grad_w', 'grad_conv_pw_b': 'grad_w', 'grad_w_out': 'grad_w', 'grad_final_g': 'grad_w', 'delta_meta_tokens': 'delta_w', 'delta_ln_g': 'delta_w', 'delta_w_in': 'delta_w', 'delta_ret_gn_g': 'delta_w', 'delta_conv_dw_w': 'delta_w', 'delta_conv_dw_b': 'delta_w', 'delta_conv_ln_g': 'delta_w', 'delta_conv_ln_b': 'delta_w', 'delta_conv_pw_w': 'delta_w', 'delta_conv_pw_b': 'delta_w', 'delta_w_out': 'delta_w', 'delta_final_g': 'delta_w', 'new_m_meta_tokens': 'new_m', 'new_m_ln_g': 'new_m', 'new_m_w_in': 'new_m', 'new_m_ret_gn_g': 'new_m', 'new_m_conv_dw_w': 'new_m', 'new_m_conv_dw_b': 'new_m', 'new_m_conv_ln_g': 'new_m', 'new_m_conv_ln_b': 'new_m', 'new_m_conv_pw_w': 'new_m', 'new_m_conv_pw_b': 'new_m', 'new_m_w_out': 'new_m', 'new_m_final_g': 'new_m', 'new_v_meta_tokens': 'new_v', 'new_v_ln_g': 'new_v', 'new_v_w_in': 'new_v', 'new_v_ret_gn_g': 'new_v', 'new_v_conv_dw_w': 'new_v', 'new_v_conv_dw_b': 'new_v', 'new_v_conv_ln_g': 'new_v', 'new_v_conv_ln_b': 'new_v', 'new_v_conv_pw_w': 'new_v', 'new_v_conv_pw_b': 'new_v', 'new_v_w_out': 'new_v', 'new_v_final_g': 'new_v'}


def _forward(args):
    return _fwd_reference(*[args[k] for k in FWD_PARAMS])


def _output_shape():
    def fwd():
        inp = _fwd_setup_inputs(0)
        return _fwd_reference(*[inp[k] for k in FWD_PARAMS])
    out = _jax.eval_shape(fwd)
    return out.shape, out.dtype

N_MICROBATCH = 1
ADAM_LR = 0.001
ADAM_B1 = 0.9
ADAM_B2 = 0.999
ADAM_EPS = 1e-08
ADAM_WD = 0.01
ADAM_STEP = 10
PER_EXAMPLE_BATCH_AXIS = {'x': 0, 'loss_target': 0}
SHARED_INPUTS = []
_WEIGHT_DTYPES = {'meta_tokens': _jnp.float32, 'ln_g': _jnp.float32, 'w_in': _jnp.float32, 'ret_gn_g': _jnp.float32, 'conv_dw_w': _jnp.float32, 'conv_dw_b': _jnp.float32, 'conv_ln_g': _jnp.float32, 'conv_ln_b': _jnp.float32, 'conv_pw_w': _jnp.float32, 'conv_pw_b': _jnp.float32, 'w_out': _jnp.float32, 'final_g': _jnp.float32}
MOMENT_SCALE = {'meta_tokens': 5.393394e-03, 'ln_g': 1.077247e-01, 'w_in': 5.636724e-02, 'ret_gn_g': 6.588978e-02, 'conv_dw_w': 4.170986e-02, 'conv_dw_b': 7.823477e-02, 'conv_ln_g': 4.853903e-02, 'conv_ln_b': 4.186257e-02, 'conv_pw_w': 4.024315e-02, 'conv_pw_b': 6.806505e-02, 'w_out': 5.547650e-02, 'final_g': 3.196588e+01}


def _to_microbatches(a, axis):
    t = _jnp.moveaxis(a, axis, 0)
    t = t.reshape((N_MICROBATCH, t.shape[0] // N_MICROBATCH) + t.shape[1:])
    return _jnp.moveaxis(t, 1, axis + 1)


def setup_inputs(seed: int = 0) -> dict:
    inp = _fwd_setup_inputs(seed)
    key = _jax.random.fold_in(_jax.random.key(seed), 7919)
    shape, _ = _output_shape()
    out = dict(inp)
    out["loss_target"] = _jax.random.normal(_jax.random.fold_in(key, 0), shape, _jnp.float32)
    for i, name in enumerate(TWIN_WEIGHTS):
        w = inp[name].astype(_jnp.float32)
        if MOMENT_SCALE is None:
            s = _jnp.sqrt(_jnp.mean(_jnp.square(w)) + 1e-30)
        else:
            s = MOMENT_SCALE[name]
        km, kv = _jax.random.split(_jax.random.fold_in(key, i + 1))
        out[name] = w
        out["m_" + name] = s * _jax.random.normal(km, w.shape, _jnp.float32)
        out["v_" + name] = (s * s) * _jax.random.uniform(kv, w.shape, _jnp.float32, 0.5, 1.5)
    if N_MICROBATCH > 1:
        for name, axis in PER_EXAMPLE_BATCH_AXIS.items():
            out[name] = _to_microbatches(out[name], axis)
    return {'x': out['x'], 'meta_tokens': out['meta_tokens'], 'ln_g': out['ln_g'], 'w_in': out['w_in'], 'ret_gn_g': out['ret_gn_g'], 'conv_dw_w': out['conv_dw_w'], 'conv_dw_b': out['conv_dw_b'], 'conv_ln_g': out['conv_ln_g'], 'conv_ln_b': out['conv_ln_b'], 'conv_pw_w': out['conv_pw_w'], 'conv_pw_b': out['conv_pw_b'], 'w_out': out['w_out'], 'final_g': out['final_g'], 'loss_target': out['loss_target'], 'm_meta_tokens': out['m_meta_tokens'], 'm_ln_g': out['m_ln_g'], 'm_w_in': out['m_w_in'], 'm_ret_gn_g': out['m_ret_gn_g'], 'm_conv_dw_w': out['m_conv_dw_w'], 'm_conv_dw_b': out['m_conv_dw_b'], 'm_conv_ln_g': out['m_conv_ln_g'], 'm_conv_ln_b': out['m_conv_ln_b'], 'm_conv_pw_w': out['m_conv_pw_w'], 'm_conv_pw_b': out['m_conv_pw_b'], 'm_w_out': out['m_w_out'], 'm_final_g': out['m_final_g'], 'v_meta_tokens': out['v_meta_tokens'], 'v_ln_g': out['v_ln_g'], 'v_w_in': out['v_w_in'], 'v_ret_gn_g': out['v_ret_gn_g'], 'v_conv_dw_w': out['v_conv_dw_w'], 'v_conv_dw_b': out['v_conv_dw_b'], 'v_conv_ln_g': out['v_conv_ln_g'], 'v_conv_ln_b': out['v_conv_ln_b'], 'v_conv_pw_w': out['v_conv_pw_w'], 'v_conv_pw_b': out['v_conv_pw_b'], 'v_w_out': out['v_w_out'], 'v_final_g': out['v_final_g']}


def _loss(weights, diff, rest, loss_target):
    with _jax.named_scope("forward"):
        args = {**rest, TWIN_DIFF_INPUT: diff, **{k: w.astype(_WEIGHT_DTYPES[k]) for k, w in weights.items()}}
        y = _forward(args)
    with _jax.named_scope("loss_head"):
        err = _jnp.square(y.astype(_jnp.float32) - loss_target)
        return 0.5 * _jnp.sum(_jnp.mean(err, axis=-1)) if err.ndim else 0.5 * err


def _adamw(w, g, m, v):
    m = ADAM_B1 * m + (1.0 - ADAM_B1) * g
    v = ADAM_B2 * v + (1.0 - ADAM_B2) * _jnp.square(g)
    m_hat = m / (1.0 - ADAM_B1 ** ADAM_STEP)
    v_hat = v / (1.0 - ADAM_B2 ** ADAM_STEP)
    delta = -ADAM_LR * (m_hat / (_jnp.sqrt(v_hat) + ADAM_EPS) + ADAM_WD * w)
    return delta, m, v


def reference(x, meta_tokens, ln_g, w_in, ret_gn_g, conv_dw_w, conv_dw_b, conv_ln_g, conv_ln_b, conv_pw_w, conv_pw_b, w_out, final_g, loss_target, m_meta_tokens, m_ln_g, m_w_in, m_ret_gn_g, m_conv_dw_w, m_conv_dw_b, m_conv_ln_g, m_conv_ln_b, m_conv_pw_w, m_conv_pw_b, m_w_out, m_final_g, v_meta_tokens, v_ln_g, v_w_in, v_ret_gn_g, v_conv_dw_w, v_conv_dw_b, v_conv_ln_g, v_conv_ln_b, v_conv_pw_w, v_conv_pw_b, v_w_out, v_final_g):
    given = dict(x=x, meta_tokens=meta_tokens, ln_g=ln_g, w_in=w_in, ret_gn_g=ret_gn_g, conv_dw_w=conv_dw_w, conv_dw_b=conv_dw_b, conv_ln_g=conv_ln_g, conv_ln_b=conv_ln_b, conv_pw_w=conv_pw_w, conv_pw_b=conv_pw_b, w_out=w_out, final_g=final_g, loss_target=loss_target, m_meta_tokens=m_meta_tokens, m_ln_g=m_ln_g, m_w_in=m_w_in, m_ret_gn_g=m_ret_gn_g, m_conv_dw_w=m_conv_dw_w, m_conv_dw_b=m_conv_dw_b, m_conv_ln_g=m_conv_ln_g, m_conv_ln_b=m_conv_ln_b, m_conv_pw_w=m_conv_pw_w, m_conv_pw_b=m_conv_pw_b, m_w_out=m_w_out, m_final_g=m_final_g, v_meta_tokens=v_meta_tokens, v_ln_g=v_ln_g, v_w_in=v_w_in, v_ret_gn_g=v_ret_gn_g, v_conv_dw_w=v_conv_dw_w, v_conv_dw_b=v_conv_dw_b, v_conv_ln_g=v_conv_ln_g, v_conv_ln_b=v_conv_ln_b, v_conv_pw_w=v_conv_pw_w, v_conv_pw_b=v_conv_pw_b, v_w_out=v_w_out, v_final_g=v_final_g)
    weights = {n: given[n] for n in TWIN_WEIGHTS}
    shared = {n: given[n] for n in SHARED_INPUTS}
    per_example = {n: given[n] for n in ['x']}
    grad_fn = _jax.value_and_grad(_loss, argnums=(0, 1))

    def one_microbatch(ex, loss_target):
        ex = dict(ex)
        diff = ex.pop(TWIN_DIFF_INPUT)
        return grad_fn(weights, diff, {**shared, **ex}, loss_target)

    if N_MICROBATCH == 1:
        loss, (grad_w, grad_x) = one_microbatch(per_example, given["loss_target"])
    else:
        def body(carry, xs):
            loss_sum, grad_sum = carry
            l_k, (gw_k, gx_k) = one_microbatch(xs[0], xs[1])
            with _jax.named_scope("update"):
                return (loss_sum + l_k, _jax.tree.map(_jnp.add, grad_sum, gw_k)), gx_k

        init = (_jnp.zeros((), _jnp.float32), _jax.tree.map(_jnp.zeros_like, weights))
        (loss, grad_w), grad_x = _jax.lax.scan(body, init, (per_example, given["loss_target"]))
    with _jax.named_scope("update"):
        delta_w, new_m, new_v = {}, {}, {}
        for n in TWIN_WEIGHTS:
            delta_w[n], new_m[n], new_v[n] = _adamw(weights[n], grad_w[n], given["m_" + n], given["v_" + n])
    return (loss, grad_x, *[grad_w[n] for n in TWIN_WEIGHTS], *[delta_w[n] for n in TWIN_WEIGHTS],
            *[new_m[n] for n in TWIN_WEIGHTS], *[new_v[n] for n in TWIN_WEIGHTS])
```

```python
import functools

import jax
import jax.numpy as jnp
from jax import lax
from jax.experimental import pallas as pl
from jax.experimental.pallas import tpu as pltpu

F32 = jnp.float32
BF16 = jnp.bfloat16
MESH = pl.DeviceIdType.MESH

N_META = 16
CHUNK = 128
LEAD = (-N_META) % CHUNK
RET_HEADS = 4
CONV_K = 31
HALO = 32
ROPE_BASE = 10000.0
EPS = 1e-6
N_CHIPS = 4
N_DEV = 8

ADAM_LR = 0.001
ADAM_B1 = 0.9
ADAM_B2 = 0.999
ADAM_EPS = 1e-08
ADAM_WD = 0.01
ADAM_STEP = 10

MIB = 2 ** 20


def _params(n_grid_axes, vmem_mib):
    return pltpu.CompilerParams(dimension_semantics=("arbitrary",) * n_grid_axes,
                                vmem_limit_bytes=vmem_mib * MIB)


def _row_tile(n, pref):
    for t in (1664, 1280, 1024, 640, 512, 384, 320, 256, 128, 64, 32, 16, 8):
        if t <= pref and n % t == 0:
            return t
    raise ValueError(f"no row tile for {n}")


def _dot(a, b):
    return jnp.dot(a, b, preferred_element_type=F32)


def _dot_nt(a, b):
    return lax.dot_general(a, b, (((1,), (1,)), ((), ())), preferred_element_type=F32)


def _dot_tn(a, b):
    return lax.dot_general(a, b, (((0,), (0,)), ((), ())), preferred_element_type=F32)


def _sigmoid(x):
    return jax.nn.sigmoid(x)


def _dsilu(x, s):
    return s * (1.0 + x * (1.0 - s))


def _mean(x):
    return jnp.mean(x, axis=-1, keepdims=True)


def _colsum(x):
    return jnp.sum(x, axis=0, keepdims=True)


def _rope(x, cos, sin):
    half = x.shape[-1] // 2
    x1, x2 = x[:, :half], x[:, half:]
    return jnp.concatenate([x1 * cos - x2 * sin, x1 * sin + x2 * cos], axis=-1)


def _rope_t(d, cos, sin):
    half = d.shape[-1] // 2
    d1, d2 = d[:, :half], d[:, half:]
    return jnp.concatenate([d1 * cos + d2 * sin, d2 * cos - d1 * sin], axis=-1)


def _adamw(w, g, m, v):
    m = ADAM_B1 * m + (1.0 - ADAM_B1) * g
    v = ADAM_B2 * v + (1.0 - ADAM_B2) * (g * g)
    m_hat = m / (1.0 - ADAM_B1 ** ADAM_STEP)
    v_hat = v / (1.0 - ADAM_B2 ** ADAM_STEP)
    delta = -ADAM_LR * (m_hat / (jnp.sqrt(v_hat) + ADAM_EPS) + ADAM_WD * w)
    return delta, m, v


def _ret_consts():
    h = jnp.arange(RET_HEADS, dtype=F32)
    log_g = jnp.log(1.0 - jnp.exp2(-5.0 - h))
    idx = jnp.arange(CHUNK, dtype=F32)
    rel = idx[:, None] - idx[None, :]
    dmask = jnp.where(rel[None] >= 0, jnp.exp(jnp.maximum(rel, 0.0)[None] * log_g[:, None, None]), 0.0)
    qd = jnp.exp((idx[None, :] + 1.0) * log_g[:, None])[:, :, None]
    kd = jnp.exp((CHUNK - 1.0 - idx[None, :]) * log_g[:, None])[:, :, None]
    cd = jnp.exp(CHUNK * log_g)[:, None, None]
    return dmask, qd, kd, cd


def _rope_tables(n_rows, half):
    pos = jnp.arange(n_rows, dtype=F32) - float(LEAD)
    inv_freq = ROPE_BASE ** (-jnp.arange(half, dtype=F32) / half)
    ang = pos[:, None] * inv_freq[None, :]
    return jnp.cos(ang), jnp.sin(ang)


def _cast_bf16(a, name):
    rows, cols = a.shape
    tr = _row_tile(rows, 256)

    def body(a_ref, o_ref):
        o_ref[...] = a_ref[...].astype(BF16)

    return pl.pallas_call(
        body, name=name, grid=(rows // tr,),
        in_specs=[pl.BlockSpec((tr, cols), lambda i: (i, 0))],
        out_specs=pl.BlockSpec((tr, cols), lambda i: (i, 0)),
        out_shape=jax.ShapeDtypeStruct((rows, cols), BF16),
        compiler_params=_params(1, 32))(a)


def _in_proj(hp, ln_g, w4):
    lp, d = hp.shape
    ns = w4.shape[2]
    tm = _row_tile(lp, 640)

    def body(h_ref, g_ref, w_ref, proj_ref, hn_ref, r_ref, hn_s):
        @pl.when(pl.program_id(1) == 0)
        def _():
            h = h_ref[...]
            r = lax.rsqrt(_mean(h * h) + EPS)
            hn = ((h * r) * g_ref[...]).astype(BF16)
            hn_s[...] = hn
            hn_ref[...] = hn
            r_ref[...] = r

        proj_ref[...] = _dot(hn_s[...], w_ref[0])

    return pl.pallas_call(
        body, name="in_proj", grid=(lp // tm, N_CHIPS),
        in_specs=[pl.BlockSpec((tm, d), lambda i, s: (i, 0)),
                  pl.BlockSpec((1, d), lambda i, s: (0, 0)),
                  pl.BlockSpec((1, d, ns), lambda i, s: (s, 0, 0))],
        out_specs=[pl.BlockSpec((tm, ns), lambda i, s: (i, s)),
                   pl.BlockSpec((tm, d), lambda i, s: (i, 0)),
                   pl.BlockSpec((tm, 1), lambda i, s: (i, 0))],
        out_shape=[jax.ShapeDtypeStruct((lp, N_CHIPS * ns), F32),
                   jax.ShapeDtypeStruct((lp, d), BF16),
                   jax.ShapeDtypeStruct((lp, 1), F32)],
        scratch_shapes=[pltpu.VMEM((tm, d), BF16)],
        compiler_params=_params(2, 48))(hp, ln_g, w4)


def _ret_fwd(proj, cos, sin, gn_g, consts):
    lp = proj.shape[0]
    w = gn_g.shape[1]
    hd = w // RET_HEADS
    nch = lp // CHUNK
    dmask, qd, kd, cd = consts

    def body(q_ref, k_ref, v_ref, g_ref, cos_ref, sin_ref, gn_ref, m_ref, qd_ref, kd_ref, cd_ref,
             y_ref, ssave_ref, s_scr):
        @pl.when(pl.program_id(0) == 0)
        def _():
            s_scr[...] = jnp.zeros_like(s_scr)

        cos_t, sin_t = cos_ref[...], sin_ref[...]
        for h in range(RET_HEADS):
            sl = slice(h * hd, (h + 1) * hd)
            qr = _rope(q_ref[:, sl], cos_t, sin_t)
            kr = _rope(k_ref[:, sl], cos_t, sin_t) * (hd ** -0.5)
            vb = v_ref[:, sl].astype(BF16)
            sc = _dot_nt(qr.astype(BF16), kr.astype(BF16)) * m_ref[h]
            state = s_scr[h]
            sb = state.astype(BF16)
            ssave_ref[0, h] = sb
            out = _dot(sc.astype(BF16), vb) + _dot((qr * qd_ref[h]).astype(BF16), sb)
            s_scr[h] = state * cd_ref[h] + _dot_tn((kr * kd_ref[h]).astype(BF16), vb)
            dev = out - _mean(out)
            yn = dev * lax.rsqrt(_mean(dev * dev) + EPS)
            g = g_ref[:, sl]
            y_ref[:, sl] = ((yn * gn_ref[:, sl]) * (g * _sigmoid(g))).astype(BF16)

    def col(j):
        return pl.BlockSpec((CHUNK, w), lambda i: (i, j))

    def whole(a):
        return pl.BlockSpec(a.shape, lambda i: (0,) * a.ndim)

    return pl.pallas_call(
        body, name="ret_fwd", grid=(nch,),
        in_specs=[col(0), col(1), col(2), col(3),
                  pl.BlockSpec((CHUNK, hd // 2), lambda i: (i, 0)),
                  pl.BlockSpec((CHUNK, hd // 2), lambda i: (i, 0)),
                  whole(gn_g), whole(dmask), whole(qd), whole(kd), whole(cd)],
        out_specs=[pl.BlockSpec((CHUNK, w), lambda i: (i, 0)),
                   pl.BlockSpec((1, RET_HEADS, hd, hd), lambda i: (i, 0, 0, 0))],
        out_shape=[jax.ShapeDtypeStruct((lp, w), BF16),
                   jax.ShapeDtypeStruct((nch, RET_HEADS, hd, hd), BF16)],
        scratch_shapes=[pltpu.VMEM((RET_HEADS, hd, hd), F32)],
        compiler_params=_params(1, 32))(proj, proj, proj, proj, cos, sin, gn_g, dmask, qd, kd, cd)


def _dw_taps(src_ref, w_ref, dst_ref, bias, *, rows, start, flip, rb):
    cw = dst_ref.shape[1]
    lb = min(128, cw)

    def rb_body(r, carry):
        base = pl.multiple_of(r * rb, rb)
        for cb in range(cw // lb):
            ls = slice(cb * lb, (cb + 1) * lb)
            win = src_ref[pl.ds(base, rb + HALO), ls]
            acc = jnp.zeros((rb, lb), F32) if bias is None else jnp.broadcast_to(bias[:, ls], (rb, lb))
            for j in range(CONV_K):
                o = start + (CONV_K - 1 - j if flip else j)
                acc = acc + win[o:o + rb, :] * w_ref[j:j + 1, ls]
            dst_ref[pl.ds(base, rb), ls] = acc
        return carry

    lax.fori_loop(0, rows // rb, rb_body, 0)


def _conv_fwd(proj, dw_w, dw_b, cln_g, cln_b, pw_w, pw_b):
    lp = proj.shape[0]
    cw = dw_b.shape[1]
    tm = _row_tile(lp, 640)
    rb = _row_tile(tm, 64)

    def body(a_ref, b_ref, gc_ref, w_ref, wb_ref, lg_ref, lb_ref, pw_ref, pb_ref, y_ref, u1_ref, buf):
        @pl.when(pl.program_id(0) == 0)
        def _():
            buf[0:HALO, :] = jnp.zeros((HALO, cw), F32)

        buf[HALO:HALO + tm, :] = a_ref[...] * _sigmoid(b_ref[...])
        _dw_taps(buf, w_ref, u1_ref, wb_ref[...], rows=tm, start=HALO - (CONV_K - 1), flip=False, rb=rb)
        buf[0:HALO, :] = buf[tm:tm + HALO, :]
        u1 = u1_ref[...]
        dev = u1 - _mean(u1)
        z = dev * lax.rsqrt(_mean(dev * dev) + EPS) * lg_ref[...] + lb_ref[...]
        u3 = (z * _sigmoid(z)).astype(BF16)
        u4 = _dot(u3, pw_ref[...]) + pb_ref[...]
        gc = gc_ref[...]
        y_ref[...] = (u4 * (gc * _sigmoid(gc))).astype(BF16)

    def col(j):
        return pl.BlockSpec((tm, cw), lambda i: (i, j))

    def whole(a):
        return pl.BlockSpec(a.shape, lambda i: (0,) * a.ndim)

    return pl.pallas_call(
        body, name="conv_fwd", grid=(lp // tm,),
        in_specs=[col(4), col(5), col(6), whole(dw_w), whole(dw_b), whole(cln_g), whole(cln_b),
                  whole(pw_w), whole(pw_b)],
        out_specs=[pl.BlockSpec((tm, cw), lambda i: (i, 0)), pl.BlockSpec((tm, cw), lambda i: (i, 0))],
        out_shape=[jax.ShapeDtypeStruct((lp, cw), BF16), jax.ShapeDtypeStruct((lp, cw), F32)],
        scratch_shapes=[pltpu.VMEM((tm + HALO, cw), F32)],
        compiler_params=_params(1, 48))(proj, proj, proj, dw_w, dw_b, cln_g, cln_b, pw_w, pw_b)


def _out_proj_loss(yr, yc, hp, tgtp, w_out, final_g):
    lp, d = hp.shape
    w = yr.shape[1]
    tm = _row_tile(lp, 320)

    def body(yr_ref, yc_ref, h_ref, t_ref, w_ref, fg_ref, dh2_ref, dy_ref, loss_ref, dfg_ref):
        i = pl.program_id(0)

        @pl.when(i == 0)
        def _():
            loss_ref[...] = jnp.zeros_like(loss_ref)
            dfg_ref[...] = jnp.zeros_like(dfg_ref)

        h2 = h_ref[...] + (_dot(yr_ref[...], w_ref[0:w, :]) + _dot(yc_ref[...], w_ref[w:2 * w, :]))
        r2 = lax.rsqrt(_mean(h2 * h2) + EPS)
        h2n = h2 * r2
        fg = fg_ref[...]
        rows = i * tm + lax.broadcasted_iota(jnp.int32, (tm, 1), 0)
        err = jnp.where(rows >= CHUNK, h2n * fg - t_ref[...], 0.0)
        loss_ref[...] += _colsum(err * err)
        dout = err * (1.0 / d)
        dfg_ref[...] += _colsum(dout * h2n)
        dz = dout * fg
        dh2 = r2 * (dz - h2n * _mean(dz * h2n))
        dh2_ref[...] = dh2
        db = dh2.astype(BF16)
        dy_ref[:, 0:w] = _dot_nt(db, w_ref[0:w, :])
        dy_ref[:, w:2 * w] = _dot_nt(db, w_ref[w:2 * w, :])

    def row(cols):
        return pl.BlockSpec((tm, cols), lambda i: (i, 0))

    return pl.pallas_call(
        body, name="out_proj_loss", grid=(lp // tm,),
        in_specs=[row(w), row(w), row(d), row(d),
                  pl.BlockSpec(memory_space=pltpu.VMEM),
                  pl.BlockSpec((1, d), lambda i: (0, 0))],
        out_specs=[row(d), row(2 * w), pl.BlockSpec((1, d), lambda i: (0, 0)),
                   pl.BlockSpec((1, d), lambda i: (0, 0))],
        out_shape=[jax.ShapeDtypeStruct((lp, d), F32), jax.ShapeDtypeStruct((lp, 2 * w), F32),
                   jax.ShapeDtypeStruct((1, d), F32), jax.ShapeDtypeStruct((1, d), F32)],
        compiler_params=_params(1, 56))(yr, yc, hp, tgtp, w_out, final_g)


def _dw_out(yr, yc, dh2):
    lp, d = dh2.shape
    w = yr.shape[1]
    tm = _row_tile(lp, 640)
    nb = 2
    dn = d // nb

    def body(yr_ref, yc_ref, d_ref, o_ref):
        @pl.when(pl.program_id(1) == 0)
        def _():
            o_ref[...] = jnp.zeros_like(o_ref)

        db = d_ref[...].astype(BF16)
        o_ref[0:w, :] += _dot_tn(yr_ref[...], db)
        o_ref[w:2 * w, :] += _dot_tn(yc_ref[...], db)

    return pl.pallas_call(
        body, name="dw_out", grid=(nb, lp // tm),
        in_specs=[pl.BlockSpec((tm, w), lambda n, i: (i, 0)),
                  pl.BlockSpec((tm, w), lambda n, i: (i, 0)),
                  pl.BlockSpec((tm, dn), lambda n, i: (i, n))],
        out_specs=pl.BlockSpec((2 * w, dn), lambda n, i: (0, n)),
        out_shape=jax.ShapeDtypeStruct((2 * w, d), F32),
        compiler_params=_params(2, 48))(yr, yc, dh2)


def _ret_bwd(proj, dy, ssave, cos, sin, gn_g, consts):
    lp = proj.shape[0]
    w = gn_g.shape[1]
    hd = w // RET_HEADS
    nch = lp // CHUNK
    dmask, qd, kd, cd = consts

    def body(q_ref, k_ref, v_ref, g_ref, dy_ref, ss_ref, cos_ref, sin_ref, gn_ref, m_ref, qd_ref, kd_ref,
             cd_ref, dp_ref, dgn_ref, ds_scr):
        @pl.when(pl.program_id(0) == 0)
        def _():
            ds_scr[...] = jnp.zeros_like(ds_scr)
            dgn_ref[...] = jnp.zeros_like(dgn_ref)

        cos_t, sin_t = cos_ref[...], sin_ref[...]
        for h in range(RET_HEADS):
            sl = slice(h * hd, (h + 1) * hd)
            qr = _rope(q_ref[:, sl], cos_t, sin_t)
            kr = _rope(k_ref[:, sl], cos_t, sin_t) * (hd ** -0.5)
            qb, kb = qr.astype(BF16), kr.astype(BF16)
            vb = v_ref[:, sl].astype(BF16)
            sb = ss_ref[0, h]
            mask = m_ref[h]
            qdec, kdec = qd_ref[h], kd_ref[h]
            scb = (_dot_nt(qb, kb) * mask).astype(BF16)
            qdq = (qr * qdec).astype(BF16)
            kdk = (kr * kdec).astype(BF16)
            out = _dot(scb, vb) + _dot(qdq, sb)
            dev = out - _mean(out)
            rstd = lax.rsqrt(_mean(dev * dev) + EPS)
            yn = dev * rstd
            g = g_ref[:, sl]
            sg = _sigmoid(g)
            gng = gn_ref[:, sl]
            dyv = dy_ref[:, sl]
            dgr = dyv * (yn * gng) * _dsilu(g, sg)
            silu_g = g * sg
            dgn_ref[:, sl] += _colsum(dyv * yn * silu_g)
            dyn = dyv * gng * silu_g
            dout = rstd * (dyn - _mean(dyn) - yn * _mean(dyn * yn))
            dob = dout.astype(BF16)
            dscb = (_dot_nt(dob, vb) * mask).astype(BF16)
            dstate = ds_scr[h]
            dsb = dstate.astype(BF16)
            dq = _dot(dscb, kb) + _dot_nt(dob, sb) * qdec
            dk = _dot_tn(dscb, qb) + _dot_nt(vb, dsb) * kdec
            dv = _dot_tn(scb, dob) + _dot(kdk, dsb)
            ds_scr[h] = dstate * cd_ref[h] + _dot_tn(qdq, dob)
            dp_ref[:, 0 * w + h * hd:0 * w + (h + 1) * hd] = _rope_t(dq, cos_t, sin_t).astype(BF16)
            dp_ref[:, 1 * w + h * hd:1 * w + (h + 1) * hd] = (_rope_t(dk, cos_t, sin_t) * (hd ** -0.5)).astype(BF16)
            dp_ref[:, 2 * w + h * hd:2 * w + (h + 1) * hd] = dv.astype(BF16)
            dp_ref[:, 3 * w + h * hd:3 * w + (h + 1) * hd] = dgr.astype(BF16)

    def rev(i):
        return nch - 1 - i

    def col(j):
        return pl.BlockSpec((CHUNK, w), lambda i: (rev(i), j))

    def whole(a):
        return pl.BlockSpec(a.shape, lambda i: (0,) * a.ndim)

    return pl.pallas_call(
        body, name="ret_bwd", grid=(nch,),
        in_specs=[col(0), col(1), col(2), col(3),
                  pl.BlockSpec((CHUNK, w), lambda i: (rev(i), 0)),
                  pl.BlockSpec((1, RET_HEADS, hd, hd), lambda i: (rev(i), 0, 0, 0)),
                  pl.BlockSpec((CHUNK, hd // 2), lambda i: (rev(i), 0)),
                  pl.BlockSpec((CHUNK, hd // 2), lambda i: (rev(i), 0)),
                  whole(gn_g), whole(dmask), whole(qd), whole(kd), whole(cd)],
        out_specs=[pl.BlockSpec((CHUNK, 4 * w), lambda i: (rev(i), 0)),
                   pl.BlockSpec((1, w), lambda i: (0, 0))],
        out_shape=[jax.ShapeDtypeStruct((lp, 7 * w), BF16), jax.ShapeDtypeStruct((1, w), F32)],
        scratch_shapes=[pltpu.VMEM((RET_HEADS, hd, hd), F32)],
        compiler_params=_params(1, 32))(proj, proj, proj, proj, dy, ssave, cos, sin, gn_g, dmask, qd, kd, cd)


def _conv_bwd_pw(dy, proj, u1, cln_g, cln_b, pw_w, pw_b, dproj):
    lp, cw = u1.shape
    tm = _row_tile(lp, 320)

    def body(dy_ref, gc_ref, u1_ref, lg_ref, lb_ref, pw_ref, pb_ref, dp_in, dp_ref, du1_ref, dpw_ref,
             dpb_ref, dlg_ref, dlb_ref):
        del dp_in

        @pl.when(pl.program_id(0) == 0)
        def _():
            dpw_ref[...] = jnp.zeros_like(dpw_ref)
            dpb_ref[...] = jnp.zeros_like(dpb_ref)
            dlg_ref[...] = jnp.zeros_like(dlg_ref)
            dlb_ref[...] = jnp.zeros_like(dlb_ref)

        u1 = u1_ref[...]
        dev = u1 - _mean(u1)
        rstd = lax.rsqrt(_mean(dev * dev) + EPS)
        u1n = dev * rstd
        lg = lg_ref[...]
        z = u1n * lg + lb_ref[...]
        sz = _sigmoid(z)
        u3b = (z * sz).astype(BF16)
        u4 = _dot(u3b, pw_ref[...]) + pb_ref[...]
        gc = gc_ref[...]
        sgc = _sigmoid(gc)
        dyc = dy_ref[...]
        du4 = dyc * (gc * sgc)
        dp_ref[...] = (dyc * u4 * _dsilu(gc, sgc)).astype(BF16)
        du4b = du4.astype(BF16)
        dpb_ref[...] += _colsum(du4)
        dpw_ref[...] += _dot_tn(u3b, du4b)
        dz = _dot_nt(du4b, pw_ref[...]) * _dsilu(z, sz)
        dlg_ref[...] += _colsum(dz * u1n)
        dlb_ref[...] += _colsum(dz)
        dn = dz * lg
        du1_ref[...] = rstd * (dn - _mean(dn) - u1n * _mean(dn * u1n))

    def row(j):
        return pl.BlockSpec((tm, cw), lambda i: (i, j))

    def whole(a):
        return pl.BlockSpec(a.shape, lambda i: (0,) * a.ndim)

    def acc(r):
        return pl.BlockSpec((r, cw), lambda i: (0, 0))

    return pl.pallas_call(
        body, name="conv_bwd_pw", grid=(lp // tm,),
        in_specs=[row(1), row(6), row(0), whole(cln_g), whole(cln_b), whole(pw_w), whole(pw_b),
                  pl.BlockSpec(memory_space=pl.ANY)],
        out_specs=[row(6), row(0), acc(cw), acc(1), acc(1), acc(1)],
        out_shape=[jax.ShapeDtypeStruct(dproj.shape, dproj.dtype), jax.ShapeDtypeStruct((lp, cw), F32),
                   jax.ShapeDtypeStruct((cw, cw), F32), jax.ShapeDtypeStruct((1, cw), F32),
                   jax.ShapeDtypeStruct((1, cw), F32), jax.ShapeDtypeStruct((1, cw), F32)],
        input_output_aliases={7: 0},
        compiler_params=_params(1, 48))(dy, proj, u1, cln_g, cln_b, pw_w, pw_b, dproj)


def _conv_bwd_dw(du1, proj, dw_w, dproj):
    lp, cw = du1.shape
    tm = _row_tile(lp, 640)
    rb = _row_tile(tm, 64)
    nt = lp // tm
    hb = tm // HALO

    def body(a_ref, b_ref, du_ref, nx_ref, w_ref, dp_in, dp_ref, dww_ref, dwb_ref, ubuf, dbuf, du0, acc):
        del dp_in
        i = pl.program_id(0)

        @pl.when(i == 0)
        def _():
            ubuf[0:HALO, :] = jnp.zeros((HALO, cw), F32)
            acc[...] = jnp.zeros_like(acc)
            dwb_ref[...] = jnp.zeros_like(dwb_ref)

        a = a_ref[...]
        sb = _sigmoid(b_ref[...])
        ubuf[HALO:HALO + tm, :] = a * sb
        du = du_ref[...]
        dbuf[0:tm, :] = du
        dbuf[tm:tm + HALO, :] = jnp.where(i == nt - 1, 0.0, nx_ref[...])
        dwb_ref[...] += _colsum(du)
        _dw_taps(dbuf, w_ref, du0, None, rows=tm, start=0, flip=True, rb=rb)
        d0 = du0[...]
        dp_ref[:, 0:cw] = (d0 * sb).astype(BF16)
        dp_ref[:, cw:2 * cw] = (d0 * a * sb * (1.0 - sb)).astype(BF16)

        lb = min(128, cw)
        off = HALO - (CONV_K - 1)

        def rb_body(r, carry):
            base = pl.multiple_of(r * rb, rb)
            for cb in range(cw // lb):
                ls = slice(cb * lb, (cb + 1) * lb)
                win = ubuf[pl.ds(base, rb + HALO), ls]
                dv = dbuf[pl.ds(base, rb), ls]
                for j in range(CONV_K):
                    prod = dv * win[off + j:off + j + rb, :]
                    acc[8 * j:8 * j + 8, ls] += jnp.sum(prod.reshape(rb // 8, 8, lb), axis=0)
            return carry

        lax.fori_loop(0, tm // rb, rb_body, 0)
        ubuf[0:HALO, :] = ubuf[tm:tm + HALO, :]

        @pl.when(i == nt - 1)
        def _():
            for j in range(CONV_K):
                dww_ref[j:j + 1, :] = _colsum(acc[8 * j:8 * j + 8, :])
            dww_ref[CONV_K:HALO, :] = jnp.zeros((HALO - CONV_K, cw), F32)

    def col(j):
        return pl.BlockSpec((tm, cw), lambda i: (i, j))

    return pl.pallas_call(
        body, name="conv_bwd_dw", grid=(nt,),
        in_specs=[col(4), col(5), col(0),
                  pl.BlockSpec((HALO, cw), lambda i: (jnp.minimum((i + 1) * hb, nt * hb - 1), 0)),
                  pl.BlockSpec(dw_w.shape, lambda i: (0, 0)),
                  pl.BlockSpec(memory_space=pl.ANY)],
        out_specs=[pl.BlockSpec((tm, 2 * cw), lambda i: (i, 2)),
                   pl.BlockSpec((HALO, cw), lambda i: (0, 0)),
                   pl.BlockSpec((1, cw), lambda i: (0, 0))],
        out_shape=[jax.ShapeDtypeStruct(dproj.shape, dproj.dtype), jax.ShapeDtypeStruct((HALO, cw), F32),
                   jax.ShapeDtypeStruct((1, cw), F32)],
        scratch_shapes=[pltpu.VMEM((tm + HALO, cw), F32), pltpu.VMEM((tm + HALO, cw), F32),
                        pltpu.VMEM((tm, cw), F32), pltpu.VMEM((8 * HALO, cw), F32)],
        input_output_aliases={5: 0},
        compiler_params=_params(1, 56))(proj, proj, du1, du1, dw_w, dproj)


def _dw_in(hn, dproj, ns):
    lp, d = hn.shape
    tm = _row_tile(lp, 640)

    def body(hn_ref, dp_ref, o_ref):
        @pl.when(pl.program_id(1) == 0)
        def _():
            o_ref[...] = jnp.zeros_like(o_ref)

        o_ref[0] += _dot_tn(hn_ref[...], dp_ref[...])

    return pl.pallas_call(
        body, name="dw_in", grid=(N_CHIPS, lp // tm),
        in_specs=[pl.BlockSpec((tm, d), lambda s, i: (i, 0)),
                  pl.BlockSpec((tm, ns), lambda s, i: (i, s))],
        out_specs=pl.BlockSpec((1, d, ns), lambda s, i: (s, 0, 0)),
        out_shape=jax.ShapeDtypeStruct((N_CHIPS, d, ns), F32),
        compiler_params=_params(2, 56))(hn, dproj)


def _in_proj_bwd(dproj, w4, hp, r1, dh2, ln_g):
    lp, d = hp.shape
    ns = w4.shape[2]
    tm = _row_tile(lp, 320)

    def body(dp_ref, w_ref, h_ref, r_ref, d2_ref, g_ref, dh_ref, dlg_ref, acc):
        i, s = pl.program_id(0), pl.program_id(1)

        @pl.when((i == 0) & (s == 0))
        def _():
            dlg_ref[...] = jnp.zeros_like(dlg_ref)

        part = _dot_nt(dp_ref[...], w_ref[0])

        @pl.when(s == 0)
        def _():
            acc[...] = part

        @pl.when(s > 0)
        def _():
            acc[...] += part

        @pl.when(s == N_CHIPS - 1)
        def _():
            dhn = acc[...]
            r = r_ref[...]
            hn0 = h_ref[...] * r
            dlg_ref[...] += _colsum(dhn * hn0)
            t = dhn * g_ref[...]
            dh_ref[...] = d2_ref[...] + r * (t - hn0 * _mean(t * hn0))

    def row(cols):
        return pl.BlockSpec((tm, cols), lambda i, s: (i, 0))

    return pl.pallas_call(
        body, name="in_proj_bwd", grid=(lp // tm, N_CHIPS),
        in_specs=[pl.BlockSpec((tm, ns), lambda i, s: (i, s)),
                  pl.BlockSpec((1, d, ns), lambda i, s: (s, 0, 0)),
                  row(d), row(1), row(d), pl.BlockSpec((1, d), lambda i, s: (0, 0))],
        out_specs=[row(d), pl.BlockSpec((1, d), lambda i, s: (0, 0))],
        out_shape=[jax.ShapeDtypeStruct((lp, d), F32), jax.ShapeDtypeStruct((1, d), F32)],
        scratch_shapes=[pltpu.VMEM((tm, d), F32)],
        compiler_params=_params(2, 48))(dproj, w4, hp, r1, dh2, ln_g)


def _mesh_pos():
    return lax.axis_index("x"), lax.axis_index("y"), lax.axis_index("c")


def _other_chips(x, y):
    return [(1 - x, y), (x, 1 - y), (1 - x, 1 - y)]


def _gather_shards(shards):
    n = len(shards)
    halves = [a.shape[0] // 2 for a in shards]

    def body(*refs):
        ins, outs = refs[:n], refs[n:2 * n]
        send_sems, recv_sems, loc_sems = refs[2 * n:]
        x, y, c = _mesh_pos()
        me, sibling = (x, y, c), (x, y, 1 - c)
        my_s = 2 * x + y
        chips = _other_chips(x, y)

        def half(k, s, cc):
            return outs[k].at[s, pl.ds(cc * halves[k], halves[k])]

        def rcopy(k, j, src, dst, to):
            return pltpu.make_async_remote_copy(
                src_ref=src, dst_ref=dst, send_sem=send_sems.at[6 * k + j], recv_sem=recv_sems.at[6 * k + j],
                device_id=to, device_id_type=MESH)

        local = [pltpu.make_async_copy(ins[k], outs[k].at[my_s], loc_sems.at[k]) for k in range(n)]
        for cp in local:
            cp.start()
        started = []
        for k in range(n):
            for j, chip in enumerate(chips):
                cp = rcopy(k, j, ins[k].at[pl.ds(c * halves[k], halves[k])], half(k, my_s, c), (*chip, c))
                cp.start()
                started.append(cp)
        for j, chip in enumerate(chips):
            s_j = 2 * chip[0] + chip[1]
            for k in range(n):
                rcopy(k, j, half(k, s_j, c), half(k, s_j, c), me).wait_recv()
                cp = rcopy(k, 3 + j, half(k, s_j, c), half(k, s_j, c), sibling)
                cp.start()
                started.append(cp)
        for j, chip in enumerate(chips):
            s_j = 2 * chip[0] + chip[1]
            for k in range(n):
                rcopy(k, 3 + j, half(k, s_j, 1 - c), half(k, s_j, 1 - c), me).wait_recv()
        for cp in started:
            cp.wait_send()
        for cp in local:
            cp.wait()

    return pl.pallas_call(
        body, name="gather_weights",
        in_specs=[pl.BlockSpec(memory_space=pl.ANY)] * n,
        out_specs=[pl.BlockSpec(memory_space=pl.ANY)] * n,
        out_shape=[jax.ShapeDtypeStruct((N_CHIPS,) + a.shape, a.dtype) for a in shards],
        scratch_shapes=[pltpu.SemaphoreType.DMA((6 * n,)), pltpu.SemaphoreType.DMA((6 * n,)),
                        pltpu.SemaphoreType.DMA((n,))],
    )(*shards)


def _rs_pair_exchange(gs):
    n = len(gs)
    halves = [g.shape[1] // 2 for g in gs]

    def body(*refs):
        ins, outs = refs[:n], refs[n:2 * n]
        send_sems, recv_sems = refs[2 * n:]
        x, y, c = _mesh_pos()
        cps = []
        for k in range(n):
            cp = pltpu.make_async_remote_copy(
                src_ref=ins[k].at[:, pl.ds((1 - c) * halves[k], halves[k])], dst_ref=outs[k],
                send_sem=send_sems.at[k], recv_sem=recv_sems.at[k], device_id=(x, y, 1 - c), device_id_type=MESH)
            cp.start()
            cps.append(cp)
        for cp in cps:
            cp.wait()

    return pl.pallas_call(
        body, name="rs_pair_exchange",
        in_specs=[pl.BlockSpec(memory_space=pl.ANY)] * n,
        out_specs=[pl.BlockSpec(memory_space=pl.ANY)] * n,
        out_shape=[jax.ShapeDtypeStruct((N_CHIPS, h) + g.shape[2:], g.dtype) for g, h in zip(gs, halves)],
        scratch_shapes=[pltpu.SemaphoreType.DMA((n,)), pltpu.SemaphoreType.DMA((n,))],
    )(*gs)


def _rs_cross_chip(cs):
    n = len(cs)

    def body(*refs):
        ins, outs = refs[:n], refs[n:2 * n]
        send_sems, recv_sems = refs[2 * n:]
        x, y, c = _mesh_pos()
        cps = []
        for k in range(n):
            for j, chip in enumerate(_other_chips(x, y)):
                cp = pltpu.make_async_remote_copy(
                    src_ref=ins[k].at[2 * chip[0] + chip[1]], dst_ref=outs[k].at[j],
                    send_sem=send_sems.at[3 * k + j], recv_sem=recv_sems.at[3 * k + j],
                    device_id=(*chip, c), device_id_type=MESH)
                cp.start()
                cps.append(cp)
        for cp in cps:
            cp.wait()

    return pl.pallas_call(
        body, name="rs_cross_chip",
        in_specs=[pl.BlockSpec(memory_space=pl.ANY)] * n,
        out_specs=[pl.BlockSpec(memory_space=pl.ANY)] * n,
        out_shape=[jax.ShapeDtypeStruct((3,) + a.shape[1:], a.dtype) for a in cs],
        scratch_shapes=[pltpu.SemaphoreType.DMA((3 * n,)), pltpu.SemaphoreType.DMA((3 * n,))],
    )(*cs)


def _rs_pair_share(fins):
    n = len(fins)
    halves = [f.shape[0] for f in fins]

    def body(*refs):
        ins, outs = refs[:n], refs[n:2 * n]
        send_sems, recv_sems, loc_sems = refs[2 * n:]
        x, y, c = _mesh_pos()
        cps, local = [], []
        for k in range(n):
            mine = outs[k].at[pl.ds(c * halves[k], halves[k])]
            lc = pltpu.make_async_copy(ins[k], mine, loc_sems.at[k])
            lc.start()
            local.append(lc)
            cp = pltpu.make_async_remote_copy(
                src_ref=ins[k], dst_ref=mine, send_sem=send_sems.at[k], recv_sem=recv_sems.at[k],
                device_id=(x, y, 1 - c), device_id_type=MESH)
            cp.start()
            cps.append(cp)
        for k in range(n):
            theirs = outs[k].at[pl.ds((1 - c) * halves[k], halves[k])]
            pltpu.make_async_remote_copy(
                src_ref=ins[k], dst_ref=theirs, send_sem=send_sems.at[k], recv_sem=recv_sems.at[k],
                device_id=(x, y, c), device_id_type=MESH).wait_recv()
        for cp in cps:
            cp.wait_send()
        for lc in local:
            lc.wait()

    return pl.pallas_call(
        body, name="rs_pair_share",
        in_specs=[pl.BlockSpec(memory_space=pl.ANY)] * n,
        out_specs=[pl.BlockSpec(memory_space=pl.ANY)] * n,
        out_shape=[jax.ShapeDtypeStruct((2 * f.shape[0],) + f.shape[1:], f.dtype) for f in fins],
        scratch_shapes=[pltpu.SemaphoreType.DMA((n,)), pltpu.SemaphoreType.DMA((n,)),
                        pltpu.SemaphoreType.DMA((n,))],
    )(*fins)


def _pair_sum(g, recv, c_arr, name):
    _, rows, cols = g.shape
    h = rows // 2
    tr = _row_tile(h, 256)
    nb = h // tr

    def body(c_ref, g_ref, r_ref, o_ref):
        del c_ref
        o_ref[...] = g_ref[...] + r_ref[...]

    return pl.pallas_call(
        body, name=name,
        grid_spec=pltpu.PrefetchScalarGridSpec(
            num_scalar_prefetch=1, grid=(N_CHIPS, nb),
            in_specs=[pl.BlockSpec((1, tr, cols), lambda s, r, c_ref: (s, c_ref[0] * nb + r, 0)),
                      pl.BlockSpec((1, tr, cols), lambda s, r, c_ref: (s, r, 0))],
            out_specs=pl.BlockSpec((1, tr, cols), lambda s, r, c_ref: (s, r, 0))),
        out_shape=jax.ShapeDtypeStruct((N_CHIPS, h, cols), F32),
        compiler_params=_params(2, 32))(c_arr, g, recv)


def _chip_sum(cs, rb, s_arr, name):
    _, h, cols = cs.shape
    tr = _row_tile(h, 256)

    def body(s_ref, c_ref, r_ref, o_ref):
        del s_ref
        o_ref[...] = ((c_ref[0] + r_ref[0]) + r_ref[1]) + r_ref[2]

    return pl.pallas_call(
        body, name=name,
        grid_spec=pltpu.PrefetchScalarGridSpec(
            num_scalar_prefetch=1, grid=(h // tr,),
            in_specs=[pl.BlockSpec((1, tr, cols), lambda r, s_ref: (s_ref[0], r, 0)),
                      pl.BlockSpec((3, tr, cols), lambda r, s_ref: (0, r, 0))],
            out_specs=pl.BlockSpec((tr, cols), lambda r, s_ref: (r, 0))),
        out_shape=jax.ShapeDtypeStruct((h, cols), F32),
        compiler_params=_params(1, 32))(s_arr, cs, rb)


def _adamw_big(w, g, m, v, name):
    rows, cols = w.shape
    tr = _row_tile(rows, 256)

    def body(w_ref, g_ref, m_ref, v_ref, d_ref, nm_ref, nv_ref):
        d_ref[...], nm_ref[...], nv_ref[...] = _adamw(w_ref[...], g_ref[...], m_ref[...], v_ref[...])

    spec = pl.BlockSpec((tr, cols), lambda i: (i, 0))
    return pl.pallas_call(
        body, name=name, grid=(rows // tr,),
        in_specs=[spec] * 4, out_specs=[spec] * 3,
        out_shape=[jax.ShapeDtypeStruct((rows, cols), F32)] * 3,
        compiler_params=_params(1, 48))(w, g, m, v)


def _gather_small(loss, dfg, dlg, dgn, ddwb, dclg, dclb, dpwb, ddww, dmeta):
    d = loss.shape[1]
    w = dgn.shape[1]

    def body(loss_ref, dfg_ref, dlg_ref, dgn_ref, ddwb_ref, dclg_ref, dclb_ref, dpwb_ref, ddww_ref, dmeta_ref,
             gs_ref, gd_ref, gm_ref, send_sems, recv_sems):
        x, y, c = _mesh_pos()
        me = 4 * x + 2 * y + c
        gs_ref[me, 0:1, :] = loss_ref[...]
        gs_ref[me, 1:2, :] = dfg_ref[...]
        gs_ref[me, 2:3, :] = dlg_ref[...]
        gs_ref[me, 3:4, 0:w] = dgn_ref[...]
        gs_ref[me, 3:4, w:2 * w] = ddwb_ref[...]
        gs_ref[me, 4:5, 0:w] = dclg_ref[...]
        gs_ref[me, 4:5, w:2 * w] = dclb_ref[...]
        gs_ref[me, 5:6, 0:w] = dpwb_ref[...]
        gs_ref[me, 5:6, w:2 * w] = jnp.zeros((1, d - w), F32)
        gs_ref[me, 6:8, :] = jnp.zeros((2, d), F32)
        gd_ref[me] = ddww_ref[...]
        gm_ref[me] = dmeta_ref[...]
        bufs = (gs_ref, gd_ref, gm_ref)

        def peer(j):
            return (1 - x if j & 4 else x), (1 - y if j & 2 else y), (1 - c if j & 1 else c)

        def copy(k, j, slot, to):
            return pltpu.make_async_remote_copy(
                src_ref=bufs[k].at[slot], dst_ref=bufs[k].at[slot],
                send_sem=send_sems.at[7 * k + j - 1], recv_sem=recv_sems.at[7 * k + j - 1],
                device_id=to, device_id_type=MESH)

        cps = []
        for k in range(3):
            for j in range(1, N_DEV):
                cp = copy(k, j, me, peer(j))
                cp.start()
                cps.append(cp)
        for k in range(3):
            for j in range(1, N_DEV):
                px, py, pc = peer(j)
                copy(k, j, 4 * px + 2 * py + pc, (x, y, c)).wait_recv()
        for cp in cps:
            cp.wait_send()

    vm = pl.BlockSpec(memory_space=pltpu.VMEM)
    return pl.pallas_call(
        body, name="gather_small",
        in_specs=[vm] * 10, out_specs=[vm] * 3,
        out_shape=[jax.ShapeDtypeStruct((N_DEV, 8, d), F32),
                   jax.ShapeDtypeStruct((N_DEV,) + ddww.shape, F32),
                   jax.ShapeDtypeStruct((N_DEV,) + dmeta.shape, F32)],
        scratch_shapes=[pltpu.SemaphoreType.DMA((21,)), pltpu.SemaphoreType.DMA((21,))],
    )(loss, dfg, dlg, dgn, ddwb, dclg, dclb, dpwb, ddww, dmeta)


def _small_update(s_arr, gs, gd, gm, weights, ms, vs):
    d = gs.shape[2]
    w = d // 2
    n = len(weights)

    def body(s_ref, gs_ref, gd_ref, gm_ref, *refs):
        del s_ref
        w_refs, m_refs, v_refs = refs[:n], refs[n:2 * n], refs[2 * n:3 * n]
        loss_ref = refs[3 * n]
        g_refs = refs[3 * n + 1:4 * n + 1]
        d_refs = refs[4 * n + 1:5 * n + 1]
        nm_refs = refs[5 * n + 1:6 * n + 1]
        nv_refs = refs[6 * n + 1:7 * n + 1]

        def total(ref):
            t = ref[0]
            for dev in range(1, N_DEV):
                t = t + ref[dev]
            return t

        packed = total(gs_ref)
        loss_ref[...] = jnp.sum(packed[0:1, :], axis=1, keepdims=True) * (0.5 / d)
        grads = [packed[2:3, :], packed[1:2, :], packed[3:4, 0:w], packed[3:4, w:2 * w], packed[4:5, 0:w],
                 packed[4:5, w:2 * w], packed[5:6, 0:w], total(gd_ref), total(gm_ref)]
        for k in range(n):
            g = grads[k]
            g_refs[k][...] = g
            d_refs[k][...], nm_refs[k][...], nv_refs[k][...] = _adamw(w_refs[k][...], g, m_refs[k][...], v_refs[k][...])

    def whole(shape):
        return pl.BlockSpec(shape, lambda i, s_ref: (0,) * len(shape))

    wc = weights[7].shape[1]
    mc = weights[8].shape[1]
    shapes = [a.shape for a in weights]
    in_specs = ([whole(gs.shape),
                 pl.BlockSpec((N_DEV, gd.shape[1], wc), lambda i, s_ref: (0, 0, s_ref[0])),
                 pl.BlockSpec((N_DEV, gm.shape[1], mc), lambda i, s_ref: (0, 0, s_ref[0]))]
                + [whole(s) for s in shapes] * 3)
    out_specs = [whole((1, 1))] + [whole(s) for s in shapes] * 4
    out_shape = [jax.ShapeDtypeStruct((1, 1), F32)] + [jax.ShapeDtypeStruct(s, F32) for s in shapes] * 4
    outs = pl.pallas_call(
        body, name="small_update",
        grid_spec=pltpu.PrefetchScalarGridSpec(num_scalar_prefetch=1, grid=(1,), in_specs=in_specs,
                                               out_specs=out_specs),
        out_shape=out_shape,
        compiler_params=_params(1, 32))(s_arr, gs, gd, gm, *weights, *ms, *vs)
    loss = outs[0]
    return loss, outs[1:n + 1], outs[n + 1:2 * n + 1], outs[2 * n + 1:3 * n + 1], outs[3 * n + 1:4 * n + 1]


def kernel(x, meta_tokens, ln_g, w_in, ret_gn_g, conv_dw_w, conv_dw_b, conv_ln_g, conv_ln_b, conv_pw_w, conv_pw_b, w_out, final_g, loss_target, m_meta_tokens, m_ln_g, m_w_in, m_ret_gn_g, m_conv_dw_w, m_conv_dw_b, m_conv_ln_g, m_conv_ln_b, m_conv_pw_w, m_conv_pw_b, m_w_out, m_final_g, v_meta_tokens, v_ln_g, v_w_in, v_ret_gn_g, v_conv_dw_w, v_conv_dw_b, v_conv_ln_g, v_conv_ln_b, v_conv_pw_w, v_conv_pw_b, v_w_out, v_final_g):
    seq, d = x.shape[1], x.shape[2]
    w = ret_gn_g.shape[1]
    hd = w // RET_HEADS
    lp = CHUNK + seq
    ns = w_in.shape[2]
    c_arr = lax.axis_index("c").astype(jnp.int32).reshape(1)
    s_arr = (2 * lax.axis_index("x") + lax.axis_index("y")).astype(jnp.int32).reshape(1)

    dw_pad = jnp.pad(conv_dw_w[0], ((0, HALO - CONV_K), (0, 0)))
    w4, pw4, wo4, dw4, meta4 = _gather_shards([
        _cast_bf16(w_in[0], "cast_w_in"), _cast_bf16(conv_pw_w[0], "cast_pw_w"),
        _cast_bf16(w_out[0], "cast_w_out"), dw_pad, meta_tokens])
    pw_full = pw4.reshape(w, w)
    wo_full = wo4.reshape(2 * w, d)
    dw_full = dw4.transpose(1, 0, 2).reshape(HALO, w)
    meta_full = meta4.transpose(1, 0, 2).reshape(N_META, d)

    hp = jnp.concatenate([jnp.zeros((LEAD, d), F32), meta_full, x[0]], axis=0)
    tgtp = jnp.concatenate([jnp.zeros((CHUNK, d), F32), loss_target[0]], axis=0)
    consts = _ret_consts()
    cos, sin = _rope_tables(lp, hd // 2)
    fg2 = final_g.reshape(1, d)

    proj, hn, r1 = _in_proj(hp, ln_g, w4)
    y_ret, ssave = _ret_fwd(proj, cos, sin, ret_gn_g, consts)
    y_conv, u1 = _conv_fwd(proj, dw_full, conv_dw_b, conv_ln_g, conv_ln_b, pw_full, conv_pw_b)
    dh2, dy, loss_l, dfg = _out_proj_loss(y_ret, y_conv, hp, tgtp, wo_full, fg2)

    g_wo = _dw_out(y_ret, y_conv, dh2)
    dproj, dgn = _ret_bwd(proj, dy, ssave, cos, sin, ret_gn_g, consts)
    dproj, du1, g_pw, dpwb, dclg, dclb = _conv_bwd_pw(dy, proj, u1, conv_ln_g, conv_ln_b, pw_full, conv_pw_b, dproj)
    dproj, ddww, ddwb = _conv_bwd_dw(du1, proj, dw_full, dproj)
    g_win = _dw_in(hn, dproj, ns)
    dh, dlg = _in_proj_bwd(dproj, w4, hp, r1, dh2, ln_g)
    grad_x = dh[CHUNK:][None]
    dmeta = dh[LEAD:CHUNK]

    gs = [g_win, g_wo.reshape(N_CHIPS, (2 * w) // N_CHIPS, d), g_pw.reshape(N_CHIPS, w // N_CHIPS, w)]
    recv = _rs_pair_exchange(gs)
    names = ("w_in", "w_out", "pw_w")
    cs = [_pair_sum(g, r, c_arr, "pair_sum_" + nm) for g, r, nm in zip(gs, recv, names)]
    rb = _rs_cross_chip(cs)
    fins = [_chip_sum(c_, r, s_arr, "chip_sum_" + nm) for c_, r, nm in zip(cs, rb, names)]
    grad_w_in, grad_w_out, grad_pw = _rs_pair_share(fins)
    d_win, nm_win, nv_win = _adamw_big(w_in[0], grad_w_in, m_w_in[0], v_w_in[0], "adamw_w_in")
    d_wo, nm_wo, nv_wo = _adamw_big(w_out[0], grad_w_out, m_w_out[0], v_w_out[0], "adamw_w_out")
    d_pw, nm_pw, nv_pw = _adamw_big(conv_pw_w[0], grad_pw, m_conv_pw_w[0], v_conv_pw_w[0], "adamw_pw_w")

    gsm, gdm, gmm = _gather_small(loss_l, dfg, dlg, dgn, ddwb, dclg, dclb, dpwb, ddww, dmeta)

    def pad_dw(a):
        return jnp.pad(a[0], ((0, HALO - CONV_K), (0, 0)))

    small_w = [ln_g, fg2, ret_gn_g, conv_dw_b, conv_ln_g, conv_ln_b, conv_pw_b, dw_pad, meta_tokens]
    small_m = [m_ln_g, m_final_g.reshape(1, d), m_ret_gn_g, m_conv_dw_b, m_conv_ln_g, m_conv_ln_b, m_conv_pw_b,
               pad_dw(m_conv_dw_w), m_meta_tokens]
    small_v = [v_ln_g, v_final_g.reshape(1, d), v_ret_gn_g, v_conv_dw_b, v_conv_ln_g, v_conv_ln_b, v_conv_pw_b,
               pad_dw(v_conv_dw_w), v_meta_tokens]
    loss, sg, sd, snm, snv = _small_update(s_arr, gsm, gdm, gmm, small_w, small_m, small_v)

    def assemble(small, big_in, big_pw, big_out):
        ln, fg, gn, dwb, clg, clb, pwb, dww, meta = small
        return (meta, ln, big_in[None], gn, dww[:CONV_K][None], dwb, clg, clb, big_pw[None], pwb, big_out[None],
                fg.reshape(d))

    return (loss.reshape(()), grad_x,
            *assemble(sg, grad_w_in, grad_pw, grad_w_out),
            *assemble(sd, d_win, d_pw, d_wo),
            *assemble(snm, nm_win, nm_pw, nm_wo),
            *assemble(snv, nv_win, nv_pw, nv_wo))
```

```python
import functools

import jax
import jax.numpy as jnp
from jax import lax
from jax.experimental import pallas as pl
from jax.experimental.pallas import tpu as pltpu

F32 = jnp.float32
BF16 = jnp.bfloat16
MESH = pl.DeviceIdType.MESH

N_META = 16
CHUNK = 128
LEAD = (-N_META) % CHUNK
RET_HEADS = 4
CONV_K = 31
HALO = 32
ROPE_BASE = 10000.0
EPS = 1e-6
N_CHIPS = 4
N_DEV = 8

ADAM_LR = 0.001
ADAM_B1 = 0.9
ADAM_B2 = 0.999
ADAM_EPS = 1e-08
ADAM_WD = 0.01
ADAM_STEP = 10

MIB = 2 ** 20


def _params(n_grid_axes, vmem_mib):
    return pltpu.CompilerParams(dimension_semantics=("arbitrary",) * n_grid_axes,
                                vmem_limit_bytes=vmem_mib * MIB)


def _row_tile(n, pref):
    for t in (1664, 1280, 1024, 640, 512, 384, 320, 256, 128, 64, 32, 16, 8):
        if t <= pref and n % t == 0:
            return t
    raise ValueError(f"no row tile for {n}")


def _dot(a, b):
    return jnp.dot(a, b, preferred_element_type=F32)


def _dot_nt(a, b):
    return lax.dot_general(a, b, (((1,), (1,)), ((), ())), preferred_element_type=F32)


def _dot_tn(a, b):
    return lax.dot_general(a, b, (((0,), (0,)), ((), ())), preferred_element_type=F32)


def _sigmoid(x):
    return jax.nn.sigmoid(x)


def _dsilu(x, s):
    return s * (1.0 + x * (1.0 - s))


def _mean(x):
    return jnp.mean(x, axis=-1, keepdims=True)


def _colsum(x):
    return jnp.sum(x, axis=0, keepdims=True)


def _rope(x, cos, sin):
    half = x.shape[-1] // 2
    x1, x2 = x[:, :half], x[:, half:]
    return jnp.concatenate([x1 * cos - x2 * sin, x1 * sin + x2 * cos], axis=-1)


def _rope_t(d, cos, sin):
    half = d.shape[-1] // 2
    d1, d2 = d[:, :half], d[:, half:]
    return jnp.concatenate([d1 * cos + d2 * sin, d2 * cos - d1 * sin], axis=-1)


def _adamw(w, g, m, v):
    m = ADAM_B1 * m + (1.0 - ADAM_B1) * g
    v = ADAM_B2 * v + (1.0 - ADAM_B2) * (g * g)
    m_hat = m / (1.0 - ADAM_B1 ** ADAM_STEP)
    v_hat = v / (1.0 - ADAM_B2 ** ADAM_STEP)
    delta = -ADAM_LR * (m_hat / (jnp.sqrt(v_hat) + ADAM_EPS) + ADAM_WD * w)
    return delta, m, v


def _ret_consts():
    h = jnp.arange(RET_HEADS, dtype=F32)
    log_g = jnp.log(1.0 - jnp.exp2(-5.0 - h))
    idx = jnp.arange(CHUNK, dtype=F32)
    rel = idx[:, None] - idx[None, :]
    dmask = jnp.where(rel[None] >= 0, jnp.exp(jnp.maximum(rel, 0.0)[None] * log_g[:, None, None]), 0.0)
    qd = jnp.exp((idx[None, :] + 1.0) * log_g[:, None])[:, :, None]
    kd = jnp.exp((CHUNK - 1.0 - idx[None, :]) * log_g[:, None])[:, :, None]
    cd = jnp.exp(CHUNK * log_g)[:, None, None]
    return dmask, qd, kd, cd


def _rope_tables(n_rows, half):
    pos = jnp.arange(n_rows, dtype=F32) - float(LEAD)
    inv_freq = ROPE_BASE ** (-jnp.arange(half, dtype=F32) / half)
    ang = pos[:, None] * inv_freq[None, :]
    return jnp.cos(ang), jnp.sin(ang)


def _cast_bf16(a, name):
    rows, cols = a.shape
    tr = _row_tile(rows, 256)

    def body(a_ref, o_ref):
        o_ref[...] = a_ref[...].astype(BF16)

    return pl.pallas_call(
        body, name=name, grid=(rows // tr,),
        in_specs=[pl.BlockSpec((tr, cols), lambda i: (i, 0))],
        out_specs=pl.BlockSpec((tr, cols), lambda i: (i, 0)),
        out_shape=jax.ShapeDtypeStruct((rows, cols), BF16),
        compiler_params=_params(1, 32))(a)


def _in_proj_gather(order, hp, ln_g, w_own, pw_own, wo_own):
    lp, d = hp.shape
    ns = w_own.shape[1]
    tm = _row_tile(lp, 640)
    nt = lp // tm
    assert nt >= 2, "the hn write-back of a row tile is waited for one step later, before any pass re-reads it"
    shards = (w_own, pw_own, wo_own)
    halves = [a.shape[0] // 2 for a in shards]
    n = len(shards)

    def body(order_ref, h_ref, g_ref, w_in, pw_in, wo_in, proj_ref, r_ref, hn_hbm, w4, pw4, wo4,
             wbuf, hnbuf, send_sems, recv_sems, loc_sems, hn_out_sems, hn_in_sems, w_sem):
        del order_ref
        t, i = pl.program_id(0), pl.program_id(1)
        slot = (t * nt + i) % 2
        x, y, c = _mesh_pos()
        me, sibling = (x, y, c), (x, y, 1 - c)
        my_s = 2 * x + y
        chips = _other_chips(x, y)
        ins, outs = (w_in, pw_in, wo_in), (w4, pw4, wo4)

        def half(k, s, cc):
            return outs[k].at[s, pl.ds(cc * halves[k], halves[k])]

        def rcopy(k, j, src, dst, to):
            return pltpu.make_async_remote_copy(
                src_ref=src, dst_ref=dst, send_sem=send_sems.at[6 * k + j], recv_sem=recv_sems.at[6 * k + j],
                device_id=to, device_id_type=MESH)

        def send(k, j):
            return rcopy(k, j, ins[k].at[pl.ds(c * halves[k], halves[k])], half(k, my_s, c), (*chips[j], c))

        def shard_of(j):
            return 2 * chips[j][0] + chips[j][1]

        def forward(k, j):
            return rcopy(k, 3 + j, half(k, shard_of(j), c), half(k, shard_of(j), c), sibling)

        def land(k, j):
            rcopy(k, j, half(k, shard_of(j), c), half(k, shard_of(j), c), me).wait_recv()
            forward(k, j).start()

        def landed_from_sibling(k, j):
            rcopy(k, 3 + j, half(k, shard_of(j), 1 - c), half(k, shard_of(j), 1 - c), me).wait_recv()

        def load_w(src):
            cp = pltpu.make_async_copy(src, wbuf, w_sem)
            cp.start()
            cp.wait()

        def own_w_copy():
            return pltpu.make_async_copy(wbuf, w4.at[my_s], loc_sems.at[0])

        def own_copies():
            return [pltpu.make_async_copy(ins[k], outs[k].at[my_s], loc_sems.at[k]) for k in range(1, n)]

        def hn_out(sl, row_tile):
            return pltpu.make_async_copy(hnbuf.at[sl], hn_hbm.at[pl.ds(row_tile * tm, tm)], hn_out_sems.at[sl])

        def hn_in(sl, row_tile):
            return pltpu.make_async_copy(hn_hbm.at[pl.ds(row_tile * tm, tm)], hnbuf.at[sl], hn_in_sems.at[sl])

        @pl.when((t == 0) & (i == 0))
        def _():
            load_w(w_in)
            for k in range(n):
                for j in range(3):
                    send(k, j).start()
            own_w_copy().start()
            for cp in own_copies():
                cp.start()

        for j in range(3):
            @pl.when((t == j + 1) & (i == 0))
            def _(j=j):
                if j == 0:
                    own_w_copy().wait()
                land(0, j)
                landed_from_sibling(0, j)
                load_w(w4.at[shard_of(j)])

        @pl.when(t == 0)
        def _():
            h = h_ref[...]
            r = lax.rsqrt(_mean(h * h) + EPS)
            hnbuf[slot] = ((h * r) * g_ref[...]).astype(BF16)
            r_ref[...] = r
            hn_out(slot, i).start()

        @pl.when(t > 0)
        def _():
            hn_in(slot, i).wait()

        proj_ref[...] = _dot(hnbuf[slot], wbuf[...])

        @pl.when(((t == 0) & (i > 0)) | ((t == 1) & (i == 0)))
        def _():
            hn_out(1 - slot, jnp.where(i > 0, i - 1, nt - 1)).wait()

        last = (t == N_CHIPS - 1) & (i == nt - 1)

        @pl.when(((t > 0) | (i == nt - 1)) & jnp.logical_not(last))
        def _():
            hn_in(1 - slot, jnp.where(i == nt - 1, 0, i + 1)).start()

        @pl.when(last)
        def _():
            for k in range(1, n):
                for j in range(3):
                    land(k, j)
            for k in range(1, n):
                for j in range(3):
                    landed_from_sibling(k, j)
            for k in range(n):
                for j in range(3):
                    send(k, j).wait_send()
                    forward(k, j).wait_send()
            for cp in own_copies():
                cp.wait()

    def frozen(t, i):
        return jnp.where(t == 0, i, nt - 1)

    any_spec = pl.BlockSpec(memory_space=pl.ANY)
    return pl.pallas_call(
        body, name="in_proj_gather",
        grid_spec=pltpu.PrefetchScalarGridSpec(
            num_scalar_prefetch=1, grid=(N_CHIPS, nt),
            in_specs=[pl.BlockSpec((tm, d), lambda t, i, o: (frozen(t, i), 0)),
                      pl.BlockSpec((1, d), lambda t, i, o: (0, 0)),
                      any_spec, any_spec, any_spec],
            out_specs=[pl.BlockSpec((tm, ns), lambda t, i, o: (i, o[t])),
                       pl.BlockSpec((tm, 1), lambda t, i, o: (frozen(t, i), 0)),
                       any_spec, any_spec, any_spec, any_spec],
            scratch_shapes=[pltpu.VMEM((d, ns), BF16), pltpu.VMEM((2, tm, d), BF16),
                            pltpu.SemaphoreType.DMA((6 * n,)), pltpu.SemaphoreType.DMA((6 * n,)),
                            pltpu.SemaphoreType.DMA((n,)), pltpu.SemaphoreType.DMA((2,)),
                            pltpu.SemaphoreType.DMA((2,)), pltpu.SemaphoreType.DMA]),
        out_shape=[jax.ShapeDtypeStruct((lp, N_CHIPS * ns), F32),
                   jax.ShapeDtypeStruct((lp, 1), F32),
                   jax.ShapeDtypeStruct((lp, d), BF16)]
                  + [jax.ShapeDtypeStruct((N_CHIPS,) + a.shape, a.dtype) for a in shards],
        compiler_params=_params(2, 48))(order, hp, ln_g, w_own, pw_own, wo_own)


def _ret_fwd(proj, cos, sin, gn_g, consts):
    lp = proj.shape[0]
    w = gn_g.shape[1]
    hd = w // RET_HEADS
    nch = lp // CHUNK
    dmask, qd, kd, cd = consts

    def body(q_ref, k_ref, v_ref, g_ref, cos_ref, sin_ref, gn_ref, m_ref, qd_ref, kd_ref, cd_ref,
             y_ref, ssave_ref, s_scr):
        @pl.when(pl.program_id(0) == 0)
        def _():
            s_scr[...] = jnp.zeros_like(s_scr)

        cos_t, sin_t = cos_ref[...], sin_ref[...]
        for h in range(RET_HEADS):
            sl = slice(h * hd, (h + 1) * hd)
            qr = _rope(q_ref[:, sl], cos_t, sin_t)
            kr = _rope(k_ref[:, sl], cos_t, sin_t) * (hd ** -0.5)
            vb = v_ref[:, sl].astype(BF16)
            sc = _dot_nt(qr.astype(BF16), kr.astype(BF16)) * m_ref[h]
            state = s_scr[h]
            sb = state.astype(BF16)
            ssave_ref[0, h] = sb
            out = _dot(sc.astype(BF16), vb) + _dot((qr * qd_ref[h]).astype(BF16), sb)
            s_scr[h] = state * cd_ref[h] + _dot_tn((kr * kd_ref[h]).astype(BF16), vb)
            dev = out - _mean(out)
            yn = dev * lax.rsqrt(_mean(dev * dev) + EPS)
            g = g_ref[:, sl]
            y_ref[:, sl] = ((yn * gn_ref[:, sl]) * (g * _sigmoid(g))).astype(BF16)

    def col(j):
        return pl.BlockSpec((CHUNK, w), lambda i: (i, j))

    def whole(a):
        return pl.BlockSpec(a.shape, lambda i: (0,) * a.ndim)

    return pl.pallas_call(
        body, name="ret_fwd", grid=(nch,),
        in_specs=[col(0), col(1), col(2), col(3),
                  pl.BlockSpec((CHUNK, hd // 2), lambda i: (i, 0)),
                  pl.BlockSpec((CHUNK, hd // 2), lambda i: (i, 0)),
                  whole(gn_g), whole(dmask), whole(qd), whole(kd), whole(cd)],
        out_specs=[pl.BlockSpec((CHUNK, w), lambda i: (i, 0)),
                   pl.BlockSpec((1, RET_HEADS, hd, hd), lambda i: (i, 0, 0, 0))],
        out_shape=[jax.ShapeDtypeStruct((lp, w), BF16),
                   jax.ShapeDtypeStruct((nch, RET_HEADS, hd, hd), BF16)],
        scratch_shapes=[pltpu.VMEM((RET_HEADS, hd, hd), F32)],
        compiler_params=_params(1, 32))(proj, proj, proj, proj, cos, sin, gn_g, dmask, qd, kd, cd)


def _tap_groups(start, flip):
    groups = {}
    for j in range(CONV_K):
        o = start + (CONV_K - 1 - j if flip else j)
        groups.setdefault(o % 8, []).append((o // 8, j))
    return groups


def _dw_taps(src_ref, w_ref, dst_ref, bias, *, rows, start, flip, rb):
    cw = dst_ref.shape[1]
    lb = min(128, cw)
    groups = _tap_groups(start, flip)

    def rb_body(r, carry):
        base = pl.multiple_of(r * rb, rb)
        for cb in range(cw // lb):
            ls = slice(cb * lb, (cb + 1) * lb)
            win = src_ref[pl.ds(base, rb + HALO), ls]
            acc = jnp.zeros((rb, lb), F32) if bias is None else jnp.broadcast_to(bias[:, ls], (rb, lb))
            for s, taps in groups.items():
                span = rb + 8 * max(a for a, _ in taps)
                ws = win[s:s + span, :]
                for a, j in taps:
                    acc = acc + ws[8 * a:8 * a + rb, :] * w_ref[j:j + 1, ls]
            dst_ref[pl.ds(base, rb), ls] = acc
        return carry

    lax.fori_loop(0, rows // rb, rb_body, 0)


def _conv_fwd(proj, dw_w, dw_b, cln_g, cln_b, pw_w, pw_b):
    lp = proj.shape[0]
    cw = dw_b.shape[1]
    tm = _row_tile(lp, 640)
    rb = _row_tile(tm, 64)

    def body(a_ref, b_ref, gc_ref, w_ref, wb_ref, lg_ref, lb_ref, pw_ref, pb_ref, y_ref, u1_ref, buf):
        @pl.when(pl.program_id(0) == 0)
        def _():
            buf[0:HALO, :] = jnp.zeros((HALO, cw), F32)

        buf[HALO:HALO + tm, :] = a_ref[...] * _sigmoid(b_ref[...])
        _dw_taps(buf, w_ref, u1_ref, wb_ref[...], rows=tm, start=HALO - (CONV_K - 1), flip=False, rb=rb)
        buf[0:HALO, :] = buf[tm:tm + HALO, :]
        u1 = u1_ref[...]
        dev = u1 - _mean(u1)
        z = dev * lax.rsqrt(_mean(dev * dev) + EPS) * lg_ref[...] + lb_ref[...]
        u3 = (z * _sigmoid(z)).astype(BF16)
        u4 = _dot(u3, pw_ref[...]) + pb_ref[...]
        gc = gc_ref[...]
        y_ref[...] = (u4 * (gc * _sigmoid(gc))).astype(BF16)

    def col(j):
        return pl.BlockSpec((tm, cw), lambda i: (i, j))

    def whole(a):
        return pl.BlockSpec(a.shape, lambda i: (0,) * a.ndim)

    return pl.pallas_call(
        body, name="conv_fwd", grid=(lp // tm,),
        in_specs=[col(4), col(5), col(6), whole(dw_w), whole(dw_b), whole(cln_g), whole(cln_b),
                  whole(pw_w), whole(pw_b)],
        out_specs=[pl.BlockSpec((tm, cw), lambda i: (i, 0)), pl.BlockSpec((tm, cw), lambda i: (i, 0))],
        out_shape=[jax.ShapeDtypeStruct((lp, cw), BF16), jax.ShapeDtypeStruct((lp, cw), F32)],
        scratch_shapes=[pltpu.VMEM((tm + HALO, cw), F32)],
        compiler_params=_params(1, 48))(proj, proj, proj, dw_w, dw_b, cln_g, cln_b, pw_w, pw_b)


def _out_proj_loss(yr, yc, hp, tgtp, w_out, final_g):
    lp, d = hp.shape
    w = yr.shape[1]
    tm = _row_tile(lp, 320)

    def body(yr_ref, yc_ref, h_ref, t_ref, w_ref, fg_ref, dh2_ref, dy_ref, loss_ref, dfg_ref):
        i = pl.program_id(0)

        @pl.when(i == 0)
        def _():
            loss_ref[...] = jnp.zeros_like(loss_ref)
            dfg_ref[...] = jnp.zeros_like(dfg_ref)

        h2 = h_ref[...] + (_dot(yr_ref[...], w_ref[0:w, :]) + _dot(yc_ref[...], w_ref[w:2 * w, :]))
        r2 = lax.rsqrt(_mean(h2 * h2) + EPS)
        h2n = h2 * r2
        fg = fg_ref[...]
        rows = i * tm + lax.broadcasted_iota(jnp.int32, (tm, 1), 0)
        err = jnp.where(rows >= CHUNK, h2n * fg - t_ref[...], 0.0)
        loss_ref[...] += _colsum(err * err)
        dout = err * (1.0 / d)
        dfg_ref[...] += _colsum(dout * h2n)
        dz = dout * fg
        dh2 = r2 * (dz - h2n * _mean(dz * h2n))
        dh2_ref[...] = dh2
        db = dh2.astype(BF16)
        dy_ref[:, 0:w] = _dot_nt(db, w_ref[0:w, :])
        dy_ref[:, w:2 * w] = _dot_nt(db, w_ref[w:2 * w, :])

    def row(cols):
        return pl.BlockSpec((tm, cols), lambda i: (i, 0))

    return pl.pallas_call(
        body, name="out_proj_loss", grid=(lp // tm,),
        in_specs=[row(w), row(w), row(d), row(d),
                  pl.BlockSpec(memory_space=pltpu.VMEM),
                  pl.BlockSpec((1, d), lambda i: (0, 0))],
        out_specs=[row(d), row(2 * w), pl.BlockSpec((1, d), lambda i: (0, 0)),
                   pl.BlockSpec((1, d), lambda i: (0, 0))],
        out_shape=[jax.ShapeDtypeStruct((lp, d), F32), jax.ShapeDtypeStruct((lp, 2 * w), F32),
                   jax.ShapeDtypeStruct((1, d), F32), jax.ShapeDtypeStruct((1, d), F32)],
        compiler_params=_params(1, 56))(yr, yc, hp, tgtp, w_out, final_g)


def _dw_out(yr, yc, dh2):
    lp, d = dh2.shape
    w = yr.shape[1]
    tm = _row_tile(lp, 640)
    nb = 2
    dn = d // nb

    def body(yr_ref, yc_ref, d_ref, o_ref):
        @pl.when(pl.program_id(1) == 0)
        def _():
            o_ref[...] = jnp.zeros_like(o_ref)

        db = d_ref[...].astype(BF16)
        o_ref[0:w, :] += _dot_tn(yr_ref[...], db)
        o_ref[w:2 * w, :] += _dot_tn(yc_ref[...], db)

    return pl.pallas_call(
        body, name="dw_out", grid=(nb, lp // tm),
        in_specs=[pl.BlockSpec((tm, w), lambda n, i: (i, 0)),
                  pl.BlockSpec((tm, w), lambda n, i: (i, 0)),
                  pl.BlockSpec((tm, dn), lambda n, i: (i, n))],
        out_specs=pl.BlockSpec((2 * w, dn), lambda n, i: (0, n)),
        out_shape=jax.ShapeDtypeStruct((2 * w, d), F32),
        compiler_params=_params(2, 48))(yr, yc, dh2)


def _ret_bwd(proj, dy, ssave, cos, sin, gn_g, consts):
    lp = proj.shape[0]
    w = gn_g.shape[1]
    hd = w // RET_HEADS
    nch = lp // CHUNK
    dmask, qd, kd, cd = consts

    def body(q_ref, k_ref, v_ref, g_ref, dy_ref, ss_ref, cos_ref, sin_ref, gn_ref, m_ref, qd_ref, kd_ref,
             cd_ref, dp_ref, dgn_ref, ds_scr):
        @pl.when(pl.program_id(0) == 0)
        def _():
            ds_scr[...] = jnp.zeros_like(ds_scr)
            dgn_ref[...] = jnp.zeros_like(dgn_ref)

        cos_t, sin_t = cos_ref[...], sin_ref[...]
        for h in range(RET_HEADS):
            sl = slice(h * hd, (h + 1) * hd)
            qr = _rope(q_ref[:, sl], cos_t, sin_t)
            kr = _rope(k_ref[:, sl], cos_t, sin_t) * (hd ** -0.5)
            qb, kb = qr.astype(BF16), kr.astype(BF16)
            vb = v_ref[:, sl].astype(BF16)
            sb = ss_ref[0, h]
            mask = m_ref[h]
            qdec, kdec = qd_ref[h], kd_ref[h]
            scb = (_dot_nt(qb, kb) * mask).astype(BF16)
            qdq = (qr * qdec).astype(BF16)
            kdk = (kr * kdec).astype(BF16)
            out = _dot(scb, vb) + _dot(qdq, sb)
            dev = out - _mean(out)
            rstd = lax.rsqrt(_mean(dev * dev) + EPS)
            yn = dev * rstd
            g = g_ref[:, sl]
            sg = _sigmoid(g)
            gng = gn_ref[:, sl]
            dyv = dy_ref[:, sl]
            dgr = dyv * (yn * gng) * _dsilu(g, sg)
            silu_g = g * sg
            dgn_ref[:, sl] += _colsum(dyv * yn * silu_g)
            dyn = dyv * gng * silu_g
            dout = rstd * (dyn - _mean(dyn) - yn * _mean(dyn * yn))
            dob = dout.astype(BF16)
            dscb = (_dot_nt(dob, vb) * mask).astype(BF16)
            dstate = ds_scr[h]
            dsb = dstate.astype(BF16)
            dq = _dot(dscb, kb) + _dot_nt(dob, sb) * qdec
            dk = _dot_tn(dscb, qb) + _dot_nt(vb, dsb) * kdec
            dv = _dot_tn(scb, dob) + _dot(kdk, dsb)
            ds_scr[h] = dstate * cd_ref[h] + _dot_tn(qdq, dob)
            dp_ref[:, 0 * w + h * hd:0 * w + (h + 1) * hd] = _rope_t(dq, cos_t, sin_t).astype(BF16)
            dp_ref[:, 1 * w + h * hd:1 * w + (h + 1) * hd] = (_rope_t(dk, cos_t, sin_t) * (hd ** -0.5)).astype(BF16)
            dp_ref[:, 2 * w + h * hd:2 * w + (h + 1) * hd] = dv.astype(BF16)
            dp_ref[:, 3 * w + h * hd:3 * w + (h + 1) * hd] = dgr.astype(BF16)

    def rev(i):
        return nch - 1 - i

    def col(j):
        return pl.BlockSpec((CHUNK, w), lambda i: (rev(i), j))

    def whole(a):
        return pl.BlockSpec(a.shape, lambda i: (0,) * a.ndim)

    return pl.pallas_call(
        body, name="ret_bwd", grid=(nch,),
        in_specs=[col(0), col(1), col(2), col(3),
                  pl.BlockSpec((CHUNK, w), lambda i: (rev(i), 0)),
                  pl.BlockSpec((1, RET_HEADS, hd, hd), lambda i: (rev(i), 0, 0, 0)),
                  pl.BlockSpec((CHUNK, hd // 2), lambda i: (rev(i), 0)),
                  pl.BlockSpec((CHUNK, hd // 2), lambda i: (rev(i), 0)),
                  whole(gn_g), whole(dmask), whole(qd), whole(kd), whole(cd)],
        out_specs=[pl.BlockSpec((CHUNK, 4 * w), lambda i: (rev(i), 0)),
                   pl.BlockSpec((1, w), lambda i: (0, 0))],
        out_shape=[jax.ShapeDtypeStruct((lp, 7 * w), BF16), jax.ShapeDtypeStruct((1, w), F32)],
        scratch_shapes=[pltpu.VMEM((RET_HEADS, hd, hd), F32)],
        compiler_params=_params(1, 32))(proj, proj, proj, proj, dy, ssave, cos, sin, gn_g, dmask, qd, kd, cd)


def _conv_bwd_pw(dy, proj, u1, cln_g, cln_b, pw_w, pw_b, dproj):
    lp, cw = u1.shape
    tm = _row_tile(lp, 320)

    def body(dy_ref, gc_ref, u1_ref, lg_ref, lb_ref, pw_ref, pb_ref, dp_in, dp_ref, du1_ref, dpw_ref,
             dpb_ref, dlg_ref, dlb_ref):
        del dp_in

        @pl.when(pl.program_id(0) == 0)
        def _():
            dpw_ref[...] = jnp.zeros_like(dpw_ref)
            dpb_ref[...] = jnp.zeros_like(dpb_ref)
            dlg_ref[...] = jnp.zeros_like(dlg_ref)
            dlb_ref[...] = jnp.zeros_like(dlb_ref)

        u1 = u1_ref[...]
        dev = u1 - _mean(u1)
        rstd = lax.rsqrt(_mean(dev * dev) + EPS)
        u1n = dev * rstd
        lg = lg_ref[...]
        z = u1n * lg + lb_ref[...]
        sz = _sigmoid(z)
        u3b = (z * sz).astype(BF16)
        u4 = _dot(u3b, pw_ref[...]) + pb_ref[...]
        gc = gc_ref[...]
        sgc = _sigmoid(gc)
        dyc = dy_ref[...]
        du4 = dyc * (gc * sgc)
        dp_ref[...] = (dyc * u4 * _dsilu(gc, sgc)).astype(BF16)
        du4b = du4.astype(BF16)
        dpb_ref[...] += _colsum(du4)
        dpw_ref[...] += _dot_tn(u3b, du4b)
        dz = _dot_nt(du4b, pw_ref[...]) * _dsilu(z, sz)
        dlg_ref[...] += _colsum(dz * u1n)
        dlb_ref[...] += _colsum(dz)
        dn = dz * lg
        du1_ref[...] = rstd * (dn - _mean(dn) - u1n * _mean(dn * u1n))

    def row(j):
        return pl.BlockSpec((tm, cw), lambda i: (i, j))

    def whole(a):
        return pl.BlockSpec(a.shape, lambda i: (0,) * a.ndim)

    def acc(r):
        return pl.BlockSpec((r, cw), lambda i: (0, 0))

    return pl.pallas_call(
        body, name="conv_bwd_pw", grid=(lp // tm,),
        in_specs=[row(1), row(6), row(0), whole(cln_g), whole(cln_b), whole(pw_w), whole(pw_b),
                  pl.BlockSpec(memory_space=pl.ANY)],
        out_specs=[row(6), row(0), acc(cw), acc(1), acc(1), acc(1)],
        out_shape=[jax.ShapeDtypeStruct(dproj.shape, dproj.dtype), jax.ShapeDtypeStruct((lp, cw), F32),
                   jax.ShapeDtypeStruct((cw, cw), F32), jax.ShapeDtypeStruct((1, cw), F32),
                   jax.ShapeDtypeStruct((1, cw), F32), jax.ShapeDtypeStruct((1, cw), F32)],
        input_output_aliases={7: 0},
        compiler_params=_params(1, 48))(dy, proj, u1, cln_g, cln_b, pw_w, pw_b, dproj)


def _conv_bwd_dw(du1, proj, dw_w, dproj):
    lp, cw = du1.shape
    tm = _row_tile(lp, 640)
    rb = _row_tile(tm, 64)
    nt = lp // tm
    hb = tm // HALO

    def body(a_ref, b_ref, du_ref, nx_ref, w_ref, dp_in, dp_ref, dww_ref, dwb_ref, ubuf, dbuf, du0, acc):
        del dp_in
        i = pl.program_id(0)

        @pl.when(i == 0)
        def _():
            ubuf[0:HALO, :] = jnp.zeros((HALO, cw), F32)
            acc[...] = jnp.zeros_like(acc)
            dwb_ref[...] = jnp.zeros_like(dwb_ref)

        a = a_ref[...]
        sb = _sigmoid(b_ref[...])
        ubuf[HALO:HALO + tm, :] = a * sb
        du = du_ref[...]
        dbuf[0:tm, :] = du
        dbuf[tm:tm + HALO, :] = jnp.where(i == nt - 1, 0.0, nx_ref[...])
        dwb_ref[...] += _colsum(du)
        _dw_taps(dbuf, w_ref, du0, None, rows=tm, start=0, flip=True, rb=rb)
        d0 = du0[...]
        dp_ref[:, 0:cw] = (d0 * sb).astype(BF16)
        dp_ref[:, cw:2 * cw] = (d0 * a * sb * (1.0 - sb)).astype(BF16)

        lb = min(128, cw)
        groups = _tap_groups(HALO - (CONV_K - 1), False)

        def rb_body(r, carry):
            base = pl.multiple_of(r * rb, rb)
            for cb in range(cw // lb):
                ls = slice(cb * lb, (cb + 1) * lb)
                win = ubuf[pl.ds(base, rb + HALO), ls]
                dv = dbuf[pl.ds(base, rb), ls]
                for s, taps in groups.items():
                    span = rb + 8 * max(a for a, _ in taps)
                    ws = win[s:s + span, :]
                    for a, j in taps:
                        prod = dv * ws[8 * a:8 * a + rb, :]
                        acc[8 * j:8 * j + 8, ls] += jnp.sum(prod.reshape(rb // 8, 8, lb), axis=0)
            return carry

        lax.fori_loop(0, tm // rb, rb_body, 0)
        ubuf[0:HALO, :] = ubuf[tm:tm + HALO, :]

        @pl.when(i == nt - 1)
        def _():
            for j in range(CONV_K):
                dww_ref[j:j + 1, :] = _colsum(acc[8 * j:8 * j + 8, :])
            dww_ref[CONV_K:HALO, :] = jnp.zeros((HALO - CONV_K, cw), F32)

    def col(j):
        return pl.BlockSpec((tm, cw), lambda i: (i, j))

    return pl.pallas_call(
        body, name="conv_bwd_dw", grid=(nt,),
        in_specs=[col(4), col(5), col(0),
                  pl.BlockSpec((HALO, cw), lambda i: (jnp.minimum((i + 1) * hb, nt * hb - 1), 0)),
                  pl.BlockSpec(dw_w.shape, lambda i: (0, 0)),
                  pl.BlockSpec(memory_space=pl.ANY)],
        out_specs=[pl.BlockSpec((tm, 2 * cw), lambda i: (i, 2)),
                   pl.BlockSpec((HALO, cw), lambda i: (0, 0)),
                   pl.BlockSpec((1, cw), lambda i: (0, 0))],
        out_shape=[jax.ShapeDtypeStruct(dproj.shape, dproj.dtype), jax.ShapeDtypeStruct((HALO, cw), F32),
                   jax.ShapeDtypeStruct((1, cw), F32)],
        scratch_shapes=[pltpu.VMEM((tm + HALO, cw), F32), pltpu.VMEM((tm + HALO, cw), F32),
                        pltpu.VMEM((tm, cw), F32), pltpu.VMEM((8 * HALO, cw), F32)],
        input_output_aliases={5: 0},
        compiler_params=_params(1, 56))(proj, proj, du1, du1, dw_w, dproj)


def _dw_in(hn, dproj, ns):
    lp, d = hn.shape
    tm = _row_tile(lp, 640)

    def body(hn_ref, dp_ref, o_ref):
        @pl.when(pl.program_id(1) == 0)
        def _():
            o_ref[...] = jnp.zeros_like(o_ref)

        o_ref[0] += _dot_tn(hn_ref[...], dp_ref[...])

    return pl.pallas_call(
        body, name="dw_in", grid=(N_CHIPS, lp // tm),
        in_specs=[pl.BlockSpec((tm, d), lambda s, i: (i, 0)),
                  pl.BlockSpec((tm, ns), lambda s, i: (i, s))],
        out_specs=pl.BlockSpec((1, d, ns), lambda s, i: (s, 0, 0)),
        out_shape=jax.ShapeDtypeStruct((N_CHIPS, d, ns), F32),
        compiler_params=_params(2, 56))(hn, dproj)


def _in_proj_bwd(dproj, w4, hp, r1, dh2, ln_g, cs):
    lp, d = hp.shape
    ns = w4.shape[2]
    tm = _row_tile(lp, 320)
    nt = lp // tm
    n = len(cs)

    def body(dp_ref, w_ref, h_ref, r_ref, d2_ref, g_ref, *refs):
        cs_refs, (dh_ref, dlg_ref), rb_refs = refs[:n], refs[n:n + 2], refs[n + 2:2 * n + 2]
        acc, send_sems, recv_sems = refs[2 * n + 2:]
        i, s = pl.program_id(0), pl.program_id(1)
        x, y, c = _mesh_pos()

        def exchange():
            return [pltpu.make_async_remote_copy(
                src_ref=cs_refs[k].at[2 * chip[0] + chip[1]], dst_ref=rb_refs[k].at[j],
                send_sem=send_sems.at[3 * k + j], recv_sem=recv_sems.at[3 * k + j],
                device_id=(*chip, c), device_id_type=MESH)
                for k in range(n) for j, chip in enumerate(_other_chips(x, y))]

        @pl.when((i == 0) & (s == 0))
        def _():
            dlg_ref[...] = jnp.zeros_like(dlg_ref)
            for cp in exchange():
                cp.start()

        @pl.when((i == nt - 1) & (s == N_CHIPS - 1))
        def _():
            for cp in exchange():
                cp.wait()

        part = _dot_nt(dp_ref[...], w_ref[0])

        @pl.when(s == 0)
        def _():
            acc[...] = part

        @pl.when(s > 0)
        def _():
            acc[...] += part

        @pl.when(s == N_CHIPS - 1)
        def _():
            dhn = acc[...]
            r = r_ref[...]
            hn0 = h_ref[...] * r
            dlg_ref[...] += _colsum(dhn * hn0)
            t = dhn * g_ref[...]
            dh_ref[...] = d2_ref[...] + r * (t - hn0 * _mean(t * hn0))

    def row(cols):
        return pl.BlockSpec((tm, cols), lambda i, s: (i, 0))

    any_spec = pl.BlockSpec(memory_space=pl.ANY)
    outs = pl.pallas_call(
        body, name="in_proj_bwd", grid=(nt, N_CHIPS),
        in_specs=[pl.BlockSpec((tm, ns), lambda i, s: (i, s)),
                  pl.BlockSpec((1, d, ns), lambda i, s: (s, 0, 0)),
                  row(d), row(1), row(d), pl.BlockSpec((1, d), lambda i, s: (0, 0))] + [any_spec] * n,
        out_specs=[row(d), pl.BlockSpec((1, d), lambda i, s: (0, 0))] + [any_spec] * n,
        out_shape=[jax.ShapeDtypeStruct((lp, d), F32), jax.ShapeDtypeStruct((1, d), F32)]
                  + [jax.ShapeDtypeStruct((3,) + a.shape[1:], a.dtype) for a in cs],
        scratch_shapes=[pltpu.VMEM((tm, d), F32), pltpu.SemaphoreType.DMA((3 * n,)),
                        pltpu.SemaphoreType.DMA((3 * n,))],
        compiler_params=_params(2, 48))(dproj, w4, hp, r1, dh2, ln_g, *cs)
    return outs[0], outs[1], outs[2:]


def _mesh_pos():
    return lax.axis_index("x"), lax.axis_index("y"), lax.axis_index("c")


def _other_chips(x, y):
    return [(1 - x, y), (x, 1 - y), (1 - x, 1 - y)]


def _gather_shards(shards):
    n = len(shards)
    halves = [a.shape[0] // 2 for a in shards]

    def body(*refs):
        ins, outs = refs[:n], refs[n:2 * n]
        send_sems, recv_sems, loc_sems = refs[2 * n:]
        x, y, c = _mesh_pos()
        me, sibling = (x, y, c), (x, y, 1 - c)
        my_s = 2 * x + y
        chips = _other_chips(x, y)

        def half(k, s, cc):
            return outs[k].at[s, pl.ds(cc * halves[k], halves[k])]

        def rcopy(k, j, src, dst, to):
            return pltpu.make_async_remote_copy(
                src_ref=src, dst_ref=dst, send_sem=send_sems.at[6 * k + j], recv_sem=recv_sems.at[6 * k + j],
                device_id=to, device_id_type=MESH)

        local = [pltpu.make_async_copy(ins[k], outs[k].at[my_s], loc_sems.at[k]) for k in range(n)]
        for cp in local:
            cp.start()
        started = []
        for k in range(n):
            for j, chip in enumerate(chips):
                cp = rcopy(k, j, ins[k].at[pl.ds(c * halves[k], halves[k])], half(k, my_s, c), (*chip, c))
                cp.start()
                started.append(cp)
        for j, chip in enumerate(chips):
            s_j = 2 * chip[0] + chip[1]
            for k in range(n):
                rcopy(k, j, half(k, s_j, c), half(k, s_j, c), me).wait_recv()
                cp = rcopy(k, 3 + j, half(k, s_j, c), half(k, s_j, c), sibling)
                cp.start()
                started.append(cp)
        for j, chip in enumerate(chips):
            s_j = 2 * chip[0] + chip[1]
            for k in range(n):
                rcopy(k, 3 + j, half(k, s_j, 1 - c), half(k, s_j, 1 - c), me).wait_recv()
        for cp in started:
            cp.wait_send()
        for cp in local:
            cp.wait()

    return pl.pallas_call(
        body, name="gather_weights",
        in_specs=[pl.BlockSpec(memory_space=pl.ANY)] * n,
        out_specs=[pl.BlockSpec(memory_space=pl.ANY)] * n,
        out_shape=[jax.ShapeDtypeStruct((N_CHIPS,) + a.shape, a.dtype) for a in shards],
        scratch_shapes=[pltpu.SemaphoreType.DMA((6 * n,)), pltpu.SemaphoreType.DMA((6 * n,)),
                        pltpu.SemaphoreType.DMA((n,))],
    )(*shards)


def _rs_pair_exchange(gs):
    n = len(gs)
    halves = [g.shape[1] // 2 for g in gs]

    def body(*refs):
        ins, outs = refs[:n], refs[n:2 * n]
        send_sems, recv_sems = refs[2 * n:]
        x, y, c = _mesh_pos()
        cps = []
        for k in range(n):
            cp = pltpu.make_async_remote_copy(
                src_ref=ins[k].at[:, pl.ds((1 - c) * halves[k], halves[k])], dst_ref=outs[k],
                send_sem=send_sems.at[k], recv_sem=recv_sems.at[k], device_id=(x, y, 1 - c), device_id_type=MESH)
            cp.start()
            cps.append(cp)
        for cp in cps:
            cp.wait()

    return pl.pallas_call(
        body, name="rs_pair_exchange",
        in_specs=[pl.BlockSpec(memory_space=pl.ANY)] * n,
        out_specs=[pl.BlockSpec(memory_space=pl.ANY)] * n,
        out_shape=[jax.ShapeDtypeStruct((N_CHIPS, h) + g.shape[2:], g.dtype) for g, h in zip(gs, halves)],
        scratch_shapes=[pltpu.SemaphoreType.DMA((n,)), pltpu.SemaphoreType.DMA((n,))],
    )(*gs)


def _rs_pair_share(fulls):
    n = len(fulls)
    halves = [f.shape[0] // 2 for f in fulls]

    def body(*refs):
        outs = refs[n:2 * n]
        send_sems, recv_sems = refs[2 * n:]
        x, y, c = _mesh_pos()

        def copy(k, cc, to):
            rows = outs[k].at[pl.ds(cc * halves[k], halves[k])]
            return pltpu.make_async_remote_copy(
                src_ref=rows, dst_ref=rows, send_sem=send_sems.at[k], recv_sem=recv_sems.at[k],
                device_id=to, device_id_type=MESH)

        cps = [copy(k, c, (x, y, 1 - c)) for k in range(n)]
        for cp in cps:
            cp.start()
        for k in range(n):
            copy(k, 1 - c, (x, y, c)).wait_recv()
        for cp in cps:
            cp.wait_send()

    return pl.pallas_call(
        body, name="rs_pair_share",
        in_specs=[pl.BlockSpec(memory_space=pl.ANY)] * n,
        out_specs=[pl.BlockSpec(memory_space=pl.ANY)] * n,
        out_shape=[jax.ShapeDtypeStruct(f.shape, f.dtype) for f in fulls],
        input_output_aliases={k: k for k in range(n)},
        scratch_shapes=[pltpu.SemaphoreType.DMA((n,)), pltpu.SemaphoreType.DMA((n,))],
    )(*fulls)


def _pair_sum(g, recv, pos, name):
    _, rows, cols = g.shape
    h = rows // 2
    tr = _row_tile(h, 256)
    nb = h // tr

    def body(pos_ref, g_ref, r_ref, o_ref, own_ref):
        total = g_ref[0] + r_ref[0]
        o_ref[0] = total.astype(BF16)

        @pl.when(pl.program_id(1) == pos_ref[1])
        def _():
            own_ref[...] = total

    return pl.pallas_call(
        body, name=name,
        grid_spec=pltpu.PrefetchScalarGridSpec(
            num_scalar_prefetch=1, grid=(nb, N_CHIPS),
            in_specs=[pl.BlockSpec((1, tr, cols), lambda r, s, pos_ref: (s, pos_ref[0] * nb + r, 0)),
                      pl.BlockSpec((1, tr, cols), lambda r, s, pos_ref: (s, r, 0))],
            out_specs=[pl.BlockSpec((1, tr, cols), lambda r, s, pos_ref: (s, r, 0)),
                       pl.BlockSpec((tr, cols), lambda r, s, pos_ref: (r, 0))]),
        out_shape=[jax.ShapeDtypeStruct((N_CHIPS, h, cols), BF16), jax.ShapeDtypeStruct((h, cols), F32)],
        compiler_params=_params(2, 32))(pos, g, recv)


def _chip_sum(own, rb, pos, name):
    h, cols = own.shape
    tr = _row_tile(h, 256)
    nb = h // tr

    def body(pos_ref, c_ref, r_ref, o_ref):
        del pos_ref
        o_ref[...] = ((c_ref[...] + r_ref[0].astype(F32)) + r_ref[1].astype(F32)) + r_ref[2].astype(F32)

    return pl.pallas_call(
        body, name=name,
        grid_spec=pltpu.PrefetchScalarGridSpec(
            num_scalar_prefetch=1, grid=(nb,),
            in_specs=[pl.BlockSpec((tr, cols), lambda r, pos_ref: (r, 0)),
                      pl.BlockSpec((3, tr, cols), lambda r, pos_ref: (0, r, 0))],
            out_specs=pl.BlockSpec((tr, cols), lambda r, pos_ref: (pos_ref[0] * nb + r, 0))),
        out_shape=jax.ShapeDtypeStruct((2 * h, cols), F32),
        compiler_params=_params(1, 32))(pos, own, rb)


def _adamw_big(w, g, m, v, name):
    rows, cols = w.shape
    tr = _row_tile(rows, 256)

    def body(w_ref, g_ref, m_ref, v_ref, d_ref, nm_ref, nv_ref):
        d_ref[...], nm_ref[...], nv_ref[...] = _adamw(w_ref[...], g_ref[...], m_ref[...], v_ref[...])

    spec = pl.BlockSpec((tr, cols), lambda i: (i, 0))
    return pl.pallas_call(
        body, name=name, grid=(rows // tr,),
        in_specs=[spec] * 4, out_specs=[spec] * 3,
        out_shape=[jax.ShapeDtypeStruct((rows, cols), F32)] * 3,
        compiler_params=_params(1, 48))(w, g, m, v)


def _gather_small(loss, dfg, dlg, dgn, ddwb, dclg, dclb, dpwb, ddww, dmeta):
    d = loss.shape[1]
    w = dgn.shape[1]

    def body(loss_ref, dfg_ref, dlg_ref, dgn_ref, ddwb_ref, dclg_ref, dclb_ref, dpwb_ref, ddww_ref, dmeta_ref,
             gs_ref, gd_ref, gm_ref, send_sems, recv_sems):
        x, y, c = _mesh_pos()
        me = 4 * x + 2 * y + c
        gs_ref[me, 0:1, :] = loss_ref[...]
        gs_ref[me, 1:2, :] = dfg_ref[...]
        gs_ref[me, 2:3, :] = dlg_ref[...]
        gs_ref[me, 3:4, 0:w] = dgn_ref[...]
        gs_ref[me, 3:4, w:2 * w] = ddwb_ref[...]
        gs_ref[me, 4:5, 0:w] = dclg_ref[...]
        gs_ref[me, 4:5, w:2 * w] = dclb_ref[...]
        gs_ref[me, 5:6, 0:w] = dpwb_ref[...]
        gs_ref[me, 5:6, w:2 * w] = jnp.zeros((1, d - w), F32)
        gs_ref[me, 6:8, :] = jnp.zeros((2, d), F32)
        gd_ref[me] = ddww_ref[...]
        gm_ref[me] = dmeta_ref[...]
        bufs = (gs_ref, gd_ref, gm_ref)

        def peer(j):
            return (1 - x if j & 4 else x), (1 - y if j & 2 else y), (1 - c if j & 1 else c)

        def copy(k, j, slot, to):
            return pltpu.make_async_remote_copy(
                src_ref=bufs[k].at[slot], dst_ref=bufs[k].at[slot],
                send_sem=send_sems.at[7 * k + j - 1], recv_sem=recv_sems.at[7 * k + j - 1],
                device_id=to, device_id_type=MESH)

        cps = []
        for k in range(3):
            for j in range(1, N_DEV):
                cp = copy(k, j, me, peer(j))
                cp.start()
                cps.append(cp)
        for k in range(3):
            for j in range(1, N_DEV):
                px, py, pc = peer(j)
                copy(k, j, 4 * px + 2 * py + pc, (x, y, c)).wait_recv()
        for cp in cps:
            cp.wait_send()

    vm = pl.BlockSpec(memory_space=pltpu.VMEM)
    return pl.pallas_call(
        body, name="gather_small",
        in_specs=[vm] * 10, out_specs=[vm] * 3,
        out_shape=[jax.ShapeDtypeStruct((N_DEV, 8, d), F32),
                   jax.ShapeDtypeStruct((N_DEV,) + ddww.shape, F32),
                   jax.ShapeDtypeStruct((N_DEV,) + dmeta.shape, F32)],
        scratch_shapes=[pltpu.SemaphoreType.DMA((21,)), pltpu.SemaphoreType.DMA((21,))],
    )(loss, dfg, dlg, dgn, ddwb, dclg, dclb, dpwb, ddww, dmeta)


def _small_update(s_arr, gs, gd, gm, weights, ms, vs):
    d = gs.shape[2]
    w = d // 2
    n = len(weights)

    def body(s_ref, gs_ref, gd_ref, gm_ref, *refs):
        del s_ref
        w_refs, m_refs, v_refs = refs[:n], refs[n:2 * n], refs[2 * n:3 * n]
        loss_ref = refs[3 * n]
        g_refs = refs[3 * n + 1:4 * n + 1]
        d_refs = refs[4 * n + 1:5 * n + 1]
        nm_refs = refs[5 * n + 1:6 * n + 1]
        nv_refs = refs[6 * n + 1:7 * n + 1]

        def total(ref):
            t = ref[0]
            for dev in range(1, N_DEV):
                t = t + ref[dev]
            return t

        packed = total(gs_ref)
        loss_ref[...] = jnp.sum(packed[0:1, :], axis=1, keepdims=True) * (0.5 / d)
        grads = [packed[2:3, :], packed[1:2, :], packed[3:4, 0:w], packed[3:4, w:2 * w], packed[4:5, 0:w],
                 packed[4:5, w:2 * w], packed[5:6, 0:w], total(gd_ref), total(gm_ref)]
        for k in range(n):
            g = grads[k]
            g_refs[k][...] = g
            d_refs[k][...], nm_refs[k][...], nv_refs[k][...] = _adamw(w_refs[k][...], g, m_refs[k][...], v_refs[k][...])

    def whole(shape):
        return pl.BlockSpec(shape, lambda i, s_ref: (0,) * len(shape))

    wc = weights[7].shape[1]
    mc = weights[8].shape[1]
    shapes = [a.shape for a in weights]
    in_specs = ([whole(gs.shape),
                 pl.BlockSpec((N_DEV, gd.shape[1], wc), lambda i, s_ref: (0, 0, s_ref[0])),
                 pl.BlockSpec((N_DEV, gm.shape[1], mc), lambda i, s_ref: (0, 0, s_ref[0]))]
                + [whole(s) for s in shapes] * 3)
    out_specs = [whole((1, 1))] + [whole(s) for s in shapes] * 4
    out_shape = [jax.ShapeDtypeStruct((1, 1), F32)] + [jax.ShapeDtypeStruct(s, F32) for s in shapes] * 4
    outs = pl.pallas_call(
        body, name="small_update",
        grid_spec=pltpu.PrefetchScalarGridSpec(num_scalar_prefetch=1, grid=(1,), in_specs=in_specs,
                                               out_specs=out_specs),
        out_shape=out_shape,
        compiler_params=_params(1, 32))(s_arr, gs, gd, gm, *weights, *ms, *vs)
    loss = outs[0]
    return loss, outs[1:n + 1], outs[n + 1:2 * n + 1], outs[2 * n + 1:3 * n + 1], outs[3 * n + 1:4 * n + 1]


def kernel(x, meta_tokens, ln_g, w_in, ret_gn_g, conv_dw_w, conv_dw_b, conv_ln_g, conv_ln_b, conv_pw_w, conv_pw_b, w_out, final_g, loss_target, m_meta_tokens, m_ln_g, m_w_in, m_ret_gn_g, m_conv_dw_w, m_conv_dw_b, m_conv_ln_g, m_conv_ln_b, m_conv_pw_w, m_conv_pw_b, m_w_out, m_final_g, v_meta_tokens, v_ln_g, v_w_in, v_ret_gn_g, v_conv_dw_w, v_conv_dw_b, v_conv_ln_g, v_conv_ln_b, v_conv_pw_w, v_conv_pw_b, v_w_out, v_final_g):
    seq, d = x.shape[1], x.shape[2]
    w = ret_gn_g.shape[1]
    hd = w // RET_HEADS
    lp = CHUNK + seq
    ns = w_in.shape[2]
    mx, my, mc = lax.axis_index("x"), lax.axis_index("y"), lax.axis_index("c")
    my_s = 2 * mx + my
    s_arr = my_s.astype(jnp.int32).reshape(1)
    pos = jnp.stack([mc, my_s]).astype(jnp.int32)
    order = jnp.stack([my_s, 2 * (1 - mx) + my, 2 * mx + (1 - my), 2 * (1 - mx) + (1 - my)]).astype(jnp.int32)

    dw_pad = jnp.pad(conv_dw_w[0], ((0, HALO - CONV_K), (0, 0)))
    dw4, meta4 = _gather_shards([dw_pad, meta_tokens])
    dw_full = dw4.transpose(1, 0, 2).reshape(HALO, w)
    meta_full = meta4.transpose(1, 0, 2).reshape(N_META, d)

    hp = jnp.concatenate([jnp.zeros((LEAD, d), F32), meta_full, x[0]], axis=0)
    tgtp = jnp.concatenate([jnp.zeros((CHUNK, d), F32), loss_target[0]], axis=0)
    consts = _ret_consts()
    cos, sin = _rope_tables(lp, hd // 2)
    fg2 = final_g.reshape(1, d)

    proj, r1, hn, w4, pw4, wo4 = _in_proj_gather(
        order, hp, ln_g, _cast_bf16(w_in[0], "cast_w_in"), _cast_bf16(conv_pw_w[0], "cast_pw_w"),
        _cast_bf16(w_out[0], "cast_w_out"))
    pw_full = pw4.reshape(w, w)
    wo_full = wo4.reshape(2 * w, d)
    y_ret, ssave = _ret_fwd(proj, cos, sin, ret_gn_g, consts)
    y_conv, u1 = _conv_fwd(proj, dw_full, conv_dw_b, conv_ln_g, conv_ln_b, pw_full, conv_pw_b)
    dh2, dy, loss_l, dfg = _out_proj_loss(y_ret, y_conv, hp, tgtp, wo_full, fg2)

    g_wo = _dw_out(y_ret, y_conv, dh2)
    dproj, dgn = _ret_bwd(proj, dy, ssave, cos, sin, ret_gn_g, consts)
    dproj, du1, g_pw, dpwb, dclg, dclb = _conv_bwd_pw(dy, proj, u1, conv_ln_g, conv_ln_b, pw_full, conv_pw_b, dproj)
    dproj, ddww, ddwb = _conv_bwd_dw(du1, proj, dw_full, dproj)
    g_win = _dw_in(hn, dproj, ns)

    gs = [g_win, g_wo.reshape(N_CHIPS, (2 * w) // N_CHIPS, d), g_pw.reshape(N_CHIPS, w // N_CHIPS, w)]
    recv = _rs_pair_exchange(gs)
    names = ("w_in", "w_out", "pw_w")
    sums = [_pair_sum(g, r, pos, "pair_sum_" + nm) for g, r, nm in zip(gs, recv, names)]
    dh, dlg, rb = _in_proj_bwd(dproj, w4, hp, r1, dh2, ln_g, [cs_ for cs_, _ in sums])
    grad_x = dh[CHUNK:][None]
    dmeta = dh[LEAD:CHUNK]
    fulls = [_chip_sum(own, r, pos, "chip_sum_" + nm) for (_, own), r, nm in zip(sums, rb, names)]
    grad_w_in, grad_w_out, grad_pw = _rs_pair_share(fulls)
    d_win, nm_win, nv_win = _adamw_big(w_in[0], grad_w_in, m_w_in[0], v_w_in[0], "adamw_w_in")
    d_wo, nm_wo, nv_wo = _adamw_big(w_out[0], grad_w_out, m_w_out[0], v_w_out[0], "adamw_w_out")
    d_pw, nm_pw, nv_pw = _adamw_big(conv_pw_w[0], grad_pw, m_conv_pw_w[0], v_conv_pw_w[0], "adamw_pw_w")

    gsm, gdm, gmm = _gather_small(loss_l, dfg, dlg, dgn, ddwb, dclg, dclb, dpwb, ddww, dmeta)

    def pad_dw(a):
        return jnp.pad(a[0], ((0, HALO - CONV_K), (0, 0)))

    small_w = [ln_g, fg2, ret_gn_g, conv_dw_b, conv_ln_g, conv_ln_b, conv_pw_b, dw_pad, meta_tokens]
    small_m = [m_ln_g, m_final_g.reshape(1, d), m_ret_gn_g, m_conv_dw_b, m_conv_ln_g, m_conv_ln_b, m_conv_pw_b,
               pad_dw(m_conv_dw_w), m_meta_tokens]
    small_v = [v_ln_g, v_final_g.reshape(1, d), v_ret_gn_g, v_conv_dw_b, v_conv_ln_g, v_conv_ln_b, v_conv_pw_b,
               pad_dw(v_conv_dw_w), v_meta_tokens]
    loss, sg, sd, snm, snv = _small_update(s_arr, gsm, gdm, gmm, small_w, small_m, small_v)

    def assemble(small, big_in, big_pw, big_out):
        ln, fg, gn, dwb, clg, clb, pwb, dww, meta = small
        return (meta, ln, big_in[None], gn, dww[:CONV_K][None], dwb, clg, clb, big_pw[None], pwb, big_out[None],
                fg.reshape(d))

    return (loss.reshape(()), grad_x,
            *assemble(sg, grad_w_in, grad_pw, grad_w_out),
            *assemble(sd, d_win, d_pw, d_wo),
            *assemble(snm, nm_win, nm_pw, nm_wo),
            *assemble(snv, nv_win, nv_pw, nv_wo))
```

```python
import functools

import jax
import jax.numpy as jnp
from jax import lax
from jax.experimental import pallas as pl
from jax.experimental.pallas import tpu as pltpu

F32 = jnp.float32
BF16 = jnp.bfloat16
MESH = pl.DeviceIdType.MESH

N_META = 16
CHUNK = 128
LEAD = (-N_META) % CHUNK
RET_HEADS = 4
CONV_K = 31
HALO = 32
ROPE_BASE = 10000.0
EPS = 1e-6
N_CHIPS = 4
N_DEV = 8

ADAM_LR = 0.001
ADAM_B1 = 0.9
ADAM_B2 = 0.999
ADAM_EPS = 1e-08
ADAM_WD = 0.01
ADAM_STEP = 10

MIB = 2 ** 20


def _params(n_grid_axes, vmem_mib):
    return pltpu.CompilerParams(dimension_semantics=("arbitrary",) * n_grid_axes,
                                vmem_limit_bytes=vmem_mib * MIB)


def _row_tile(n, pref):
    for t in (1664, 1280, 1024, 640, 512, 384, 320, 256, 128, 64, 32, 16, 8):
        if t <= pref and n % t == 0:
            return t
    raise ValueError(f"no row tile for {n}")


def _dot(a, b):
    return jnp.dot(a, b, preferred_element_type=F32)


def _dot_nt(a, b):
    return lax.dot_general(a, b, (((1,), (1,)), ((), ())), preferred_element_type=F32)


def _dot_tn(a, b):
    return lax.dot_general(a, b, (((0,), (0,)), ((), ())), preferred_element_type=F32)


def _sigmoid(x):
    return jax.nn.sigmoid(x)


def _dsilu(x, s):
    return s * (1.0 + x * (1.0 - s))


def _mean(x):
    return jnp.mean(x, axis=-1, keepdims=True)


def _colsum(x):
    return jnp.sum(x, axis=0, keepdims=True)


def _rope(x, cos, sin):
    half = x.shape[-1] // 2
    x1, x2 = x[:, :half], x[:, half:]
    return jnp.concatenate([x1 * cos - x2 * sin, x1 * sin + x2 * cos], axis=-1)


def _rope_t(d, cos, sin):
    half = d.shape[-1] // 2
    d1, d2 = d[:, :half], d[:, half:]
    return jnp.concatenate([d1 * cos + d2 * sin, d2 * cos - d1 * sin], axis=-1)


def _adamw(w, g, m, v):
    m = ADAM_B1 * m + (1.0 - ADAM_B1) * g
    v = ADAM_B2 * v + (1.0 - ADAM_B2) * (g * g)
    m_hat = m / (1.0 - ADAM_B1 ** ADAM_STEP)
    v_hat = v / (1.0 - ADAM_B2 ** ADAM_STEP)
    delta = -ADAM_LR * (m_hat / (jnp.sqrt(v_hat) + ADAM_EPS) + ADAM_WD * w)
    return delta, m, v


def _ret_consts():
    h = jnp.arange(RET_HEADS, dtype=F32)
    log_g = jnp.log(1.0 - jnp.exp2(-5.0 - h))
    idx = jnp.arange(CHUNK, dtype=F32)
    rel = idx[:, None] - idx[None, :]
    dmask = jnp.where(rel[None] >= 0, jnp.exp(jnp.maximum(rel, 0.0)[None] * log_g[:, None, None]), 0.0)
    qd = jnp.exp((idx[None, :] + 1.0) * log_g[:, None])[:, :, None]
    kd = jnp.exp((CHUNK - 1.0 - idx[None, :]) * log_g[:, None])[:, :, None]
    cd = jnp.exp(CHUNK * log_g)[:, None, None]
    return dmask, qd, kd, cd


def _rope_tables(n_rows, half):
    pos = jnp.arange(n_rows, dtype=F32) - float(LEAD)
    inv_freq = ROPE_BASE ** (-jnp.arange(half, dtype=F32) / half)
    ang = pos[:, None] * inv_freq[None, :]
    return jnp.cos(ang), jnp.sin(ang)


def _cast_bf16(a, name):
    rows, cols = a.shape
    tr = _row_tile(rows, 256)

    def body(a_ref, o_ref):
        o_ref[...] = a_ref[...].astype(BF16)

    return pl.pallas_call(
        body, name=name, grid=(rows // tr,),
        in_specs=[pl.BlockSpec((tr, cols), lambda i: (i, 0))],
        out_specs=pl.BlockSpec((tr, cols), lambda i: (i, 0)),
        out_shape=jax.ShapeDtypeStruct((rows, cols), BF16),
        compiler_params=_params(1, 32))(a)


def _in_proj_gather(order, hp, ln_g, w_own, pw_own, wo_own):
    lp, d = hp.shape
    ns = w_own.shape[1]
    tm = _row_tile(lp, 640)
    nt = lp // tm
    assert nt >= 2, "the hn write-back of a row tile is waited for one step later, before any pass re-reads it"
    shards = (w_own, pw_own, wo_own)
    halves = [a.shape[0] // 2 for a in shards]
    n = len(shards)

    def body(order_ref, h_ref, g_ref, w_in, pw_in, wo_in, proj_ref, r_ref, hn_hbm, w4, pw4, wo4,
             wbuf, hnbuf, send_sems, recv_sems, loc_sems, hn_out_sems, hn_in_sems, w_sem):
        del order_ref
        t, i = pl.program_id(0), pl.program_id(1)
        slot = (t * nt + i) % 2
        x, y, c = _mesh_pos()
        me, sibling = (x, y, c), (x, y, 1 - c)
        my_s = 2 * x + y
        chips = _other_chips(x, y)
        ins, outs = (w_in, pw_in, wo_in), (w4, pw4, wo4)

        def half(k, s, cc):
            return outs[k].at[s, pl.ds(cc * halves[k], halves[k])]

        def rcopy(k, j, src, dst, to):
            return pltpu.make_async_remote_copy(
                src_ref=src, dst_ref=dst, send_sem=send_sems.at[6 * k + j], recv_sem=recv_sems.at[6 * k + j],
                device_id=to, device_id_type=MESH)

        def send(k, j):
            return rcopy(k, j, ins[k].at[pl.ds(c * halves[k], halves[k])], half(k, my_s, c), (*chips[j], c))

        def shard_of(j):
            return 2 * chips[j][0] + chips[j][1]

        def forward(k, j):
            return rcopy(k, 3 + j, half(k, shard_of(j), c), half(k, shard_of(j), c), sibling)

        def land(k, j):
            rcopy(k, j, half(k, shard_of(j), c), half(k, shard_of(j), c), me).wait_recv()
            forward(k, j).start()

        def landed_from_sibling(k, j):
            rcopy(k, 3 + j, half(k, shard_of(j), 1 - c), half(k, shard_of(j), 1 - c), me).wait_recv()

        def load_w(src):
            cp = pltpu.make_async_copy(src, wbuf, w_sem)
            cp.start()
            cp.wait()

        def own_w_copy():
            return pltpu.make_async_copy(wbuf, w4.at[my_s], loc_sems.at[0])

        def own_copies():
            return [pltpu.make_async_copy(ins[k], outs[k].at[my_s], loc_sems.at[k]) for k in range(1, n)]

        def hn_out(sl, row_tile):
            return pltpu.make_async_copy(hnbuf.at[sl], hn_hbm.at[pl.ds(row_tile * tm, tm)], hn_out_sems.at[sl])

        def hn_in(sl, row_tile):
            return pltpu.make_async_copy(hn_hbm.at[pl.ds(row_tile * tm, tm)], hnbuf.at[sl], hn_in_sems.at[sl])

        @pl.when((t == 0) & (i == 0))
        def _():
            load_w(w_in)
            for j in range(2):
                send(0, j).start()
            own_w_copy().start()
            for cp in own_copies():
                cp.start()

        for j in range(3):
            @pl.when((t == j + 1) & (i == 0))
            def _(j=j):
                if j == 0:
                    own_w_copy().wait()
                    send(0, 2).start()
                    for k in range(1, n):
                        for jj in range(3):
                            send(k, jj).start()
                land(0, j)
                landed_from_sibling(0, j)
                load_w(w4.at[shard_of(j)])

        @pl.when(t == 0)
        def _():
            h = h_ref[...]
            r = lax.rsqrt(_mean(h * h) + EPS)
            hnbuf[slot] = ((h * r) * g_ref[...]).astype(BF16)
            r_ref[...] = r
            hn_out(slot, i).start()

        @pl.when(t > 0)
        def _():
            hn_in(slot, i).wait()

        proj_ref[...] = _dot(hnbuf[slot], wbuf[...])

        @pl.when(((t == 0) & (i > 0)) | ((t == 1) & (i == 0)))
        def _():
            hn_out(1 - slot, jnp.where(i > 0, i - 1, nt - 1)).wait()

        last = (t == N_CHIPS - 1) & (i == nt - 1)

        @pl.when(((t > 0) | (i == nt - 1)) & jnp.logical_not(last))
        def _():
            hn_in(1 - slot, jnp.where(i == nt - 1, 0, i + 1)).start()

        @pl.when(last)
        def _():
            for k in range(1, n):
                for j in range(3):
                    land(k, j)
            for k in range(1, n):
                for j in range(3):
                    landed_from_sibling(k, j)
            for k in range(n):
                for j in range(3):
                    send(k, j).wait_send()
                    forward(k, j).wait_send()
            for cp in own_copies():
                cp.wait()

    def frozen(t, i):
        return jnp.where(t == 0, i, nt - 1)

    any_spec = pl.BlockSpec(memory_space=pl.ANY)
    return pl.pallas_call(
        body, name="in_proj_gather",
        grid_spec=pltpu.PrefetchScalarGridSpec(
            num_scalar_prefetch=1, grid=(N_CHIPS, nt),
            in_specs=[pl.BlockSpec((tm, d), lambda t, i, o: (frozen(t, i), 0)),
                      pl.BlockSpec((1, d), lambda t, i, o: (0, 0)),
                      any_spec, any_spec, any_spec],
            out_specs=[pl.BlockSpec((tm, ns), lambda t, i, o: (i, o[t])),
                       pl.BlockSpec((tm, 1), lambda t, i, o: (frozen(t, i), 0)),
                       any_spec, any_spec, any_spec, any_spec],
            scratch_shapes=[pltpu.VMEM((d, ns), BF16), pltpu.VMEM((2, tm, d), BF16),
                            pltpu.SemaphoreType.DMA((6 * n,)), pltpu.SemaphoreType.DMA((6 * n,)),
                            pltpu.SemaphoreType.DMA((n,)), pltpu.SemaphoreType.DMA((2,)),
                            pltpu.SemaphoreType.DMA((2,)), pltpu.SemaphoreType.DMA]),
        out_shape=[jax.ShapeDtypeStruct((lp, N_CHIPS * ns), F32),
                   jax.ShapeDtypeStruct((lp, 1), F32),
                   jax.ShapeDtypeStruct((lp, d), BF16)]
                  + [jax.ShapeDtypeStruct((N_CHIPS,) + a.shape, a.dtype) for a in shards],
        compiler_params=_params(2, 48))(order, hp, ln_g, w_own, pw_own, wo_own)


def _ret_fwd(proj, cos, sin, gn_g, consts):
    lp = proj.shape[0]
    w = gn_g.shape[1]
    hd = w // RET_HEADS
    nch = lp // CHUNK
    dmask, qd, kd, cd = consts

    def body(q_ref, k_ref, v_ref, g_ref, cos_ref, sin_ref, gn_ref, m_ref, qd_ref, kd_ref, cd_ref,
             y_ref, ssave_ref, s_scr):
        @pl.when(pl.program_id(0) == 0)
        def _():
            s_scr[...] = jnp.zeros_like(s_scr)

        cos_t, sin_t = cos_ref[...], sin_ref[...]
        for h in range(RET_HEADS):
            sl = slice(h * hd, (h + 1) * hd)
            qr = _rope(q_ref[:, sl], cos_t, sin_t)
            kr = _rope(k_ref[:, sl], cos_t, sin_t) * (hd ** -0.5)
            vb = v_ref[:, sl].astype(BF16)
            sc = _dot_nt(qr.astype(BF16), kr.astype(BF16)) * m_ref[h]
            state = s_scr[h]
            sb = state.astype(BF16)
            ssave_ref[0, h] = sb
            out = _dot(sc.astype(BF16), vb) + _dot((qr * qd_ref[h]).astype(BF16), sb)
            s_scr[h] = state * cd_ref[h] + _dot_tn((kr * kd_ref[h]).astype(BF16), vb)
            dev = out - _mean(out)
            yn = dev * lax.rsqrt(_mean(dev * dev) + EPS)
            g = g_ref[:, sl]
            y_ref[:, sl] = ((yn * gn_ref[:, sl]) * (g * _sigmoid(g))).astype(BF16)

    def col(j):
        return pl.BlockSpec((CHUNK, w), lambda i: (i, j))

    def whole(a):
        return pl.BlockSpec(a.shape, lambda i: (0,) * a.ndim)

    return pl.pallas_call(
        body, name="ret_fwd", grid=(nch,),
        in_specs=[col(0), col(1), col(2), col(3),
                  pl.BlockSpec((CHUNK, hd // 2), lambda i: (i, 0)),
                  pl.BlockSpec((CHUNK, hd // 2), lambda i: (i, 0)),
                  whole(gn_g), whole(dmask), whole(qd), whole(kd), whole(cd)],
        out_specs=[pl.BlockSpec((CHUNK, w), lambda i: (i, 0)),
                   pl.BlockSpec((1, RET_HEADS, hd, hd), lambda i: (i, 0, 0, 0))],
        out_shape=[jax.ShapeDtypeStruct((lp, w), BF16),
                   jax.ShapeDtypeStruct((nch, RET_HEADS, hd, hd), BF16)],
        scratch_shapes=[pltpu.VMEM((RET_HEADS, hd, hd), F32)],
        compiler_params=_params(1, 32))(proj, proj, proj, proj, cos, sin, gn_g, dmask, qd, kd, cd)


def _tap_groups(start, flip):
    groups = {}
    for j in range(CONV_K):
        o = start + (CONV_K - 1 - j if flip else j)
        groups.setdefault(o % 8, []).append((o // 8, j))
    return groups


def _shift_up(win, s):
    return win if s == 0 else pltpu.roll(win, win.shape[0] - s, axis=0)


def _dw_taps(src_ref, w_ref, dst_ref, bias, *, rows, start, flip, rb):
    cw = dst_ref.shape[1]
    lb = min(128, cw)
    groups = _tap_groups(start, flip)

    def rb_body(r, carry):
        base = pl.multiple_of(r * rb, rb)
        for cb in range(cw // lb):
            ls = slice(cb * lb, (cb + 1) * lb)
            win = src_ref[pl.ds(base, rb + HALO), ls]
            acc = jnp.zeros((rb, lb), F32) if bias is None else jnp.broadcast_to(bias[:, ls], (rb, lb))
            for s, taps in groups.items():
                ws = _shift_up(win, s)
                for a, j in taps:
                    acc = acc + ws[8 * a:8 * a + rb, :] * w_ref[j:j + 1, ls]
            dst_ref[pl.ds(base, rb), ls] = acc
        return carry

    lax.fori_loop(0, rows // rb, rb_body, 0)


def _conv_fwd(proj, dw_w, dw_b, cln_g, cln_b, pw_w, pw_b):
    lp = proj.shape[0]
    cw = dw_b.shape[1]
    tm = _row_tile(lp, 640)
    rb = _row_tile(tm, 64)

    def body(a_ref, b_ref, gc_ref, w_ref, wb_ref, lg_ref, lb_ref, pw_ref, pb_ref, y_ref, u1_ref, buf):
        @pl.when(pl.program_id(0) == 0)
        def _():
            buf[0:HALO, :] = jnp.zeros((HALO, cw), F32)

        buf[HALO:HALO + tm, :] = a_ref[...] * _sigmoid(b_ref[...])
        _dw_taps(buf, w_ref, u1_ref, wb_ref[...], rows=tm, start=HALO - (CONV_K - 1), flip=False, rb=rb)
        buf[0:HALO, :] = buf[tm:tm + HALO, :]
        u1 = u1_ref[...]
        dev = u1 - _mean(u1)
        z = dev * lax.rsqrt(_mean(dev * dev) + EPS) * lg_ref[...] + lb_ref[...]
        u3 = (z * _sigmoid(z)).astype(BF16)
        u4 = _dot(u3, pw_ref[...]) + pb_ref[...]
        gc = gc_ref[...]
        y_ref[...] = (u4 * (gc * _sigmoid(gc))).astype(BF16)

    def col(j):
        return pl.BlockSpec((tm, cw), lambda i: (i, j))

    def whole(a):
        return pl.BlockSpec(a.shape, lambda i: (0,) * a.ndim)

    return pl.pallas_call(
        body, name="conv_fwd", grid=(lp // tm,),
        in_specs=[col(4), col(5), col(6), whole(dw_w), whole(dw_b), whole(cln_g), whole(cln_b),
                  whole(pw_w), whole(pw_b)],
        out_specs=[pl.BlockSpec((tm, cw), lambda i: (i, 0)), pl.BlockSpec((tm, cw), lambda i: (i, 0))],
        out_shape=[jax.ShapeDtypeStruct((lp, cw), BF16), jax.ShapeDtypeStruct((lp, cw), F32)],
        scratch_shapes=[pltpu.VMEM((tm + HALO, cw), F32)],
        compiler_params=_params(1, 48))(proj, proj, proj, dw_w, dw_b, cln_g, cln_b, pw_w, pw_b)


def _out_proj_loss(yr, yc, hp, tgtp, w_out, final_g):
    lp, d = hp.shape
    w = yr.shape[1]
    tm = _row_tile(lp, 320)

    def body(yr_ref, yc_ref, h_ref, t_ref, w_ref, fg_ref, dh2_ref, dy_ref, loss_ref, dfg_ref):
        i = pl.program_id(0)

        @pl.when(i == 0)
        def _():
            loss_ref[...] = jnp.zeros_like(loss_ref)
            dfg_ref[...] = jnp.zeros_like(dfg_ref)

        h2 = h_ref[...] + (_dot(yr_ref[...], w_ref[0:w, :]) + _dot(yc_ref[...], w_ref[w:2 * w, :]))
        r2 = lax.rsqrt(_mean(h2 * h2) + EPS)
        h2n = h2 * r2
        fg = fg_ref[...]
        rows = i * tm + lax.broadcasted_iota(jnp.int32, (tm, 1), 0)
        err = jnp.where(rows >= CHUNK, h2n * fg - t_ref[...], 0.0)
        loss_ref[...] += _colsum(err * err)
        dout = err * (1.0 / d)
        dfg_ref[...] += _colsum(dout * h2n)
        dz = dout * fg
        dh2 = r2 * (dz - h2n * _mean(dz * h2n))
        dh2_ref[...] = dh2
        db = dh2.astype(BF16)
        dy_ref[:, 0:w] = _dot_nt(db, w_ref[0:w, :])
        dy_ref[:, w:2 * w] = _dot_nt(db, w_ref[w:2 * w, :])

    def row(cols):
        return pl.BlockSpec((tm, cols), lambda i: (i, 0))

    return pl.pallas_call(
        body, name="out_proj_loss", grid=(lp // tm,),
        in_specs=[row(w), row(w), row(d), row(d),
                  pl.BlockSpec(memory_space=pltpu.VMEM),
                  pl.BlockSpec((1, d), lambda i: (0, 0))],
        out_specs=[row(d), row(2 * w), pl.BlockSpec((1, d), lambda i: (0, 0)),
                   pl.BlockSpec((1, d), lambda i: (0, 0))],
        out_shape=[jax.ShapeDtypeStruct((lp, d), F32), jax.ShapeDtypeStruct((lp, 2 * w), F32),
                   jax.ShapeDtypeStruct((1, d), F32), jax.ShapeDtypeStruct((1, d), F32)],
        compiler_params=_params(1, 56))(yr, yc, hp, tgtp, w_out, final_g)


def _dw_out(yr, yc, dh2):
    lp, d = dh2.shape
    w = yr.shape[1]
    tm = _row_tile(lp, 640)
    nb = 2
    dn = d // nb

    def body(yr_ref, yc_ref, d_ref, o_ref):
        @pl.when(pl.program_id(1) == 0)
        def _():
            o_ref[...] = jnp.zeros_like(o_ref)

        db = d_ref[...].astype(BF16)
        o_ref[0:w, :] += _dot_tn(yr_ref[...], db)
        o_ref[w:2 * w, :] += _dot_tn(yc_ref[...], db)

    return pl.pallas_call(
        body, name="dw_out", grid=(nb, lp // tm),
        in_specs=[pl.BlockSpec((tm, w), lambda n, i: (i, 0)),
                  pl.BlockSpec((tm, w), lambda n, i: (i, 0)),
                  pl.BlockSpec((tm, dn), lambda n, i: (i, n))],
        out_specs=pl.BlockSpec((2 * w, dn), lambda n, i: (0, n)),
        out_shape=jax.ShapeDtypeStruct((2 * w, d), F32),
        compiler_params=_params(2, 48))(yr, yc, dh2)


def _ret_bwd(proj, dy, ssave, cos, sin, gn_g, consts):
    lp = proj.shape[0]
    w = gn_g.shape[1]
    hd = w // RET_HEADS
    nch = lp // CHUNK
    dmask, qd, kd, cd = consts

    def body(q_ref, k_ref, v_ref, g_ref, dy_ref, ss_ref, cos_ref, sin_ref, gn_ref, m_ref, qd_ref, kd_ref,
             cd_ref, dp_ref, dgn_ref, ds_scr):
        @pl.when(pl.program_id(0) == 0)
        def _():
            ds_scr[...] = jnp.zeros_like(ds_scr)
            dgn_ref[...] = jnp.zeros_like(dgn_ref)

        cos_t, sin_t = cos_ref[...], sin_ref[...]
        for h in range(RET_HEADS):
            sl = slice(h * hd, (h + 1) * hd)
            qr = _rope(q_ref[:, sl], cos_t, sin_t)
            kr = _rope(k_ref[:, sl], cos_t, sin_t) * (hd ** -0.5)
            qb, kb = qr.astype(BF16), kr.astype(BF16)
            vb = v_ref[:, sl].astype(BF16)
            sb = ss_ref[0, h]
            mask = m_ref[h]
            qdec, kdec = qd_ref[h], kd_ref[h]
            scb = (_dot_nt(qb, kb) * mask).astype(BF16)
            qdq = (qr * qdec).astype(BF16)
            kdk = (kr * kdec).astype(BF16)
            out = _dot(scb, vb) + _dot(qdq, sb)
            dev = out - _mean(out)
            rstd = lax.rsqrt(_mean(dev * dev) + EPS)
            yn = dev * rstd
            g = g_ref[:, sl]
            sg = _sigmoid(g)
            gng = gn_ref[:, sl]
            dyv = dy_ref[:, sl]
            dgr = dyv * (yn * gng) * _dsilu(g, sg)
            silu_g = g * sg
            dgn_ref[:, sl] += _colsum(dyv * yn * silu_g)
            dyn = dyv * gng * silu_g
            dout = rstd * (dyn - _mean(dyn) - yn * _mean(dyn * yn))
            dob = dout.astype(BF16)
            dscb = (_dot_nt(dob, vb) * mask).astype(BF16)
            dstate = ds_scr[h]
            dsb = dstate.astype(BF16)
            dq = _dot(dscb, kb) + _dot_nt(dob, sb) * qdec
            dk = _dot_tn(dscb, qb) + _dot_nt(vb, dsb) * kdec
            dv = _dot_tn(scb, dob) + _dot(kdk, dsb)
            ds_scr[h] = dstate * cd_ref[h] + _dot_tn(qdq, dob)
            dp_ref[:, 0 * w + h * hd:0 * w + (h + 1) * hd] = _rope_t(dq, cos_t, sin_t).astype(BF16)
            dp_ref[:, 1 * w + h * hd:1 * w + (h + 1) * hd] = (_rope_t(dk, cos_t, sin_t) * (hd ** -0.5)).astype(BF16)
            dp_ref[:, 2 * w + h * hd:2 * w + (h + 1) * hd] = dv.astype(BF16)
            dp_ref[:, 3 * w + h * hd:3 * w + (h + 1) * hd] = dgr.astype(BF16)

    def rev(i):
        return nch - 1 - i

    def col(j):
        return pl.BlockSpec((CHUNK, w), lambda i: (rev(i), j))

    def whole(a):
        return pl.BlockSpec(a.shape, lambda i: (0,) * a.ndim)

    return pl.pallas_call(
        body, name="ret_bwd", grid=(nch,),
        in_specs=[col(0), col(1), col(2), col(3),
                  pl.BlockSpec((CHUNK, w), lambda i: (rev(i), 0)),
                  pl.BlockSpec((1, RET_HEADS, hd, hd), lambda i: (rev(i), 0, 0, 0)),
                  pl.BlockSpec((CHUNK, hd // 2), lambda i: (rev(i), 0)),
                  pl.BlockSpec((CHUNK, hd // 2), lambda i: (rev(i), 0)),
                  whole(gn_g), whole(dmask), whole(qd), whole(kd), whole(cd)],
        out_specs=[pl.BlockSpec((CHUNK, 4 * w), lambda i: (rev(i), 0)),
                   pl.BlockSpec((1, w), lambda i: (0, 0))],
        out_shape=[jax.ShapeDtypeStruct((lp, 7 * w), BF16), jax.ShapeDtypeStruct((1, w), F32)],
        scratch_shapes=[pltpu.VMEM((RET_HEADS, hd, hd), F32)],
        compiler_params=_params(1, 32))(proj, proj, proj, proj, dy, ssave, cos, sin, gn_g, dmask, qd, kd, cd)


def _conv_bwd_pw(dy, proj, u1, cln_g, cln_b, pw_w, pw_b, dproj):
    lp, cw = u1.shape
    tm = _row_tile(lp, 320)

    def body(dy_ref, gc_ref, u1_ref, lg_ref, lb_ref, pw_ref, pb_ref, dp_in, dp_ref, du1_ref, dpw_ref,
             dpb_ref, dlg_ref, dlb_ref):
        del dp_in

        @pl.when(pl.program_id(0) == 0)
        def _():
            dpw_ref[...] = jnp.zeros_like(dpw_ref)
            dpb_ref[...] = jnp.zeros_like(dpb_ref)
            dlg_ref[...] = jnp.zeros_like(dlg_ref)
            dlb_ref[...] = jnp.zeros_like(dlb_ref)

        u1 = u1_ref[...]
        dev = u1 - _mean(u1)
        rstd = lax.rsqrt(_mean(dev * dev) + EPS)
        u1n = dev * rstd
        lg = lg_ref[...]
        z = u1n * lg + lb_ref[...]
        sz = _sigmoid(z)
        u3b = (z * sz).astype(BF16)
        u4 = _dot(u3b, pw_ref[...]) + pb_ref[...]
        gc = gc_ref[...]
        sgc = _sigmoid(gc)
        dyc = dy_ref[...]
        du4 = dyc * (gc * sgc)
        dp_ref[...] = (dyc * u4 * _dsilu(gc, sgc)).astype(BF16)
        du4b = du4.astype(BF16)
        dpb_ref[...] += _colsum(du4)
        dpw_ref[...] += _dot_tn(u3b, du4b)
        dz = _dot_nt(du4b, pw_ref[...]) * _dsilu(z, sz)
        dlg_ref[...] += _colsum(dz * u1n)
        dlb_ref[...] += _colsum(dz)
        dn = dz * lg
        du1_ref[...] = rstd * (dn - _mean(dn) - u1n * _mean(dn * u1n))

    def row(j):
        return pl.BlockSpec((tm, cw), lambda i: (i, j))

    def whole(a):
        return pl.BlockSpec(a.shape, lambda i: (0,) * a.ndim)

    def acc(r):
        return pl.BlockSpec((r, cw), lambda i: (0, 0))

    return pl.pallas_call(
        body, name="conv_bwd_pw", grid=(lp // tm,),
        in_specs=[row(1), row(6), row(0), whole(cln_g), whole(cln_b), whole(pw_w), whole(pw_b),
                  pl.BlockSpec(memory_space=pl.ANY)],
        out_specs=[row(6), row(0), acc(cw), acc(1), acc(1), acc(1)],
        out_shape=[jax.ShapeDtypeStruct(dproj.shape, dproj.dtype), jax.ShapeDtypeStruct((lp, cw), F32),
                   jax.ShapeDtypeStruct((cw, cw), F32), jax.ShapeDtypeStruct((1, cw), F32),
                   jax.ShapeDtypeStruct((1, cw), F32), jax.ShapeDtypeStruct((1, cw), F32)],
        input_output_aliases={7: 0},
        compiler_params=_params(1, 48))(dy, proj, u1, cln_g, cln_b, pw_w, pw_b, dproj)


def _conv_bwd_dw(du1, proj, dw_w, dproj):
    lp, cw = du1.shape
    tm = _row_tile(lp, 640)
    rb = _row_tile(tm, 64)
    nt = lp // tm
    hb = tm // HALO

    def body(a_ref, b_ref, du_ref, nx_ref, w_ref, dp_in, dp_ref, dww_ref, dwb_ref, ubuf, dbuf, du0, acc):
        del dp_in
        i = pl.program_id(0)

        @pl.when(i == 0)
        def _():
            ubuf[0:HALO, :] = jnp.zeros((HALO, cw), F32)
            acc[...] = jnp.zeros_like(acc)
            dwb_ref[...] = jnp.zeros_like(dwb_ref)

        a = a_ref[...]
        sb = _sigmoid(b_ref[...])
        ubuf[HALO:HALO + tm, :] = a * sb
        du = du_ref[...]
        dbuf[0:tm, :] = du
        dbuf[tm:tm + HALO, :] = jnp.where(i == nt - 1, 0.0, nx_ref[...])
        dwb_ref[...] += _colsum(du)
        _dw_taps(dbuf, w_ref, du0, None, rows=tm, start=0, flip=True, rb=rb)
        d0 = du0[...]
        dp_ref[:, 0:cw] = (d0 * sb).astype(BF16)
        dp_ref[:, cw:2 * cw] = (d0 * a * sb * (1.0 - sb)).astype(BF16)

        lb = min(128, cw)
        groups = _tap_groups(HALO - (CONV_K - 1), False)

        def rb_body(r, carry):
            base = pl.multiple_of(r * rb, rb)
            for cb in range(cw // lb):
                ls = slice(cb * lb, (cb + 1) * lb)
                win = ubuf[pl.ds(base, rb + HALO), ls]
                dv = dbuf[pl.ds(base, rb), ls]
                for s, taps in groups.items():
                    ws = _shift_up(win, s)
                    for a, j in taps:
                        prod = dv * ws[8 * a:8 * a + rb, :]
                        acc[8 * j:8 * j + 8, ls] += jnp.sum(prod.reshape(rb // 8, 8, lb), axis=0)
            return carry

        lax.fori_loop(0, tm // rb, rb_body, 0)
        ubuf[0:HALO, :] = ubuf[tm:tm + HALO, :]

        @pl.when(i == nt - 1)
        def _():
            for j in range(CONV_K):
                dww_ref[j:j + 1, :] = _colsum(acc[8 * j:8 * j + 8, :])
            dww_ref[CONV_K:HALO, :] = jnp.zeros((HALO - CONV_K, cw), F32)

    def col(j):
        return pl.BlockSpec((tm, cw), lambda i: (i, j))

    return pl.pallas_call(
        body, name="conv_bwd_dw", grid=(nt,),
        in_specs=[col(4), col(5), col(0),
                  pl.BlockSpec((HALO, cw), lambda i: (jnp.minimum((i + 1) * hb, nt * hb - 1), 0)),
                  pl.BlockSpec(dw_w.shape, lambda i: (0, 0)),
                  pl.BlockSpec(memory_space=pl.ANY)],
        out_specs=[pl.BlockSpec((tm, 2 * cw), lambda i: (i, 2)),
                   pl.BlockSpec((HALO, cw), lambda i: (0, 0)),
                   pl.BlockSpec((1, cw), lambda i: (0, 0))],
        out_shape=[jax.ShapeDtypeStruct(dproj.shape, dproj.dtype), jax.ShapeDtypeStruct((HALO, cw), F32),
                   jax.ShapeDtypeStruct((1, cw), F32)],
        scratch_shapes=[pltpu.VMEM((tm + HALO, cw), F32), pltpu.VMEM((tm + HALO, cw), F32),
                        pltpu.VMEM((tm, cw), F32), pltpu.VMEM((8 * HALO, cw), F32)],
        input_output_aliases={5: 0},
        compiler_params=_params(1, 56))(proj, proj, du1, du1, dw_w, dproj)


def _dw_in(hn, dproj, ns):
    lp, d = hn.shape
    tm = _row_tile(lp, 640)

    def body(hn_ref, dp_ref, o_ref):
        @pl.when(pl.program_id(1) == 0)
        def _():
            o_ref[...] = jnp.zeros_like(o_ref)

        o_ref[0] += _dot_tn(hn_ref[...], dp_ref[...])

    return pl.pallas_call(
        body, name="dw_in", grid=(N_CHIPS, lp // tm),
        in_specs=[pl.BlockSpec((tm, d), lambda s, i: (i, 0)),
                  pl.BlockSpec((tm, ns), lambda s, i: (i, s))],
        out_specs=pl.BlockSpec((1, d, ns), lambda s, i: (s, 0, 0)),
        out_shape=jax.ShapeDtypeStruct((N_CHIPS, d, ns), F32),
        compiler_params=_params(2, 56))(hn, dproj)


def _in_proj_bwd(dproj, w4, hp, r1, dh2, ln_g, cs):
    lp, d = hp.shape
    ns = w4.shape[2]
    tm = _row_tile(lp, 320)
    nt = lp // tm
    n = len(cs)

    def body(dp_ref, w_ref, h_ref, r_ref, d2_ref, g_ref, *refs):
        cs_refs, (dh_ref, dlg_ref), rb_refs = refs[:n], refs[n:n + 2], refs[n + 2:2 * n + 2]
        send_sems, recv_sems = refs[2 * n + 2:]
        i = pl.program_id(0)
        x, y, c = _mesh_pos()

        def exchange():
            return [pltpu.make_async_remote_copy(
                src_ref=cs_refs[k].at[2 * chip[0] + chip[1]], dst_ref=rb_refs[k].at[j],
                send_sem=send_sems.at[3 * k + j], recv_sem=recv_sems.at[3 * k + j],
                device_id=(*chip, c), device_id_type=MESH)
                for k in range(n) for j, chip in enumerate(_other_chips(x, y))]

        @pl.when(i == 0)
        def _():
            dlg_ref[...] = jnp.zeros_like(dlg_ref)
            for cp in exchange():
                cp.start()

        dhn = _dot_nt(dp_ref[:, 0:ns], w_ref[0])
        for s in range(1, N_CHIPS):
            dhn = dhn + _dot_nt(dp_ref[:, s * ns:(s + 1) * ns], w_ref[s])
        r = r_ref[...]
        hn0 = h_ref[...] * r
        dlg_ref[...] += _colsum(dhn * hn0)
        t = dhn * g_ref[...]
        dh_ref[...] = d2_ref[...] + r * (t - hn0 * _mean(t * hn0))

        @pl.when(i == nt - 1)
        def _():
            for cp in exchange():
                cp.wait()

    def row(cols):
        return pl.BlockSpec((tm, cols), lambda i: (i, 0))

    any_spec = pl.BlockSpec(memory_space=pl.ANY)
    outs = pl.pallas_call(
        body, name="in_proj_bwd", grid=(nt,),
        in_specs=[row(N_CHIPS * ns), pl.BlockSpec(memory_space=pltpu.VMEM),
                  row(d), row(1), row(d), pl.BlockSpec((1, d), lambda i: (0, 0))] + [any_spec] * n,
        out_specs=[row(d), pl.BlockSpec((1, d), lambda i: (0, 0))] + [any_spec] * n,
        out_shape=[jax.ShapeDtypeStruct((lp, d), F32), jax.ShapeDtypeStruct((1, d), F32)]
                  + [jax.ShapeDtypeStruct((3,) + a.shape[1:], a.dtype) for a in cs],
        scratch_shapes=[pltpu.SemaphoreType.DMA((3 * n,)), pltpu.SemaphoreType.DMA((3 * n,))],
        compiler_params=_params(1, 58))(dproj, w4, hp, r1, dh2, ln_g, *cs)
    return outs[0], outs[1], outs[2:]


def _mesh_pos():
    return lax.axis_index("x"), lax.axis_index("y"), lax.axis_index("c")


def _other_chips(x, y):
    return [(1 - x, y), (x, 1 - y), (1 - x, 1 - y)]


def _gather_shards(shards):
    n = len(shards)
    halves = [a.shape[0] // 2 for a in shards]

    def body(*refs):
        ins, outs = refs[:n], refs[n:2 * n]
        send_sems, recv_sems, loc_sems = refs[2 * n:]
        x, y, c = _mesh_pos()
        me, sibling = (x, y, c), (x, y, 1 - c)
        my_s = 2 * x + y
        chips = _other_chips(x, y)

        def half(k, s, cc):
            return outs[k].at[s, pl.ds(cc * halves[k], halves[k])]

        def rcopy(k, j, src, dst, to):
            return pltpu.make_async_remote_copy(
                src_ref=src, dst_ref=dst, send_sem=send_sems.at[6 * k + j], recv_sem=recv_sems.at[6 * k + j],
                device_id=to, device_id_type=MESH)

        local = [pltpu.make_async_copy(ins[k], outs[k].at[my_s], loc_sems.at[k]) for k in range(n)]
        for cp in local:
            cp.start()
        started = []
        for k in range(n):
            for j, chip in enumerate(chips):
                cp = rcopy(k, j, ins[k].at[pl.ds(c * halves[k], halves[k])], half(k, my_s, c), (*chip, c))
                cp.start()
                started.append(cp)
        for j, chip in enumerate(chips):
            s_j = 2 * chip[0] + chip[1]
            for k in range(n):
                rcopy(k, j, half(k, s_j, c), half(k, s_j, c), me).wait_recv()
                cp = rcopy(k, 3 + j, half(k, s_j, c), half(k, s_j, c), sibling)
                cp.start()
                started.append(cp)
        for j, chip in enumerate(chips):
            s_j = 2 * chip[0] + chip[1]
            for k in range(n):
                rcopy(k, 3 + j, half(k, s_j, 1 - c), half(k, s_j, 1 - c), me).wait_recv()
        for cp in started:
            cp.wait_send()
        for cp in local:
            cp.wait()

    return pl.pallas_call(
        body, name="gather_weights",
        in_specs=[pl.BlockSpec(memory_space=pl.ANY)] * n,
        out_specs=[pl.BlockSpec(memory_space=pl.ANY)] * n,
        out_shape=[jax.ShapeDtypeStruct((N_CHIPS,) + a.shape, a.dtype) for a in shards],
        scratch_shapes=[pltpu.SemaphoreType.DMA((6 * n,)), pltpu.SemaphoreType.DMA((6 * n,)),
                        pltpu.SemaphoreType.DMA((n,))],
    )(*shards)


def _rs_pair_exchange(gs):
    n = len(gs)
    halves = [g.shape[1] // 2 for g in gs]

    def body(*refs):
        ins, outs = refs[:n], refs[n:2 * n]
        send_sems, recv_sems = refs[2 * n:]
        x, y, c = _mesh_pos()
        cps = []
        for k in range(n):
            cp = pltpu.make_async_remote_copy(
                src_ref=ins[k].at[:, pl.ds((1 - c) * halves[k], halves[k])], dst_ref=outs[k],
                send_sem=send_sems.at[k], recv_sem=recv_sems.at[k], device_id=(x, y, 1 - c), device_id_type=MESH)
            cp.start()
            cps.append(cp)
        for cp in cps:
            cp.wait()

    return pl.pallas_call(
        body, name="rs_pair_exchange",
        in_specs=[pl.BlockSpec(memory_space=pl.ANY)] * n,
        out_specs=[pl.BlockSpec(memory_space=pl.ANY)] * n,
        out_shape=[jax.ShapeDtypeStruct((N_CHIPS, h) + g.shape[2:], g.dtype) for g, h in zip(gs, halves)],
        scratch_shapes=[pltpu.SemaphoreType.DMA((n,)), pltpu.SemaphoreType.DMA((n,))],
    )(*gs)


def _rs_pair_share(fulls):
    n = len(fulls)
    halves = [f.shape[0] // 2 for f in fulls]

    def body(*refs):
        outs = refs[n:2 * n]
        send_sems, recv_sems = refs[2 * n:]
        x, y, c = _mesh_pos()

        def copy(k, cc, to):
            rows = outs[k].at[pl.ds(cc * halves[k], halves[k])]
            return pltpu.make_async_remote_copy(
                src_ref=rows, dst_ref=rows, send_sem=send_sems.at[k], recv_sem=recv_sems.at[k],
                device_id=to, device_id_type=MESH)

        cps = [copy(k, c, (x, y, 1 - c)) for k in range(n)]
        for cp in cps:
            cp.start()
        for k in range(n):
            copy(k, 1 - c, (x, y, c)).wait_recv()
        for cp in cps:
            cp.wait_send()

    return pl.pallas_call(
        body, name="rs_pair_share",
        in_specs=[pl.BlockSpec(memory_space=pl.ANY)] * n,
        out_specs=[pl.BlockSpec(memory_space=pl.ANY)] * n,
        out_shape=[jax.ShapeDtypeStruct(f.shape, f.dtype) for f in fulls],
        input_output_aliases={k: k for k in range(n)},
        scratch_shapes=[pltpu.SemaphoreType.DMA((n,)), pltpu.SemaphoreType.DMA((n,))],
    )(*fulls)


def _pair_sum(g, recv, pos, name):
    _, rows, cols = g.shape
    h = rows // 2
    tr = _row_tile(h, 256)
    nb = h // tr

    def body(pos_ref, g_ref, r_ref, o_ref, own_ref):
        total = g_ref[0] + r_ref[0]
        o_ref[0] = total.astype(BF16)

        @pl.when(pl.program_id(1) == pos_ref[1])
        def _():
            own_ref[...] = total

    return pl.pallas_call(
        body, name=name,
        grid_spec=pltpu.PrefetchScalarGridSpec(
            num_scalar_prefetch=1, grid=(nb, N_CHIPS),
            in_specs=[pl.BlockSpec((1, tr, cols), lambda r, s, pos_ref: (s, pos_ref[0] * nb + r, 0)),
                      pl.BlockSpec((1, tr, cols), lambda r, s, pos_ref: (s, r, 0))],
            out_specs=[pl.BlockSpec((1, tr, cols), lambda r, s, pos_ref: (s, r, 0)),
                       pl.BlockSpec((tr, cols), lambda r, s, pos_ref: (r, 0))]),
        out_shape=[jax.ShapeDtypeStruct((N_CHIPS, h, cols), BF16), jax.ShapeDtypeStruct((h, cols), F32)],
        compiler_params=_params(2, 32))(pos, g, recv)


def _chip_sum(own, rb, pos, name):
    h, cols = own.shape
    tr = _row_tile(h, 256)
    nb = h // tr

    def body(pos_ref, c_ref, r_ref, o_ref):
        del pos_ref
        o_ref[...] = ((c_ref[...] + r_ref[0].astype(F32)) + r_ref[1].astype(F32)) + r_ref[2].astype(F32)

    return pl.pallas_call(
        body, name=name,
        grid_spec=pltpu.PrefetchScalarGridSpec(
            num_scalar_prefetch=1, grid=(nb,),
            in_specs=[pl.BlockSpec((tr, cols), lambda r, pos_ref: (r, 0)),
                      pl.BlockSpec((3, tr, cols), lambda r, pos_ref: (0, r, 0))],
            out_specs=pl.BlockSpec((tr, cols), lambda r, pos_ref: (pos_ref[0] * nb + r, 0))),
        out_shape=jax.ShapeDtypeStruct((2 * h, cols), F32),
        compiler_params=_params(1, 32))(pos, own, rb)


def _adamw_big(w, g, m, v, name):
    rows, cols = w.shape
    tr = _row_tile(rows, 256)

    def body(w_ref, g_ref, m_ref, v_ref, d_ref, nm_ref, nv_ref):
        d_ref[...], nm_ref[...], nv_ref[...] = _adamw(w_ref[...], g_ref[...], m_ref[...], v_ref[...])

    spec = pl.BlockSpec((tr, cols), lambda i: (i, 0))
    return pl.pallas_call(
        body, name=name, grid=(rows // tr,),
        in_specs=[spec] * 4, out_specs=[spec] * 3,
        out_shape=[jax.ShapeDtypeStruct((rows, cols), F32)] * 3,
        compiler_params=_params(1, 48))(w, g, m, v)


def _gather_small(loss, dfg, dlg, dgn, ddwb, dclg, dclb, dpwb, ddww, dmeta):
    d = loss.shape[1]
    w = dgn.shape[1]

    def body(loss_ref, dfg_ref, dlg_ref, dgn_ref, ddwb_ref, dclg_ref, dclb_ref, dpwb_ref, ddww_ref, dmeta_ref,
             gs_ref, gd_ref, gm_ref, send_sems, recv_sems):
        x, y, c = _mesh_pos()
        me = 4 * x + 2 * y + c
        gs_ref[me, 0:1, :] = loss_ref[...]
        gs_ref[me, 1:2, :] = dfg_ref[...]
        gs_ref[me, 2:3, :] = dlg_ref[...]
        gs_ref[me, 3:4, 0:w] = dgn_ref[...]
        gs_ref[me, 3:4, w:2 * w] = ddwb_ref[...]
        gs_ref[me, 4:5, 0:w] = dclg_ref[...]
        gs_ref[me, 4:5, w:2 * w] = dclb_ref[...]
        gs_ref[me, 5:6, 0:w] = dpwb_ref[...]
        gs_ref[me, 5:6, w:2 * w] = jnp.zeros((1, d - w), F32)
        gs_ref[me, 6:8, :] = jnp.zeros((2, d), F32)
        gd_ref[me] = ddww_ref[...]
        gm_ref[me] = dmeta_ref[...]
        bufs = (gs_ref, gd_ref, gm_ref)

        def peer(j):
            return (1 - x if j & 4 else x), (1 - y if j & 2 else y), (1 - c if j & 1 else c)

        def copy(k, j, slot, to):
            return pltpu.make_async_remote_copy(
                src_ref=bufs[k].at[slot], dst_ref=bufs[k].at[slot],
                send_sem=send_sems.at[7 * k + j - 1], recv_sem=recv_sems.at[7 * k + j - 1],
                device_id=to, device_id_type=MESH)

        cps = []
        for k in range(3):
            for j in range(1, N_DEV):
                cp = copy(k, j, me, peer(j))
                cp.start()
                cps.append(cp)
        for k in range(3):
            for j in range(1, N_DEV):
                px, py, pc = peer(j)
                copy(k, j, 4 * px + 2 * py + pc, (x, y, c)).wait_recv()
        for cp in cps:
            cp.wait_send()

    vm = pl.BlockSpec(memory_space=pltpu.VMEM)
    return pl.pallas_call(
        body, name="gather_small",
        in_specs=[vm] * 10, out_specs=[vm] * 3,
        out_shape=[jax.ShapeDtypeStruct((N_DEV, 8, d), F32),
                   jax.ShapeDtypeStruct((N_DEV,) + ddww.shape, F32),
                   jax.ShapeDtypeStruct((N_DEV,) + dmeta.shape, F32)],
        scratch_shapes=[pltpu.SemaphoreType.DMA((21,)), pltpu.SemaphoreType.DMA((21,))],
    )(loss, dfg, dlg, dgn, ddwb, dclg, dclb, dpwb, ddww, dmeta)


def _small_update(s_arr, gs, gd, gm, weights, ms, vs):
    d = gs.shape[2]
    w = d // 2
    n = len(weights)

    def body(s_ref, gs_ref, gd_ref, gm_ref, *refs):
        del s_ref
        w_refs, m_refs, v_refs = refs[:n], refs[n:2 * n], refs[2 * n:3 * n]
        loss_ref = refs[3 * n]
        g_refs = refs[3 * n + 1:4 * n + 1]
        d_refs = refs[4 * n + 1:5 * n + 1]
        nm_refs = refs[5 * n + 1:6 * n + 1]
        nv_refs = refs[6 * n + 1:7 * n + 1]

        def total(ref):
            t = ref[0]
            for dev in range(1, N_DEV):
                t = t + ref[dev]
            return t

        packed = total(gs_ref)
        loss_ref[...] = jnp.sum(packed[0:1, :], axis=1, keepdims=True) * (0.5 / d)
        grads = [packed[2:3, :], packed[1:2, :], packed[3:4, 0:w], packed[3:4, w:2 * w], packed[4:5, 0:w],
                 packed[4:5, w:2 * w], packed[5:6, 0:w], total(gd_ref), total(gm_ref)]
        for k in range(n):
            g = grads[k]
            g_refs[k][...] = g
            d_refs[k][...], nm_refs[k][...], nv_refs[k][...] = _adamw(w_refs[k][...], g, m_refs[k][...], v_refs[k][...])

    def whole(shape):
        return pl.BlockSpec(shape, lambda i, s_ref: (0,) * len(shape))

    wc = weights[7].shape[1]
    mc = weights[8].shape[1]
    shapes = [a.shape for a in weights]
    in_specs = ([whole(gs.shape),
                 pl.BlockSpec((N_DEV, gd.shape[1], wc), lambda i, s_ref: (0, 0, s_ref[0])),
                 pl.BlockSpec((N_DEV, gm.shape[1], mc), lambda i, s_ref: (0, 0, s_ref[0]))]
                + [whole(s) for s in shapes] * 3)
    out_specs = [whole((1, 1))] + [whole(s) for s in shapes] * 4
    out_shape = [jax.ShapeDtypeStruct((1, 1), F32)] + [jax.ShapeDtypeStruct(s, F32) for s in shapes] * 4
    outs = pl.pallas_call(
        body, name="small_update",
        grid_spec=pltpu.PrefetchScalarGridSpec(num_scalar_prefetch=1, grid=(1,), in_specs=in_specs,
                                               out_specs=out_specs),
        out_shape=out_shape,
        compiler_params=_params(1, 32))(s_arr, gs, gd, gm, *weights, *ms, *vs)
    loss = outs[0]
    return loss, outs[1:n + 1], outs[n + 1:2 * n + 1], outs[2 * n + 1:3 * n + 1], outs[3 * n + 1:4 * n + 1]


def kernel(x, meta_tokens, ln_g, w_in, ret_gn_g, conv_dw_w, conv_dw_b, conv_ln_g, conv_ln_b, conv_pw_w, conv_pw_b, w_out, final_g, loss_target, m_meta_tokens, m_ln_g, m_w_in, m_ret_gn_g, m_conv_dw_w, m_conv_dw_b, m_conv_ln_g, m_conv_ln_b, m_conv_pw_w, m_conv_pw_b, m_w_out, m_final_g, v_meta_tokens, v_ln_g, v_w_in, v_ret_gn_g, v_conv_dw_w, v_conv_dw_b, v_conv_ln_g, v_conv_ln_b, v_conv_pw_w, v_conv_pw_b, v_w_out, v_final_g):
    seq, d = x.shape[1], x.shape[2]
    w = ret_gn_g.shape[1]
    hd = w // RET_HEADS
    lp = CHUNK + seq
    ns = w_in.shape[2]
    mx, my, mc = lax.axis_index("x"), lax.axis_index("y"), lax.axis_index("c")
    my_s = 2 * mx + my
    s_arr = my_s.astype(jnp.int32).reshape(1)
    pos = jnp.stack([mc, my_s]).astype(jnp.int32)
    order = jnp.stack([my_s, 2 * (1 - mx) + my, 2 * mx + (1 - my), 2 * (1 - mx) + (1 - my)]).astype(jnp.int32)

    dw_pad = jnp.pad(conv_dw_w[0], ((0, HALO - CONV_K), (0, 0)))
    dw4, meta4 = _gather_shards([dw_pad, meta_tokens])
    dw_full = dw4.transpose(1, 0, 2).reshape(HALO, w)
    meta_full = meta4.transpose(1, 0, 2).reshape(N_META, d)

    hp = jnp.concatenate([jnp.zeros((LEAD, d), F32), meta_full, x[0]], axis=0)
    tgtp = jnp.concatenate([jnp.zeros((CHUNK, d), F32), loss_target[0]], axis=0)
    consts = _ret_consts()
    cos, sin = _rope_tables(lp, hd // 2)
    fg2 = final_g.reshape(1, d)

    proj, r1, hn, w4, pw4, wo4 = _in_proj_gather(
        order, hp, ln_g, _cast_bf16(w_in[0], "cast_w_in"), _cast_bf16(conv_pw_w[0], "cast_pw_w"),
        _cast_bf16(w_out[0], "cast_w_out"))
    pw_full = pw4.reshape(w, w)
    wo_full = wo4.reshape(2 * w, d)
    y_ret, ssave = _ret_fwd(proj, cos, sin, ret_gn_g, consts)
    y_conv, u1 = _conv_fwd(proj, dw_full, conv_dw_b, conv_ln_g, conv_ln_b, pw_full, conv_pw_b)
    dh2, dy, loss_l, dfg = _out_proj_loss(y_ret, y_conv, hp, tgtp, wo_full, fg2)

    g_wo = _dw_out(y_ret, y_conv, dh2)
    dproj, dgn = _ret_bwd(proj, dy, ssave, cos, sin, ret_gn_g, consts)
    dproj, du1, g_pw, dpwb, dclg, dclb = _conv_bwd_pw(dy, proj, u1, conv_ln_g, conv_ln_b, pw_full, conv_pw_b, dproj)
    dproj, ddww, ddwb = _conv_bwd_dw(du1, proj, dw_full, dproj)
    g_win = _dw_in(hn, dproj, ns)

    gs = [g_win, g_wo.reshape(N_CHIPS, (2 * w) // N_CHIPS, d), g_pw.reshape(N_CHIPS, w // N_CHIPS, w)]
    recv = _rs_pair_exchange(gs)
    names = ("w_in", "w_out", "pw_w")
    sums = [_pair_sum(g, r, pos, "pair_sum_" + nm) for g, r, nm in zip(gs, recv, names)]
    dh, dlg, rb = _in_proj_bwd(dproj, w4, hp, r1, dh2, ln_g, [cs_ for cs_, _ in sums])
    grad_x = dh[CHUNK:][None]
    dmeta = dh[LEAD:CHUNK]
    fulls = [_chip_sum(own, r, pos, "chip_sum_" + nm) for (_, own), r, nm in zip(sums, rb, names)]
    grad_w_in, grad_w_out, grad_pw = _rs_pair_share(fulls)
    d_win, nm_win, nv_win = _adamw_big(w_in[0], grad_w_in, m_w_in[0], v_w_in[0], "adamw_w_in")
    d_wo, nm_wo, nv_wo = _adamw_big(w_out[0], grad_w_out, m_w_out[0], v_w_out[0], "adamw_w_out")
    d_pw, nm_pw, nv_pw = _adamw_big(conv_pw_w[0], grad_pw, m_conv_pw_w[0], v_conv_pw_w[0], "adamw_pw_w")

    gsm, gdm, gmm = _gather_small(loss_l, dfg, dlg, dgn, ddwb, dclg, dclb, dpwb, ddww, dmeta)

    def pad_dw(a):
        return jnp.pad(a[0], ((0, HALO - CONV_K), (0, 0)))

    small_w = [ln_g, fg2, ret_gn_g, conv_dw_b, conv_ln_g, conv_ln_b, conv_pw_b, dw_pad, meta_tokens]
    small_m = [m_ln_g, m_final_g.reshape(1, d), m_ret_gn_g, m_conv_dw_b, m_conv_ln_g, m_conv_ln_b, m_conv_pw_b,
               pad_dw(m_conv_dw_w), m_meta_tokens]
    small_v = [v_ln_g, v_final_g.reshape(1, d), v_ret_gn_g, v_conv_dw_b, v_conv_ln_g, v_conv_ln_b, v_conv_pw_b,
               pad_dw(v_conv_dw_w), v_meta_tokens]
    loss, sg, sd, snm, snv = _small_update(s_arr, gsm, gdm, gmm, small_w, small_m, small_v)

    def assemble(small, big_in, big_pw, big_out):
        ln, fg, gn, dwb, clg, clb, pwb, dww, meta = small
        return (meta, ln, big_in[None], gn, dww[:CONV_K][None], dwb, clg, clb, big_pw[None], pwb, big_out[None],
                fg.reshape(d))

    return (loss.reshape(()), grad_x,
            *assemble(sg, grad_w_in, grad_pw, grad_w_out),
            *assemble(sd, d_win, d_pw, d_wo),
            *assemble(snm, nm_win, nm_pw, nm_wo),
            *assemble(snv, nv_win, nv_pw, nv_wo))
```

```python
import functools

import jax
import jax.numpy as jnp
from jax import lax
from jax.experimental import pallas as pl
from jax.experimental.pallas import tpu as pltpu

F32 = jnp.float32
BF16 = jnp.bfloat16
MESH = pl.DeviceIdType.MESH

N_META = 16
CHUNK = 128
LEAD = (-N_META) % CHUNK
RET_HEADS = 4
CONV_K = 31
HALO = 32
ROPE_BASE = 10000.0
EPS = 1e-6
N_CHIPS = 4
N_DEV = 8

ADAM_LR = 0.001
ADAM_B1 = 0.9
ADAM_B2 = 0.999
ADAM_EPS = 1e-08
ADAM_WD = 0.01
ADAM_STEP = 10

MIB = 2 ** 20


def _params(n_grid_axes, vmem_mib):
    return pltpu.CompilerParams(dimension_semantics=("arbitrary",) * n_grid_axes,
                                vmem_limit_bytes=vmem_mib * MIB)


def _row_tile(n, pref):
    for t in (1664, 1280, 1024, 640, 512, 384, 320, 256, 128, 64, 32, 16, 8):
        if t <= pref and n % t == 0:
            return t
    raise ValueError(f"no row tile for {n}")


def _dot(a, b):
    return jnp.dot(a, b, preferred_element_type=F32)


def _dot_nt(a, b):
    return lax.dot_general(a, b, (((1,), (1,)), ((), ())), preferred_element_type=F32)


def _dot_tn(a, b):
    return lax.dot_general(a, b, (((0,), (0,)), ((), ())), preferred_element_type=F32)


def _sigmoid(x):
    return jax.nn.sigmoid(x)


def _dsilu(x, s):
    return s * (1.0 + x * (1.0 - s))


def _mean(x):
    return jnp.mean(x, axis=-1, keepdims=True)


def _colsum(x):
    return jnp.sum(x, axis=0, keepdims=True)


def _rope(x, cos, sin):
    half = x.shape[-1] // 2
    x1, x2 = x[:, :half], x[:, half:]
    return jnp.concatenate([x1 * cos - x2 * sin, x1 * sin + x2 * cos], axis=-1)


def _rope_t(d, cos, sin):
    half = d.shape[-1] // 2
    d1, d2 = d[:, :half], d[:, half:]
    return jnp.concatenate([d1 * cos + d2 * sin, d2 * cos - d1 * sin], axis=-1)


def _adamw(w, g, m, v):
    m = ADAM_B1 * m + (1.0 - ADAM_B1) * g
    v = ADAM_B2 * v + (1.0 - ADAM_B2) * (g * g)
    m_hat = m / (1.0 - ADAM_B1 ** ADAM_STEP)
    v_hat = v / (1.0 - ADAM_B2 ** ADAM_STEP)
    delta = -ADAM_LR * (m_hat / (jnp.sqrt(v_hat) + ADAM_EPS) + ADAM_WD * w)
    return delta, m, v


def _ret_consts():
    h = jnp.arange(RET_HEADS, dtype=F32)
    log_g = jnp.log(1.0 - jnp.exp2(-5.0 - h))
    idx = jnp.arange(CHUNK, dtype=F32)
    rel = idx[:, None] - idx[None, :]
    dmask = jnp.where(rel[None] >= 0, jnp.exp(jnp.maximum(rel, 0.0)[None] * log_g[:, None, None]), 0.0)
    qd = jnp.exp((idx[None, :] + 1.0) * log_g[:, None])[:, :, None]
    kd = jnp.exp((CHUNK - 1.0 - idx[None, :]) * log_g[:, None])[:, :, None]
    cd = jnp.exp(CHUNK * log_g)[:, None, None]
    return dmask, qd, kd, cd


def _rope_tables(n_rows, half):
    pos = jnp.arange(n_rows, dtype=F32) - float(LEAD)
    inv_freq = ROPE_BASE ** (-jnp.arange(half, dtype=F32) / half)
    ang = pos[:, None] * inv_freq[None, :]
    return jnp.cos(ang), jnp.sin(ang)


def _cast_into_gathered(a, s_arr, name):
    rows, cols = a.shape
    tr = _row_tile(rows, 256)

    def body(s_ref, a_ref, o_ref):
        del s_ref
        o_ref[0] = a_ref[...].astype(BF16)

    return pl.pallas_call(
        body, name=name,
        grid_spec=pltpu.PrefetchScalarGridSpec(
            num_scalar_prefetch=1, grid=(rows // tr,),
            in_specs=[pl.BlockSpec((tr, cols), lambda i, s_ref: (i, 0))],
            out_specs=pl.BlockSpec((1, tr, cols), lambda i, s_ref: (s_ref[0], i, 0))),
        out_shape=jax.ShapeDtypeStruct((N_CHIPS, rows, cols), BF16),
        compiler_params=_params(1, 32))(s_arr, a)


def _in_proj_gather(order, hp, ln_g, w4, pw4, wo4):
    lp, d = hp.shape
    ns = w4.shape[2]
    tm = _row_tile(lp, 640)
    nt = lp // tm
    assert nt >= 2, "the hn write-back of a row tile is waited for one step later, before any pass re-reads it"
    gathered = (w4, pw4, wo4)
    halves = [a.shape[1] // 2 for a in gathered]
    n = len(gathered)

    def body(order_ref, h_ref, g_ref, w_in, pw_in, wo_in, proj_ref, r_ref, hn_hbm, w_out, pw_out, wo_out,
             wbuf, hnbuf, send_sems, recv_sems, hn_out_sems, hn_in_sems, w_sem):
        del order_ref, w_in, pw_in, wo_in
        t, i = pl.program_id(0), pl.program_id(1)
        slot = (t * nt + i) % 2
        x, y, c = _mesh_pos()
        me, sibling = (x, y, c), (x, y, 1 - c)
        my_s = 2 * x + y
        chips = _other_chips(x, y)
        outs = (w_out, pw_out, wo_out)

        def half(k, s, cc):
            return outs[k].at[s, pl.ds(cc * halves[k], halves[k])]

        def rcopy(k, j, rows, to):
            return pltpu.make_async_remote_copy(
                src_ref=rows, dst_ref=rows, send_sem=send_sems.at[6 * k + j], recv_sem=recv_sems.at[6 * k + j],
                device_id=to, device_id_type=MESH)

        def send(k, j):
            return rcopy(k, j, half(k, my_s, c), (*chips[j], c))

        def shard_of(j):
            return 2 * chips[j][0] + chips[j][1]

        def forward(k, j):
            return rcopy(k, 3 + j, half(k, shard_of(j), c), sibling)

        def land(k, j):
            rcopy(k, j, half(k, shard_of(j), c), me).wait_recv()
            forward(k, j).start()

        def landed_from_sibling(k, j):
            rcopy(k, 3 + j, half(k, shard_of(j), 1 - c), me).wait_recv()

        def load_w(s):
            cp = pltpu.make_async_copy(w_out.at[s], wbuf, w_sem)
            cp.start()
            cp.wait()

        def hn_out(sl, row_tile):
            return pltpu.make_async_copy(hnbuf.at[sl], hn_hbm.at[pl.ds(row_tile * tm, tm)], hn_out_sems.at[sl])

        def hn_in(sl, row_tile):
            return pltpu.make_async_copy(hn_hbm.at[pl.ds(row_tile * tm, tm)], hnbuf.at[sl], hn_in_sems.at[sl])

        @pl.when((t == 0) & (i == 0))
        def _():
            for j in range(2):
                send(0, j).start()
            load_w(my_s)

        for j in range(3):
            @pl.when((t == j + 1) & (i == 0))
            def _(j=j):
                if j == 0:
                    send(0, 2).start()
                    for k in range(1, n):
                        for jj in range(3):
                            send(k, jj).start()
                land(0, j)
                landed_from_sibling(0, j)
                load_w(shard_of(j))

        @pl.when(t == 0)
        def _():
            h = h_ref[...]
            r = lax.rsqrt(_mean(h * h) + EPS)
            hnbuf[slot] = ((h * r) * g_ref[...]).astype(BF16)
            r_ref[...] = r
            hn_out(slot, i).start()

        @pl.when(t > 0)
        def _():
            hn_in(slot, i).wait()

        proj_ref[...] = _dot(hnbuf[slot], wbuf[...])

        @pl.when(((t == 0) & (i > 0)) | ((t == 1) & (i == 0)))
        def _():
            hn_out(1 - slot, jnp.where(i > 0, i - 1, nt - 1)).wait()

        last = (t == N_CHIPS - 1) & (i == nt - 1)

        @pl.when(((t > 0) | (i == nt - 1)) & jnp.logical_not(last))
        def _():
            hn_in(1 - slot, jnp.where(i == nt - 1, 0, i + 1)).start()

        @pl.when(last)
        def _():
            for k in range(1, n):
                for j in range(3):
                    land(k, j)
            for k in range(1, n):
                for j in range(3):
                    landed_from_sibling(k, j)
            for k in range(n):
                for j in range(3):
                    send(k, j).wait_send()
                    forward(k, j).wait_send()

    def frozen(t, i):
        return jnp.where(t == 0, i, nt - 1)

    any_spec = pl.BlockSpec(memory_space=pl.ANY)
    return pl.pallas_call(
        body, name="in_proj_gather",
        grid_spec=pltpu.PrefetchScalarGridSpec(
            num_scalar_prefetch=1, grid=(N_CHIPS, nt),
            in_specs=[pl.BlockSpec((tm, d), lambda t, i, o: (frozen(t, i), 0)),
                      pl.BlockSpec((1, d), lambda t, i, o: (0, 0)),
                      any_spec, any_spec, any_spec],
            out_specs=[pl.BlockSpec((tm, ns), lambda t, i, o: (i, o[t])),
                       pl.BlockSpec((tm, 1), lambda t, i, o: (frozen(t, i), 0)),
                       any_spec, any_spec, any_spec, any_spec],
            scratch_shapes=[pltpu.VMEM((d, ns), BF16), pltpu.VMEM((2, tm, d), BF16),
                            pltpu.SemaphoreType.DMA((6 * n,)), pltpu.SemaphoreType.DMA((6 * n,)),
                            pltpu.SemaphoreType.DMA((2,)), pltpu.SemaphoreType.DMA((2,)),
                            pltpu.SemaphoreType.DMA]),
        out_shape=[jax.ShapeDtypeStruct((lp, N_CHIPS * ns), F32),
                   jax.ShapeDtypeStruct((lp, 1), F32),
                   jax.ShapeDtypeStruct((lp, d), BF16)]
                  + [jax.ShapeDtypeStruct(a.shape, a.dtype) for a in gathered],
        input_output_aliases={3: 3, 4: 4, 5: 5},
        compiler_params=_params(2, 48))(order, hp, ln_g, w4, pw4, wo4)


def _ret_fwd(proj, cos, sin, gn_g, consts):
    lp = proj.shape[0]
    w = gn_g.shape[1]
    hd = w // RET_HEADS
    nch = lp // CHUNK
    dmask, qd, kd, cd = consts

    def body(q_ref, k_ref, v_ref, g_ref, cos_ref, sin_ref, gn_ref, m_ref, qd_ref, kd_ref, cd_ref,
             y_ref, ssave_ref, s_scr):
        @pl.when(pl.program_id(0) == 0)
        def _():
            s_scr[...] = jnp.zeros_like(s_scr)

        cos_t, sin_t = cos_ref[...], sin_ref[...]
        for h in range(RET_HEADS):
            sl = slice(h * hd, (h + 1) * hd)
            qr = _rope(q_ref[:, sl], cos_t, sin_t)
            kr = _rope(k_ref[:, sl], cos_t, sin_t) * (hd ** -0.5)
            vb = v_ref[:, sl].astype(BF16)
            sc = _dot_nt(qr.astype(BF16), kr.astype(BF16)) * m_ref[h]
            state = s_scr[h]
            sb = state.astype(BF16)
            ssave_ref[0, h] = sb
            out = _dot(sc.astype(BF16), vb) + _dot((qr * qd_ref[h]).astype(BF16), sb)
            s_scr[h] = state * cd_ref[h] + _dot_tn((kr * kd_ref[h]).astype(BF16), vb)
            dev = out - _mean(out)
            yn = dev * lax.rsqrt(_mean(dev * dev) + EPS)
            g = g_ref[:, sl]
            y_ref[:, sl] = ((yn * gn_ref[:, sl]) * (g * _sigmoid(g))).astype(BF16)

    def col(j):
        return pl.BlockSpec((CHUNK, w), lambda i: (i, j))

    def whole(a):
        return pl.BlockSpec(a.shape, lambda i: (0,) * a.ndim)

    return pl.pallas_call(
        body, name="ret_fwd", grid=(nch,),
        in_specs=[col(0), col(1), col(2), col(3),
                  pl.BlockSpec((CHUNK, hd // 2), lambda i: (i, 0)),
                  pl.BlockSpec((CHUNK, hd // 2), lambda i: (i, 0)),
                  whole(gn_g), whole(dmask), whole(qd), whole(kd), whole(cd)],
        out_specs=[pl.BlockSpec((CHUNK, w), lambda i: (i, 0)),
                   pl.BlockSpec((1, RET_HEADS, hd, hd), lambda i: (i, 0, 0, 0))],
        out_shape=[jax.ShapeDtypeStruct((lp, w), BF16),
                   jax.ShapeDtypeStruct((nch, RET_HEADS, hd, hd), BF16)],
        scratch_shapes=[pltpu.VMEM((RET_HEADS, hd, hd), F32)],
        compiler_params=_params(1, 32))(proj, proj, proj, proj, cos, sin, gn_g, dmask, qd, kd, cd)


def _tap_groups(start, flip):
    groups = {}
    for j in range(CONV_K):
        o = start + (CONV_K - 1 - j if flip else j)
        groups.setdefault(o % 8, []).append((o // 8, j))
    return groups


def _shift_up(win, s):
    return win if s == 0 else pltpu.roll(win, win.shape[0] - s, axis=0)


def _dw_taps(src_ref, w_ref, dst_ref, bias, *, rows, start, flip, rb):
    cw = dst_ref.shape[1]
    lb = min(128, cw)
    groups = _tap_groups(start, flip)

    def rb_body(r, carry):
        base = pl.multiple_of(r * rb, rb)
        for cb in range(cw // lb):
            ls = slice(cb * lb, (cb + 1) * lb)
            win = src_ref[pl.ds(base, rb + HALO), ls]
            acc = jnp.zeros((rb, lb), F32) if bias is None else jnp.broadcast_to(bias[:, ls], (rb, lb))
            for s, taps in groups.items():
                ws = _shift_up(win, s)
                for a, j in taps:
                    acc = acc + ws[8 * a:8 * a + rb, :] * w_ref[j:j + 1, ls]
            dst_ref[pl.ds(base, rb), ls] = acc
        return carry

    lax.fori_loop(0, rows // rb, rb_body, 0)


def _conv_fwd(proj, dw_w, dw_b, cln_g, cln_b, pw_w, pw_b):
    lp = proj.shape[0]
    cw = dw_b.shape[1]
    tm = _row_tile(lp, 640)
    rb = _row_tile(tm, 64)

    def body(a_ref, b_ref, gc_ref, w_ref, wb_ref, lg_ref, lb_ref, pw_ref, pb_ref, y_ref, u1_ref, buf):
        @pl.when(pl.program_id(0) == 0)
        def _():
            buf[0:HALO, :] = jnp.zeros((HALO, cw), F32)

        buf[HALO:HALO + tm, :] = a_ref[...] * _sigmoid(b_ref[...])
        _dw_taps(buf, w_ref, u1_ref, wb_ref[...], rows=tm, start=HALO - (CONV_K - 1), flip=False, rb=rb)
        buf[0:HALO, :] = buf[tm:tm + HALO, :]
        u1 = u1_ref[...]
        dev = u1 - _mean(u1)
        z = dev * lax.rsqrt(_mean(dev * dev) + EPS) * lg_ref[...] + lb_ref[...]
        u3 = (z * _sigmoid(z)).astype(BF16)
        u4 = _dot(u3, pw_ref[...]) + pb_ref[...]
        gc = gc_ref[...]
        y_ref[...] = (u4 * (gc * _sigmoid(gc))).astype(BF16)

    def col(j):
        return pl.BlockSpec((tm, cw), lambda i: (i, j))

    def whole(a):
        return pl.BlockSpec(a.shape, lambda i: (0,) * a.ndim)

    return pl.pallas_call(
        body, name="conv_fwd", grid=(lp // tm,),
        in_specs=[col(4), col(5), col(6), whole(dw_w), whole(dw_b), whole(cln_g), whole(cln_b),
                  whole(pw_w), whole(pw_b)],
        out_specs=[pl.BlockSpec((tm, cw), lambda i: (i, 0)), pl.BlockSpec((tm, cw), lambda i: (i, 0))],
        out_shape=[jax.ShapeDtypeStruct((lp, cw), BF16), jax.ShapeDtypeStruct((lp, cw), F32)],
        scratch_shapes=[pltpu.VMEM((tm + HALO, cw), F32)],
        compiler_params=_params(1, 48))(proj, proj, proj, dw_w, dw_b, cln_g, cln_b, pw_w, pw_b)


def _out_proj_loss(yr, yc, hp, tgt, w_out, final_g):
    lp, d = hp.shape
    w = yr.shape[1]
    tm = _row_tile(lp, 320)
    nt = lp // tm
    assert tm > CHUNK and nt >= 2

    def body(yr_ref, yc_ref, h_ref, t_hbm, w_ref, fg_ref, dh2_ref, dy_ref, loss_ref, dfg_ref, tbuf, tsems):
        i = pl.program_id(0)
        slot = i % 2

        def first_fetch():
            return pltpu.make_async_copy(t_hbm.at[pl.ds(0, tm - CHUNK)], tbuf.at[0, pl.ds(CHUNK, tm - CHUNK)],
                                         tsems.at[0])

        def fetch(sl, tile):
            return pltpu.make_async_copy(t_hbm.at[pl.ds(tile * tm - CHUNK, tm)], tbuf.at[sl], tsems.at[sl])

        @pl.when(i == 0)
        def _():
            loss_ref[...] = jnp.zeros_like(loss_ref)
            dfg_ref[...] = jnp.zeros_like(dfg_ref)
            tbuf[0, 0:CHUNK, :] = jnp.zeros((CHUNK, d), F32)
            first_fetch().start()

        @pl.when(i + 1 < nt)
        def _():
            fetch(1 - slot, i + 1).start()

        @pl.when(i == 0)
        def _():
            first_fetch().wait()

        @pl.when(i > 0)
        def _():
            fetch(slot, i).wait()

        h2 = h_ref[...] + (_dot(yr_ref[...], w_ref[0:w, :]) + _dot(yc_ref[...], w_ref[w:2 * w, :]))
        r2 = lax.rsqrt(_mean(h2 * h2) + EPS)
        h2n = h2 * r2
        fg = fg_ref[...]
        rows = i * tm + lax.broadcasted_iota(jnp.int32, (tm, 1), 0)
        err = jnp.where(rows >= CHUNK, h2n * fg - tbuf[slot], 0.0)
        loss_ref[...] += _colsum(err * err)
        dout = err * (1.0 / d)
        dfg_ref[...] += _colsum(dout * h2n)
        dz = dout * fg
        dh2 = r2 * (dz - h2n * _mean(dz * h2n))
        dh2_ref[...] = dh2
        db = dh2.astype(BF16)
        dy_ref[:, 0:w] = _dot_nt(db, w_ref[0:w, :])
        dy_ref[:, w:2 * w] = _dot_nt(db, w_ref[w:2 * w, :])

    def row(cols):
        return pl.BlockSpec((tm, cols), lambda i: (i, 0))

    return pl.pallas_call(
        body, name="out_proj_loss", grid=(nt,),
        in_specs=[row(w), row(w), row(d), pl.BlockSpec(memory_space=pl.ANY),
                  pl.BlockSpec(memory_space=pltpu.VMEM),
                  pl.BlockSpec((1, d), lambda i: (0, 0))],
        out_specs=[row(d), row(2 * w), pl.BlockSpec((1, d), lambda i: (0, 0)),
                   pl.BlockSpec((1, d), lambda i: (0, 0))],
        out_shape=[jax.ShapeDtypeStruct((lp, d), F32), jax.ShapeDtypeStruct((lp, 2 * w), F32),
                   jax.ShapeDtypeStruct((1, d), F32), jax.ShapeDtypeStruct((1, d), F32)],
        scratch_shapes=[pltpu.VMEM((2, tm, d), F32), pltpu.SemaphoreType.DMA((2,))],
        compiler_params=_params(1, 56))(yr, yc, hp, tgt, w_out, final_g)


def _dw_out(yr, yc, dh2):
    lp, d = dh2.shape
    w = yr.shape[1]
    tm = _row_tile(lp, 1664)
    nb = 2
    dn = d // nb

    def body(yr_ref, yc_ref, d_ref, o_ref):
        @pl.when(pl.program_id(1) == 0)
        def _():
            o_ref[...] = jnp.zeros_like(o_ref)

        db = d_ref[...].astype(BF16)
        o_ref[0:w, :] += _dot_tn(yr_ref[...], db)
        o_ref[w:2 * w, :] += _dot_tn(yc_ref[...], db)

    return pl.pallas_call(
        body, name="dw_out", grid=(nb, lp // tm),
        in_specs=[pl.BlockSpec((tm, w), lambda n, i: (i, 0)),
                  pl.BlockSpec((tm, w), lambda n, i: (i, 0)),
                  pl.BlockSpec((tm, dn), lambda n, i: (i, n))],
        out_specs=pl.BlockSpec((2 * w, dn), lambda n, i: (0, n)),
        out_shape=jax.ShapeDtypeStruct((2 * w, d), F32),
        compiler_params=_params(2, 52))(yr, yc, dh2)


def _ret_bwd(proj, dy, ssave, cos, sin, gn_g, consts):
    lp = proj.shape[0]
    w = gn_g.shape[1]
    hd = w // RET_HEADS
    nch = lp // CHUNK
    dmask, qd, kd, cd = consts

    def body(q_ref, k_ref, v_ref, g_ref, dy_ref, ss_ref, cos_ref, sin_ref, gn_ref, m_ref, qd_ref, kd_ref,
             cd_ref, dp_ref, dgn_ref, ds_scr):
        @pl.when(pl.program_id(0) == 0)
        def _():
            ds_scr[...] = jnp.zeros_like(ds_scr)
            dgn_ref[...] = jnp.zeros_like(dgn_ref)

        cos_t, sin_t = cos_ref[...], sin_ref[...]
        for h in range(RET_HEADS):
            sl = slice(h * hd, (h + 1) * hd)
            qr = _rope(q_ref[:, sl], cos_t, sin_t)
            kr = _rope(k_ref[:, sl], cos_t, sin_t) * (hd ** -0.5)
            qb, kb = qr.astype(BF16), kr.astype(BF16)
            vb = v_ref[:, sl].astype(BF16)
            sb = ss_ref[0, h]
            mask = m_ref[h]
            qdec, kdec = qd_ref[h], kd_ref[h]
            scb = (_dot_nt(qb, kb) * mask).astype(BF16)
            qdq = (qr * qdec).astype(BF16)
            kdk = (kr * kdec).astype(BF16)
            out = _dot(scb, vb) + _dot(qdq, sb)
            dev = out - _mean(out)
            rstd = lax.rsqrt(_mean(dev * dev) + EPS)
            yn = dev * rstd
            g = g_ref[:, sl]
            sg = _sigmoid(g)
            gng = gn_ref[:, sl]
            dyv = dy_ref[:, sl]
            dgr = dyv * (yn * gng) * _dsilu(g, sg)
            silu_g = g * sg
            dgn_ref[:, sl] += _colsum(dyv * yn * silu_g)
            dyn = dyv * gng * silu_g
            dout = rstd * (dyn - _mean(dyn) - yn * _mean(dyn * yn))
            dob = dout.astype(BF16)
            dscb = (_dot_nt(dob, vb) * mask).astype(BF16)
            dstate = ds_scr[h]
            dsb = dstate.astype(BF16)
            dq = _dot(dscb, kb) + _dot_nt(dob, sb) * qdec
            dk = _dot_tn(dscb, qb) + _dot_nt(vb, dsb) * kdec
            dv = _dot_tn(scb, dob) + _dot(kdk, dsb)
            ds_scr[h] = dstate * cd_ref[h] + _dot_tn(qdq, dob)
            dp_ref[:, 0 * w + h * hd:0 * w + (h + 1) * hd] = _rope_t(dq, cos_t, sin_t).astype(BF16)
            dp_ref[:, 1 * w + h * hd:1 * w + (h + 1) * hd] = (_rope_t(dk, cos_t, sin_t) * (hd ** -0.5)).astype(BF16)
            dp_ref[:, 2 * w + h * hd:2 * w + (h + 1) * hd] = dv.astype(BF16)
            dp_ref[:, 3 * w + h * hd:3 * w + (h + 1) * hd] = dgr.astype(BF16)

    def rev(i):
        return nch - 1 - i

    def col(j):
        return pl.BlockSpec((CHUNK, w), lambda i: (rev(i), j))

    def whole(a):
        return pl.BlockSpec(a.shape, lambda i: (0,) * a.ndim)

    return pl.pallas_call(
        body, name="ret_bwd", grid=(nch,),
        in_specs=[col(0), col(1), col(2), col(3),
                  pl.BlockSpec((CHUNK, w), lambda i: (rev(i), 0)),
                  pl.BlockSpec((1, RET_HEADS, hd, hd), lambda i: (rev(i), 0, 0, 0)),
                  pl.BlockSpec((CHUNK, hd // 2), lambda i: (rev(i), 0)),
                  pl.BlockSpec((CHUNK, hd // 2), lambda i: (rev(i), 0)),
                  whole(gn_g), whole(dmask), whole(qd), whole(kd), whole(cd)],
        out_specs=[pl.BlockSpec((CHUNK, 4 * w), lambda i: (rev(i), 0)),
                   pl.BlockSpec((1, w), lambda i: (0, 0))],
        out_shape=[jax.ShapeDtypeStruct((lp, 7 * w), BF16), jax.ShapeDtypeStruct((1, w), F32)],
        scratch_shapes=[pltpu.VMEM((RET_HEADS, hd, hd), F32)],
        compiler_params=_params(1, 32))(proj, proj, proj, proj, dy, ssave, cos, sin, gn_g, dmask, qd, kd, cd)


def _conv_bwd_pw(dy, proj, u1, cln_g, cln_b, pw_w, pw_b, dproj):
    lp, cw = u1.shape
    tm = _row_tile(lp, 320)

    def body(dy_ref, gc_ref, u1_ref, lg_ref, lb_ref, pw_ref, pb_ref, dp_in, dp_ref, du1_ref, dpw_ref,
             dpb_ref, dlg_ref, dlb_ref):
        del dp_in

        @pl.when(pl.program_id(0) == 0)
        def _():
            dpw_ref[...] = jnp.zeros_like(dpw_ref)
            dpb_ref[...] = jnp.zeros_like(dpb_ref)
            dlg_ref[...] = jnp.zeros_like(dlg_ref)
            dlb_ref[...] = jnp.zeros_like(dlb_ref)

        u1 = u1_ref[...]
        dev = u1 - _mean(u1)
        rstd = lax.rsqrt(_mean(dev * dev) + EPS)
        u1n = dev * rstd
        lg = lg_ref[...]
        z = u1n * lg + lb_ref[...]
        sz = _sigmoid(z)
        u3b = (z * sz).astype(BF16)
        u4 = _dot(u3b, pw_ref[...]) + pb_ref[...]
        gc = gc_ref[...]
        sgc = _sigmoid(gc)
        dyc = dy_ref[...]
        du4 = dyc * (gc * sgc)
        dp_ref[...] = (dyc * u4 * _dsilu(gc, sgc)).astype(BF16)
        du4b = du4.astype(BF16)
        dpb_ref[...] += _colsum(du4)
        dpw_ref[...] += _dot_tn(u3b, du4b)
        dz = _dot_nt(du4b, pw_ref[...]) * _dsilu(z, sz)
        dlg_ref[...] += _colsum(dz * u1n)
        dlb_ref[...] += _colsum(dz)
        dn = dz * lg
        du1_ref[...] = rstd * (dn - _mean(dn) - u1n * _mean(dn * u1n))

    def row(j):
        return pl.BlockSpec((tm, cw), lambda i: (i, j))

    def whole(a):
        return pl.BlockSpec(a.shape, lambda i: (0,) * a.ndim)

    def acc(r):
        return pl.BlockSpec((r, cw), lambda i: (0, 0))

    return pl.pallas_call(
        body, name="conv_bwd_pw", grid=(lp // tm,),
        in_specs=[row(1), row(6), row(0), whole(cln_g), whole(cln_b), whole(pw_w), whole(pw_b),
                  pl.BlockSpec(memory_space=pl.ANY)],
        out_specs=[row(6), row(0), acc(cw), acc(1), acc(1), acc(1)],
        out_shape=[jax.ShapeDtypeStruct(dproj.shape, dproj.dtype), jax.ShapeDtypeStruct((lp, cw), F32),
                   jax.ShapeDtypeStruct((cw, cw), F32), jax.ShapeDtypeStruct((1, cw), F32),
                   jax.ShapeDtypeStruct((1, cw), F32), jax.ShapeDtypeStruct((1, cw), F32)],
        input_output_aliases={7: 0},
        compiler_params=_params(1, 48))(dy, proj, u1, cln_g, cln_b, pw_w, pw_b, dproj)


def _conv_bwd_dw(du1, proj, dw_w, dproj):
    lp, cw = du1.shape
    tm = _row_tile(lp, 640)
    rb = _row_tile(tm, 64)
    nt = lp // tm
    hb = tm // HALO

    def body(a_ref, b_ref, du_ref, nx_ref, w_ref, dp_in, dp_ref, dww_ref, dwb_ref, ubuf, dbuf, du0, acc):
        del dp_in
        i = pl.program_id(0)

        @pl.when(i == 0)
        def _():
            ubuf[0:HALO, :] = jnp.zeros((HALO, cw), F32)
            acc[...] = jnp.zeros_like(acc)
            dwb_ref[...] = jnp.zeros_like(dwb_ref)

        a = a_ref[...]
        sb = _sigmoid(b_ref[...])
        ubuf[HALO:HALO + tm, :] = a * sb
        du = du_ref[...]
        dbuf[0:tm, :] = du
        dbuf[tm:tm + HALO, :] = jnp.where(i == nt - 1, 0.0, nx_ref[...])
        dwb_ref[...] += _colsum(du)
        _dw_taps(dbuf, w_ref, du0, None, rows=tm, start=0, flip=True, rb=rb)
        d0 = du0[...]
        dp_ref[:, 0:cw] = (d0 * sb).astype(BF16)
        dp_ref[:, cw:2 * cw] = (d0 * a * sb * (1.0 - sb)).astype(BF16)

        lb = min(128, cw)
        groups = _tap_groups(HALO - (CONV_K - 1), False)

        def rb_body(r, carry):
            base = pl.multiple_of(r * rb, rb)
            for cb in range(cw // lb):
                ls = slice(cb * lb, (cb + 1) * lb)
                win = ubuf[pl.ds(base, rb + HALO), ls]
                dv = dbuf[pl.ds(base, rb), ls]
                for s, taps in groups.items():
                    ws = _shift_up(win, s)
                    for a, j in taps:
                        prod = dv * ws[8 * a:8 * a + rb, :]
                        acc[8 * j:8 * j + 8, ls] += jnp.sum(prod.reshape(rb // 8, 8, lb), axis=0)
            return carry

        lax.fori_loop(0, tm // rb, rb_body, 0)
        ubuf[0:HALO, :] = ubuf[tm:tm + HALO, :]

        @pl.when(i == nt - 1)
        def _():
            for j in range(CONV_K):
                dww_ref[j:j + 1, :] = _colsum(acc[8 * j:8 * j + 8, :])
            dww_ref[CONV_K:HALO, :] = jnp.zeros((HALO - CONV_K, cw), F32)

    def col(j):
        return pl.BlockSpec((tm, cw), lambda i: (i, j))

    return pl.pallas_call(
        body, name="conv_bwd_dw", grid=(nt,),
        in_specs=[col(4), col(5), col(0),
                  pl.BlockSpec((HALO, cw), lambda i: (jnp.minimum((i + 1) * hb, nt * hb - 1), 0)),
                  pl.BlockSpec(dw_w.shape, lambda i: (0, 0)),
                  pl.BlockSpec(memory_space=pl.ANY)],
        out_specs=[pl.BlockSpec((tm, 2 * cw), lambda i: (i, 2)),
                   pl.BlockSpec((HALO, cw), lambda i: (0, 0)),
                   pl.BlockSpec((1, cw), lambda i: (0, 0))],
        out_shape=[jax.ShapeDtypeStruct(dproj.shape, dproj.dtype), jax.ShapeDtypeStruct((HALO, cw), F32),
                   jax.ShapeDtypeStruct((1, cw), F32)],
        scratch_shapes=[pltpu.VMEM((tm + HALO, cw), F32), pltpu.VMEM((tm + HALO, cw), F32),
                        pltpu.VMEM((tm, cw), F32), pltpu.VMEM((8 * HALO, cw), F32)],
        input_output_aliases={5: 0},
        compiler_params=_params(1, 56))(proj, proj, du1, du1, dw_w, dproj)


def _dw_in(hn, dproj, ns):
    lp, d = hn.shape
    tm = _row_tile(lp, 1664)
    nt = lp // tm

    mb = 512 if d % 512 == 0 else d

    def body(hn_ref, dp_ref, o_hbm, acc, sem):
        s, i = pl.program_id(0), pl.program_id(1)

        @pl.when(i == 0)
        def _():
            acc[...] = jnp.zeros_like(acc)

        for m in range(d // mb):
            rows = slice(m * mb, (m + 1) * mb)
            acc[rows, :] += _dot_tn(hn_ref[:, rows], dp_ref[...])

        @pl.when(i == nt - 1)
        def _():
            cp = pltpu.make_async_copy(acc, o_hbm.at[s], sem)
            cp.start()
            cp.wait()

    return pl.pallas_call(
        body, name="dw_in", grid=(N_CHIPS, nt),
        in_specs=[pl.BlockSpec((tm, d), lambda s, i: (i, 0)),
                  pl.BlockSpec((tm, ns), lambda s, i: (i, s))],
        out_specs=pl.BlockSpec(memory_space=pl.ANY),
        out_shape=jax.ShapeDtypeStruct((N_CHIPS, d, ns), F32),
        scratch_shapes=[pltpu.VMEM((d, ns), F32), pltpu.SemaphoreType.DMA],
        compiler_params=_params(2, 56))(hn, dproj)


def _in_proj_bwd(dproj, w4, hp, r1, dh2, ln_g, cs):
    lp, d = hp.shape
    ns = w4.shape[2]
    tm = _row_tile(lp, 320)
    nt = lp // tm
    n = len(cs)

    def body(dp_ref, w_ref, h_ref, r_ref, d2_ref, g_ref, *refs):
        cs_refs, (dh_ref, dlg_ref), rb_refs = refs[:n], refs[n:n + 2], refs[n + 2:2 * n + 2]
        send_sems, recv_sems = refs[2 * n + 2:]
        i = pl.program_id(0)
        x, y, c = _mesh_pos()

        def exchange():
            return [pltpu.make_async_remote_copy(
                src_ref=cs_refs[k].at[2 * chip[0] + chip[1]], dst_ref=rb_refs[k].at[j],
                send_sem=send_sems.at[3 * k + j], recv_sem=recv_sems.at[3 * k + j],
                device_id=(*chip, c), device_id_type=MESH)
                for k in range(n) for j, chip in enumerate(_other_chips(x, y))]

        @pl.when(i == 0)
        def _():
            dlg_ref[...] = jnp.zeros_like(dlg_ref)
            for cp in exchange():
                cp.start()

        dhn = _dot_nt(dp_ref[:, 0:ns], w_ref[0])
        for s in range(1, N_CHIPS):
            dhn = dhn + _dot_nt(dp_ref[:, s * ns:(s + 1) * ns], w_ref[s])
        r = r_ref[...]
        hn0 = h_ref[...] * r
        dlg_ref[...] += _colsum(dhn * hn0)
        t = dhn * g_ref[...]
        dh_ref[...] = d2_ref[...] + r * (t - hn0 * _mean(t * hn0))

        @pl.when(i == nt - 1)
        def _():
            for cp in exchange():
                cp.wait()

    def row(cols):
        return pl.BlockSpec((tm, cols), lambda i: (i, 0))

    any_spec = pl.BlockSpec(memory_space=pl.ANY)
    outs = pl.pallas_call(
        body, name="in_proj_bwd", grid=(nt,),
        in_specs=[row(N_CHIPS * ns), pl.BlockSpec(memory_space=pltpu.VMEM),
                  row(d), row(1), row(d), pl.BlockSpec((1, d), lambda i: (0, 0))] + [any_spec] * n,
        out_specs=[row(d), pl.BlockSpec((1, d), lambda i: (0, 0))] + [any_spec] * n,
        out_shape=[jax.ShapeDtypeStruct((lp, d), F32), jax.ShapeDtypeStruct((1, d), F32)]
                  + [jax.ShapeDtypeStruct((3,) + a.shape[1:], a.dtype) for a in cs],
        scratch_shapes=[pltpu.SemaphoreType.DMA((3 * n,)), pltpu.SemaphoreType.DMA((3 * n,))],
        compiler_params=_params(1, 58))(dproj, w4, hp, r1, dh2, ln_g, *cs)
    return outs[0], outs[1], outs[2:]


def _mesh_pos():
    return lax.axis_index("x"), lax.axis_index("y"), lax.axis_index("c")


def _other_chips(x, y):
    return [(1 - x, y), (x, 1 - y), (1 - x, 1 - y)]


def _gather_shards(shards):
    n = len(shards)
    halves = [a.shape[0] // 2 for a in shards]

    def body(*refs):
        ins, outs = refs[:n], refs[n:2 * n]
        send_sems, recv_sems, loc_sems = refs[2 * n:]
        x, y, c = _mesh_pos()
        me, sibling = (x, y, c), (x, y, 1 - c)
        my_s = 2 * x + y
        chips = _other_chips(x, y)

        def half(k, s, cc):
            return outs[k].at[s, pl.ds(cc * halves[k], halves[k])]

        def rcopy(k, j, src, dst, to):
            return pltpu.make_async_remote_copy(
                src_ref=src, dst_ref=dst, send_sem=send_sems.at[6 * k + j], recv_sem=recv_sems.at[6 * k + j],
                device_id=to, device_id_type=MESH)

        local = [pltpu.make_async_copy(ins[k], outs[k].at[my_s], loc_sems.at[k]) for k in range(n)]
        for cp in local:
            cp.start()
        started = []
        for k in range(n):
            for j, chip in enumerate(chips):
                cp = rcopy(k, j, ins[k].at[pl.ds(c * halves[k], halves[k])], half(k, my_s, c), (*chip, c))
                cp.start()
                started.append(cp)
        for j, chip in enumerate(chips):
            s_j = 2 * chip[0] + chip[1]
            for k in range(n):
                rcopy(k, j, half(k, s_j, c), half(k, s_j, c), me).wait_recv()
                cp = rcopy(k, 3 + j, half(k, s_j, c), half(k, s_j, c), sibling)
                cp.start()
                started.append(cp)
        for j, chip in enumerate(chips):
            s_j = 2 * chip[0] + chip[1]
            for k in range(n):
                rcopy(k, 3 + j, half(k, s_j, 1 - c), half(k, s_j, 1 - c), me).wait_recv()
        for cp in started:
            cp.wait_send()
        for cp in local:
            cp.wait()

    return pl.pallas_call(
        body, name="gather_weights",
        in_specs=[pl.BlockSpec(memory_space=pl.ANY)] * n,
        out_specs=[pl.BlockSpec(memory_space=pl.ANY)] * n,
        out_shape=[jax.ShapeDtypeStruct((N_CHIPS,) + a.shape, a.dtype) for a in shards],
        scratch_shapes=[pltpu.SemaphoreType.DMA((6 * n,)), pltpu.SemaphoreType.DMA((6 * n,)),
                        pltpu.SemaphoreType.DMA((n,))],
    )(*shards)


def _rs_pair_exchange(gs):
    n = len(gs)
    halves = [g.shape[1] // 2 for g in gs]

    def body(*refs):
        ins, outs = refs[:n], refs[n:2 * n]
        send_sems, recv_sems = refs[2 * n:]
        x, y, c = _mesh_pos()
        cps = []
        for k in range(n):
            cp = pltpu.make_async_remote_copy(
                src_ref=ins[k].at[:, pl.ds((1 - c) * halves[k], halves[k])], dst_ref=outs[k],
                send_sem=send_sems.at[k], recv_sem=recv_sems.at[k], device_id=(x, y, 1 - c), device_id_type=MESH)
            cp.start()
            cps.append(cp)
        for cp in cps:
            cp.wait()

    return pl.pallas_call(
        body, name="rs_pair_exchange",
        in_specs=[pl.BlockSpec(memory_space=pl.ANY)] * n,
        out_specs=[pl.BlockSpec(memory_space=pl.ANY)] * n,
        out_shape=[jax.ShapeDtypeStruct((N_CHIPS, h) + g.shape[2:], g.dtype) for g, h in zip(gs, halves)],
        scratch_shapes=[pltpu.SemaphoreType.DMA((n,)), pltpu.SemaphoreType.DMA((n,))],
    )(*gs)


def _rs_pair_share(fulls):
    n = len(fulls)
    halves = [f.shape[0] // 2 for f in fulls]

    def body(*refs):
        outs = refs[n:2 * n]
        send_sems, recv_sems = refs[2 * n:]
        x, y, c = _mesh_pos()

        def copy(k, cc, to):
            rows = outs[k].at[pl.ds(cc * halves[k], halves[k])]
            return pltpu.make_async_remote_copy(
                src_ref=rows, dst_ref=rows, send_sem=send_sems.at[k], recv_sem=recv_sems.at[k],
                device_id=to, device_id_type=MESH)

        cps = [copy(k, c, (x, y, 1 - c)) for k in range(n)]
        for cp in cps:
            cp.start()
        for k in range(n):
            copy(k, 1 - c, (x, y, c)).wait_recv()
        for cp in cps:
            cp.wait_send()

    return pl.pallas_call(
        body, name="rs_pair_share",
        in_specs=[pl.BlockSpec(memory_space=pl.ANY)] * n,
        out_specs=[pl.BlockSpec(memory_space=pl.ANY)] * n,
        out_shape=[jax.ShapeDtypeStruct(f.shape, f.dtype) for f in fulls],
        input_output_aliases={k: k for k in range(n)},
        scratch_shapes=[pltpu.SemaphoreType.DMA((n,)), pltpu.SemaphoreType.DMA((n,))],
    )(*fulls)


def _pair_sum(g, recv, pos, name):
    _, rows, cols = g.shape
    h = rows // 2
    tr = _row_tile(h, 256)
    nb = h // tr

    def body(pos_ref, g_ref, r_ref, o_ref, own_ref):
        total = g_ref[0] + r_ref[0]
        o_ref[0] = total.astype(BF16)

        @pl.when(pl.program_id(1) == pos_ref[1])
        def _():
            own_ref[...] = total

    return pl.pallas_call(
        body, name=name,
        grid_spec=pltpu.PrefetchScalarGridSpec(
            num_scalar_prefetch=1, grid=(nb, N_CHIPS),
            in_specs=[pl.BlockSpec((1, tr, cols), lambda r, s, pos_ref: (s, pos_ref[0] * nb + r, 0)),
                      pl.BlockSpec((1, tr, cols), lambda r, s, pos_ref: (s, r, 0))],
            out_specs=[pl.BlockSpec((1, tr, cols), lambda r, s, pos_ref: (s, r, 0)),
                       pl.BlockSpec((tr, cols), lambda r, s, pos_ref: (r, 0))]),
        out_shape=[jax.ShapeDtypeStruct((N_CHIPS, h, cols), BF16), jax.ShapeDtypeStruct((h, cols), F32)],
        compiler_params=_params(2, 32))(pos, g, recv)


def _chip_sum(own, rb, pos, name):
    h, cols = own.shape
    tr = _row_tile(h, 256)
    nb = h // tr

    def body(pos_ref, c_ref, r_ref, o_ref):
        del pos_ref
        o_ref[...] = ((c_ref[...] + r_ref[0].astype(F32)) + r_ref[1].astype(F32)) + r_ref[2].astype(F32)

    return pl.pallas_call(
        body, name=name,
        grid_spec=pltpu.PrefetchScalarGridSpec(
            num_scalar_prefetch=1, grid=(nb,),
            in_specs=[pl.BlockSpec((tr, cols), lambda r, pos_ref: (r, 0)),
                      pl.BlockSpec((3, tr, cols), lambda r, pos_ref: (0, r, 0))],
            out_specs=pl.BlockSpec((tr, cols), lambda r, pos_ref: (pos_ref[0] * nb + r, 0))),
        out_shape=jax.ShapeDtypeStruct((2 * h, cols), F32),
        compiler_params=_params(1, 32))(pos, own, rb)


def _adamw_big(w, g, m, v, name):
    rows, cols = w.shape
    tr = _row_tile(rows, 256)

    def body(w_ref, g_ref, m_ref, v_ref, d_ref, nm_ref, nv_ref):
        d_ref[...], nm_ref[...], nv_ref[...] = _adamw(w_ref[...], g_ref[...], m_ref[...], v_ref[...])

    spec = pl.BlockSpec((tr, cols), lambda i: (i, 0))
    return pl.pallas_call(
        body, name=name, grid=(rows // tr,),
        in_specs=[spec] * 4, out_specs=[spec] * 3,
        out_shape=[jax.ShapeDtypeStruct((rows, cols), F32)] * 3,
        compiler_params=_params(1, 48))(w, g, m, v)


def _gather_small(loss, dfg, dlg, dgn, ddwb, dclg, dclb, dpwb, ddww, dmeta):
    d = loss.shape[1]
    w = dgn.shape[1]

    def body(loss_ref, dfg_ref, dlg_ref, dgn_ref, ddwb_ref, dclg_ref, dclb_ref, dpwb_ref, ddww_ref, dmeta_ref,
             gs_ref, gd_ref, gm_ref, send_sems, recv_sems):
        x, y, c = _mesh_pos()
        me = 4 * x + 2 * y + c
        gs_ref[me, 0:1, :] = loss_ref[...]
        gs_ref[me, 1:2, :] = dfg_ref[...]
        gs_ref[me, 2:3, :] = dlg_ref[...]
        gs_ref[me, 3:4, 0:w] = dgn_ref[...]
        gs_ref[me, 3:4, w:2 * w] = ddwb_ref[...]
        gs_ref[me, 4:5, 0:w] = dclg_ref[...]
        gs_ref[me, 4:5, w:2 * w] = dclb_ref[...]
        gs_ref[me, 5:6, 0:w] = dpwb_ref[...]
        gs_ref[me, 5:6, w:2 * w] = jnp.zeros((1, d - w), F32)
        gs_ref[me, 6:8, :] = jnp.zeros((2, d), F32)
        gd_ref[me] = ddww_ref[...]
        gm_ref[me] = dmeta_ref[...]
        bufs = (gs_ref, gd_ref, gm_ref)

        def peer(j):
            return (1 - x if j & 4 else x), (1 - y if j & 2 else y), (1 - c if j & 1 else c)

        def copy(k, j, slot, to):
            return pltpu.make_async_remote_copy(
                src_ref=bufs[k].at[slot], dst_ref=bufs[k].at[slot],
                send_sem=send_sems.at[7 * k + j - 1], recv_sem=recv_sems.at[7 * k + j - 1],
                device_id=to, device_id_type=MESH)

        cps = []
        for k in range(3):
            for j in range(1, N_DEV):
                cp = copy(k, j, me, peer(j))
                cp.start()
                cps.append(cp)
        for k in range(3):
            for j in range(1, N_DEV):
                px, py, pc = peer(j)
                copy(k, j, 4 * px + 2 * py + pc, (x, y, c)).wait_recv()
        for cp in cps:
            cp.wait_send()

    vm = pl.BlockSpec(memory_space=pltpu.VMEM)
    return pl.pallas_call(
        body, name="gather_small",
        in_specs=[vm] * 10, out_specs=[vm] * 3,
        out_shape=[jax.ShapeDtypeStruct((N_DEV, 8, d), F32),
                   jax.ShapeDtypeStruct((N_DEV,) + ddww.shape, F32),
                   jax.ShapeDtypeStruct((N_DEV,) + dmeta.shape, F32)],
        scratch_shapes=[pltpu.SemaphoreType.DMA((21,)), pltpu.SemaphoreType.DMA((21,))],
    )(loss, dfg, dlg, dgn, ddwb, dclg, dclb, dpwb, ddww, dmeta)


def _small_update(s_arr, gs, gd, gm, weights, ms, vs):
    d = gs.shape[2]
    w = d // 2
    n = len(weights)

    def body(s_ref, gs_ref, gd_ref, gm_ref, *refs):
        del s_ref
        w_refs, m_refs, v_refs = refs[:n], refs[n:2 * n], refs[2 * n:3 * n]
        loss_ref = refs[3 * n]
        g_refs = refs[3 * n + 1:4 * n + 1]
        d_refs = refs[4 * n + 1:5 * n + 1]
        nm_refs = refs[5 * n + 1:6 * n + 1]
        nv_refs = refs[6 * n + 1:7 * n + 1]

        def total(ref):
            t = ref[0]
            for dev in range(1, N_DEV):
                t = t + ref[dev]
            return t

        packed = total(gs_ref)
        loss_ref[...] = jnp.sum(packed[0:1, :], axis=1, keepdims=True) * (0.5 / d)
        grads = [packed[2:3, :], packed[1:2, :], packed[3:4, 0:w], packed[3:4, w:2 * w], packed[4:5, 0:w],
                 packed[4:5, w:2 * w], packed[5:6, 0:w], total(gd_ref), total(gm_ref)]
        for k in range(n):
            g = grads[k]
            g_refs[k][...] = g
            d_refs[k][...], nm_refs[k][...], nv_refs[k][...] = _adamw(w_refs[k][...], g, m_refs[k][...], v_refs[k][...])

    def whole(shape):
        return pl.BlockSpec(shape, lambda i, s_ref: (0,) * len(shape))

    wc = weights[7].shape[1]
    mc = weights[8].shape[1]
    shapes = [a.shape for a in weights]
    in_specs = ([whole(gs.shape),
                 pl.BlockSpec((N_DEV, gd.shape[1], wc), lambda i, s_ref: (0, 0, s_ref[0])),
                 pl.BlockSpec((N_DEV, gm.shape[1], mc), lambda i, s_ref: (0, 0, s_ref[0]))]
                + [whole(s) for s in shapes] * 3)
    out_specs = [whole((1, 1))] + [whole(s) for s in shapes] * 4
    out_shape = [jax.ShapeDtypeStruct((1, 1), F32)] + [jax.ShapeDtypeStruct(s, F32) for s in shapes] * 4
    outs = pl.pallas_call(
        body, name="small_update",
        grid_spec=pltpu.PrefetchScalarGridSpec(num_scalar_prefetch=1, grid=(1,), in_specs=in_specs,
                                               out_specs=out_specs),
        out_shape=out_shape,
        compiler_params=_params(1, 32))(s_arr, gs, gd, gm, *weights, *ms, *vs)
    loss = outs[0]
    return loss, outs[1:n + 1], outs[n + 1:2 * n + 1], outs[2 * n + 1:3 * n + 1], outs[3 * n + 1:4 * n + 1]


def kernel(x, meta_tokens, ln_g, w_in, ret_gn_g, conv_dw_w, conv_dw_b, conv_ln_g, conv_ln_b, conv_pw_w, conv_pw_b, w_out, final_g, loss_target, m_meta_tokens, m_ln_g, m_w_in, m_ret_gn_g, m_conv_dw_w, m_conv_dw_b, m_conv_ln_g, m_conv_ln_b, m_conv_pw_w, m_conv_pw_b, m_w_out, m_final_g, v_meta_tokens, v_ln_g, v_w_in, v_ret_gn_g, v_conv_dw_w, v_conv_dw_b, v_conv_ln_g, v_conv_ln_b, v_conv_pw_w, v_conv_pw_b, v_w_out, v_final_g):
    seq, d = x.shape[1], x.shape[2]
    w = ret_gn_g.shape[1]
    hd = w // RET_HEADS
    lp = CHUNK + seq
    ns = w_in.shape[2]
    mx, my, mc = lax.axis_index("x"), lax.axis_index("y"), lax.axis_index("c")
    my_s = 2 * mx + my
    s_arr = my_s.astype(jnp.int32).reshape(1)
    pos = jnp.stack([mc, my_s]).astype(jnp.int32)
    order = jnp.stack([my_s, 2 * (1 - mx) + my, 2 * mx + (1 - my), 2 * (1 - mx) + (1 - my)]).astype(jnp.int32)

    dw_pad = jnp.pad(conv_dw_w[0], ((0, HALO - CONV_K), (0, 0)))
    dw4, meta4 = _gather_shards([dw_pad, meta_tokens])
    dw_full = dw4.transpose(1, 0, 2).reshape(HALO, w)
    meta_full = meta4.transpose(1, 0, 2).reshape(N_META, d)

    hp = jnp.concatenate([jnp.zeros((LEAD, d), F32), meta_full, x[0]], axis=0)
    consts = _ret_consts()
    cos, sin = _rope_tables(lp, hd // 2)
    fg2 = final_g.reshape(1, d)

    proj, r1, hn, w4, pw4, wo4 = _in_proj_gather(
        order, hp, ln_g, _cast_into_gathered(w_in[0], s_arr, "cast_w_in"),
        _cast_into_gathered(conv_pw_w[0], s_arr, "cast_pw_w"), _cast_into_gathered(w_out[0], s_arr, "cast_w_out"))
    pw_full = pw4.reshape(w, w)
    wo_full = wo4.reshape(2 * w, d)
    y_ret, ssave = _ret_fwd(proj, cos, sin, ret_gn_g, consts)
    y_conv, u1 = _conv_fwd(proj, dw_full, conv_dw_b, conv_ln_g, conv_ln_b, pw_full, conv_pw_b)
    dh2, dy, loss_l, dfg = _out_proj_loss(y_ret, y_conv, hp, loss_target[0], wo_full, fg2)

    g_wo = _dw_out(y_ret, y_conv, dh2)
    dproj, dgn = _ret_bwd(proj, dy, ssave, cos, sin, ret_gn_g, consts)
    dproj, du1, g_pw, dpwb, dclg, dclb = _conv_bwd_pw(dy, proj, u1, conv_ln_g, conv_ln_b, pw_full, conv_pw_b, dproj)
    dproj, ddww, ddwb = _conv_bwd_dw(du1, proj, dw_full, dproj)
    g_win = _dw_in(hn, dproj, ns)

    gs = [g_win, g_wo.reshape(N_CHIPS, (2 * w) // N_CHIPS, d), g_pw.reshape(N_CHIPS, w // N_CHIPS, w)]
    recv = _rs_pair_exchange(gs)
    names = ("w_in", "w_out", "pw_w")
    sums = [_pair_sum(g, r, pos, "pair_sum_" + nm) for g, r, nm in zip(gs, recv, names)]
    dh, dlg, rb = _in_proj_bwd(dproj, w4, hp, r1, dh2, ln_g, [cs_ for cs_, _ in sums])
    grad_x = dh[CHUNK:][None]
    dmeta = dh[LEAD:CHUNK]
    fulls = [_chip_sum(own, r, pos, "chip_sum_" + nm) for (_, own), r, nm in zip(sums, rb, names)]
    grad_w_in, grad_w_out, grad_pw = _rs_pair_share(fulls)
    d_win, nm_win, nv_win = _adamw_big(w_in[0], grad_w_in, m_w_in[0], v_w_in[0], "adamw_w_in")
    d_wo, nm_wo, nv_wo = _adamw_big(w_out[0], grad_w_out, m_w_out[0], v_w_out[0], "adamw_w_out")
    d_pw, nm_pw, nv_pw = _adamw_big(conv_pw_w[0], grad_pw, m_conv_pw_w[0], v_conv_pw_w[0], "adamw_pw_w")

    gsm, gdm, gmm = _gather_small(loss_l, dfg, dlg, dgn, ddwb, dclg, dclb, dpwb, ddww, dmeta)

    def pad_dw(a):
        return jnp.pad(a[0], ((0, HALO - CONV_K), (0, 0)))

    small_w = [ln_g, fg2, ret_gn_g, conv_dw_b, conv_ln_g, conv_ln_b, conv_pw_b, dw_pad, meta_tokens]
    small_m = [m_ln_g, m_final_g.reshape(1, d), m_ret_gn_g, m_conv_dw_b, m_conv_ln_g, m_conv_ln_b, m_conv_pw_b,
               pad_dw(m_conv_dw_w), m_meta_tokens]
    small_v = [v_ln_g, v_final_g.reshape(1, d), v_ret_gn_g, v_conv_dw_b, v_conv_ln_g, v_conv_ln_b, v_conv_pw_b,
               pad_dw(v_conv_dw_w), v_meta_tokens]
    loss, sg, sd, snm, snv = _small_update(s_arr, gsm, gdm, gmm, small_w, small_m, small_v)

    def assemble(small, big_in, big_pw, big_out):
        ln, fg, gn, dwb, clg, clb, pwb, dww, meta = small
        return (meta, ln, big_in[None], gn, dww[:CONV_K][None], dwb, clg, clb, big_pw[None], pwb, big_out[None],
                fg.reshape(d))

    return (loss.reshape(()), grad_x,
            *assemble(sg, grad_w_in, grad_pw, grad_w_out),
            *assemble(sd, d_win, d_pw, d_wo),
            *assemble(snm, nm_win, nm_pw, nm_wo),
            *assemble(snv, nv_win, nv_pw, nv_wo))
```

```python
import functools

import jax
import jax.numpy as jnp
from jax import lax
from jax.experimental import pallas as pl
from jax.experimental.pallas import tpu as pltpu

F32 = jnp.float32
BF16 = jnp.bfloat16
MESH = pl.DeviceIdType.MESH

N_META = 16
CHUNK = 128
LEAD = (-N_META) % CHUNK
RET_HEADS = 4
CONV_K = 31
HALO = 32
ROPE_BASE = 10000.0
EPS = 1e-6
N_CHIPS = 4
N_DEV = 8
SEND_PIECES = 8

ADAM_LR = 0.001
ADAM_B1 = 0.9
ADAM_B2 = 0.999
ADAM_EPS = 1e-08
ADAM_WD = 0.01
ADAM_STEP = 10

MIB = 2 ** 20


def _params(n_grid_axes, vmem_mib):
    return pltpu.CompilerParams(dimension_semantics=("arbitrary",) * n_grid_axes,
                                vmem_limit_bytes=vmem_mib * MIB)


def _row_tile(n, pref):
    for t in (1664, 1280, 1024, 640, 512, 384, 320, 256, 128, 64, 32, 16, 8):
        if t <= pref and n % t == 0:
            return t
    raise ValueError(f"no row tile for {n}")


def _dot(a, b):
    return jnp.dot(a, b, preferred_element_type=F32)


def _dot_nt(a, b):
    return lax.dot_general(a, b, (((1,), (1,)), ((), ())), preferred_element_type=F32)


def _dot_tn(a, b):
    return lax.dot_general(a, b, (((0,), (0,)), ((), ())), preferred_element_type=F32)


def _sigmoid(x):
    return jax.nn.sigmoid(x)


def _dsilu(x, s):
    return s * (1.0 + x * (1.0 - s))


def _mean(x):
    return jnp.mean(x, axis=-1, keepdims=True)


def _colsum(x):
    return jnp.sum(x, axis=0, keepdims=True)


def _rope(x, cos, sin):
    half = x.shape[-1] // 2
    x1, x2 = x[:, :half], x[:, half:]
    return jnp.concatenate([x1 * cos - x2 * sin, x1 * sin + x2 * cos], axis=-1)


def _rope_t(d, cos, sin):
    half = d.shape[-1] // 2
    d1, d2 = d[:, :half], d[:, half:]
    return jnp.concatenate([d1 * cos + d2 * sin, d2 * cos - d1 * sin], axis=-1)


def _adamw(w, g, m, v):
    m = ADAM_B1 * m + (1.0 - ADAM_B1) * g
    v = ADAM_B2 * v + (1.0 - ADAM_B2) * (g * g)
    m_hat = m / (1.0 - ADAM_B1 ** ADAM_STEP)
    v_hat = v / (1.0 - ADAM_B2 ** ADAM_STEP)
    delta = -ADAM_LR * (m_hat / (jnp.sqrt(v_hat) + ADAM_EPS) + ADAM_WD * w)
    return delta, m, v


def _ret_consts():
    h = jnp.arange(RET_HEADS, dtype=F32)
    log_g = jnp.log(1.0 - jnp.exp2(-5.0 - h))
    idx = jnp.arange(CHUNK, dtype=F32)
    rel = idx[:, None] - idx[None, :]
    dmask = jnp.where(rel[None] >= 0, jnp.exp(jnp.maximum(rel, 0.0)[None] * log_g[:, None, None]), 0.0)
    qd = jnp.exp((idx[None, :] + 1.0) * log_g[:, None])[:, :, None]
    kd = jnp.exp((CHUNK - 1.0 - idx[None, :]) * log_g[:, None])[:, :, None]
    cd = jnp.exp(CHUNK * log_g)[:, None, None]
    return dmask, qd, kd, cd


def _rope_tables(n_rows, half):
    pos = jnp.arange(n_rows, dtype=F32) - float(LEAD)
    inv_freq = ROPE_BASE ** (-jnp.arange(half, dtype=F32) / half)
    ang = pos[:, None] * inv_freq[None, :]
    return jnp.cos(ang), jnp.sin(ang)


def _cast_into_gathered(a, s_arr, name):
    rows, cols = a.shape
    tr = _row_tile(rows, 256)

    def body(s_ref, a_ref, o_ref):
        del s_ref
        o_ref[0] = a_ref[...].astype(BF16)

    return pl.pallas_call(
        body, name=name,
        grid_spec=pltpu.PrefetchScalarGridSpec(
            num_scalar_prefetch=1, grid=(rows // tr,),
            in_specs=[pl.BlockSpec((tr, cols), lambda i, s_ref: (i, 0))],
            out_specs=pl.BlockSpec((1, tr, cols), lambda i, s_ref: (s_ref[0], i, 0))),
        out_shape=jax.ShapeDtypeStruct((N_CHIPS, rows, cols), BF16),
        compiler_params=_params(1, 32))(s_arr, a)


def _in_proj_gather(order, hp, ln_g, w4, pw4, wo4):
    lp, d = hp.shape
    ns = w4.shape[2]
    tm = _row_tile(lp, 640)
    nt = lp // tm
    assert nt >= 2, "the hn write-back of a row tile is waited for one step later, before any pass re-reads it"
    gathered = (w4, pw4, wo4)
    halves = [a.shape[1] // 2 for a in gathered]
    n = len(gathered)

    def body(order_ref, h_ref, g_ref, w_in, pw_in, wo_in, proj_ref, r_ref, hn_hbm, w_out, pw_out, wo_out,
             wbuf, hnbuf, send_sems, recv_sems, hn_out_sems, hn_in_sems, w_sem):
        del order_ref, w_in, pw_in, wo_in
        t, i = pl.program_id(0), pl.program_id(1)
        slot = (t * nt + i) % 2
        x, y, c = _mesh_pos()
        me, sibling = (x, y, c), (x, y, 1 - c)
        my_s = 2 * x + y
        chips = _other_chips(x, y)
        outs = (w_out, pw_out, wo_out)

        def half(k, s, cc):
            return outs[k].at[s, pl.ds(cc * halves[k], halves[k])]

        def rcopy(k, j, rows, to):
            return pltpu.make_async_remote_copy(
                src_ref=rows, dst_ref=rows, send_sem=send_sems.at[6 * k + j], recv_sem=recv_sems.at[6 * k + j],
                device_id=to, device_id_type=MESH)

        def send(k, j):
            return rcopy(k, j, half(k, my_s, c), (*chips[j], c))

        def send_in_pieces(k, j):
            rows = halves[k] // SEND_PIECES
            for q in range(SEND_PIECES):
                piece = outs[k].at[my_s, pl.ds(c * halves[k] + q * rows, rows)]
                rcopy(k, j, piece, (*chips[j], c)).start()

        def shard_of(j):
            return 2 * chips[j][0] + chips[j][1]

        def forward(k, j):
            return rcopy(k, 3 + j, half(k, shard_of(j), c), sibling)

        def land(k, j):
            rcopy(k, j, half(k, shard_of(j), c), me).wait_recv()
            forward(k, j).start()

        def landed_from_sibling(k, j):
            rcopy(k, 3 + j, half(k, shard_of(j), 1 - c), me).wait_recv()

        def load_w(s):
            cp = pltpu.make_async_copy(w_out.at[s], wbuf, w_sem)
            cp.start()
            cp.wait()

        def hn_out(sl, row_tile):
            return pltpu.make_async_copy(hnbuf.at[sl], hn_hbm.at[pl.ds(row_tile * tm, tm)], hn_out_sems.at[sl])

        def hn_in(sl, row_tile):
            return pltpu.make_async_copy(hn_hbm.at[pl.ds(row_tile * tm, tm)], hnbuf.at[sl], hn_in_sems.at[sl])

        @pl.when((t == 0) & (i == 0))
        def _():
            for j in range(2):
                send_in_pieces(0, j)
            load_w(my_s)

        for j in range(3):
            @pl.when((t == j + 1) & (i == 0))
            def _(j=j):
                if j == 0:
                    send_in_pieces(0, 2)
                    for k in range(1, n):
                        for jj in range(3):
                            send(k, jj).start()
                land(0, j)
                landed_from_sibling(0, j)
                load_w(shard_of(j))

        @pl.when(t == 0)
        def _():
            h = h_ref[...]
            r = lax.rsqrt(_mean(h * h) + EPS)
            hnbuf[slot] = ((h * r) * g_ref[...]).astype(BF16)
            r_ref[...] = r
            hn_out(slot, i).start()

        @pl.when(t > 0)
        def _():
            hn_in(slot, i).wait()

        proj_ref[...] = _dot(hnbuf[slot], wbuf[...])

        @pl.when(((t == 0) & (i > 0)) | ((t == 1) & (i == 0)))
        def _():
            hn_out(1 - slot, jnp.where(i > 0, i - 1, nt - 1)).wait()

        last = (t == N_CHIPS - 1) & (i == nt - 1)

        @pl.when(((t > 0) | (i == nt - 1)) & jnp.logical_not(last))
        def _():
            hn_in(1 - slot, jnp.where(i == nt - 1, 0, i + 1)).start()

        @pl.when(last)
        def _():
            for k in range(1, n):
                for j in range(3):
                    land(k, j)
            for k in range(1, n):
                for j in range(3):
                    landed_from_sibling(k, j)
            for k in range(n):
                for j in range(3):
                    send(k, j).wait_send()
                    forward(k, j).wait_send()

    def frozen(t, i):
        return jnp.where(t == 0, i, nt - 1)

    any_spec = pl.BlockSpec(memory_space=pl.ANY)
    return pl.pallas_call(
        body, name="in_proj_gather",
        grid_spec=pltpu.PrefetchScalarGridSpec(
            num_scalar_prefetch=1, grid=(N_CHIPS, nt),
            in_specs=[pl.BlockSpec((tm, d), lambda t, i, o: (frozen(t, i), 0)),
                      pl.BlockSpec((1, d), lambda t, i, o: (0, 0)),
                      any_spec, any_spec, any_spec],
            out_specs=[pl.BlockSpec((tm, ns), lambda t, i, o: (i, o[t])),
                       pl.BlockSpec((tm, 1), lambda t, i, o: (frozen(t, i), 0)),
                       any_spec, any_spec, any_spec, any_spec],
            scratch_shapes=[pltpu.VMEM((d, ns), BF16), pltpu.VMEM((2, tm, d), BF16),
                            pltpu.SemaphoreType.DMA((6 * n,)), pltpu.SemaphoreType.DMA((6 * n,)),
                            pltpu.SemaphoreType.DMA((2,)), pltpu.SemaphoreType.DMA((2,)),
                            pltpu.SemaphoreType.DMA]),
        out_shape=[jax.ShapeDtypeStruct((lp, N_CHIPS * ns), F32),
                   jax.ShapeDtypeStruct((lp, 1), F32),
                   jax.ShapeDtypeStruct((lp, d), BF16)]
                  + [jax.ShapeDtypeStruct(a.shape, a.dtype) for a in gathered],
        input_output_aliases={3: 3, 4: 4, 5: 5},
        compiler_params=_params(2, 48))(order, hp, ln_g, w4, pw4, wo4)


def _ret_fwd(proj, cos, sin, gn_g, consts):
    lp = proj.shape[0]
    w = gn_g.shape[1]
    hd = w // RET_HEADS
    nch = lp // CHUNK
    dmask, qd, kd, cd = consts

    def body(q_ref, k_ref, v_ref, g_ref, cos_ref, sin_ref, gn_ref, m_ref, qd_ref, kd_ref, cd_ref,
             y_ref, ssave_ref, s_scr):
        @pl.when(pl.program_id(0) == 0)
        def _():
            s_scr[...] = jnp.zeros_like(s_scr)

        cos_t, sin_t = cos_ref[...], sin_ref[...]
        for h in range(RET_HEADS):
            sl = slice(h * hd, (h + 1) * hd)
            qr = _rope(q_ref[:, sl], cos_t, sin_t)
            kr = _rope(k_ref[:, sl], cos_t, sin_t) * (hd ** -0.5)
            vb = v_ref[:, sl].astype(BF16)
            sc = _dot_nt(qr.astype(BF16), kr.astype(BF16)) * m_ref[h]
            state = s_scr[h]
            sb = state.astype(BF16)
            ssave_ref[0, h] = sb
            out = _dot(sc.astype(BF16), vb) + _dot((qr * qd_ref[h]).astype(BF16), sb)
            s_scr[h] = state * cd_ref[h] + _dot_tn((kr * kd_ref[h]).astype(BF16), vb)
            dev = out - _mean(out)
            yn = dev * lax.rsqrt(_mean(dev * dev) + EPS)
            g = g_ref[:, sl]
            y_ref[:, sl] = ((yn * gn_ref[:, sl]) * (g * _sigmoid(g))).astype(BF16)

    def col(j):
        return pl.BlockSpec((CHUNK, w), lambda i: (i, j))

    def whole(a):
        return pl.BlockSpec(a.shape, lambda i: (0,) * a.ndim)

    return pl.pallas_call(
        body, name="ret_fwd", grid=(nch,),
        in_specs=[col(0), col(1), col(2), col(3),
                  pl.BlockSpec((CHUNK, hd // 2), lambda i: (i, 0)),
                  pl.BlockSpec((CHUNK, hd // 2), lambda i: (i, 0)),
                  whole(gn_g), whole(dmask), whole(qd), whole(kd), whole(cd)],
        out_specs=[pl.BlockSpec((CHUNK, w), lambda i: (i, 0)),
                   pl.BlockSpec((1, RET_HEADS, hd, hd), lambda i: (i, 0, 0, 0))],
        out_shape=[jax.ShapeDtypeStruct((lp, w), BF16),
                   jax.ShapeDtypeStruct((nch, RET_HEADS, hd, hd), BF16)],
        scratch_shapes=[pltpu.VMEM((RET_HEADS, hd, hd), F32)],
        compiler_params=_params(1, 32))(proj, proj, proj, proj, cos, sin, gn_g, dmask, qd, kd, cd)


def _tap_groups(start, flip):
    groups = {}
    for j in range(CONV_K):
        o = start + (CONV_K - 1 - j if flip else j)
        groups.setdefault(o % 8, []).append((o // 8, j))
    return groups


def _shift_up(win, s):
    return win if s == 0 else pltpu.roll(win, win.shape[0] - s, axis=0)


def _dw_taps(src_ref, w_ref, dst_ref, bias, *, rows, start, flip, rb):
    cw = dst_ref.shape[1]
    lb = min(128, cw)
    groups = _tap_groups(start, flip)

    def rb_body(r, carry):
        base = pl.multiple_of(r * rb, rb)
        for cb in range(cw // lb):
            ls = slice(cb * lb, (cb + 1) * lb)
            win = src_ref[pl.ds(base, rb + HALO), ls]
            acc = jnp.zeros((rb, lb), F32) if bias is None else jnp.broadcast_to(bias[:, ls], (rb, lb))
            for s, taps in groups.items():
                ws = _shift_up(win, s)
                for a, j in taps:
                    acc = acc + ws[8 * a:8 * a + rb, :] * w_ref[j:j + 1, ls]
            dst_ref[pl.ds(base, rb), ls] = acc
        return carry

    lax.fori_loop(0, rows // rb, rb_body, 0)


def _conv_fwd(proj, dw_w, dw_b, cln_g, cln_b, pw_w, pw_b):
    lp = proj.shape[0]
    cw = dw_b.shape[1]
    tm = _row_tile(lp, 640)
    rb = _row_tile(tm, 64)

    def body(a_ref, b_ref, gc_ref, w_ref, wb_ref, lg_ref, lb_ref, pw_ref, pb_ref, y_ref, u1_ref, buf):
        @pl.when(pl.program_id(0) == 0)
        def _():
            buf[0:HALO, :] = jnp.zeros((HALO, cw), F32)

        buf[HALO:HALO + tm, :] = a_ref[...] * _sigmoid(b_ref[...])
        _dw_taps(buf, w_ref, u1_ref, wb_ref[...], rows=tm, start=HALO - (CONV_K - 1), flip=False, rb=rb)
        buf[0:HALO, :] = buf[tm:tm + HALO, :]
        u1 = u1_ref[...]
        dev = u1 - _mean(u1)
        z = dev * lax.rsqrt(_mean(dev * dev) + EPS) * lg_ref[...] + lb_ref[...]
        u3 = (z * _sigmoid(z)).astype(BF16)
        u4 = _dot(u3, pw_ref[...]) + pb_ref[...]
        gc = gc_ref[...]
        y_ref[...] = (u4 * (gc * _sigmoid(gc))).astype(BF16)

    def col(j):
        return pl.BlockSpec((tm, cw), lambda i: (i, j))

    def whole(a):
        return pl.BlockSpec(a.shape, lambda i: (0,) * a.ndim)

    return pl.pallas_call(
        body, name="conv_fwd", grid=(lp // tm,),
        in_specs=[col(4), col(5), col(6), whole(dw_w), whole(dw_b), whole(cln_g), whole(cln_b),
                  whole(pw_w), whole(pw_b)],
        out_specs=[pl.BlockSpec((tm, cw), lambda i: (i, 0)), pl.BlockSpec((tm, cw), lambda i: (i, 0))],
        out_shape=[jax.ShapeDtypeStruct((lp, cw), BF16), jax.ShapeDtypeStruct((lp, cw), F32)],
        scratch_shapes=[pltpu.VMEM((tm + HALO, cw), F32)],
        compiler_params=_params(1, 48))(proj, proj, proj, dw_w, dw_b, cln_g, cln_b, pw_w, pw_b)


def _out_proj_loss(yr, yc, hp, tgt, w_out, final_g):
    lp, d = hp.shape
    w = yr.shape[1]
    tm = _row_tile(lp, 320)
    nt = lp // tm
    assert tm > CHUNK and nt >= 2

    def body(yr_ref, yc_ref, h_ref, t_hbm, w_ref, fg_ref, dh2_ref, dy_ref, loss_ref, dfg_ref, tbuf, tsems):
        i = pl.program_id(0)
        slot = i % 2

        def first_fetch():
            return pltpu.make_async_copy(t_hbm.at[pl.ds(0, tm - CHUNK)], tbuf.at[0, pl.ds(CHUNK, tm - CHUNK)],
                                         tsems.at[0])

        def fetch(sl, tile):
            return pltpu.make_async_copy(t_hbm.at[pl.ds(tile * tm - CHUNK, tm)], tbuf.at[sl], tsems.at[sl])

        @pl.when(i == 0)
        def _():
            loss_ref[...] = jnp.zeros_like(loss_ref)
            dfg_ref[...] = jnp.zeros_like(dfg_ref)
            tbuf[0, 0:CHUNK, :] = jnp.zeros((CHUNK, d), F32)
            first_fetch().start()

        @pl.when(i + 1 < nt)
        def _():
            fetch(1 - slot, i + 1).start()

        @pl.when(i == 0)
        def _():
            first_fetch().wait()

        @pl.when(i > 0)
        def _():
            fetch(slot, i).wait()

        h2 = h_ref[...] + (_dot(yr_ref[...], w_ref[0:w, :]) + _dot(yc_ref[...], w_ref[w:2 * w, :]))
        r2 = lax.rsqrt(_mean(h2 * h2) + EPS)
        h2n = h2 * r2
        fg = fg_ref[...]
        rows = i * tm + lax.broadcasted_iota(jnp.int32, (tm, 1), 0)
        err = jnp.where(rows >= CHUNK, h2n * fg - tbuf[slot], 0.0)
        loss_ref[...] += _colsum(err * err)
        dout = err * (1.0 / d)
        dfg_ref[...] += _colsum(dout * h2n)
        dz = dout * fg
        dh2 = r2 * (dz - h2n * _mean(dz * h2n))
        dh2_ref[...] = dh2
        db = dh2.astype(BF16)
        dy_ref[:, 0:w] = _dot_nt(db, w_ref[0:w, :])
        dy_ref[:, w:2 * w] = _dot_nt(db, w_ref[w:2 * w, :])

    def row(cols):
        return pl.BlockSpec((tm, cols), lambda i: (i, 0))

    return pl.pallas_call(
        body, name="out_proj_loss", grid=(nt,),
        in_specs=[row(w), row(w), row(d), pl.BlockSpec(memory_space=pl.ANY),
                  pl.BlockSpec(memory_space=pltpu.VMEM),
                  pl.BlockSpec((1, d), lambda i: (0, 0))],
        out_specs=[row(d), row(2 * w), pl.BlockSpec((1, d), lambda i: (0, 0)),
                   pl.BlockSpec((1, d), lambda i: (0, 0))],
        out_shape=[jax.ShapeDtypeStruct((lp, d), F32), jax.ShapeDtypeStruct((lp, 2 * w), F32),
                   jax.ShapeDtypeStruct((1, d), F32), jax.ShapeDtypeStruct((1, d), F32)],
        scratch_shapes=[pltpu.VMEM((2, tm, d), F32), pltpu.SemaphoreType.DMA((2,))],
        compiler_params=_params(1, 56))(yr, yc, hp, tgt, w_out, final_g)


def _dw_out(yr, yc, dh2):
    lp, d = dh2.shape
    w = yr.shape[1]
    tm = _row_tile(lp, 1664)
    nb = 2
    dn = d // nb

    def body(yr_ref, yc_ref, d_ref, o_ref):
        @pl.when(pl.program_id(1) == 0)
        def _():
            o_ref[...] = jnp.zeros_like(o_ref)

        db = d_ref[...].astype(BF16)
        o_ref[0:w, :] += _dot_tn(yr_ref[...], db)
        o_ref[w:2 * w, :] += _dot_tn(yc_ref[...], db)

    return pl.pallas_call(
        body, name="dw_out", grid=(nb, lp // tm),
        in_specs=[pl.BlockSpec((tm, w), lambda n, i: (i, 0)),
                  pl.BlockSpec((tm, w), lambda n, i: (i, 0)),
                  pl.BlockSpec((tm, dn), lambda n, i: (i, n))],
        out_specs=pl.BlockSpec((2 * w, dn), lambda n, i: (0, n)),
        out_shape=jax.ShapeDtypeStruct((2 * w, d), F32),
        compiler_params=_params(2, 52))(yr, yc, dh2)


def _ret_bwd(proj, dy, ssave, cos, sin, gn_g, consts):
    lp = proj.shape[0]
    w = gn_g.shape[1]
    hd = w // RET_HEADS
    nch = lp // CHUNK
    dmask, qd, kd, cd = consts

    def body(q_ref, k_ref, v_ref, g_ref, dy_ref, ss_ref, cos_ref, sin_ref, gn_ref, m_ref, qd_ref, kd_ref,
             cd_ref, dp_ref, dgn_ref, ds_scr):
        @pl.when(pl.program_id(0) == 0)
        def _():
            ds_scr[...] = jnp.zeros_like(ds_scr)
            dgn_ref[...] = jnp.zeros_like(dgn_ref)

        cos_t, sin_t = cos_ref[...], sin_ref[...]
        for h in range(RET_HEADS):
            sl = slice(h * hd, (h + 1) * hd)
            qr = _rope(q_ref[:, sl], cos_t, sin_t)
            kr = _rope(k_ref[:, sl], cos_t, sin_t) * (hd ** -0.5)
            qb, kb = qr.astype(BF16), kr.astype(BF16)
            vb = v_ref[:, sl].astype(BF16)
            sb = ss_ref[0, h]
            mask = m_ref[h]
            qdec, kdec = qd_ref[h], kd_ref[h]
            scb = (_dot_nt(qb, kb) * mask).astype(BF16)
            qdq = (qr * qdec).astype(BF16)
            kdk = (kr * kdec).astype(BF16)
            out = _dot(scb, vb) + _dot(qdq, sb)
            dev = out - _mean(out)
            rstd = lax.rsqrt(_mean(dev * dev) + EPS)
            yn = dev * rstd
            g = g_ref[:, sl]
            sg = _sigmoid(g)
            gng = gn_ref[:, sl]
            dyv = dy_ref[:, sl]
            dgr = dyv * (yn * gng) * _dsilu(g, sg)
            silu_g = g * sg
            dgn_ref[:, sl] += _colsum(dyv * yn * silu_g)
            dyn = dyv * gng * silu_g
            dout = rstd * (dyn - _mean(dyn) - yn * _mean(dyn * yn))
            dob = dout.astype(BF16)
            dscb = (_dot_nt(dob, vb) * mask).astype(BF16)
            dstate = ds_scr[h]
            dsb = dstate.astype(BF16)
            dq = _dot(dscb, kb) + _dot_nt(dob, sb) * qdec
            dk = _dot_tn(dscb, qb) + _dot_nt(vb, dsb) * kdec
            dv = _dot_tn(scb, dob) + _dot(kdk, dsb)
            ds_scr[h] = dstate * cd_ref[h] + _dot_tn(qdq, dob)
            dp_ref[:, 0 * w + h * hd:0 * w + (h + 1) * hd] = _rope_t(dq, cos_t, sin_t).astype(BF16)
            dp_ref[:, 1 * w + h * hd:1 * w + (h + 1) * hd] = (_rope_t(dk, cos_t, sin_t) * (hd ** -0.5)).astype(BF16)
            dp_ref[:, 2 * w + h * hd:2 * w + (h + 1) * hd] = dv.astype(BF16)
            dp_ref[:, 3 * w + h * hd:3 * w + (h + 1) * hd] = dgr.astype(BF16)

    def rev(i):
        return nch - 1 - i

    def col(j):
        return pl.BlockSpec((CHUNK, w), lambda i: (rev(i), j))

    def whole(a):
        return pl.BlockSpec(a.shape, lambda i: (0,) * a.ndim)

    return pl.pallas_call(
        body, name="ret_bwd", grid=(nch,),
        in_specs=[col(0), col(1), col(2), col(3),
                  pl.BlockSpec((CHUNK, w), lambda i: (rev(i), 0)),
                  pl.BlockSpec((1, RET_HEADS, hd, hd), lambda i: (rev(i), 0, 0, 0)),
                  pl.BlockSpec((CHUNK, hd // 2), lambda i: (rev(i), 0)),
                  pl.BlockSpec((CHUNK, hd // 2), lambda i: (rev(i), 0)),
                  whole(gn_g), whole(dmask), whole(qd), whole(kd), whole(cd)],
        out_specs=[pl.BlockSpec((CHUNK, 4 * w), lambda i: (rev(i), 0)),
                   pl.BlockSpec((1, w), lambda i: (0, 0))],
        out_shape=[jax.ShapeDtypeStruct((lp, 7 * w), BF16), jax.ShapeDtypeStruct((1, w), F32)],
        scratch_shapes=[pltpu.VMEM((RET_HEADS, hd, hd), F32)],
        compiler_params=_params(1, 32))(proj, proj, proj, proj, dy, ssave, cos, sin, gn_g, dmask, qd, kd, cd)


def _conv_bwd_pw(dy, proj, u1, cln_g, cln_b, pw_w, pw_b, dproj):
    lp, cw = u1.shape
    tm = _row_tile(lp, 320)

    def body(dy_ref, gc_ref, u1_ref, lg_ref, lb_ref, pw_ref, pb_ref, dp_in, dp_ref, du1_ref, dpw_ref,
             dpb_ref, dlg_ref, dlb_ref):
        del dp_in

        @pl.when(pl.program_id(0) == 0)
        def _():
            dpw_ref[...] = jnp.zeros_like(dpw_ref)
            dpb_ref[...] = jnp.zeros_like(dpb_ref)
            dlg_ref[...] = jnp.zeros_like(dlg_ref)
            dlb_ref[...] = jnp.zeros_like(dlb_ref)

        u1 = u1_ref[...]
        dev = u1 - _mean(u1)
        rstd = lax.rsqrt(_mean(dev * dev) + EPS)
        u1n = dev * rstd
        lg = lg_ref[...]
        z = u1n * lg + lb_ref[...]
        sz = _sigmoid(z)
        u3b = (z * sz).astype(BF16)
        u4 = _dot(u3b, pw_ref[...]) + pb_ref[...]
        gc = gc_ref[...]
        sgc = _sigmoid(gc)
        dyc = dy_ref[...]
        du4 = dyc * (gc * sgc)
        dp_ref[...] = (dyc * u4 * _dsilu(gc, sgc)).astype(BF16)
        du4b = du4.astype(BF16)
        dpb_ref[...] += _colsum(du4)
        dpw_ref[...] += _dot_tn(u3b, du4b)
        dz = _dot_nt(du4b, pw_ref[...]) * _dsilu(z, sz)
        dlg_ref[...] += _colsum(dz * u1n)
        dlb_ref[...] += _colsum(dz)
        dn = dz * lg
        du1_ref[...] = rstd * (dn - _mean(dn) - u1n * _mean(dn * u1n))

    def row(j):
        return pl.BlockSpec((tm, cw), lambda i: (i, j))

    def whole(a):
        return pl.BlockSpec(a.shape, lambda i: (0,) * a.ndim)

    def acc(r):
        return pl.BlockSpec((r, cw), lambda i: (0, 0))

    return pl.pallas_call(
        body, name="conv_bwd_pw", grid=(lp // tm,),
        in_specs=[row(1), row(6), row(0), whole(cln_g), whole(cln_b), whole(pw_w), whole(pw_b),
                  pl.BlockSpec(memory_space=pl.ANY)],
        out_specs=[row(6), row(0), acc(cw), acc(1), acc(1), acc(1)],
        out_shape=[jax.ShapeDtypeStruct(dproj.shape, dproj.dtype), jax.ShapeDtypeStruct((lp, cw), F32),
                   jax.ShapeDtypeStruct((cw, cw), F32), jax.ShapeDtypeStruct((1, cw), F32),
                   jax.ShapeDtypeStruct((1, cw), F32), jax.ShapeDtypeStruct((1, cw), F32)],
        input_output_aliases={7: 0},
        compiler_params=_params(1, 48))(dy, proj, u1, cln_g, cln_b, pw_w, pw_b, dproj)


def _conv_bwd_dw(du1, proj, dw_w, dproj):
    lp, cw = du1.shape
    tm = _row_tile(lp, 640)
    rb = _row_tile(tm, 64)
    nt = lp // tm
    hb = tm // HALO

    def body(a_ref, b_ref, du_ref, nx_ref, w_ref, dp_in, dp_ref, dww_ref, dwb_ref, ubuf, dbuf, du0, acc):
        del dp_in
        i = pl.program_id(0)

        @pl.when(i == 0)
        def _():
            ubuf[0:HALO, :] = jnp.zeros((HALO, cw), F32)
            acc[...] = jnp.zeros_like(acc)
            dwb_ref[...] = jnp.zeros_like(dwb_ref)

        a = a_ref[...]
        sb = _sigmoid(b_ref[...])
        ubuf[HALO:HALO + tm, :] = a * sb
        du = du_ref[...]
        dbuf[0:tm, :] = du
        dbuf[tm:tm + HALO, :] = jnp.where(i == nt - 1, 0.0, nx_ref[...])
        dwb_ref[...] += _colsum(du)
        _dw_taps(dbuf, w_ref, du0, None, rows=tm, start=0, flip=True, rb=rb)
        d0 = du0[...]
        dp_ref[:, 0:cw] = (d0 * sb).astype(BF16)
        dp_ref[:, cw:2 * cw] = (d0 * a * sb * (1.0 - sb)).astype(BF16)

        lb = min(128, cw)
        groups = _tap_groups(HALO - (CONV_K - 1), False)

        def rb_body(r, carry):
            base = pl.multiple_of(r * rb, rb)
            for cb in range(cw // lb):
                ls = slice(cb * lb, (cb + 1) * lb)
                win = ubuf[pl.ds(base, rb + HALO), ls]
                dv = dbuf[pl.ds(base, rb), ls]
                for s, taps in groups.items():
                    ws = _shift_up(win, s)
                    for a, j in taps:
                        prod = dv * ws[8 * a:8 * a + rb, :]
                        acc[8 * j:8 * j + 8, ls] += jnp.sum(prod.reshape(rb // 8, 8, lb), axis=0)
            return carry

        lax.fori_loop(0, tm // rb, rb_body, 0)
        ubuf[0:HALO, :] = ubuf[tm:tm + HALO, :]

        @pl.when(i == nt - 1)
        def _():
            for j in range(CONV_K):
                dww_ref[j:j + 1, :] = _colsum(acc[8 * j:8 * j + 8, :])
            dww_ref[CONV_K:HALO, :] = jnp.zeros((HALO - CONV_K, cw), F32)

    def col(j):
        return pl.BlockSpec((tm, cw), lambda i: (i, j))

    return pl.pallas_call(
        body, name="conv_bwd_dw", grid=(nt,),
        in_specs=[col(4), col(5), col(0),
                  pl.BlockSpec((HALO, cw), lambda i: (jnp.minimum((i + 1) * hb, nt * hb - 1), 0)),
                  pl.BlockSpec(dw_w.shape, lambda i: (0, 0)),
                  pl.BlockSpec(memory_space=pl.ANY)],
        out_specs=[pl.BlockSpec((tm, 2 * cw), lambda i: (i, 2)),
                   pl.BlockSpec((HALO, cw), lambda i: (0, 0)),
                   pl.BlockSpec((1, cw), lambda i: (0, 0))],
        out_shape=[jax.ShapeDtypeStruct(dproj.shape, dproj.dtype), jax.ShapeDtypeStruct((HALO, cw), F32),
                   jax.ShapeDtypeStruct((1, cw), F32)],
        scratch_shapes=[pltpu.VMEM((tm + HALO, cw), F32), pltpu.VMEM((tm + HALO, cw), F32),
                        pltpu.VMEM((tm, cw), F32), pltpu.VMEM((8 * HALO, cw), F32)],
        input_output_aliases={5: 0},
        compiler_params=_params(1, 56))(proj, proj, du1, du1, dw_w, dproj)


def _dw_in(hn, dproj, ns):
    lp, d = hn.shape
    tm = _row_tile(lp, 1664)
    nt = lp // tm

    mb = 512 if d % 512 == 0 else d

    def body(hn_ref, dp_ref, o_hbm, acc, sem):
        s, i = pl.program_id(0), pl.program_id(1)

        @pl.when(i == 0)
        def _():
            acc[...] = jnp.zeros_like(acc)

        for m in range(d // mb):
            rows = slice(m * mb, (m + 1) * mb)
            acc[rows, :] += _dot_tn(hn_ref[:, rows], dp_ref[...])

        @pl.when(i == nt - 1)
        def _():
            cp = pltpu.make_async_copy(acc, o_hbm.at[s], sem)
            cp.start()
            cp.wait()

    return pl.pallas_call(
        body, name="dw_in", grid=(N_CHIPS, nt),
        in_specs=[pl.BlockSpec((tm, d), lambda s, i: (i, 0)),
                  pl.BlockSpec((tm, ns), lambda s, i: (i, s))],
        out_specs=pl.BlockSpec(memory_space=pl.ANY),
        out_shape=jax.ShapeDtypeStruct((N_CHIPS, d, ns), F32),
        scratch_shapes=[pltpu.VMEM((d, ns), F32), pltpu.SemaphoreType.DMA],
        compiler_params=_params(2, 56))(hn, dproj)


def _in_proj_bwd(dproj, w4, hp, r1, dh2, ln_g, cs):
    lp, d = hp.shape
    ns = w4.shape[2]
    tm = _row_tile(lp, 320)
    nt = lp // tm
    n = len(cs)

    def body(dp_ref, w_ref, h_ref, r_ref, d2_ref, g_ref, *refs):
        cs_refs, (dh_ref, dlg_ref), rb_refs = refs[:n], refs[n:n + 2], refs[n + 2:2 * n + 2]
        send_sems, recv_sems = refs[2 * n + 2:]
        i = pl.program_id(0)
        x, y, c = _mesh_pos()

        def exchange():
            return [pltpu.make_async_remote_copy(
                src_ref=cs_refs[k].at[2 * chip[0] + chip[1]], dst_ref=rb_refs[k].at[j],
                send_sem=send_sems.at[3 * k + j], recv_sem=recv_sems.at[3 * k + j],
                device_id=(*chip, c), device_id_type=MESH)
                for k in range(n) for j, chip in enumerate(_other_chips(x, y))]

        @pl.when(i == 0)
        def _():
            dlg_ref[...] = jnp.zeros_like(dlg_ref)
            for cp in exchange():
                cp.start()

        dhn = _dot_nt(dp_ref[:, 0:ns], w_ref[0])
        for s in range(1, N_CHIPS):
            dhn = dhn + _dot_nt(dp_ref[:, s * ns:(s + 1) * ns], w_ref[s])
        r = r_ref[...]
        hn0 = h_ref[...] * r
        dlg_ref[...] += _colsum(dhn * hn0)
        t = dhn * g_ref[...]
        dh_ref[...] = d2_ref[...] + r * (t - hn0 * _mean(t * hn0))

        @pl.when(i == nt - 1)
        def _():
            for cp in exchange():
                cp.wait()

    def row(cols):
        return pl.BlockSpec((tm, cols), lambda i: (i, 0))

    any_spec = pl.BlockSpec(memory_space=pl.ANY)
    outs = pl.pallas_call(
        body, name="in_proj_bwd", grid=(nt,),
        in_specs=[row(N_CHIPS * ns), pl.BlockSpec(memory_space=pltpu.VMEM),
                  row(d), row(1), row(d), pl.BlockSpec((1, d), lambda i: (0, 0))] + [any_spec] * n,
        out_specs=[row(d), pl.BlockSpec((1, d), lambda i: (0, 0))] + [any_spec] * n,
        out_shape=[jax.ShapeDtypeStruct((lp, d), F32), jax.ShapeDtypeStruct((1, d), F32)]
                  + [jax.ShapeDtypeStruct((3,) + a.shape[1:], a.dtype) for a in cs],
        scratch_shapes=[pltpu.SemaphoreType.DMA((3 * n,)), pltpu.SemaphoreType.DMA((3 * n,))],
        compiler_params=_params(1, 58))(dproj, w4, hp, r1, dh2, ln_g, *cs)
    return outs[0], outs[1], outs[2:]


def _mesh_pos():
    return lax.axis_index("x"), lax.axis_index("y"), lax.axis_index("c")


def _other_chips(x, y):
    return [(1 - x, y), (x, 1 - y), (1 - x, 1 - y)]


def _gather_shards(shards):
    n = len(shards)
    halves = [a.shape[0] // 2 for a in shards]

    def body(*refs):
        ins, outs = refs[:n], refs[n:2 * n]
        send_sems, recv_sems, loc_sems = refs[2 * n:]
        x, y, c = _mesh_pos()
        me, sibling = (x, y, c), (x, y, 1 - c)
        my_s = 2 * x + y
        chips = _other_chips(x, y)

        def half(k, s, cc):
            return outs[k].at[s, pl.ds(cc * halves[k], halves[k])]

        def rcopy(k, j, src, dst, to):
            return pltpu.make_async_remote_copy(
                src_ref=src, dst_ref=dst, send_sem=send_sems.at[6 * k + j], recv_sem=recv_sems.at[6 * k + j],
                device_id=to, device_id_type=MESH)

        local = [pltpu.make_async_copy(ins[k], outs[k].at[my_s], loc_sems.at[k]) for k in range(n)]
        for cp in local:
            cp.start()
        started = []
        for k in range(n):
            for j, chip in enumerate(chips):
                cp = rcopy(k, j, ins[k].at[pl.ds(c * halves[k], halves[k])], half(k, my_s, c), (*chip, c))
                cp.start()
                started.append(cp)
        for j, chip in enumerate(chips):
            s_j = 2 * chip[0] + chip[1]
            for k in range(n):
                rcopy(k, j, half(k, s_j, c), half(k, s_j, c), me).wait_recv()
                cp = rcopy(k, 3 + j, half(k, s_j, c), half(k, s_j, c), sibling)
                cp.start()
                started.append(cp)
        for j, chip in enumerate(chips):
            s_j = 2 * chip[0] + chip[1]
            for k in range(n):
                rcopy(k, 3 + j, half(k, s_j, 1 - c), half(k, s_j, 1 - c), me).wait_recv()
        for cp in started:
            cp.wait_send()
        for cp in local:
            cp.wait()

    return pl.pallas_call(
        body, name="gather_weights",
        in_specs=[pl.BlockSpec(memory_space=pl.ANY)] * n,
        out_specs=[pl.BlockSpec(memory_space=pl.ANY)] * n,
        out_shape=[jax.ShapeDtypeStruct((N_CHIPS,) + a.shape, a.dtype) for a in shards],
        scratch_shapes=[pltpu.SemaphoreType.DMA((6 * n,)), pltpu.SemaphoreType.DMA((6 * n,)),
                        pltpu.SemaphoreType.DMA((n,))],
    )(*shards)


def _rs_pair_exchange(gs):
    n = len(gs)
    halves = [g.shape[1] // 2 for g in gs]

    def body(*refs):
        ins, outs = refs[:n], refs[n:2 * n]
        send_sems, recv_sems = refs[2 * n:]
        x, y, c = _mesh_pos()
        cps = []
        for k in range(n):
            cp = pltpu.make_async_remote_copy(
                src_ref=ins[k].at[:, pl.ds((1 - c) * halves[k], halves[k])], dst_ref=outs[k],
                send_sem=send_sems.at[k], recv_sem=recv_sems.at[k], device_id=(x, y, 1 - c), device_id_type=MESH)
            cp.start()
            cps.append(cp)
        for cp in cps:
            cp.wait()

    return pl.pallas_call(
        body, name="rs_pair_exchange",
        in_specs=[pl.BlockSpec(memory_space=pl.ANY)] * n,
        out_specs=[pl.BlockSpec(memory_space=pl.ANY)] * n,
        out_shape=[jax.ShapeDtypeStruct((N_CHIPS, h) + g.shape[2:], g.dtype) for g, h in zip(gs, halves)],
        scratch_shapes=[pltpu.SemaphoreType.DMA((n,)), pltpu.SemaphoreType.DMA((n,))],
    )(*gs)


def _rs_pair_share(fulls):
    n = len(fulls)
    halves = [f.shape[0] // 2 for f in fulls]

    def body(*refs):
        outs = refs[n:2 * n]
        send_sems, recv_sems = refs[2 * n:]
        x, y, c = _mesh_pos()

        def copy(k, cc, to):
            rows = outs[k].at[pl.ds(cc * halves[k], halves[k])]
            return pltpu.make_async_remote_copy(
                src_ref=rows, dst_ref=rows, send_sem=send_sems.at[k], recv_sem=recv_sems.at[k],
                device_id=to, device_id_type=MESH)

        cps = [copy(k, c, (x, y, 1 - c)) for k in range(n)]
        for cp in cps:
            cp.start()
        for k in range(n):
            copy(k, 1 - c, (x, y, c)).wait_recv()
        for cp in cps:
            cp.wait_send()

    return pl.pallas_call(
        body, name="rs_pair_share",
        in_specs=[pl.BlockSpec(memory_space=pl.ANY)] * n,
        out_specs=[pl.BlockSpec(memory_space=pl.ANY)] * n,
        out_shape=[jax.ShapeDtypeStruct(f.shape, f.dtype) for f in fulls],
        input_output_aliases={k: k for k in range(n)},
        scratch_shapes=[pltpu.SemaphoreType.DMA((n,)), pltpu.SemaphoreType.DMA((n,))],
    )(*fulls)


def _pair_sum(g, recv, pos, name):
    _, rows, cols = g.shape
    h = rows // 2
    tr = _row_tile(h, 256)
    nb = h // tr

    def body(pos_ref, g_ref, r_ref, o_ref, own_ref):
        total = g_ref[0] + r_ref[0]
        o_ref[0] = total.astype(BF16)

        @pl.when(pl.program_id(1) == pos_ref[1])
        def _():
            own_ref[...] = total

    return pl.pallas_call(
        body, name=name,
        grid_spec=pltpu.PrefetchScalarGridSpec(
            num_scalar_prefetch=1, grid=(nb, N_CHIPS),
            in_specs=[pl.BlockSpec((1, tr, cols), lambda r, s, pos_ref: (s, pos_ref[0] * nb + r, 0)),
                      pl.BlockSpec((1, tr, cols), lambda r, s, pos_ref: (s, r, 0))],
            out_specs=[pl.BlockSpec((1, tr, cols), lambda r, s, pos_ref: (s, r, 0)),
                       pl.BlockSpec((tr, cols), lambda r, s, pos_ref: (r, 0))]),
        out_shape=[jax.ShapeDtypeStruct((N_CHIPS, h, cols), BF16), jax.ShapeDtypeStruct((h, cols), F32)],
        compiler_params=_params(2, 32))(pos, g, recv)


def _chip_sum(own, rb, pos, name):
    h, cols = own.shape
    tr = _row_tile(h, 256)
    nb = h // tr

    def body(pos_ref, c_ref, r_ref, o_ref):
        del pos_ref
        o_ref[...] = ((c_ref[...] + r_ref[0].astype(F32)) + r_ref[1].astype(F32)) + r_ref[2].astype(F32)

    return pl.pallas_call(
        body, name=name,
        grid_spec=pltpu.PrefetchScalarGridSpec(
            num_scalar_prefetch=1, grid=(nb,),
            in_specs=[pl.BlockSpec((tr, cols), lambda r, pos_ref: (r, 0)),
                      pl.BlockSpec((3, tr, cols), lambda r, pos_ref: (0, r, 0))],
            out_specs=pl.BlockSpec((tr, cols), lambda r, pos_ref: (pos_ref[0] * nb + r, 0))),
        out_shape=jax.ShapeDtypeStruct((2 * h, cols), F32),
        compiler_params=_params(1, 32))(pos, own, rb)


def _adamw_big(w, g, m, v, name):
    rows, cols = w.shape
    tr = _row_tile(rows, 256)

    def body(w_ref, g_ref, m_ref, v_ref, d_ref, nm_ref, nv_ref):
        d_ref[...], nm_ref[...], nv_ref[...] = _adamw(w_ref[...], g_ref[...], m_ref[...], v_ref[...])

    spec = pl.BlockSpec((tr, cols), lambda i: (i, 0))
    return pl.pallas_call(
        body, name=name, grid=(rows // tr,),
        in_specs=[spec] * 4, out_specs=[spec] * 3,
        out_shape=[jax.ShapeDtypeStruct((rows, cols), F32)] * 3,
        compiler_params=_params(1, 48))(w, g, m, v)


def _gather_small(loss, dfg, dlg, dgn, ddwb, dclg, dclb, dpwb, ddww, dmeta):
    d = loss.shape[1]
    w = dgn.shape[1]

    def body(loss_ref, dfg_ref, dlg_ref, dgn_ref, ddwb_ref, dclg_ref, dclb_ref, dpwb_ref, ddww_ref, dmeta_ref,
             gs_ref, gd_ref, gm_ref, send_sems, recv_sems):
        x, y, c = _mesh_pos()
        me = 4 * x + 2 * y + c
        gs_ref[me, 0:1, :] = loss_ref[...]
        gs_ref[me, 1:2, :] = dfg_ref[...]
        gs_ref[me, 2:3, :] = dlg_ref[...]
        gs_ref[me, 3:4, 0:w] = dgn_ref[...]
        gs_ref[me, 3:4, w:2 * w] = ddwb_ref[...]
        gs_ref[me, 4:5, 0:w] = dclg_ref[...]
        gs_ref[me, 4:5, w:2 * w] = dclb_ref[...]
        gs_ref[me, 5:6, 0:w] = dpwb_ref[...]
        gs_ref[me, 5:6, w:2 * w] = jnp.zeros((1, d - w), F32)
        gs_ref[me, 6:8, :] = jnp.zeros((2, d), F32)
        gd_ref[me] = ddww_ref[...]
        gm_ref[me] = dmeta_ref[...]
        bufs = (gs_ref, gd_ref, gm_ref)

        def peer(j):
            return (1 - x if j & 4 else x), (1 - y if j & 2 else y), (1 - c if j & 1 else c)

        def copy(k, j, slot, to):
            return pltpu.make_async_remote_copy(
                src_ref=bufs[k].at[slot], dst_ref=bufs[k].at[slot],
                send_sem=send_sems.at[7 * k + j - 1], recv_sem=recv_sems.at[7 * k + j - 1],
                device_id=to, device_id_type=MESH)

        cps = []
        for k in range(3):
            for j in range(1, N_DEV):
                cp = copy(k, j, me, peer(j))
                cp.start()
                cps.append(cp)
        for k in range(3):
            for j in range(1, N_DEV):
                px, py, pc = peer(j)
                copy(k, j, 4 * px + 2 * py + pc, (x, y, c)).wait_recv()
        for cp in cps:
            cp.wait_send()

    vm = pl.BlockSpec(memory_space=pltpu.VMEM)
    return pl.pallas_call(
        body, name="gather_small",
        in_specs=[vm] * 10, out_specs=[vm] * 3,
        out_shape=[jax.ShapeDtypeStruct((N_DEV, 8, d), F32),
                   jax.ShapeDtypeStruct((N_DEV,) + ddww.shape, F32),
                   jax.ShapeDtypeStruct((N_DEV,) + dmeta.shape, F32)],
        scratch_shapes=[pltpu.SemaphoreType.DMA((21,)), pltpu.SemaphoreType.DMA((21,))],
    )(loss, dfg, dlg, dgn, ddwb, dclg, dclb, dpwb, ddww, dmeta)


def _small_update(s_arr, gs, gd, gm, weights, ms, vs):
    d = gs.shape[2]
    w = d // 2
    n = len(weights)

    def body(s_ref, gs_ref, gd_ref, gm_ref, *refs):
        del s_ref
        w_refs, m_refs, v_refs = refs[:n], refs[n:2 * n], refs[2 * n:3 * n]
        loss_ref = refs[3 * n]
        g_refs = refs[3 * n + 1:4 * n + 1]
        d_refs = refs[4 * n + 1:5 * n + 1]
        nm_refs = refs[5 * n + 1:6 * n + 1]
        nv_refs = refs[6 * n + 1:7 * n + 1]

        def total(ref):
            t = ref[0]
            for dev in range(1, N_DEV):
                t = t + ref[dev]
            return t

        packed = total(gs_ref)
        loss_ref[...] = jnp.sum(packed[0:1, :], axis=1, keepdims=True) * (0.5 / d)
        grads = [packed[2:3, :], packed[1:2, :], packed[3:4, 0:w], packed[3:4, w:2 * w], packed[4:5, 0:w],
                 packed[4:5, w:2 * w], packed[5:6, 0:w], total(gd_ref), total(gm_ref)]
        for k in range(n):
            g = grads[k]
            g_refs[k][...] = g
            d_refs[k][...], nm_refs[k][...], nv_refs[k][...] = _adamw(w_refs[k][...], g, m_refs[k][...], v_refs[k][...])

    def whole(shape):
        return pl.BlockSpec(shape, lambda i, s_ref: (0,) * len(shape))

    wc = weights[7].shape[1]
    mc = weights[8].shape[1]
    shapes = [a.shape for a in weights]
    in_specs = ([whole(gs.shape),
                 pl.BlockSpec((N_DEV, gd.shape[1], wc), lambda i, s_ref: (0, 0, s_ref[0])),
                 pl.BlockSpec((N_DEV, gm.shape[1], mc), lambda i, s_ref: (0, 0, s_ref[0]))]
                + [whole(s) for s in shapes] * 3)
    out_specs = [whole((1, 1))] + [whole(s) for s in shapes] * 4
    out_shape = [jax.ShapeDtypeStruct((1, 1), F32)] + [jax.ShapeDtypeStruct(s, F32) for s in shapes] * 4
    outs = pl.pallas_call(
        body, name="small_update",
        grid_spec=pltpu.PrefetchScalarGridSpec(num_scalar_prefetch=1, grid=(1,), in_specs=in_specs,
                                               out_specs=out_specs),
        out_shape=out_shape,
        compiler_params=_params(1, 32))(s_arr, gs, gd, gm, *weights, *ms, *vs)
    loss = outs[0]
    return loss, outs[1:n + 1], outs[n + 1:2 * n + 1], outs[2 * n + 1:3 * n + 1], outs[3 * n + 1:4 * n + 1]


def kernel(x, meta_tokens, ln_g, w_in, ret_gn_g, conv_dw_w, conv_dw_b, conv_ln_g, conv_ln_b, conv_pw_w, conv_pw_b, w_out, final_g, loss_target, m_meta_tokens, m_ln_g, m_w_in, m_ret_gn_g, m_conv_dw_w, m_conv_dw_b, m_conv_ln_g, m_conv_ln_b, m_conv_pw_w, m_conv_pw_b, m_w_out, m_final_g, v_meta_tokens, v_ln_g, v_w_in, v_ret_gn_g, v_conv_dw_w, v_conv_dw_b, v_conv_ln_g, v_conv_ln_b, v_conv_pw_w, v_conv_pw_b, v_w_out, v_final_g):
    seq, d = x.shape[1], x.shape[2]
    w = ret_gn_g.shape[1]
    hd = w // RET_HEADS
    lp = CHUNK + seq
    ns = w_in.shape[2]
    mx, my, mc = lax.axis_index("x"), lax.axis_index("y"), lax.axis_index("c")
    my_s = 2 * mx + my
    s_arr = my_s.astype(jnp.int32).reshape(1)
    pos = jnp.stack([mc, my_s]).astype(jnp.int32)
    order = jnp.stack([my_s, 2 * (1 - mx) + my, 2 * mx + (1 - my), 2 * (1 - mx) + (1 - my)]).astype(jnp.int32)

    dw_pad = jnp.pad(conv_dw_w[0], ((0, HALO - CONV_K), (0, 0)))
    dw4, meta4 = _gather_shards([dw_pad, meta_tokens])
    dw_full = dw4.transpose(1, 0, 2).reshape(HALO, w)
    meta_full = meta4.transpose(1, 0, 2).reshape(N_META, d)

    hp = jnp.concatenate([jnp.zeros((LEAD, d), F32), meta_full, x[0]], axis=0)
    consts = _ret_consts()
    cos, sin = _rope_tables(lp, hd // 2)
    fg2 = final_g.reshape(1, d)

    proj, r1, hn, w4, pw4, wo4 = _in_proj_gather(
        order, hp, ln_g, _cast_into_gathered(w_in[0], s_arr, "cast_w_in"),
        _cast_into_gathered(conv_pw_w[0], s_arr, "cast_pw_w"), _cast_into_gathered(w_out[0], s_arr, "cast_w_out"))
    pw_full = pw4.reshape(w, w)
    wo_full = wo4.reshape(2 * w, d)
    y_ret, ssave = _ret_fwd(proj, cos, sin, ret_gn_g, consts)
    y_conv, u1 = _conv_fwd(proj, dw_full, conv_dw_b, conv_ln_g, conv_ln_b, pw_full, conv_pw_b)
    dh2, dy, loss_l, dfg = _out_proj_loss(y_ret, y_conv, hp, loss_target[0], wo_full, fg2)

    g_wo = _dw_out(y_ret, y_conv, dh2)
    dproj, dgn = _ret_bwd(proj, dy, ssave, cos, sin, ret_gn_g, consts)
    dproj, du1, g_pw, dpwb, dclg, dclb = _conv_bwd_pw(dy, proj, u1, conv_ln_g, conv_ln_b, pw_full, conv_pw_b, dproj)
    dproj, ddww, ddwb = _conv_bwd_dw(du1, proj, dw_full, dproj)
    g_win = _dw_in(hn, dproj, ns)

    gs = [g_win, g_wo.reshape(N_CHIPS, (2 * w) // N_CHIPS, d), g_pw.reshape(N_CHIPS, w // N_CHIPS, w)]
    recv = _rs_pair_exchange(gs)
    names = ("w_in", "w_out", "pw_w")
    sums = [_pair_sum(g, r, pos, "pair_sum_" + nm) for g, r, nm in zip(gs, recv, names)]
    dh, dlg, rb = _in_proj_bwd(dproj, w4, hp, r1, dh2, ln_g, [cs_ for cs_, _ in sums])
    grad_x = dh[CHUNK:][None]
    dmeta = dh[LEAD:CHUNK]
    fulls = [_chip_sum(own, r, pos, "chip_sum_" + nm) for (_, own), r, nm in zip(sums, rb, names)]
    grad_w_in, grad_w_out, grad_pw = _rs_pair_share(fulls)
    d_win, nm_win, nv_win = _adamw_big(w_in[0], grad_w_in, m_w_in[0], v_w_in[0], "adamw_w_in")
    d_wo, nm_wo, nv_wo = _adamw_big(w_out[0], grad_w_out, m_w_out[0], v_w_out[0], "adamw_w_out")
    d_pw, nm_pw, nv_pw = _adamw_big(conv_pw_w[0], grad_pw, m_conv_pw_w[0], v_conv_pw_w[0], "adamw_pw_w")

    gsm, gdm, gmm = _gather_small(loss_l, dfg, dlg, dgn, ddwb, dclg, dclb, dpwb, ddww, dmeta)

    def pad_dw(a):
        return jnp.pad(a[0], ((0, HALO - CONV_K), (0, 0)))

    small_w = [ln_g, fg2, ret_gn_g, conv_dw_b, conv_ln_g, conv_ln_b, conv_pw_b, dw_pad, meta_tokens]
    small_m = [m_ln_g, m_final_g.reshape(1, d), m_ret_gn_g, m_conv_dw_b, m_conv_ln_g, m_conv_ln_b, m_conv_pw_b,
               pad_dw(m_conv_dw_w), m_meta_tokens]
    small_v = [v_ln_g, v_final_g.reshape(1, d), v_ret_gn_g, v_conv_dw_b, v_conv_ln_g, v_conv_ln_b, v_conv_pw_b,
               pad_dw(v_conv_dw_w), v_meta_tokens]
    loss, sg, sd, snm, snv = _small_update(s_arr, gsm, gdm, gmm, small_w, small_m, small_v)

    def assemble(small, big_in, big_pw, big_out):
        ln, fg, gn, dwb, clg, clb, pwb, dww, meta = small
        return (meta, ln, big_in[None], gn, dww[:CONV_K][None], dwb, clg, clb, big_pw[None], pwb, big_out[None],
                fg.reshape(d))

    return (loss.reshape(()), grad_x,
            *assemble(sg, grad_w_in, grad_pw, grad_w_out),
            *assemble(sd, d_win, d_pw, d_wo),
            *assemble(snm, nm_win, nm_pw, nm_wo),
            *assemble(snv, nv_win, nv_pw, nv_wo))
```

```python
import functools

import jax
import jax.numpy as jnp
from jax import lax
from jax.experimental import pallas as pl
from jax.experimental.pallas import tpu as pltpu

F32 = jnp.float32
BF16 = jnp.bfloat16
MESH = pl.DeviceIdType.MESH

N_META = 16
CHUNK = 128
LEAD = (-N_META) % CHUNK
RET_HEADS = 4
CONV_K = 31
HALO = 32
ROPE_BASE = 10000.0
EPS = 1e-6
N_CHIPS = 4
N_DEV = 8
SEND_PIECES = 8

ADAM_LR = 0.001
ADAM_B1 = 0.9
ADAM_B2 = 0.999
ADAM_EPS = 1e-08
ADAM_WD = 0.01
ADAM_STEP = 10

MIB = 2 ** 20


def _params(n_grid_axes, vmem_mib):
    return pltpu.CompilerParams(dimension_semantics=("arbitrary",) * n_grid_axes,
                                vmem_limit_bytes=vmem_mib * MIB)


def _row_tile(n, pref):
    for t in (1664, 1280, 1024, 640, 512, 384, 320, 256, 128, 64, 32, 16, 8):
        if t <= pref and n % t == 0:
            return t
    raise ValueError(f"no row tile for {n}")


def _dot(a, b):
    return jnp.dot(a, b, preferred_element_type=F32)


def _dot_nt(a, b):
    return lax.dot_general(a, b, (((1,), (1,)), ((), ())), preferred_element_type=F32)


def _dot_tn(a, b):
    return lax.dot_general(a, b, (((0,), (0,)), ((), ())), preferred_element_type=F32)


def _sigmoid(x):
    return jax.nn.sigmoid(x)


def _dsilu(x, s):
    return s * (1.0 + x * (1.0 - s))


def _mean(x):
    return jnp.mean(x, axis=-1, keepdims=True)


def _colsum(x):
    return jnp.sum(x, axis=0, keepdims=True)


def _rope(x, cos, sin):
    half = x.shape[-1] // 2
    x1, x2 = x[:, :half], x[:, half:]
    return jnp.concatenate([x1 * cos - x2 * sin, x1 * sin + x2 * cos], axis=-1)


def _rope_t(d, cos, sin):
    half = d.shape[-1] // 2
    d1, d2 = d[:, :half], d[:, half:]
    return jnp.concatenate([d1 * cos + d2 * sin, d2 * cos - d1 * sin], axis=-1)


def _adamw(w, g, m, v):
    m = ADAM_B1 * m + (1.0 - ADAM_B1) * g
    v = ADAM_B2 * v + (1.0 - ADAM_B2) * (g * g)
    m_hat = m / (1.0 - ADAM_B1 ** ADAM_STEP)
    v_hat = v / (1.0 - ADAM_B2 ** ADAM_STEP)
    delta = -ADAM_LR * (m_hat / (jnp.sqrt(v_hat) + ADAM_EPS) + ADAM_WD * w)
    return delta, m, v


def _ret_consts():
    h = jnp.arange(RET_HEADS, dtype=F32)
    log_g = jnp.log(1.0 - jnp.exp2(-5.0 - h))
    idx = jnp.arange(CHUNK, dtype=F32)
    rel = idx[:, None] - idx[None, :]
    dmask = jnp.where(rel[None] >= 0, jnp.exp(jnp.maximum(rel, 0.0)[None] * log_g[:, None, None]), 0.0)
    qd = jnp.exp((idx[None, :] + 1.0) * log_g[:, None])[:, :, None]
    kd = jnp.exp((CHUNK - 1.0 - idx[None, :]) * log_g[:, None])[:, :, None]
    cd = jnp.exp(CHUNK * log_g)[:, None, None]
    return dmask, qd, kd, cd


def _rope_tables(n_rows, half):
    pos = jnp.arange(n_rows, dtype=F32) - float(LEAD)
    inv_freq = ROPE_BASE ** (-jnp.arange(half, dtype=F32) / half)
    ang = pos[:, None] * inv_freq[None, :]
    return jnp.cos(ang), jnp.sin(ang)


def _cast_into_gathered(a, s_arr, name):
    rows, cols = a.shape
    tr = _row_tile(rows, 256)

    def body(s_ref, a_ref, o_ref):
        del s_ref
        o_ref[0] = a_ref[...].astype(BF16)

    return pl.pallas_call(
        body, name=name,
        grid_spec=pltpu.PrefetchScalarGridSpec(
            num_scalar_prefetch=1, grid=(rows // tr,),
            in_specs=[pl.BlockSpec((tr, cols), lambda i, s_ref: (i, 0))],
            out_specs=pl.BlockSpec((1, tr, cols), lambda i, s_ref: (s_ref[0], i, 0))),
        out_shape=jax.ShapeDtypeStruct((N_CHIPS, rows, cols), BF16),
        compiler_params=_params(1, 32))(s_arr, a)


def _in_proj_gather(order, hp, ln_g, w4, pw4, wo4):
    lp, d = hp.shape
    ns = w4.shape[2]
    tm = _row_tile(lp, 640)
    nt = lp // tm
    assert nt >= 2, "the hn write-back of a row tile is waited for one step later, before any pass re-reads it"
    gathered = (w4, pw4, wo4)
    halves = [a.shape[1] // 2 for a in gathered]
    n = len(gathered)

    def body(order_ref, h_ref, g_ref, w_in, pw_in, wo_in, proj_ref, r_ref, hn_hbm, w_out, pw_out, wo_out,
             wbuf, hnbuf, send_sems, recv_sems, hn_out_sems, hn_in_sems, w_sem):
        del order_ref, w_in, pw_in, wo_in
        t, i = pl.program_id(0), pl.program_id(1)
        slot = (t * nt + i) % 2
        x, y, c = _mesh_pos()
        me, sibling = (x, y, c), (x, y, 1 - c)
        my_s = 2 * x + y
        chips = _other_chips(x, y)
        outs = (w_out, pw_out, wo_out)

        def half(k, s, cc):
            return outs[k].at[s, pl.ds(cc * halves[k], halves[k])]

        def rcopy(k, j, rows, to):
            return pltpu.make_async_remote_copy(
                src_ref=rows, dst_ref=rows, send_sem=send_sems.at[6 * k + j], recv_sem=recv_sems.at[6 * k + j],
                device_id=to, device_id_type=MESH)

        def send(k, j):
            return rcopy(k, j, half(k, my_s, c), (*chips[j], c))

        def send_in_pieces(k, j):
            rows = halves[k] // SEND_PIECES
            for q in range(SEND_PIECES):
                piece = outs[k].at[my_s, pl.ds(c * halves[k] + q * rows, rows)]
                rcopy(k, j, piece, (*chips[j], c)).start()

        def shard_of(j):
            return 2 * chips[j][0] + chips[j][1]

        def forward(k, j):
            return rcopy(k, 3 + j, half(k, shard_of(j), c), sibling)

        def land(k, j):
            rcopy(k, j, half(k, shard_of(j), c), me).wait_recv()
            forward(k, j).start()

        def landed_from_sibling(k, j):
            rcopy(k, 3 + j, half(k, shard_of(j), 1 - c), me).wait_recv()

        def load_w(s):
            cp = pltpu.make_async_copy(w_out.at[s], wbuf, w_sem)
            cp.start()
            cp.wait()

        def hn_out(sl, row_tile):
            return pltpu.make_async_copy(hnbuf.at[sl], hn_hbm.at[pl.ds(row_tile * tm, tm)], hn_out_sems.at[sl])

        def hn_in(sl, row_tile):
            return pltpu.make_async_copy(hn_hbm.at[pl.ds(row_tile * tm, tm)], hnbuf.at[sl], hn_in_sems.at[sl])

        @pl.when((t == 0) & (i == 0))
        def _():
            for j in range(2):
                send_in_pieces(0, j)
            load_w(my_s)

        for j in range(3):
            @pl.when((t == j + 1) & (i == 0))
            def _(j=j):
                if j == 0:
                    send_in_pieces(0, 2)
                    for k in range(1, n):
                        for jj in range(3):
                            send(k, jj).start()
                land(0, j)
                landed_from_sibling(0, j)
                load_w(shard_of(j))

        @pl.when(t == 0)
        def _():
            h = h_ref[...]
            r = lax.rsqrt(_mean(h * h) + EPS)
            hnbuf[slot] = ((h * r) * g_ref[...]).astype(BF16)
            r_ref[...] = r
            hn_out(slot, i).start()

        @pl.when(t > 0)
        def _():
            hn_in(slot, i).wait()

        @pl.when(((t == 0) & (i > 0)) | ((t == 1) & (i == 0)))
        def _():
            hn_out(1 - slot, jnp.where(i > 0, i - 1, nt - 1)).wait()

        last = (t == N_CHIPS - 1) & (i == nt - 1)

        @pl.when(((t > 0) | (i == nt - 1)) & jnp.logical_not(last))
        def _():
            hn_in(1 - slot, jnp.where(i == nt - 1, 0, i + 1)).start()

        proj_ref[...] = _dot(hnbuf[slot], wbuf[...])

        @pl.when(last)
        def _():
            for k in range(1, n):
                for j in range(3):
                    land(k, j)
            for k in range(1, n):
                for j in range(3):
                    landed_from_sibling(k, j)
            for k in range(n):
                for j in range(3):
                    send(k, j).wait_send()
                    forward(k, j).wait_send()

    def frozen(t, i):
        return jnp.where(t == 0, i, nt - 1)

    any_spec = pl.BlockSpec(memory_space=pl.ANY)
    return pl.pallas_call(
        body, name="in_proj_gather",
        grid_spec=pltpu.PrefetchScalarGridSpec(
            num_scalar_prefetch=1, grid=(N_CHIPS, nt),
            in_specs=[pl.BlockSpec((tm, d), lambda t, i, o: (frozen(t, i), 0)),
                      pl.BlockSpec((1, d), lambda t, i, o: (0, 0)),
                      any_spec, any_spec, any_spec],
            out_specs=[pl.BlockSpec((tm, ns), lambda t, i, o: (i, o[t])),
                       pl.BlockSpec((tm, 1), lambda t, i, o: (frozen(t, i), 0)),
                       any_spec, any_spec, any_spec, any_spec],
            scratch_shapes=[pltpu.VMEM((d, ns), BF16), pltpu.VMEM((2, tm, d), BF16),
                            pltpu.SemaphoreType.DMA((6 * n,)), pltpu.SemaphoreType.DMA((6 * n,)),
                            pltpu.SemaphoreType.DMA((2,)), pltpu.SemaphoreType.DMA((2,)),
                            pltpu.SemaphoreType.DMA]),
        out_shape=[jax.ShapeDtypeStruct((lp, N_CHIPS * ns), F32),
                   jax.ShapeDtypeStruct((lp, 1), F32),
                   jax.ShapeDtypeStruct((lp, d), BF16)]
                  + [jax.ShapeDtypeStruct(a.shape, a.dtype) for a in gathered],
        input_output_aliases={3: 3, 4: 4, 5: 5},
        compiler_params=_params(2, 48))(order, hp, ln_g, w4, pw4, wo4)


def _ret_fwd(proj, cos, sin, gn_g, consts):
    lp = proj.shape[0]
    w = gn_g.shape[1]
    hd = w // RET_HEADS
    nch = lp // CHUNK
    dmask, qd, kd, cd = consts

    def body(q_ref, k_ref, v_ref, g_ref, cos_ref, sin_ref, gn_ref, m_ref, qd_ref, kd_ref, cd_ref,
             y_ref, ssave_ref, s_scr):
        @pl.when(pl.program_id(0) == 0)
        def _():
            s_scr[...] = jnp.zeros_like(s_scr)

        cos_t, sin_t = cos_ref[...], sin_ref[...]
        for h in range(RET_HEADS):
            sl = slice(h * hd, (h + 1) * hd)
            qr = _rope(q_ref[:, sl], cos_t, sin_t)
            kr = _rope(k_ref[:, sl], cos_t, sin_t) * (hd ** -0.5)
            vb = v_ref[:, sl].astype(BF16)
            sc = _dot_nt(qr.astype(BF16), kr.astype(BF16)) * m_ref[h]
            state = s_scr[h]
            sb = state.astype(BF16)
            ssave_ref[0, h] = sb
            out = _dot(sc.astype(BF16), vb) + _dot((qr * qd_ref[h]).astype(BF16), sb)
            s_scr[h] = state * cd_ref[h] + _dot_tn((kr * kd_ref[h]).astype(BF16), vb)
            dev = out - _mean(out)
            yn = dev * lax.rsqrt(_mean(dev * dev) + EPS)
            g = g_ref[:, sl]
            y_ref[:, sl] = ((yn * gn_ref[:, sl]) * (g * _sigmoid(g))).astype(BF16)

    def col(j):
        return pl.BlockSpec((CHUNK, w), lambda i: (i, j))

    def whole(a):
        return pl.BlockSpec(a.shape, lambda i: (0,) * a.ndim)

    return pl.pallas_call(
        body, name="ret_fwd", grid=(nch,),
        in_specs=[col(0), col(1), col(2), col(3),
                  pl.BlockSpec((CHUNK, hd // 2), lambda i: (i, 0)),
                  pl.BlockSpec((CHUNK, hd // 2), lambda i: (i, 0)),
                  whole(gn_g), whole(dmask), whole(qd), whole(kd), whole(cd)],
        out_specs=[pl.BlockSpec((CHUNK, w), lambda i: (i, 0)),
                   pl.BlockSpec((1, RET_HEADS, hd, hd), lambda i: (i, 0, 0, 0))],
        out_shape=[jax.ShapeDtypeStruct((lp, w), BF16),
                   jax.ShapeDtypeStruct((nch, RET_HEADS, hd, hd), BF16)],
        scratch_shapes=[pltpu.VMEM((RET_HEADS, hd, hd), F32)],
        compiler_params=_params(1, 32))(proj, proj, proj, proj, cos, sin, gn_g, dmask, qd, kd, cd)


def _tap_groups(start, flip):
    groups = {}
    for j in range(CONV_K):
        o = start + (CONV_K - 1 - j if flip else j)
        groups.setdefault(o % 8, []).append((o // 8, j))
    return groups


def _shift_up(win, s):
    return win if s == 0 else pltpu.roll(win, win.shape[0] - s, axis=0)


def _dw_taps(src_ref, w_ref, dst_ref, bias, *, rows, start, flip, rb):
    cw = dst_ref.shape[1]
    lb = min(128, cw)
    groups = _tap_groups(start, flip)

    def rb_body(r, carry):
        base = pl.multiple_of(r * rb, rb)
        for cb in range(cw // lb):
            ls = slice(cb * lb, (cb + 1) * lb)
            win = src_ref[pl.ds(base, rb + HALO), ls]
            acc = jnp.zeros((rb, lb), F32) if bias is None else jnp.broadcast_to(bias[:, ls], (rb, lb))
            for s, taps in groups.items():
                ws = _shift_up(win, s)
                for a, j in taps:
                    acc = acc + ws[8 * a:8 * a + rb, :] * w_ref[j:j + 1, ls]
            dst_ref[pl.ds(base, rb), ls] = acc
        return carry

    lax.fori_loop(0, rows // rb, rb_body, 0)


def _conv_fwd(proj, dw_w, dw_b, cln_g, cln_b, pw_w, pw_b):
    lp = proj.shape[0]
    cw = dw_b.shape[1]
    tm = _row_tile(lp, 640)
    rb = _row_tile(tm, 64)

    def body(a_ref, b_ref, gc_ref, w_ref, wb_ref, lg_ref, lb_ref, pw_ref, pb_ref, y_ref, u1_ref, buf):
        @pl.when(pl.program_id(0) == 0)
        def _():
            buf[0:HALO, :] = jnp.zeros((HALO, cw), F32)

        buf[HALO:HALO + tm, :] = a_ref[...] * _sigmoid(b_ref[...])
        _dw_taps(buf, w_ref, u1_ref, wb_ref[...], rows=tm, start=HALO - (CONV_K - 1), flip=False, rb=rb)
        buf[0:HALO, :] = buf[tm:tm + HALO, :]
        u1 = u1_ref[...]
        dev = u1 - _mean(u1)
        z = dev * lax.rsqrt(_mean(dev * dev) + EPS) * lg_ref[...] + lb_ref[...]
        u3 = (z * _sigmoid(z)).astype(BF16)
        u4 = _dot(u3, pw_ref[...]) + pb_ref[...]
        gc = gc_ref[...]
        y_ref[...] = (u4 * (gc * _sigmoid(gc))).astype(BF16)

    def col(j):
        return pl.BlockSpec((tm, cw), lambda i: (i, j))

    def whole(a):
        return pl.BlockSpec(a.shape, lambda i: (0,) * a.ndim)

    return pl.pallas_call(
        body, name="conv_fwd", grid=(lp // tm,),
        in_specs=[col(4), col(5), col(6), whole(dw_w), whole(dw_b), whole(cln_g), whole(cln_b),
                  whole(pw_w), whole(pw_b)],
        out_specs=[pl.BlockSpec((tm, cw), lambda i: (i, 0)), pl.BlockSpec((tm, cw), lambda i: (i, 0))],
        out_shape=[jax.ShapeDtypeStruct((lp, cw), BF16), jax.ShapeDtypeStruct((lp, cw), F32)],
        scratch_shapes=[pltpu.VMEM((tm + HALO, cw), F32)],
        compiler_params=_params(1, 48))(proj, proj, proj, dw_w, dw_b, cln_g, cln_b, pw_w, pw_b)


def _out_proj_loss(yr, yc, hp, tgt, w_out, final_g):
    lp, d = hp.shape
    w = yr.shape[1]
    tm = _row_tile(lp, 320)
    nt = lp // tm
    assert tm > CHUNK and nt >= 2

    def body(yr_ref, yc_ref, h_ref, t_hbm, w_ref, fg_ref, dh2_ref, dy_ref, loss_ref, dfg_ref, tbuf, tsems):
        i = pl.program_id(0)
        slot = i % 2

        def first_fetch():
            return pltpu.make_async_copy(t_hbm.at[pl.ds(0, tm - CHUNK)], tbuf.at[0, pl.ds(CHUNK, tm - CHUNK)],
                                         tsems.at[0])

        def fetch(sl, tile):
            return pltpu.make_async_copy(t_hbm.at[pl.ds(tile * tm - CHUNK, tm)], tbuf.at[sl], tsems.at[sl])

        @pl.when(i == 0)
        def _():
            loss_ref[...] = jnp.zeros_like(loss_ref)
            dfg_ref[...] = jnp.zeros_like(dfg_ref)
            tbuf[0, 0:CHUNK, :] = jnp.zeros((CHUNK, d), F32)
            first_fetch().start()

        @pl.when(i + 1 < nt)
        def _():
            fetch(1 - slot, i + 1).start()

        @pl.when(i == 0)
        def _():
            first_fetch().wait()

        @pl.when(i > 0)
        def _():
            fetch(slot, i).wait()

        h2 = h_ref[...] + (_dot(yr_ref[...], w_ref[0:w, :]) + _dot(yc_ref[...], w_ref[w:2 * w, :]))
        r2 = lax.rsqrt(_mean(h2 * h2) + EPS)
        h2n = h2 * r2
        fg = fg_ref[...]
        rows = i * tm + lax.broadcasted_iota(jnp.int32, (tm, 1), 0)
        err = jnp.where(rows >= CHUNK, h2n * fg - tbuf[slot], 0.0)
        loss_ref[...] += _colsum(err * err)
        dout = err * (1.0 / d)
        dfg_ref[...] += _colsum(dout * h2n)
        dz = dout * fg
        dh2 = r2 * (dz - h2n * _mean(dz * h2n))
        dh2_ref[...] = dh2
        db = dh2.astype(BF16)
        dy_ref[:, 0:w] = _dot_nt(db, w_ref[0:w, :])
        dy_ref[:, w:2 * w] = _dot_nt(db, w_ref[w:2 * w, :])

    def row(cols):
        return pl.BlockSpec((tm, cols), lambda i: (i, 0))

    return pl.pallas_call(
        body, name="out_proj_loss", grid=(nt,),
        in_specs=[row(w), row(w), row(d), pl.BlockSpec(memory_space=pl.ANY),
                  pl.BlockSpec(memory_space=pltpu.VMEM),
                  pl.BlockSpec((1, d), lambda i: (0, 0))],
        out_specs=[row(d), row(2 * w), pl.BlockSpec((1, d), lambda i: (0, 0)),
                   pl.BlockSpec((1, d), lambda i: (0, 0))],
        out_shape=[jax.ShapeDtypeStruct((lp, d), F32), jax.ShapeDtypeStruct((lp, 2 * w), F32),
                   jax.ShapeDtypeStruct((1, d), F32), jax.ShapeDtypeStruct((1, d), F32)],
        scratch_shapes=[pltpu.VMEM((2, tm, d), F32), pltpu.SemaphoreType.DMA((2,))],
        compiler_params=_params(1, 56))(yr, yc, hp, tgt, w_out, final_g)


def _dw_out(yr, yc, dh2):
    lp, d = dh2.shape
    w = yr.shape[1]
    tm = _row_tile(lp, 1664)
    nb = 2
    dn = d // nb

    def body(yr_ref, yc_ref, d_ref, o_ref):
        @pl.when(pl.program_id(1) == 0)
        def _():
            o_ref[...] = jnp.zeros_like(o_ref)

        db = d_ref[...].astype(BF16)
        o_ref[0:w, :] += _dot_tn(yr_ref[...], db)
        o_ref[w:2 * w, :] += _dot_tn(yc_ref[...], db)

    return pl.pallas_call(
        body, name="dw_out", grid=(nb, lp // tm),
        in_specs=[pl.BlockSpec((tm, w), lambda n, i: (i, 0)),
                  pl.BlockSpec((tm, w), lambda n, i: (i, 0)),
                  pl.BlockSpec((tm, dn), lambda n, i: (i, n))],
        out_specs=pl.BlockSpec((2 * w, dn), lambda n, i: (0, n)),
        out_shape=jax.ShapeDtypeStruct((2 * w, d), F32),
        compiler_params=_params(2, 52))(yr, yc, dh2)


def _ret_bwd(proj, dy, ssave, cos, sin, gn_g, consts):
    lp = proj.shape[0]
    w = gn_g.shape[1]
    hd = w // RET_HEADS
    nch = lp // CHUNK
    dmask, qd, kd, cd = consts

    def body(q_ref, k_ref, v_ref, g_ref, dy_ref, ss_ref, cos_ref, sin_ref, gn_ref, m_ref, qd_ref, kd_ref,
             cd_ref, dp_ref, dgn_ref, ds_scr):
        @pl.when(pl.program_id(0) == 0)
        def _():
            ds_scr[...] = jnp.zeros_like(ds_scr)
            dgn_ref[...] = jnp.zeros_like(dgn_ref)

        cos_t, sin_t = cos_ref[...], sin_ref[...]
        for h in range(RET_HEADS):
            sl = slice(h * hd, (h + 1) * hd)
            qr = _rope(q_ref[:, sl], cos_t, sin_t)
            kr = _rope(k_ref[:, sl], cos_t, sin_t) * (hd ** -0.5)
            qb, kb = qr.astype(BF16), kr.astype(BF16)
            vb = v_ref[:, sl].astype(BF16)
            sb = ss_ref[0, h]
            mask = m_ref[h]
            qdec, kdec = qd_ref[h], kd_ref[h]
            scb = (_dot_nt(qb, kb) * mask).astype(BF16)
            qdq = (qr * qdec).astype(BF16)
            kdk = (kr * kdec).astype(BF16)
            out = _dot(scb, vb) + _dot(qdq, sb)
            dev = out - _mean(out)
            rstd = lax.rsqrt(_mean(dev * dev) + EPS)
            yn = dev * rstd
            g = g_ref[:, sl]
            sg = _sigmoid(g)
            gng = gn_ref[:, sl]
            dyv = dy_ref[:, sl]
            dgr = dyv * (yn * gng) * _dsilu(g, sg)
            silu_g = g * sg
            dgn_ref[:, sl] += _colsum(dyv * yn * silu_g)
            dyn = dyv * gng * silu_g
            dout = rstd * (dyn - _mean(dyn) - yn * _mean(dyn * yn))
            dob = dout.astype(BF16)
            dscb = (_dot_nt(dob, vb) * mask).astype(BF16)
            dstate = ds_scr[h]
            dsb = dstate.astype(BF16)
            dq = _dot(dscb, kb) + _dot_nt(dob, sb) * qdec
            dk = _dot_tn(dscb, qb) + _dot_nt(vb, dsb) * kdec
            dv = _dot_tn(scb, dob) + _dot(kdk, dsb)
            ds_scr[h] = dstate * cd_ref[h] + _dot_tn(qdq, dob)
            dp_ref[:, 0 * w + h * hd:0 * w + (h + 1) * hd] = _rope_t(dq, cos_t, sin_t).astype(BF16)
            dp_ref[:, 1 * w + h * hd:1 * w + (h + 1) * hd] = (_rope_t(dk, cos_t, sin_t) * (hd ** -0.5)).astype(BF16)
            dp_ref[:, 2 * w + h * hd:2 * w + (h + 1) * hd] = dv.astype(BF16)
            dp_ref[:, 3 * w + h * hd:3 * w + (h + 1) * hd] = dgr.astype(BF16)

    def rev(i):
        return nch - 1 - i

    def col(j):
        return pl.BlockSpec((CHUNK, w), lambda i: (rev(i), j))

    def whole(a):
        return pl.BlockSpec(a.shape, lambda i: (0,) * a.ndim)

    return pl.pallas_call(
        body, name="ret_bwd", grid=(nch,),
        in_specs=[col(0), col(1), col(2), col(3),
                  pl.BlockSpec((CHUNK, w), lambda i: (rev(i), 0)),
                  pl.BlockSpec((1, RET_HEADS, hd, hd), lambda i: (rev(i), 0, 0, 0)),
                  pl.BlockSpec((CHUNK, hd // 2), lambda i: (rev(i), 0)),
                  pl.BlockSpec((CHUNK, hd // 2), lambda i: (rev(i), 0)),
                  whole(gn_g), whole(dmask), whole(qd), whole(kd), whole(cd)],
        out_specs=[pl.BlockSpec((CHUNK, 4 * w), lambda i: (rev(i), 0)),
                   pl.BlockSpec((1, w), lambda i: (0, 0))],
        out_shape=[jax.ShapeDtypeStruct((lp, 7 * w), BF16), jax.ShapeDtypeStruct((1, w), F32)],
        scratch_shapes=[pltpu.VMEM((RET_HEADS, hd, hd), F32)],
        compiler_params=_params(1, 32))(proj, proj, proj, proj, dy, ssave, cos, sin, gn_g, dmask, qd, kd, cd)


def _conv_bwd_pw(dy, proj, u1, cln_g, cln_b, pw_w, pw_b, dproj):
    lp, cw = u1.shape
    tm = _row_tile(lp, 320)

    def body(dy_ref, gc_ref, u1_ref, lg_ref, lb_ref, pw_ref, pb_ref, dp_in, dp_ref, du1_ref, dpw_ref,
             dpb_ref, dlg_ref, dlb_ref):
        del dp_in

        @pl.when(pl.program_id(0) == 0)
        def _():
            dpw_ref[...] = jnp.zeros_like(dpw_ref)
            dpb_ref[...] = jnp.zeros_like(dpb_ref)
            dlg_ref[...] = jnp.zeros_like(dlg_ref)
            dlb_ref[...] = jnp.zeros_like(dlb_ref)

        u1 = u1_ref[...]
        dev = u1 - _mean(u1)
        rstd = lax.rsqrt(_mean(dev * dev) + EPS)
        u1n = dev * rstd
        lg = lg_ref[...]
        z = u1n * lg + lb_ref[...]
        sz = _sigmoid(z)
        u3b = (z * sz).astype(BF16)
        u4 = _dot(u3b, pw_ref[...]) + pb_ref[...]
        gc = gc_ref[...]
        sgc = _sigmoid(gc)
        dyc = dy_ref[...]
        du4 = dyc * (gc * sgc)
        dp_ref[...] = (dyc * u4 * _dsilu(gc, sgc)).astype(BF16)
        du4b = du4.astype(BF16)
        dpb_ref[...] += _colsum(du4)
        dpw_ref[...] += _dot_tn(u3b, du4b)
        dz = _dot_nt(du4b, pw_ref[...]) * _dsilu(z, sz)
        dlg_ref[...] += _colsum(dz * u1n)
        dlb_ref[...] += _colsum(dz)
        dn = dz * lg
        du1_ref[...] = rstd * (dn - _mean(dn) - u1n * _mean(dn * u1n))

    def row(j):
        return pl.BlockSpec((tm, cw), lambda i: (i, j))

    def whole(a):
        return pl.BlockSpec(a.shape, lambda i: (0,) * a.ndim)

    def acc(r):
        return pl.BlockSpec((r, cw), lambda i: (0, 0))

    return pl.pallas_call(
        body, name="conv_bwd_pw", grid=(lp // tm,),
        in_specs=[row(1), row(6), row(0), whole(cln_g), whole(cln_b), whole(pw_w), whole(pw_b),
                  pl.BlockSpec(memory_space=pl.ANY)],
        out_specs=[row(6), row(0), acc(cw), acc(1), acc(1), acc(1)],
        out_shape=[jax.ShapeDtypeStruct(dproj.shape, dproj.dtype), jax.ShapeDtypeStruct((lp, cw), F32),
                   jax.ShapeDtypeStruct((cw, cw), F32), jax.ShapeDtypeStruct((1, cw), F32),
                   jax.ShapeDtypeStruct((1, cw), F32), jax.ShapeDtypeStruct((1, cw), F32)],
        input_output_aliases={7: 0},
        compiler_params=_params(1, 48))(dy, proj, u1, cln_g, cln_b, pw_w, pw_b, dproj)


def _conv_bwd_dw(du1, proj, dw_w, dproj):
    lp, cw = du1.shape
    tm = _row_tile(lp, 640)
    rb = _row_tile(tm, 64)
    nt = lp // tm
    hb = tm // HALO

    def body(a_ref, b_ref, du_ref, nx_ref, w_ref, dp_in, dp_ref, dww_ref, dwb_ref, ubuf, dbuf, du0, acc):
        del dp_in
        i = pl.program_id(0)

        @pl.when(i == 0)
        def _():
            ubuf[0:HALO, :] = jnp.zeros((HALO, cw), F32)
            acc[...] = jnp.zeros_like(acc)
            dwb_ref[...] = jnp.zeros_like(dwb_ref)

        a = a_ref[...]
        sb = _sigmoid(b_ref[...])
        ubuf[HALO:HALO + tm, :] = a * sb
        du = du_ref[...]
        dbuf[0:tm, :] = du
        dbuf[tm:tm + HALO, :] = jnp.where(i == nt - 1, 0.0, nx_ref[...])
        dwb_ref[...] += _colsum(du)
        _dw_taps(dbuf, w_ref, du0, None, rows=tm, start=0, flip=True, rb=rb)
        d0 = du0[...]
        dp_ref[:, 0:cw] = (d0 * sb).astype(BF16)
        dp_ref[:, cw:2 * cw] = (d0 * a * sb * (1.0 - sb)).astype(BF16)

        lb = min(128, cw)
        groups = _tap_groups(HALO - (CONV_K - 1), False)

        def rb_body(r, carry):
            base = pl.multiple_of(r * rb, rb)
            for cb in range(cw // lb):
                ls = slice(cb * lb, (cb + 1) * lb)
                win = ubuf[pl.ds(base, rb + HALO), ls]
                dv = dbuf[pl.ds(base, rb), ls]
                for s, taps in groups.items():
                    ws = _shift_up(win, s)
                    for a, j in taps:
                        prod = dv * ws[8 * a:8 * a + rb, :]
                        acc[8 * j:8 * j + 8, ls] += jnp.sum(prod.reshape(rb // 8, 8, lb), axis=0)
            return carry

        lax.fori_loop(0, tm // rb, rb_body, 0)
        ubuf[0:HALO, :] = ubuf[tm:tm + HALO, :]

        @pl.when(i == nt - 1)
        def _():
            for j in range(CONV_K):
                dww_ref[j:j + 1, :] = _colsum(acc[8 * j:8 * j + 8, :])
            dww_ref[CONV_K:HALO, :] = jnp.zeros((HALO - CONV_K, cw), F32)

    def col(j):
        return pl.BlockSpec((tm, cw), lambda i: (i, j))

    return pl.pallas_call(
        body, name="conv_bwd_dw", grid=(nt,),
        in_specs=[col(4), col(5), col(0),
                  pl.BlockSpec((HALO, cw), lambda i: (jnp.minimum((i + 1) * hb, nt * hb - 1), 0)),
                  pl.BlockSpec(dw_w.shape, lambda i: (0, 0)),
                  pl.BlockSpec(memory_space=pl.ANY)],
        out_specs=[pl.BlockSpec((tm, 2 * cw), lambda i: (i, 2)),
                   pl.BlockSpec((HALO, cw), lambda i: (0, 0)),
                   pl.BlockSpec((1, cw), lambda i: (0, 0))],
        out_shape=[jax.ShapeDtypeStruct(dproj.shape, dproj.dtype), jax.ShapeDtypeStruct((HALO, cw), F32),
                   jax.ShapeDtypeStruct((1, cw), F32)],
        scratch_shapes=[pltpu.VMEM((tm + HALO, cw), F32), pltpu.VMEM((tm + HALO, cw), F32),
                        pltpu.VMEM((tm, cw), F32), pltpu.VMEM((8 * HALO, cw), F32)],
        input_output_aliases={5: 0},
        compiler_params=_params(1, 56))(proj, proj, du1, du1, dw_w, dproj)


def _dw_in(hn, dproj, ns, others):
    lp, d = hn.shape
    tm = _row_tile(lp, 1664)
    nt = lp // tm
    mb = 512 if d % 512 == 0 else d
    n = len(others)
    halves = [d // 2] + [g.shape[1] // 2 for g in others]

    def body(hn_ref, dp_ref, *refs):
        other_refs, o_hbm, recv_refs = refs[:n], refs[n], refs[n + 1:2 * n + 2]
        acc, sem, send_sems, recv_sems = refs[2 * n + 2:]
        s, i = pl.program_id(0), pl.program_id(1)
        x, y, c = _mesh_pos()

        def to_sibling(src, dst, k):
            return pltpu.make_async_remote_copy(
                src_ref=src, dst_ref=dst, send_sem=send_sems.at[k], recv_sem=recv_sems.at[k],
                device_id=(x, y, 1 - c), device_id_type=MESH)

        def shard_half(p):
            return to_sibling(o_hbm.at[p, pl.ds((1 - c) * halves[0], halves[0])], recv_refs[0].at[p], p)

        def other_halves(k):
            return to_sibling(other_refs[k].at[:, pl.ds((1 - c) * halves[1 + k], halves[1 + k])],
                              recv_refs[1 + k], N_CHIPS + k)

        @pl.when((s == 0) & (i == 0))
        def _():
            for k in range(n):
                other_halves(k).start()

        @pl.when(i == 0)
        def _():
            acc[...] = jnp.zeros_like(acc)

        for m in range(d // mb):
            rows = slice(m * mb, (m + 1) * mb)
            acc[rows, :] += _dot_tn(hn_ref[:, rows], dp_ref[...])

        @pl.when(i == nt - 1)
        def _():
            cp = pltpu.make_async_copy(acc, o_hbm.at[s], sem)
            cp.start()
            cp.wait()
            shard_half(s).start()

        @pl.when((s == N_CHIPS - 1) & (i == nt - 1))
        def _():
            for p in range(N_CHIPS):
                shard_half(p).wait()
            for k in range(n):
                other_halves(k).wait()

    any_spec = pl.BlockSpec(memory_space=pl.ANY)
    outs = pl.pallas_call(
        body, name="dw_in", grid=(N_CHIPS, nt),
        in_specs=[pl.BlockSpec((tm, d), lambda s, i: (i, 0)),
                  pl.BlockSpec((tm, ns), lambda s, i: (i, s))] + [any_spec] * n,
        out_specs=[any_spec] * (n + 2),
        out_shape=[jax.ShapeDtypeStruct((N_CHIPS, d, ns), F32), jax.ShapeDtypeStruct((N_CHIPS, d // 2, ns), F32)]
                  + [jax.ShapeDtypeStruct((N_CHIPS, g.shape[1] // 2) + g.shape[2:], g.dtype) for g in others],
        scratch_shapes=[pltpu.VMEM((d, ns), F32), pltpu.SemaphoreType.DMA,
                        pltpu.SemaphoreType.DMA((N_CHIPS + n,)), pltpu.SemaphoreType.DMA((N_CHIPS + n,))],
        compiler_params=_params(2, 56))(hn, dproj, *others)
    return outs[0], outs[1:]


def _in_proj_bwd(dproj, w4, hp, r1, dh2, ln_g, cs):
    lp, d = hp.shape
    ns = w4.shape[2]
    tm = _row_tile(lp, 320)
    nt = lp // tm
    n = len(cs)

    def body(dp_ref, w_ref, h_ref, r_ref, d2_ref, g_ref, *refs):
        cs_refs, (dh_ref, dlg_ref), rb_refs = refs[:n], refs[n:n + 2], refs[n + 2:2 * n + 2]
        send_sems, recv_sems = refs[2 * n + 2:]
        i = pl.program_id(0)
        x, y, c = _mesh_pos()

        def exchange():
            return [pltpu.make_async_remote_copy(
                src_ref=cs_refs[k].at[2 * chip[0] + chip[1]], dst_ref=rb_refs[k].at[j],
                send_sem=send_sems.at[3 * k + j], recv_sem=recv_sems.at[3 * k + j],
                device_id=(*chip, c), device_id_type=MESH)
                for k in range(n) for j, chip in enumerate(_other_chips(x, y))]

        @pl.when(i == 0)
        def _():
            dlg_ref[...] = jnp.zeros_like(dlg_ref)
            for cp in exchange():
                cp.start()

        dhn = _dot_nt(dp_ref[:, 0:ns], w_ref[0])
        for s in range(1, N_CHIPS):
            dhn = dhn + _dot_nt(dp_ref[:, s * ns:(s + 1) * ns], w_ref[s])
        r = r_ref[...]
        hn0 = h_ref[...] * r
        dlg_ref[...] += _colsum(dhn * hn0)
        t = dhn * g_ref[...]
        dh_ref[...] = d2_ref[...] + r * (t - hn0 * _mean(t * hn0))

        @pl.when(i == nt - 1)
        def _():
            for cp in exchange():
                cp.wait()

    def row(cols):
        return pl.BlockSpec((tm, cols), lambda i: (i, 0))

    any_spec = pl.BlockSpec(memory_space=pl.ANY)
    outs = pl.pallas_call(
        body, name="in_proj_bwd", grid=(nt,),
        in_specs=[row(N_CHIPS * ns), pl.BlockSpec(memory_space=pltpu.VMEM),
                  row(d), row(1), row(d), pl.BlockSpec((1, d), lambda i: (0, 0))] + [any_spec] * n,
        out_specs=[row(d), pl.BlockSpec((1, d), lambda i: (0, 0))] + [any_spec] * n,
        out_shape=[jax.ShapeDtypeStruct((lp, d), F32), jax.ShapeDtypeStruct((1, d), F32)]
                  + [jax.ShapeDtypeStruct((3,) + a.shape[1:], a.dtype) for a in cs],
        scratch_shapes=[pltpu.SemaphoreType.DMA((3 * n,)), pltpu.SemaphoreType.DMA((3 * n,))],
        compiler_params=_params(1, 58))(dproj, w4, hp, r1, dh2, ln_g, *cs)
    return outs[0], outs[1], outs[2:]


def _mesh_pos():
    return lax.axis_index("x"), lax.axis_index("y"), lax.axis_index("c")


def _other_chips(x, y):
    return [(1 - x, y), (x, 1 - y), (1 - x, 1 - y)]


def _gather_shards(shards):
    n = len(shards)
    halves = [a.shape[0] // 2 for a in shards]

    def body(*refs):
        ins, outs = refs[:n], refs[n:2 * n]
        send_sems, recv_sems, loc_sems = refs[2 * n:]
        x, y, c = _mesh_pos()
        me, sibling = (x, y, c), (x, y, 1 - c)
        my_s = 2 * x + y
        chips = _other_chips(x, y)

        def half(k, s, cc):
            return outs[k].at[s, pl.ds(cc * halves[k], halves[k])]

        def rcopy(k, j, src, dst, to):
            return pltpu.make_async_remote_copy(
                src_ref=src, dst_ref=dst, send_sem=send_sems.at[6 * k + j], recv_sem=recv_sems.at[6 * k + j],
                device_id=to, device_id_type=MESH)

        local = [pltpu.make_async_copy(ins[k], outs[k].at[my_s], loc_sems.at[k]) for k in range(n)]
        for cp in local:
            cp.start()
        started = []
        for k in range(n):
            for j, chip in enumerate(chips):
                cp = rcopy(k, j, ins[k].at[pl.ds(c * halves[k], halves[k])], half(k, my_s, c), (*chip, c))
                cp.start()
                started.append(cp)
        for j, chip in enumerate(chips):
            s_j = 2 * chip[0] + chip[1]
            for k in range(n):
                rcopy(k, j, half(k, s_j, c), half(k, s_j, c), me).wait_recv()
                cp = rcopy(k, 3 + j, half(k, s_j, c), half(k, s_j, c), sibling)
                cp.start()
                started.append(cp)
        for j, chip in enumerate(chips):
            s_j = 2 * chip[0] + chip[1]
            for k in range(n):
                rcopy(k, 3 + j, half(k, s_j, 1 - c), half(k, s_j, 1 - c), me).wait_recv()
        for cp in started:
            cp.wait_send()
        for cp in local:
            cp.wait()

    return pl.pallas_call(
        body, name="gather_weights",
        in_specs=[pl.BlockSpec(memory_space=pl.ANY)] * n,
        out_specs=[pl.BlockSpec(memory_space=pl.ANY)] * n,
        out_shape=[jax.ShapeDtypeStruct((N_CHIPS,) + a.shape, a.dtype) for a in shards],
        scratch_shapes=[pltpu.SemaphoreType.DMA((6 * n,)), pltpu.SemaphoreType.DMA((6 * n,)),
                        pltpu.SemaphoreType.DMA((n,))],
    )(*shards)


def _rs_pair_share(fulls):
    n = len(fulls)
    halves = [f.shape[0] // 2 for f in fulls]

    def body(*refs):
        outs = refs[n:2 * n]
        send_sems, recv_sems = refs[2 * n:]
        x, y, c = _mesh_pos()

        def copy(k, cc, to):
            rows = outs[k].at[pl.ds(cc * halves[k], halves[k])]
            return pltpu.make_async_remote_copy(
                src_ref=rows, dst_ref=rows, send_sem=send_sems.at[k], recv_sem=recv_sems.at[k],
                device_id=to, device_id_type=MESH)

        cps = [copy(k, c, (x, y, 1 - c)) for k in range(n)]
        for cp in cps:
            cp.start()
        for k in range(n):
            copy(k, 1 - c, (x, y, c)).wait_recv()
        for cp in cps:
            cp.wait_send()

    return pl.pallas_call(
        body, name="rs_pair_share",
        in_specs=[pl.BlockSpec(memory_space=pl.ANY)] * n,
        out_specs=[pl.BlockSpec(memory_space=pl.ANY)] * n,
        out_shape=[jax.ShapeDtypeStruct(f.shape, f.dtype) for f in fulls],
        input_output_aliases={k: k for k in range(n)},
        scratch_shapes=[pltpu.SemaphoreType.DMA((n,)), pltpu.SemaphoreType.DMA((n,))],
    )(*fulls)


def _pair_sum(g, recv, pos, name):
    _, rows, cols = g.shape
    h = rows // 2
    tr = _row_tile(h, 256)
    nb = h // tr

    def body(pos_ref, g_ref, r_ref, o_ref, own_ref):
        total = g_ref[0] + r_ref[0]
        o_ref[0] = total.astype(BF16)

        @pl.when(pl.program_id(1) == pos_ref[1])
        def _():
            own_ref[...] = total

    return pl.pallas_call(
        body, name=name,
        grid_spec=pltpu.PrefetchScalarGridSpec(
            num_scalar_prefetch=1, grid=(nb, N_CHIPS),
            in_specs=[pl.BlockSpec((1, tr, cols), lambda r, s, pos_ref: (s, pos_ref[0] * nb + r, 0)),
                      pl.BlockSpec((1, tr, cols), lambda r, s, pos_ref: (s, r, 0))],
            out_specs=[pl.BlockSpec((1, tr, cols), lambda r, s, pos_ref: (s, r, 0)),
                       pl.BlockSpec((tr, cols), lambda r, s, pos_ref: (r, 0))]),
        out_shape=[jax.ShapeDtypeStruct((N_CHIPS, h, cols), BF16), jax.ShapeDtypeStruct((h, cols), F32)],
        compiler_params=_params(2, 32))(pos, g, recv)


def _chip_sum(own, rb, pos, name):
    h, cols = own.shape
    tr = _row_tile(h, 256)
    nb = h // tr

    def body(pos_ref, c_ref, r_ref, o_ref):
        del pos_ref
        o_ref[...] = ((c_ref[...] + r_ref[0].astype(F32)) + r_ref[1].astype(F32)) + r_ref[2].astype(F32)

    return pl.pallas_call(
        body, name=name,
        grid_spec=pltpu.PrefetchScalarGridSpec(
            num_scalar_prefetch=1, grid=(nb,),
            in_specs=[pl.BlockSpec((tr, cols), lambda r, pos_ref: (r, 0)),
                      pl.BlockSpec((3, tr, cols), lambda r, pos_ref: (0, r, 0))],
            out_specs=pl.BlockSpec((tr, cols), lambda r, pos_ref: (pos_ref[0] * nb + r, 0))),
        out_shape=jax.ShapeDtypeStruct((2 * h, cols), F32),
        compiler_params=_params(1, 32))(pos, own, rb)


def _adamw_big(w, g, m, v, name):
    rows, cols = w.shape
    tr = _row_tile(rows, 256)

    def body(w_ref, g_ref, m_ref, v_ref, d_ref, nm_ref, nv_ref):
        d_ref[...], nm_ref[...], nv_ref[...] = _adamw(w_ref[...], g_ref[...], m_ref[...], v_ref[...])

    spec = pl.BlockSpec((tr, cols), lambda i: (i, 0))
    return pl.pallas_call(
        body, name=name, grid=(rows // tr,),
        in_specs=[spec] * 4, out_specs=[spec] * 3,
        out_shape=[jax.ShapeDtypeStruct((rows, cols), F32)] * 3,
        compiler_params=_params(1, 48))(w, g, m, v)


def _gather_small(loss, dfg, dlg, dgn, ddwb, dclg, dclb, dpwb, ddww, dmeta):
    d = loss.shape[1]
    w = dgn.shape[1]

    def body(loss_ref, dfg_ref, dlg_ref, dgn_ref, ddwb_ref, dclg_ref, dclb_ref, dpwb_ref, ddww_ref, dmeta_ref,
             gs_ref, gd_ref, gm_ref, send_sems, recv_sems):
        x, y, c = _mesh_pos()
        me = 4 * x + 2 * y + c
        gs_ref[me, 0:1, :] = loss_ref[...]
        gs_ref[me, 1:2, :] = dfg_ref[...]
        gs_ref[me, 2:3, :] = dlg_ref[...]
        gs_ref[me, 3:4, 0:w] = dgn_ref[...]
        gs_ref[me, 3:4, w:2 * w] = ddwb_ref[...]
        gs_ref[me, 4:5, 0:w] = dclg_ref[...]
        gs_ref[me, 4:5, w:2 * w] = dclb_ref[...]
        gs_ref[me, 5:6, 0:w] = dpwb_ref[...]
        gs_ref[me, 5:6, w:2 * w] = jnp.zeros((1, d - w), F32)
        gs_ref[me, 6:8, :] = jnp.zeros((2, d), F32)
        gd_ref[me] = ddww_ref[...]
        gm_ref[me] = dmeta_ref[...]
        bufs = (gs_ref, gd_ref, gm_ref)

        def peer(j):
            return (1 - x if j & 4 else x), (1 - y if j & 2 else y), (1 - c if j & 1 else c)

        def copy(k, j, slot, to):
            return pltpu.make_async_remote_copy(
                src_ref=bufs[k].at[slot], dst_ref=bufs[k].at[slot],
                send_sem=send_sems.at[7 * k + j - 1], recv_sem=recv_sems.at[7 * k + j - 1],
                device_id=to, device_id_type=MESH)

        cps = []
        for k in range(3):
            for j in range(1, N_DEV):
                cp = copy(k, j, me, peer(j))
                cp.start()
                cps.append(cp)
        for k in range(3):
            for j in range(1, N_DEV):
                px, py, pc = peer(j)
                copy(k, j, 4 * px + 2 * py + pc, (x, y, c)).wait_recv()
        for cp in cps:
            cp.wait_send()

    vm = pl.BlockSpec(memory_space=pltpu.VMEM)
    return pl.pallas_call(
        body, name="gather_small",
        in_specs=[vm] * 10, out_specs=[vm] * 3,
        out_shape=[jax.ShapeDtypeStruct((N_DEV, 8, d), F32),
                   jax.ShapeDtypeStruct((N_DEV,) + ddww.shape, F32),
                   jax.ShapeDtypeStruct((N_DEV,) + dmeta.shape, F32)],
        scratch_shapes=[pltpu.SemaphoreType.DMA((21,)), pltpu.SemaphoreType.DMA((21,))],
    )(loss, dfg, dlg, dgn, ddwb, dclg, dclb, dpwb, ddww, dmeta)


def _small_update(s_arr, gs, gd, gm, weights, ms, vs):
    d = gs.shape[2]
    w = d // 2
    n = len(weights)

    def body(s_ref, gs_ref, gd_ref, gm_ref, *refs):
        del s_ref
        w_refs, m_refs, v_refs = refs[:n], refs[n:2 * n], refs[2 * n:3 * n]
        loss_ref = refs[3 * n]
        g_refs = refs[3 * n + 1:4 * n + 1]
        d_refs = refs[4 * n + 1:5 * n + 1]
        nm_refs = refs[5 * n + 1:6 * n + 1]
        nv_refs = refs[6 * n + 1:7 * n + 1]

        def total(ref):
            t = ref[0]
            for dev in range(1, N_DEV):
                t = t + ref[dev]
            return t

        packed = total(gs_ref)
        loss_ref[...] = jnp.sum(packed[0:1, :], axis=1, keepdims=True) * (0.5 / d)
        grads = [packed[2:3, :], packed[1:2, :], packed[3:4, 0:w], packed[3:4, w:2 * w], packed[4:5, 0:w],
                 packed[4:5, w:2 * w], packed[5:6, 0:w], total(gd_ref), total(gm_ref)]
        for k in range(n):
            g = grads[k]
            g_refs[k][...] = g
            d_refs[k][...], nm_refs[k][...], nv_refs[k][...] = _adamw(w_refs[k][...], g, m_refs[k][...], v_refs[k][...])

    def whole(shape):
        return pl.BlockSpec(shape, lambda i, s_ref: (0,) * len(shape))

    wc = weights[7].shape[1]
    mc = weights[8].shape[1]
    shapes = [a.shape for a in weights]
    in_specs = ([whole(gs.shape),
                 pl.BlockSpec((N_DEV, gd.shape[1], wc), lambda i, s_ref: (0, 0, s_ref[0])),
                 pl.BlockSpec((N_DEV, gm.shape[1], mc), lambda i, s_ref: (0, 0, s_ref[0]))]
                + [whole(s) for s in shapes] * 3)
    out_specs = [whole((1, 1))] + [whole(s) for s in shapes] * 4
    out_shape = [jax.ShapeDtypeStruct((1, 1), F32)] + [jax.ShapeDtypeStruct(s, F32) for s in shapes] * 4
    outs = pl.pallas_call(
        body, name="small_update",
        grid_spec=pltpu.PrefetchScalarGridSpec(num_scalar_prefetch=1, grid=(1,), in_specs=in_specs,
                                               out_specs=out_specs),
        out_shape=out_shape,
        compiler_params=_params(1, 32))(s_arr, gs, gd, gm, *weights, *ms, *vs)
    loss = outs[0]
    return loss, outs[1:n + 1], outs[n + 1:2 * n + 1], outs[2 * n + 1:3 * n + 1], outs[3 * n + 1:4 * n + 1]


def kernel(x, meta_tokens, ln_g, w_in, ret_gn_g, conv_dw_w, conv_dw_b, conv_ln_g, conv_ln_b, conv_pw_w, conv_pw_b, w_out, final_g, loss_target, m_meta_tokens, m_ln_g, m_w_in, m_ret_gn_g, m_conv_dw_w, m_conv_dw_b, m_conv_ln_g, m_conv_ln_b, m_conv_pw_w, m_conv_pw_b, m_w_out, m_final_g, v_meta_tokens, v_ln_g, v_w_in, v_ret_gn_g, v_conv_dw_w, v_conv_dw_b, v_conv_ln_g, v_conv_ln_b, v_conv_pw_w, v_conv_pw_b, v_w_out, v_final_g):
    seq, d = x.shape[1], x.shape[2]
    w = ret_gn_g.shape[1]
    hd = w // RET_HEADS
    lp = CHUNK + seq
    ns = w_in.shape[2]
    mx, my, mc = lax.axis_index("x"), lax.axis_index("y"), lax.axis_index("c")
    my_s = 2 * mx + my
    s_arr = my_s.astype(jnp.int32).reshape(1)
    pos = jnp.stack([mc, my_s]).astype(jnp.int32)
    order = jnp.stack([my_s, 2 * (1 - mx) + my, 2 * mx + (1 - my), 2 * (1 - mx) + (1 - my)]).astype(jnp.int32)

    dw_pad = jnp.pad(conv_dw_w[0], ((0, HALO - CONV_K), (0, 0)))
    dw4, meta4 = _gather_shards([dw_pad, meta_tokens])
    dw_full = dw4.transpose(1, 0, 2).reshape(HALO, w)
    meta_full = meta4.transpose(1, 0, 2).reshape(N_META, d)

    hp = jnp.concatenate([jnp.zeros((LEAD, d), F32), meta_full, x[0]], axis=0)
    consts = _ret_consts()
    cos, sin = _rope_tables(lp, hd // 2)
    fg2 = final_g.reshape(1, d)

    proj, r1, hn, w4, pw4, wo4 = _in_proj_gather(
        order, hp, ln_g, _cast_into_gathered(w_in[0], s_arr, "cast_w_in"),
        _cast_into_gathered(conv_pw_w[0], s_arr, "cast_pw_w"), _cast_into_gathered(w_out[0], s_arr, "cast_w_out"))
    pw_full = pw4.reshape(w, w)
    wo_full = wo4.reshape(2 * w, d)
    y_ret, ssave = _ret_fwd(proj, cos, sin, ret_gn_g, consts)
    y_conv, u1 = _conv_fwd(proj, dw_full, conv_dw_b, conv_ln_g, conv_ln_b, pw_full, conv_pw_b)
    dh2, dy, loss_l, dfg = _out_proj_loss(y_ret, y_conv, hp, loss_target[0], wo_full, fg2)

    g_wo = _dw_out(y_ret, y_conv, dh2)
    dproj, dgn = _ret_bwd(proj, dy, ssave, cos, sin, ret_gn_g, consts)
    dproj, du1, g_pw, dpwb, dclg, dclb = _conv_bwd_pw(dy, proj, u1, conv_ln_g, conv_ln_b, pw_full, conv_pw_b, dproj)
    dproj, ddww, ddwb = _conv_bwd_dw(du1, proj, dw_full, dproj)
    g_wo4 = g_wo.reshape(N_CHIPS, (2 * w) // N_CHIPS, d)
    g_pw4 = g_pw.reshape(N_CHIPS, w // N_CHIPS, w)
    g_win, recv = _dw_in(hn, dproj, ns, [g_wo4, g_pw4])
    gs = [g_win, g_wo4, g_pw4]
    names = ("w_in", "w_out", "pw_w")
    sums = [_pair_sum(g, r, pos, "pair_sum_" + nm) for g, r, nm in zip(gs, recv, names)]
    dh, dlg, rb = _in_proj_bwd(dproj, w4, hp, r1, dh2, ln_g, [cs_ for cs_, _ in sums])
    grad_x = dh[CHUNK:][None]
    dmeta = dh[LEAD:CHUNK]
    fulls = [_chip_sum(own, r, pos, "chip_sum_" + nm) for (_, own), r, nm in zip(sums, rb, names)]
    grad_w_in, grad_w_out, grad_pw = _rs_pair_share(fulls)
    d_win, nm_win, nv_win = _adamw_big(w_in[0], grad_w_in, m_w_in[0], v_w_in[0], "adamw_w_in")
    d_wo, nm_wo, nv_wo = _adamw_big(w_out[0], grad_w_out, m_w_out[0], v_w_out[0], "adamw_w_out")
    d_pw, nm_pw, nv_pw = _adamw_big(conv_pw_w[0], grad_pw, m_conv_pw_w[0], v_conv_pw_w[0], "adamw_pw_w")

    gsm, gdm, gmm = _gather_small(loss_l, dfg, dlg, dgn, ddwb, dclg, dclb, dpwb, ddww, dmeta)

    def pad_dw(a):
        return jnp.pad(a[0], ((0, HALO - CONV_K), (0, 0)))

    small_w = [ln_g, fg2, ret_gn_g, conv_dw_b, conv_ln_g, conv_ln_b, conv_pw_b, dw_pad, meta_tokens]
    small_m = [m_ln_g, m_final_g.reshape(1, d), m_ret_gn_g, m_conv_dw_b, m_conv_ln_g, m_conv_ln_b, m_conv_pw_b,
               pad_dw(m_conv_dw_w), m_meta_tokens]
    small_v = [v_ln_g, v_final_g.reshape(1, d), v_ret_gn_g, v_conv_dw_b, v_conv_ln_g, v_conv_ln_b, v_conv_pw_b,
               pad_dw(v_conv_dw_w), v_meta_tokens]
    loss, sg, sd, snm, snv = _small_update(s_arr, gsm, gdm, gmm, small_w, small_m, small_v)

    def assemble(small, big_in, big_pw, big_out):
        ln, fg, gn, dwb, clg, clb, pwb, dww, meta = small
        return (meta, ln, big_in[None], gn, dww[:CONV_K][None], dwb, clg, clb, big_pw[None], pwb, big_out[None],
                fg.reshape(d))

    return (loss.reshape(()), grad_x,
            *assemble(sg, grad_w_in, grad_pw, grad_w_out),
            *assemble(sd, d_win, d_pw, d_wo),
            *assemble(snm, nm_win, nm_pw, nm_wo),
            *assemble(snv, nv_win, nv_pw, nv_wo))
```

```python
import functools

import jax
import jax.numpy as jnp
from jax import lax
from jax.experimental import pallas as pl
from jax.experimental.pallas import tpu as pltpu

F32 = jnp.float32
BF16 = jnp.bfloat16
MESH = pl.DeviceIdType.MESH

N_META = 16
CHUNK = 128
LEAD = (-N_META) % CHUNK
RET_HEADS = 4
CONV_K = 31
HALO = 32
ROPE_BASE = 10000.0
EPS = 1e-6
N_CHIPS = 4
N_DEV = 8
SEND_PIECES = 8

ADAM_LR = 0.001
ADAM_B1 = 0.9
ADAM_B2 = 0.999
ADAM_EPS = 1e-08
ADAM_WD = 0.01
ADAM_STEP = 10

MIB = 2 ** 20


def _params(n_grid_axes, vmem_mib):
    return pltpu.CompilerParams(dimension_semantics=("arbitrary",) * n_grid_axes,
                                vmem_limit_bytes=vmem_mib * MIB)


def _row_tile(n, pref):
    for t in (1664, 1280, 1024, 640, 512, 384, 320, 256, 128, 64, 32, 16, 8):
        if t <= pref and n % t == 0:
            return t
    raise ValueError(f"no row tile for {n}")


def _dot(a, b):
    return jnp.dot(a, b, preferred_element_type=F32)


def _dot_nt(a, b):
    return lax.dot_general(a, b, (((1,), (1,)), ((), ())), preferred_element_type=F32)


def _dot_tn(a, b):
    return lax.dot_general(a, b, (((0,), (0,)), ((), ())), preferred_element_type=F32)


def _sigmoid(x):
    return jax.nn.sigmoid(x)


def _dsilu(x, s):
    return s * (1.0 + x * (1.0 - s))


def _mean(x):
    return jnp.mean(x, axis=-1, keepdims=True)


def _colsum(x):
    return jnp.sum(x, axis=0, keepdims=True)


def _rope(x, cos, sin):
    half = x.shape[-1] // 2
    x1, x2 = x[:, :half], x[:, half:]
    return jnp.concatenate([x1 * cos - x2 * sin, x1 * sin + x2 * cos], axis=-1)


def _rope_t(d, cos, sin):
    half = d.shape[-1] // 2
    d1, d2 = d[:, :half], d[:, half:]
    return jnp.concatenate([d1 * cos + d2 * sin, d2 * cos - d1 * sin], axis=-1)


def _adamw(w, g, m, v):
    m = ADAM_B1 * m + (1.0 - ADAM_B1) * g
    v = ADAM_B2 * v + (1.0 - ADAM_B2) * (g * g)
    m_hat = m / (1.0 - ADAM_B1 ** ADAM_STEP)
    v_hat = v / (1.0 - ADAM_B2 ** ADAM_STEP)
    delta = -ADAM_LR * (m_hat / (jnp.sqrt(v_hat) + ADAM_EPS) + ADAM_WD * w)
    return delta, m, v


def _ret_consts():
    h = jnp.arange(RET_HEADS, dtype=F32)
    log_g = jnp.log(1.0 - jnp.exp2(-5.0 - h))
    idx = jnp.arange(CHUNK, dtype=F32)
    rel = idx[:, None] - idx[None, :]
    dmask = jnp.where(rel[None] >= 0, jnp.exp(jnp.maximum(rel, 0.0)[None] * log_g[:, None, None]), 0.0)
    qd = jnp.exp((idx[None, :] + 1.0) * log_g[:, None])[:, :, None]
    kd = jnp.exp((CHUNK - 1.0 - idx[None, :]) * log_g[:, None])[:, :, None]
    cd = jnp.exp(CHUNK * log_g)[:, None, None]
    return dmask, qd, kd, cd


def _rope_tables(n_rows, half):
    pos = jnp.arange(n_rows, dtype=F32) - float(LEAD)
    inv_freq = ROPE_BASE ** (-jnp.arange(half, dtype=F32) / half)
    ang = pos[:, None] * inv_freq[None, :]
    return jnp.cos(ang), jnp.sin(ang)


def _padded_tile_stream(x_hbm, lead_hbm, buf, sems, tm):
    def start_first(slot):
        pltpu.make_async_copy(lead_hbm, buf.at[slot, pl.ds(0, CHUNK)], sems.at[slot]).start()
        pltpu.make_async_copy(x_hbm.at[pl.ds(0, tm - CHUNK)], buf.at[slot, pl.ds(CHUNK, tm - CHUNK)],
                              sems.at[slot]).start()

    def start(slot, tile):
        pltpu.make_async_copy(x_hbm.at[pl.ds(tile * tm - CHUNK, tm)], buf.at[slot], sems.at[slot]).start()

    def wait(slot):
        pltpu.make_async_copy(x_hbm.at[pl.ds(0, tm)], buf.at[slot], sems.at[slot]).wait()

    return start_first, start, wait


def _cast_into_gathered(a, s_arr, name):
    rows, cols = a.shape
    tr = _row_tile(rows, 256)

    def body(s_ref, a_ref, o_ref):
        del s_ref
        o_ref[0] = a_ref[...].astype(BF16)

    return pl.pallas_call(
        body, name=name,
        grid_spec=pltpu.PrefetchScalarGridSpec(
            num_scalar_prefetch=1, grid=(rows // tr,),
            in_specs=[pl.BlockSpec((tr, cols), lambda i, s_ref: (i, 0))],
            out_specs=pl.BlockSpec((1, tr, cols), lambda i, s_ref: (s_ref[0], i, 0))),
        out_shape=jax.ShapeDtypeStruct((N_CHIPS, rows, cols), BF16),
        compiler_params=_params(1, 32))(s_arr, a)


def _in_proj_gather(order, x, lead, ln_g, w4, pw4, wo4):
    seq, d = x.shape
    lp = CHUNK + seq
    ns = w4.shape[2]
    tm = _row_tile(lp, 640)
    nt = lp // tm
    assert nt >= 2, "the hn write-back of a row tile is waited for one step later, before any pass re-reads it"
    gathered = (w4, pw4, wo4)
    halves = [a.shape[1] // 2 for a in gathered]
    n = len(gathered)

    def body(order_ref, x_hbm, lead_hbm, g_ref, w_in, pw_in, wo_in, proj_ref, r_ref, hn_hbm, w_out, pw_out,
             wo_out, wbuf, hnbuf, hbuf, send_sems, recv_sems, hn_out_sems, hn_in_sems, h_sems, w_sem):
        del order_ref, w_in, pw_in, wo_in
        t, i = pl.program_id(0), pl.program_id(1)
        slot = (t * nt + i) % 2
        h_first, h_start, h_wait = _padded_tile_stream(x_hbm, lead_hbm, hbuf, h_sems, tm)
        x, y, c = _mesh_pos()
        me, sibling = (x, y, c), (x, y, 1 - c)
        my_s = 2 * x + y
        chips = _other_chips(x, y)
        outs = (w_out, pw_out, wo_out)

        def half(k, s, cc):
            return outs[k].at[s, pl.ds(cc * halves[k], halves[k])]

        def rcopy(k, j, rows, to):
            return pltpu.make_async_remote_copy(
                src_ref=rows, dst_ref=rows, send_sem=send_sems.at[6 * k + j], recv_sem=recv_sems.at[6 * k + j],
                device_id=to, device_id_type=MESH)

        def send(k, j):
            return rcopy(k, j, half(k, my_s, c), (*chips[j], c))

        def send_in_pieces(k, j):
            rows = halves[k] // SEND_PIECES
            for q in range(SEND_PIECES):
                piece = outs[k].at[my_s, pl.ds(c * halves[k] + q * rows, rows)]
                rcopy(k, j, piece, (*chips[j], c)).start()

        def shard_of(j):
            return 2 * chips[j][0] + chips[j][1]

        def forward(k, j):
            return rcopy(k, 3 + j, half(k, shard_of(j), c), sibling)

        def land(k, j):
            rcopy(k, j, half(k, shard_of(j), c), me).wait_recv()
            forward(k, j).start()

        def landed_from_sibling(k, j):
            rcopy(k, 3 + j, half(k, shard_of(j), 1 - c), me).wait_recv()

        def load_w(s):
            cp = pltpu.make_async_copy(w_out.at[s], wbuf, w_sem)
            cp.start()
            cp.wait()

        def hn_out(sl, row_tile):
            return pltpu.make_async_copy(hnbuf.at[sl], hn_hbm.at[pl.ds(row_tile * tm, tm)], hn_out_sems.at[sl])

        def hn_in(sl, row_tile):
            return pltpu.make_async_copy(hn_hbm.at[pl.ds(row_tile * tm, tm)], hnbuf.at[sl], hn_in_sems.at[sl])

        @pl.when((t == 0) & (i == 0))
        def _():
            for j in range(2):
                send_in_pieces(0, j)
            load_w(my_s)

        for j in range(3):
            @pl.when((t == j + 1) & (i == 0))
            def _(j=j):
                if j == 0:
                    send_in_pieces(0, 2)
                    for k in range(1, n):
                        for jj in range(3):
                            send(k, jj).start()
                land(0, j)
                landed_from_sibling(0, j)
                load_w(shard_of(j))

        @pl.when((t == 0) & (i == 0))
        def _():
            h_first(0)

        @pl.when((t == 0) & (i + 1 < nt))
        def _():
            h_start((i + 1) % 2, i + 1)

        @pl.when(t == 0)
        def _():
            h_wait(i % 2)
            h = hbuf[i % 2]
            r = lax.rsqrt(_mean(h * h) + EPS)
            hnbuf[slot] = ((h * r) * g_ref[...]).astype(BF16)
            r_ref[...] = r
            hn_out(slot, i).start()

        @pl.when(t > 0)
        def _():
            hn_in(slot, i).wait()

        @pl.when(((t == 0) & (i > 0)) | ((t == 1) & (i == 0)))
        def _():
            hn_out(1 - slot, jnp.where(i > 0, i - 1, nt - 1)).wait()

        last = (t == N_CHIPS - 1) & (i == nt - 1)

        @pl.when(((t > 0) | (i == nt - 1)) & jnp.logical_not(last))
        def _():
            hn_in(1 - slot, jnp.where(i == nt - 1, 0, i + 1)).start()

        proj_ref[...] = _dot(hnbuf[slot], wbuf[...])

        @pl.when(last)
        def _():
            for k in range(1, n):
                for j in range(3):
                    land(k, j)
            for k in range(1, n):
                for j in range(3):
                    landed_from_sibling(k, j)
            for k in range(n):
                for j in range(3):
                    send(k, j).wait_send()
                    forward(k, j).wait_send()

    def frozen(t, i):
        return jnp.where(t == 0, i, nt - 1)

    any_spec = pl.BlockSpec(memory_space=pl.ANY)
    return pl.pallas_call(
        body, name="in_proj_gather",
        grid_spec=pltpu.PrefetchScalarGridSpec(
            num_scalar_prefetch=1, grid=(N_CHIPS, nt),
            in_specs=[any_spec, any_spec, pl.BlockSpec((1, d), lambda t, i, o: (0, 0)),
                      any_spec, any_spec, any_spec],
            out_specs=[pl.BlockSpec((tm, ns), lambda t, i, o: (i, o[t])),
                       pl.BlockSpec((tm, 1), lambda t, i, o: (frozen(t, i), 0)),
                       any_spec, any_spec, any_spec, any_spec],
            scratch_shapes=[pltpu.VMEM((d, ns), BF16), pltpu.VMEM((2, tm, d), BF16), pltpu.VMEM((2, tm, d), F32),
                            pltpu.SemaphoreType.DMA((6 * n,)), pltpu.SemaphoreType.DMA((6 * n,)),
                            pltpu.SemaphoreType.DMA((2,)), pltpu.SemaphoreType.DMA((2,)),
                            pltpu.SemaphoreType.DMA((2,)), pltpu.SemaphoreType.DMA]),
        out_shape=[jax.ShapeDtypeStruct((lp, N_CHIPS * ns), F32),
                   jax.ShapeDtypeStruct((lp, 1), F32),
                   jax.ShapeDtypeStruct((lp, d), BF16)]
                  + [jax.ShapeDtypeStruct(a.shape, a.dtype) for a in gathered],
        input_output_aliases={4: 3, 5: 4, 6: 5},
        compiler_params=_params(2, 48))(order, x, lead, ln_g, w4, pw4, wo4)


def _ret_fwd(proj, cos, sin, gn_g, consts):
    lp = proj.shape[0]
    w = gn_g.shape[1]
    hd = w // RET_HEADS
    nch = lp // CHUNK
    dmask, qd, kd, cd = consts

    def body(q_ref, k_ref, v_ref, g_ref, cos_ref, sin_ref, gn_ref, m_ref, qd_ref, kd_ref, cd_ref,
             y_ref, ssave_ref, s_scr):
        @pl.when(pl.program_id(0) == 0)
        def _():
            s_scr[...] = jnp.zeros_like(s_scr)

        cos_t, sin_t = cos_ref[...], sin_ref[...]
        for h in range(RET_HEADS):
            sl = slice(h * hd, (h + 1) * hd)
            qr = _rope(q_ref[:, sl], cos_t, sin_t)
            kr = _rope(k_ref[:, sl], cos_t, sin_t) * (hd ** -0.5)
            vb = v_ref[:, sl].astype(BF16)
            sc = _dot_nt(qr.astype(BF16), kr.astype(BF16)) * m_ref[h]
            state = s_scr[h]
            sb = state.astype(BF16)
            ssave_ref[0, h] = sb
            out = _dot(sc.astype(BF16), vb) + _dot((qr * qd_ref[h]).astype(BF16), sb)
            s_scr[h] = state * cd_ref[h] + _dot_tn((kr * kd_ref[h]).astype(BF16), vb)
            dev = out - _mean(out)
            yn = dev * lax.rsqrt(_mean(dev * dev) + EPS)
            g = g_ref[:, sl]
            y_ref[:, sl] = ((yn * gn_ref[:, sl]) * (g * _sigmoid(g))).astype(BF16)

    def col(j):
        return pl.BlockSpec((CHUNK, w), lambda i: (i, j))

    def whole(a):
        return pl.BlockSpec(a.shape, lambda i: (0,) * a.ndim)

    return pl.pallas_call(
        body, name="ret_fwd", grid=(nch,),
        in_specs=[col(0), col(1), col(2), col(3),
                  pl.BlockSpec((CHUNK, hd // 2), lambda i: (i, 0)),
                  pl.BlockSpec((CHUNK, hd // 2), lambda i: (i, 0)),
                  whole(gn_g), whole(dmask), whole(qd), whole(kd), whole(cd)],
        out_specs=[pl.BlockSpec((CHUNK, w), lambda i: (i, 0)),
                   pl.BlockSpec((1, RET_HEADS, hd, hd), lambda i: (i, 0, 0, 0))],
        out_shape=[jax.ShapeDtypeStruct((lp, w), BF16),
                   jax.ShapeDtypeStruct((nch, RET_HEADS, hd, hd), BF16)],
        scratch_shapes=[pltpu.VMEM((RET_HEADS, hd, hd), F32)],
        compiler_params=_params(1, 32))(proj, proj, proj, proj, cos, sin, gn_g, dmask, qd, kd, cd)


def _tap_groups(start, flip):
    groups = {}
    for j in range(CONV_K):
        o = start + (CONV_K - 1 - j if flip else j)
        groups.setdefault(o % 8, []).append((o // 8, j))
    return groups


def _shift_up(win, s):
    return win if s == 0 else pltpu.roll(win, win.shape[0] - s, axis=0)


def _dw_taps(src_ref, w_ref, dst_ref, bias, *, rows, start, flip, rb):
    cw = dst_ref.shape[1]
    lb = min(128, cw)
    groups = _tap_groups(start, flip)

    def rb_body(r, carry):
        base = pl.multiple_of(r * rb, rb)
        for cb in range(cw // lb):
            ls = slice(cb * lb, (cb + 1) * lb)
            win = src_ref[pl.ds(base, rb + HALO), ls]
            acc = jnp.zeros((rb, lb), F32) if bias is None else jnp.broadcast_to(bias[:, ls], (rb, lb))
            for s, taps in groups.items():
                ws = _shift_up(win, s)
                for a, j in taps:
                    acc = acc + ws[8 * a:8 * a + rb, :] * w_ref[j:j + 1, ls]
            dst_ref[pl.ds(base, rb), ls] = acc
        return carry

    lax.fori_loop(0, rows // rb, rb_body, 0)


def _conv_fwd(proj, dw_w, dw_b, cln_g, cln_b, pw_w, pw_b):
    lp = proj.shape[0]
    cw = dw_b.shape[1]
    tm = _row_tile(lp, 640)
    rb = _row_tile(tm, 64)

    def body(a_ref, b_ref, gc_ref, w_ref, wb_ref, lg_ref, lb_ref, pw_ref, pb_ref, y_ref, u1_ref, buf):
        @pl.when(pl.program_id(0) == 0)
        def _():
            buf[0:HALO, :] = jnp.zeros((HALO, cw), F32)

        buf[HALO:HALO + tm, :] = a_ref[...] * _sigmoid(b_ref[...])
        _dw_taps(buf, w_ref, u1_ref, wb_ref[...], rows=tm, start=HALO - (CONV_K - 1), flip=False, rb=rb)
        buf[0:HALO, :] = buf[tm:tm + HALO, :]
        u1 = u1_ref[...]
        dev = u1 - _mean(u1)
        z = dev * lax.rsqrt(_mean(dev * dev) + EPS) * lg_ref[...] + lb_ref[...]
        u3 = (z * _sigmoid(z)).astype(BF16)
        u4 = _dot(u3, pw_ref[...]) + pb_ref[...]
        gc = gc_ref[...]
        y_ref[...] = (u4 * (gc * _sigmoid(gc))).astype(BF16)

    def col(j):
        return pl.BlockSpec((tm, cw), lambda i: (i, j))

    def whole(a):
        return pl.BlockSpec(a.shape, lambda i: (0,) * a.ndim)

    return pl.pallas_call(
        body, name="conv_fwd", grid=(lp // tm,),
        in_specs=[col(4), col(5), col(6), whole(dw_w), whole(dw_b), whole(cln_g), whole(cln_b),
                  whole(pw_w), whole(pw_b)],
        out_specs=[pl.BlockSpec((tm, cw), lambda i: (i, 0)), pl.BlockSpec((tm, cw), lambda i: (i, 0))],
        out_shape=[jax.ShapeDtypeStruct((lp, cw), BF16), jax.ShapeDtypeStruct((lp, cw), F32)],
        scratch_shapes=[pltpu.VMEM((tm + HALO, cw), F32)],
        compiler_params=_params(1, 48))(proj, proj, proj, dw_w, dw_b, cln_g, cln_b, pw_w, pw_b)


def _out_proj_loss(yr, yc, x, lead, tgt, zero_lead, w_out, final_g):
    seq, d = x.shape
    lp = CHUNK + seq
    w = yr.shape[1]
    tm = _row_tile(lp, 320)
    nt = lp // tm
    assert tm > CHUNK

    def body(yr_ref, yc_ref, x_hbm, lead_hbm, t_hbm, zlead_hbm, w_ref, fg_ref, dh2_ref, dy_ref, loss_ref, dfg_ref,
             hbuf, tbuf, hsems, tsems):
        i = pl.program_id(0)
        slot = i % 2
        streams = (_padded_tile_stream(x_hbm, lead_hbm, hbuf, hsems, tm),
                   _padded_tile_stream(t_hbm, zlead_hbm, tbuf, tsems, tm))

        @pl.when(i == 0)
        def _():
            loss_ref[...] = jnp.zeros_like(loss_ref)
            dfg_ref[...] = jnp.zeros_like(dfg_ref)
            for start_first, _, _ in streams:
                start_first(0)

        @pl.when(i + 1 < nt)
        def _():
            for _, start, _ in streams:
                start(1 - slot, i + 1)

        for _, _, wait in streams:
            wait(slot)

        h2 = hbuf[slot] + (_dot(yr_ref[...], w_ref[0:w, :]) + _dot(yc_ref[...], w_ref[w:2 * w, :]))
        r2 = lax.rsqrt(_mean(h2 * h2) + EPS)
        h2n = h2 * r2
        fg = fg_ref[...]
        rows = i * tm + lax.broadcasted_iota(jnp.int32, (tm, 1), 0)
        err = jnp.where(rows >= CHUNK, h2n * fg - tbuf[slot], 0.0)
        loss_ref[...] += _colsum(err * err)
        dout = err * (1.0 / d)
        dfg_ref[...] += _colsum(dout * h2n)
        dz = dout * fg
        dh2 = r2 * (dz - h2n * _mean(dz * h2n))
        dh2_ref[...] = dh2
        db = dh2.astype(BF16)
        dy_ref[:, 0:w] = _dot_nt(db, w_ref[0:w, :])
        dy_ref[:, w:2 * w] = _dot_nt(db, w_ref[w:2 * w, :])

    def row(cols):
        return pl.BlockSpec((tm, cols), lambda i: (i, 0))

    any_spec = pl.BlockSpec(memory_space=pl.ANY)
    return pl.pallas_call(
        body, name="out_proj_loss", grid=(nt,),
        in_specs=[row(w), row(w), any_spec, any_spec, any_spec, any_spec,
                  pl.BlockSpec(memory_space=pltpu.VMEM),
                  pl.BlockSpec((1, d), lambda i: (0, 0))],
        out_specs=[row(d), row(2 * w), pl.BlockSpec((1, d), lambda i: (0, 0)),
                   pl.BlockSpec((1, d), lambda i: (0, 0))],
        out_shape=[jax.ShapeDtypeStruct((lp, d), F32), jax.ShapeDtypeStruct((lp, 2 * w), F32),
                   jax.ShapeDtypeStruct((1, d), F32), jax.ShapeDtypeStruct((1, d), F32)],
        scratch_shapes=[pltpu.VMEM((2, tm, d), F32), pltpu.VMEM((2, tm, d), F32),
                        pltpu.SemaphoreType.DMA((2,)), pltpu.SemaphoreType.DMA((2,))],
        compiler_params=_params(1, 56))(yr, yc, x, lead, tgt, zero_lead, w_out, final_g)


def _dw_out(yr, yc, dh2):
    lp, d = dh2.shape
    w = yr.shape[1]
    tm = _row_tile(lp, 1664)
    nb = 2
    dn = d // nb

    def body(yr_ref, yc_ref, d_ref, o_ref):
        @pl.when(pl.program_id(1) == 0)
        def _():
            o_ref[...] = jnp.zeros_like(o_ref)

        db = d_ref[...].astype(BF16)
        o_ref[0:w, :] += _dot_tn(yr_ref[...], db)
        o_ref[w:2 * w, :] += _dot_tn(yc_ref[...], db)

    return pl.pallas_call(
        body, name="dw_out", grid=(nb, lp // tm),
        in_specs=[pl.BlockSpec((tm, w), lambda n, i: (i, 0)),
                  pl.BlockSpec((tm, w), lambda n, i: (i, 0)),
                  pl.BlockSpec((tm, dn), lambda n, i: (i, n))],
        out_specs=pl.BlockSpec((2 * w, dn), lambda n, i: (0, n)),
        out_shape=jax.ShapeDtypeStruct((2 * w, d), F32),
        compiler_params=_params(2, 52))(yr, yc, dh2)


def _ret_bwd(proj, dy, ssave, cos, sin, gn_g, consts):
    lp = proj.shape[0]
    w = gn_g.shape[1]
    hd = w // RET_HEADS
    nch = lp // CHUNK
    dmask, qd, kd, cd = consts

    def body(q_ref, k_ref, v_ref, g_ref, dy_ref, ss_ref, cos_ref, sin_ref, gn_ref, m_ref, qd_ref, kd_ref,
             cd_ref, dp_ref, dgn_ref, ds_scr):
        @pl.when(pl.program_id(0) == 0)
        def _():
            ds_scr[...] = jnp.zeros_like(ds_scr)
            dgn_ref[...] = jnp.zeros_like(dgn_ref)

        cos_t, sin_t = cos_ref[...], sin_ref[...]
        for h in range(RET_HEADS):
            sl = slice(h * hd, (h + 1) * hd)
            qr = _rope(q_ref[:, sl], cos_t, sin_t)
            kr = _rope(k_ref[:, sl], cos_t, sin_t) * (hd ** -0.5)
            qb, kb = qr.astype(BF16), kr.astype(BF16)
            vb = v_ref[:, sl].astype(BF16)
            sb = ss_ref[0, h]
            mask = m_ref[h]
            qdec, kdec = qd_ref[h], kd_ref[h]
            scb = (_dot_nt(qb, kb) * mask).astype(BF16)
            qdq = (qr * qdec).astype(BF16)
            kdk = (kr * kdec).astype(BF16)
            out = _dot(scb, vb) + _dot(qdq, sb)
            dev = out - _mean(out)
            rstd = lax.rsqrt(_mean(dev * dev) + EPS)
            yn = dev * rstd
            g = g_ref[:, sl]
            sg = _sigmoid(g)
            gng = gn_ref[:, sl]
            dyv = dy_ref[:, sl]
            dgr = dyv * (yn * gng) * _dsilu(g, sg)
            silu_g = g * sg
            dgn_ref[:, sl] += _colsum(dyv * yn * silu_g)
            dyn = dyv * gng * silu_g
            dout = rstd * (dyn - _mean(dyn) - yn * _mean(dyn * yn))
            dob = dout.astype(BF16)
            dscb = (_dot_nt(dob, vb) * mask).astype(BF16)
            dstate = ds_scr[h]
            dsb = dstate.astype(BF16)
            dq = _dot(dscb, kb) + _dot_nt(dob, sb) * qdec
            dk = _dot_tn(dscb, qb) + _dot_nt(vb, dsb) * kdec
            dv = _dot_tn(scb, dob) + _dot(kdk, dsb)
            ds_scr[h] = dstate * cd_ref[h] + _dot_tn(qdq, dob)
            dp_ref[:, 0 * w + h * hd:0 * w + (h + 1) * hd] = _rope_t(dq, cos_t, sin_t).astype(BF16)
            dp_ref[:, 1 * w + h * hd:1 * w + (h + 1) * hd] = (_rope_t(dk, cos_t, sin_t) * (hd ** -0.5)).astype(BF16)
            dp_ref[:, 2 * w + h * hd:2 * w + (h + 1) * hd] = dv.astype(BF16)
            dp_ref[:, 3 * w + h * hd:3 * w + (h + 1) * hd] = dgr.astype(BF16)

    def rev(i):
        return nch - 1 - i

    def col(j):
        return pl.BlockSpec((CHUNK, w), lambda i: (rev(i), j))

    def whole(a):
        return pl.BlockSpec(a.shape, lambda i: (0,) * a.ndim)

    return pl.pallas_call(
        body, name="ret_bwd", grid=(nch,),
        in_specs=[col(0), col(1), col(2), col(3),
                  pl.BlockSpec((CHUNK, w), lambda i: (rev(i), 0)),
                  pl.BlockSpec((1, RET_HEADS, hd, hd), lambda i: (rev(i), 0, 0, 0)),
                  pl.BlockSpec((CHUNK, hd // 2), lambda i: (rev(i), 0)),
                  pl.BlockSpec((CHUNK, hd // 2), lambda i: (rev(i), 0)),
                  whole(gn_g), whole(dmask), whole(qd), whole(kd), whole(cd)],
        out_specs=[pl.BlockSpec((CHUNK, 4 * w), lambda i: (rev(i), 0)),
                   pl.BlockSpec((1, w), lambda i: (0, 0))],
        out_shape=[jax.ShapeDtypeStruct((lp, 7 * w), BF16), jax.ShapeDtypeStruct((1, w), F32)],
        scratch_shapes=[pltpu.VMEM((RET_HEADS, hd, hd), F32)],
        compiler_params=_params(1, 32))(proj, proj, proj, proj, dy, ssave, cos, sin, gn_g, dmask, qd, kd, cd)


def _conv_bwd_pw(dy, proj, u1, cln_g, cln_b, pw_w, pw_b, dproj):
    lp, cw = u1.shape
    tm = _row_tile(lp, 320)

    def body(dy_ref, gc_ref, u1_ref, lg_ref, lb_ref, pw_ref, pb_ref, dp_in, dp_ref, du1_ref, dpw_ref,
             dpb_ref, dlg_ref, dlb_ref):
        del dp_in

        @pl.when(pl.program_id(0) == 0)
        def _():
            dpw_ref[...] = jnp.zeros_like(dpw_ref)
            dpb_ref[...] = jnp.zeros_like(dpb_ref)
            dlg_ref[...] = jnp.zeros_like(dlg_ref)
            dlb_ref[...] = jnp.zeros_like(dlb_ref)

        u1 = u1_ref[...]
        dev = u1 - _mean(u1)
        rstd = lax.rsqrt(_mean(dev * dev) + EPS)
        u1n = dev * rstd
        lg = lg_ref[...]
        z = u1n * lg + lb_ref[...]
        sz = _sigmoid(z)
        u3b = (z * sz).astype(BF16)
        u4 = _dot(u3b, pw_ref[...]) + pb_ref[...]
        gc = gc_ref[...]
        sgc = _sigmoid(gc)
        dyc = dy_ref[...]
        du4 = dyc * (gc * sgc)
        dp_ref[...] = (dyc * u4 * _dsilu(gc, sgc)).astype(BF16)
        du4b = du4.astype(BF16)
        dpb_ref[...] += _colsum(du4)
        dpw_ref[...] += _dot_tn(u3b, du4b)
        dz = _dot_nt(du4b, pw_ref[...]) * _dsilu(z, sz)
        dlg_ref[...] += _colsum(dz * u1n)
        dlb_ref[...] += _colsum(dz)
        dn = dz * lg
        du1_ref[...] = rstd * (dn - _mean(dn) - u1n * _mean(dn * u1n))

    def row(j):
        return pl.BlockSpec((tm, cw), lambda i: (i, j))

    def whole(a):
        return pl.BlockSpec(a.shape, lambda i: (0,) * a.ndim)

    def acc(r):
        return pl.BlockSpec((r, cw), lambda i: (0, 0))

    return pl.pallas_call(
        body, name="conv_bwd_pw", grid=(lp // tm,),
        in_specs=[row(1), row(6), row(0), whole(cln_g), whole(cln_b), whole(pw_w), whole(pw_b),
                  pl.BlockSpec(memory_space=pl.ANY)],
        out_specs=[row(6), row(0), acc(cw), acc(1), acc(1), acc(1)],
        out_shape=[jax.ShapeDtypeStruct(dproj.shape, dproj.dtype), jax.ShapeDtypeStruct((lp, cw), F32),
                   jax.ShapeDtypeStruct((cw, cw), F32), jax.ShapeDtypeStruct((1, cw), F32),
                   jax.ShapeDtypeStruct((1, cw), F32), jax.ShapeDtypeStruct((1, cw), F32)],
        input_output_aliases={7: 0},
        compiler_params=_params(1, 48))(dy, proj, u1, cln_g, cln_b, pw_w, pw_b, dproj)


def _conv_bwd_dw(du1, proj, dw_w, dproj):
    lp, cw = du1.shape
    tm = _row_tile(lp, 640)
    rb = _row_tile(tm, 64)
    nt = lp // tm
    hb = tm // HALO

    def body(a_ref, b_ref, du_ref, nx_ref, w_ref, dp_in, dp_ref, dww_ref, dwb_ref, ubuf, dbuf, du0, acc):
        del dp_in
        i = pl.program_id(0)

        @pl.when(i == 0)
        def _():
            ubuf[0:HALO, :] = jnp.zeros((HALO, cw), F32)
            acc[...] = jnp.zeros_like(acc)
            dwb_ref[...] = jnp.zeros_like(dwb_ref)

        a = a_ref[...]
        sb = _sigmoid(b_ref[...])
        ubuf[HALO:HALO + tm, :] = a * sb
        du = du_ref[...]
        dbuf[0:tm, :] = du
        dbuf[tm:tm + HALO, :] = jnp.where(i == nt - 1, 0.0, nx_ref[...])
        dwb_ref[...] += _colsum(du)
        _dw_taps(dbuf, w_ref, du0, None, rows=tm, start=0, flip=True, rb=rb)
        d0 = du0[...]
        dp_ref[:, 0:cw] = (d0 * sb).astype(BF16)
        dp_ref[:, cw:2 * cw] = (d0 * a * sb * (1.0 - sb)).astype(BF16)

        lb = min(128, cw)
        groups = _tap_groups(HALO - (CONV_K - 1), False)

        def rb_body(r, carry):
            base = pl.multiple_of(r * rb, rb)
            for cb in range(cw // lb):
                ls = slice(cb * lb, (cb + 1) * lb)
                win = ubuf[pl.ds(base, rb + HALO), ls]
                dv = dbuf[pl.ds(base, rb), ls]
                for s, taps in groups.items():
                    ws = _shift_up(win, s)
                    for a, j in taps:
                        prod = dv * ws[8 * a:8 * a + rb, :]
                        acc[8 * j:8 * j + 8, ls] += jnp.sum(prod.reshape(rb // 8, 8, lb), axis=0)
            return carry

        lax.fori_loop(0, tm // rb, rb_body, 0)
        ubuf[0:HALO, :] = ubuf[tm:tm + HALO, :]

        @pl.when(i == nt - 1)
        def _():
            for j in range(CONV_K):
                dww_ref[j:j + 1, :] = _colsum(acc[8 * j:8 * j + 8, :])
            dww_ref[CONV_K:HALO, :] = jnp.zeros((HALO - CONV_K, cw), F32)

    def col(j):
        return pl.BlockSpec((tm, cw), lambda i: (i, j))

    return pl.pallas_call(
        body, name="conv_bwd_dw", grid=(nt,),
        in_specs=[col(4), col(5), col(0),
                  pl.BlockSpec((HALO, cw), lambda i: (jnp.minimum((i + 1) * hb, nt * hb - 1), 0)),
                  pl.BlockSpec(dw_w.shape, lambda i: (0, 0)),
                  pl.BlockSpec(memory_space=pl.ANY)],
        out_specs=[pl.BlockSpec((tm, 2 * cw), lambda i: (i, 2)),
                   pl.BlockSpec((HALO, cw), lambda i: (0, 0)),
                   pl.BlockSpec((1, cw), lambda i: (0, 0))],
        out_shape=[jax.ShapeDtypeStruct(dproj.shape, dproj.dtype), jax.ShapeDtypeStruct((HALO, cw), F32),
                   jax.ShapeDtypeStruct((1, cw), F32)],
        scratch_shapes=[pltpu.VMEM((tm + HALO, cw), F32), pltpu.VMEM((tm + HALO, cw), F32),
                        pltpu.VMEM((tm, cw), F32), pltpu.VMEM((8 * HALO, cw), F32)],
        input_output_aliases={5: 0},
        compiler_params=_params(1, 56))(proj, proj, du1, du1, dw_w, dproj)


def _dw_in(hn, dproj, ns, others):
    lp, d = hn.shape
    tm = _row_tile(lp, 1664)
    nt = lp // tm
    mb = 512 if d % 512 == 0 else d
    n = len(others)
    halves = [d // 2] + [g.shape[1] // 2 for g in others]

    def body(hn_ref, dp_ref, *refs):
        other_refs, o_hbm, recv_refs = refs[:n], refs[n], refs[n + 1:2 * n + 2]
        acc, sem, send_sems, recv_sems = refs[2 * n + 2:]
        s, i = pl.program_id(0), pl.program_id(1)
        x, y, c = _mesh_pos()

        def to_sibling(src, dst, k):
            return pltpu.make_async_remote_copy(
                src_ref=src, dst_ref=dst, send_sem=send_sems.at[k], recv_sem=recv_sems.at[k],
                device_id=(x, y, 1 - c), device_id_type=MESH)

        def shard_half(p):
            return to_sibling(o_hbm.at[p, pl.ds((1 - c) * halves[0], halves[0])], recv_refs[0].at[p], p)

        def other_halves(k):
            return to_sibling(other_refs[k].at[:, pl.ds((1 - c) * halves[1 + k], halves[1 + k])],
                              recv_refs[1 + k], N_CHIPS + k)

        @pl.when((s == 0) & (i == 0))
        def _():
            for k in range(n):
                other_halves(k).start()

        @pl.when(i == 0)
        def _():
            acc[...] = jnp.zeros_like(acc)

        for m in range(d // mb):
            rows = slice(m * mb, (m + 1) * mb)
            acc[rows, :] += _dot_tn(hn_ref[:, rows], dp_ref[...])

        @pl.when(i == nt - 1)
        def _():
            cp = pltpu.make_async_copy(acc, o_hbm.at[s], sem)
            cp.start()
            cp.wait()
            shard_half(s).start()

        @pl.when((s == N_CHIPS - 1) & (i == nt - 1))
        def _():
            for p in range(N_CHIPS):
                shard_half(p).wait()
            for k in range(n):
                other_halves(k).wait()

    any_spec = pl.BlockSpec(memory_space=pl.ANY)
    outs = pl.pallas_call(
        body, name="dw_in", grid=(N_CHIPS, nt),
        in_specs=[pl.BlockSpec((tm, d), lambda s, i: (i, 0)),
                  pl.BlockSpec((tm, ns), lambda s, i: (i, s))] + [any_spec] * n,
        out_specs=[any_spec] * (n + 2),
        out_shape=[jax.ShapeDtypeStruct((N_CHIPS, d, ns), F32), jax.ShapeDtypeStruct((N_CHIPS, d // 2, ns), F32)]
                  + [jax.ShapeDtypeStruct((N_CHIPS, g.shape[1] // 2) + g.shape[2:], g.dtype) for g in others],
        scratch_shapes=[pltpu.VMEM((d, ns), F32), pltpu.SemaphoreType.DMA,
                        pltpu.SemaphoreType.DMA((N_CHIPS + n,)), pltpu.SemaphoreType.DMA((N_CHIPS + n,))],
        compiler_params=_params(2, 56))(hn, dproj, *others)
    return outs[0], outs[1:]


def _in_proj_bwd(dproj, w4, x, lead, r1, dh2, ln_g, cs):
    seq, d = x.shape
    lp = CHUNK + seq
    ns = w4.shape[2]
    tm = _row_tile(lp, 320)
    nt = lp // tm
    n = len(cs)
    assert tm > CHUNK

    def body(dp_ref, w_ref, x_hbm, lead_hbm, r_ref, d2_ref, g_ref, *refs):
        cs_refs, (dh_ref, dlg_ref), rb_refs = refs[:n], refs[n:n + 2], refs[n + 2:2 * n + 2]
        hbuf, h_sems, send_sems, recv_sems = refs[2 * n + 2:]
        i = pl.program_id(0)
        slot = i % 2
        x, y, c = _mesh_pos()
        h_first, h_start, h_wait = _padded_tile_stream(x_hbm, lead_hbm, hbuf, h_sems, tm)

        @pl.when(i == 0)
        def _():
            h_first(0)

        @pl.when(i + 1 < nt)
        def _():
            h_start(1 - slot, i + 1)

        def exchange():
            return [pltpu.make_async_remote_copy(
                src_ref=cs_refs[k].at[2 * chip[0] + chip[1]], dst_ref=rb_refs[k].at[j],
                send_sem=send_sems.at[3 * k + j], recv_sem=recv_sems.at[3 * k + j],
                device_id=(*chip, c), device_id_type=MESH)
                for k in range(n) for j, chip in enumerate(_other_chips(x, y))]

        @pl.when(i == 0)
        def _():
            dlg_ref[...] = jnp.zeros_like(dlg_ref)
            for cp in exchange():
                cp.start()

        dhn = _dot_nt(dp_ref[:, 0:ns], w_ref[0])
        for s in range(1, N_CHIPS):
            dhn = dhn + _dot_nt(dp_ref[:, s * ns:(s + 1) * ns], w_ref[s])
        r = r_ref[...]
        h_wait(slot)
        hn0 = hbuf[slot] * r
        dlg_ref[...] += _colsum(dhn * hn0)
        t = dhn * g_ref[...]
        dh_ref[...] = d2_ref[...] + r * (t - hn0 * _mean(t * hn0))

        @pl.when(i == nt - 1)
        def _():
            for cp in exchange():
                cp.wait()

    def row(cols):
        return pl.BlockSpec((tm, cols), lambda i: (i, 0))

    any_spec = pl.BlockSpec(memory_space=pl.ANY)
    outs = pl.pallas_call(
        body, name="in_proj_bwd", grid=(nt,),
        in_specs=[row(N_CHIPS * ns), pl.BlockSpec(memory_space=pltpu.VMEM),
                  any_spec, any_spec, row(1), row(d), pl.BlockSpec((1, d), lambda i: (0, 0))] + [any_spec] * n,
        out_specs=[row(d), pl.BlockSpec((1, d), lambda i: (0, 0))] + [any_spec] * n,
        out_shape=[jax.ShapeDtypeStruct((lp, d), F32), jax.ShapeDtypeStruct((1, d), F32)]
                  + [jax.ShapeDtypeStruct((3,) + a.shape[1:], a.dtype) for a in cs],
        scratch_shapes=[pltpu.VMEM((2, tm, d), F32), pltpu.SemaphoreType.DMA((2,)),
                        pltpu.SemaphoreType.DMA((3 * n,)), pltpu.SemaphoreType.DMA((3 * n,))],
        compiler_params=_params(1, 58))(dproj, w4, x, lead, r1, dh2, ln_g, *cs)
    return outs[0], outs[1], outs[2:]


def _mesh_pos():
    return lax.axis_index("x"), lax.axis_index("y"), lax.axis_index("c")


def _other_chips(x, y):
    return [(1 - x, y), (x, 1 - y), (1 - x, 1 - y)]


def _gather_shards(shards):
    n = len(shards)
    halves = [a.shape[0] // 2 for a in shards]

    def body(*refs):
        ins, outs = refs[:n], refs[n:2 * n]
        send_sems, recv_sems, loc_sems = refs[2 * n:]
        x, y, c = _mesh_pos()
        me, sibling = (x, y, c), (x, y, 1 - c)
        my_s = 2 * x + y
        chips = _other_chips(x, y)

        def half(k, s, cc):
            return outs[k].at[s, pl.ds(cc * halves[k], halves[k])]

        def rcopy(k, j, src, dst, to):
            return pltpu.make_async_remote_copy(
                src_ref=src, dst_ref=dst, send_sem=send_sems.at[6 * k + j], recv_sem=recv_sems.at[6 * k + j],
                device_id=to, device_id_type=MESH)

        local = [pltpu.make_async_copy(ins[k], outs[k].at[my_s], loc_sems.at[k]) for k in range(n)]
        for cp in local:
            cp.start()
        started = []
        for k in range(n):
            for j, chip in enumerate(chips):
                cp = rcopy(k, j, ins[k].at[pl.ds(c * halves[k], halves[k])], half(k, my_s, c), (*chip, c))
                cp.start()
                started.append(cp)
        for j, chip in enumerate(chips):
            s_j = 2 * chip[0] + chip[1]
            for k in range(n):
                rcopy(k, j, half(k, s_j, c), half(k, s_j, c), me).wait_recv()
                cp = rcopy(k, 3 + j, half(k, s_j, c), half(k, s_j, c), sibling)
                cp.start()
                started.append(cp)
        for j, chip in enumerate(chips):
            s_j = 2 * chip[0] + chip[1]
            for k in range(n):
                rcopy(k, 3 + j, half(k, s_j, 1 - c), half(k, s_j, 1 - c), me).wait_recv()
        for cp in started:
            cp.wait_send()
        for cp in local:
            cp.wait()

    return pl.pallas_call(
        body, name="gather_weights",
        in_specs=[pl.BlockSpec(memory_space=pl.ANY)] * n,
        out_specs=[pl.BlockSpec(memory_space=pl.ANY)] * n,
        out_shape=[jax.ShapeDtypeStruct((N_CHIPS,) + a.shape, a.dtype) for a in shards],
        scratch_shapes=[pltpu.SemaphoreType.DMA((6 * n,)), pltpu.SemaphoreType.DMA((6 * n,)),
                        pltpu.SemaphoreType.DMA((n,))],
    )(*shards)


def _rs_pair_share(fulls):
    n = len(fulls)
    halves = [f.shape[0] // 2 for f in fulls]

    def body(*refs):
        outs = refs[n:2 * n]
        send_sems, recv_sems = refs[2 * n:]
        x, y, c = _mesh_pos()

        def copy(k, cc, to):
            rows = outs[k].at[pl.ds(cc * halves[k], halves[k])]
            return pltpu.make_async_remote_copy(
                src_ref=rows, dst_ref=rows, send_sem=send_sems.at[k], recv_sem=recv_sems.at[k],
                device_id=to, device_id_type=MESH)

        cps = [copy(k, c, (x, y, 1 - c)) for k in range(n)]
        for cp in cps:
            cp.start()
        for k in range(n):
            copy(k, 1 - c, (x, y, c)).wait_recv()
        for cp in cps:
            cp.wait_send()

    return pl.pallas_call(
        body, name="rs_pair_share",
        in_specs=[pl.BlockSpec(memory_space=pl.ANY)] * n,
        out_specs=[pl.BlockSpec(memory_space=pl.ANY)] * n,
        out_shape=[jax.ShapeDtypeStruct(f.shape, f.dtype) for f in fulls],
        input_output_aliases={k: k for k in range(n)},
        scratch_shapes=[pltpu.SemaphoreType.DMA((n,)), pltpu.SemaphoreType.DMA((n,))],
    )(*fulls)


def _pair_sum(g, recv, pos, name):
    _, rows, cols = g.shape
    h = rows // 2
    tr = _row_tile(h, 256)
    nb = h // tr

    def body(pos_ref, g_ref, r_ref, o_ref, own_ref):
        total = g_ref[0] + r_ref[0]
        o_ref[0] = total.astype(BF16)

        @pl.when(pl.program_id(1) == pos_ref[1])
        def _():
            own_ref[...] = total

    return pl.pallas_call(
        body, name=name,
        grid_spec=pltpu.PrefetchScalarGridSpec(
            num_scalar_prefetch=1, grid=(nb, N_CHIPS),
            in_specs=[pl.BlockSpec((1, tr, cols), lambda r, s, pos_ref: (s, pos_ref[0] * nb + r, 0)),
                      pl.BlockSpec((1, tr, cols), lambda r, s, pos_ref: (s, r, 0))],
            out_specs=[pl.BlockSpec((1, tr, cols), lambda r, s, pos_ref: (s, r, 0)),
                       pl.BlockSpec((tr, cols), lambda r, s, pos_ref: (r, 0))]),
        out_shape=[jax.ShapeDtypeStruct((N_CHIPS, h, cols), BF16), jax.ShapeDtypeStruct((h, cols), F32)],
        compiler_params=_params(2, 32))(pos, g, recv)


def _chip_sum(own, rb, pos, name):
    h, cols = own.shape
    tr = _row_tile(h, 256)
    nb = h // tr

    def body(pos_ref, c_ref, r_ref, o_ref):
        del pos_ref
        o_ref[...] = ((c_ref[...] + r_ref[0].astype(F32)) + r_ref[1].astype(F32)) + r_ref[2].astype(F32)

    return pl.pallas_call(
        body, name=name,
        grid_spec=pltpu.PrefetchScalarGridSpec(
            num_scalar_prefetch=1, grid=(nb,),
            in_specs=[pl.BlockSpec((tr, cols), lambda r, pos_ref: (r, 0)),
                      pl.BlockSpec((3, tr, cols), lambda r, pos_ref: (0, r, 0))],
            out_specs=pl.BlockSpec((tr, cols), lambda r, pos_ref: (pos_ref[0] * nb + r, 0))),
        out_shape=jax.ShapeDtypeStruct((2 * h, cols), F32),
        compiler_params=_params(1, 32))(pos, own, rb)


def _adamw_big(w, g, m, v, name):
    rows, cols = w.shape
    tr = _row_tile(rows, 256)

    def body(w_ref, g_ref, m_ref, v_ref, d_ref, nm_ref, nv_ref):
        d_ref[...], nm_ref[...], nv_ref[...] = _adamw(w_ref[...], g_ref[...], m_ref[...], v_ref[...])

    spec = pl.BlockSpec((tr, cols), lambda i: (i, 0))
    return pl.pallas_call(
        body, name=name, grid=(rows // tr,),
        in_specs=[spec] * 4, out_specs=[spec] * 3,
        out_shape=[jax.ShapeDtypeStruct((rows, cols), F32)] * 3,
        compiler_params=_params(1, 48))(w, g, m, v)


def _gather_small(loss, dfg, dlg, dgn, ddwb, dclg, dclb, dpwb, ddww, dmeta):
    d = loss.shape[1]
    w = dgn.shape[1]
    wc, mc = ddww.shape[1] // N_CHIPS, dmeta.shape[1] // N_CHIPS

    def body(loss_ref, dfg_ref, dlg_ref, dgn_ref, ddwb_ref, dclg_ref, dclb_ref, dpwb_ref, ddww_ref, dmeta_ref,
             gs_ref, gd_ref, gm_ref, send_sems, recv_sems, loc_sems):
        x, y, c = _mesh_pos()
        me = 4 * x + 2 * y + c
        gs_ref[me, 0:1, :] = loss_ref[...]
        gs_ref[me, 1:2, :] = dfg_ref[...]
        gs_ref[me, 2:3, :] = dlg_ref[...]
        gs_ref[me, 3:4, 0:w] = dgn_ref[...]
        gs_ref[me, 3:4, w:2 * w] = ddwb_ref[...]
        gs_ref[me, 4:5, 0:w] = dclg_ref[...]
        gs_ref[me, 4:5, w:2 * w] = dclb_ref[...]
        gs_ref[me, 5:6, 0:w] = dpwb_ref[...]
        gs_ref[me, 5:6, w:2 * w] = jnp.zeros((1, d - w), F32)
        gs_ref[me, 6:8, :] = jnp.zeros((2, d), F32)
        bufs = (gs_ref, gd_ref, gm_ref)

        def mine_for(k, shard):
            if k == 0:
                return gs_ref.at[me]
            ref, width = ((ddww_ref, wc), (dmeta_ref, mc))[k - 1]
            return ref.at[:, pl.ds(pl.multiple_of(shard * width, width), width)]

        def peer(j):
            return (1 - x if j & 4 else x), (1 - y if j & 2 else y), (1 - c if j & 1 else c)

        def copy(k, j, src, slot, to):
            return pltpu.make_async_remote_copy(
                src_ref=src, dst_ref=bufs[k].at[slot],
                send_sem=send_sems.at[7 * k + j - 1], recv_sem=recv_sems.at[7 * k + j - 1],
                device_id=to, device_id_type=MESH)

        own = [pltpu.make_async_copy(mine_for(k, 2 * x + y), bufs[k].at[me], loc_sems.at[k - 1]) for k in (1, 2)]
        for cp in own:
            cp.start()
        cps = []
        for k in range(3):
            for j in range(1, N_DEV):
                px, py, pc = peer(j)
                cp = copy(k, j, mine_for(k, 2 * px + py), me, (px, py, pc))
                cp.start()
                cps.append(cp)
        for k in range(3):
            for j in range(1, N_DEV):
                px, py, pc = peer(j)
                slot = 4 * px + 2 * py + pc
                copy(k, j, bufs[k].at[slot], slot, (x, y, c)).wait_recv()
        for cp in cps:
            cp.wait_send()
        for cp in own:
            cp.wait()

    vm = pl.BlockSpec(memory_space=pltpu.VMEM)
    return pl.pallas_call(
        body, name="gather_small",
        in_specs=[vm] * 10, out_specs=[vm] * 3,
        out_shape=[jax.ShapeDtypeStruct((N_DEV, 8, d), F32),
                   jax.ShapeDtypeStruct((N_DEV, ddww.shape[0], wc), F32),
                   jax.ShapeDtypeStruct((N_DEV, dmeta.shape[0], mc), F32)],
        scratch_shapes=[pltpu.SemaphoreType.DMA((21,)), pltpu.SemaphoreType.DMA((21,)),
                        pltpu.SemaphoreType.DMA((2,))],
    )(loss, dfg, dlg, dgn, ddwb, dclg, dclb, dpwb, ddww, dmeta)


def _small_update(gs, gd, gm, weights, ms, vs):
    d = gs.shape[2]
    w = d // 2
    n = len(weights)

    def body(gs_ref, gd_ref, gm_ref, *refs):
        w_refs, m_refs, v_refs = refs[:n], refs[n:2 * n], refs[2 * n:3 * n]
        loss_ref = refs[3 * n]
        g_refs = refs[3 * n + 1:4 * n + 1]
        d_refs = refs[4 * n + 1:5 * n + 1]
        nm_refs = refs[5 * n + 1:6 * n + 1]
        nv_refs = refs[6 * n + 1:7 * n + 1]

        def total(ref):
            t = ref[0]
            for dev in range(1, N_DEV):
                t = t + ref[dev]
            return t

        packed = total(gs_ref)
        loss_ref[...] = jnp.sum(packed[0:1, :], axis=1, keepdims=True) * (0.5 / d)
        grads = [packed[2:3, :], packed[1:2, :], packed[3:4, 0:w], packed[3:4, w:2 * w], packed[4:5, 0:w],
                 packed[4:5, w:2 * w], packed[5:6, 0:w], total(gd_ref), total(gm_ref)]
        for k in range(n):
            g = grads[k]
            g_refs[k][...] = g
            d_refs[k][...], nm_refs[k][...], nv_refs[k][...] = _adamw(w_refs[k][...], g, m_refs[k][...], v_refs[k][...])

    def whole(shape):
        return pl.BlockSpec(shape, lambda i: (0,) * len(shape))

    shapes = [a.shape for a in weights]
    in_specs = [whole(gs.shape), whole(gd.shape), whole(gm.shape)] + [whole(s) for s in shapes] * 3
    out_specs = [whole((1, 1))] + [whole(s) for s in shapes] * 4
    out_shape = [jax.ShapeDtypeStruct((1, 1), F32)] + [jax.ShapeDtypeStruct(s, F32) for s in shapes] * 4
    outs = pl.pallas_call(
        body, name="small_update", grid=(1,), in_specs=in_specs, out_specs=out_specs, out_shape=out_shape,
        compiler_params=_params(1, 32))(gs, gd, gm, *weights, *ms, *vs)
    loss = outs[0]
    return loss, outs[1:n + 1], outs[n + 1:2 * n + 1], outs[2 * n + 1:3 * n + 1], outs[3 * n + 1:4 * n + 1]


def kernel(x, meta_tokens, ln_g, w_in, ret_gn_g, conv_dw_w, conv_dw_b, conv_ln_g, conv_ln_b, conv_pw_w, conv_pw_b, w_out, final_g, loss_target, m_meta_tokens, m_ln_g, m_w_in, m_ret_gn_g, m_conv_dw_w, m_conv_dw_b, m_conv_ln_g, m_conv_ln_b, m_conv_pw_w, m_conv_pw_b, m_w_out, m_final_g, v_meta_tokens, v_ln_g, v_w_in, v_ret_gn_g, v_conv_dw_w, v_conv_dw_b, v_conv_ln_g, v_conv_ln_b, v_conv_pw_w, v_conv_pw_b, v_w_out, v_final_g):
    seq, d = x.shape[1], x.shape[2]
    w = ret_gn_g.shape[1]
    hd = w // RET_HEADS
    lp = CHUNK + seq
    ns = w_in.shape[2]
    mx, my, mc = lax.axis_index("x"), lax.axis_index("y"), lax.axis_index("c")
    my_s = 2 * mx + my
    s_arr = my_s.astype(jnp.int32).reshape(1)
    pos = jnp.stack([mc, my_s]).astype(jnp.int32)
    order = jnp.stack([my_s, 2 * (1 - mx) + my, 2 * mx + (1 - my), 2 * (1 - mx) + (1 - my)]).astype(jnp.int32)

    dw_pad = jnp.pad(conv_dw_w[0], ((0, HALO - CONV_K), (0, 0)))
    dw4, meta4 = _gather_shards([dw_pad, meta_tokens])
    dw_full = dw4.transpose(1, 0, 2).reshape(HALO, w)
    meta_full = meta4.transpose(1, 0, 2).reshape(N_META, d)

    lead = jnp.concatenate([jnp.zeros((LEAD, d), F32), meta_full], axis=0)
    zero_lead = jnp.zeros((CHUNK, d), F32)
    consts = _ret_consts()
    cos, sin = _rope_tables(lp, hd // 2)
    fg2 = final_g.reshape(1, d)

    proj, r1, hn, w4, pw4, wo4 = _in_proj_gather(
        order, x[0], lead, ln_g, _cast_into_gathered(w_in[0], s_arr, "cast_w_in"),
        _cast_into_gathered(conv_pw_w[0], s_arr, "cast_pw_w"), _cast_into_gathered(w_out[0], s_arr, "cast_w_out"))
    pw_full = pw4.reshape(w, w)
    wo_full = wo4.reshape(2 * w, d)
    y_ret, ssave = _ret_fwd(proj, cos, sin, ret_gn_g, consts)
    y_conv, u1 = _conv_fwd(proj, dw_full, conv_dw_b, conv_ln_g, conv_ln_b, pw_full, conv_pw_b)
    dh2, dy, loss_l, dfg = _out_proj_loss(y_ret, y_conv, x[0], lead, loss_target[0], zero_lead, wo_full, fg2)

    g_wo = _dw_out(y_ret, y_conv, dh2)
    dproj, dgn = _ret_bwd(proj, dy, ssave, cos, sin, ret_gn_g, consts)
    dproj, du1, g_pw, dpwb, dclg, dclb = _conv_bwd_pw(dy, proj, u1, conv_ln_g, conv_ln_b, pw_full, conv_pw_b, dproj)
    dproj, ddww, ddwb = _conv_bwd_dw(du1, proj, dw_full, dproj)
    g_wo4 = g_wo.reshape(N_CHIPS, (2 * w) // N_CHIPS, d)
    g_pw4 = g_pw.reshape(N_CHIPS, w // N_CHIPS, w)
    g_win, recv = _dw_in(hn, dproj, ns, [g_wo4, g_pw4])
    gs = [g_win, g_wo4, g_pw4]
    names = ("w_in", "w_out", "pw_w")
    sums = [_pair_sum(g, r, pos, "pair_sum_" + nm) for g, r, nm in zip(gs, recv, names)]
    dh, dlg, rb = _in_proj_bwd(dproj, w4, x[0], lead, r1, dh2, ln_g, [cs_ for cs_, _ in sums])
    grad_x = dh[CHUNK:][None]
    dmeta = dh[LEAD:CHUNK]
    fulls = [_chip_sum(own, r, pos, "chip_sum_" + nm) for (_, own), r, nm in zip(sums, rb, names)]
    grad_w_in, grad_w_out, grad_pw = _rs_pair_share(fulls)
    d_win, nm_win, nv_win = _adamw_big(w_in[0], grad_w_in, m_w_in[0], v_w_in[0], "adamw_w_in")
    d_wo, nm_wo, nv_wo = _adamw_big(w_out[0], grad_w_out, m_w_out[0], v_w_out[0], "adamw_w_out")
    d_pw, nm_pw, nv_pw = _adamw_big(conv_pw_w[0], grad_pw, m_conv_pw_w[0], v_conv_pw_w[0], "adamw_pw_w")

    gsm, gdm, gmm = _gather_small(loss_l, dfg, dlg, dgn, ddwb, dclg, dclb, dpwb, ddww, dmeta)

    def pad_dw(a):
        return jnp.pad(a[0], ((0, HALO - CONV_K), (0, 0)))

    small_w = [ln_g, fg2, ret_gn_g, conv_dw_b, conv_ln_g, conv_ln_b, conv_pw_b, dw_pad, meta_tokens]
    small_m = [m_ln_g, m_final_g.reshape(1, d), m_ret_gn_g, m_conv_dw_b, m_conv_ln_g, m_conv_ln_b, m_conv_pw_b,
               pad_dw(m_conv_dw_w), m_meta_tokens]
    small_v = [v_ln_g, v_final_g.reshape(1, d), v_ret_gn_g, v_conv_dw_b, v_conv_ln_g, v_conv_ln_b, v_conv_pw_b,
               pad_dw(v_conv_dw_w), v_meta_tokens]
    loss, sg, sd, snm, snv = _small_update(gsm, gdm, gmm, small_w, small_m, small_v)

    def assemble(small, big_in, big_pw, big_out):
        ln, fg, gn, dwb, clg, clb, pwb, dww, meta = small
        return (meta, ln, big_in[None], gn, dww[:CONV_K][None], dwb, clg, clb, big_pw[None], pwb, big_out[None],
                fg.reshape(d))

    return (loss.reshape(()), grad_x,
            *assemble(sg, grad_w_in, grad_pw, grad_w_out),
            *assemble(sd, d_win, d_pw, d_wo),
            *assemble(snm, nm_win, nm_pw, nm_wo),
            *assemble(snv, nv_win, nv_pw, nv_wo))
```

```python
import functools

import jax
import jax.numpy as jnp
from jax import lax
from jax.experimental import pallas as pl
from jax.experimental.pallas import tpu as pltpu

F32 = jnp.float32
BF16 = jnp.bfloat16
MESH = pl.DeviceIdType.MESH

N_META = 16
CHUNK = 128
LEAD = (-N_META) % CHUNK
RET_HEADS = 4
CONV_K = 31
HALO = 32
ROPE_BASE = 10000.0
EPS = 1e-6
N_CHIPS = 4
N_DEV = 8

ADAM_LR = 0.001
ADAM_B1 = 0.9
ADAM_B2 = 0.999
ADAM_EPS = 1e-08
ADAM_WD = 0.01
ADAM_STEP = 10

MIB = 2 ** 20


def _params(n_grid_axes, vmem_mib):
    return pltpu.CompilerParams(dimension_semantics=("arbitrary",) * n_grid_axes,
                                vmem_limit_bytes=vmem_mib * MIB)


def _row_tile(n, pref):
    for t in (1664, 1280, 1024, 640, 512, 384, 320, 256, 128, 64, 32, 16, 8):
        if t <= pref and n % t == 0:
            return t
    raise ValueError(f"no row tile for {n}")


def _dot(a, b):
    return jnp.dot(a, b, preferred_element_type=F32)


def _dot_nt(a, b):
    return lax.dot_general(a, b, (((1,), (1,)), ((), ())), preferred_element_type=F32)


def _dot_tn(a, b):
    return lax.dot_general(a, b, (((0,), (0,)), ((), ())), preferred_element_type=F32)


def _sigmoid(x):
    return jax.nn.sigmoid(x)


def _dsilu(x, s):
    return s * (1.0 + x * (1.0 - s))


def _mean(x):
    return jnp.mean(x, axis=-1, keepdims=True)


def _colsum(x):
    return jnp.sum(x, axis=0, keepdims=True)


def _rope(x, cos, sin):
    half = x.shape[-1] // 2
    x1, x2 = x[:, :half], x[:, half:]
    return jnp.concatenate([x1 * cos - x2 * sin, x1 * sin + x2 * cos], axis=-1)


def _rope_t(d, cos, sin):
    half = d.shape[-1] // 2
    d1, d2 = d[:, :half], d[:, half:]
    return jnp.concatenate([d1 * cos + d2 * sin, d2 * cos - d1 * sin], axis=-1)


def _adamw(w, g, m, v):
    m = ADAM_B1 * m + (1.0 - ADAM_B1) * g
    v = ADAM_B2 * v + (1.0 - ADAM_B2) * (g * g)
    m_hat = m / (1.0 - ADAM_B1 ** ADAM_STEP)
    v_hat = v / (1.0 - ADAM_B2 ** ADAM_STEP)
    delta = -ADAM_LR * (m_hat / (jnp.sqrt(v_hat) + ADAM_EPS) + ADAM_WD * w)
    return delta, m, v


def _ret_consts():
    h = jnp.arange(RET_HEADS, dtype=F32)
    log_g = jnp.log(1.0 - jnp.exp2(-5.0 - h))
    idx = jnp.arange(CHUNK, dtype=F32)
    rel = idx[:, None] - idx[None, :]
    dmask = jnp.where(rel[None] >= 0, jnp.exp(jnp.maximum(rel, 0.0)[None] * log_g[:, None, None]), 0.0)
    qd = jnp.exp((idx[None, :] + 1.0) * log_g[:, None])[:, :, None]
    kd = jnp.exp((CHUNK - 1.0 - idx[None, :]) * log_g[:, None])[:, :, None]
    cd = jnp.exp(CHUNK * log_g)[:, None, None]
    return dmask, qd, kd, cd


def _rope_tables(n_rows, half):
    pos = jnp.arange(n_rows, dtype=F32) - float(LEAD)
    inv_freq = ROPE_BASE ** (-jnp.arange(half, dtype=F32) / half)
    ang = pos[:, None] * inv_freq[None, :]
    return jnp.cos(ang), jnp.sin(ang)


def _padded_tile_stream(x_hbm, lead_hbm, buf, sems, tm):
    def start_first(slot):
        pltpu.make_async_copy(lead_hbm, buf.at[slot, pl.ds(0, CHUNK)], sems.at[slot]).start()
        pltpu.make_async_copy(x_hbm.at[pl.ds(0, tm - CHUNK)], buf.at[slot, pl.ds(CHUNK, tm - CHUNK)],
                              sems.at[slot]).start()

    def start(slot, tile):
        pltpu.make_async_copy(x_hbm.at[pl.ds(tile * tm - CHUNK, tm)], buf.at[slot], sems.at[slot]).start()

    def wait(slot):
        pltpu.make_async_copy(x_hbm.at[pl.ds(0, tm)], buf.at[slot], sems.at[slot]).wait()

    return start_first, start, wait


def _cast_into_gathered(a, s_arr, name):
    rows, cols = a.shape
    tr = _row_tile(rows, 256)

    def body(s_ref, a_ref, o_ref):
        del s_ref
        o_ref[0] = a_ref[...].astype(BF16)

    return pl.pallas_call(
        body, name=name,
        grid_spec=pltpu.PrefetchScalarGridSpec(
            num_scalar_prefetch=1, grid=(rows // tr,),
            in_specs=[pl.BlockSpec((tr, cols), lambda i, s_ref: (i, 0))],
            out_specs=pl.BlockSpec((1, tr, cols), lambda i, s_ref: (s_ref[0], i, 0))),
        out_shape=jax.ShapeDtypeStruct((N_CHIPS, rows, cols), BF16),
        compiler_params=_params(1, 32))(s_arr, a)


def _in_proj_gather(order, x, lead, ln_g, w4, pw4, wo4):
    seq, d = x.shape
    lp = CHUNK + seq
    ns = w4.shape[2]
    tm = _row_tile(lp, 640)
    nt = lp // tm
    assert nt >= 2, "the hn write-back of a row tile is waited for one step later, before any pass re-reads it"
    gathered = (w4, pw4, wo4)
    halves = [a.shape[1] // 2 for a in gathered]
    n = len(gathered)

    def body(order_ref, x_hbm, lead_hbm, g_ref, w_in, pw_in, wo_in, proj_ref, r_ref, hn_hbm, w_out, pw_out,
             wo_out, wbuf, hnbuf, hbuf, send_sems, recv_sems, hn_out_sems, hn_in_sems, h_sems, w_sem):
        del order_ref, w_in, pw_in, wo_in
        t, i = pl.program_id(0), pl.program_id(1)
        slot = (t * nt + i) % 2
        h_first, h_start, h_wait = _padded_tile_stream(x_hbm, lead_hbm, hbuf, h_sems, tm)
        x, y, c = _mesh_pos()
        me, sibling = (x, y, c), (x, y, 1 - c)
        my_s = 2 * x + y
        chips = _other_chips(x, y)
        outs = (w_out, pw_out, wo_out)

        def half(k, s, cc):
            return outs[k].at[s, pl.ds(cc * halves[k], halves[k])]

        def rcopy(k, j, rows, to):
            return pltpu.make_async_remote_copy(
                src_ref=rows, dst_ref=rows, send_sem=send_sems.at[6 * k + j], recv_sem=recv_sems.at[6 * k + j],
                device_id=to, device_id_type=MESH)

        def send(k, j):
            return rcopy(k, j, half(k, my_s, c), (*chips[j], c))

        def shard_of(j):
            return 2 * chips[j][0] + chips[j][1]

        def forward(k, j):
            return rcopy(k, 3 + j, half(k, shard_of(j), c), sibling)

        def land(k, j):
            rcopy(k, j, half(k, shard_of(j), c), me).wait_recv()
            forward(k, j).start()

        def landed_from_sibling(k, j):
            rcopy(k, 3 + j, half(k, shard_of(j), 1 - c), me).wait_recv()

        def load_w(s):
            cp = pltpu.make_async_copy(w_out.at[s], wbuf, w_sem)
            cp.start()
            cp.wait()

        def hn_out(sl, row_tile):
            return pltpu.make_async_copy(hnbuf.at[sl], hn_hbm.at[pl.ds(row_tile * tm, tm)], hn_out_sems.at[sl])

        def hn_in(sl, row_tile):
            return pltpu.make_async_copy(hn_hbm.at[pl.ds(row_tile * tm, tm)], hnbuf.at[sl], hn_in_sems.at[sl])

        @pl.when((t == 0) & (i == 0))
        def _():
            for j in range(2):
                send(0, j).start()
            load_w(my_s)

        for j in range(3):
            @pl.when((t == j + 1) & (i == 0))
            def _(j=j):
                if j == 0:
                    send(0, 2).start()
                    for k in range(1, n):
                        for jj in range(3):
                            send(k, jj).start()
                land(0, j)
                landed_from_sibling(0, j)
                load_w(shard_of(j))

        @pl.when((t == 0) & (i == 0))
        def _():
            h_first(0)

        @pl.when((t == 0) & (i + 1 < nt))
        def _():
            h_start((i + 1) % 2, i + 1)

        @pl.when(t == 0)
        def _():
            h_wait(i % 2)
            h = hbuf[i % 2]
            r = lax.rsqrt(_mean(h * h) + EPS)
            hnbuf[slot] = ((h * r) * g_ref[...]).astype(BF16)
            r_ref[...] = r
            hn_out(slot, i).start()

        @pl.when(t > 0)
        def _():
            hn_in(slot, i).wait()

        @pl.when(((t == 0) & (i > 0)) | ((t == 1) & (i == 0)))
        def _():
            hn_out(1 - slot, jnp.where(i > 0, i - 1, nt - 1)).wait()

        last = (t == N_CHIPS - 1) & (i == nt - 1)

        @pl.when(((t > 0) | (i == nt - 1)) & jnp.logical_not(last))
        def _():
            hn_in(1 - slot, jnp.where(i == nt - 1, 0, i + 1)).start()

        proj_ref[...] = _dot(hnbuf[slot], wbuf[...])

        @pl.when(last)
        def _():
            for k in range(1, n):
                for j in range(3):
                    land(k, j)
            for k in range(1, n):
                for j in range(3):
                    landed_from_sibling(k, j)
            for k in range(n):
                for j in range(3):
                    send(k, j).wait_send()
                    forward(k, j).wait_send()

    def frozen(t, i):
        return jnp.where(t == 0, i, nt - 1)

    any_spec = pl.BlockSpec(memory_space=pl.ANY)
    return pl.pallas_call(
        body, name="in_proj_gather",
        grid_spec=pltpu.PrefetchScalarGridSpec(
            num_scalar_prefetch=1, grid=(N_CHIPS, nt),
            in_specs=[any_spec, any_spec, pl.BlockSpec((1, d), lambda t, i, o: (0, 0)),
                      any_spec, any_spec, any_spec],
            out_specs=[pl.BlockSpec((tm, ns), lambda t, i, o: (i, o[t])),
                       pl.BlockSpec((tm, 1), lambda t, i, o: (frozen(t, i), 0)),
                       any_spec, any_spec, any_spec, any_spec],
            scratch_shapes=[pltpu.VMEM((d, ns), BF16), pltpu.VMEM((2, tm, d), BF16), pltpu.VMEM((2, tm, d), F32),
                            pltpu.SemaphoreType.DMA((6 * n,)), pltpu.SemaphoreType.DMA((6 * n,)),
                            pltpu.SemaphoreType.DMA((2,)), pltpu.SemaphoreType.DMA((2,)),
                            pltpu.SemaphoreType.DMA((2,)), pltpu.SemaphoreType.DMA]),
        out_shape=[jax.ShapeDtypeStruct((lp, N_CHIPS * ns), F32),
                   jax.ShapeDtypeStruct((lp, 1), F32),
                   jax.ShapeDtypeStruct((lp, d), BF16)]
                  + [jax.ShapeDtypeStruct(a.shape, a.dtype) for a in gathered],
        input_output_aliases={4: 3, 5: 4, 6: 5},
        compiler_params=_params(2, 48))(order, x, lead, ln_g, w4, pw4, wo4)


def _ret_fwd(proj, cos, sin, gn_g, consts):
    lp = proj.shape[0]
    w = gn_g.shape[1]
    hd = w // RET_HEADS
    nch = lp // CHUNK
    dmask, qd, kd, cd = consts

    def body(q_ref, k_ref, v_ref, g_ref, cos_ref, sin_ref, gn_ref, m_ref, qd_ref, kd_ref, cd_ref,
             y_ref, ssave_ref, ops_ref, sc_ref, out_ref, s_scr):
        @pl.when(pl.program_id(0) == 0)
        def _():
            s_scr[...] = jnp.zeros_like(s_scr)

        cos_t, sin_t = cos_ref[...], sin_ref[...]
        for h in range(RET_HEADS):
            sl = slice(h * hd, (h + 1) * hd)
            qr = _rope(q_ref[:, sl], cos_t, sin_t)
            kr = _rope(k_ref[:, sl], cos_t, sin_t) * (hd ** -0.5)
            qb, kb = qr.astype(BF16), kr.astype(BF16)
            qdq, kdk = (qr * qd_ref[h]).astype(BF16), (kr * kd_ref[h]).astype(BF16)
            vb = v_ref[:, sl].astype(BF16)
            for k, operand in enumerate((qb, kb, qdq, kdk, vb)):
                ops_ref[k, :, sl] = operand
            scb = (_dot_nt(qb, kb) * m_ref[h]).astype(BF16)
            sc_ref[0, h] = scb
            state = s_scr[h]
            sb = state.astype(BF16)
            ssave_ref[0, h] = sb
            out = _dot(scb, vb) + _dot(qdq, sb)
            out_ref[:, sl] = out
            s_scr[h] = state * cd_ref[h] + _dot_tn(kdk, vb)
            dev = out - _mean(out)
            yn = dev * lax.rsqrt(_mean(dev * dev) + EPS)
            g = g_ref[:, sl]
            y_ref[:, sl] = ((yn * gn_ref[:, sl]) * (g * _sigmoid(g))).astype(BF16)

    def col(j):
        return pl.BlockSpec((CHUNK, w), lambda i: (i, j))

    def whole(a):
        return pl.BlockSpec(a.shape, lambda i: (0,) * a.ndim)

    return pl.pallas_call(
        body, name="ret_fwd", grid=(nch,),
        in_specs=[col(0), col(1), col(2), col(3),
                  pl.BlockSpec((CHUNK, hd // 2), lambda i: (i, 0)),
                  pl.BlockSpec((CHUNK, hd // 2), lambda i: (i, 0)),
                  whole(gn_g), whole(dmask), whole(qd), whole(kd), whole(cd)],
        out_specs=[pl.BlockSpec((CHUNK, w), lambda i: (i, 0)),
                   pl.BlockSpec((1, RET_HEADS, hd, hd), lambda i: (i, 0, 0, 0)),
                   pl.BlockSpec((5, CHUNK, w), lambda i: (0, i, 0)),
                   pl.BlockSpec((1, RET_HEADS, CHUNK, CHUNK), lambda i: (i, 0, 0, 0)),
                   pl.BlockSpec((CHUNK, w), lambda i: (i, 0))],
        out_shape=[jax.ShapeDtypeStruct((lp, w), BF16),
                   jax.ShapeDtypeStruct((nch, RET_HEADS, hd, hd), BF16),
                   jax.ShapeDtypeStruct((5, lp, w), BF16),
                   jax.ShapeDtypeStruct((nch, RET_HEADS, CHUNK, CHUNK), BF16),
                   jax.ShapeDtypeStruct((lp, w), F32)],
        scratch_shapes=[pltpu.VMEM((RET_HEADS, hd, hd), F32)],
        compiler_params=_params(1, 32))(proj, proj, proj, proj, cos, sin, gn_g, dmask, qd, kd, cd)


def _tap_groups(start, flip):
    groups = {}
    for j in range(CONV_K):
        o = start + (CONV_K - 1 - j if flip else j)
        groups.setdefault(o % 8, []).append((o // 8, j))
    return groups


def _shift_up(win, s):
    return win if s == 0 else pltpu.roll(win, win.shape[0] - s, axis=0)


def _dw_taps(src_ref, w_ref, dst_ref, bias, *, rows, start, flip, rb):
    cw = dst_ref.shape[1]
    lb = min(128, cw)
    groups = _tap_groups(start, flip)

    def rb_body(r, carry):
        base = pl.multiple_of(r * rb, rb)
        for cb in range(cw // lb):
            ls = slice(cb * lb, (cb + 1) * lb)
            win = src_ref[pl.ds(base, rb + HALO), ls]
            acc = jnp.zeros((rb, lb), F32) if bias is None else jnp.broadcast_to(bias[:, ls], (rb, lb))
            for s, taps in groups.items():
                ws = _shift_up(win, s)
                for a, j in taps:
                    acc = acc + ws[8 * a:8 * a + rb, :] * w_ref[j:j + 1, ls]
            dst_ref[pl.ds(base, rb), ls] = acc
        return carry

    lax.fori_loop(0, rows // rb, rb_body, 0)


def _conv_fwd(proj, dw_w, dw_b, cln_g, cln_b, pw_w, pw_b):
    lp = proj.shape[0]
    cw = dw_b.shape[1]
    tm = _row_tile(lp, 640)
    rb = _row_tile(tm, 64)

    def body(a_ref, b_ref, gc_ref, w_ref, wb_ref, lg_ref, lb_ref, pw_ref, pb_ref, y_ref, u1_ref, buf):
        @pl.when(pl.program_id(0) == 0)
        def _():
            buf[0:HALO, :] = jnp.zeros((HALO, cw), F32)

        buf[HALO:HALO + tm, :] = a_ref[...] * _sigmoid(b_ref[...])
        _dw_taps(buf, w_ref, u1_ref, wb_ref[...], rows=tm, start=HALO - (CONV_K - 1), flip=False, rb=rb)
        buf[0:HALO, :] = buf[tm:tm + HALO, :]
        u1 = u1_ref[...]
        dev = u1 - _mean(u1)
        z = dev * lax.rsqrt(_mean(dev * dev) + EPS) * lg_ref[...] + lb_ref[...]
        u3 = (z * _sigmoid(z)).astype(BF16)
        u4 = _dot(u3, pw_ref[...]) + pb_ref[...]
        gc = gc_ref[...]
        y_ref[...] = (u4 * (gc * _sigmoid(gc))).astype(BF16)

    def col(j):
        return pl.BlockSpec((tm, cw), lambda i: (i, j))

    def whole(a):
        return pl.BlockSpec(a.shape, lambda i: (0,) * a.ndim)

    return pl.pallas_call(
        body, name="conv_fwd", grid=(lp // tm,),
        in_specs=[col(4), col(5), col(6), whole(dw_w), whole(dw_b), whole(cln_g), whole(cln_b),
                  whole(pw_w), whole(pw_b)],
        out_specs=[pl.BlockSpec((tm, cw), lambda i: (i, 0)), pl.BlockSpec((tm, cw), lambda i: (i, 0))],
        out_shape=[jax.ShapeDtypeStruct((lp, cw), BF16), jax.ShapeDtypeStruct((lp, cw), F32)],
        scratch_shapes=[pltpu.VMEM((tm + HALO, cw), F32)],
        compiler_params=_params(1, 48))(proj, proj, proj, dw_w, dw_b, cln_g, cln_b, pw_w, pw_b)


def _out_proj_loss(yr, yc, x, lead, tgt, zero_lead, w_out, final_g):
    seq, d = x.shape
    lp = CHUNK + seq
    w = yr.shape[1]
    tm = _row_tile(lp, 320)
    nt = lp // tm
    assert tm > CHUNK

    def body(yr_ref, yc_ref, x_hbm, lead_hbm, t_hbm, zlead_hbm, w_ref, fg_ref, dh2_ref, dy_ref, loss_ref, dfg_ref,
             hbuf, tbuf, hsems, tsems):
        i = pl.program_id(0)
        slot = i % 2
        streams = (_padded_tile_stream(x_hbm, lead_hbm, hbuf, hsems, tm),
                   _padded_tile_stream(t_hbm, zlead_hbm, tbuf, tsems, tm))

        @pl.when(i == 0)
        def _():
            loss_ref[...] = jnp.zeros_like(loss_ref)
            dfg_ref[...] = jnp.zeros_like(dfg_ref)
            for start_first, _, _ in streams:
                start_first(0)

        @pl.when(i + 1 < nt)
        def _():
            for _, start, _ in streams:
                start(1 - slot, i + 1)

        for _, _, wait in streams:
            wait(slot)

        h2 = hbuf[slot] + (_dot(yr_ref[...], w_ref[0:w, :]) + _dot(yc_ref[...], w_ref[w:2 * w, :]))
        r2 = lax.rsqrt(_mean(h2 * h2) + EPS)
        h2n = h2 * r2
        fg = fg_ref[...]
        rows = i * tm + lax.broadcasted_iota(jnp.int32, (tm, 1), 0)
        err = jnp.where(rows >= CHUNK, h2n * fg - tbuf[slot], 0.0)
        loss_ref[...] += _colsum(err * err)
        dout = err * (1.0 / d)
        dfg_ref[...] += _colsum(dout * h2n)
        dz = dout * fg
        dh2 = r2 * (dz - h2n * _mean(dz * h2n))
        dh2_ref[...] = dh2
        db = dh2.astype(BF16)
        dy_ref[:, 0:w] = _dot_nt(db, w_ref[0:w, :])
        dy_ref[:, w:2 * w] = _dot_nt(db, w_ref[w:2 * w, :])

    def row(cols):
        return pl.BlockSpec((tm, cols), lambda i: (i, 0))

    any_spec = pl.BlockSpec(memory_space=pl.ANY)
    return pl.pallas_call(
        body, name="out_proj_loss", grid=(nt,),
        in_specs=[row(w), row(w), any_spec, any_spec, any_spec, any_spec,
                  pl.BlockSpec(memory_space=pltpu.VMEM),
                  pl.BlockSpec((1, d), lambda i: (0, 0))],
        out_specs=[row(d), row(2 * w), pl.BlockSpec((1, d), lambda i: (0, 0)),
                   pl.BlockSpec((1, d), lambda i: (0, 0))],
        out_shape=[jax.ShapeDtypeStruct((lp, d), F32), jax.ShapeDtypeStruct((lp, 2 * w), F32),
                   jax.ShapeDtypeStruct((1, d), F32), jax.ShapeDtypeStruct((1, d), F32)],
        scratch_shapes=[pltpu.VMEM((2, tm, d), F32), pltpu.VMEM((2, tm, d), F32),
                        pltpu.SemaphoreType.DMA((2,)), pltpu.SemaphoreType.DMA((2,))],
        compiler_params=_params(1, 56))(yr, yc, x, lead, tgt, zero_lead, w_out, final_g)


def _dw_out(yr, yc, dh2):
    lp, d = dh2.shape
    w = yr.shape[1]
    tm = _row_tile(lp, 1664)
    nb = 2
    dn = d // nb

    def body(yr_ref, yc_ref, d_ref, o_ref):
        @pl.when(pl.program_id(1) == 0)
        def _():
            o_ref[...] = jnp.zeros_like(o_ref)

        db = d_ref[...].astype(BF16)
        o_ref[0:w, :] += _dot_tn(yr_ref[...], db)
        o_ref[w:2 * w, :] += _dot_tn(yc_ref[...], db)

    return pl.pallas_call(
        body, name="dw_out", grid=(nb, lp // tm),
        in_specs=[pl.BlockSpec((tm, w), lambda n, i: (i, 0)),
                  pl.BlockSpec((tm, w), lambda n, i: (i, 0)),
                  pl.BlockSpec((tm, dn), lambda n, i: (i, n))],
        out_specs=pl.BlockSpec((2 * w, dn), lambda n, i: (0, n)),
        out_shape=jax.ShapeDtypeStruct((2 * w, d), F32),
        compiler_params=_params(2, 52))(yr, yc, dh2)


def _ret_bwd(proj, dy, ssave, ops, scores, out_pre, cos, sin, gn_g, consts):
    lp = proj.shape[0]
    w = gn_g.shape[1]
    hd = w // RET_HEADS
    nch = lp // CHUNK
    dmask, qd, kd, cd = consts

    def body(g_ref, dy_ref, ss_ref, ops_ref, sc_ref, out_ref, cos_ref, sin_ref, gn_ref, m_ref, qd_ref, kd_ref,
             cd_ref, dp_ref, dgn_ref, ds_scr):
        @pl.when(pl.program_id(0) == 0)
        def _():
            ds_scr[...] = jnp.zeros_like(ds_scr)
            dgn_ref[...] = jnp.zeros_like(dgn_ref)

        cos_t, sin_t = cos_ref[...], sin_ref[...]
        for h in range(RET_HEADS):
            sl = slice(h * hd, (h + 1) * hd)
            qb, kb, qdq, kdk, vb = (ops_ref[k, :, sl] for k in range(5))
            sb = ss_ref[0, h]
            scb = sc_ref[0, h]
            mask = m_ref[h]
            qdec, kdec = qd_ref[h], kd_ref[h]
            out = out_ref[:, sl]
            dev = out - _mean(out)
            rstd = lax.rsqrt(_mean(dev * dev) + EPS)
            yn = dev * rstd
            g = g_ref[:, sl]
            sg = _sigmoid(g)
            gng = gn_ref[:, sl]
            dyv = dy_ref[:, sl]
            dgr = dyv * (yn * gng) * _dsilu(g, sg)
            silu_g = g * sg
            dgn_ref[:, sl] += _colsum(dyv * yn * silu_g)
            dyn = dyv * gng * silu_g
            dout = rstd * (dyn - _mean(dyn) - yn * _mean(dyn * yn))
            dob = dout.astype(BF16)
            by_v_and_s = _dot_nt(dob, jnp.concatenate([vb, sb], axis=0))
            dscb = (by_v_and_s[:, 0:CHUNK] * mask).astype(BF16)
            dstate = ds_scr[h]
            dsb = dstate.astype(BF16)
            dq = _dot(dscb, kb) + by_v_and_s[:, CHUNK:] * qdec
            dk = _dot_tn(dscb, qb) + _dot_nt(vb, dsb) * kdec
            onto_do = _dot_tn(jnp.concatenate([scb, qdq], axis=1), dob)
            dv = onto_do[0:CHUNK] + _dot(kdk, dsb)
            ds_scr[h] = dstate * cd_ref[h] + onto_do[CHUNK:]
            dp_ref[:, 0 * w + h * hd:0 * w + (h + 1) * hd] = _rope_t(dq, cos_t, sin_t).astype(BF16)
            dp_ref[:, 1 * w + h * hd:1 * w + (h + 1) * hd] = (_rope_t(dk, cos_t, sin_t) * (hd ** -0.5)).astype(BF16)
            dp_ref[:, 2 * w + h * hd:2 * w + (h + 1) * hd] = dv.astype(BF16)
            dp_ref[:, 3 * w + h * hd:3 * w + (h + 1) * hd] = dgr.astype(BF16)

    def rev(i):
        return nch - 1 - i

    def col(j):
        return pl.BlockSpec((CHUNK, w), lambda i: (rev(i), j))

    def whole(a):
        return pl.BlockSpec(a.shape, lambda i: (0,) * a.ndim)

    return pl.pallas_call(
        body, name="ret_bwd", grid=(nch,),
        in_specs=[col(3),
                  pl.BlockSpec((CHUNK, w), lambda i: (rev(i), 0)),
                  pl.BlockSpec((1, RET_HEADS, hd, hd), lambda i: (rev(i), 0, 0, 0)),
                  pl.BlockSpec((5, CHUNK, w), lambda i: (0, rev(i), 0)),
                  pl.BlockSpec((1, RET_HEADS, CHUNK, CHUNK), lambda i: (rev(i), 0, 0, 0)),
                  pl.BlockSpec((CHUNK, w), lambda i: (rev(i), 0)),
                  pl.BlockSpec((CHUNK, hd // 2), lambda i: (rev(i), 0)),
                  pl.BlockSpec((CHUNK, hd // 2), lambda i: (rev(i), 0)),
                  whole(gn_g), whole(dmask), whole(qd), whole(kd), whole(cd)],
        out_specs=[pl.BlockSpec((CHUNK, 4 * w), lambda i: (rev(i), 0)),
                   pl.BlockSpec((1, w), lambda i: (0, 0))],
        out_shape=[jax.ShapeDtypeStruct((lp, 7 * w), BF16), jax.ShapeDtypeStruct((1, w), F32)],
        scratch_shapes=[pltpu.VMEM((RET_HEADS, hd, hd), F32)],
        compiler_params=_params(1, 32))(proj, dy, ssave, ops, scores, out_pre, cos, sin, gn_g, dmask, qd, kd, cd)


def _conv_bwd_pw(dy, proj, u1, cln_g, cln_b, pw_w, pw_b, dproj):
    lp, cw = u1.shape
    tm = _row_tile(lp, 320)

    def body(dy_ref, gc_ref, u1_ref, lg_ref, lb_ref, pw_ref, pb_ref, dp_in, dp_ref, du1_ref, dpw_ref,
             dpb_ref, dlg_ref, dlb_ref):
        del dp_in

        @pl.when(pl.program_id(0) == 0)
        def _():
            dpw_ref[...] = jnp.zeros_like(dpw_ref)
            dpb_ref[...] = jnp.zeros_like(dpb_ref)
            dlg_ref[...] = jnp.zeros_like(dlg_ref)
            dlb_ref[...] = jnp.zeros_like(dlb_ref)

        u1 = u1_ref[...]
        dev = u1 - _mean(u1)
        rstd = lax.rsqrt(_mean(dev * dev) + EPS)
        u1n = dev * rstd
        lg = lg_ref[...]
        z = u1n * lg + lb_ref[...]
        sz = _sigmoid(z)
        u3b = (z * sz).astype(BF16)
        u4 = _dot(u3b, pw_ref[...]) + pb_ref[...]
        gc = gc_ref[...]
        sgc = _sigmoid(gc)
        dyc = dy_ref[...]
        du4 = dyc * (gc * sgc)
        dp_ref[...] = (dyc * u4 * _dsilu(gc, sgc)).astype(BF16)
        du4b = du4.astype(BF16)
        dpb_ref[...] += _colsum(du4)
        dpw_ref[...] += _dot_tn(u3b, du4b)
        dz = _dot_nt(du4b, pw_ref[...]) * _dsilu(z, sz)
        dlg_ref[...] += _colsum(dz * u1n)
        dlb_ref[...] += _colsum(dz)
        dn = dz * lg
        du1_ref[...] = rstd * (dn - _mean(dn) - u1n * _mean(dn * u1n))

    def row(j):
        return pl.BlockSpec((tm, cw), lambda i: (i, j))

    def whole(a):
        return pl.BlockSpec(a.shape, lambda i: (0,) * a.ndim)

    def acc(r):
        return pl.BlockSpec((r, cw), lambda i: (0, 0))

    return pl.pallas_call(
        body, name="conv_bwd_pw", grid=(lp // tm,),
        in_specs=[row(1), row(6), row(0), whole(cln_g), whole(cln_b), whole(pw_w), whole(pw_b),
                  pl.BlockSpec(memory_space=pl.ANY)],
        out_specs=[row(6), row(0), acc(cw), acc(1), acc(1), acc(1)],
        out_shape=[jax.ShapeDtypeStruct(dproj.shape, dproj.dtype), jax.ShapeDtypeStruct((lp, cw), F32),
                   jax.ShapeDtypeStruct((cw, cw), F32), jax.ShapeDtypeStruct((1, cw), F32),
                   jax.ShapeDtypeStruct((1, cw), F32), jax.ShapeDtypeStruct((1, cw), F32)],
        input_output_aliases={7: 0},
        compiler_params=_params(1, 48))(dy, proj, u1, cln_g, cln_b, pw_w, pw_b, dproj)


def _conv_bwd_dw(du1, proj, dw_w, dproj):
    lp, cw = du1.shape
    tm = _row_tile(lp, 640)
    rb = _row_tile(tm, 64)
    nt = lp // tm
    hb = tm // HALO

    def body(a_ref, b_ref, du_ref, nx_ref, w_ref, dp_in, dp_ref, dww_ref, dwb_ref, ubuf, dbuf, du0, acc):
        del dp_in
        i = pl.program_id(0)

        @pl.when(i == 0)
        def _():
            ubuf[0:HALO, :] = jnp.zeros((HALO, cw), F32)
            acc[...] = jnp.zeros_like(acc)
            dwb_ref[...] = jnp.zeros_like(dwb_ref)

        a = a_ref[...]
        sb = _sigmoid(b_ref[...])
        ubuf[HALO:HALO + tm, :] = a * sb
        du = du_ref[...]
        dbuf[0:tm, :] = du
        dbuf[tm:tm + HALO, :] = jnp.where(i == nt - 1, 0.0, nx_ref[...])
        dwb_ref[...] += _colsum(du)
        _dw_taps(dbuf, w_ref, du0, None, rows=tm, start=0, flip=True, rb=rb)
        d0 = du0[...]
        dp_ref[:, 0:cw] = (d0 * sb).astype(BF16)
        dp_ref[:, cw:2 * cw] = (d0 * a * sb * (1.0 - sb)).astype(BF16)

        lb = min(128, cw)
        groups = _tap_groups(HALO - (CONV_K - 1), False)

        def rb_body(r, carry):
            base = pl.multiple_of(r * rb, rb)
            for cb in range(cw // lb):
                ls = slice(cb * lb, (cb + 1) * lb)
                win = ubuf[pl.ds(base, rb + HALO), ls]
                dv = dbuf[pl.ds(base, rb), ls]
                for s, taps in groups.items():
                    ws = _shift_up(win, s)
                    for a, j in taps:
                        prod = dv * ws[8 * a:8 * a + rb, :]
                        acc[8 * j:8 * j + 8, ls] += jnp.sum(prod.reshape(rb // 8, 8, lb), axis=0)
            return carry

        lax.fori_loop(0, tm // rb, rb_body, 0)
        ubuf[0:HALO, :] = ubuf[tm:tm + HALO, :]

        @pl.when(i == nt - 1)
        def _():
            for j in range(CONV_K):
                dww_ref[j:j + 1, :] = _colsum(acc[8 * j:8 * j + 8, :])
            dww_ref[CONV_K:HALO, :] = jnp.zeros((HALO - CONV_K, cw), F32)

    def col(j):
        return pl.BlockSpec((tm, cw), lambda i: (i, j))

    return pl.pallas_call(
        body, name="conv_bwd_dw", grid=(nt,),
        in_specs=[col(4), col(5), col(0),
                  pl.BlockSpec((HALO, cw), lambda i: (jnp.minimum((i + 1) * hb, nt * hb - 1), 0)),
                  pl.BlockSpec(dw_w.shape, lambda i: (0, 0)),
                  pl.BlockSpec(memory_space=pl.ANY)],
        out_specs=[pl.BlockSpec((tm, 2 * cw), lambda i: (i, 2)),
                   pl.BlockSpec((HALO, cw), lambda i: (0, 0)),
                   pl.BlockSpec((1, cw), lambda i: (0, 0))],
        out_shape=[jax.ShapeDtypeStruct(dproj.shape, dproj.dtype), jax.ShapeDtypeStruct((HALO, cw), F32),
                   jax.ShapeDtypeStruct((1, cw), F32)],
        scratch_shapes=[pltpu.VMEM((tm + HALO, cw), F32), pltpu.VMEM((tm + HALO, cw), F32),
                        pltpu.VMEM((tm, cw), F32), pltpu.VMEM((8 * HALO, cw), F32)],
        input_output_aliases={5: 0},
        compiler_params=_params(1, 56))(proj, proj, du1, du1, dw_w, dproj)


def _dw_in(hn, dproj, ns, others):
    lp, d = hn.shape
    tm = _row_tile(lp, 1664)
    nt = lp // tm
    mb = 512 if d % 512 == 0 else d
    n = len(others)
    halves = [d // 2] + [g.shape[1] // 2 for g in others]

    def body(hn_ref, dp_ref, *refs):
        other_refs, o_hbm, recv_refs = refs[:n], refs[n], refs[n + 1:2 * n + 2]
        acc, sem, send_sems, recv_sems = refs[2 * n + 2:]
        s, i = pl.program_id(0), pl.program_id(1)
        x, y, c = _mesh_pos()

        def to_sibling(src, dst, k):
            return pltpu.make_async_remote_copy(
                src_ref=src, dst_ref=dst, send_sem=send_sems.at[k], recv_sem=recv_sems.at[k],
                device_id=(x, y, 1 - c), device_id_type=MESH)

        def shard_half(p):
            return to_sibling(o_hbm.at[p, pl.ds((1 - c) * halves[0], halves[0])], recv_refs[0].at[p], p)

        def other_halves(k):
            return to_sibling(other_refs[k].at[:, pl.ds((1 - c) * halves[1 + k], halves[1 + k])],
                              recv_refs[1 + k], N_CHIPS + k)

        @pl.when((s == 0) & (i == 0))
        def _():
            for k in range(n):
                other_halves(k).start()

        @pl.when(i == 0)
        def _():
            acc[...] = jnp.zeros_like(acc)

        for m in range(d // mb):
            rows = slice(m * mb, (m + 1) * mb)
            acc[rows, :] += _dot_tn(hn_ref[:, rows], dp_ref[...])

        @pl.when(i == nt - 1)
        def _():
            cp = pltpu.make_async_copy(acc, o_hbm.at[s], sem)
            cp.start()
            cp.wait()
            shard_half(s).start()

        @pl.when((s == N_CHIPS - 1) & (i == nt - 1))
        def _():
            for p in range(N_CHIPS):
                shard_half(p).wait()
            for k in range(n):
                other_halves(k).wait()

    any_spec = pl.BlockSpec(memory_space=pl.ANY)
    outs = pl.pallas_call(
        body, name="dw_in", grid=(N_CHIPS, nt),
        in_specs=[pl.BlockSpec((tm, d), lambda s, i: (i, 0)),
                  pl.BlockSpec((tm, ns), lambda s, i: (i, s))] + [any_spec] * n,
        out_specs=[any_spec] * (n + 2),
        out_shape=[jax.ShapeDtypeStruct((N_CHIPS, d, ns), F32), jax.ShapeDtypeStruct((N_CHIPS, d // 2, ns), F32)]
                  + [jax.ShapeDtypeStruct((N_CHIPS, g.shape[1] // 2) + g.shape[2:], g.dtype) for g in others],
        scratch_shapes=[pltpu.VMEM((d, ns), F32), pltpu.SemaphoreType.DMA,
                        pltpu.SemaphoreType.DMA((N_CHIPS + n,)), pltpu.SemaphoreType.DMA((N_CHIPS + n,))],
        compiler_params=_params(2, 56))(hn, dproj, *others)
    return outs[0], outs[1:]


def _in_proj_bwd(dproj, w4, x, lead, r1, dh2, ln_g, cs):
    seq, d = x.shape
    lp = CHUNK + seq
    ns = w4.shape[2]
    tm = _row_tile(lp, 320)
    nt = lp // tm
    n = len(cs)
    assert tm > CHUNK

    def body(dp_ref, w_ref, x_hbm, lead_hbm, r_ref, d2_ref, g_ref, *refs):
        cs_refs, (dh_ref, dlg_ref), rb_refs = refs[:n], refs[n:n + 2], refs[n + 2:2 * n + 2]
        hbuf, h_sems, send_sems, recv_sems = refs[2 * n + 2:]
        i = pl.program_id(0)
        slot = i % 2
        x, y, c = _mesh_pos()
        h_first, h_start, h_wait = _padded_tile_stream(x_hbm, lead_hbm, hbuf, h_sems, tm)

        @pl.when(i == 0)
        def _():
            h_first(0)

        @pl.when(i + 1 < nt)
        def _():
            h_start(1 - slot, i + 1)

        h_wait(slot)

        def exchange():
            return [pltpu.make_async_remote_copy(
                src_ref=cs_refs[k].at[2 * chip[0] + chip[1]], dst_ref=rb_refs[k].at[j],
                send_sem=send_sems.at[3 * k + j], recv_sem=recv_sems.at[3 * k + j],
                device_id=(*chip, c), device_id_type=MESH)
                for k in range(n) for j, chip in enumerate(_other_chips(x, y))]

        @pl.when(i == 0)
        def _():
            dlg_ref[...] = jnp.zeros_like(dlg_ref)
            for cp in exchange():
                cp.start()

        dhn = _dot_nt(dp_ref[:, 0:ns], w_ref[0])
        for s in range(1, N_CHIPS):
            dhn = dhn + _dot_nt(dp_ref[:, s * ns:(s + 1) * ns], w_ref[s])
        r = r_ref[...]
        hn0 = hbuf[slot] * r
        dlg_ref[...] += _colsum(dhn * hn0)
        t = dhn * g_ref[...]
        dh_ref[...] = d2_ref[...] + r * (t - hn0 * _mean(t * hn0))

        @pl.when(i == nt - 1)
        def _():
            for cp in exchange():
                cp.wait()

    def row(cols):
        return pl.BlockSpec((tm, cols), lambda i: (i, 0))

    any_spec = pl.BlockSpec(memory_space=pl.ANY)
    outs = pl.pallas_call(
        body, name="in_proj_bwd", grid=(nt,),
        in_specs=[row(N_CHIPS * ns), pl.BlockSpec(memory_space=pltpu.VMEM),
                  any_spec, any_spec, row(1), row(d), pl.BlockSpec((1, d), lambda i: (0, 0))] + [any_spec] * n,
        out_specs=[row(d), pl.BlockSpec((1, d), lambda i: (0, 0))] + [any_spec] * n,
        out_shape=[jax.ShapeDtypeStruct((lp, d), F32), jax.ShapeDtypeStruct((1, d), F32)]
                  + [jax.ShapeDtypeStruct((3,) + a.shape[1:], a.dtype) for a in cs],
        scratch_shapes=[pltpu.VMEM((2, tm, d), F32), pltpu.SemaphoreType.DMA((2,)),
                        pltpu.SemaphoreType.DMA((3 * n,)), pltpu.SemaphoreType.DMA((3 * n,))],
        compiler_params=_params(1, 58))(dproj, w4, x, lead, r1, dh2, ln_g, *cs)
    return outs[0], outs[1], outs[2:]


def _mesh_pos():
    return lax.axis_index("x"), lax.axis_index("y"), lax.axis_index("c")


def _other_chips(x, y):
    return [(1 - x, y), (x, 1 - y), (1 - x, 1 - y)]


def _gather_shards(shards):
    n = len(shards)
    halves = [a.shape[0] // 2 for a in shards]

    def body(*refs):
        ins, outs = refs[:n], refs[n:2 * n]
        send_sems, recv_sems, loc_sems = refs[2 * n:]
        x, y, c = _mesh_pos()
        me, sibling = (x, y, c), (x, y, 1 - c)
        my_s = 2 * x + y
        chips = _other_chips(x, y)

        def half(k, s, cc):
            return outs[k].at[s, pl.ds(cc * halves[k], halves[k])]

        def rcopy(k, j, src, dst, to):
            return pltpu.make_async_remote_copy(
                src_ref=src, dst_ref=dst, send_sem=send_sems.at[6 * k + j], recv_sem=recv_sems.at[6 * k + j],
                device_id=to, device_id_type=MESH)

        local = [pltpu.make_async_copy(ins[k], outs[k].at[my_s], loc_sems.at[k]) for k in range(n)]
        for cp in local:
            cp.start()
        started = []
        for k in range(n):
            for j, chip in enumerate(chips):
                cp = rcopy(k, j, ins[k].at[pl.ds(c * halves[k], halves[k])], half(k, my_s, c), (*chip, c))
                cp.start()
                started.append(cp)
        for j, chip in enumerate(chips):
            s_j = 2 * chip[0] + chip[1]
            for k in range(n):
                rcopy(k, j, half(k, s_j, c), half(k, s_j, c), me).wait_recv()
                cp = rcopy(k, 3 + j, half(k, s_j, c), half(k, s_j, c), sibling)
                cp.start()
                started.append(cp)
        for j, chip in enumerate(chips):
            s_j = 2 * chip[0] + chip[1]
            for k in range(n):
                rcopy(k, 3 + j, half(k, s_j, 1 - c), half(k, s_j, 1 - c), me).wait_recv()
        for cp in started:
            cp.wait_send()
        for cp in local:
            cp.wait()

    return pl.pallas_call(
        body, name="gather_weights",
        in_specs=[pl.BlockSpec(memory_space=pl.ANY)] * n,
        out_specs=[pl.BlockSpec(memory_space=pl.ANY)] * n,
        out_shape=[jax.ShapeDtypeStruct((N_CHIPS,) + a.shape, a.dtype) for a in shards],
        scratch_shapes=[pltpu.SemaphoreType.DMA((6 * n,)), pltpu.SemaphoreType.DMA((6 * n,)),
                        pltpu.SemaphoreType.DMA((n,))],
    )(*shards)


def _rs_pair_share(fulls):
    n = len(fulls)
    halves = [f.shape[0] // 2 for f in fulls]

    def body(*refs):
        outs = refs[n:2 * n]
        send_sems, recv_sems = refs[2 * n:]
        x, y, c = _mesh_pos()

        def copy(k, cc, to):
            rows = outs[k].at[pl.ds(cc * halves[k], halves[k])]
            return pltpu.make_async_remote_copy(
                src_ref=rows, dst_ref=rows, send_sem=send_sems.at[k], recv_sem=recv_sems.at[k],
                device_id=to, device_id_type=MESH)

        cps = [copy(k, c, (x, y, 1 - c)) for k in range(n)]
        for cp in cps:
            cp.start()
        for k in range(n):
            copy(k, 1 - c, (x, y, c)).wait_recv()
        for cp in cps:
            cp.wait_send()

    return pl.pallas_call(
        body, name="rs_pair_share",
        in_specs=[pl.BlockSpec(memory_space=pl.ANY)] * n,
        out_specs=[pl.BlockSpec(memory_space=pl.ANY)] * n,
        out_shape=[jax.ShapeDtypeStruct(f.shape, f.dtype) for f in fulls],
        input_output_aliases={k: k for k in range(n)},
        scratch_shapes=[pltpu.SemaphoreType.DMA((n,)), pltpu.SemaphoreType.DMA((n,))],
    )(*fulls)


def _pair_sum(g, recv, pos, name):
    _, rows, cols = g.shape
    h = rows // 2
    tr = _row_tile(h, 256)
    nb = h // tr

    def body(pos_ref, g_ref, r_ref, o_ref, own_ref):
        total = g_ref[0] + r_ref[0]
        o_ref[0] = total.astype(BF16)

        @pl.when(pl.program_id(1) == pos_ref[1])
        def _():
            own_ref[...] = total

    return pl.pallas_call(
        body, name=name,
        grid_spec=pltpu.PrefetchScalarGridSpec(
            num_scalar_prefetch=1, grid=(nb, N_CHIPS),
            in_specs=[pl.BlockSpec((1, tr, cols), lambda r, s, pos_ref: (s, pos_ref[0] * nb + r, 0)),
                      pl.BlockSpec((1, tr, cols), lambda r, s, pos_ref: (s, r, 0))],
            out_specs=[pl.BlockSpec((1, tr, cols), lambda r, s, pos_ref: (s, r, 0)),
                       pl.BlockSpec((tr, cols), lambda r, s, pos_ref: (r, 0))]),
        out_shape=[jax.ShapeDtypeStruct((N_CHIPS, h, cols), BF16), jax.ShapeDtypeStruct((h, cols), F32)],
        compiler_params=_params(2, 32))(pos, g, recv)


def _chip_sum(own, rb, pos, name):
    h, cols = own.shape
    tr = _row_tile(h, 256)
    nb = h // tr

    def body(pos_ref, c_ref, r_ref, o_ref):
        del pos_ref
        o_ref[...] = ((c_ref[...] + r_ref[0].astype(F32)) + r_ref[1].astype(F32)) + r_ref[2].astype(F32)

    return pl.pallas_call(
        body, name=name,
        grid_spec=pltpu.PrefetchScalarGridSpec(
            num_scalar_prefetch=1, grid=(nb,),
            in_specs=[pl.BlockSpec((tr, cols), lambda r, pos_ref: (r, 0)),
                      pl.BlockSpec((3, tr, cols), lambda r, pos_ref: (0, r, 0))],
            out_specs=pl.BlockSpec((tr, cols), lambda r, pos_ref: (pos_ref[0] * nb + r, 0))),
        out_shape=jax.ShapeDtypeStruct((2 * h, cols), F32),
        compiler_params=_params(1, 32))(pos, own, rb)


def _adamw_big(w, g, m, v, name):
    rows, cols = w.shape
    tr = _row_tile(rows, 256)

    def body(w_ref, g_ref, m_ref, v_ref, d_ref, nm_ref, nv_ref):
        d_ref[...], nm_ref[...], nv_ref[...] = _adamw(w_ref[...], g_ref[...], m_ref[...], v_ref[...])

    spec = pl.BlockSpec((tr, cols), lambda i: (i, 0))
    return pl.pallas_call(
        body, name=name, grid=(rows // tr,),
        in_specs=[spec] * 4, out_specs=[spec] * 3,
        out_shape=[jax.ShapeDtypeStruct((rows, cols), F32)] * 3,
        compiler_params=_params(1, 48))(w, g, m, v)


def _gather_small(loss, dfg, dlg, dgn, ddwb, dclg, dclb, dpwb, ddww, dmeta):
    d = loss.shape[1]
    w = dgn.shape[1]
    wc, mc = ddww.shape[1] // N_CHIPS, dmeta.shape[1] // N_CHIPS

    def body(loss_ref, dfg_ref, dlg_ref, dgn_ref, ddwb_ref, dclg_ref, dclb_ref, dpwb_ref, ddww_ref, dmeta_ref,
             gs_ref, gd_ref, gm_ref, send_sems, recv_sems, loc_sems):
        x, y, c = _mesh_pos()
        me = 4 * x + 2 * y + c
        gs_ref[me, 0:1, :] = loss_ref[...]
        gs_ref[me, 1:2, :] = dfg_ref[...]
        gs_ref[me, 2:3, :] = dlg_ref[...]
        gs_ref[me, 3:4, 0:w] = dgn_ref[...]
        gs_ref[me, 3:4, w:2 * w] = ddwb_ref[...]
        gs_ref[me, 4:5, 0:w] = dclg_ref[...]
        gs_ref[me, 4:5, w:2 * w] = dclb_ref[...]
        gs_ref[me, 5:6, 0:w] = dpwb_ref[...]
        gs_ref[me, 5:6, w:2 * w] = jnp.zeros((1, d - w), F32)
        gs_ref[me, 6:8, :] = jnp.zeros((2, d), F32)
        bufs = (gs_ref, gd_ref, gm_ref)

        def mine_for(k, shard):
            if k == 0:
                return gs_ref.at[me]
            ref, width = ((ddww_ref, wc), (dmeta_ref, mc))[k - 1]
            return ref.at[:, pl.ds(pl.multiple_of(shard * width, width), width)]

        def peer(j):
            return (1 - x if j & 4 else x), (1 - y if j & 2 else y), (1 - c if j & 1 else c)

        def copy(k, j, src, slot, to):
            return pltpu.make_async_remote_copy(
                src_ref=src, dst_ref=bufs[k].at[slot],
                send_sem=send_sems.at[7 * k + j - 1], recv_sem=recv_sems.at[7 * k + j - 1],
                device_id=to, device_id_type=MESH)

        own = [pltpu.make_async_copy(mine_for(k, 2 * x + y), bufs[k].at[me], loc_sems.at[k - 1]) for k in (1, 2)]
        for cp in own:
            cp.start()
        cps = []
        for k in range(3):
            for j in range(1, N_DEV):
                px, py, pc = peer(j)
                cp = copy(k, j, mine_for(k, 2 * px + py), me, (px, py, pc))
                cp.start()
                cps.append(cp)
        for k in range(3):
            for j in range(1, N_DEV):
                px, py, pc = peer(j)
                slot = 4 * px + 2 * py + pc
                copy(k, j, bufs[k].at[slot], slot, (x, y, c)).wait_recv()
        for cp in cps:
            cp.wait_send()
        for cp in own:
            cp.wait()

    vm = pl.BlockSpec(memory_space=pltpu.VMEM)
    return pl.pallas_call(
        body, name="gather_small",
        in_specs=[vm] * 10, out_specs=[vm] * 3,
        out_shape=[jax.ShapeDtypeStruct((N_DEV, 8, d), F32),
                   jax.ShapeDtypeStruct((N_DEV, ddww.shape[0], wc), F32),
                   jax.ShapeDtypeStruct((N_DEV, dmeta.shape[0], mc), F32)],
        scratch_shapes=[pltpu.SemaphoreType.DMA((21,)), pltpu.SemaphoreType.DMA((21,)),
                        pltpu.SemaphoreType.DMA((2,))],
    )(loss, dfg, dlg, dgn, ddwb, dclg, dclb, dpwb, ddww, dmeta)


def _small_update(gs, gd, gm, weights, ms, vs):
    d = gs.shape[2]
    w = d // 2
    n = len(weights)

    def body(gs_ref, gd_ref, gm_ref, *refs):
        w_refs, m_refs, v_refs = refs[:n], refs[n:2 * n], refs[2 * n:3 * n]
        loss_ref = refs[3 * n]
        g_refs = refs[3 * n + 1:4 * n + 1]
        d_refs = refs[4 * n + 1:5 * n + 1]
        nm_refs = refs[5 * n + 1:6 * n + 1]
        nv_refs = refs[6 * n + 1:7 * n + 1]

        def total(ref):
            t = ref[0]
            for dev in range(1, N_DEV):
                t = t + ref[dev]
            return t

        packed = total(gs_ref)
        loss_ref[...] = jnp.sum(packed[0:1, :], axis=1, keepdims=True) * (0.5 / d)
        grads = [packed[2:3, :], packed[1:2, :], packed[3:4, 0:w], packed[3:4, w:2 * w], packed[4:5, 0:w],
                 packed[4:5, w:2 * w], packed[5:6, 0:w], total(gd_ref), total(gm_ref)]
        for k in range(n):
            g = grads[k]
            g_refs[k][...] = g
            d_refs[k][...], nm_refs[k][...], nv_refs[k][...] = _adamw(w_refs[k][...], g, m_refs[k][...], v_refs[k][...])

    def whole(shape):
        return pl.BlockSpec(shape, lambda i: (0,) * len(shape))

    shapes = [a.shape for a in weights]
    in_specs = [whole(gs.shape), whole(gd.shape), whole(gm.shape)] + [whole(s) for s in shapes] * 3
    out_specs = [whole((1, 1))] + [whole(s) for s in shapes] * 4
    out_shape = [jax.ShapeDtypeStruct((1, 1), F32)] + [jax.ShapeDtypeStruct(s, F32) for s in shapes] * 4
    outs = pl.pallas_call(
        body, name="small_update", grid=(1,), in_specs=in_specs, out_specs=out_specs, out_shape=out_shape,
        compiler_params=_params(1, 32))(gs, gd, gm, *weights, *ms, *vs)
    loss = outs[0]
    return loss, outs[1:n + 1], outs[n + 1:2 * n + 1], outs[2 * n + 1:3 * n + 1], outs[3 * n + 1:4 * n + 1]


def kernel(x, meta_tokens, ln_g, w_in, ret_gn_g, conv_dw_w, conv_dw_b, conv_ln_g, conv_ln_b, conv_pw_w, conv_pw_b, w_out, final_g, loss_target, m_meta_tokens, m_ln_g, m_w_in, m_ret_gn_g, m_conv_dw_w, m_conv_dw_b, m_conv_ln_g, m_conv_ln_b, m_conv_pw_w, m_conv_pw_b, m_w_out, m_final_g, v_meta_tokens, v_ln_g, v_w_in, v_ret_gn_g, v_conv_dw_w, v_conv_dw_b, v_conv_ln_g, v_conv_ln_b, v_conv_pw_w, v_conv_pw_b, v_w_out, v_final_g):
    seq, d = x.shape[1], x.shape[2]
    w = ret_gn_g.shape[1]
    hd = w // RET_HEADS
    lp = CHUNK + seq
    ns = w_in.shape[2]
    mx, my, mc = lax.axis_index("x"), lax.axis_index("y"), lax.axis_index("c")
    my_s = 2 * mx + my
    s_arr = my_s.astype(jnp.int32).reshape(1)
    pos = jnp.stack([mc, my_s]).astype(jnp.int32)
    order = jnp.stack([my_s, 2 * (1 - mx) + my, 2 * mx + (1 - my), 2 * (1 - mx) + (1 - my)]).astype(jnp.int32)

    dw_pad = jnp.pad(conv_dw_w[0], ((0, HALO - CONV_K), (0, 0)))
    dw4, meta4 = _gather_shards([dw_pad, meta_tokens])
    dw_full = dw4.transpose(1, 0, 2).reshape(HALO, w)
    meta_full = meta4.transpose(1, 0, 2).reshape(N_META, d)

    lead = jnp.concatenate([jnp.zeros((LEAD, d), F32), meta_full], axis=0)
    zero_lead = jnp.zeros((CHUNK, d), F32)
    consts = _ret_consts()
    cos, sin = _rope_tables(lp, hd // 2)
    fg2 = final_g.reshape(1, d)

    proj, r1, hn, w4, pw4, wo4 = _in_proj_gather(
        order, x[0], lead, ln_g, _cast_into_gathered(w_in[0], s_arr, "cast_w_in"),
        _cast_into_gathered(conv_pw_w[0], s_arr, "cast_pw_w"), _cast_into_gathered(w_out[0], s_arr, "cast_w_out"))
    pw_full = pw4.reshape(w, w)
    wo_full = wo4.reshape(2 * w, d)
    y_ret, ssave, ret_ops, ret_scores, ret_out = _ret_fwd(proj, cos, sin, ret_gn_g, consts)
    y_conv, u1 = _conv_fwd(proj, dw_full, conv_dw_b, conv_ln_g, conv_ln_b, pw_full, conv_pw_b)
    dh2, dy, loss_l, dfg = _out_proj_loss(y_ret, y_conv, x[0], lead, loss_target[0], zero_lead, wo_full, fg2)

    g_wo = _dw_out(y_ret, y_conv, dh2)
    dproj, dgn = _ret_bwd(proj, dy, ssave, ret_ops, ret_scores, ret_out, cos, sin, ret_gn_g, consts)
    dproj, du1, g_pw, dpwb, dclg, dclb = _conv_bwd_pw(dy, proj, u1, conv_ln_g, conv_ln_b, pw_full, conv_pw_b, dproj)
    dproj, ddww, ddwb = _conv_bwd_dw(du1, proj, dw_full, dproj)
    g_wo4 = g_wo.reshape(N_CHIPS, (2 * w) // N_CHIPS, d)
    g_pw4 = g_pw.reshape(N_CHIPS, w // N_CHIPS, w)
    g_win, recv = _dw_in(hn, dproj, ns, [g_wo4, g_pw4])
    gs = [g_win, g_wo4, g_pw4]
    names = ("w_in", "w_out", "pw_w")
    sums = [_pair_sum(g, r, pos, "pair_sum_" + nm) for g, r, nm in zip(gs, recv, names)]
    dh, dlg, rb = _in_proj_bwd(dproj, w4, x[0], lead, r1, dh2, ln_g, [cs_ for cs_, _ in sums])
    grad_x = dh[CHUNK:][None]
    dmeta = dh[LEAD:CHUNK]
    fulls = [_chip_sum(own, r, pos, "chip_sum_" + nm) for (_, own), r, nm in zip(sums, rb, names)]
    grad_w_in, grad_w_out, grad_pw = _rs_pair_share(fulls)
    d_win, nm_win, nv_win = _adamw_big(w_in[0], grad_w_in, m_w_in[0], v_w_in[0], "adamw_w_in")
    d_wo, nm_wo, nv_wo = _adamw_big(w_out[0], grad_w_out, m_w_out[0], v_w_out[0], "adamw_w_out")
    d_pw, nm_pw, nv_pw = _adamw_big(conv_pw_w[0], grad_pw, m_conv_pw_w[0], v_conv_pw_w[0], "adamw_pw_w")

    gsm, gdm, gmm = _gather_small(loss_l, dfg, dlg, dgn, ddwb, dclg, dclb, dpwb, ddww, dmeta)

    def pad_dw(a):
        return jnp.pad(a[0], ((0, HALO - CONV_K), (0, 0)))

    small_w = [ln_g, fg2, ret_gn_g, conv_dw_b, conv_ln_g, conv_ln_b, conv_pw_b, dw_pad, meta_tokens]
    small_m = [m_ln_g, m_final_g.reshape(1, d), m_ret_gn_g, m_conv_dw_b, m_conv_ln_g, m_conv_ln_b, m_conv_pw_b,
               pad_dw(m_conv_dw_w), m_meta_tokens]
    small_v = [v_ln_g, v_final_g.reshape(1, d), v_ret_gn_g, v_conv_dw_b, v_conv_ln_g, v_conv_ln_b, v_conv_pw_b,
               pad_dw(v_conv_dw_w), v_meta_tokens]
    loss, sg, sd, snm, snv = _small_update(gsm, gdm, gmm, small_w, small_m, small_v)

    def assemble(small, big_in, big_pw, big_out):
        ln, fg, gn, dwb, clg, clb, pwb, dww, meta = small
        return (meta, ln, big_in[None], gn, dww[:CONV_K][None], dwb, clg, clb, big_pw[None], pwb, big_out[None],
                fg.reshape(d))

    return (loss.reshape(()), grad_x,
            *assemble(sg, grad_w_in, grad_pw, grad_w_out),
            *assemble(sd, d_win, d_pw, d_wo),
            *assemble(snm, nm_win, nm_pw, nm_wo),
            *assemble(snv, nv_win, nv_pw, nv_wo))
```

```python
import functools

import jax
import jax.numpy as jnp
from jax import lax
from jax.experimental import pallas as pl
from jax.experimental.pallas import tpu as pltpu

F32 = jnp.float32
BF16 = jnp.bfloat16
MESH = pl.DeviceIdType.MESH

N_META = 16
CHUNK = 128
LEAD = (-N_META) % CHUNK
RET_HEADS = 4
CONV_K = 31
HALO = 32
ROPE_BASE = 10000.0
EPS = 1e-6
N_CHIPS = 4
N_DEV = 8

ADAM_LR = 0.001
ADAM_B1 = 0.9
ADAM_B2 = 0.999
ADAM_EPS = 1e-08
ADAM_WD = 0.01
ADAM_STEP = 10

MIB = 2 ** 20


def _params(n_grid_axes, vmem_mib):
    return pltpu.CompilerParams(dimension_semantics=("arbitrary",) * n_grid_axes,
                                vmem_limit_bytes=vmem_mib * MIB)


def _row_tile(n, pref):
    for t in (1664, 1280, 1024, 640, 512, 384, 320, 256, 128, 64, 32, 16, 8):
        if t <= pref and n % t == 0:
            return t
    raise ValueError(f"no row tile for {n}")


def _dot(a, b):
    return jnp.dot(a, b, preferred_element_type=F32)


def _dot_nt(a, b):
    return lax.dot_general(a, b, (((1,), (1,)), ((), ())), preferred_element_type=F32)


def _dot_tn(a, b):
    return lax.dot_general(a, b, (((0,), (0,)), ((), ())), preferred_element_type=F32)


def _sigmoid(x):
    return jax.nn.sigmoid(x)


def _dsilu(x, s):
    return s * (1.0 + x * (1.0 - s))


def _mean(x):
    return jnp.mean(x, axis=-1, keepdims=True)


def _colsum(x):
    return jnp.sum(x, axis=0, keepdims=True)


def _rope(x, cos, sin):
    half = x.shape[-1] // 2
    x1, x2 = x[:, :half], x[:, half:]
    return jnp.concatenate([x1 * cos - x2 * sin, x1 * sin + x2 * cos], axis=-1)


def _rope_t(d, cos, sin):
    half = d.shape[-1] // 2
    d1, d2 = d[:, :half], d[:, half:]
    return jnp.concatenate([d1 * cos + d2 * sin, d2 * cos - d1 * sin], axis=-1)


def _adamw(w, g, m, v):
    m = ADAM_B1 * m + (1.0 - ADAM_B1) * g
    v = ADAM_B2 * v + (1.0 - ADAM_B2) * (g * g)
    m_hat = m / (1.0 - ADAM_B1 ** ADAM_STEP)
    v_hat = v / (1.0 - ADAM_B2 ** ADAM_STEP)
    delta = -ADAM_LR * (m_hat / (jnp.sqrt(v_hat) + ADAM_EPS) + ADAM_WD * w)
    return delta, m, v


def _ret_consts():
    h = jnp.arange(RET_HEADS, dtype=F32)
    log_g = jnp.log(1.0 - jnp.exp2(-5.0 - h))
    idx = jnp.arange(CHUNK, dtype=F32)
    rel = idx[:, None] - idx[None, :]
    dmask = jnp.where(rel[None] >= 0, jnp.exp(jnp.maximum(rel, 0.0)[None] * log_g[:, None, None]), 0.0)
    qd = jnp.exp((idx[None, :] + 1.0) * log_g[:, None])[:, :, None]
    kd = jnp.exp((CHUNK - 1.0 - idx[None, :]) * log_g[:, None])[:, :, None]
    cd = jnp.exp(CHUNK * log_g)[:, None, None]
    return dmask, qd, kd, cd


def _rope_tables(n_rows, half):
    pos = jnp.arange(n_rows, dtype=F32) - float(LEAD)
    inv_freq = ROPE_BASE ** (-jnp.arange(half, dtype=F32) / half)
    ang = pos[:, None] * inv_freq[None, :]
    return jnp.cos(ang), jnp.sin(ang)


def _padded_tile_stream(x_hbm, lead_hbm, buf, sems, tm):
    def start_first(slot):
        pltpu.make_async_copy(lead_hbm, buf.at[slot, pl.ds(0, CHUNK)], sems.at[slot]).start()
        pltpu.make_async_copy(x_hbm.at[pl.ds(0, tm - CHUNK)], buf.at[slot, pl.ds(CHUNK, tm - CHUNK)],
                              sems.at[slot]).start()

    def start(slot, tile):
        pltpu.make_async_copy(x_hbm.at[pl.ds(tile * tm - CHUNK, tm)], buf.at[slot], sems.at[slot]).start()

    def wait(slot):
        pltpu.make_async_copy(x_hbm.at[pl.ds(0, tm)], buf.at[slot], sems.at[slot]).wait()

    return start_first, start, wait


def _cast_into_gathered(a, s_arr, name):
    rows, cols = a.shape
    tr = _row_tile(rows, 256)

    def body(s_ref, a_ref, o_ref):
        del s_ref
        o_ref[0] = a_ref[...].astype(BF16)

    return pl.pallas_call(
        body, name=name,
        grid_spec=pltpu.PrefetchScalarGridSpec(
            num_scalar_prefetch=1, grid=(rows // tr,),
            in_specs=[pl.BlockSpec((tr, cols), lambda i, s_ref: (i, 0))],
            out_specs=pl.BlockSpec((1, tr, cols), lambda i, s_ref: (s_ref[0], i, 0))),
        out_shape=jax.ShapeDtypeStruct((N_CHIPS, rows, cols), BF16),
        compiler_params=_params(1, 32))(s_arr, a)


def _in_proj_gather(order, x, lead, ln_g, w4, pw4, wo4):
    seq, d = x.shape
    lp = CHUNK + seq
    ns = w4.shape[2]
    tm = _row_tile(lp, 640)
    nt = lp // tm
    assert nt >= 2, "the hn write-back of a row tile is waited for one step later, before any pass re-reads it"
    land_step, load_step = max(nt - 3, 0), max(nt - 2, 0)
    gathered = (w4, pw4, wo4)
    halves = [a.shape[1] // 2 for a in gathered]
    n = len(gathered)

    def body(order_ref, x_hbm, lead_hbm, g_ref, w_in, pw_in, wo_in, proj_ref, r_ref, hn_hbm, w_out, pw_out,
             wo_out, wbuf, hnbuf, hbuf, send_sems, recv_sems, hn_out_sems, hn_in_sems, h_sems, w_sem):
        del order_ref, w_in, pw_in, wo_in
        t, i = pl.program_id(0), pl.program_id(1)
        slot = (t * nt + i) % 2
        h_first, h_start, h_wait = _padded_tile_stream(x_hbm, lead_hbm, hbuf, h_sems, tm)
        x, y, c = _mesh_pos()
        me, sibling = (x, y, c), (x, y, 1 - c)
        my_s = 2 * x + y
        chips = _other_chips(x, y)
        outs = (w_out, pw_out, wo_out)

        def half(k, s, cc):
            return outs[k].at[s, pl.ds(cc * halves[k], halves[k])]

        def rcopy(k, j, rows, to):
            return pltpu.make_async_remote_copy(
                src_ref=rows, dst_ref=rows, send_sem=send_sems.at[6 * k + j], recv_sem=recv_sems.at[6 * k + j],
                device_id=to, device_id_type=MESH)

        def send(k, j):
            return rcopy(k, j, half(k, my_s, c), (*chips[j], c))

        def shard_of(j):
            return 2 * chips[j][0] + chips[j][1]

        def forward(k, j):
            return rcopy(k, 3 + j, half(k, shard_of(j), c), sibling)

        def land(k, j):
            rcopy(k, j, half(k, shard_of(j), c), me).wait_recv()
            forward(k, j).start()

        def landed_from_sibling(k, j):
            rcopy(k, 3 + j, half(k, shard_of(j), 1 - c), me).wait_recv()

        def load_w(s, wslot):
            return pltpu.make_async_copy(w_out.at[s], wbuf.at[wslot], w_sem)

        def hn_out(sl, row_tile):
            return pltpu.make_async_copy(hnbuf.at[sl], hn_hbm.at[pl.ds(row_tile * tm, tm)], hn_out_sems.at[sl])

        def hn_in(sl, row_tile):
            return pltpu.make_async_copy(hn_hbm.at[pl.ds(row_tile * tm, tm)], hnbuf.at[sl], hn_in_sems.at[sl])

        @pl.when((t == 0) & (i == 0))
        def _():
            for j in range(2):
                send(0, j).start()
            load_w(my_s, 0).start()
            load_w(my_s, 0).wait()

        @pl.when((t == 1) & (i == 0))
        def _():
            send(0, 2).start()
            for k in range(1, n):
                for jj in range(3):
                    send(k, jj).start()

        for j in range(3):
            @pl.when((t == j) & (i == land_step))
            def _(j=j):
                land(0, j)

            @pl.when((t == j) & (i == load_step))
            def _(j=j):
                landed_from_sibling(0, j)
                load_w(shard_of(j), (j + 1) % 2).start()

            @pl.when((t == j + 1) & (i == 0))
            def _(j=j):
                load_w(shard_of(j), (j + 1) % 2).wait()

        @pl.when((t == 0) & (i == 0))
        def _():
            h_first(0)

        @pl.when((t == 0) & (i + 1 < nt))
        def _():
            h_start((i + 1) % 2, i + 1)

        @pl.when(t == 0)
        def _():
            h_wait(i % 2)
            h = hbuf[i % 2]
            r = lax.rsqrt(_mean(h * h) + EPS)
            hnbuf[slot] = ((h * r) * g_ref[...]).astype(BF16)
            r_ref[...] = r
            hn_out(slot, i).start()

        @pl.when(t > 0)
        def _():
            hn_in(slot, i).wait()

        @pl.when(((t == 0) & (i > 0)) | ((t == 1) & (i == 0)))
        def _():
            hn_out(1 - slot, jnp.where(i > 0, i - 1, nt - 1)).wait()

        last = (t == N_CHIPS - 1) & (i == nt - 1)

        @pl.when(((t > 0) | (i == nt - 1)) & jnp.logical_not(last))
        def _():
            hn_in(1 - slot, jnp.where(i == nt - 1, 0, i + 1)).start()

        proj_ref[...] = _dot(hnbuf[slot], wbuf[t % 2])

        @pl.when(last)
        def _():
            for k in range(1, n):
                for j in range(3):
                    land(k, j)
            for k in range(1, n):
                for j in range(3):
                    landed_from_sibling(k, j)
            for k in range(n):
                for j in range(3):
                    send(k, j).wait_send()
                    forward(k, j).wait_send()

    def frozen(t, i):
        return jnp.where(t == 0, i, nt - 1)

    any_spec = pl.BlockSpec(memory_space=pl.ANY)
    return pl.pallas_call(
        body, name="in_proj_gather",
        grid_spec=pltpu.PrefetchScalarGridSpec(
            num_scalar_prefetch=1, grid=(N_CHIPS, nt),
            in_specs=[any_spec, any_spec, pl.BlockSpec((1, d), lambda t, i, o: (0, 0)),
                      any_spec, any_spec, any_spec],
            out_specs=[pl.BlockSpec((tm, ns), lambda t, i, o: (i, o[t])),
                       pl.BlockSpec((tm, 1), lambda t, i, o: (frozen(t, i), 0)),
                       any_spec, any_spec, any_spec, any_spec],
            scratch_shapes=[pltpu.VMEM((2, d, ns), BF16), pltpu.VMEM((2, tm, d), BF16), pltpu.VMEM((2, tm, d), F32),
                            pltpu.SemaphoreType.DMA((6 * n,)), pltpu.SemaphoreType.DMA((6 * n,)),
                            pltpu.SemaphoreType.DMA((2,)), pltpu.SemaphoreType.DMA((2,)),
                            pltpu.SemaphoreType.DMA((2,)), pltpu.SemaphoreType.DMA]),
        out_shape=[jax.ShapeDtypeStruct((lp, N_CHIPS * ns), F32),
                   jax.ShapeDtypeStruct((lp, 1), F32),
                   jax.ShapeDtypeStruct((lp, d), BF16)]
                  + [jax.ShapeDtypeStruct(a.shape, a.dtype) for a in gathered],
        input_output_aliases={4: 3, 5: 4, 6: 5},
        compiler_params=_params(2, 48))(order, x, lead, ln_g, w4, pw4, wo4)


def _ret_fwd(proj, cos, sin, gn_g, consts):
    lp = proj.shape[0]
    w = gn_g.shape[1]
    hd = w // RET_HEADS
    nch = lp // CHUNK
    dmask, qd, kd, cd = consts

    def body(q_ref, k_ref, v_ref, g_ref, cos_ref, sin_ref, gn_ref, m_ref, qd_ref, kd_ref, cd_ref,
             y_ref, ssave_ref, ops_ref, sc_ref, out_ref, s_scr):
        @pl.when(pl.program_id(0) == 0)
        def _():
            s_scr[...] = jnp.zeros_like(s_scr)

        cos_t, sin_t = cos_ref[...], sin_ref[...]
        for h in range(RET_HEADS):
            sl = slice(h * hd, (h + 1) * hd)
            qr = _rope(q_ref[:, sl], cos_t, sin_t)
            kr = _rope(k_ref[:, sl], cos_t, sin_t) * (hd ** -0.5)
            qb, kb = qr.astype(BF16), kr.astype(BF16)
            qdq, kdk = (qr * qd_ref[h]).astype(BF16), (kr * kd_ref[h]).astype(BF16)
            vb = v_ref[:, sl].astype(BF16)
            for k, operand in enumerate((qb, kb, qdq, kdk, vb)):
                ops_ref[k, :, sl] = operand
            scb = (_dot_nt(qb, kb) * m_ref[h]).astype(BF16)
            sc_ref[0, h] = scb
            state = s_scr[h]
            sb = state.astype(BF16)
            ssave_ref[0, h] = sb
            out = _dot(scb, vb) + _dot(qdq, sb)
            out_ref[:, sl] = out
            s_scr[h] = state * cd_ref[h] + _dot_tn(kdk, vb)
            dev = out - _mean(out)
            yn = dev * lax.rsqrt(_mean(dev * dev) + EPS)
            g = g_ref[:, sl]
            y_ref[:, sl] = ((yn * gn_ref[:, sl]) * (g * _sigmoid(g))).astype(BF16)

    def col(j):
        return pl.BlockSpec((CHUNK, w), lambda i: (i, j))

    def whole(a):
        return pl.BlockSpec(a.shape, lambda i: (0,) * a.ndim)

    return pl.pallas_call(
        body, name="ret_fwd", grid=(nch,),
        in_specs=[col(0), col(1), col(2), col(3),
                  pl.BlockSpec((CHUNK, hd // 2), lambda i: (i, 0)),
                  pl.BlockSpec((CHUNK, hd // 2), lambda i: (i, 0)),
                  whole(gn_g), whole(dmask), whole(qd), whole(kd), whole(cd)],
        out_specs=[pl.BlockSpec((CHUNK, w), lambda i: (i, 0)),
                   pl.BlockSpec((1, RET_HEADS, hd, hd), lambda i: (i, 0, 0, 0)),
                   pl.BlockSpec((5, CHUNK, w), lambda i: (0, i, 0)),
                   pl.BlockSpec((1, RET_HEADS, CHUNK, CHUNK), lambda i: (i, 0, 0, 0)),
                   pl.BlockSpec((CHUNK, w), lambda i: (i, 0))],
        out_shape=[jax.ShapeDtypeStruct((lp, w), BF16),
                   jax.ShapeDtypeStruct((nch, RET_HEADS, hd, hd), BF16),
                   jax.ShapeDtypeStruct((5, lp, w), BF16),
                   jax.ShapeDtypeStruct((nch, RET_HEADS, CHUNK, CHUNK), BF16),
                   jax.ShapeDtypeStruct((lp, w), F32)],
        scratch_shapes=[pltpu.VMEM((RET_HEADS, hd, hd), F32)],
        compiler_params=_params(1, 32))(proj, proj, proj, proj, cos, sin, gn_g, dmask, qd, kd, cd)


def _tap_groups(start, flip):
    groups = {}
    for j in range(CONV_K):
        o = start + (CONV_K - 1 - j if flip else j)
        groups.setdefault(o % 8, []).append((o // 8, j))
    return groups


def _shift_up(win, s):
    return win if s == 0 else pltpu.roll(win, win.shape[0] - s, axis=0)


def _dw_taps(src_ref, w_ref, dst_ref, bias, *, rows, start, flip, rb):
    cw = dst_ref.shape[1]
    lb = min(128, cw)
    groups = _tap_groups(start, flip)

    def rb_body(r, carry):
        base = pl.multiple_of(r * rb, rb)
        for cb in range(cw // lb):
            ls = slice(cb * lb, (cb + 1) * lb)
            win = src_ref[pl.ds(base, rb + HALO), ls]
            acc = jnp.zeros((rb, lb), F32) if bias is None else jnp.broadcast_to(bias[:, ls], (rb, lb))
            for s, taps in groups.items():
                ws = _shift_up(win, s)
                for a, j in taps:
                    acc = acc + ws[8 * a:8 * a + rb, :] * w_ref[j:j + 1, ls]
            dst_ref[pl.ds(base, rb), ls] = acc
        return carry

    lax.fori_loop(0, rows // rb, rb_body, 0)


def _conv_fwd(proj, dw_w, dw_b, cln_g, cln_b, pw_w, pw_b):
    lp = proj.shape[0]
    cw = dw_b.shape[1]
    tm = _row_tile(lp, 640)
    rb = _row_tile(tm, 64)

    def body(a_ref, b_ref, gc_ref, w_ref, wb_ref, lg_ref, lb_ref, pw_ref, pb_ref, y_ref, u1_ref, buf):
        @pl.when(pl.program_id(0) == 0)
        def _():
            buf[0:HALO, :] = jnp.zeros((HALO, cw), F32)

        buf[HALO:HALO + tm, :] = a_ref[...] * _sigmoid(b_ref[...])
        _dw_taps(buf, w_ref, u1_ref, wb_ref[...], rows=tm, start=HALO - (CONV_K - 1), flip=False, rb=rb)
        buf[0:HALO, :] = buf[tm:tm + HALO, :]
        u1 = u1_ref[...]
        dev = u1 - _mean(u1)
        z = dev * lax.rsqrt(_mean(dev * dev) + EPS) * lg_ref[...] + lb_ref[...]
        u3 = (z * _sigmoid(z)).astype(BF16)
        u4 = _dot(u3, pw_ref[...]) + pb_ref[...]
        gc = gc_ref[...]
        y_ref[...] = (u4 * (gc * _sigmoid(gc))).astype(BF16)

    def col(j):
        return pl.BlockSpec((tm, cw), lambda i: (i, j))

    def whole(a):
        return pl.BlockSpec(a.shape, lambda i: (0,) * a.ndim)

    return pl.pallas_call(
        body, name="conv_fwd", grid=(lp // tm,),
        in_specs=[col(4), col(5), col(6), whole(dw_w), whole(dw_b), whole(cln_g), whole(cln_b),
                  whole(pw_w), whole(pw_b)],
        out_specs=[pl.BlockSpec((tm, cw), lambda i: (i, 0)), pl.BlockSpec((tm, cw), lambda i: (i, 0))],
        out_shape=[jax.ShapeDtypeStruct((lp, cw), BF16), jax.ShapeDtypeStruct((lp, cw), F32)],
        scratch_shapes=[pltpu.VMEM((tm + HALO, cw), F32)],
        compiler_params=_params(1, 48))(proj, proj, proj, dw_w, dw_b, cln_g, cln_b, pw_w, pw_b)


def _out_proj_loss(yr, yc, x, lead, tgt, zero_lead, w_out, final_g):
    seq, d = x.shape
    lp = CHUNK + seq
    w = yr.shape[1]
    tm = _row_tile(lp, 320)
    nt = lp // tm
    assert tm > CHUNK

    def body(yr_ref, yc_ref, x_hbm, lead_hbm, t_hbm, zlead_hbm, w_ref, fg_ref, dh2_ref, dy_ref, loss_ref, dfg_ref,
             hbuf, tbuf, hsems, tsems):
        i = pl.program_id(0)
        slot = i % 2
        streams = (_padded_tile_stream(x_hbm, lead_hbm, hbuf, hsems, tm),
                   _padded_tile_stream(t_hbm, zlead_hbm, tbuf, tsems, tm))

        @pl.when(i == 0)
        def _():
            loss_ref[...] = jnp.zeros_like(loss_ref)
            dfg_ref[...] = jnp.zeros_like(dfg_ref)
            for start_first, _, _ in streams:
                start_first(0)

        @pl.when(i + 1 < nt)
        def _():
            for _, start, _ in streams:
                start(1 - slot, i + 1)

        for _, _, wait in streams:
            wait(slot)

        h2 = hbuf[slot] + (_dot(yr_ref[...], w_ref[0:w, :]) + _dot(yc_ref[...], w_ref[w:2 * w, :]))
        r2 = lax.rsqrt(_mean(h2 * h2) + EPS)
        h2n = h2 * r2
        fg = fg_ref[...]
        rows = i * tm + lax.broadcasted_iota(jnp.int32, (tm, 1), 0)
        err = jnp.where(rows >= CHUNK, h2n * fg - tbuf[slot], 0.0)
        loss_ref[...] += _colsum(err * err)
        dout = err * (1.0 / d)
        dfg_ref[...] += _colsum(dout * h2n)
        dz = dout * fg
        dh2 = r2 * (dz - h2n * _mean(dz * h2n))
        dh2_ref[...] = dh2
        db = dh2.astype(BF16)
        dy_ref[:, 0:w] = _dot_nt(db, w_ref[0:w, :])
        dy_ref[:, w:2 * w] = _dot_nt(db, w_ref[w:2 * w, :])

    def row(cols):
        return pl.BlockSpec((tm, cols), lambda i: (i, 0))

    any_spec = pl.BlockSpec(memory_space=pl.ANY)
    return pl.pallas_call(
        body, name="out_proj_loss", grid=(nt,),
        in_specs=[row(w), row(w), any_spec, any_spec, any_spec, any_spec,
                  pl.BlockSpec(memory_space=pltpu.VMEM),
                  pl.BlockSpec((1, d), lambda i: (0, 0))],
        out_specs=[row(d), row(2 * w), pl.BlockSpec((1, d), lambda i: (0, 0)),
                   pl.BlockSpec((1, d), lambda i: (0, 0))],
        out_shape=[jax.ShapeDtypeStruct((lp, d), F32), jax.ShapeDtypeStruct((lp, 2 * w), F32),
                   jax.ShapeDtypeStruct((1, d), F32), jax.ShapeDtypeStruct((1, d), F32)],
        scratch_shapes=[pltpu.VMEM((2, tm, d), F32), pltpu.VMEM((2, tm, d), F32),
                        pltpu.SemaphoreType.DMA((2,)), pltpu.SemaphoreType.DMA((2,))],
        compiler_params=_params(1, 56))(yr, yc, x, lead, tgt, zero_lead, w_out, final_g)


def _dw_out(yr, yc, dh2):
    lp, d = dh2.shape
    w = yr.shape[1]
    tm = _row_tile(lp, 1664)
    nb = 2
    dn = d // nb

    def body(yr_ref, yc_ref, d_ref, o_ref):
        @pl.when(pl.program_id(1) == 0)
        def _():
            o_ref[...] = jnp.zeros_like(o_ref)

        db = d_ref[...].astype(BF16)
        o_ref[0:w, :] += _dot_tn(yr_ref[...], db)
        o_ref[w:2 * w, :] += _dot_tn(yc_ref[...], db)

    return pl.pallas_call(
        body, name="dw_out", grid=(nb, lp // tm),
        in_specs=[pl.BlockSpec((tm, w), lambda n, i: (i, 0)),
                  pl.BlockSpec((tm, w), lambda n, i: (i, 0)),
                  pl.BlockSpec((tm, dn), lambda n, i: (i, n))],
        out_specs=pl.BlockSpec((2 * w, dn), lambda n, i: (0, n)),
        out_shape=jax.ShapeDtypeStruct((2 * w, d), F32),
        compiler_params=_params(2, 52))(yr, yc, dh2)


def _ret_bwd(proj, dy, ssave, ops, scores, out_pre, cos, sin, gn_g, consts):
    lp = proj.shape[0]
    w = gn_g.shape[1]
    hd = w // RET_HEADS
    nch = lp // CHUNK
    dmask, qd, kd, cd = consts

    def body(g_ref, dy_ref, ss_ref, ops_ref, sc_ref, out_ref, cos_ref, sin_ref, gn_ref, m_ref, qd_ref, kd_ref,
             cd_ref, dp_ref, dgn_ref, ds_scr):
        @pl.when(pl.program_id(0) == 0)
        def _():
            ds_scr[...] = jnp.zeros_like(ds_scr)
            dgn_ref[...] = jnp.zeros_like(dgn_ref)

        cos_t, sin_t = cos_ref[...], sin_ref[...]
        for h in range(RET_HEADS):
            sl = slice(h * hd, (h + 1) * hd)
            qb, kb, qdq, kdk, vb = (ops_ref[k, :, sl] for k in range(5))
            sb = ss_ref[0, h]
            scb = sc_ref[0, h]
            mask = m_ref[h]
            qdec, kdec = qd_ref[h], kd_ref[h]
            out = out_ref[:, sl]
            dev = out - _mean(out)
            rstd = lax.rsqrt(_mean(dev * dev) + EPS)
            yn = dev * rstd
            g = g_ref[:, sl]
            sg = _sigmoid(g)
            gng = gn_ref[:, sl]
            dyv = dy_ref[:, sl]
            dgr = dyv * (yn * gng) * _dsilu(g, sg)
            silu_g = g * sg
            dgn_ref[:, sl] += _colsum(dyv * yn * silu_g)
            dyn = dyv * gng * silu_g
            dout = rstd * (dyn - _mean(dyn) - yn * _mean(dyn * yn))
            dob = dout.astype(BF16)
            by_v_and_s = _dot_nt(dob, jnp.concatenate([vb, sb], axis=0))
            dscb = (by_v_and_s[:, 0:CHUNK] * mask).astype(BF16)
            dstate = ds_scr[h]
            dsb = dstate.astype(BF16)
            dq = _dot(dscb, kb) + by_v_and_s[:, CHUNK:] * qdec
            dk = _dot_tn(dscb, qb) + _dot_nt(vb, dsb) * kdec
            onto_do = _dot_tn(jnp.concatenate([scb, qdq], axis=1), dob)
            dv = onto_do[0:CHUNK] + _dot(kdk, dsb)
            ds_scr[h] = dstate * cd_ref[h] + onto_do[CHUNK:]
            dp_ref[:, 0 * w + h * hd:0 * w + (h + 1) * hd] = _rope_t(dq, cos_t, sin_t).astype(BF16)
            dp_ref[:, 1 * w + h * hd:1 * w + (h + 1) * hd] = (_rope_t(dk, cos_t, sin_t) * (hd ** -0.5)).astype(BF16)
            dp_ref[:, 2 * w + h * hd:2 * w + (h + 1) * hd] = dv.astype(BF16)
            dp_ref[:, 3 * w + h * hd:3 * w + (h + 1) * hd] = dgr.astype(BF16)

    def rev(i):
        return nch - 1 - i

    def col(j):
        return pl.BlockSpec((CHUNK, w), lambda i: (rev(i), j))

    def whole(a):
        return pl.BlockSpec(a.shape, lambda i: (0,) * a.ndim)

    return pl.pallas_call(
        body, name="ret_bwd", grid=(nch,),
        in_specs=[col(3),
                  pl.BlockSpec((CHUNK, w), lambda i: (rev(i), 0)),
                  pl.BlockSpec((1, RET_HEADS, hd, hd), lambda i: (rev(i), 0, 0, 0)),
                  pl.BlockSpec((5, CHUNK, w), lambda i: (0, rev(i), 0)),
                  pl.BlockSpec((1, RET_HEADS, CHUNK, CHUNK), lambda i: (rev(i), 0, 0, 0)),
                  pl.BlockSpec((CHUNK, w), lambda i: (rev(i), 0)),
                  pl.BlockSpec((CHUNK, hd // 2), lambda i: (rev(i), 0)),
                  pl.BlockSpec((CHUNK, hd // 2), lambda i: (rev(i), 0)),
                  whole(gn_g), whole(dmask), whole(qd), whole(kd), whole(cd)],
        out_specs=[pl.BlockSpec((CHUNK, 4 * w), lambda i: (rev(i), 0)),
                   pl.BlockSpec((1, w), lambda i: (0, 0))],
        out_shape=[jax.ShapeDtypeStruct((lp, 7 * w), BF16), jax.ShapeDtypeStruct((1, w), F32)],
        scratch_shapes=[pltpu.VMEM((RET_HEADS, hd, hd), F32)],
        compiler_params=_params(1, 32))(proj, dy, ssave, ops, scores, out_pre, cos, sin, gn_g, dmask, qd, kd, cd)


def _conv_bwd_pw(dy, proj, u1, cln_g, cln_b, pw_w, pw_b, dproj):
    lp, cw = u1.shape
    tm = _row_tile(lp, 320)

    def body(dy_ref, gc_ref, u1_ref, lg_ref, lb_ref, pw_ref, pb_ref, dp_in, dp_ref, du1_ref, dpw_ref,
             dpb_ref, dlg_ref, dlb_ref):
        del dp_in

        @pl.when(pl.program_id(0) == 0)
        def _():
            dpw_ref[...] = jnp.zeros_like(dpw_ref)
            dpb_ref[...] = jnp.zeros_like(dpb_ref)
            dlg_ref[...] = jnp.zeros_like(dlg_ref)
            dlb_ref[...] = jnp.zeros_like(dlb_ref)

        u1 = u1_ref[...]
        dev = u1 - _mean(u1)
        rstd = lax.rsqrt(_mean(dev * dev) + EPS)
        u1n = dev * rstd
        lg = lg_ref[...]
        z = u1n * lg + lb_ref[...]
        sz = _sigmoid(z)
        u3b = (z * sz).astype(BF16)
        u4 = _dot(u3b, pw_ref[...]) + pb_ref[...]
        gc = gc_ref[...]
        sgc = _sigmoid(gc)
        dyc = dy_ref[...]
        du4 = dyc * (gc * sgc)
        dp_ref[...] = (dyc * u4 * _dsilu(gc, sgc)).astype(BF16)
        du4b = du4.astype(BF16)
        dpb_ref[...] += _colsum(du4)
        dpw_ref[...] += _dot_tn(u3b, du4b)
        dz = _dot_nt(du4b, pw_ref[...]) * _dsilu(z, sz)
        dlg_ref[...] += _colsum(dz * u1n)
        dlb_ref[...] += _colsum(dz)
        dn = dz * lg
        du1_ref[...] = rstd * (dn - _mean(dn) - u1n * _mean(dn * u1n))

    def row(j):
        return pl.BlockSpec((tm, cw), lambda i: (i, j))

    def whole(a):
        return pl.BlockSpec(a.shape, lambda i: (0,) * a.ndim)

    def acc(r):
        return pl.BlockSpec((r, cw), lambda i: (0, 0))

    return pl.pallas_call(
        body, name="conv_bwd_pw", grid=(lp // tm,),
        in_specs=[row(1), row(6), row(0), whole(cln_g), whole(cln_b), whole(pw_w), whole(pw_b),
                  pl.BlockSpec(memory_space=pl.ANY)],
        out_specs=[row(6), row(0), acc(cw), acc(1), acc(1), acc(1)],
        out_shape=[jax.ShapeDtypeStruct(dproj.shape, dproj.dtype), jax.ShapeDtypeStruct((lp, cw), F32),
                   jax.ShapeDtypeStruct((cw, cw), F32), jax.ShapeDtypeStruct((1, cw), F32),
                   jax.ShapeDtypeStruct((1, cw), F32), jax.ShapeDtypeStruct((1, cw), F32)],
        input_output_aliases={7: 0},
        compiler_params=_params(1, 48))(dy, proj, u1, cln_g, cln_b, pw_w, pw_b, dproj)


def _conv_bwd_dw(du1, proj, dw_w, dproj):
    lp, cw = du1.shape
    tm = _row_tile(lp, 640)
    rb = _row_tile(tm, 64)
    nt = lp // tm
    hb = tm // HALO

    def body(a_ref, b_ref, du_ref, nx_ref, w_ref, dp_in, dp_ref, dww_ref, dwb_ref, ubuf, dbuf, du0, acc):
        del dp_in
        i = pl.program_id(0)

        @pl.when(i == 0)
        def _():
            ubuf[0:HALO, :] = jnp.zeros((HALO, cw), F32)
            acc[...] = jnp.zeros_like(acc)
            dwb_ref[...] = jnp.zeros_like(dwb_ref)

        a = a_ref[...]
        sb = _sigmoid(b_ref[...])
        ubuf[HALO:HALO + tm, :] = a * sb
        du = du_ref[...]
        dbuf[0:tm, :] = du
        dbuf[tm:tm + HALO, :] = jnp.where(i == nt - 1, 0.0, nx_ref[...])
        dwb_ref[...] += _colsum(du)
        _dw_taps(dbuf, w_ref, du0, None, rows=tm, start=0, flip=True, rb=rb)
        d0 = du0[...]
        dp_ref[:, 0:cw] = (d0 * sb).astype(BF16)
        dp_ref[:, cw:2 * cw] = (d0 * a * sb * (1.0 - sb)).astype(BF16)

        lb = min(128, cw)
        groups = _tap_groups(HALO - (CONV_K - 1), False)

        def rb_body(r, carry):
            base = pl.multiple_of(r * rb, rb)
            for cb in range(cw // lb):
                ls = slice(cb * lb, (cb + 1) * lb)
                win = ubuf[pl.ds(base, rb + HALO), ls]
                dv = dbuf[pl.ds(base, rb), ls]
                for s, taps in groups.items():
                    ws = _shift_up(win, s)
                    for a, j in taps:
                        prod = dv * ws[8 * a:8 * a + rb, :]
                        acc[8 * j:8 * j + 8, ls] += jnp.sum(prod.reshape(rb // 8, 8, lb), axis=0)
            return carry

        lax.fori_loop(0, tm // rb, rb_body, 0)
        ubuf[0:HALO, :] = ubuf[tm:tm + HALO, :]

        @pl.when(i == nt - 1)
        def _():
            for j in range(CONV_K):
                dww_ref[j:j + 1, :] = _colsum(acc[8 * j:8 * j + 8, :])
            dww_ref[CONV_K:HALO, :] = jnp.zeros((HALO - CONV_K, cw), F32)

    def col(j):
        return pl.BlockSpec((tm, cw), lambda i: (i, j))

    return pl.pallas_call(
        body, name="conv_bwd_dw", grid=(nt,),
        in_specs=[col(4), col(5), col(0),
                  pl.BlockSpec((HALO, cw), lambda i: (jnp.minimum((i + 1) * hb, nt * hb - 1), 0)),
                  pl.BlockSpec(dw_w.shape, lambda i: (0, 0)),
                  pl.BlockSpec(memory_space=pl.ANY)],
        out_specs=[pl.BlockSpec((tm, 2 * cw), lambda i: (i, 2)),
                   pl.BlockSpec((HALO, cw), lambda i: (0, 0)),
                   pl.BlockSpec((1, cw), lambda i: (0, 0))],
        out_shape=[jax.ShapeDtypeStruct(dproj.shape, dproj.dtype), jax.ShapeDtypeStruct((HALO, cw), F32),
                   jax.ShapeDtypeStruct((1, cw), F32)],
        scratch_shapes=[pltpu.VMEM((tm + HALO, cw), F32), pltpu.VMEM((tm + HALO, cw), F32),
                        pltpu.VMEM((tm, cw), F32), pltpu.VMEM((8 * HALO, cw), F32)],
        input_output_aliases={5: 0},
        compiler_params=_params(1, 56))(proj, proj, du1, du1, dw_w, dproj)


def _dw_in(hn, dproj, ns, others):
    lp, d = hn.shape
    tm = _row_tile(lp, 1664)
    nt = lp // tm
    mb = 512 if d % 512 == 0 else d
    n = len(others)
    halves = [d // 2] + [g.shape[1] // 2 for g in others]

    def body(hn_ref, dp_ref, *refs):
        other_refs, o_hbm, recv_refs = refs[:n], refs[n], refs[n + 1:2 * n + 2]
        acc, sem, send_sems, recv_sems = refs[2 * n + 2:]
        s, i = pl.program_id(0), pl.program_id(1)
        x, y, c = _mesh_pos()

        def to_sibling(src, dst, k):
            return pltpu.make_async_remote_copy(
                src_ref=src, dst_ref=dst, send_sem=send_sems.at[k], recv_sem=recv_sems.at[k],
                device_id=(x, y, 1 - c), device_id_type=MESH)

        def shard_half(p):
            return to_sibling(o_hbm.at[p, pl.ds((1 - c) * halves[0], halves[0])], recv_refs[0].at[p], p)

        def other_halves(k):
            return to_sibling(other_refs[k].at[:, pl.ds((1 - c) * halves[1 + k], halves[1 + k])],
                              recv_refs[1 + k], N_CHIPS + k)

        @pl.when((s == 0) & (i == 0))
        def _():
            for k in range(n):
                other_halves(k).start()

        @pl.when(i == 0)
        def _():
            acc[...] = jnp.zeros_like(acc)

        for m in range(d // mb):
            rows = slice(m * mb, (m + 1) * mb)
            acc[rows, :] += _dot_tn(hn_ref[:, rows], dp_ref[...])

        @pl.when(i == nt - 1)
        def _():
            cp = pltpu.make_async_copy(acc, o_hbm.at[s], sem)
            cp.start()
            cp.wait()
            shard_half(s).start()

        @pl.when((s == N_CHIPS - 1) & (i == nt - 1))
        def _():
            for p in range(N_CHIPS):
                shard_half(p).wait()
            for k in range(n):
                other_halves(k).wait()

    any_spec = pl.BlockSpec(memory_space=pl.ANY)
    outs = pl.pallas_call(
        body, name="dw_in", grid=(N_CHIPS, nt),
        in_specs=[pl.BlockSpec((tm, d), lambda s, i: (i, 0)),
                  pl.BlockSpec((tm, ns), lambda s, i: (i, s))] + [any_spec] * n,
        out_specs=[any_spec] * (n + 2),
        out_shape=[jax.ShapeDtypeStruct((N_CHIPS, d, ns), F32), jax.ShapeDtypeStruct((N_CHIPS, d // 2, ns), F32)]
                  + [jax.ShapeDtypeStruct((N_CHIPS, g.shape[1] // 2) + g.shape[2:], g.dtype) for g in others],
        scratch_shapes=[pltpu.VMEM((d, ns), F32), pltpu.SemaphoreType.DMA,
                        pltpu.SemaphoreType.DMA((N_CHIPS + n,)), pltpu.SemaphoreType.DMA((N_CHIPS + n,))],
        compiler_params=_params(2, 56))(hn, dproj, *others)
    return outs[0], outs[1:]


def _in_proj_bwd(dproj, w4, x, lead, r1, dh2, ln_g, cs):
    seq, d = x.shape
    lp = CHUNK + seq
    ns = w4.shape[2]
    tm = _row_tile(lp, 320)
    nt = lp // tm
    n = len(cs)
    assert tm > CHUNK

    def body(dp_ref, w_ref, x_hbm, lead_hbm, r_ref, d2_ref, g_ref, *refs):
        cs_refs, (dh_ref, dlg_ref), rb_refs = refs[:n], refs[n:n + 2], refs[n + 2:2 * n + 2]
        hbuf, h_sems, send_sems, recv_sems = refs[2 * n + 2:]
        i = pl.program_id(0)
        slot = i % 2
        x, y, c = _mesh_pos()
        h_first, h_start, h_wait = _padded_tile_stream(x_hbm, lead_hbm, hbuf, h_sems, tm)

        @pl.when(i == 0)
        def _():
            h_first(0)

        @pl.when(i + 1 < nt)
        def _():
            h_start(1 - slot, i + 1)

        h_wait(slot)

        def exchange():
            return [pltpu.make_async_remote_copy(
                src_ref=cs_refs[k].at[2 * chip[0] + chip[1]], dst_ref=rb_refs[k].at[j],
                send_sem=send_sems.at[3 * k + j], recv_sem=recv_sems.at[3 * k + j],
                device_id=(*chip, c), device_id_type=MESH)
                for k in range(n) for j, chip in enumerate(_other_chips(x, y))]

        @pl.when(i == 0)
        def _():
            dlg_ref[...] = jnp.zeros_like(dlg_ref)
            for cp in exchange():
                cp.start()

        dhn = _dot_nt(dp_ref[:, 0:ns], w_ref[0])
        for s in range(1, N_CHIPS):
            dhn = dhn + _dot_nt(dp_ref[:, s * ns:(s + 1) * ns], w_ref[s])
        r = r_ref[...]
        hn0 = hbuf[slot] * r
        dlg_ref[...] += _colsum(dhn * hn0)
        t = dhn * g_ref[...]
        dh_ref[...] = d2_ref[...] + r * (t - hn0 * _mean(t * hn0))

        @pl.when(i == nt - 1)
        def _():
            for cp in exchange():
                cp.wait()

    def row(cols):
        return pl.BlockSpec((tm, cols), lambda i: (i, 0))

    any_spec = pl.BlockSpec(memory_space=pl.ANY)
    outs = pl.pallas_call(
        body, name="in_proj_bwd", grid=(nt,),
        in_specs=[row(N_CHIPS * ns), pl.BlockSpec(memory_space=pltpu.VMEM),
                  any_spec, any_spec, row(1), row(d), pl.BlockSpec((1, d), lambda i: (0, 0))] + [any_spec] * n,
        out_specs=[row(d), pl.BlockSpec((1, d), lambda i: (0, 0))] + [any_spec] * n,
        out_shape=[jax.ShapeDtypeStruct((lp, d), F32), jax.ShapeDtypeStruct((1, d), F32)]
                  + [jax.ShapeDtypeStruct((3,) + a.shape[1:], a.dtype) for a in cs],
        scratch_shapes=[pltpu.VMEM((2, tm, d), F32), pltpu.SemaphoreType.DMA((2,)),
                        pltpu.SemaphoreType.DMA((3 * n,)), pltpu.SemaphoreType.DMA((3 * n,))],
        compiler_params=_params(1, 58))(dproj, w4, x, lead, r1, dh2, ln_g, *cs)
    return outs[0], outs[1], outs[2:]


def _mesh_pos():
    return lax.axis_index("x"), lax.axis_index("y"), lax.axis_index("c")


def _other_chips(x, y):
    return [(1 - x, y), (x, 1 - y), (1 - x, 1 - y)]


def _gather_shards(shards):
    n = len(shards)
    halves = [a.shape[0] // 2 for a in shards]

    def body(*refs):
        ins, outs = refs[:n], refs[n:2 * n]
        send_sems, recv_sems, loc_sems = refs[2 * n:]
        x, y, c = _mesh_pos()
        me, sibling = (x, y, c), (x, y, 1 - c)
        my_s = 2 * x + y
        chips = _other_chips(x, y)

        def half(k, s, cc):
            return outs[k].at[s, pl.ds(cc * halves[k], halves[k])]

        def rcopy(k, j, src, dst, to):
            return pltpu.make_async_remote_copy(
                src_ref=src, dst_ref=dst, send_sem=send_sems.at[6 * k + j], recv_sem=recv_sems.at[6 * k + j],
                device_id=to, device_id_type=MESH)

        local = [pltpu.make_async_copy(ins[k], outs[k].at[my_s], loc_sems.at[k]) for k in range(n)]
        for cp in local:
            cp.start()
        started = []
        for k in range(n):
            for j, chip in enumerate(chips):
                cp = rcopy(k, j, ins[k].at[pl.ds(c * halves[k], halves[k])], half(k, my_s, c), (*chip, c))
                cp.start()
                started.append(cp)
        for j, chip in enumerate(chips):
            s_j = 2 * chip[0] + chip[1]
            for k in range(n):
                rcopy(k, j, half(k, s_j, c), half(k, s_j, c), me).wait_recv()
                cp = rcopy(k, 3 + j, half(k, s_j, c), half(k, s_j, c), sibling)
                cp.start()
                started.append(cp)
        for j, chip in enumerate(chips):
            s_j = 2 * chip[0] + chip[1]
            for k in range(n):
                rcopy(k, 3 + j, half(k, s_j, 1 - c), half(k, s_j, 1 - c), me).wait_recv()
        for cp in started:
            cp.wait_send()
        for cp in local:
            cp.wait()

    return pl.pallas_call(
        body, name="gather_weights",
        in_specs=[pl.BlockSpec(memory_space=pl.ANY)] * n,
        out_specs=[pl.BlockSpec(memory_space=pl.ANY)] * n,
        out_shape=[jax.ShapeDtypeStruct((N_CHIPS,) + a.shape, a.dtype) for a in shards],
        scratch_shapes=[pltpu.SemaphoreType.DMA((6 * n,)), pltpu.SemaphoreType.DMA((6 * n,)),
                        pltpu.SemaphoreType.DMA((n,))],
    )(*shards)


def _rs_pair_share(fulls):
    n = len(fulls)
    halves = [f.shape[0] // 2 for f in fulls]

    def body(*refs):
        outs = refs[n:2 * n]
        send_sems, recv_sems = refs[2 * n:]
        x, y, c = _mesh_pos()

        def copy(k, cc, to):
            rows = outs[k].at[pl.ds(cc * halves[k], halves[k])]
            return pltpu.make_async_remote_copy(
                src_ref=rows, dst_ref=rows, send_sem=send_sems.at[k], recv_sem=recv_sems.at[k],
                device_id=to, device_id_type=MESH)

        cps = [copy(k, c, (x, y, 1 - c)) for k in range(n)]
        for cp in cps:
            cp.start()
        for k in range(n):
            copy(k, 1 - c, (x, y, c)).wait_recv()
        for cp in cps:
            cp.wait_send()

    return pl.pallas_call(
        body, name="rs_pair_share",
        in_specs=[pl.BlockSpec(memory_space=pl.ANY)] * n,
        out_specs=[pl.BlockSpec(memory_space=pl.ANY)] * n,
        out_shape=[jax.ShapeDtypeStruct(f.shape, f.dtype) for f in fulls],
        input_output_aliases={k: k for k in range(n)},
        scratch_shapes=[pltpu.SemaphoreType.DMA((n,)), pltpu.SemaphoreType.DMA((n,))],
    )(*fulls)


def _pair_sum(g, recv, pos, name):
    _, rows, cols = g.shape
    h = rows // 2
    tr = _row_tile(h, 256)
    nb = h // tr

    def body(pos_ref, g_ref, r_ref, o_ref, own_ref):
        total = g_ref[0] + r_ref[0]
        o_ref[0] = total.astype(BF16)

        @pl.when(pl.program_id(1) == pos_ref[1])
        def _():
            own_ref[...] = total

    return pl.pallas_call(
        body, name=name,
        grid_spec=pltpu.PrefetchScalarGridSpec(
            num_scalar_prefetch=1, grid=(nb, N_CHIPS),
            in_specs=[pl.BlockSpec((1, tr, cols), lambda r, s, pos_ref: (s, pos_ref[0] * nb + r, 0)),
                      pl.BlockSpec((1, tr, cols), lambda r, s, pos_ref: (s, r, 0))],
            out_specs=[pl.BlockSpec((1, tr, cols), lambda r, s, pos_ref: (s, r, 0)),
                       pl.BlockSpec((tr, cols), lambda r, s, pos_ref: (r, 0))]),
        out_shape=[jax.ShapeDtypeStruct((N_CHIPS, h, cols), BF16), jax.ShapeDtypeStruct((h, cols), F32)],
        compiler_params=_params(2, 32))(pos, g, recv)


def _chip_sum(own, rb, pos, name):
    h, cols = own.shape
    tr = _row_tile(h, 256)
    nb = h // tr

    def body(pos_ref, c_ref, r_ref, o_ref):
        del pos_ref
        o_ref[...] = ((c_ref[...] + r_ref[0].astype(F32)) + r_ref[1].astype(F32)) + r_ref[2].astype(F32)

    return pl.pallas_call(
        body, name=name,
        grid_spec=pltpu.PrefetchScalarGridSpec(
            num_scalar_prefetch=1, grid=(nb,),
            in_specs=[pl.BlockSpec((tr, cols), lambda r, pos_ref: (r, 0)),
                      pl.BlockSpec((3, tr, cols), lambda r, pos_ref: (0, r, 0))],
            out_specs=pl.BlockSpec((tr, cols), lambda r, pos_ref: (pos_ref[0] * nb + r, 0))),
        out_shape=jax.ShapeDtypeStruct((2 * h, cols), F32),
        compiler_params=_params(1, 32))(pos, own, rb)


def _adamw_big(w, g, m, v, name):
    rows, cols = w.shape
    tr = _row_tile(rows, 256)

    def body(w_ref, g_ref, m_ref, v_ref, d_ref, nm_ref, nv_ref):
        d_ref[...], nm_ref[...], nv_ref[...] = _adamw(w_ref[...], g_ref[...], m_ref[...], v_ref[...])

    spec = pl.BlockSpec((tr, cols), lambda i: (i, 0))
    return pl.pallas_call(
        body, name=name, grid=(rows // tr,),
        in_specs=[spec] * 4, out_specs=[spec] * 3,
        out_shape=[jax.ShapeDtypeStruct((rows, cols), F32)] * 3,
        compiler_params=_params(1, 48))(w, g, m, v)


def _gather_small(loss, dfg, dlg, dgn, ddwb, dclg, dclb, dpwb, ddww, dmeta):
    d = loss.shape[1]
    w = dgn.shape[1]
    wc, mc = ddww.shape[1] // N_CHIPS, dmeta.shape[1] // N_CHIPS

    def body(loss_ref, dfg_ref, dlg_ref, dgn_ref, ddwb_ref, dclg_ref, dclb_ref, dpwb_ref, ddww_ref, dmeta_ref,
             gs_ref, gd_ref, gm_ref, send_sems, recv_sems, loc_sems):
        x, y, c = _mesh_pos()
        me = 4 * x + 2 * y + c
        gs_ref[me, 0:1, :] = loss_ref[...]
        gs_ref[me, 1:2, :] = dfg_ref[...]
        gs_ref[me, 2:3, :] = dlg_ref[...]
        gs_ref[me, 3:4, 0:w] = dgn_ref[...]
        gs_ref[me, 3:4, w:2 * w] = ddwb_ref[...]
        gs_ref[me, 4:5, 0:w] = dclg_ref[...]
        gs_ref[me, 4:5, w:2 * w] = dclb_ref[...]
        gs_ref[me, 5:6, 0:w] = dpwb_ref[...]
        gs_ref[me, 5:6, w:2 * w] = jnp.zeros((1, d - w), F32)
        gs_ref[me, 6:8, :] = jnp.zeros((2, d), F32)
        bufs = (gs_ref, gd_ref, gm_ref)

        def mine_for(k, shard):
            if k == 0:
                return gs_ref.at[me]
            ref, width = ((ddww_ref, wc), (dmeta_ref, mc))[k - 1]
            return ref.at[:, pl.ds(pl.multiple_of(shard * width, width), width)]

        def peer(j):
            return (1 - x if j & 4 else x), (1 - y if j & 2 else y), (1 - c if j & 1 else c)

        def copy(k, j, src, slot, to):
            return pltpu.make_async_remote_copy(
                src_ref=src, dst_ref=bufs[k].at[slot],
                send_sem=send_sems.at[7 * k + j - 1], recv_sem=recv_sems.at[7 * k + j - 1],
                device_id=to, device_id_type=MESH)

        own = [pltpu.make_async_copy(mine_for(k, 2 * x + y), bufs[k].at[me], loc_sems.at[k - 1]) for k in (1, 2)]
        for cp in own:
            cp.start()
        cps = []
        for k in range(3):
            for j in range(1, N_DEV):
                px, py, pc = peer(j)
                cp = copy(k, j, mine_for(k, 2 * px + py), me, (px, py, pc))
                cp.start()
                cps.append(cp)
        for k in range(3):
            for j in range(1, N_DEV):
                px, py, pc = peer(j)
                slot = 4 * px + 2 * py + pc
                copy(k, j, bufs[k].at[slot], slot, (x, y, c)).wait_recv()
        for cp in cps:
            cp.wait_send()
        for cp in own:
            cp.wait()

    vm = pl.BlockSpec(memory_space=pltpu.VMEM)
    return pl.pallas_call(
        body, name="gather_small",
        in_specs=[vm] * 10, out_specs=[vm] * 3,
        out_shape=[jax.ShapeDtypeStruct((N_DEV, 8, d), F32),
                   jax.ShapeDtypeStruct((N_DEV, ddww.shape[0], wc), F32),
                   jax.ShapeDtypeStruct((N_DEV, dmeta.shape[0], mc), F32)],
        scratch_shapes=[pltpu.SemaphoreType.DMA((21,)), pltpu.SemaphoreType.DMA((21,)),
                        pltpu.SemaphoreType.DMA((2,))],
    )(loss, dfg, dlg, dgn, ddwb, dclg, dclb, dpwb, ddww, dmeta)


def _small_update(gs, gd, gm, weights, ms, vs):
    d = gs.shape[2]
    w = d // 2
    n = len(weights)

    def body(gs_ref, gd_ref, gm_ref, *refs):
        w_refs, m_refs, v_refs = refs[:n], refs[n:2 * n], refs[2 * n:3 * n]
        loss_ref = refs[3 * n]
        g_refs = refs[3 * n + 1:4 * n + 1]
        d_refs = refs[4 * n + 1:5 * n + 1]
        nm_refs = refs[5 * n + 1:6 * n + 1]
        nv_refs = refs[6 * n + 1:7 * n + 1]

        def total(ref):
            t = ref[0]
            for dev in range(1, N_DEV):
                t = t + ref[dev]
            return t

        packed = total(gs_ref)
        loss_ref[...] = jnp.sum(packed[0:1, :], axis=1, keepdims=True) * (0.5 / d)
        grads = [packed[2:3, :], packed[1:2, :], packed[3:4, 0:w], packed[3:4, w:2 * w], packed[4:5, 0:w],
                 packed[4:5, w:2 * w], packed[5:6, 0:w], total(gd_ref), total(gm_ref)]
        for k in range(n):
            g = grads[k]
            g_refs[k][...] = g
            d_refs[k][...], nm_refs[k][...], nv_refs[k][...] = _adamw(w_refs[k][...], g, m_refs[k][...], v_refs[k][...])

    def whole(shape):
        return pl.BlockSpec(shape, lambda i: (0,) * len(shape))

    shapes = [a.shape for a in weights]
    in_specs = [whole(gs.shape), whole(gd.shape), whole(gm.shape)] + [whole(s) for s in shapes] * 3
    out_specs = [whole((1, 1))] + [whole(s) for s in shapes] * 4
    out_shape = [jax.ShapeDtypeStruct((1, 1), F32)] + [jax.ShapeDtypeStruct(s, F32) for s in shapes] * 4
    outs = pl.pallas_call(
        body, name="small_update", grid=(1,), in_specs=in_specs, out_specs=out_specs, out_shape=out_shape,
        compiler_params=_params(1, 32))(gs, gd, gm, *weights, *ms, *vs)
    loss = outs[0]
    return loss, outs[1:n + 1], outs[n + 1:2 * n + 1], outs[2 * n + 1:3 * n + 1], outs[3 * n + 1:4 * n + 1]


def kernel(x, meta_tokens, ln_g, w_in, ret_gn_g, conv_dw_w, conv_dw_b, conv_ln_g, conv_ln_b, conv_pw_w, conv_pw_b, w_out, final_g, loss_target, m_meta_tokens, m_ln_g, m_w_in, m_ret_gn_g, m_conv_dw_w, m_conv_dw_b, m_conv_ln_g, m_conv_ln_b, m_conv_pw_w, m_conv_pw_b, m_w_out, m_final_g, v_meta_tokens, v_ln_g, v_w_in, v_ret_gn_g, v_conv_dw_w, v_conv_dw_b, v_conv_ln_g, v_conv_ln_b, v_conv_pw_w, v_conv_pw_b, v_w_out, v_final_g):
    seq, d = x.shape[1], x.shape[2]
    w = ret_gn_g.shape[1]
    hd = w // RET_HEADS
    lp = CHUNK + seq
    ns = w_in.shape[2]
    mx, my, mc = lax.axis_index("x"), lax.axis_index("y"), lax.axis_index("c")
    my_s = 2 * mx + my
    s_arr = my_s.astype(jnp.int32).reshape(1)
    pos = jnp.stack([mc, my_s]).astype(jnp.int32)
    order = jnp.stack([my_s, 2 * (1 - mx) + my, 2 * mx + (1 - my), 2 * (1 - mx) + (1 - my)]).astype(jnp.int32)

    dw_pad = jnp.pad(conv_dw_w[0], ((0, HALO - CONV_K), (0, 0)))
    dw4, meta4 = _gather_shards([dw_pad, meta_tokens])
    dw_full = dw4.transpose(1, 0, 2).reshape(HALO, w)
    meta_full = meta4.transpose(1, 0, 2).reshape(N_META, d)

    lead = jnp.concatenate([jnp.zeros((LEAD, d), F32), meta_full], axis=0)
    zero_lead = jnp.zeros((CHUNK, d), F32)
    consts = _ret_consts()
    cos, sin = _rope_tables(lp, hd // 2)
    fg2 = final_g.reshape(1, d)

    proj, r1, hn, w4, pw4, wo4 = _in_proj_gather(
        order, x[0], lead, ln_g, _cast_into_gathered(w_in[0], s_arr, "cast_w_in"),
        _cast_into_gathered(conv_pw_w[0], s_arr, "cast_pw_w"), _cast_into_gathered(w_out[0], s_arr, "cast_w_out"))
    pw_full = pw4.reshape(w, w)
    wo_full = wo4.reshape(2 * w, d)
    y_ret, ssave, ret_ops, ret_scores, ret_out = _ret_fwd(proj, cos, sin, ret_gn_g, consts)
    y_conv, u1 = _conv_fwd(proj, dw_full, conv_dw_b, conv_ln_g, conv_ln_b, pw_full, conv_pw_b)
    dh2, dy, loss_l, dfg = _out_proj_loss(y_ret, y_conv, x[0], lead, loss_target[0], zero_lead, wo_full, fg2)

    g_wo = _dw_out(y_ret, y_conv, dh2)
    dproj, dgn = _ret_bwd(proj, dy, ssave, ret_ops, ret_scores, ret_out, cos, sin, ret_gn_g, consts)
    dproj, du1, g_pw, dpwb, dclg, dclb = _conv_bwd_pw(dy, proj, u1, conv_ln_g, conv_ln_b, pw_full, conv_pw_b, dproj)
    dproj, ddww, ddwb = _conv_bwd_dw(du1, proj, dw_full, dproj)
    g_wo4 = g_wo.reshape(N_CHIPS, (2 * w) // N_CHIPS, d)
    g_pw4 = g_pw.reshape(N_CHIPS, w // N_CHIPS, w)
    g_win, recv = _dw_in(hn, dproj, ns, [g_wo4, g_pw4])
    gs = [g_win, g_wo4, g_pw4]
    names = ("w_in", "w_out", "pw_w")
    sums = [_pair_sum(g, r, pos, "pair_sum_" + nm) for g, r, nm in zip(gs, recv, names)]
    dh, dlg, rb = _in_proj_bwd(dproj, w4, x[0], lead, r1, dh2, ln_g, [cs_ for cs_, _ in sums])
    grad_x = dh[CHUNK:][None]
    dmeta = dh[LEAD:CHUNK]
    fulls = [_chip_sum(own, r, pos, "chip_sum_" + nm) for (_, own), r, nm in zip(sums, rb, names)]
    grad_w_in, grad_w_out, grad_pw = _rs_pair_share(fulls)
    d_win, nm_win, nv_win = _adamw_big(w_in[0], grad_w_in, m_w_in[0], v_w_in[0], "adamw_w_in")
    d_wo, nm_wo, nv_wo = _adamw_big(w_out[0], grad_w_out, m_w_out[0], v_w_out[0], "adamw_w_out")
    d_pw, nm_pw, nv_pw = _adamw_big(conv_pw_w[0], grad_pw, m_conv_pw_w[0], v_conv_pw_w[0], "adamw_pw_w")

    gsm, gdm, gmm = _gather_small(loss_l, dfg, dlg, dgn, ddwb, dclg, dclb, dpwb, ddww, dmeta)

    def pad_dw(a):
        return jnp.pad(a[0], ((0, HALO - CONV_K), (0, 0)))

    small_w = [ln_g, fg2, ret_gn_g, conv_dw_b, conv_ln_g, conv_ln_b, conv_pw_b, dw_pad, meta_tokens]
    small_m = [m_ln_g, m_final_g.reshape(1, d), m_ret_gn_g, m_conv_dw_b, m_conv_ln_g, m_conv_ln_b, m_conv_pw_b,
               pad_dw(m_conv_dw_w), m_meta_tokens]
    small_v = [v_ln_g, v_final_g.reshape(1, d), v_ret_gn_g, v_conv_dw_b, v_conv_ln_g, v_conv_ln_b, v_conv_pw_b,
               pad_dw(v_conv_dw_w), v_meta_tokens]
    loss, sg, sd, snm, snv = _small_update(gsm, gdm, gmm, small_w, small_m, small_v)

    def assemble(small, big_in, big_pw, big_out):
        ln, fg, gn, dwb, clg, clb, pwb, dww, meta = small
        return (meta, ln, big_in[None], gn, dww[:CONV_K][None], dwb, clg, clb, big_pw[None], pwb, big_out[None],
                fg.reshape(d))

    return (loss.reshape(()), grad_x,
            *assemble(sg, grad_w_in, grad_pw, grad_w_out),
            *assemble(sd, d_win, d_pw, d_wo),
            *assemble(snm, nm_win, nm_pw, nm_wo),
            *assemble(snv, nv_win, nv_pw, nv_wo))
```

```python
import functools

import jax
import jax.numpy as jnp
from jax import lax
from jax.experimental import pallas as pl
from jax.experimental.pallas import tpu as pltpu

F32 = jnp.float32
BF16 = jnp.bfloat16
MESH = pl.DeviceIdType.MESH

N_META = 16
CHUNK = 128
LEAD = (-N_META) % CHUNK
RET_HEADS = 4
CONV_K = 31
HALO = 32
ROPE_BASE = 10000.0
EPS = 1e-6
N_CHIPS = 4
N_DEV = 8

ADAM_LR = 0.001
ADAM_B1 = 0.9
ADAM_B2 = 0.999
ADAM_EPS = 1e-08
ADAM_WD = 0.01
ADAM_STEP = 10

MIB = 2 ** 20


def _params(n_grid_axes, vmem_mib):
    return pltpu.CompilerParams(dimension_semantics=("arbitrary",) * n_grid_axes,
                                vmem_limit_bytes=vmem_mib * MIB)


def _row_tile(n, pref):
    for t in (1664, 1280, 1024, 640, 512, 384, 320, 256, 128, 64, 32, 16, 8):
        if t <= pref and n % t == 0:
            return t
    raise ValueError(f"no row tile for {n}")


def _dot(a, b):
    return jnp.dot(a, b, preferred_element_type=F32)


def _dot_nt(a, b):
    return lax.dot_general(a, b, (((1,), (1,)), ((), ())), preferred_element_type=F32)


def _dot_tn(a, b):
    return lax.dot_general(a, b, (((0,), (0,)), ((), ())), preferred_element_type=F32)


def _sigmoid(x):
    return jax.nn.sigmoid(x)


def _dsilu(x, s):
    return s * (1.0 + x * (1.0 - s))


def _mean(x):
    return jnp.mean(x, axis=-1, keepdims=True)


def _colsum(x):
    return jnp.sum(x, axis=0, keepdims=True)


def _rope(x, cos, sin):
    half = x.shape[-1] // 2
    x1, x2 = x[:, :half], x[:, half:]
    return jnp.concatenate([x1 * cos - x2 * sin, x1 * sin + x2 * cos], axis=-1)


def _rope_t(d, cos, sin):
    half = d.shape[-1] // 2
    d1, d2 = d[:, :half], d[:, half:]
    return jnp.concatenate([d1 * cos + d2 * sin, d2 * cos - d1 * sin], axis=-1)


def _adamw(w, g, m, v):
    m = ADAM_B1 * m + (1.0 - ADAM_B1) * g
    v = ADAM_B2 * v + (1.0 - ADAM_B2) * (g * g)
    m_hat = m / (1.0 - ADAM_B1 ** ADAM_STEP)
    v_hat = v / (1.0 - ADAM_B2 ** ADAM_STEP)
    delta = -ADAM_LR * (m_hat / (jnp.sqrt(v_hat) + ADAM_EPS) + ADAM_WD * w)
    return delta, m, v


def _ret_consts():
    h = jnp.arange(RET_HEADS, dtype=F32)
    log_g = jnp.log(1.0 - jnp.exp2(-5.0 - h))
    idx = jnp.arange(CHUNK, dtype=F32)
    rel = idx[:, None] - idx[None, :]
    dmask = jnp.where(rel[None] >= 0, jnp.exp(jnp.maximum(rel, 0.0)[None] * log_g[:, None, None]), 0.0)
    qd = jnp.exp((idx[None, :] + 1.0) * log_g[:, None])[:, :, None]
    kd = jnp.exp((CHUNK - 1.0 - idx[None, :]) * log_g[:, None])[:, :, None]
    cd = jnp.exp(CHUNK * log_g)[:, None, None]
    return dmask, qd, kd, cd


def _rope_tables(n_rows, half):
    pos = jnp.arange(n_rows, dtype=F32) - float(LEAD)
    inv_freq = ROPE_BASE ** (-jnp.arange(half, dtype=F32) / half)
    ang = pos[:, None] * inv_freq[None, :]
    return jnp.cos(ang), jnp.sin(ang)


def _padded_tile_stream(x_hbm, lead_hbm, buf, sems, tm):
    def start_first(slot):
        pltpu.make_async_copy(lead_hbm, buf.at[slot, pl.ds(0, CHUNK)], sems.at[slot]).start()
        pltpu.make_async_copy(x_hbm.at[pl.ds(0, tm - CHUNK)], buf.at[slot, pl.ds(CHUNK, tm - CHUNK)],
                              sems.at[slot]).start()

    def start(slot, tile):
        pltpu.make_async_copy(x_hbm.at[pl.ds(tile * tm - CHUNK, tm)], buf.at[slot], sems.at[slot]).start()

    def wait(slot):
        pltpu.make_async_copy(x_hbm.at[pl.ds(0, tm)], buf.at[slot], sems.at[slot]).wait()

    return start_first, start, wait


def _cast_into_gathered(a, s_arr, name):
    rows, cols = a.shape
    tr = _row_tile(rows, 256)

    def body(s_ref, a_ref, o_ref):
        del s_ref
        o_ref[0] = a_ref[...].astype(BF16)

    return pl.pallas_call(
        body, name=name,
        grid_spec=pltpu.PrefetchScalarGridSpec(
            num_scalar_prefetch=1, grid=(rows // tr,),
            in_specs=[pl.BlockSpec((tr, cols), lambda i, s_ref: (i, 0))],
            out_specs=pl.BlockSpec((1, tr, cols), lambda i, s_ref: (s_ref[0], i, 0))),
        out_shape=jax.ShapeDtypeStruct((N_CHIPS, rows, cols), BF16),
        compiler_params=_params(1, 32))(s_arr, a)


def _in_proj_gather(order, x, lead, ln_g, w4, pw4, wo4):
    seq, d = x.shape
    lp = CHUNK + seq
    ns = w4.shape[2]
    tm = _row_tile(lp, 640)
    nt = lp // tm
    assert nt >= 2, "the hn write-back of a row tile is waited for one step later, before any pass re-reads it"
    land_step, load_step = max(nt - 3, 0), max(nt - 2, 0)
    gathered = (w4, pw4, wo4)
    halves = [a.shape[1] // 2 for a in gathered]
    n = len(gathered)

    def body(order_ref, x_hbm, lead_hbm, g_ref, w_in, pw_in, wo_in, proj_ref, r_ref, hn_hbm, w_out, pw_out,
             wo_out, wbuf, hnbuf, hbuf, send_sems, recv_sems, hn_out_sems, hn_in_sems, h_sems, w_sem):
        del order_ref, w_in, pw_in, wo_in
        t, i = pl.program_id(0), pl.program_id(1)
        slot = (t * nt + i) % 2
        h_first, h_start, h_wait = _padded_tile_stream(x_hbm, lead_hbm, hbuf, h_sems, tm)
        x, y, c = _mesh_pos()
        me, sibling = (x, y, c), (x, y, 1 - c)
        my_s = 2 * x + y
        chips = _other_chips(x, y)
        outs = (w_out, pw_out, wo_out)

        def half(k, s, cc):
            return outs[k].at[s, pl.ds(cc * halves[k], halves[k])]

        def rcopy(k, j, rows, to):
            return pltpu.make_async_remote_copy(
                src_ref=rows, dst_ref=rows, send_sem=send_sems.at[6 * k + j], recv_sem=recv_sems.at[6 * k + j],
                device_id=to, device_id_type=MESH)

        def send(k, j):
            return rcopy(k, j, half(k, my_s, c), (*chips[j], c))

        def shard_of(j):
            return 2 * chips[j][0] + chips[j][1]

        def forward(k, j):
            return rcopy(k, 3 + j, half(k, shard_of(j), c), sibling)

        def land(k, j):
            rcopy(k, j, half(k, shard_of(j), c), me).wait_recv()
            forward(k, j).start()

        def landed_from_sibling(k, j):
            rcopy(k, 3 + j, half(k, shard_of(j), 1 - c), me).wait_recv()

        def load_w(s, wslot):
            return pltpu.make_async_copy(w_out.at[s], wbuf.at[wslot], w_sem)

        def hn_out(sl, row_tile):
            return pltpu.make_async_copy(hnbuf.at[sl], hn_hbm.at[pl.ds(row_tile * tm, tm)], hn_out_sems.at[sl])

        def hn_in(sl, row_tile):
            return pltpu.make_async_copy(hn_hbm.at[pl.ds(row_tile * tm, tm)], hnbuf.at[sl], hn_in_sems.at[sl])

        @pl.when((t == 0) & (i == 0))
        def _():
            for j in range(2):
                send(0, j).start()
            load_w(my_s, 0).start()
            load_w(my_s, 0).wait()

        @pl.when((t == 1) & (i == 0))
        def _():
            send(0, 2).start()
            for k in range(1, n):
                for jj in range(3):
                    send(k, jj).start()

        for j in range(3):
            @pl.when((t == j) & (i == land_step))
            def _(j=j):
                land(0, j)

            @pl.when((t == j) & (i == load_step))
            def _(j=j):
                landed_from_sibling(0, j)
                load_w(shard_of(j), (j + 1) % 2).start()

            @pl.when((t == j + 1) & (i == 0))
            def _(j=j):
                load_w(shard_of(j), (j + 1) % 2).wait()

        @pl.when((t == 0) & (i == 0))
        def _():
            h_first(0)

        @pl.when((t == 0) & (i + 1 < nt))
        def _():
            h_start((i + 1) % 2, i + 1)

        @pl.when(t == 0)
        def _():
            h_wait(i % 2)
            h = hbuf[i % 2]
            r = lax.rsqrt(_mean(h * h) + EPS)
            hnbuf[slot] = ((h * r) * g_ref[...]).astype(BF16)
            r_ref[...] = r
            hn_out(slot, i).start()

        @pl.when(t > 0)
        def _():
            hn_in(slot, i).wait()

        @pl.when(((t == 0) & (i > 0)) | ((t == 1) & (i == 0)))
        def _():
            hn_out(1 - slot, jnp.where(i > 0, i - 1, nt - 1)).wait()

        last = (t == N_CHIPS - 1) & (i == nt - 1)

        @pl.when(((t > 0) | (i == nt - 1)) & jnp.logical_not(last))
        def _():
            hn_in(1 - slot, jnp.where(i == nt - 1, 0, i + 1)).start()

        proj_ref[...] = _dot(hnbuf[slot], wbuf[t % 2])

        @pl.when(last)
        def _():
            for k in range(1, n):
                for j in range(3):
                    land(k, j)
            for k in range(1, n):
                for j in range(3):
                    landed_from_sibling(k, j)
            for k in range(n):
                for j in range(3):
                    send(k, j).wait_send()
                    forward(k, j).wait_send()

    def frozen(t, i):
        return jnp.where(t == 0, i, nt - 1)

    any_spec = pl.BlockSpec(memory_space=pl.ANY)
    return pl.pallas_call(
        body, name="in_proj_gather",
        grid_spec=pltpu.PrefetchScalarGridSpec(
            num_scalar_prefetch=1, grid=(N_CHIPS, nt),
            in_specs=[any_spec, any_spec, pl.BlockSpec((1, d), lambda t, i, o: (0, 0)),
                      any_spec, any_spec, any_spec],
            out_specs=[pl.BlockSpec((tm, ns), lambda t, i, o: (i, o[t])),
                       pl.BlockSpec((tm, 1), lambda t, i, o: (frozen(t, i), 0)),
                       any_spec, any_spec, any_spec, any_spec],
            scratch_shapes=[pltpu.VMEM((2, d, ns), BF16), pltpu.VMEM((2, tm, d), BF16), pltpu.VMEM((2, tm, d), F32),
                            pltpu.SemaphoreType.DMA((6 * n,)), pltpu.SemaphoreType.DMA((6 * n,)),
                            pltpu.SemaphoreType.DMA((2,)), pltpu.SemaphoreType.DMA((2,)),
                            pltpu.SemaphoreType.DMA((2,)), pltpu.SemaphoreType.DMA]),
        out_shape=[jax.ShapeDtypeStruct((lp, N_CHIPS * ns), F32),
                   jax.ShapeDtypeStruct((lp, 1), F32),
                   jax.ShapeDtypeStruct((lp, d), BF16)]
                  + [jax.ShapeDtypeStruct(a.shape, a.dtype) for a in gathered],
        input_output_aliases={4: 3, 5: 4, 6: 5},
        compiler_params=_params(2, 48))(order, x, lead, ln_g, w4, pw4, wo4)


def _chunks_per_step(nch):
    return next(n for n in (5, 4, 3, 2, 1) if nch % n == 0)


def _ret_fwd(proj, cos, sin, gn_g, consts):
    lp = proj.shape[0]
    w = gn_g.shape[1]
    hd = w // RET_HEADS
    nch = lp // CHUNK
    cps = _chunks_per_step(nch)
    tr = cps * CHUNK
    dmask, qd, kd, cd = consts

    def body(q_ref, k_ref, v_ref, g_ref, cos_ref, sin_ref, gn_ref, m_ref, qd_ref, kd_ref, cd_ref,
             y_ref, ssave_ref, ops_ref, sc_ref, out_ref, s_scr):
        @pl.when(pl.program_id(0) == 0)
        def _():
            s_scr[...] = jnp.zeros_like(s_scr)

        def chunk(cc, carry):
            rows = pl.ds(pl.multiple_of(cc * CHUNK, CHUNK), CHUNK)
            cos_t, sin_t = cos_ref[rows, :], sin_ref[rows, :]
            for h in range(RET_HEADS):
                sl = slice(h * hd, (h + 1) * hd)
                qr = _rope(q_ref[rows, sl], cos_t, sin_t)
                kr = _rope(k_ref[rows, sl], cos_t, sin_t) * (hd ** -0.5)
                qb, kb = qr.astype(BF16), kr.astype(BF16)
                qdq, kdk = (qr * qd_ref[h]).astype(BF16), (kr * kd_ref[h]).astype(BF16)
                vb = v_ref[rows, sl].astype(BF16)
                for k, operand in enumerate((qb, kb, qdq, kdk, vb)):
                    ops_ref[k, rows, sl] = operand
                scb = (_dot_nt(qb, kb) * m_ref[h]).astype(BF16)
                sc_ref[cc, h] = scb
                state = s_scr[h]
                sb = state.astype(BF16)
                ssave_ref[cc, h] = sb
                out = _dot(scb, vb) + _dot(qdq, sb)
                out_ref[rows, sl] = out
                s_scr[h] = state * cd_ref[h] + _dot_tn(kdk, vb)
                dev = out - _mean(out)
                yn = dev * lax.rsqrt(_mean(dev * dev) + EPS)
                g = g_ref[rows, sl]
                y_ref[rows, sl] = ((yn * gn_ref[:, sl]) * (g * _sigmoid(g))).astype(BF16)
            return carry

        lax.fori_loop(0, cps, chunk, 0)

    def col(j):
        return pl.BlockSpec((tr, w), lambda i: (i, j))

    def whole(a):
        return pl.BlockSpec(a.shape, lambda i: (0,) * a.ndim)

    return pl.pallas_call(
        body, name="ret_fwd", grid=(nch // cps,),
        in_specs=[col(0), col(1), col(2), col(3),
                  pl.BlockSpec((tr, hd // 2), lambda i: (i, 0)),
                  pl.BlockSpec((tr, hd // 2), lambda i: (i, 0)),
                  whole(gn_g), whole(dmask), whole(qd), whole(kd), whole(cd)],
        out_specs=[pl.BlockSpec((tr, w), lambda i: (i, 0)),
                   pl.BlockSpec((cps, RET_HEADS, hd, hd), lambda i: (i, 0, 0, 0)),
                   pl.BlockSpec((5, tr, w), lambda i: (0, i, 0)),
                   pl.BlockSpec((cps, RET_HEADS, CHUNK, CHUNK), lambda i: (i, 0, 0, 0)),
                   pl.BlockSpec((tr, w), lambda i: (i, 0))],
        out_shape=[jax.ShapeDtypeStruct((lp, w), BF16),
                   jax.ShapeDtypeStruct((nch, RET_HEADS, hd, hd), BF16),
                   jax.ShapeDtypeStruct((5, lp, w), BF16),
                   jax.ShapeDtypeStruct((nch, RET_HEADS, CHUNK, CHUNK), BF16),
                   jax.ShapeDtypeStruct((lp, w), F32)],
        scratch_shapes=[pltpu.VMEM((RET_HEADS, hd, hd), F32)],
        compiler_params=_params(1, 56))(proj, proj, proj, proj, cos, sin, gn_g, dmask, qd, kd, cd)


def _tap_groups(start, flip):
    groups = {}
    for j in range(CONV_K):
        o = start + (CONV_K - 1 - j if flip else j)
        groups.setdefault(o % 8, []).append((o // 8, j))
    return groups


def _shift_up(win, s):
    return win if s == 0 else pltpu.roll(win, win.shape[0] - s, axis=0)


def _dw_taps(src_ref, w_ref, dst_ref, bias, *, rows, start, flip, rb):
    cw = dst_ref.shape[1]
    lb = min(128, cw)
    groups = _tap_groups(start, flip)

    def rb_body(r, carry):
        base = pl.multiple_of(r * rb, rb)
        for cb in range(cw // lb):
            ls = slice(cb * lb, (cb + 1) * lb)
            win = src_ref[pl.ds(base, rb + HALO), ls]
            acc = jnp.zeros((rb, lb), F32) if bias is None else jnp.broadcast_to(bias[:, ls], (rb, lb))
            for s, taps in groups.items():
                ws = _shift_up(win, s)
                for a, j in taps:
                    acc = acc + ws[8 * a:8 * a + rb, :] * w_ref[j:j + 1, ls]
            dst_ref[pl.ds(base, rb), ls] = acc
        return carry

    lax.fori_loop(0, rows // rb, rb_body, 0)


def _conv_fwd(proj, dw_w, dw_b, cln_g, cln_b, pw_w, pw_b):
    lp = proj.shape[0]
    cw = dw_b.shape[1]
    tm = _row_tile(lp, 640)
    rb = _row_tile(tm, 64)

    def body(a_ref, b_ref, gc_ref, w_ref, wb_ref, lg_ref, lb_ref, pw_ref, pb_ref, y_ref, u1_ref, buf):
        @pl.when(pl.program_id(0) == 0)
        def _():
            buf[0:HALO, :] = jnp.zeros((HALO, cw), F32)

        buf[HALO:HALO + tm, :] = a_ref[...] * _sigmoid(b_ref[...])
        _dw_taps(buf, w_ref, u1_ref, wb_ref[...], rows=tm, start=HALO - (CONV_K - 1), flip=False, rb=rb)
        buf[0:HALO, :] = buf[tm:tm + HALO, :]
        u1 = u1_ref[...]
        dev = u1 - _mean(u1)
        z = dev * lax.rsqrt(_mean(dev * dev) + EPS) * lg_ref[...] + lb_ref[...]
        u3 = (z * _sigmoid(z)).astype(BF16)
        u4 = _dot(u3, pw_ref[...]) + pb_ref[...]
        gc = gc_ref[...]
        y_ref[...] = (u4 * (gc * _sigmoid(gc))).astype(BF16)

    def col(j):
        return pl.BlockSpec((tm, cw), lambda i: (i, j))

    def whole(a):
        return pl.BlockSpec(a.shape, lambda i: (0,) * a.ndim)

    return pl.pallas_call(
        body, name="conv_fwd", grid=(lp // tm,),
        in_specs=[col(4), col(5), col(6), whole(dw_w), whole(dw_b), whole(cln_g), whole(cln_b),
                  whole(pw_w), whole(pw_b)],
        out_specs=[pl.BlockSpec((tm, cw), lambda i: (i, 0)), pl.BlockSpec((tm, cw), lambda i: (i, 0))],
        out_shape=[jax.ShapeDtypeStruct((lp, cw), BF16), jax.ShapeDtypeStruct((lp, cw), F32)],
        scratch_shapes=[pltpu.VMEM((tm + HALO, cw), F32)],
        compiler_params=_params(1, 48))(proj, proj, proj, dw_w, dw_b, cln_g, cln_b, pw_w, pw_b)


def _out_proj_loss(yr, yc, x, lead, tgt, zero_lead, w_out, final_g):
    seq, d = x.shape
    lp = CHUNK + seq
    w = yr.shape[1]
    tm = _row_tile(lp, 320)
    nt = lp // tm
    assert tm > CHUNK

    def body(yr_ref, yc_ref, x_hbm, lead_hbm, t_hbm, zlead_hbm, w_ref, fg_ref, dh2_ref, dy_ref, loss_ref, dfg_ref,
             hbuf, tbuf, hsems, tsems):
        i = pl.program_id(0)
        slot = i % 2
        streams = (_padded_tile_stream(x_hbm, lead_hbm, hbuf, hsems, tm),
                   _padded_tile_stream(t_hbm, zlead_hbm, tbuf, tsems, tm))

        @pl.when(i == 0)
        def _():
            loss_ref[...] = jnp.zeros_like(loss_ref)
            dfg_ref[...] = jnp.zeros_like(dfg_ref)
            for start_first, _, _ in streams:
                start_first(0)

        @pl.when(i + 1 < nt)
        def _():
            for _, start, _ in streams:
                start(1 - slot, i + 1)

        for _, _, wait in streams:
            wait(slot)

        h2 = hbuf[slot] + (_dot(yr_ref[...], w_ref[0:w, :]) + _dot(yc_ref[...], w_ref[w:2 * w, :]))
        r2 = lax.rsqrt(_mean(h2 * h2) + EPS)
        h2n = h2 * r2
        fg = fg_ref[...]
        rows = i * tm + lax.broadcasted_iota(jnp.int32, (tm, 1), 0)
        err = jnp.where(rows >= CHUNK, h2n * fg - tbuf[slot], 0.0)
        loss_ref[...] += _colsum(err * err)
        dout = err * (1.0 / d)
        dfg_ref[...] += _colsum(dout * h2n)
        dz = dout * fg
        dh2 = r2 * (dz - h2n * _mean(dz * h2n))
        dh2_ref[...] = dh2
        db = dh2.astype(BF16)
        dy_ref[:, 0:w] = _dot_nt(db, w_ref[0:w, :])
        dy_ref[:, w:2 * w] = _dot_nt(db, w_ref[w:2 * w, :])

    def row(cols):
        return pl.BlockSpec((tm, cols), lambda i: (i, 0))

    any_spec = pl.BlockSpec(memory_space=pl.ANY)
    return pl.pallas_call(
        body, name="out_proj_loss", grid=(nt,),
        in_specs=[row(w), row(w), any_spec, any_spec, any_spec, any_spec,
                  pl.BlockSpec(memory_space=pltpu.VMEM),
                  pl.BlockSpec((1, d), lambda i: (0, 0))],
        out_specs=[row(d), row(2 * w), pl.BlockSpec((1, d), lambda i: (0, 0)),
                   pl.BlockSpec((1, d), lambda i: (0, 0))],
        out_shape=[jax.ShapeDtypeStruct((lp, d), F32), jax.ShapeDtypeStruct((lp, 2 * w), F32),
                   jax.ShapeDtypeStruct((1, d), F32), jax.ShapeDtypeStruct((1, d), F32)],
        scratch_shapes=[pltpu.VMEM((2, tm, d), F32), pltpu.VMEM((2, tm, d), F32),
                        pltpu.SemaphoreType.DMA((2,)), pltpu.SemaphoreType.DMA((2,))],
        compiler_params=_params(1, 56))(yr, yc, x, lead, tgt, zero_lead, w_out, final_g)


def _dw_out(yr, yc, dh2):
    lp, d = dh2.shape
    w = yr.shape[1]
    tm = _row_tile(lp, 1664)
    nb = 2
    dn = d // nb

    def body(yr_ref, yc_ref, d_ref, o_ref):
        @pl.when(pl.program_id(1) == 0)
        def _():
            o_ref[...] = jnp.zeros_like(o_ref)

        db = d_ref[...].astype(BF16)
        o_ref[0:w, :] += _dot_tn(yr_ref[...], db)
        o_ref[w:2 * w, :] += _dot_tn(yc_ref[...], db)

    return pl.pallas_call(
        body, name="dw_out", grid=(nb, lp // tm),
        in_specs=[pl.BlockSpec((tm, w), lambda n, i: (i, 0)),
                  pl.BlockSpec((tm, w), lambda n, i: (i, 0)),
                  pl.BlockSpec((tm, dn), lambda n, i: (i, n))],
        out_specs=pl.BlockSpec((2 * w, dn), lambda n, i: (0, n)),
        out_shape=jax.ShapeDtypeStruct((2 * w, d), F32),
        compiler_params=_params(2, 52))(yr, yc, dh2)


def _ret_bwd(proj, dy, ssave, ops, scores, out_pre, cos, sin, gn_g, consts):
    lp = proj.shape[0]
    w = gn_g.shape[1]
    hd = w // RET_HEADS
    nch = lp // CHUNK
    cps = _chunks_per_step(nch)
    tr = cps * CHUNK
    dmask, qd, kd, cd = consts

    def body(g_ref, dy_ref, ss_ref, ops_ref, sc_ref, out_ref, cos_ref, sin_ref, gn_ref, m_ref, qd_ref, kd_ref,
             cd_ref, dp_ref, dgn_ref, ds_scr):
        @pl.when(pl.program_id(0) == 0)
        def _():
            ds_scr[...] = jnp.zeros_like(ds_scr)
            dgn_ref[...] = jnp.zeros_like(dgn_ref)

        def chunk(n, carry):
            cc = cps - 1 - n
            rows = pl.ds(pl.multiple_of(cc * CHUNK, CHUNK), CHUNK)
            cos_t, sin_t = cos_ref[rows, :], sin_ref[rows, :]
            for h in range(RET_HEADS):
                sl = slice(h * hd, (h + 1) * hd)
                qb, kb, qdq, kdk, vb = (ops_ref[k, rows, sl] for k in range(5))
                sb = ss_ref[cc, h]
                scb = sc_ref[cc, h]
                mask = m_ref[h]
                qdec, kdec = qd_ref[h], kd_ref[h]
                out = out_ref[rows, sl]
                dev = out - _mean(out)
                rstd = lax.rsqrt(_mean(dev * dev) + EPS)
                yn = dev * rstd
                g = g_ref[rows, sl]
                sg = _sigmoid(g)
                gng = gn_ref[:, sl]
                dyv = dy_ref[rows, sl]
                dgr = dyv * (yn * gng) * _dsilu(g, sg)
                silu_g = g * sg
                dgn_ref[:, sl] += _colsum(dyv * yn * silu_g)
                dyn = dyv * gng * silu_g
                dout = rstd * (dyn - _mean(dyn) - yn * _mean(dyn * yn))
                dob = dout.astype(BF16)
                by_v_and_s = _dot_nt(dob, jnp.concatenate([vb, sb], axis=0))
                dscb = (by_v_and_s[:, 0:CHUNK] * mask).astype(BF16)
                dstate = ds_scr[h]
                dsb = dstate.astype(BF16)
                dq = _dot(dscb, kb) + by_v_and_s[:, CHUNK:] * qdec
                dk = _dot_tn(dscb, qb) + _dot_nt(vb, dsb) * kdec
                onto_do = _dot_tn(jnp.concatenate([scb, qdq], axis=1), dob)
                dv = onto_do[0:CHUNK] + _dot(kdk, dsb)
                ds_scr[h] = dstate * cd_ref[h] + onto_do[CHUNK:]
                dp_ref[rows, 0 * w + h * hd:0 * w + (h + 1) * hd] = _rope_t(dq, cos_t, sin_t).astype(BF16)
                dp_ref[rows, 1 * w + h * hd:1 * w + (h + 1) * hd] = (
                    _rope_t(dk, cos_t, sin_t) * (hd ** -0.5)).astype(BF16)
                dp_ref[rows, 2 * w + h * hd:2 * w + (h + 1) * hd] = dv.astype(BF16)
                dp_ref[rows, 3 * w + h * hd:3 * w + (h + 1) * hd] = dgr.astype(BF16)
            return carry

        lax.fori_loop(0, cps, chunk, 0)

    def rev(i):
        return nch // cps - 1 - i

    def col(j):
        return pl.BlockSpec((tr, w), lambda i: (rev(i), j))

    def whole(a):
        return pl.BlockSpec(a.shape, lambda i: (0,) * a.ndim)

    return pl.pallas_call(
        body, name="ret_bwd", grid=(nch // cps,),
        in_specs=[col(3),
                  pl.BlockSpec((tr, w), lambda i: (rev(i), 0)),
                  pl.BlockSpec((cps, RET_HEADS, hd, hd), lambda i: (rev(i), 0, 0, 0)),
                  pl.BlockSpec((5, tr, w), lambda i: (0, rev(i), 0)),
                  pl.BlockSpec((cps, RET_HEADS, CHUNK, CHUNK), lambda i: (rev(i), 0, 0, 0)),
                  pl.BlockSpec((tr, w), lambda i: (rev(i), 0)),
                  pl.BlockSpec((tr, hd // 2), lambda i: (rev(i), 0)),
                  pl.BlockSpec((tr, hd // 2), lambda i: (rev(i), 0)),
                  whole(gn_g), whole(dmask), whole(qd), whole(kd), whole(cd)],
        out_specs=[pl.BlockSpec((tr, 4 * w), lambda i: (rev(i), 0)),
                   pl.BlockSpec((1, w), lambda i: (0, 0))],
        out_shape=[jax.ShapeDtypeStruct((lp, 7 * w), BF16), jax.ShapeDtypeStruct((1, w), F32)],
        scratch_shapes=[pltpu.VMEM((RET_HEADS, hd, hd), F32)],
        compiler_params=_params(1, 56))(proj, dy, ssave, ops, scores, out_pre, cos, sin, gn_g, dmask, qd, kd, cd)


def _conv_bwd_pw(dy, proj, u1, cln_g, cln_b, pw_w, pw_b, dproj):
    lp, cw = u1.shape
    tm = _row_tile(lp, 320)

    def body(dy_ref, gc_ref, u1_ref, lg_ref, lb_ref, pw_ref, pb_ref, dp_in, dp_ref, du1_ref, dpw_ref,
             dpb_ref, dlg_ref, dlb_ref):
        del dp_in

        @pl.when(pl.program_id(0) == 0)
        def _():
            dpw_ref[...] = jnp.zeros_like(dpw_ref)
            dpb_ref[...] = jnp.zeros_like(dpb_ref)
            dlg_ref[...] = jnp.zeros_like(dlg_ref)
            dlb_ref[...] = jnp.zeros_like(dlb_ref)

        u1 = u1_ref[...]
        dev = u1 - _mean(u1)
        rstd = lax.rsqrt(_mean(dev * dev) + EPS)
        u1n = dev * rstd
        lg = lg_ref[...]
        z = u1n * lg + lb_ref[...]
        sz = _sigmoid(z)
        u3b = (z * sz).astype(BF16)
        u4 = _dot(u3b, pw_ref[...]) + pb_ref[...]
        gc = gc_ref[...]
        sgc = _sigmoid(gc)
        dyc = dy_ref[...]
        du4 = dyc * (gc * sgc)
        dp_ref[...] = (dyc * u4 * _dsilu(gc, sgc)).astype(BF16)
        du4b = du4.astype(BF16)
        dpb_ref[...] += _colsum(du4)
        dpw_ref[...] += _dot_tn(u3b, du4b)
        dz = _dot_nt(du4b, pw_ref[...]) * _dsilu(z, sz)
        dlg_ref[...] += _colsum(dz * u1n)
        dlb_ref[...] += _colsum(dz)
        dn = dz * lg
        du1_ref[...] = rstd * (dn - _mean(dn) - u1n * _mean(dn * u1n))

    def row(j):
        return pl.BlockSpec((tm, cw), lambda i: (i, j))

    def whole(a):
        return pl.BlockSpec(a.shape, lambda i: (0,) * a.ndim)

    def acc(r):
        return pl.BlockSpec((r, cw), lambda i: (0, 0))

    return pl.pallas_call(
        body, name="conv_bwd_pw", grid=(lp // tm,),
        in_specs=[row(1), row(6), row(0), whole(cln_g), whole(cln_b), whole(pw_w), whole(pw_b),
                  pl.BlockSpec(memory_space=pl.ANY)],
        out_specs=[row(6), row(0), acc(cw), acc(1), acc(1), acc(1)],
        out_shape=[jax.ShapeDtypeStruct(dproj.shape, dproj.dtype), jax.ShapeDtypeStruct((lp, cw), F32),
                   jax.ShapeDtypeStruct((cw, cw), F32), jax.ShapeDtypeStruct((1, cw), F32),
                   jax.ShapeDtypeStruct((1, cw), F32), jax.ShapeDtypeStruct((1, cw), F32)],
        input_output_aliases={7: 0},
        compiler_params=_params(1, 48))(dy, proj, u1, cln_g, cln_b, pw_w, pw_b, dproj)


def _conv_bwd_dw(du1, proj, dw_w, dproj):
    lp, cw = du1.shape
    tm = _row_tile(lp, 640)
    rb = _row_tile(tm, 64)
    nt = lp // tm
    hb = tm // HALO

    def body(a_ref, b_ref, du_ref, nx_ref, w_ref, dp_in, dp_ref, dww_ref, dwb_ref, ubuf, dbuf, du0, acc):
        del dp_in
        i = pl.program_id(0)

        @pl.when(i == 0)
        def _():
            ubuf[0:HALO, :] = jnp.zeros((HALO, cw), F32)
            acc[...] = jnp.zeros_like(acc)
            dwb_ref[...] = jnp.zeros_like(dwb_ref)

        a = a_ref[...]
        sb = _sigmoid(b_ref[...])
        ubuf[HALO:HALO + tm, :] = a * sb
        du = du_ref[...]
        dbuf[0:tm, :] = du
        dbuf[tm:tm + HALO, :] = jnp.where(i == nt - 1, 0.0, nx_ref[...])
        dwb_ref[...] += _colsum(du)
        _dw_taps(dbuf, w_ref, du0, None, rows=tm, start=0, flip=True, rb=rb)
        d0 = du0[...]
        dp_ref[:, 0:cw] = (d0 * sb).astype(BF16)
        dp_ref[:, cw:2 * cw] = (d0 * a * sb * (1.0 - sb)).astype(BF16)

        lb = min(128, cw)
        groups = _tap_groups(HALO - (CONV_K - 1), False)

        def rb_body(r, carry):
            base = pl.multiple_of(r * rb, rb)
            for cb in range(cw // lb):
                ls = slice(cb * lb, (cb + 1) * lb)
                win = ubuf[pl.ds(base, rb + HALO), ls]
                dv = dbuf[pl.ds(base, rb), ls]
                for s, taps in groups.items():
                    ws = _shift_up(win, s)
                    for a, j in taps:
                        prod = dv * ws[8 * a:8 * a + rb, :]
                        acc[8 * j:8 * j + 8, ls] += jnp.sum(prod.reshape(rb // 8, 8, lb), axis=0)
            return carry

        lax.fori_loop(0, tm // rb, rb_body, 0)
        ubuf[0:HALO, :] = ubuf[tm:tm + HALO, :]

        @pl.when(i == nt - 1)
        def _():
            for j in range(CONV_K):
                dww_ref[j:j + 1, :] = _colsum(acc[8 * j:8 * j + 8, :])
            dww_ref[CONV_K:HALO, :] = jnp.zeros((HALO - CONV_K, cw), F32)

    def col(j):
        return pl.BlockSpec((tm, cw), lambda i: (i, j))

    return pl.pallas_call(
        body, name="conv_bwd_dw", grid=(nt,),
        in_specs=[col(4), col(5), col(0),
                  pl.BlockSpec((HALO, cw), lambda i: (jnp.minimum((i + 1) * hb, nt * hb - 1), 0)),
                  pl.BlockSpec(dw_w.shape, lambda i: (0, 0)),
                  pl.BlockSpec(memory_space=pl.ANY)],
        out_specs=[pl.BlockSpec((tm, 2 * cw), lambda i: (i, 2)),
                   pl.BlockSpec((HALO, cw), lambda i: (0, 0)),
                   pl.BlockSpec((1, cw), lambda i: (0, 0))],
        out_shape=[jax.ShapeDtypeStruct(dproj.shape, dproj.dtype), jax.ShapeDtypeStruct((HALO, cw), F32),
                   jax.ShapeDtypeStruct((1, cw), F32)],
        scratch_shapes=[pltpu.VMEM((tm + HALO, cw), F32), pltpu.VMEM((tm + HALO, cw), F32),
                        pltpu.VMEM((tm, cw), F32), pltpu.VMEM((8 * HALO, cw), F32)],
        input_output_aliases={5: 0},
        compiler_params=_params(1, 56))(proj, proj, du1, du1, dw_w, dproj)


def _dw_in(hn, dproj, ns, others):
    lp, d = hn.shape
    tm = _row_tile(lp, 1664)
    nt = lp // tm
    mb = 512 if d % 512 == 0 else d
    n = len(others)
    halves = [d // 2] + [g.shape[1] // 2 for g in others]

    def body(hn_ref, dp_ref, *refs):
        other_refs, o_hbm, recv_refs = refs[:n], refs[n], refs[n + 1:2 * n + 2]
        acc, sem, send_sems, recv_sems = refs[2 * n + 2:]
        s, i = pl.program_id(0), pl.program_id(1)
        x, y, c = _mesh_pos()

        def to_sibling(src, dst, k):
            return pltpu.make_async_remote_copy(
                src_ref=src, dst_ref=dst, send_sem=send_sems.at[k], recv_sem=recv_sems.at[k],
                device_id=(x, y, 1 - c), device_id_type=MESH)

        def shard_half(p):
            return to_sibling(o_hbm.at[p, pl.ds((1 - c) * halves[0], halves[0])], recv_refs[0].at[p], p)

        def other_halves(k):
            return to_sibling(other_refs[k].at[:, pl.ds((1 - c) * halves[1 + k], halves[1 + k])],
                              recv_refs[1 + k], N_CHIPS + k)

        @pl.when((s == 0) & (i == 0))
        def _():
            for k in range(n):
                other_halves(k).start()

        @pl.when(i == 0)
        def _():
            acc[...] = jnp.zeros_like(acc)

        for m in range(d // mb):
            rows = slice(m * mb, (m + 1) * mb)
            acc[rows, :] += _dot_tn(hn_ref[:, rows], dp_ref[...])

        @pl.when(i == nt - 1)
        def _():
            cp = pltpu.make_async_copy(acc, o_hbm.at[s], sem)
            cp.start()
            cp.wait()
            shard_half(s).start()

        @pl.when((s == N_CHIPS - 1) & (i == nt - 1))
        def _():
            for p in range(N_CHIPS):
                shard_half(p).wait()
            for k in range(n):
                other_halves(k).wait()

    any_spec = pl.BlockSpec(memory_space=pl.ANY)
    outs = pl.pallas_call(
        body, name="dw_in", grid=(N_CHIPS, nt),
        in_specs=[pl.BlockSpec((tm, d), lambda s, i: (i, 0)),
                  pl.BlockSpec((tm, ns), lambda s, i: (i, s))] + [any_spec] * n,
        out_specs=[any_spec] * (n + 2),
        out_shape=[jax.ShapeDtypeStruct((N_CHIPS, d, ns), F32), jax.ShapeDtypeStruct((N_CHIPS, d // 2, ns), F32)]
                  + [jax.ShapeDtypeStruct((N_CHIPS, g.shape[1] // 2) + g.shape[2:], g.dtype) for g in others],
        scratch_shapes=[pltpu.VMEM((d, ns), F32), pltpu.SemaphoreType.DMA,
                        pltpu.SemaphoreType.DMA((N_CHIPS + n,)), pltpu.SemaphoreType.DMA((N_CHIPS + n,))],
        compiler_params=_params(2, 56))(hn, dproj, *others)
    return outs[0], outs[1:]


def _in_proj_bwd(dproj, w4, x, lead, r1, dh2, ln_g, cs):
    seq, d = x.shape
    lp = CHUNK + seq
    ns = w4.shape[2]
    tm = _row_tile(lp, 320)
    nt = lp // tm
    n = len(cs)
    assert tm > CHUNK

    def body(dp_ref, w_ref, x_hbm, lead_hbm, r_ref, d2_ref, g_ref, *refs):
        cs_refs, (dh_ref, dlg_ref), rb_refs = refs[:n], refs[n:n + 2], refs[n + 2:2 * n + 2]
        hbuf, h_sems, send_sems, recv_sems = refs[2 * n + 2:]
        i = pl.program_id(0)
        slot = i % 2
        x, y, c = _mesh_pos()
        h_first, h_start, h_wait = _padded_tile_stream(x_hbm, lead_hbm, hbuf, h_sems, tm)

        @pl.when(i == 0)
        def _():
            h_first(0)

        @pl.when(i + 1 < nt)
        def _():
            h_start(1 - slot, i + 1)

        h_wait(slot)

        def exchange():
            return [pltpu.make_async_remote_copy(
                src_ref=cs_refs[k].at[2 * chip[0] + chip[1]], dst_ref=rb_refs[k].at[j],
                send_sem=send_sems.at[3 * k + j], recv_sem=recv_sems.at[3 * k + j],
                device_id=(*chip, c), device_id_type=MESH)
                for k in range(n) for j, chip in enumerate(_other_chips(x, y))]

        @pl.when(i == 0)
        def _():
            dlg_ref[...] = jnp.zeros_like(dlg_ref)
            for cp in exchange():
                cp.start()

        dhn = _dot_nt(dp_ref[:, 0:ns], w_ref[0])
        for s in range(1, N_CHIPS):
            dhn = dhn + _dot_nt(dp_ref[:, s * ns:(s + 1) * ns], w_ref[s])
        r = r_ref[...]
        hn0 = hbuf[slot] * r
        dlg_ref[...] += _colsum(dhn * hn0)
        t = dhn * g_ref[...]
        dh_ref[...] = d2_ref[...] + r * (t - hn0 * _mean(t * hn0))

        @pl.when(i == nt - 1)
        def _():
            for cp in exchange():
                cp.wait()

    def row(cols):
        return pl.BlockSpec((tm, cols), lambda i: (i, 0))

    any_spec = pl.BlockSpec(memory_space=pl.ANY)
    outs = pl.pallas_call(
        body, name="in_proj_bwd", grid=(nt,),
        in_specs=[row(N_CHIPS * ns), pl.BlockSpec(memory_space=pltpu.VMEM),
                  any_spec, any_spec, row(1), row(d), pl.BlockSpec((1, d), lambda i: (0, 0))] + [any_spec] * n,
        out_specs=[row(d), pl.BlockSpec((1, d), lambda i: (0, 0))] + [any_spec] * n,
        out_shape=[jax.ShapeDtypeStruct((lp, d), F32), jax.ShapeDtypeStruct((1, d), F32)]
                  + [jax.ShapeDtypeStruct((3,) + a.shape[1:], a.dtype) for a in cs],
        scratch_shapes=[pltpu.VMEM((2, tm, d), F32), pltpu.SemaphoreType.DMA((2,)),
                        pltpu.SemaphoreType.DMA((3 * n,)), pltpu.SemaphoreType.DMA((3 * n,))],
        compiler_params=_params(1, 58))(dproj, w4, x, lead, r1, dh2, ln_g, *cs)
    return outs[0], outs[1], outs[2:]


def _mesh_pos():
    return lax.axis_index("x"), lax.axis_index("y"), lax.axis_index("c")


def _other_chips(x, y):
    return [(1 - x, y), (x, 1 - y), (1 - x, 1 - y)]


def _gather_shards(shards):
    n = len(shards)
    halves = [a.shape[0] // 2 for a in shards]

    def body(*refs):
        ins, outs = refs[:n], refs[n:2 * n]
        send_sems, recv_sems, loc_sems = refs[2 * n:]
        x, y, c = _mesh_pos()
        me, sibling = (x, y, c), (x, y, 1 - c)
        my_s = 2 * x + y
        chips = _other_chips(x, y)

        def half(k, s, cc):
            return outs[k].at[s, pl.ds(cc * halves[k], halves[k])]

        def rcopy(k, j, src, dst, to):
            return pltpu.make_async_remote_copy(
                src_ref=src, dst_ref=dst, send_sem=send_sems.at[6 * k + j], recv_sem=recv_sems.at[6 * k + j],
                device_id=to, device_id_type=MESH)

        local = [pltpu.make_async_copy(ins[k], outs[k].at[my_s], loc_sems.at[k]) for k in range(n)]
        for cp in local:
            cp.start()
        started = []
        for k in range(n):
            for j, chip in enumerate(chips):
                cp = rcopy(k, j, ins[k].at[pl.ds(c * halves[k], halves[k])], half(k, my_s, c), (*chip, c))
                cp.start()
                started.append(cp)
        for j, chip in enumerate(chips):
            s_j = 2 * chip[0] + chip[1]
            for k in range(n):
                rcopy(k, j, half(k, s_j, c), half(k, s_j, c), me).wait_recv()
                cp = rcopy(k, 3 + j, half(k, s_j, c), half(k, s_j, c), sibling)
                cp.start()
                started.append(cp)
        for j, chip in enumerate(chips):
            s_j = 2 * chip[0] + chip[1]
            for k in range(n):
                rcopy(k, 3 + j, half(k, s_j, 1 - c), half(k, s_j, 1 - c), me).wait_recv()
        for cp in started:
            cp.wait_send()
        for cp in local:
            cp.wait()

    return pl.pallas_call(
        body, name="gather_weights",
        in_specs=[pl.BlockSpec(memory_space=pl.ANY)] * n,
        out_specs=[pl.BlockSpec(memory_space=pl.ANY)] * n,
        out_shape=[jax.ShapeDtypeStruct((N_CHIPS,) + a.shape, a.dtype) for a in shards],
        scratch_shapes=[pltpu.SemaphoreType.DMA((6 * n,)), pltpu.SemaphoreType.DMA((6 * n,)),
                        pltpu.SemaphoreType.DMA((n,))],
    )(*shards)


def _rs_pair_share(fulls):
    n = len(fulls)
    halves = [f.shape[0] // 2 for f in fulls]

    def body(*refs):
        outs = refs[n:2 * n]
        send_sems, recv_sems = refs[2 * n:]
        x, y, c = _mesh_pos()

        def copy(k, cc, to):
            rows = outs[k].at[pl.ds(cc * halves[k], halves[k])]
            return pltpu.make_async_remote_copy(
                src_ref=rows, dst_ref=rows, send_sem=send_sems.at[k], recv_sem=recv_sems.at[k],
                device_id=to, device_id_type=MESH)

        cps = [copy(k, c, (x, y, 1 - c)) for k in range(n)]
        for cp in cps:
            cp.start()
        for k in range(n):
            copy(k, 1 - c, (x, y, c)).wait_recv()
        for cp in cps:
            cp.wait_send()

    return pl.pallas_call(
        body, name="rs_pair_share",
        in_specs=[pl.BlockSpec(memory_space=pl.ANY)] * n,
        out_specs=[pl.BlockSpec(memory_space=pl.ANY)] * n,
        out_shape=[jax.ShapeDtypeStruct(f.shape, f.dtype) for f in fulls],
        input_output_aliases={k: k for k in range(n)},
        scratch_shapes=[pltpu.SemaphoreType.DMA((n,)), pltpu.SemaphoreType.DMA((n,))],
    )(*fulls)


def _pair_sum(g, recv, pos, name):
    _, rows, cols = g.shape
    h = rows // 2
    tr = _row_tile(h, 256)
    nb = h // tr

    def body(pos_ref, g_ref, r_ref, o_ref, own_ref):
        total = g_ref[0] + r_ref[0]
        o_ref[0] = total.astype(BF16)

        @pl.when(pl.program_id(1) == pos_ref[1])
        def _():
            own_ref[...] = total

    return pl.pallas_call(
        body, name=name,
        grid_spec=pltpu.PrefetchScalarGridSpec(
            num_scalar_prefetch=1, grid=(nb, N_CHIPS),
            in_specs=[pl.BlockSpec((1, tr, cols), lambda r, s, pos_ref: (s, pos_ref[0] * nb + r, 0)),
                      pl.BlockSpec((1, tr, cols), lambda r, s, pos_ref: (s, r, 0))],
            out_specs=[pl.BlockSpec((1, tr, cols), lambda r, s, pos_ref: (s, r, 0)),
                       pl.BlockSpec((tr, cols), lambda r, s, pos_ref: (r, 0))]),
        out_shape=[jax.ShapeDtypeStruct((N_CHIPS, h, cols), BF16), jax.ShapeDtypeStruct((h, cols), F32)],
        compiler_params=_params(2, 32))(pos, g, recv)


def _chip_sum(own, rb, pos, name):
    h, cols = own.shape
    tr = _row_tile(h, 256)
    nb = h // tr

    def body(pos_ref, c_ref, r_ref, o_ref):
        del pos_ref
        o_ref[...] = ((c_ref[...] + r_ref[0].astype(F32)) + r_ref[1].astype(F32)) + r_ref[2].astype(F32)

    return pl.pallas_call(
        body, name=name,
        grid_spec=pltpu.PrefetchScalarGridSpec(
            num_scalar_prefetch=1, grid=(nb,),
            in_specs=[pl.BlockSpec((tr, cols), lambda r, pos_ref: (r, 0)),
                      pl.BlockSpec((3, tr, cols), lambda r, pos_ref: (0, r, 0))],
            out_specs=pl.BlockSpec((tr, cols), lambda r, pos_ref: (pos_ref[0] * nb + r, 0))),
        out_shape=jax.ShapeDtypeStruct((2 * h, cols), F32),
        compiler_params=_params(1, 32))(pos, own, rb)


def _adamw_big(w, g, m, v, name):
    rows, cols = w.shape
    tr = _row_tile(rows, 256)

    def body(w_ref, g_ref, m_ref, v_ref, d_ref, nm_ref, nv_ref):
        d_ref[...], nm_ref[...], nv_ref[...] = _adamw(w_ref[...], g_ref[...], m_ref[...], v_ref[...])

    spec = pl.BlockSpec((tr, cols), lambda i: (i, 0))
    return pl.pallas_call(
        body, name=name, grid=(rows // tr,),
        in_specs=[spec] * 4, out_specs=[spec] * 3,
        out_shape=[jax.ShapeDtypeStruct((rows, cols), F32)] * 3,
        compiler_params=_params(1, 48))(w, g, m, v)


def _gather_small(loss, dfg, dlg, dgn, ddwb, dclg, dclb, dpwb, ddww, dmeta):
    d = loss.shape[1]
    w = dgn.shape[1]
    wc, mc = ddww.shape[1] // N_CHIPS, dmeta.shape[1] // N_CHIPS

    def body(loss_ref, dfg_ref, dlg_ref, dgn_ref, ddwb_ref, dclg_ref, dclb_ref, dpwb_ref, ddww_ref, dmeta_ref,
             gs_ref, gd_ref, gm_ref, send_sems, recv_sems, loc_sems):
        x, y, c = _mesh_pos()
        me = 4 * x + 2 * y + c
        gs_ref[me, 0:1, :] = loss_ref[...]
        gs_ref[me, 1:2, :] = dfg_ref[...]
        gs_ref[me, 2:3, :] = dlg_ref[...]
        gs_ref[me, 3:4, 0:w] = dgn_ref[...]
        gs_ref[me, 3:4, w:2 * w] = ddwb_ref[...]
        gs_ref[me, 4:5, 0:w] = dclg_ref[...]
        gs_ref[me, 4:5, w:2 * w] = dclb_ref[...]
        gs_ref[me, 5:6, 0:w] = dpwb_ref[...]
        gs_ref[me, 5:6, w:2 * w] = jnp.zeros((1, d - w), F32)
        gs_ref[me, 6:8, :] = jnp.zeros((2, d), F32)
        bufs = (gs_ref, gd_ref, gm_ref)

        def mine_for(k, shard):
            if k == 0:
                return gs_ref.at[me]
            ref, width = ((ddww_ref, wc), (dmeta_ref, mc))[k - 1]
            return ref.at[:, pl.ds(pl.multiple_of(shard * width, width), width)]

        def peer(j):
            return (1 - x if j & 4 else x), (1 - y if j & 2 else y), (1 - c if j & 1 else c)

        def copy(k, j, src, slot, to):
            return pltpu.make_async_remote_copy(
                src_ref=src, dst_ref=bufs[k].at[slot],
                send_sem=send_sems.at[7 * k + j - 1], recv_sem=recv_sems.at[7 * k + j - 1],
                device_id=to, device_id_type=MESH)

        own = [pltpu.make_async_copy(mine_for(k, 2 * x + y), bufs[k].at[me], loc_sems.at[k - 1]) for k in (1, 2)]
        for cp in own:
            cp.start()
        cps = []
        for k in range(3):
            for j in range(1, N_DEV):
                px, py, pc = peer(j)
                cp = copy(k, j, mine_for(k, 2 * px + py), me, (px, py, pc))
                cp.start()
                cps.append(cp)
        for k in range(3):
            for j in range(1, N_DEV):
                px, py, pc = peer(j)
                slot = 4 * px + 2 * py + pc
                copy(k, j, bufs[k].at[slot], slot, (x, y, c)).wait_recv()
        for cp in cps:
            cp.wait_send()
        for cp in own:
            cp.wait()

    vm = pl.BlockSpec(memory_space=pltpu.VMEM)
    return pl.pallas_call(
        body, name="gather_small",
        in_specs=[vm] * 10, out_specs=[vm] * 3,
        out_shape=[jax.ShapeDtypeStruct((N_DEV, 8, d), F32),
                   jax.ShapeDtypeStruct((N_DEV, ddww.shape[0], wc), F32),
                   jax.ShapeDtypeStruct((N_DEV, dmeta.shape[0], mc), F32)],
        scratch_shapes=[pltpu.SemaphoreType.DMA((21,)), pltpu.SemaphoreType.DMA((21,)),
                        pltpu.SemaphoreType.DMA((2,))],
    )(loss, dfg, dlg, dgn, ddwb, dclg, dclb, dpwb, ddww, dmeta)


def _small_update(gs, gd, gm, weights, ms, vs):
    d = gs.shape[2]
    w = d // 2
    n = len(weights)

    def body(gs_ref, gd_ref, gm_ref, *refs):
        w_refs, m_refs, v_refs = refs[:n], refs[n:2 * n], refs[2 * n:3 * n]
        loss_ref = refs[3 * n]
        g_refs = refs[3 * n + 1:4 * n + 1]
        d_refs = refs[4 * n + 1:5 * n + 1]
        nm_refs = refs[5 * n + 1:6 * n + 1]
        nv_refs = refs[6 * n + 1:7 * n + 1]

        def total(ref):
            t = ref[0]
            for dev in range(1, N_DEV):
                t = t + ref[dev]
            return t

        packed = total(gs_ref)
        loss_ref[...] = jnp.sum(packed[0:1, :], axis=1, keepdims=True) * (0.5 / d)
        grads = [packed[2:3, :], packed[1:2, :], packed[3:4, 0:w], packed[3:4, w:2 * w], packed[4:5, 0:w],
                 packed[4:5, w:2 * w], packed[5:6, 0:w], total(gd_ref), total(gm_ref)]
        for k in range(n):
            g = grads[k]
            g_refs[k][...] = g
            d_refs[k][...], nm_refs[k][...], nv_refs[k][...] = _adamw(w_refs[k][...], g, m_refs[k][...], v_refs[k][...])

    def whole(shape):
        return pl.BlockSpec(shape, lambda i: (0,) * len(shape))

    shapes = [a.shape for a in weights]
    in_specs = [whole(gs.shape), whole(gd.shape), whole(gm.shape)] + [whole(s) for s in shapes] * 3
    out_specs = [whole((1, 1))] + [whole(s) for s in shapes] * 4
    out_shape = [jax.ShapeDtypeStruct((1, 1), F32)] + [jax.ShapeDtypeStruct(s, F32) for s in shapes] * 4
    outs = pl.pallas_call(
        body, name="small_update", grid=(1,), in_specs=in_specs, out_specs=out_specs, out_shape=out_shape,
        compiler_params=_params(1, 32))(gs, gd, gm, *weights, *ms, *vs)
    loss = outs[0]
    return loss, outs[1:n + 1], outs[n + 1:2 * n + 1], outs[2 * n + 1:3 * n + 1], outs[3 * n + 1:4 * n + 1]


def kernel(x, meta_tokens, ln_g, w_in, ret_gn_g, conv_dw_w, conv_dw_b, conv_ln_g, conv_ln_b, conv_pw_w, conv_pw_b, w_out, final_g, loss_target, m_meta_tokens, m_ln_g, m_w_in, m_ret_gn_g, m_conv_dw_w, m_conv_dw_b, m_conv_ln_g, m_conv_ln_b, m_conv_pw_w, m_conv_pw_b, m_w_out, m_final_g, v_meta_tokens, v_ln_g, v_w_in, v_ret_gn_g, v_conv_dw_w, v_conv_dw_b, v_conv_ln_g, v_conv_ln_b, v_conv_pw_w, v_conv_pw_b, v_w_out, v_final_g):
    seq, d = x.shape[1], x.shape[2]
    w = ret_gn_g.shape[1]
    hd = w // RET_HEADS
    lp = CHUNK + seq
    ns = w_in.shape[2]
    mx, my, mc = lax.axis_index("x"), lax.axis_index("y"), lax.axis_index("c")
    my_s = 2 * mx + my
    s_arr = my_s.astype(jnp.int32).reshape(1)
    pos = jnp.stack([mc, my_s]).astype(jnp.int32)
    order = jnp.stack([my_s, 2 * (1 - mx) + my, 2 * mx + (1 - my), 2 * (1 - mx) + (1 - my)]).astype(jnp.int32)

    dw_pad = jnp.pad(conv_dw_w[0], ((0, HALO - CONV_K), (0, 0)))
    dw4, meta4 = _gather_shards([dw_pad, meta_tokens])
    dw_full = dw4.transpose(1, 0, 2).reshape(HALO, w)
    meta_full = meta4.transpose(1, 0, 2).reshape(N_META, d)

    lead = jnp.concatenate([jnp.zeros((LEAD, d), F32), meta_full], axis=0)
    zero_lead = jnp.zeros((CHUNK, d), F32)
    consts = _ret_consts()
    cos, sin = _rope_tables(lp, hd // 2)
    fg2 = final_g.reshape(1, d)

    proj, r1, hn, w4, pw4, wo4 = _in_proj_gather(
        order, x[0], lead, ln_g, _cast_into_gathered(w_in[0], s_arr, "cast_w_in"),
        _cast_into_gathered(conv_pw_w[0], s_arr, "cast_pw_w"), _cast_into_gathered(w_out[0], s_arr, "cast_w_out"))
    pw_full = pw4.reshape(w, w)
    wo_full = wo4.reshape(2 * w, d)
    y_ret, ssave, ret_ops, ret_scores, ret_out = _ret_fwd(proj, cos, sin, ret_gn_g, consts)
    y_conv, u1 = _conv_fwd(proj, dw_full, conv_dw_b, conv_ln_g, conv_ln_b, pw_full, conv_pw_b)
    dh2, dy, loss_l, dfg = _out_proj_loss(y_ret, y_conv, x[0], lead, loss_target[0], zero_lead, wo_full, fg2)

    g_wo = _dw_out(y_ret, y_conv, dh2)
    dproj, dgn = _ret_bwd(proj, dy, ssave, ret_ops, ret_scores, ret_out, cos, sin, ret_gn_g, consts)
    dproj, du1, g_pw, dpwb, dclg, dclb = _conv_bwd_pw(dy, proj, u1, conv_ln_g, conv_ln_b, pw_full, conv_pw_b, dproj)
    dproj, ddww, ddwb = _conv_bwd_dw(du1, proj, dw_full, dproj)
    g_wo4 = g_wo.reshape(N_CHIPS, (2 * w) // N_CHIPS, d)
    g_pw4 = g_pw.reshape(N_CHIPS, w // N_CHIPS, w)
    g_win, recv = _dw_in(hn, dproj, ns, [g_wo4, g_pw4])
    gs = [g_win, g_wo4, g_pw4]
    names = ("w_in", "w_out", "pw_w")
    sums = [_pair_sum(g, r, pos, "pair_sum_" + nm) for g, r, nm in zip(gs, recv, names)]
    dh, dlg, rb = _in_proj_bwd(dproj, w4, x[0], lead, r1, dh2, ln_g, [cs_ for cs_, _ in sums])
    grad_x = dh[CHUNK:][None]
    dmeta = dh[LEAD:CHUNK]
    fulls = [_chip_sum(own, r, pos, "chip_sum_" + nm) for (_, own), r, nm in zip(sums, rb, names)]
    grad_w_in, grad_w_out, grad_pw = _rs_pair_share(fulls)
    d_win, nm_win, nv_win = _adamw_big(w_in[0], grad_w_in, m_w_in[0], v_w_in[0], "adamw_w_in")
    d_wo, nm_wo, nv_wo = _adamw_big(w_out[0], grad_w_out, m_w_out[0], v_w_out[0], "adamw_w_out")
    d_pw, nm_pw, nv_pw = _adamw_big(conv_pw_w[0], grad_pw, m_conv_pw_w[0], v_conv_pw_w[0], "adamw_pw_w")

    gsm, gdm, gmm = _gather_small(loss_l, dfg, dlg, dgn, ddwb, dclg, dclb, dpwb, ddww, dmeta)

    def pad_dw(a):
        return jnp.pad(a[0], ((0, HALO - CONV_K), (0, 0)))

    small_w = [ln_g, fg2, ret_gn_g, conv_dw_b, conv_ln_g, conv_ln_b, conv_pw_b, dw_pad, meta_tokens]
    small_m = [m_ln_g, m_final_g.reshape(1, d), m_ret_gn_g, m_conv_dw_b, m_conv_ln_g, m_conv_ln_b, m_conv_pw_b,
               pad_dw(m_conv_dw_w), m_meta_tokens]
    small_v = [v_ln_g, v_final_g.reshape(1, d), v_ret_gn_g, v_conv_dw_b, v_conv_ln_g, v_conv_ln_b, v_conv_pw_b,
               pad_dw(v_conv_dw_w), v_meta_tokens]
    loss, sg, sd, snm, snv = _small_update(gsm, gdm, gmm, small_w, small_m, small_v)

    def assemble(small, big_in, big_pw, big_out):
        ln, fg, gn, dwb, clg, clb, pwb, dww, meta = small
        return (meta, ln, big_in[None], gn, dww[:CONV_K][None], dwb, clg, clb, big_pw[None], pwb, big_out[None],
                fg.reshape(d))

    return (loss.reshape(()), grad_x,
            *assemble(sg, grad_w_in, grad_pw, grad_w_out),
            *assemble(sd, d_win, d_pw, d_wo),
            *assemble(snm, nm_win, nm_pw, nm_wo),
            *assemble(snv, nv_win, nv_pw, nv_wo))
```

```python
import functools

import jax
import jax.numpy as jnp
from jax import lax
from jax.experimental import pallas as pl
from jax.experimental.pallas import tpu as pltpu

F32 = jnp.float32
BF16 = jnp.bfloat16
MESH = pl.DeviceIdType.MESH

N_META = 16
CHUNK = 128
LEAD = (-N_META) % CHUNK
RET_HEADS = 4
CONV_K = 31
HALO = 32
ROPE_BASE = 10000.0
EPS = 1e-6
N_CHIPS = 4
N_DEV = 8

ADAM_LR = 0.001
ADAM_B1 = 0.9
ADAM_B2 = 0.999
ADAM_EPS = 1e-08
ADAM_WD = 0.01
ADAM_STEP = 10

MIB = 2 ** 20


def _params(n_grid_axes, vmem_mib):
    return pltpu.CompilerParams(dimension_semantics=("arbitrary",) * n_grid_axes,
                                vmem_limit_bytes=vmem_mib * MIB)


def _row_tile(n, pref):
    for t in (1664, 1280, 1024, 640, 512, 384, 320, 256, 128, 64, 32, 16, 8):
        if t <= pref and n % t == 0:
            return t
    raise ValueError(f"no row tile for {n}")


def _dot(a, b):
    return jnp.dot(a, b, preferred_element_type=F32)


def _dot_nt(a, b):
    return lax.dot_general(a, b, (((1,), (1,)), ((), ())), preferred_element_type=F32)


def _dot_tn(a, b):
    return lax.dot_general(a, b, (((0,), (0,)), ((), ())), preferred_element_type=F32)


def _sigmoid(x):
    return jax.nn.sigmoid(x)


def _dsilu(x, s):
    return s * (1.0 + x * (1.0 - s))


def _mean(x):
    return jnp.mean(x, axis=-1, keepdims=True)


def _colsum(x):
    return jnp.sum(x, axis=0, keepdims=True)


def _rope(x, cos, sin):
    half = x.shape[-1] // 2
    x1, x2 = x[:, :half], x[:, half:]
    return jnp.concatenate([x1 * cos - x2 * sin, x1 * sin + x2 * cos], axis=-1)


def _rope_t(d, cos, sin):
    half = d.shape[-1] // 2
    d1, d2 = d[:, :half], d[:, half:]
    return jnp.concatenate([d1 * cos + d2 * sin, d2 * cos - d1 * sin], axis=-1)


def _adamw(w, g, m, v):
    m = ADAM_B1 * m + (1.0 - ADAM_B1) * g
    v = ADAM_B2 * v + (1.0 - ADAM_B2) * (g * g)
    m_hat = m / (1.0 - ADAM_B1 ** ADAM_STEP)
    v_hat = v / (1.0 - ADAM_B2 ** ADAM_STEP)
    delta = -ADAM_LR * (m_hat / (jnp.sqrt(v_hat) + ADAM_EPS) + ADAM_WD * w)
    return delta, m, v


def _ret_consts():
    h = jnp.arange(RET_HEADS, dtype=F32)
    log_g = jnp.log(1.0 - jnp.exp2(-5.0 - h))
    idx = jnp.arange(CHUNK, dtype=F32)
    rel = idx[:, None] - idx[None, :]
    dmask = jnp.where(rel[None] >= 0, jnp.exp(jnp.maximum(rel, 0.0)[None] * log_g[:, None, None]), 0.0)
    qd = jnp.exp((idx[None, :] + 1.0) * log_g[:, None])[:, :, None]
    kd = jnp.exp((CHUNK - 1.0 - idx[None, :]) * log_g[:, None])[:, :, None]
    cd = jnp.exp(CHUNK * log_g)[:, None, None]
    return dmask, qd, kd, cd


def _rope_tables(nch, half):
    inv_freq = ROPE_BASE ** (-jnp.arange(half, dtype=F32) / half)
    start = (jnp.arange(nch, dtype=F32) * float(CHUNK) - float(LEAD))[:, None] * inv_freq[None, :]
    within = jnp.arange(CHUNK, dtype=F32)[:, None] * inv_freq[None, :]
    return jnp.cos(start)[:, None, :], jnp.sin(start)[:, None, :], jnp.cos(within), jnp.sin(within)


def _rope_of_chunk(ca, sa, cb, sb):
    return ca * cb - sa * sb, sa * cb + ca * sb


def _padded_tile_stream(x_hbm, lead_hbm, buf, sems, tm):
    def start_first(slot):
        pltpu.make_async_copy(lead_hbm, buf.at[slot, pl.ds(0, CHUNK)], sems.at[slot]).start()
        pltpu.make_async_copy(x_hbm.at[pl.ds(0, tm - CHUNK)], buf.at[slot, pl.ds(CHUNK, tm - CHUNK)],
                              sems.at[slot]).start()

    def start(slot, tile):
        pltpu.make_async_copy(x_hbm.at[pl.ds(tile * tm - CHUNK, tm)], buf.at[slot], sems.at[slot]).start()

    def wait(slot):
        pltpu.make_async_copy(x_hbm.at[pl.ds(0, tm)], buf.at[slot], sems.at[slot]).wait()

    return start_first, start, wait


def _cast_into_gathered(a, s_arr, name):
    rows, cols = a.shape
    tr = _row_tile(rows, 256)

    def body(s_ref, a_ref, o_ref):
        del s_ref
        o_ref[0] = a_ref[...].astype(BF16)

    return pl.pallas_call(
        body, name=name,
        grid_spec=pltpu.PrefetchScalarGridSpec(
            num_scalar_prefetch=1, grid=(rows // tr,),
            in_specs=[pl.BlockSpec((tr, cols), lambda i, s_ref: (i, 0))],
            out_specs=pl.BlockSpec((1, tr, cols), lambda i, s_ref: (s_ref[0], i, 0))),
        out_shape=jax.ShapeDtypeStruct((N_CHIPS, rows, cols), BF16),
        compiler_params=_params(1, 32))(s_arr, a)


def _in_proj_gather(order, x, lead, ln_g, w4, pw4, wo4):
    seq, d = x.shape
    lp = CHUNK + seq
    ns = w4.shape[2]
    tm = _row_tile(lp, 640)
    nt = lp // tm
    assert nt >= 2, "the hn write-back of a row tile is waited for one step later, before any pass re-reads it"
    land_step, load_step = max(nt - 3, 0), max(nt - 2, 0)
    gathered = (w4, pw4, wo4)
    halves = [a.shape[1] // 2 for a in gathered]
    n = len(gathered)

    def body(order_ref, x_hbm, lead_hbm, g_ref, w_in, pw_in, wo_in, proj_ref, r_ref, hn_hbm, w_out, pw_out,
             wo_out, wbuf, hnbuf, hbuf, send_sems, recv_sems, hn_out_sems, hn_in_sems, h_sems, w_sem):
        del order_ref, w_in, pw_in, wo_in
        t, i = pl.program_id(0), pl.program_id(1)
        slot = (t * nt + i) % 2
        h_first, h_start, h_wait = _padded_tile_stream(x_hbm, lead_hbm, hbuf, h_sems, tm)
        x, y, c = _mesh_pos()
        me, sibling = (x, y, c), (x, y, 1 - c)
        my_s = 2 * x + y
        chips = _other_chips(x, y)
        outs = (w_out, pw_out, wo_out)

        def half(k, s, cc):
            return outs[k].at[s, pl.ds(cc * halves[k], halves[k])]

        def rcopy(k, j, rows, to):
            return pltpu.make_async_remote_copy(
                src_ref=rows, dst_ref=rows, send_sem=send_sems.at[6 * k + j], recv_sem=recv_sems.at[6 * k + j],
                device_id=to, device_id_type=MESH)

        def send(k, j):
            return rcopy(k, j, half(k, my_s, c), (*chips[j], c))

        def shard_of(j):
            return 2 * chips[j][0] + chips[j][1]

        def forward(k, j):
            return rcopy(k, 3 + j, half(k, shard_of(j), c), sibling)

        def land(k, j):
            rcopy(k, j, half(k, shard_of(j), c), me).wait_recv()
            forward(k, j).start()

        def landed_from_sibling(k, j):
            rcopy(k, 3 + j, half(k, shard_of(j), 1 - c), me).wait_recv()

        def load_w(s, wslot):
            return pltpu.make_async_copy(w_out.at[s], wbuf.at[wslot], w_sem)

        def hn_out(sl, row_tile):
            return pltpu.make_async_copy(hnbuf.at[sl], hn_hbm.at[pl.ds(row_tile * tm, tm)], hn_out_sems.at[sl])

        def hn_in(sl, row_tile):
            return pltpu.make_async_copy(hn_hbm.at[pl.ds(row_tile * tm, tm)], hnbuf.at[sl], hn_in_sems.at[sl])

        @pl.when((t == 0) & (i == 0))
        def _():
            for j in range(2):
                send(0, j).start()
            load_w(my_s, 0).start()
            load_w(my_s, 0).wait()

        @pl.when((t == 1) & (i == 0))
        def _():
            send(0, 2).start()
            for k in range(1, n):
                for jj in range(3):
                    send(k, jj).start()

        for j in range(3):
            @pl.when((t == j) & (i == land_step))
            def _(j=j):
                land(0, j)

            @pl.when((t == j) & (i == load_step))
            def _(j=j):
                landed_from_sibling(0, j)
                load_w(shard_of(j), (j + 1) % 2).start()

            @pl.when((t == j + 1) & (i == 0))
            def _(j=j):
                load_w(shard_of(j), (j + 1) % 2).wait()

        @pl.when((t == 0) & (i == 0))
        def _():
            h_first(0)

        @pl.when((t == 0) & (i + 1 < nt))
        def _():
            h_start((i + 1) % 2, i + 1)

        @pl.when(t == 0)
        def _():
            h_wait(i % 2)
            h = hbuf[i % 2]
            r = lax.rsqrt(_mean(h * h) + EPS)
            hnbuf[slot] = ((h * r) * g_ref[...]).astype(BF16)
            r_ref[...] = r
            hn_out(slot, i).start()

        @pl.when(t > 0)
        def _():
            hn_in(slot, i).wait()

        @pl.when(((t == 0) & (i > 0)) | ((t == 1) & (i == 0)))
        def _():
            hn_out(1 - slot, jnp.where(i > 0, i - 1, nt - 1)).wait()

        last = (t == N_CHIPS - 1) & (i == nt - 1)

        @pl.when(((t > 0) | (i == nt - 1)) & jnp.logical_not(last))
        def _():
            hn_in(1 - slot, jnp.where(i == nt - 1, 0, i + 1)).start()

        proj_ref[...] = _dot(hnbuf[slot], wbuf[t % 2])

        @pl.when(last)
        def _():
            for k in range(1, n):
                for j in range(3):
                    land(k, j)
            for k in range(1, n):
                for j in range(3):
                    landed_from_sibling(k, j)
            for k in range(n):
                for j in range(3):
                    send(k, j).wait_send()
                    forward(k, j).wait_send()

    def frozen(t, i):
        return jnp.where(t == 0, i, nt - 1)

    any_spec = pl.BlockSpec(memory_space=pl.ANY)
    return pl.pallas_call(
        body, name="in_proj_gather",
        grid_spec=pltpu.PrefetchScalarGridSpec(
            num_scalar_prefetch=1, grid=(N_CHIPS, nt),
            in_specs=[any_spec, any_spec, pl.BlockSpec((1, d), lambda t, i, o: (0, 0)),
                      any_spec, any_spec, any_spec],
            out_specs=[pl.BlockSpec((tm, ns), lambda t, i, o: (i, o[t])),
                       pl.BlockSpec((tm, 1), lambda t, i, o: (frozen(t, i), 0)),
                       any_spec, any_spec, any_spec, any_spec],
            scratch_shapes=[pltpu.VMEM((2, d, ns), BF16), pltpu.VMEM((2, tm, d), BF16), pltpu.VMEM((2, tm, d), F32),
                            pltpu.SemaphoreType.DMA((6 * n,)), pltpu.SemaphoreType.DMA((6 * n,)),
                            pltpu.SemaphoreType.DMA((2,)), pltpu.SemaphoreType.DMA((2,)),
                            pltpu.SemaphoreType.DMA((2,)), pltpu.SemaphoreType.DMA]),
        out_shape=[jax.ShapeDtypeStruct((lp, N_CHIPS * ns), F32),
                   jax.ShapeDtypeStruct((lp, 1), F32),
                   jax.ShapeDtypeStruct((lp, d), BF16)]
                  + [jax.ShapeDtypeStruct(a.shape, a.dtype) for a in gathered],
        input_output_aliases={4: 3, 5: 4, 6: 5},
        compiler_params=_params(2, 48))(order, x, lead, ln_g, w4, pw4, wo4)


def _chunks_per_step(nch):
    return next(n for n in (5, 4, 3, 2, 1) if nch % n == 0)


def _ret_fwd(proj, rope, gn_g, consts):
    lp = proj.shape[0]
    w = gn_g.shape[1]
    hd = w // RET_HEADS
    nch = lp // CHUNK
    cps = _chunks_per_step(nch)
    tr = cps * CHUNK
    dmask, qd, kd, cd = consts

    def body(q_ref, k_ref, v_ref, g_ref, ca_ref, sa_ref, cb_ref, sb_ref, gn_ref, m_ref, qd_ref, kd_ref, cd_ref,
             y_ref, ssave_ref, ops_ref, sc_ref, out_ref, s_scr):
        @pl.when(pl.program_id(0) == 0)
        def _():
            s_scr[...] = jnp.zeros_like(s_scr)

        def chunk(cc, carry):
            rows = pl.ds(pl.multiple_of(cc * CHUNK, CHUNK), CHUNK)
            cos_t, sin_t = _rope_of_chunk(ca_ref[cc], sa_ref[cc], cb_ref[...], sb_ref[...])
            for h in range(RET_HEADS):
                sl = slice(h * hd, (h + 1) * hd)
                qr = _rope(q_ref[rows, sl], cos_t, sin_t)
                kr = _rope(k_ref[rows, sl], cos_t, sin_t) * (hd ** -0.5)
                qb, kb = qr.astype(BF16), kr.astype(BF16)
                qdq, kdk = (qr * qd_ref[h]).astype(BF16), (kr * kd_ref[h]).astype(BF16)
                vb = v_ref[rows, sl].astype(BF16)
                for k, operand in enumerate((qb, kb, qdq, kdk, vb)):
                    ops_ref[k, rows, sl] = operand
                scb = (_dot_nt(qb, kb) * m_ref[h]).astype(BF16)
                sc_ref[cc, h] = scb
                state = s_scr[h]
                sb = state.astype(BF16)
                ssave_ref[cc, h] = sb
                out = _dot(scb, vb) + _dot(qdq, sb)
                out_ref[rows, sl] = out
                s_scr[h] = state * cd_ref[h] + _dot_tn(kdk, vb)
                dev = out - _mean(out)
                yn = dev * lax.rsqrt(_mean(dev * dev) + EPS)
                g = g_ref[rows, sl]
                y_ref[rows, sl] = ((yn * gn_ref[:, sl]) * (g * _sigmoid(g))).astype(BF16)
            return carry

        lax.fori_loop(0, cps, chunk, 0)

    def col(j):
        return pl.BlockSpec((tr, w), lambda i: (i, j))

    def whole(a):
        return pl.BlockSpec(a.shape, lambda i: (0,) * a.ndim)

    return pl.pallas_call(
        body, name="ret_fwd", grid=(nch // cps,),
        in_specs=[col(0), col(1), col(2), col(3),
                  pl.BlockSpec((cps, 1, hd // 2), lambda i: (i, 0, 0)),
                  pl.BlockSpec((cps, 1, hd // 2), lambda i: (i, 0, 0)),
                  whole(rope[2]), whole(rope[3]),
                  whole(gn_g), whole(dmask), whole(qd), whole(kd), whole(cd)],
        out_specs=[pl.BlockSpec((tr, w), lambda i: (i, 0)),
                   pl.BlockSpec((cps, RET_HEADS, hd, hd), lambda i: (i, 0, 0, 0)),
                   pl.BlockSpec((5, tr, w), lambda i: (0, i, 0)),
                   pl.BlockSpec((cps, RET_HEADS, CHUNK, CHUNK), lambda i: (i, 0, 0, 0)),
                   pl.BlockSpec((tr, w), lambda i: (i, 0))],
        out_shape=[jax.ShapeDtypeStruct((lp, w), BF16),
                   jax.ShapeDtypeStruct((nch, RET_HEADS, hd, hd), BF16),
                   jax.ShapeDtypeStruct((5, lp, w), BF16),
                   jax.ShapeDtypeStruct((nch, RET_HEADS, CHUNK, CHUNK), BF16),
                   jax.ShapeDtypeStruct((lp, w), F32)],
        scratch_shapes=[pltpu.VMEM((RET_HEADS, hd, hd), F32)],
        compiler_params=_params(1, 56))(proj, proj, proj, proj, *rope, gn_g, dmask, qd, kd, cd)


def _tap_groups(start, flip):
    groups = {}
    for j in range(CONV_K):
        o = start + (CONV_K - 1 - j if flip else j)
        groups.setdefault(o % 8, []).append((o // 8, j))
    return groups


def _shift_up(win, s):
    return win if s == 0 else pltpu.roll(win, win.shape[0] - s, axis=0)


def _dw_taps(src_ref, w_ref, dst_ref, bias, *, rows, start, flip, rb):
    cw = dst_ref.shape[1]
    lb = min(128, cw)
    groups = _tap_groups(start, flip)

    def rb_body(r, carry):
        base = pl.multiple_of(r * rb, rb)
        for cb in range(cw // lb):
            ls = slice(cb * lb, (cb + 1) * lb)
            win = src_ref[pl.ds(base, rb + HALO), ls]
            acc = jnp.zeros((rb, lb), F32) if bias is None else jnp.broadcast_to(bias[:, ls], (rb, lb))
            for s, taps in groups.items():
                ws = _shift_up(win, s)
                for a, j in taps:
                    acc = acc + ws[8 * a:8 * a + rb, :] * w_ref[j:j + 1, ls]
            dst_ref[pl.ds(base, rb), ls] = acc
        return carry

    lax.fori_loop(0, rows // rb, rb_body, 0)


def _conv_fwd(proj, dw_w, dw_b, cln_g, cln_b, pw_w, pw_b):
    lp = proj.shape[0]
    cw = dw_b.shape[1]
    tm = _row_tile(lp, 640)
    rb = _row_tile(tm, 64)

    def body(a_ref, b_ref, gc_ref, w_ref, wb_ref, lg_ref, lb_ref, pw_ref, pb_ref, y_ref, u1_ref, buf):
        @pl.when(pl.program_id(0) == 0)
        def _():
            buf[0:HALO, :] = jnp.zeros((HALO, cw), F32)

        buf[HALO:HALO + tm, :] = a_ref[...] * _sigmoid(b_ref[...])
        _dw_taps(buf, w_ref, u1_ref, wb_ref[...], rows=tm, start=HALO - (CONV_K - 1), flip=False, rb=rb)
        buf[0:HALO, :] = buf[tm:tm + HALO, :]
        u1 = u1_ref[...]
        dev = u1 - _mean(u1)
        z = dev * lax.rsqrt(_mean(dev * dev) + EPS) * lg_ref[...] + lb_ref[...]
        u3 = (z * _sigmoid(z)).astype(BF16)
        u4 = _dot(u3, pw_ref[...]) + pb_ref[...]
        gc = gc_ref[...]
        y_ref[...] = (u4 * (gc * _sigmoid(gc))).astype(BF16)

    def col(j):
        return pl.BlockSpec((tm, cw), lambda i: (i, j))

    def whole(a):
        return pl.BlockSpec(a.shape, lambda i: (0,) * a.ndim)

    return pl.pallas_call(
        body, name="conv_fwd", grid=(lp // tm,),
        in_specs=[col(4), col(5), col(6), whole(dw_w), whole(dw_b), whole(cln_g), whole(cln_b),
                  whole(pw_w), whole(pw_b)],
        out_specs=[pl.BlockSpec((tm, cw), lambda i: (i, 0)), pl.BlockSpec((tm, cw), lambda i: (i, 0))],
        out_shape=[jax.ShapeDtypeStruct((lp, cw), BF16), jax.ShapeDtypeStruct((lp, cw), F32)],
        scratch_shapes=[pltpu.VMEM((tm + HALO, cw), F32)],
        compiler_params=_params(1, 48))(proj, proj, proj, dw_w, dw_b, cln_g, cln_b, pw_w, pw_b)


def _out_proj_loss(yr, yc, x, lead, tgt, zero_lead, w_out, final_g):
    seq, d = x.shape
    lp = CHUNK + seq
    w = yr.shape[1]
    tm = _row_tile(lp, 320)
    nt = lp // tm
    assert tm > CHUNK

    def body(yr_ref, yc_ref, x_hbm, lead_hbm, t_hbm, zlead_hbm, w_ref, fg_ref, dh2_ref, dy_ref, loss_ref, dfg_ref,
             hbuf, tbuf, hsems, tsems):
        i = pl.program_id(0)
        slot = i % 2
        streams = (_padded_tile_stream(x_hbm, lead_hbm, hbuf, hsems, tm),
                   _padded_tile_stream(t_hbm, zlead_hbm, tbuf, tsems, tm))

        @pl.when(i == 0)
        def _():
            loss_ref[...] = jnp.zeros_like(loss_ref)
            dfg_ref[...] = jnp.zeros_like(dfg_ref)
            for start_first, _, _ in streams:
                start_first(0)

        @pl.when(i + 1 < nt)
        def _():
            for _, start, _ in streams:
                start(1 - slot, i + 1)

        for _, _, wait in streams:
            wait(slot)

        h2 = hbuf[slot] + (_dot(yr_ref[...], w_ref[0:w, :]) + _dot(yc_ref[...], w_ref[w:2 * w, :]))
        r2 = lax.rsqrt(_mean(h2 * h2) + EPS)
        h2n = h2 * r2
        fg = fg_ref[...]
        rows = i * tm + lax.broadcasted_iota(jnp.int32, (tm, 1), 0)
        err = jnp.where(rows >= CHUNK, h2n * fg - tbuf[slot], 0.0)
        loss_ref[...] += _colsum(err * err)
        dout = err * (1.0 / d)
        dfg_ref[...] += _colsum(dout * h2n)
        dz = dout * fg
        dh2 = r2 * (dz - h2n * _mean(dz * h2n))
        dh2_ref[...] = dh2
        db = dh2.astype(BF16)
        dy_ref[:, 0:w] = _dot_nt(db, w_ref[0:w, :])
        dy_ref[:, w:2 * w] = _dot_nt(db, w_ref[w:2 * w, :])

    def row(cols):
        return pl.BlockSpec((tm, cols), lambda i: (i, 0))

    any_spec = pl.BlockSpec(memory_space=pl.ANY)
    return pl.pallas_call(
        body, name="out_proj_loss", grid=(nt,),
        in_specs=[row(w), row(w), any_spec, any_spec, any_spec, any_spec,
                  pl.BlockSpec(memory_space=pltpu.VMEM),
                  pl.BlockSpec((1, d), lambda i: (0, 0))],
        out_specs=[row(d), row(2 * w), pl.BlockSpec((1, d), lambda i: (0, 0)),
                   pl.BlockSpec((1, d), lambda i: (0, 0))],
        out_shape=[jax.ShapeDtypeStruct((lp, d), F32), jax.ShapeDtypeStruct((lp, 2 * w), F32),
                   jax.ShapeDtypeStruct((1, d), F32), jax.ShapeDtypeStruct((1, d), F32)],
        scratch_shapes=[pltpu.VMEM((2, tm, d), F32), pltpu.VMEM((2, tm, d), F32),
                        pltpu.SemaphoreType.DMA((2,)), pltpu.SemaphoreType.DMA((2,))],
        compiler_params=_params(1, 56))(yr, yc, x, lead, tgt, zero_lead, w_out, final_g)


def _dw_out(yr, yc, dh2):
    lp, d = dh2.shape
    w = yr.shape[1]
    tm = _row_tile(lp, 1664)
    nb = 2
    dn = d // nb

    def body(yr_ref, yc_ref, d_ref, o_ref):
        @pl.when(pl.program_id(1) == 0)
        def _():
            o_ref[...] = jnp.zeros_like(o_ref)

        db = d_ref[...].astype(BF16)
        o_ref[0:w, :] += _dot_tn(yr_ref[...], db)
        o_ref[w:2 * w, :] += _dot_tn(yc_ref[...], db)

    return pl.pallas_call(
        body, name="dw_out", grid=(nb, lp // tm),
        in_specs=[pl.BlockSpec((tm, w), lambda n, i: (i, 0)),
                  pl.BlockSpec((tm, w), lambda n, i: (i, 0)),
                  pl.BlockSpec((tm, dn), lambda n, i: (i, n))],
        out_specs=pl.BlockSpec((2 * w, dn), lambda n, i: (0, n)),
        out_shape=jax.ShapeDtypeStruct((2 * w, d), F32),
        compiler_params=_params(2, 52))(yr, yc, dh2)


def _ret_bwd(proj, dy, ssave, ops, scores, out_pre, rope, gn_g, consts):
    lp = proj.shape[0]
    w = gn_g.shape[1]
    hd = w // RET_HEADS
    nch = lp // CHUNK
    cps = _chunks_per_step(nch)
    tr = cps * CHUNK
    dmask, qd, kd, cd = consts

    def body(g_ref, dy_ref, ss_ref, ops_ref, sc_ref, out_ref, ca_ref, sa_ref, cb_ref, sb_ref, gn_ref, m_ref, qd_ref,
             kd_ref, cd_ref, dp_ref, dgn_ref, ds_scr):
        @pl.when(pl.program_id(0) == 0)
        def _():
            ds_scr[...] = jnp.zeros_like(ds_scr)
            dgn_ref[...] = jnp.zeros_like(dgn_ref)

        def chunk(n, carry):
            cc = cps - 1 - n
            rows = pl.ds(pl.multiple_of(cc * CHUNK, CHUNK), CHUNK)
            cos_t, sin_t = _rope_of_chunk(ca_ref[cc], sa_ref[cc], cb_ref[...], sb_ref[...])
            for h in range(RET_HEADS):
                sl = slice(h * hd, (h + 1) * hd)
                qb, kb, qdq, kdk, vb = (ops_ref[k, rows, sl] for k in range(5))
                sb = ss_ref[cc, h]
                scb = sc_ref[cc, h]
                mask = m_ref[h]
                qdec, kdec = qd_ref[h], kd_ref[h]
                out = out_ref[rows, sl]
                dev = out - _mean(out)
                rstd = lax.rsqrt(_mean(dev * dev) + EPS)
                yn = dev * rstd
                g = g_ref[rows, sl]
                sg = _sigmoid(g)
                gng = gn_ref[:, sl]
                dyv = dy_ref[rows, sl]
                dgr = dyv * (yn * gng) * _dsilu(g, sg)
                silu_g = g * sg
                dgn_ref[:, sl] += _colsum(dyv * yn * silu_g)
                dyn = dyv * gng * silu_g
                dout = rstd * (dyn - _mean(dyn) - yn * _mean(dyn * yn))
                dob = dout.astype(BF16)
                by_v_and_s = _dot_nt(dob, jnp.concatenate([vb, sb], axis=0))
                dscb = (by_v_and_s[:, 0:CHUNK] * mask).astype(BF16)
                dstate = ds_scr[h]
                dsb = dstate.astype(BF16)
                dq = _dot(dscb, kb) + by_v_and_s[:, CHUNK:] * qdec
                dk = _dot_tn(dscb, qb) + _dot_nt(vb, dsb) * kdec
                onto_do = _dot_tn(jnp.concatenate([scb, qdq], axis=1), dob)
                dv = onto_do[0:CHUNK] + _dot(kdk, dsb)
                ds_scr[h] = dstate * cd_ref[h] + onto_do[CHUNK:]
                dp_ref[rows, 0 * w + h * hd:0 * w + (h + 1) * hd] = _rope_t(dq, cos_t, sin_t).astype(BF16)
                dp_ref[rows, 1 * w + h * hd:1 * w + (h + 1) * hd] = (
                    _rope_t(dk, cos_t, sin_t) * (hd ** -0.5)).astype(BF16)
                dp_ref[rows, 2 * w + h * hd:2 * w + (h + 1) * hd] = dv.astype(BF16)
                dp_ref[rows, 3 * w + h * hd:3 * w + (h + 1) * hd] = dgr.astype(BF16)
            return carry

        lax.fori_loop(0, cps, chunk, 0)

    def rev(i):
        return nch // cps - 1 - i

    def col(j):
        return pl.BlockSpec((tr, w), lambda i: (rev(i), j))

    def whole(a):
        return pl.BlockSpec(a.shape, lambda i: (0,) * a.ndim)

    return pl.pallas_call(
        body, name="ret_bwd", grid=(nch // cps,),
        in_specs=[col(3),
                  pl.BlockSpec((tr, w), lambda i: (rev(i), 0)),
                  pl.BlockSpec((cps, RET_HEADS, hd, hd), lambda i: (rev(i), 0, 0, 0)),
                  pl.BlockSpec((5, tr, w), lambda i: (0, rev(i), 0)),
                  pl.BlockSpec((cps, RET_HEADS, CHUNK, CHUNK), lambda i: (rev(i), 0, 0, 0)),
                  pl.BlockSpec((tr, w), lambda i: (rev(i), 0)),
                  pl.BlockSpec((cps, 1, hd // 2), lambda i: (rev(i), 0, 0)),
                  pl.BlockSpec((cps, 1, hd // 2), lambda i: (rev(i), 0, 0)),
                  whole(rope[2]), whole(rope[3]),
                  whole(gn_g), whole(dmask), whole(qd), whole(kd), whole(cd)],
        out_specs=[pl.BlockSpec((tr, 4 * w), lambda i: (rev(i), 0)),
                   pl.BlockSpec((1, w), lambda i: (0, 0))],
        out_shape=[jax.ShapeDtypeStruct((lp, 7 * w), BF16), jax.ShapeDtypeStruct((1, w), F32)],
        scratch_shapes=[pltpu.VMEM((RET_HEADS, hd, hd), F32)],
        compiler_params=_params(1, 56))(proj, dy, ssave, ops, scores, out_pre, *rope, gn_g, dmask, qd, kd, cd)


def _conv_bwd_pw(dy, proj, u1, cln_g, cln_b, pw_w, pw_b, dproj):
    lp, cw = u1.shape
    tm = _row_tile(lp, 320)

    def body(dy_ref, gc_ref, u1_ref, lg_ref, lb_ref, pw_ref, pb_ref, dp_in, dp_ref, du1_ref, dpw_ref,
             dpb_ref, dlg_ref, dlb_ref):
        del dp_in

        @pl.when(pl.program_id(0) == 0)
        def _():
            dpw_ref[...] = jnp.zeros_like(dpw_ref)
            dpb_ref[...] = jnp.zeros_like(dpb_ref)
            dlg_ref[...] = jnp.zeros_like(dlg_ref)
            dlb_ref[...] = jnp.zeros_like(dlb_ref)

        u1 = u1_ref[...]
        dev = u1 - _mean(u1)
        rstd = lax.rsqrt(_mean(dev * dev) + EPS)
        u1n = dev * rstd
        lg = lg_ref[...]
        z = u1n * lg + lb_ref[...]
        sz = _sigmoid(z)
        u3b = (z * sz).astype(BF16)
        u4 = _dot(u3b, pw_ref[...]) + pb_ref[...]
        gc = gc_ref[...]
        sgc = _sigmoid(gc)
        dyc = dy_ref[...]
        du4 = dyc * (gc * sgc)
        dp_ref[...] = (dyc * u4 * _dsilu(gc, sgc)).astype(BF16)
        du4b = du4.astype(BF16)
        dpb_ref[...] += _colsum(du4)
        dpw_ref[...] += _dot_tn(u3b, du4b)
        dz = _dot_nt(du4b, pw_ref[...]) * _dsilu(z, sz)
        dlg_ref[...] += _colsum(dz * u1n)
        dlb_ref[...] += _colsum(dz)
        dn = dz * lg
        du1_ref[...] = rstd * (dn - _mean(dn) - u1n * _mean(dn * u1n))

    def row(j):
        return pl.BlockSpec((tm, cw), lambda i: (i, j))

    def whole(a):
        return pl.BlockSpec(a.shape, lambda i: (0,) * a.ndim)

    def acc(r):
        return pl.BlockSpec((r, cw), lambda i: (0, 0))

    return pl.pallas_call(
        body, name="conv_bwd_pw", grid=(lp // tm,),
        in_specs=[row(1), row(6), row(0), whole(cln_g), whole(cln_b), whole(pw_w), whole(pw_b),
                  pl.BlockSpec(memory_space=pl.ANY)],
        out_specs=[row(6), row(0), acc(cw), acc(1), acc(1), acc(1)],
        out_shape=[jax.ShapeDtypeStruct(dproj.shape, dproj.dtype), jax.ShapeDtypeStruct((lp, cw), F32),
                   jax.ShapeDtypeStruct((cw, cw), F32), jax.ShapeDtypeStruct((1, cw), F32),
                   jax.ShapeDtypeStruct((1, cw), F32), jax.ShapeDtypeStruct((1, cw), F32)],
        input_output_aliases={7: 0},
        compiler_params=_params(1, 48))(dy, proj, u1, cln_g, cln_b, pw_w, pw_b, dproj)


def _conv_bwd_dw(du1, proj, dw_w, dproj):
    lp, cw = du1.shape
    tm = _row_tile(lp, 640)
    rb = _row_tile(tm, 64)
    nt = lp // tm
    hb = tm // HALO

    def body(a_ref, b_ref, du_ref, nx_ref, w_ref, dp_in, dp_ref, dww_ref, dwb_ref, ubuf, dbuf, du0, acc):
        del dp_in
        i = pl.program_id(0)

        @pl.when(i == 0)
        def _():
            ubuf[0:HALO, :] = jnp.zeros((HALO, cw), F32)
            acc[...] = jnp.zeros_like(acc)
            dwb_ref[...] = jnp.zeros_like(dwb_ref)

        a = a_ref[...]
        sb = _sigmoid(b_ref[...])
        ubuf[HALO:HALO + tm, :] = a * sb
        du = du_ref[...]
        dbuf[0:tm, :] = du
        dbuf[tm:tm + HALO, :] = jnp.where(i == nt - 1, 0.0, nx_ref[...])
        dwb_ref[...] += _colsum(du)
        _dw_taps(dbuf, w_ref, du0, None, rows=tm, start=0, flip=True, rb=rb)
        d0 = du0[...]
        dp_ref[:, 0:cw] = (d0 * sb).astype(BF16)
        dp_ref[:, cw:2 * cw] = (d0 * a * sb * (1.0 - sb)).astype(BF16)

        lb = min(128, cw)
        groups = _tap_groups(HALO - (CONV_K - 1), False)

        def rb_body(r, carry):
            base = pl.multiple_of(r * rb, rb)
            for cb in range(cw // lb):
                ls = slice(cb * lb, (cb + 1) * lb)
                win = ubuf[pl.ds(base, rb + HALO), ls]
                dv = dbuf[pl.ds(base, rb), ls]
                for s, taps in groups.items():
                    ws = _shift_up(win, s)
                    for a, j in taps:
                        prod = dv * ws[8 * a:8 * a + rb, :]
                        acc[8 * j:8 * j + 8, ls] += jnp.sum(prod.reshape(rb // 8, 8, lb), axis=0)
            return carry

        lax.fori_loop(0, tm // rb, rb_body, 0)
        ubuf[0:HALO, :] = ubuf[tm:tm + HALO, :]

        @pl.when(i == nt - 1)
        def _():
            for j in range(CONV_K):
                dww_ref[j:j + 1, :] = _colsum(acc[8 * j:8 * j + 8, :])
            dww_ref[CONV_K:HALO, :] = jnp.zeros((HALO - CONV_K, cw), F32)

    def col(j):
        return pl.BlockSpec((tm, cw), lambda i: (i, j))

    return pl.pallas_call(
        body, name="conv_bwd_dw", grid=(nt,),
        in_specs=[col(4), col(5), col(0),
                  pl.BlockSpec((HALO, cw), lambda i: (jnp.minimum((i + 1) * hb, nt * hb - 1), 0)),
                  pl.BlockSpec(dw_w.shape, lambda i: (0, 0)),
                  pl.BlockSpec(memory_space=pl.ANY)],
        out_specs=[pl.BlockSpec((tm, 2 * cw), lambda i: (i, 2)),
                   pl.BlockSpec((HALO, cw), lambda i: (0, 0)),
                   pl.BlockSpec((1, cw), lambda i: (0, 0))],
        out_shape=[jax.ShapeDtypeStruct(dproj.shape, dproj.dtype), jax.ShapeDtypeStruct((HALO, cw), F32),
                   jax.ShapeDtypeStruct((1, cw), F32)],
        scratch_shapes=[pltpu.VMEM((tm + HALO, cw), F32), pltpu.VMEM((tm + HALO, cw), F32),
                        pltpu.VMEM((tm, cw), F32), pltpu.VMEM((8 * HALO, cw), F32)],
        input_output_aliases={5: 0},
        compiler_params=_params(1, 56))(proj, proj, du1, du1, dw_w, dproj)


def _dw_in(hn, dproj, ns, others):
    lp, d = hn.shape
    tm = _row_tile(lp, 1664)
    nt = lp // tm
    mb = 512 if d % 512 == 0 else d
    n = len(others)
    halves = [d // 2] + [g.shape[1] // 2 for g in others]

    def body(hn_ref, dp_ref, *refs):
        other_refs, o_hbm, recv_refs = refs[:n], refs[n], refs[n + 1:2 * n + 2]
        acc, sem, send_sems, recv_sems = refs[2 * n + 2:]
        s, i = pl.program_id(0), pl.program_id(1)
        x, y, c = _mesh_pos()

        def to_sibling(src, dst, k):
            return pltpu.make_async_remote_copy(
                src_ref=src, dst_ref=dst, send_sem=send_sems.at[k], recv_sem=recv_sems.at[k],
                device_id=(x, y, 1 - c), device_id_type=MESH)

        def shard_half(p):
            return to_sibling(o_hbm.at[p, pl.ds((1 - c) * halves[0], halves[0])], recv_refs[0].at[p], p)

        def other_halves(k):
            return to_sibling(other_refs[k].at[:, pl.ds((1 - c) * halves[1 + k], halves[1 + k])],
                              recv_refs[1 + k], N_CHIPS + k)

        @pl.when((s == 0) & (i == 0))
        def _():
            for k in range(n):
                other_halves(k).start()

        @pl.when(i == 0)
        def _():
            acc[...] = jnp.zeros_like(acc)

        for m in range(d // mb):
            rows = slice(m * mb, (m + 1) * mb)
            acc[rows, :] += _dot_tn(hn_ref[:, rows], dp_ref[...])

        @pl.when(i == nt - 1)
        def _():
            cp = pltpu.make_async_copy(acc, o_hbm.at[s], sem)
            cp.start()
            cp.wait()
            shard_half(s).start()

        @pl.when((s == N_CHIPS - 1) & (i == nt - 1))
        def _():
            for p in range(N_CHIPS):
                shard_half(p).wait()
            for k in range(n):
                other_halves(k).wait()

    any_spec = pl.BlockSpec(memory_space=pl.ANY)
    outs = pl.pallas_call(
        body, name="dw_in", grid=(N_CHIPS, nt),
        in_specs=[pl.BlockSpec((tm, d), lambda s, i: (i, 0)),
                  pl.BlockSpec((tm, ns), lambda s, i: (i, s))] + [any_spec] * n,
        out_specs=[any_spec] * (n + 2),
        out_shape=[jax.ShapeDtypeStruct((N_CHIPS, d, ns), F32), jax.ShapeDtypeStruct((N_CHIPS, d // 2, ns), F32)]
                  + [jax.ShapeDtypeStruct((N_CHIPS, g.shape[1] // 2) + g.shape[2:], g.dtype) for g in others],
        scratch_shapes=[pltpu.VMEM((d, ns), F32), pltpu.SemaphoreType.DMA,
                        pltpu.SemaphoreType.DMA((N_CHIPS + n,)), pltpu.SemaphoreType.DMA((N_CHIPS + n,))],
        compiler_params=_params(2, 56))(hn, dproj, *others)
    return outs[0], outs[1:]


def _in_proj_bwd(dproj, w4, x, lead, r1, dh2, ln_g, cs):
    seq, d = x.shape
    lp = CHUNK + seq
    ns = w4.shape[2]
    tm = _row_tile(lp, 320)
    nt = lp // tm
    n = len(cs)
    assert tm > CHUNK

    def body(dp_ref, w_ref, x_hbm, lead_hbm, r_ref, d2_ref, g_ref, *refs):
        cs_refs, (dh_ref, dlg_ref), rb_refs = refs[:n], refs[n:n + 2], refs[n + 2:2 * n + 2]
        hbuf, h_sems, send_sems, recv_sems = refs[2 * n + 2:]
        i = pl.program_id(0)
        slot = i % 2
        x, y, c = _mesh_pos()
        h_first, h_start, h_wait = _padded_tile_stream(x_hbm, lead_hbm, hbuf, h_sems, tm)

        @pl.when(i == 0)
        def _():
            h_first(0)

        @pl.when(i + 1 < nt)
        def _():
            h_start(1 - slot, i + 1)

        h_wait(slot)

        def exchange():
            return [pltpu.make_async_remote_copy(
                src_ref=cs_refs[k].at[2 * chip[0] + chip[1]], dst_ref=rb_refs[k].at[j],
                send_sem=send_sems.at[3 * k + j], recv_sem=recv_sems.at[3 * k + j],
                device_id=(*chip, c), device_id_type=MESH)
                for k in range(n) for j, chip in enumerate(_other_chips(x, y))]

        @pl.when(i == 0)
        def _():
            dlg_ref[...] = jnp.zeros_like(dlg_ref)
            for cp in exchange():
                cp.start()

        dhn = _dot_nt(dp_ref[:, 0:ns], w_ref[0])
        for s in range(1, N_CHIPS):
            dhn = dhn + _dot_nt(dp_ref[:, s * ns:(s + 1) * ns], w_ref[s])
        r = r_ref[...]
        hn0 = hbuf[slot] * r
        dlg_ref[...] += _colsum(dhn * hn0)
        t = dhn * g_ref[...]
        dh_ref[...] = d2_ref[...] + r * (t - hn0 * _mean(t * hn0))

        @pl.when(i == nt - 1)
        def _():
            for cp in exchange():
                cp.wait()

    def row(cols):
        return pl.BlockSpec((tm, cols), lambda i: (i, 0))

    any_spec = pl.BlockSpec(memory_space=pl.ANY)
    outs = pl.pallas_call(
        body, name="in_proj_bwd", grid=(nt,),
        in_specs=[row(N_CHIPS * ns), pl.BlockSpec(memory_space=pltpu.VMEM),
                  any_spec, any_spec, row(1), row(d), pl.BlockSpec((1, d), lambda i: (0, 0))] + [any_spec] * n,
        out_specs=[row(d), pl.BlockSpec((1, d), lambda i: (0, 0))] + [any_spec] * n,
        out_shape=[jax.ShapeDtypeStruct((lp, d), F32), jax.ShapeDtypeStruct((1, d), F32)]
                  + [jax.ShapeDtypeStruct((3,) + a.shape[1:], a.dtype) for a in cs],
        scratch_shapes=[pltpu.VMEM((2, tm, d), F32), pltpu.SemaphoreType.DMA((2,)),
                        pltpu.SemaphoreType.DMA((3 * n,)), pltpu.SemaphoreType.DMA((3 * n,))],
        compiler_params=_params(1, 58))(dproj, w4, x, lead, r1, dh2, ln_g, *cs)
    return outs[0], outs[1], outs[2:]


def _mesh_pos():
    return lax.axis_index("x"), lax.axis_index("y"), lax.axis_index("c")


def _other_chips(x, y):
    return [(1 - x, y), (x, 1 - y), (1 - x, 1 - y)]


def _gather_shards(shards):
    n = len(shards)
    halves = [a.shape[0] // 2 for a in shards]

    def body(*refs):
        ins, outs = refs[:n], refs[n:2 * n]
        send_sems, recv_sems, loc_sems = refs[2 * n:]
        x, y, c = _mesh_pos()
        me, sibling = (x, y, c), (x, y, 1 - c)
        my_s = 2 * x + y
        chips = _other_chips(x, y)

        def half(k, s, cc):
            return outs[k].at[s, pl.ds(cc * halves[k], halves[k])]

        def rcopy(k, j, src, dst, to):
            return pltpu.make_async_remote_copy(
                src_ref=src, dst_ref=dst, send_sem=send_sems.at[6 * k + j], recv_sem=recv_sems.at[6 * k + j],
                device_id=to, device_id_type=MESH)

        local = [pltpu.make_async_copy(ins[k], outs[k].at[my_s], loc_sems.at[k]) for k in range(n)]
        for cp in local:
            cp.start()
        started = []
        for k in range(n):
            for j, chip in enumerate(chips):
                cp = rcopy(k, j, ins[k].at[pl.ds(c * halves[k], halves[k])], half(k, my_s, c), (*chip, c))
                cp.start()
                started.append(cp)
        for j, chip in enumerate(chips):
            s_j = 2 * chip[0] + chip[1]
            for k in range(n):
                rcopy(k, j, half(k, s_j, c), half(k, s_j, c), me).wait_recv()
                cp = rcopy(k, 3 + j, half(k, s_j, c), half(k, s_j, c), sibling)
                cp.start()
                started.append(cp)
        for j, chip in enumerate(chips):
            s_j = 2 * chip[0] + chip[1]
            for k in range(n):
                rcopy(k, 3 + j, half(k, s_j, 1 - c), half(k, s_j, 1 - c), me).wait_recv()
        for cp in started:
            cp.wait_send()
        for cp in local:
            cp.wait()

    return pl.pallas_call(
        body, name="gather_weights",
        in_specs=[pl.BlockSpec(memory_space=pl.ANY)] * n,
        out_specs=[pl.BlockSpec(memory_space=pl.ANY)] * n,
        out_shape=[jax.ShapeDtypeStruct((N_CHIPS,) + a.shape, a.dtype) for a in shards],
        scratch_shapes=[pltpu.SemaphoreType.DMA((6 * n,)), pltpu.SemaphoreType.DMA((6 * n,)),
                        pltpu.SemaphoreType.DMA((n,))],
    )(*shards)


def _rs_pair_share(fulls):
    n = len(fulls)
    halves = [f.shape[0] // 2 for f in fulls]

    def body(*refs):
        outs = refs[n:2 * n]
        send_sems, recv_sems = refs[2 * n:]
        x, y, c = _mesh_pos()

        def copy(k, cc, to):
            rows = outs[k].at[pl.ds(cc * halves[k], halves[k])]
            return pltpu.make_async_remote_copy(
                src_ref=rows, dst_ref=rows, send_sem=send_sems.at[k], recv_sem=recv_sems.at[k],
                device_id=to, device_id_type=MESH)

        cps = [copy(k, c, (x, y, 1 - c)) for k in range(n)]
        for cp in cps:
            cp.start()
        for k in range(n):
            copy(k, 1 - c, (x, y, c)).wait_recv()
        for cp in cps:
            cp.wait_send()

    return pl.pallas_call(
        body, name="rs_pair_share",
        in_specs=[pl.BlockSpec(memory_space=pl.ANY)] * n,
        out_specs=[pl.BlockSpec(memory_space=pl.ANY)] * n,
        out_shape=[jax.ShapeDtypeStruct(f.shape, f.dtype) for f in fulls],
        input_output_aliases={k: k for k in range(n)},
        scratch_shapes=[pltpu.SemaphoreType.DMA((n,)), pltpu.SemaphoreType.DMA((n,))],
    )(*fulls)


def _pair_sum(g, recv, pos, name):
    _, rows, cols = g.shape
    h = rows // 2
    tr = _row_tile(h, 256)
    nb = h // tr

    def body(pos_ref, g_ref, r_ref, o_ref, own_ref):
        total = g_ref[0] + r_ref[0]
        o_ref[0] = total.astype(BF16)

        @pl.when(pl.program_id(1) == pos_ref[1])
        def _():
            own_ref[...] = total

    return pl.pallas_call(
        body, name=name,
        grid_spec=pltpu.PrefetchScalarGridSpec(
            num_scalar_prefetch=1, grid=(nb, N_CHIPS),
            in_specs=[pl.BlockSpec((1, tr, cols), lambda r, s, pos_ref: (s, pos_ref[0] * nb + r, 0)),
                      pl.BlockSpec((1, tr, cols), lambda r, s, pos_ref: (s, r, 0))],
            out_specs=[pl.BlockSpec((1, tr, cols), lambda r, s, pos_ref: (s, r, 0)),
                       pl.BlockSpec((tr, cols), lambda r, s, pos_ref: (r, 0))]),
        out_shape=[jax.ShapeDtypeStruct((N_CHIPS, h, cols), BF16), jax.ShapeDtypeStruct((h, cols), F32)],
        compiler_params=_params(2, 32))(pos, g, recv)


def _chip_sum(own, rb, pos, name):
    h, cols = own.shape
    tr = _row_tile(h, 256)
    nb = h // tr

    def body(pos_ref, c_ref, r_ref, o_ref):
        del pos_ref
        o_ref[...] = ((c_ref[...] + r_ref[0].astype(F32)) + r_ref[1].astype(F32)) + r_ref[2].astype(F32)

    return pl.pallas_call(
        body, name=name,
        grid_spec=pltpu.PrefetchScalarGridSpec(
            num_scalar_prefetch=1, grid=(nb,),
            in_specs=[pl.BlockSpec((tr, cols), lambda r, pos_ref: (r, 0)),
                      pl.BlockSpec((3, tr, cols), lambda r, pos_ref: (0, r, 0))],
            out_specs=pl.BlockSpec((tr, cols), lambda r, pos_ref: (pos_ref[0] * nb + r, 0))),
        out_shape=jax.ShapeDtypeStruct((2 * h, cols), F32),
        compiler_params=_params(1, 32))(pos, own, rb)


def _adamw_big(w, g, m, v, name):
    rows, cols = w.shape
    tr = _row_tile(rows, 256)

    def body(w_ref, g_ref, m_ref, v_ref, go_ref, d_ref, nm_ref, nv_ref):
        g = g_ref[...]
        go_ref[...] = g
        d_ref[...], nm_ref[...], nv_ref[...] = _adamw(w_ref[...], g, m_ref[...], v_ref[...])

    spec = pl.BlockSpec((tr, cols), lambda i: (i, 0))
    return pl.pallas_call(
        body, name=name, grid=(rows // tr,),
        in_specs=[spec] * 4, out_specs=[spec] * 4,
        out_shape=[jax.ShapeDtypeStruct((rows, cols), F32)] * 4,
        compiler_params=_params(1, 48))(w, g, m, v)


def _gather_small(loss, dfg, dlg, dgn, ddwb, dclg, dclb, dpwb, ddww, dmeta):
    d = loss.shape[1]
    w = dgn.shape[1]
    wc, mc = ddww.shape[1] // N_CHIPS, dmeta.shape[1] // N_CHIPS

    def body(loss_ref, dfg_ref, dlg_ref, dgn_ref, ddwb_ref, dclg_ref, dclb_ref, dpwb_ref, ddww_ref, dmeta_ref,
             gs_ref, gd_ref, gm_ref, send_sems, recv_sems, loc_sems):
        x, y, c = _mesh_pos()
        me = 4 * x + 2 * y + c
        gs_ref[me, 0:1, :] = loss_ref[...]
        gs_ref[me, 1:2, :] = dfg_ref[...]
        gs_ref[me, 2:3, :] = dlg_ref[...]
        gs_ref[me, 3:4, 0:w] = dgn_ref[...]
        gs_ref[me, 3:4, w:2 * w] = ddwb_ref[...]
        gs_ref[me, 4:5, 0:w] = dclg_ref[...]
        gs_ref[me, 4:5, w:2 * w] = dclb_ref[...]
        gs_ref[me, 5:6, 0:w] = dpwb_ref[...]
        gs_ref[me, 5:6, w:2 * w] = jnp.zeros((1, d - w), F32)
        gs_ref[me, 6:8, :] = jnp.zeros((2, d), F32)
        bufs = (gs_ref, gd_ref, gm_ref)

        def mine_for(k, shard):
            if k == 0:
                return gs_ref.at[me]
            ref, width = ((ddww_ref, wc), (dmeta_ref, mc))[k - 1]
            return ref.at[:, pl.ds(pl.multiple_of(shard * width, width), width)]

        def peer(j):
            return (1 - x if j & 4 else x), (1 - y if j & 2 else y), (1 - c if j & 1 else c)

        def copy(k, j, src, slot, to):
            return pltpu.make_async_remote_copy(
                src_ref=src, dst_ref=bufs[k].at[slot],
                send_sem=send_sems.at[7 * k + j - 1], recv_sem=recv_sems.at[7 * k + j - 1],
                device_id=to, device_id_type=MESH)

        own = [pltpu.make_async_copy(mine_for(k, 2 * x + y), bufs[k].at[me], loc_sems.at[k - 1]) for k in (1, 2)]
        for cp in own:
            cp.start()
        cps = []
        for k in range(3):
            for j in range(1, N_DEV):
                px, py, pc = peer(j)
                cp = copy(k, j, mine_for(k, 2 * px + py), me, (px, py, pc))
                cp.start()
                cps.append(cp)
        for k in range(3):
            for j in range(1, N_DEV):
                px, py, pc = peer(j)
                slot = 4 * px + 2 * py + pc
                copy(k, j, bufs[k].at[slot], slot, (x, y, c)).wait_recv()
        for cp in cps:
            cp.wait_send()
        for cp in own:
            cp.wait()

    vm = pl.BlockSpec(memory_space=pltpu.VMEM)
    return pl.pallas_call(
        body, name="gather_small",
        in_specs=[vm] * 10, out_specs=[vm] * 3,
        out_shape=[jax.ShapeDtypeStruct((N_DEV, 8, d), F32),
                   jax.ShapeDtypeStruct((N_DEV, ddww.shape[0], wc), F32),
                   jax.ShapeDtypeStruct((N_DEV, dmeta.shape[0], mc), F32)],
        scratch_shapes=[pltpu.SemaphoreType.DMA((21,)), pltpu.SemaphoreType.DMA((21,)),
                        pltpu.SemaphoreType.DMA((2,))],
    )(loss, dfg, dlg, dgn, ddwb, dclg, dclb, dpwb, ddww, dmeta)


def _small_update(gs, gd, gm, weights, ms, vs):
    d = gs.shape[2]
    w = d // 2
    n = len(weights)

    def body(gs_ref, gd_ref, gm_ref, *refs):
        w_refs, m_refs, v_refs = refs[:n], refs[n:2 * n], refs[2 * n:3 * n]
        loss_ref = refs[3 * n]
        g_refs = refs[3 * n + 1:4 * n + 1]
        d_refs = refs[4 * n + 1:5 * n + 1]
        nm_refs = refs[5 * n + 1:6 * n + 1]
        nv_refs = refs[6 * n + 1:7 * n + 1]

        def total(ref):
            t = ref[0]
            for dev in range(1, N_DEV):
                t = t + ref[dev]
            return t

        packed = total(gs_ref)
        loss_ref[...] = jnp.sum(packed[0:1, :], axis=1, keepdims=True) * (0.5 / d)
        grads = [packed[2:3, :], packed[1:2, :], packed[3:4, 0:w], packed[3:4, w:2 * w], packed[4:5, 0:w],
                 packed[4:5, w:2 * w], packed[5:6, 0:w], total(gd_ref), total(gm_ref)]
        for k in range(n):
            g = grads[k]
            g_refs[k][...] = g
            d_refs[k][...], nm_refs[k][...], nv_refs[k][...] = _adamw(w_refs[k][...], g, m_refs[k][...], v_refs[k][...])

    def whole(shape):
        return pl.BlockSpec(shape, lambda i: (0,) * len(shape))

    shapes = [a.shape for a in weights]
    in_specs = [whole(gs.shape), whole(gd.shape), whole(gm.shape)] + [whole(s) for s in shapes] * 3
    out_specs = [whole((1, 1))] + [whole(s) for s in shapes] * 4
    out_shape = [jax.ShapeDtypeStruct((1, 1), F32)] + [jax.ShapeDtypeStruct(s, F32) for s in shapes] * 4
    outs = pl.pallas_call(
        body, name="small_update", grid=(1,), in_specs=in_specs, out_specs=out_specs, out_shape=out_shape,
        compiler_params=_params(1, 32))(gs, gd, gm, *weights, *ms, *vs)
    loss = outs[0]
    return loss, outs[1:n + 1], outs[n + 1:2 * n + 1], outs[2 * n + 1:3 * n + 1], outs[3 * n + 1:4 * n + 1]


def kernel(x, meta_tokens, ln_g, w_in, ret_gn_g, conv_dw_w, conv_dw_b, conv_ln_g, conv_ln_b, conv_pw_w, conv_pw_b, w_out, final_g, loss_target, m_meta_tokens, m_ln_g, m_w_in, m_ret_gn_g, m_conv_dw_w, m_conv_dw_b, m_conv_ln_g, m_conv_ln_b, m_conv_pw_w, m_conv_pw_b, m_w_out, m_final_g, v_meta_tokens, v_ln_g, v_w_in, v_ret_gn_g, v_conv_dw_w, v_conv_dw_b, v_conv_ln_g, v_conv_ln_b, v_conv_pw_w, v_conv_pw_b, v_w_out, v_final_g):
    seq, d = x.shape[1], x.shape[2]
    w = ret_gn_g.shape[1]
    hd = w // RET_HEADS
    lp = CHUNK + seq
    ns = w_in.shape[2]
    mx, my, mc = lax.axis_index("x"), lax.axis_index("y"), lax.axis_index("c")
    my_s = 2 * mx + my
    s_arr = my_s.astype(jnp.int32).reshape(1)
    pos = jnp.stack([mc, my_s]).astype(jnp.int32)
    order = jnp.stack([my_s, 2 * (1 - mx) + my, 2 * mx + (1 - my), 2 * (1 - mx) + (1 - my)]).astype(jnp.int32)

    dw_pad = jnp.pad(conv_dw_w[0], ((0, HALO - CONV_K), (0, 0)))
    dw4, meta4 = _gather_shards([dw_pad, meta_tokens])
    dw_full = dw4.transpose(1, 0, 2).reshape(HALO, w)
    meta_full = meta4.transpose(1, 0, 2).reshape(N_META, d)

    lead = jnp.concatenate([jnp.zeros((LEAD, d), F32), meta_full], axis=0)
    zero_lead = jnp.zeros((CHUNK, d), F32)
    consts = _ret_consts()
    rope = _rope_tables(lp // CHUNK, hd // 2)
    fg2 = final_g.reshape(1, d)

    proj, r1, hn, w4, pw4, wo4 = _in_proj_gather(
        order, x[0], lead, ln_g, _cast_into_gathered(w_in[0], s_arr, "cast_w_in"),
        _cast_into_gathered(conv_pw_w[0], s_arr, "cast_pw_w"), _cast_into_gathered(w_out[0], s_arr, "cast_w_out"))
    pw_full = pw4.reshape(w, w)
    wo_full = wo4.reshape(2 * w, d)
    y_ret, ssave, ret_ops, ret_scores, ret_out = _ret_fwd(proj, rope, ret_gn_g, consts)
    y_conv, u1 = _conv_fwd(proj, dw_full, conv_dw_b, conv_ln_g, conv_ln_b, pw_full, conv_pw_b)
    dh2, dy, loss_l, dfg = _out_proj_loss(y_ret, y_conv, x[0], lead, loss_target[0], zero_lead, wo_full, fg2)

    g_wo = _dw_out(y_ret, y_conv, dh2)
    dproj, dgn = _ret_bwd(proj, dy, ssave, ret_ops, ret_scores, ret_out, rope, ret_gn_g, consts)
    dproj, du1, g_pw, dpwb, dclg, dclb = _conv_bwd_pw(dy, proj, u1, conv_ln_g, conv_ln_b, pw_full, conv_pw_b, dproj)
    dproj, ddww, ddwb = _conv_bwd_dw(du1, proj, dw_full, dproj)
    g_wo4 = g_wo.reshape(N_CHIPS, (2 * w) // N_CHIPS, d)
    g_pw4 = g_pw.reshape(N_CHIPS, w // N_CHIPS, w)
    g_win, recv = _dw_in(hn, dproj, ns, [g_wo4, g_pw4])
    gs = [g_win, g_wo4, g_pw4]
    names = ("w_in", "w_out", "pw_w")
    sums = [_pair_sum(g, r, pos, "pair_sum_" + nm) for g, r, nm in zip(gs, recv, names)]
    dh, dlg, rb = _in_proj_bwd(dproj, w4, x[0], lead, r1, dh2, ln_g, [cs_ for cs_, _ in sums])
    grad_x = dh[CHUNK:][None]
    dmeta = dh[LEAD:CHUNK]
    fulls = [_chip_sum(own, r, pos, "chip_sum_" + nm) for (_, own), r, nm in zip(sums, rb, names)]
    grad_w_in, grad_w_out, grad_pw = _rs_pair_share(fulls)
    grad_w_in, d_win, nm_win, nv_win = _adamw_big(w_in[0], grad_w_in, m_w_in[0], v_w_in[0], "adamw_w_in")
    grad_w_out, d_wo, nm_wo, nv_wo = _adamw_big(w_out[0], grad_w_out, m_w_out[0], v_w_out[0], "adamw_w_out")
    grad_pw, d_pw, nm_pw, nv_pw = _adamw_big(conv_pw_w[0], grad_pw, m_conv_pw_w[0], v_conv_pw_w[0], "adamw_pw_w")

    gsm, gdm, gmm = _gather_small(loss_l, dfg, dlg, dgn, ddwb, dclg, dclb, dpwb, ddww, dmeta)

    def pad_dw(a):
        return jnp.pad(a[0], ((0, HALO - CONV_K), (0, 0)))

    small_w = [ln_g, fg2, ret_gn_g, conv_dw_b, conv_ln_g, conv_ln_b, conv_pw_b, dw_pad, meta_tokens]
    small_m = [m_ln_g, m_final_g.reshape(1, d), m_ret_gn_g, m_conv_dw_b, m_conv_ln_g, m_conv_ln_b, m_conv_pw_b,
               pad_dw(m_conv_dw_w), m_meta_tokens]
    small_v = [v_ln_g, v_final_g.reshape(1, d), v_ret_gn_g, v_conv_dw_b, v_conv_ln_g, v_conv_ln_b, v_conv_pw_b,
               pad_dw(v_conv_dw_w), v_meta_tokens]
    loss, sg, sd, snm, snv = _small_update(gsm, gdm, gmm, small_w, small_m, small_v)

    def assemble(small, big_in, big_pw, big_out):
        ln, fg, gn, dwb, clg, clb, pwb, dww, meta = small
        return (meta, ln, big_in[None], gn, dww[:CONV_K][None], dwb, clg, clb, big_pw[None], pwb, big_out[None],
                fg.reshape(d))

    return (loss.reshape(()), grad_x,
            *assemble(sg, grad_w_in, grad_pw, grad_w_out),
            *assemble(sd, d_win, d_pw, d_wo),
            *assemble(snm, nm_win, nm_pw, nm_wo),
            *assemble(snv, nv_win, nv_pw, nv_wo))
```

```python
import functools

import jax
import jax.numpy as jnp
from jax import lax
from jax.experimental import pallas as pl
from jax.experimental.pallas import tpu as pltpu

F32 = jnp.float32
BF16 = jnp.bfloat16
MESH = pl.DeviceIdType.MESH

N_META = 16
CHUNK = 128
LEAD = (-N_META) % CHUNK
RET_HEADS = 4
CONV_K = 31
HALO = 32
ROPE_BASE = 10000.0
EPS = 1e-6
N_CHIPS = 4
N_DEV = 8

ADAM_LR = 0.001
ADAM_B1 = 0.9
ADAM_B2 = 0.999
ADAM_EPS = 1e-08
ADAM_WD = 0.01
ADAM_STEP = 10

MIB = 2 ** 20


def _params(n_grid_axes, vmem_mib):
    return pltpu.CompilerParams(dimension_semantics=("arbitrary",) * n_grid_axes,
                                vmem_limit_bytes=vmem_mib * MIB)


def _in_hbm(*arrays):
    return tuple(pltpu.with_memory_space_constraint(a, pltpu.HBM) for a in arrays)


def _row_tile(n, pref):
    for t in (1664, 1280, 1024, 640, 512, 384, 320, 256, 128, 64, 32, 16, 8):
        if t <= pref and n % t == 0:
            return t
    raise ValueError(f"no row tile for {n}")


def _dot(a, b):
    return jnp.dot(a, b, preferred_element_type=F32)


def _dot_nt(a, b):
    return lax.dot_general(a, b, (((1,), (1,)), ((), ())), preferred_element_type=F32)


def _dot_tn(a, b):
    return lax.dot_general(a, b, (((0,), (0,)), ((), ())), preferred_element_type=F32)


def _sigmoid(x):
    return jax.nn.sigmoid(x)


def _dsilu(x, s):
    return s * (1.0 + x * (1.0 - s))


def _mean(x):
    return jnp.mean(x, axis=-1, keepdims=True)


def _colsum(x):
    return jnp.sum(x, axis=0, keepdims=True)


def _rope(x, cos, sin):
    half = x.shape[-1] // 2
    x1, x2 = x[:, :half], x[:, half:]
    return jnp.concatenate([x1 * cos - x2 * sin, x1 * sin + x2 * cos], axis=-1)


def _rope_t(d, cos, sin):
    half = d.shape[-1] // 2
    d1, d2 = d[:, :half], d[:, half:]
    return jnp.concatenate([d1 * cos + d2 * sin, d2 * cos - d1 * sin], axis=-1)


def _adamw(w, g, m, v):
    m = ADAM_B1 * m + (1.0 - ADAM_B1) * g
    v = ADAM_B2 * v + (1.0 - ADAM_B2) * (g * g)
    m_hat = m / (1.0 - ADAM_B1 ** ADAM_STEP)
    v_hat = v / (1.0 - ADAM_B2 ** ADAM_STEP)
    delta = -ADAM_LR * (m_hat / (jnp.sqrt(v_hat) + ADAM_EPS) + ADAM_WD * w)
    return delta, m, v


def _ret_consts():
    h = jnp.arange(RET_HEADS, dtype=F32)
    log_g = jnp.log(1.0 - jnp.exp2(-5.0 - h))
    idx = jnp.arange(CHUNK, dtype=F32)
    rel = idx[:, None] - idx[None, :]
    dmask = jnp.where(rel[None] >= 0, jnp.exp(jnp.maximum(rel, 0.0)[None] * log_g[:, None, None]), 0.0)
    qd = jnp.exp((idx[None, :] + 1.0) * log_g[:, None])[:, :, None]
    kd = jnp.exp((CHUNK - 1.0 - idx[None, :]) * log_g[:, None])[:, :, None]
    cd = jnp.exp(CHUNK * log_g)[:, None, None]
    return dmask, qd, kd, cd


def _rope_tables(nch, half):
    inv_freq = ROPE_BASE ** (-jnp.arange(half, dtype=F32) / half)
    start = (jnp.arange(nch, dtype=F32) * float(CHUNK) - float(LEAD))[:, None] * inv_freq[None, :]
    within = jnp.arange(CHUNK, dtype=F32)[:, None] * inv_freq[None, :]
    return jnp.cos(start)[:, None, :], jnp.sin(start)[:, None, :], jnp.cos(within), jnp.sin(within)


def _rope_of_chunk(ca, sa, cb, sb):
    return ca * cb - sa * sb, sa * cb + ca * sb


def _padded_tile_stream(x_hbm, lead_hbm, buf, sems, tm):
    def start_first(slot):
        pltpu.make_async_copy(lead_hbm, buf.at[slot, pl.ds(0, CHUNK)], sems.at[slot]).start()
        pltpu.make_async_copy(x_hbm.at[pl.ds(0, tm - CHUNK)], buf.at[slot, pl.ds(CHUNK, tm - CHUNK)],
                              sems.at[slot]).start()

    def start(slot, tile):
        pltpu.make_async_copy(x_hbm.at[pl.ds(tile * tm - CHUNK, tm)], buf.at[slot], sems.at[slot]).start()

    def wait(slot):
        pltpu.make_async_copy(x_hbm.at[pl.ds(0, tm)], buf.at[slot], sems.at[slot]).wait()

    return start_first, start, wait


def _cast_into_gathered(a, s_arr, name):
    rows, cols = a.shape
    tr = _row_tile(rows, 256)

    def body(s_ref, a_ref, o_ref):
        del s_ref
        o_ref[0] = a_ref[...].astype(BF16)

    return pl.pallas_call(
        body, name=name,
        grid_spec=pltpu.PrefetchScalarGridSpec(
            num_scalar_prefetch=1, grid=(rows // tr,),
            in_specs=[pl.BlockSpec((tr, cols), lambda i, s_ref: (i, 0))],
            out_specs=pl.BlockSpec((1, tr, cols), lambda i, s_ref: (s_ref[0], i, 0))),
        out_shape=pltpu.HBM((N_CHIPS, rows, cols), BF16),
        compiler_params=_params(1, 32))(s_arr, *_in_hbm(a))


def _in_proj_gather(order, x, lead, ln_g, w4, pw4, wo4):
    seq, d = x.shape
    lp = CHUNK + seq
    ns = w4.shape[2]
    tm = _row_tile(lp, 640)
    nt = lp // tm
    assert nt >= 2, "the hn write-back of a row tile is waited for one step later, before any pass re-reads it"
    land_step, load_step = max(nt - 3, 0), max(nt - 2, 0)
    gathered = (w4, pw4, wo4)
    halves = [a.shape[1] // 2 for a in gathered]
    n = len(gathered)

    def body(order_ref, x_hbm, lead_hbm, g_ref, w_in, pw_in, wo_in, proj_ref, r_ref, hn_hbm, w_out, pw_out,
             wo_out, wbuf, hnbuf, hbuf, send_sems, recv_sems, hn_out_sems, hn_in_sems, h_sems, w_sem):
        del order_ref, w_in, pw_in, wo_in
        t, i = pl.program_id(0), pl.program_id(1)
        slot = (t * nt + i) % 2
        h_first, h_start, h_wait = _padded_tile_stream(x_hbm, lead_hbm, hbuf, h_sems, tm)
        x, y, c = _mesh_pos()
        me, sibling = (x, y, c), (x, y, 1 - c)
        my_s = 2 * x + y
        chips = _other_chips(x, y)
        outs = (w_out, pw_out, wo_out)

        def half(k, s, cc):
            return outs[k].at[s, pl.ds(cc * halves[k], halves[k])]

        def rcopy(k, j, rows, to):
            return pltpu.make_async_remote_copy(
                src_ref=rows, dst_ref=rows, send_sem=send_sems.at[6 * k + j], recv_sem=recv_sems.at[6 * k + j],
                device_id=to, device_id_type=MESH)

        def send(k, j):
            return rcopy(k, j, half(k, my_s, c), (*chips[j], c))

        def shard_of(j):
            return 2 * chips[j][0] + chips[j][1]

        def forward(k, j):
            return rcopy(k, 3 + j, half(k, shard_of(j), c), sibling)

        def land(k, j):
            rcopy(k, j, half(k, shard_of(j), c), me).wait_recv()
            forward(k, j).start()

        def landed_from_sibling(k, j):
            rcopy(k, 3 + j, half(k, shard_of(j), 1 - c), me).wait_recv()

        def load_w(s, wslot):
            return pltpu.make_async_copy(w_out.at[s], wbuf.at[wslot], w_sem)

        def hn_out(sl, row_tile):
            return pltpu.make_async_copy(hnbuf.at[sl], hn_hbm.at[pl.ds(row_tile * tm, tm)], hn_out_sems.at[sl])

        def hn_in(sl, row_tile):
            return pltpu.make_async_copy(hn_hbm.at[pl.ds(row_tile * tm, tm)], hnbuf.at[sl], hn_in_sems.at[sl])

        @pl.when((t == 0) & (i == 0))
        def _():
            for j in range(2):
                send(0, j).start()
            load_w(my_s, 0).start()
            load_w(my_s, 0).wait()

        @pl.when((t == 1) & (i == 0))
        def _():
            send(0, 2).start()
            for k in range(1, n):
                for jj in range(3):
                    send(k, jj).start()

        for j in range(3):
            @pl.when((t == j) & (i == land_step))
            def _(j=j):
                land(0, j)

            @pl.when((t == j) & (i == load_step))
            def _(j=j):
                landed_from_sibling(0, j)
                load_w(shard_of(j), (j + 1) % 2).start()

            @pl.when((t == j + 1) & (i == 0))
            def _(j=j):
                load_w(shard_of(j), (j + 1) % 2).wait()

        @pl.when((t == 0) & (i == 0))
        def _():
            h_first(0)

        @pl.when((t == 0) & (i + 1 < nt))
        def _():
            h_start((i + 1) % 2, i + 1)

        @pl.when(t == 0)
        def _():
            h_wait(i % 2)
            h = hbuf[i % 2]
            r = lax.rsqrt(_mean(h * h) + EPS)
            hnbuf[slot] = ((h * r) * g_ref[...]).astype(BF16)
            r_ref[...] = r
            hn_out(slot, i).start()

        @pl.when(t > 0)
        def _():
            hn_in(slot, i).wait()

        @pl.when(((t == 0) & (i > 0)) | ((t == 1) & (i == 0)))
        def _():
            hn_out(1 - slot, jnp.where(i > 0, i - 1, nt - 1)).wait()

        last = (t == N_CHIPS - 1) & (i == nt - 1)

        @pl.when(((t > 0) | (i == nt - 1)) & jnp.logical_not(last))
        def _():
            hn_in(1 - slot, jnp.where(i == nt - 1, 0, i + 1)).start()

        proj_ref[...] = _dot(hnbuf[slot], wbuf[t % 2])

        @pl.when(last)
        def _():
            for k in range(1, n):
                for j in range(3):
                    land(k, j)
            for k in range(1, n):
                for j in range(3):
                    landed_from_sibling(k, j)
            for k in range(n):
                for j in range(3):
                    send(k, j).wait_send()
                    forward(k, j).wait_send()

    def frozen(t, i):
        return jnp.where(t == 0, i, nt - 1)

    any_spec = pl.BlockSpec(memory_space=pl.ANY)
    return pl.pallas_call(
        body, name="in_proj_gather",
        grid_spec=pltpu.PrefetchScalarGridSpec(
            num_scalar_prefetch=1, grid=(N_CHIPS, nt),
            in_specs=[any_spec, any_spec, pl.BlockSpec((1, d), lambda t, i, o: (0, 0)),
                      any_spec, any_spec, any_spec],
            out_specs=[pl.BlockSpec((tm, ns), lambda t, i, o: (i, o[t])),
                       pl.BlockSpec((tm, 1), lambda t, i, o: (frozen(t, i), 0)),
                       any_spec, any_spec, any_spec, any_spec],
            scratch_shapes=[pltpu.VMEM((2, d, ns), BF16), pltpu.VMEM((2, tm, d), BF16), pltpu.VMEM((2, tm, d), F32),
                            pltpu.SemaphoreType.DMA((6 * n,)), pltpu.SemaphoreType.DMA((6 * n,)),
                            pltpu.SemaphoreType.DMA((2,)), pltpu.SemaphoreType.DMA((2,)),
                            pltpu.SemaphoreType.DMA((2,)), pltpu.SemaphoreType.DMA]),
        out_shape=[jax.ShapeDtypeStruct((lp, N_CHIPS * ns), F32),
                   jax.ShapeDtypeStruct((lp, 1), F32),
                   jax.ShapeDtypeStruct((lp, d), BF16)]
                  + [jax.ShapeDtypeStruct(a.shape, a.dtype) for a in gathered],
        input_output_aliases={4: 3, 5: 4, 6: 5},
        compiler_params=_params(2, 48))(order, x, lead, ln_g, *_in_hbm(w4, pw4, wo4))


def _chunks_per_step(nch):
    return next(n for n in (5, 4, 3, 2, 1) if nch % n == 0)


def _ret_fwd(proj, rope, gn_g, consts):
    lp = proj.shape[0]
    w = gn_g.shape[1]
    hd = w // RET_HEADS
    nch = lp // CHUNK
    cps = _chunks_per_step(nch)
    tr = cps * CHUNK
    dmask, qd, kd, cd = consts

    def body(q_ref, k_ref, v_ref, g_ref, ca_ref, sa_ref, cb_ref, sb_ref, gn_ref, m_ref, qd_ref, kd_ref, cd_ref,
             y_ref, ssave_ref, ops_ref, sc_ref, out_ref, s_scr):
        @pl.when(pl.program_id(0) == 0)
        def _():
            s_scr[...] = jnp.zeros_like(s_scr)

        def chunk(cc, carry):
            rows = pl.ds(pl.multiple_of(cc * CHUNK, CHUNK), CHUNK)
            cos_t, sin_t = _rope_of_chunk(ca_ref[cc], sa_ref[cc], cb_ref[...], sb_ref[...])
            for h in range(RET_HEADS):
                sl = slice(h * hd, (h + 1) * hd)
                qr = _rope(q_ref[rows, sl], cos_t, sin_t)
                kr = _rope(k_ref[rows, sl], cos_t, sin_t) * (hd ** -0.5)
                qb, kb = qr.astype(BF16), kr.astype(BF16)
                qdq, kdk = (qr * qd_ref[h]).astype(BF16), (kr * kd_ref[h]).astype(BF16)
                vb = v_ref[rows, sl].astype(BF16)
                for k, operand in enumerate((qb, kb, qdq, kdk, vb)):
                    ops_ref[k, rows, sl] = operand
                scb = (_dot_nt(qb, kb) * m_ref[h]).astype(BF16)
                sc_ref[cc, h] = scb
                state = s_scr[h]
                sb = state.astype(BF16)
                ssave_ref[cc, h] = sb
                out = _dot(scb, vb) + _dot(qdq, sb)
                out_ref[rows, sl] = out
                s_scr[h] = state * cd_ref[h] + _dot_tn(kdk, vb)
                dev = out - _mean(out)
                yn = dev * lax.rsqrt(_mean(dev * dev) + EPS)
                g = g_ref[rows, sl]
                y_ref[rows, sl] = ((yn * gn_ref[:, sl]) * (g * _sigmoid(g))).astype(BF16)
            return carry

        lax.fori_loop(0, cps, chunk, 0)

    def col(j):
        return pl.BlockSpec((tr, w), lambda i: (i, j))

    def whole(a):
        return pl.BlockSpec(a.shape, lambda i: (0,) * a.ndim)

    return pl.pallas_call(
        body, name="ret_fwd", grid=(nch // cps,),
        in_specs=[col(0), col(1), col(2), col(3),
                  pl.BlockSpec((cps, 1, hd // 2), lambda i: (i, 0, 0)),
                  pl.BlockSpec((cps, 1, hd // 2), lambda i: (i, 0, 0)),
                  whole(rope[2]), whole(rope[3]),
                  whole(gn_g), whole(dmask), whole(qd), whole(kd), whole(cd)],
        out_specs=[pl.BlockSpec((tr, w), lambda i: (i, 0)),
                   pl.BlockSpec((cps, RET_HEADS, hd, hd), lambda i: (i, 0, 0, 0)),
                   pl.BlockSpec((5, tr, w), lambda i: (0, i, 0)),
                   pl.BlockSpec((cps, RET_HEADS, CHUNK, CHUNK), lambda i: (i, 0, 0, 0)),
                   pl.BlockSpec((tr, w), lambda i: (i, 0))],
        out_shape=[jax.ShapeDtypeStruct((lp, w), BF16),
                   jax.ShapeDtypeStruct((nch, RET_HEADS, hd, hd), BF16),
                   jax.ShapeDtypeStruct((5, lp, w), BF16),
                   jax.ShapeDtypeStruct((nch, RET_HEADS, CHUNK, CHUNK), BF16),
                   jax.ShapeDtypeStruct((lp, w), F32)],
        scratch_shapes=[pltpu.VMEM((RET_HEADS, hd, hd), F32)],
        compiler_params=_params(1, 56))(proj, proj, proj, proj, *rope, gn_g, dmask, qd, kd, cd)


def _tap_groups(start, flip):
    groups = {}
    for j in range(CONV_K):
        o = start + (CONV_K - 1 - j if flip else j)
        groups.setdefault(o % 8, []).append((o // 8, j))
    return groups


def _shift_up(win, s):
    return win if s == 0 else pltpu.roll(win, win.shape[0] - s, axis=0)


def _dw_taps(src_ref, w_ref, dst_ref, bias, *, rows, start, flip, rb):
    cw = dst_ref.shape[1]
    lb = min(128, cw)
    groups = _tap_groups(start, flip)

    def rb_body(r, carry):
        base = pl.multiple_of(r * rb, rb)
        for cb in range(cw // lb):
            ls = slice(cb * lb, (cb + 1) * lb)
            win = src_ref[pl.ds(base, rb + HALO), ls]
            acc = jnp.zeros((rb, lb), F32) if bias is None else jnp.broadcast_to(bias[:, ls], (rb, lb))
            for s, taps in groups.items():
                ws = _shift_up(win, s)
                for a, j in taps:
                    acc = acc + ws[8 * a:8 * a + rb, :] * w_ref[j:j + 1, ls]
            dst_ref[pl.ds(base, rb), ls] = acc
        return carry

    lax.fori_loop(0, rows // rb, rb_body, 0)


def _conv_fwd(proj, dw_w, dw_b, cln_g, cln_b, pw_w, pw_b):
    lp = proj.shape[0]
    cw = dw_b.shape[1]
    tm = _row_tile(lp, 640)
    rb = _row_tile(tm, 64)

    def body(a_ref, b_ref, gc_ref, w_ref, wb_ref, lg_ref, lb_ref, pw_ref, pb_ref, y_ref, u1_ref, buf):
        @pl.when(pl.program_id(0) == 0)
        def _():
            buf[0:HALO, :] = jnp.zeros((HALO, cw), F32)

        buf[HALO:HALO + tm, :] = a_ref[...] * _sigmoid(b_ref[...])
        _dw_taps(buf, w_ref, u1_ref, wb_ref[...], rows=tm, start=HALO - (CONV_K - 1), flip=False, rb=rb)
        buf[0:HALO, :] = buf[tm:tm + HALO, :]
        u1 = u1_ref[...]
        dev = u1 - _mean(u1)
        z = dev * lax.rsqrt(_mean(dev * dev) + EPS) * lg_ref[...] + lb_ref[...]
        u3 = (z * _sigmoid(z)).astype(BF16)
        u4 = _dot(u3, pw_ref[...]) + pb_ref[...]
        gc = gc_ref[...]
        y_ref[...] = (u4 * (gc * _sigmoid(gc))).astype(BF16)

    def col(j):
        return pl.BlockSpec((tm, cw), lambda i: (i, j))

    def whole(a):
        return pl.BlockSpec(a.shape, lambda i: (0,) * a.ndim)

    return pl.pallas_call(
        body, name="conv_fwd", grid=(lp // tm,),
        in_specs=[col(4), col(5), col(6), whole(dw_w), whole(dw_b), whole(cln_g), whole(cln_b),
                  whole(pw_w), whole(pw_b)],
        out_specs=[pl.BlockSpec((tm, cw), lambda i: (i, 0)), pl.BlockSpec((tm, cw), lambda i: (i, 0))],
        out_shape=[jax.ShapeDtypeStruct((lp, cw), BF16), jax.ShapeDtypeStruct((lp, cw), F32)],
        scratch_shapes=[pltpu.VMEM((tm + HALO, cw), F32)],
        compiler_params=_params(1, 48))(proj, proj, proj, dw_w, dw_b, cln_g, cln_b, pw_w, pw_b)


def _out_proj_loss(yr, yc, x, lead, tgt, zero_lead, w_out, final_g):
    seq, d = x.shape
    lp = CHUNK + seq
    w = yr.shape[1]
    tm = _row_tile(lp, 320)
    nt = lp // tm
    assert tm > CHUNK

    def body(yr_ref, yc_ref, x_hbm, lead_hbm, t_hbm, zlead_hbm, w_ref, fg_ref, dh2_ref, dy_ref, loss_ref, dfg_ref,
             hbuf, tbuf, hsems, tsems):
        i = pl.program_id(0)
        slot = i % 2
        streams = (_padded_tile_stream(x_hbm, lead_hbm, hbuf, hsems, tm),
                   _padded_tile_stream(t_hbm, zlead_hbm, tbuf, tsems, tm))

        @pl.when(i == 0)
        def _():
            loss_ref[...] = jnp.zeros_like(loss_ref)
            dfg_ref[...] = jnp.zeros_like(dfg_ref)
            for start_first, _, _ in streams:
                start_first(0)

        @pl.when(i + 1 < nt)
        def _():
            for _, start, _ in streams:
                start(1 - slot, i + 1)

        for _, _, wait in streams:
            wait(slot)

        h2 = hbuf[slot] + (_dot(yr_ref[...], w_ref[0:w, :]) + _dot(yc_ref[...], w_ref[w:2 * w, :]))
        r2 = lax.rsqrt(_mean(h2 * h2) + EPS)
        h2n = h2 * r2
        fg = fg_ref[...]
        rows = i * tm + lax.broadcasted_iota(jnp.int32, (tm, 1), 0)
        err = jnp.where(rows >= CHUNK, h2n * fg - tbuf[slot], 0.0)
        loss_ref[...] += _colsum(err * err)
        dout = err * (1.0 / d)
        dfg_ref[...] += _colsum(dout * h2n)
        dz = dout * fg
        dh2 = r2 * (dz - h2n * _mean(dz * h2n))
        dh2_ref[...] = dh2
        db = dh2.astype(BF16)
        dy_ref[:, 0:w] = _dot_nt(db, w_ref[0:w, :])
        dy_ref[:, w:2 * w] = _dot_nt(db, w_ref[w:2 * w, :])

    def row(cols):
        return pl.BlockSpec((tm, cols), lambda i: (i, 0))

    any_spec = pl.BlockSpec(memory_space=pl.ANY)
    return pl.pallas_call(
        body, name="out_proj_loss", grid=(nt,),
        in_specs=[row(w), row(w), any_spec, any_spec, any_spec, any_spec,
                  pl.BlockSpec(memory_space=pltpu.VMEM),
                  pl.BlockSpec((1, d), lambda i: (0, 0))],
        out_specs=[row(d), row(2 * w), pl.BlockSpec((1, d), lambda i: (0, 0)),
                   pl.BlockSpec((1, d), lambda i: (0, 0))],
        out_shape=[jax.ShapeDtypeStruct((lp, d), F32), jax.ShapeDtypeStruct((lp, 2 * w), F32),
                   jax.ShapeDtypeStruct((1, d), F32), jax.ShapeDtypeStruct((1, d), F32)],
        scratch_shapes=[pltpu.VMEM((2, tm, d), F32), pltpu.VMEM((2, tm, d), F32),
                        pltpu.SemaphoreType.DMA((2,)), pltpu.SemaphoreType.DMA((2,))],
        compiler_params=_params(1, 56))(yr, yc, x, lead, tgt, zero_lead, w_out, final_g)


def _dw_out(yr, yc, dh2):
    lp, d = dh2.shape
    w = yr.shape[1]
    tm = _row_tile(lp, 1664)
    nb = 2
    dn = d // nb

    def body(yr_ref, yc_ref, d_ref, o_ref):
        @pl.when(pl.program_id(1) == 0)
        def _():
            o_ref[...] = jnp.zeros_like(o_ref)

        db = d_ref[...].astype(BF16)
        o_ref[0:w, :] += _dot_tn(yr_ref[...], db)
        o_ref[w:2 * w, :] += _dot_tn(yc_ref[...], db)

    return pl.pallas_call(
        body, name="dw_out", grid=(nb, lp // tm),
        in_specs=[pl.BlockSpec((tm, w), lambda n, i: (i, 0)),
                  pl.BlockSpec((tm, w), lambda n, i: (i, 0)),
                  pl.BlockSpec((tm, dn), lambda n, i: (i, n))],
        out_specs=pl.BlockSpec((2 * w, dn), lambda n, i: (0, n)),
        out_shape=jax.ShapeDtypeStruct((2 * w, d), F32),
        compiler_params=_params(2, 52))(yr, yc, dh2)


def _ret_bwd(proj, dy, ssave, ops, scores, out_pre, rope, gn_g, consts):
    lp = proj.shape[0]
    w = gn_g.shape[1]
    hd = w // RET_HEADS
    nch = lp // CHUNK
    cps = _chunks_per_step(nch)
    tr = cps * CHUNK
    dmask, qd, kd, cd = consts

    def body(g_ref, dy_ref, ss_ref, ops_ref, sc_ref, out_ref, ca_ref, sa_ref, cb_ref, sb_ref, gn_ref, m_ref, qd_ref,
             kd_ref, cd_ref, dp_ref, dgn_ref, ds_scr):
        @pl.when(pl.program_id(0) == 0)
        def _():
            ds_scr[...] = jnp.zeros_like(ds_scr)
            dgn_ref[...] = jnp.zeros_like(dgn_ref)

        def chunk(n, carry):
            cc = cps - 1 - n
            rows = pl.ds(pl.multiple_of(cc * CHUNK, CHUNK), CHUNK)
            cos_t, sin_t = _rope_of_chunk(ca_ref[cc], sa_ref[cc], cb_ref[...], sb_ref[...])
            for h in range(RET_HEADS):
                sl = slice(h * hd, (h + 1) * hd)
                qb, kb, qdq, kdk, vb = (ops_ref[k, rows, sl] for k in range(5))
                sb = ss_ref[cc, h]
                scb = sc_ref[cc, h]
                mask = m_ref[h]
                qdec, kdec = qd_ref[h], kd_ref[h]
                out = out_ref[rows, sl]
                dev = out - _mean(out)
                rstd = lax.rsqrt(_mean(dev * dev) + EPS)
                yn = dev * rstd
                g = g_ref[rows, sl]
                sg = _sigmoid(g)
                gng = gn_ref[:, sl]
                dyv = dy_ref[rows, sl]
                dgr = dyv * (yn * gng) * _dsilu(g, sg)
                silu_g = g * sg
                dgn_ref[:, sl] += _colsum(dyv * yn * silu_g)
                dyn = dyv * gng * silu_g
                dout = rstd * (dyn - _mean(dyn) - yn * _mean(dyn * yn))
                dob = dout.astype(BF16)
                by_v_and_s = _dot_nt(dob, jnp.concatenate([vb, sb], axis=0))
                dscb = (by_v_and_s[:, 0:CHUNK] * mask).astype(BF16)
                dstate = ds_scr[h]
                dsb = dstate.astype(BF16)
                dq = _dot(dscb, kb) + by_v_and_s[:, CHUNK:] * qdec
                dk = _dot_tn(dscb, qb) + _dot_nt(vb, dsb) * kdec
                onto_do = _dot_tn(jnp.concatenate([scb, qdq], axis=1), dob)
                dv = onto_do[0:CHUNK] + _dot(kdk, dsb)
                ds_scr[h] = dstate * cd_ref[h] + onto_do[CHUNK:]
                dp_ref[rows, 0 * w + h * hd:0 * w + (h + 1) * hd] = _rope_t(dq, cos_t, sin_t).astype(BF16)
                dp_ref[rows, 1 * w + h * hd:1 * w + (h + 1) * hd] = (
                    _rope_t(dk, cos_t, sin_t) * (hd ** -0.5)).astype(BF16)
                dp_ref[rows, 2 * w + h * hd:2 * w + (h + 1) * hd] = dv.astype(BF16)
                dp_ref[rows, 3 * w + h * hd:3 * w + (h + 1) * hd] = dgr.astype(BF16)
            return carry

        lax.fori_loop(0, cps, chunk, 0)

    def rev(i):
        return nch // cps - 1 - i

    def col(j):
        return pl.BlockSpec((tr, w), lambda i: (rev(i), j))

    def whole(a):
        return pl.BlockSpec(a.shape, lambda i: (0,) * a.ndim)

    return pl.pallas_call(
        body, name="ret_bwd", grid=(nch // cps,),
        in_specs=[col(3),
                  pl.BlockSpec((tr, w), lambda i: (rev(i), 0)),
                  pl.BlockSpec((cps, RET_HEADS, hd, hd), lambda i: (rev(i), 0, 0, 0)),
                  pl.BlockSpec((5, tr, w), lambda i: (0, rev(i), 0)),
                  pl.BlockSpec((cps, RET_HEADS, CHUNK, CHUNK), lambda i: (rev(i), 0, 0, 0)),
                  pl.BlockSpec((tr, w), lambda i: (rev(i), 0)),
                  pl.BlockSpec((cps, 1, hd // 2), lambda i: (rev(i), 0, 0)),
                  pl.BlockSpec((cps, 1, hd // 2), lambda i: (rev(i), 0, 0)),
                  whole(rope[2]), whole(rope[3]),
                  whole(gn_g), whole(dmask), whole(qd), whole(kd), whole(cd)],
        out_specs=[pl.BlockSpec((tr, 4 * w), lambda i: (rev(i), 0)),
                   pl.BlockSpec((1, w), lambda i: (0, 0))],
        out_shape=[jax.ShapeDtypeStruct((lp, 7 * w), BF16), jax.ShapeDtypeStruct((1, w), F32)],
        scratch_shapes=[pltpu.VMEM((RET_HEADS, hd, hd), F32)],
        compiler_params=_params(1, 56))(proj, dy, ssave, ops, scores, out_pre, *rope, gn_g, dmask, qd, kd, cd)


def _conv_bwd_pw(dy, proj, u1, cln_g, cln_b, pw_w, pw_b, dproj):
    lp, cw = u1.shape
    tm = _row_tile(lp, 320)

    def body(dy_ref, gc_ref, u1_ref, lg_ref, lb_ref, pw_ref, pb_ref, dp_in, dp_ref, du1_ref, dpw_ref,
             dpb_ref, dlg_ref, dlb_ref):
        del dp_in

        @pl.when(pl.program_id(0) == 0)
        def _():
            dpw_ref[...] = jnp.zeros_like(dpw_ref)
            dpb_ref[...] = jnp.zeros_like(dpb_ref)
            dlg_ref[...] = jnp.zeros_like(dlg_ref)
            dlb_ref[...] = jnp.zeros_like(dlb_ref)

        u1 = u1_ref[...]
        dev = u1 - _mean(u1)
        rstd = lax.rsqrt(_mean(dev * dev) + EPS)
        u1n = dev * rstd
        lg = lg_ref[...]
        z = u1n * lg + lb_ref[...]
        sz = _sigmoid(z)
        u3b = (z * sz).astype(BF16)
        u4 = _dot(u3b, pw_ref[...]) + pb_ref[...]
        gc = gc_ref[...]
        sgc = _sigmoid(gc)
        dyc = dy_ref[...]
        du4 = dyc * (gc * sgc)
        dp_ref[...] = (dyc * u4 * _dsilu(gc, sgc)).astype(BF16)
        du4b = du4.astype(BF16)
        dpb_ref[...] += _colsum(du4)
        dpw_ref[...] += _dot_tn(u3b, du4b)
        dz = _dot_nt(du4b, pw_ref[...]) * _dsilu(z, sz)
        dlg_ref[...] += _colsum(dz * u1n)
        dlb_ref[...] += _colsum(dz)
        dn = dz * lg
        du1_ref[...] = rstd * (dn - _mean(dn) - u1n * _mean(dn * u1n))

    def row(j):
        return pl.BlockSpec((tm, cw), lambda i: (i, j))

    def whole(a):
        return pl.BlockSpec(a.shape, lambda i: (0,) * a.ndim)

    def acc(r):
        return pl.BlockSpec((r, cw), lambda i: (0, 0))

    return pl.pallas_call(
        body, name="conv_bwd_pw", grid=(lp // tm,),
        in_specs=[row(1), row(6), row(0), whole(cln_g), whole(cln_b), whole(pw_w), whole(pw_b),
                  pl.BlockSpec(memory_space=pl.ANY)],
        out_specs=[row(6), row(0), acc(cw), acc(1), acc(1), acc(1)],
        out_shape=[jax.ShapeDtypeStruct(dproj.shape, dproj.dtype), jax.ShapeDtypeStruct((lp, cw), F32),
                   jax.ShapeDtypeStruct((cw, cw), F32), jax.ShapeDtypeStruct((1, cw), F32),
                   jax.ShapeDtypeStruct((1, cw), F32), jax.ShapeDtypeStruct((1, cw), F32)],
        input_output_aliases={7: 0},
        compiler_params=_params(1, 48))(dy, proj, u1, cln_g, cln_b, pw_w, pw_b, dproj)


def _conv_bwd_dw(du1, proj, dw_w, dproj):
    lp, cw = du1.shape
    tm = _row_tile(lp, 640)
    rb = _row_tile(tm, 64)
    nt = lp // tm
    hb = tm // HALO

    def body(a_ref, b_ref, du_ref, nx_ref, w_ref, dp_in, dp_ref, dww_ref, dwb_ref, ubuf, dbuf, du0, acc):
        del dp_in
        i = pl.program_id(0)

        @pl.when(i == 0)
        def _():
            ubuf[0:HALO, :] = jnp.zeros((HALO, cw), F32)
            acc[...] = jnp.zeros_like(acc)
            dwb_ref[...] = jnp.zeros_like(dwb_ref)

        a = a_ref[...]
        sb = _sigmoid(b_ref[...])
        ubuf[HALO:HALO + tm, :] = a * sb
        du = du_ref[...]
        dbuf[0:tm, :] = du
        dbuf[tm:tm + HALO, :] = jnp.where(i == nt - 1, 0.0, nx_ref[...])
        dwb_ref[...] += _colsum(du)
        _dw_taps(dbuf, w_ref, du0, None, rows=tm, start=0, flip=True, rb=rb)
        d0 = du0[...]
        dp_ref[:, 0:cw] = (d0 * sb).astype(BF16)
        dp_ref[:, cw:2 * cw] = (d0 * a * sb * (1.0 - sb)).astype(BF16)

        lb = min(128, cw)
        groups = _tap_groups(HALO - (CONV_K - 1), False)

        def rb_body(r, carry):
            base = pl.multiple_of(r * rb, rb)
            for cb in range(cw // lb):
                ls = slice(cb * lb, (cb + 1) * lb)
                win = ubuf[pl.ds(base, rb + HALO), ls]
                dv = dbuf[pl.ds(base, rb), ls]
                for s, taps in groups.items():
                    ws = _shift_up(win, s)
                    for a, j in taps:
                        prod = dv * ws[8 * a:8 * a + rb, :]
                        acc[8 * j:8 * j + 8, ls] += jnp.sum(prod.reshape(rb // 8, 8, lb), axis=0)
            return carry

        lax.fori_loop(0, tm // rb, rb_body, 0)
        ubuf[0:HALO, :] = ubuf[tm:tm + HALO, :]

        @pl.when(i == nt - 1)
        def _():
            for j in range(CONV_K):
                dww_ref[j:j + 1, :] = _colsum(acc[8 * j:8 * j + 8, :])
            dww_ref[CONV_K:HALO, :] = jnp.zeros((HALO - CONV_K, cw), F32)

    def col(j):
        return pl.BlockSpec((tm, cw), lambda i: (i, j))

    return pl.pallas_call(
        body, name="conv_bwd_dw", grid=(nt,),
        in_specs=[col(4), col(5), col(0),
                  pl.BlockSpec((HALO, cw), lambda i: (jnp.minimum((i + 1) * hb, nt * hb - 1), 0)),
                  pl.BlockSpec(dw_w.shape, lambda i: (0, 0)),
                  pl.BlockSpec(memory_space=pl.ANY)],
        out_specs=[pl.BlockSpec((tm, 2 * cw), lambda i: (i, 2)),
                   pl.BlockSpec((HALO, cw), lambda i: (0, 0)),
                   pl.BlockSpec((1, cw), lambda i: (0, 0))],
        out_shape=[jax.ShapeDtypeStruct(dproj.shape, dproj.dtype), jax.ShapeDtypeStruct((HALO, cw), F32),
                   jax.ShapeDtypeStruct((1, cw), F32)],
        scratch_shapes=[pltpu.VMEM((tm + HALO, cw), F32), pltpu.VMEM((tm + HALO, cw), F32),
                        pltpu.VMEM((tm, cw), F32), pltpu.VMEM((8 * HALO, cw), F32)],
        input_output_aliases={5: 0},
        compiler_params=_params(1, 56))(proj, proj, du1, du1, dw_w, dproj)


def _dw_in(hn, dproj, ns, others):
    lp, d = hn.shape
    tm = _row_tile(lp, 1664)
    nt = lp // tm
    mb = 512 if d % 512 == 0 else d
    n = len(others)
    halves = [d // 2] + [g.shape[1] // 2 for g in others]

    def body(hn_ref, dp_ref, *refs):
        other_refs, o_hbm, recv_refs = refs[:n], refs[n], refs[n + 1:2 * n + 2]
        acc, sem, send_sems, recv_sems = refs[2 * n + 2:]
        s, i = pl.program_id(0), pl.program_id(1)
        x, y, c = _mesh_pos()

        def to_sibling(src, dst, k):
            return pltpu.make_async_remote_copy(
                src_ref=src, dst_ref=dst, send_sem=send_sems.at[k], recv_sem=recv_sems.at[k],
                device_id=(x, y, 1 - c), device_id_type=MESH)

        def shard_half(p):
            return to_sibling(o_hbm.at[p, pl.ds((1 - c) * halves[0], halves[0])], recv_refs[0].at[p], p)

        def other_halves(k):
            return to_sibling(other_refs[k].at[:, pl.ds((1 - c) * halves[1 + k], halves[1 + k])],
                              recv_refs[1 + k], N_CHIPS + k)

        @pl.when((s == 0) & (i == 0))
        def _():
            for k in range(n):
                other_halves(k).start()

        @pl.when(i == 0)
        def _():
            acc[...] = jnp.zeros_like(acc)

        for m in range(d // mb):
            rows = slice(m * mb, (m + 1) * mb)
            acc[rows, :] += _dot_tn(hn_ref[:, rows], dp_ref[...])

        @pl.when(i == nt - 1)
        def _():
            cp = pltpu.make_async_copy(acc, o_hbm.at[s], sem)
            cp.start()
            cp.wait()
            shard_half(s).start()

        @pl.when((s == N_CHIPS - 1) & (i == nt - 1))
        def _():
            for p in range(N_CHIPS):
                shard_half(p).wait()
            for k in range(n):
                other_halves(k).wait()

    any_spec = pl.BlockSpec(memory_space=pl.ANY)
    outs = pl.pallas_call(
        body, name="dw_in", grid=(N_CHIPS, nt),
        in_specs=[pl.BlockSpec((tm, d), lambda s, i: (i, 0)),
                  pl.BlockSpec((tm, ns), lambda s, i: (i, s))] + [any_spec] * n,
        out_specs=[any_spec] * (n + 2),
        out_shape=[jax.ShapeDtypeStruct((N_CHIPS, d, ns), F32), jax.ShapeDtypeStruct((N_CHIPS, d // 2, ns), F32)]
                  + [jax.ShapeDtypeStruct((N_CHIPS, g.shape[1] // 2) + g.shape[2:], g.dtype) for g in others],
        scratch_shapes=[pltpu.VMEM((d, ns), F32), pltpu.SemaphoreType.DMA,
                        pltpu.SemaphoreType.DMA((N_CHIPS + n,)), pltpu.SemaphoreType.DMA((N_CHIPS + n,))],
        compiler_params=_params(2, 56))(hn, dproj, *others)
    return outs[0], outs[1:]


def _in_proj_bwd(dproj, w4, x, lead, r1, dh2, ln_g, cs):
    seq, d = x.shape
    lp = CHUNK + seq
    ns = w4.shape[2]
    tm = _row_tile(lp, 320)
    nt = lp // tm
    n = len(cs)
    assert tm > CHUNK

    def body(dp_ref, w_ref, x_hbm, lead_hbm, r_ref, d2_ref, g_ref, *refs):
        cs_refs, (dh_ref, dlg_ref), rb_refs = refs[:n], refs[n:n + 2], refs[n + 2:2 * n + 2]
        hbuf, h_sems, send_sems, recv_sems = refs[2 * n + 2:]
        i = pl.program_id(0)
        slot = i % 2
        x, y, c = _mesh_pos()
        h_first, h_start, h_wait = _padded_tile_stream(x_hbm, lead_hbm, hbuf, h_sems, tm)

        @pl.when(i == 0)
        def _():
            h_first(0)

        @pl.when(i + 1 < nt)
        def _():
            h_start(1 - slot, i + 1)

        h_wait(slot)

        def exchange():
            return [pltpu.make_async_remote_copy(
                src_ref=cs_refs[k].at[2 * chip[0] + chip[1]], dst_ref=rb_refs[k].at[j],
                send_sem=send_sems.at[3 * k + j], recv_sem=recv_sems.at[3 * k + j],
                device_id=(*chip, c), device_id_type=MESH)
                for k in range(n) for j, chip in enumerate(_other_chips(x, y))]

        @pl.when(i == 0)
        def _():
            dlg_ref[...] = jnp.zeros_like(dlg_ref)
            for cp in exchange():
                cp.start()

        dhn = _dot_nt(dp_ref[:, 0:ns], w_ref[0])
        for s in range(1, N_CHIPS):
            dhn = dhn + _dot_nt(dp_ref[:, s * ns:(s + 1) * ns], w_ref[s])
        r = r_ref[...]
        hn0 = hbuf[slot] * r
        dlg_ref[...] += _colsum(dhn * hn0)
        t = dhn * g_ref[...]
        dh_ref[...] = d2_ref[...] + r * (t - hn0 * _mean(t * hn0))

        @pl.when(i == nt - 1)
        def _():
            for cp in exchange():
                cp.wait()

    def row(cols):
        return pl.BlockSpec((tm, cols), lambda i: (i, 0))

    any_spec = pl.BlockSpec(memory_space=pl.ANY)
    outs = pl.pallas_call(
        body, name="in_proj_bwd", grid=(nt,),
        in_specs=[row(N_CHIPS * ns), pl.BlockSpec(memory_space=pltpu.VMEM),
                  any_spec, any_spec, row(1), row(d), pl.BlockSpec((1, d), lambda i: (0, 0))] + [any_spec] * n,
        out_specs=[row(d), pl.BlockSpec((1, d), lambda i: (0, 0))] + [any_spec] * n,
        out_shape=[jax.ShapeDtypeStruct((lp, d), F32), jax.ShapeDtypeStruct((1, d), F32)]
                  + [jax.ShapeDtypeStruct((3,) + a.shape[1:], a.dtype) for a in cs],
        scratch_shapes=[pltpu.VMEM((2, tm, d), F32), pltpu.SemaphoreType.DMA((2,)),
                        pltpu.SemaphoreType.DMA((3 * n,)), pltpu.SemaphoreType.DMA((3 * n,))],
        compiler_params=_params(1, 58))(dproj, w4, x, lead, r1, dh2, ln_g, *_in_hbm(*cs))
    return outs[0], outs[1], outs[2:]


def _mesh_pos():
    return lax.axis_index("x"), lax.axis_index("y"), lax.axis_index("c")


def _other_chips(x, y):
    return [(1 - x, y), (x, 1 - y), (1 - x, 1 - y)]


def _gather_shards(shards):
    n = len(shards)
    halves = [a.shape[0] // 2 for a in shards]

    def body(*refs):
        ins, outs = refs[:n], refs[n:2 * n]
        send_sems, recv_sems, loc_sems = refs[2 * n:]
        x, y, c = _mesh_pos()
        me, sibling = (x, y, c), (x, y, 1 - c)
        my_s = 2 * x + y
        chips = _other_chips(x, y)

        def half(k, s, cc):
            return outs[k].at[s, pl.ds(cc * halves[k], halves[k])]

        def rcopy(k, j, src, dst, to):
            return pltpu.make_async_remote_copy(
                src_ref=src, dst_ref=dst, send_sem=send_sems.at[6 * k + j], recv_sem=recv_sems.at[6 * k + j],
                device_id=to, device_id_type=MESH)

        local = [pltpu.make_async_copy(ins[k], outs[k].at[my_s], loc_sems.at[k]) for k in range(n)]
        for cp in local:
            cp.start()
        started = []
        for k in range(n):
            for j, chip in enumerate(chips):
                cp = rcopy(k, j, ins[k].at[pl.ds(c * halves[k], halves[k])], half(k, my_s, c), (*chip, c))
                cp.start()
                started.append(cp)
        for j, chip in enumerate(chips):
            s_j = 2 * chip[0] + chip[1]
            for k in range(n):
                rcopy(k, j, half(k, s_j, c), half(k, s_j, c), me).wait_recv()
                cp = rcopy(k, 3 + j, half(k, s_j, c), half(k, s_j, c), sibling)
                cp.start()
                started.append(cp)
        for j, chip in enumerate(chips):
            s_j = 2 * chip[0] + chip[1]
            for k in range(n):
                rcopy(k, 3 + j, half(k, s_j, 1 - c), half(k, s_j, 1 - c), me).wait_recv()
        for cp in started:
            cp.wait_send()
        for cp in local:
            cp.wait()

    return pl.pallas_call(
        body, name="gather_weights",
        in_specs=[pl.BlockSpec(memory_space=pl.ANY)] * n,
        out_specs=[pl.BlockSpec(memory_space=pl.ANY)] * n,
        out_shape=[jax.ShapeDtypeStruct((N_CHIPS,) + a.shape, a.dtype) for a in shards],
        scratch_shapes=[pltpu.SemaphoreType.DMA((6 * n,)), pltpu.SemaphoreType.DMA((6 * n,)),
                        pltpu.SemaphoreType.DMA((n,))],
    )(*shards)


def _rs_pair_share(fulls):
    n = len(fulls)
    halves = [f.shape[0] // 2 for f in fulls]

    def body(*refs):
        outs = refs[n:2 * n]
        send_sems, recv_sems = refs[2 * n:]
        x, y, c = _mesh_pos()

        def copy(k, cc, to):
            rows = outs[k].at[pl.ds(cc * halves[k], halves[k])]
            return pltpu.make_async_remote_copy(
                src_ref=rows, dst_ref=rows, send_sem=send_sems.at[k], recv_sem=recv_sems.at[k],
                device_id=to, device_id_type=MESH)

        cps = [copy(k, c, (x, y, 1 - c)) for k in range(n)]
        for cp in cps:
            cp.start()
        for k in range(n):
            copy(k, 1 - c, (x, y, c)).wait_recv()
        for cp in cps:
            cp.wait_send()

    return pl.pallas_call(
        body, name="rs_pair_share",
        in_specs=[pl.BlockSpec(memory_space=pl.ANY)] * n,
        out_specs=[pl.BlockSpec(memory_space=pl.ANY)] * n,
        out_shape=[jax.ShapeDtypeStruct(f.shape, f.dtype) for f in fulls],
        input_output_aliases={k: k for k in range(n)},
        scratch_shapes=[pltpu.SemaphoreType.DMA((n,)), pltpu.SemaphoreType.DMA((n,))],
    )(*fulls)


def _pair_sum(g, recv, pos, name):
    _, rows, cols = g.shape
    h = rows // 2
    tr = _row_tile(h, 256)
    nb = h // tr

    def body(pos_ref, g_ref, r_ref, o_ref, own_ref):
        total = g_ref[0] + r_ref[0]
        o_ref[0] = total.astype(BF16)

        @pl.when(pl.program_id(1) == pos_ref[1])
        def _():
            own_ref[...] = total

    return pl.pallas_call(
        body, name=name,
        grid_spec=pltpu.PrefetchScalarGridSpec(
            num_scalar_prefetch=1, grid=(nb, N_CHIPS),
            in_specs=[pl.BlockSpec((1, tr, cols), lambda r, s, pos_ref: (s, pos_ref[0] * nb + r, 0)),
                      pl.BlockSpec((1, tr, cols), lambda r, s, pos_ref: (s, r, 0))],
            out_specs=[pl.BlockSpec((1, tr, cols), lambda r, s, pos_ref: (s, r, 0)),
                       pl.BlockSpec((tr, cols), lambda r, s, pos_ref: (r, 0))]),
        out_shape=[pltpu.HBM((N_CHIPS, h, cols), BF16), pltpu.HBM((h, cols), F32)],
        compiler_params=_params(2, 32))(pos, *_in_hbm(g, recv))


def _chip_sum(own, rb, pos, name):
    h, cols = own.shape
    tr = _row_tile(h, 256)
    nb = h // tr

    def body(pos_ref, c_ref, r_ref, o_ref):
        del pos_ref
        o_ref[...] = ((c_ref[...] + r_ref[0].astype(F32)) + r_ref[1].astype(F32)) + r_ref[2].astype(F32)

    return pl.pallas_call(
        body, name=name,
        grid_spec=pltpu.PrefetchScalarGridSpec(
            num_scalar_prefetch=1, grid=(nb,),
            in_specs=[pl.BlockSpec((tr, cols), lambda r, pos_ref: (r, 0)),
                      pl.BlockSpec((3, tr, cols), lambda r, pos_ref: (0, r, 0))],
            out_specs=pl.BlockSpec((tr, cols), lambda r, pos_ref: (pos_ref[0] * nb + r, 0))),
        out_shape=jax.ShapeDtypeStruct((2 * h, cols), F32),
        compiler_params=_params(1, 32))(pos, *_in_hbm(own, rb))


def _adamw_big(w, g, m, v, name):
    rows, cols = w.shape
    tr = _row_tile(rows, 256)

    def body(w_ref, g_ref, m_ref, v_ref, go_ref, d_ref, nm_ref, nv_ref):
        g = g_ref[...]
        go_ref[...] = g
        d_ref[...], nm_ref[...], nv_ref[...] = _adamw(w_ref[...], g, m_ref[...], v_ref[...])

    spec = pl.BlockSpec((tr, cols), lambda i: (i, 0))
    return pl.pallas_call(
        body, name=name, grid=(rows // tr,),
        in_specs=[spec] * 4, out_specs=[spec] * 4,
        out_shape=[jax.ShapeDtypeStruct((rows, cols), F32)] * 4,
        compiler_params=_params(1, 48))(*_in_hbm(w, g, m, v))


def _gather_small(loss, dfg, dlg, dgn, ddwb, dclg, dclb, dpwb, ddww, dmeta):
    d = loss.shape[1]
    w = dgn.shape[1]
    wc, mc = ddww.shape[1] // N_CHIPS, dmeta.shape[1] // N_CHIPS

    def body(loss_ref, dfg_ref, dlg_ref, dgn_ref, ddwb_ref, dclg_ref, dclb_ref, dpwb_ref, ddww_ref, dmeta_ref,
             gs_ref, gd_ref, gm_ref, send_sems, recv_sems, loc_sems):
        x, y, c = _mesh_pos()
        me = 4 * x + 2 * y + c
        gs_ref[me, 0:1, :] = loss_ref[...]
        gs_ref[me, 1:2, :] = dfg_ref[...]
        gs_ref[me, 2:3, :] = dlg_ref[...]
        gs_ref[me, 3:4, 0:w] = dgn_ref[...]
        gs_ref[me, 3:4, w:2 * w] = ddwb_ref[...]
        gs_ref[me, 4:5, 0:w] = dclg_ref[...]
        gs_ref[me, 4:5, w:2 * w] = dclb_ref[...]
        gs_ref[me, 5:6, 0:w] = dpwb_ref[...]
        gs_ref[me, 5:6, w:2 * w] = jnp.zeros((1, d - w), F32)
        gs_ref[me, 6:8, :] = jnp.zeros((2, d), F32)
        bufs = (gs_ref, gd_ref, gm_ref)

        def mine_for(k, shard):
            if k == 0:
                return gs_ref.at[me]
            ref, width = ((ddww_ref, wc), (dmeta_ref, mc))[k - 1]
            return ref.at[:, pl.ds(pl.multiple_of(shard * width, width), width)]

        def peer(j):
            return (1 - x if j & 4 else x), (1 - y if j & 2 else y), (1 - c if j & 1 else c)

        def copy(k, j, src, slot, to):
            return pltpu.make_async_remote_copy(
                src_ref=src, dst_ref=bufs[k].at[slot],
                send_sem=send_sems.at[7 * k + j - 1], recv_sem=recv_sems.at[7 * k + j - 1],
                device_id=to, device_id_type=MESH)

        own = [pltpu.make_async_copy(mine_for(k, 2 * x + y), bufs[k].at[me], loc_sems.at[k - 1]) for k in (1, 2)]
        for cp in own:
            cp.start()
        cps = []
        for k in range(3):
            for j in range(1, N_DEV):
                px, py, pc = peer(j)
                cp = copy(k, j, mine_for(k, 2 * px + py), me, (px, py, pc))
                cp.start()
                cps.append(cp)
        for k in range(3):
            for j in range(1, N_DEV):
                px, py, pc = peer(j)
                slot = 4 * px + 2 * py + pc
                copy(k, j, bufs[k].at[slot], slot, (x, y, c)).wait_recv()
        for cp in cps:
            cp.wait_send()
        for cp in own:
            cp.wait()

    vm = pl.BlockSpec(memory_space=pltpu.VMEM)
    return pl.pallas_call(
        body, name="gather_small",
        in_specs=[vm] * 10, out_specs=[vm] * 3,
        out_shape=[jax.ShapeDtypeStruct((N_DEV, 8, d), F32),
                   jax.ShapeDtypeStruct((N_DEV, ddww.shape[0], wc), F32),
                   jax.ShapeDtypeStruct((N_DEV, dmeta.shape[0], mc), F32)],
        scratch_shapes=[pltpu.SemaphoreType.DMA((21,)), pltpu.SemaphoreType.DMA((21,)),
                        pltpu.SemaphoreType.DMA((2,))],
    )(loss, dfg, dlg, dgn, ddwb, dclg, dclb, dpwb, ddww, dmeta)


def _small_update(gs, gd, gm, weights, ms, vs):
    d = gs.shape[2]
    w = d // 2
    n = len(weights)

    def body(gs_ref, gd_ref, gm_ref, *refs):
        w_refs, m_refs, v_refs = refs[:n], refs[n:2 * n], refs[2 * n:3 * n]
        loss_ref = refs[3 * n]
        g_refs = refs[3 * n + 1:4 * n + 1]
        d_refs = refs[4 * n + 1:5 * n + 1]
        nm_refs = refs[5 * n + 1:6 * n + 1]
        nv_refs = refs[6 * n + 1:7 * n + 1]

        def total(ref):
            t = ref[0]
            for dev in range(1, N_DEV):
                t = t + ref[dev]
            return t

        packed = total(gs_ref)
        loss_ref[...] = jnp.sum(packed[0:1, :], axis=1, keepdims=True) * (0.5 / d)
        grads = [packed[2:3, :], packed[1:2, :], packed[3:4, 0:w], packed[3:4, w:2 * w], packed[4:5, 0:w],
                 packed[4:5, w:2 * w], packed[5:6, 0:w], total(gd_ref), total(gm_ref)]
        for k in range(n):
            g = grads[k]
            g_refs[k][...] = g
            d_refs[k][...], nm_refs[k][...], nv_refs[k][...] = _adamw(w_refs[k][...], g, m_refs[k][...], v_refs[k][...])

    def whole(shape):
        return pl.BlockSpec(shape, lambda i: (0,) * len(shape))

    shapes = [a.shape for a in weights]
    in_specs = [whole(gs.shape), whole(gd.shape), whole(gm.shape)] + [whole(s) for s in shapes] * 3
    out_specs = [whole((1, 1))] + [whole(s) for s in shapes] * 4
    out_shape = [jax.ShapeDtypeStruct((1, 1), F32)] + [jax.ShapeDtypeStruct(s, F32) for s in shapes] * 4
    outs = pl.pallas_call(
        body, name="small_update", grid=(1,), in_specs=in_specs, out_specs=out_specs, out_shape=out_shape,
        compiler_params=_params(1, 32))(gs, gd, gm, *weights, *ms, *vs)
    loss = outs[0]
    return loss, outs[1:n + 1], outs[n + 1:2 * n + 1], outs[2 * n + 1:3 * n + 1], outs[3 * n + 1:4 * n + 1]


def kernel(x, meta_tokens, ln_g, w_in, ret_gn_g, conv_dw_w, conv_dw_b, conv_ln_g, conv_ln_b, conv_pw_w, conv_pw_b, w_out, final_g, loss_target, m_meta_tokens, m_ln_g, m_w_in, m_ret_gn_g, m_conv_dw_w, m_conv_dw_b, m_conv_ln_g, m_conv_ln_b, m_conv_pw_w, m_conv_pw_b, m_w_out, m_final_g, v_meta_tokens, v_ln_g, v_w_in, v_ret_gn_g, v_conv_dw_w, v_conv_dw_b, v_conv_ln_g, v_conv_ln_b, v_conv_pw_w, v_conv_pw_b, v_w_out, v_final_g):
    seq, d = x.shape[1], x.shape[2]
    w = ret_gn_g.shape[1]
    hd = w // RET_HEADS
    lp = CHUNK + seq
    ns = w_in.shape[2]
    mx, my, mc = lax.axis_index("x"), lax.axis_index("y"), lax.axis_index("c")
    my_s = 2 * mx + my
    s_arr = my_s.astype(jnp.int32).reshape(1)
    pos = jnp.stack([mc, my_s]).astype(jnp.int32)
    order = jnp.stack([my_s, 2 * (1 - mx) + my, 2 * mx + (1 - my), 2 * (1 - mx) + (1 - my)]).astype(jnp.int32)

    dw_pad = jnp.pad(conv_dw_w[0], ((0, HALO - CONV_K), (0, 0)))
    dw4, meta4 = _gather_shards([dw_pad, meta_tokens])
    dw_full = dw4.transpose(1, 0, 2).reshape(HALO, w)
    meta_full = meta4.transpose(1, 0, 2).reshape(N_META, d)

    lead = jnp.concatenate([jnp.zeros((LEAD, d), F32), meta_full], axis=0)
    zero_lead = jnp.zeros((CHUNK, d), F32)
    consts = _ret_consts()
    rope = _rope_tables(lp // CHUNK, hd // 2)
    fg2 = final_g.reshape(1, d)

    proj, r1, hn, w4, pw4, wo4 = _in_proj_gather(
        order, x[0], lead, ln_g, _cast_into_gathered(w_in[0], s_arr, "cast_w_in"),
        _cast_into_gathered(conv_pw_w[0], s_arr, "cast_pw_w"), _cast_into_gathered(w_out[0], s_arr, "cast_w_out"))
    pw_full = pw4.reshape(w, w)
    wo_full = wo4.reshape(2 * w, d)
    y_ret, ssave, ret_ops, ret_scores, ret_out = _ret_fwd(proj, rope, ret_gn_g, consts)
    y_conv, u1 = _conv_fwd(proj, dw_full, conv_dw_b, conv_ln_g, conv_ln_b, pw_full, conv_pw_b)
    dh2, dy, loss_l, dfg = _out_proj_loss(y_ret, y_conv, x[0], lead, loss_target[0], zero_lead, wo_full, fg2)

    g_wo = _dw_out(y_ret, y_conv, dh2)
    dproj, dgn = _ret_bwd(proj, dy, ssave, ret_ops, ret_scores, ret_out, rope, ret_gn_g, consts)
    dproj, du1, g_pw, dpwb, dclg, dclb = _conv_bwd_pw(dy, proj, u1, conv_ln_g, conv_ln_b, pw_full, conv_pw_b, dproj)
    dproj, ddww, ddwb = _conv_bwd_dw(du1, proj, dw_full, dproj)
    g_wo4 = g_wo.reshape(N_CHIPS, (2 * w) // N_CHIPS, d)
    g_pw4 = g_pw.reshape(N_CHIPS, w // N_CHIPS, w)
    g_win, recv = _dw_in(hn, dproj, ns, [g_wo4, g_pw4])
    gs = [g_win, g_wo4, g_pw4]
    names = ("w_in", "w_out", "pw_w")
    sums = [_pair_sum(g, r, pos, "pair_sum_" + nm) for g, r, nm in zip(gs, recv, names)]
    dh, dlg, rb = _in_proj_bwd(dproj, w4, x[0], lead, r1, dh2, ln_g, [cs_ for cs_, _ in sums])
    grad_x = dh[CHUNK:][None]
    dmeta = dh[LEAD:CHUNK]
    fulls = [_chip_sum(own, r, pos, "chip_sum_" + nm) for (_, own), r, nm in zip(sums, rb, names)]
    grad_w_in, grad_w_out, grad_pw = _rs_pair_share(fulls)
    grad_w_in, d_win, nm_win, nv_win = _adamw_big(w_in[0], grad_w_in, m_w_in[0], v_w_in[0], "adamw_w_in")
    grad_w_out, d_wo, nm_wo, nv_wo = _adamw_big(w_out[0], grad_w_out, m_w_out[0], v_w_out[0], "adamw_w_out")
    grad_pw, d_pw, nm_pw, nv_pw = _adamw_big(conv_pw_w[0], grad_pw, m_conv_pw_w[0], v_conv_pw_w[0], "adamw_pw_w")

    gsm, gdm, gmm = _gather_small(loss_l, dfg, dlg, dgn, ddwb, dclg, dclb, dpwb, ddww, dmeta)

    def pad_dw(a):
        return jnp.pad(a[0], ((0, HALO - CONV_K), (0, 0)))

    small_w = [ln_g, fg2, ret_gn_g, conv_dw_b, conv_ln_g, conv_ln_b, conv_pw_b, dw_pad, meta_tokens]
    small_m = [m_ln_g, m_final_g.reshape(1, d), m_ret_gn_g, m_conv_dw_b, m_conv_ln_g, m_conv_ln_b, m_conv_pw_b,
               pad_dw(m_conv_dw_w), m_meta_tokens]
    small_v = [v_ln_g, v_final_g.reshape(1, d), v_ret_gn_g, v_conv_dw_b, v_conv_ln_g, v_conv_ln_b, v_conv_pw_b,
               pad_dw(v_conv_dw_w), v_meta_tokens]
    loss, sg, sd, snm, snv = _small_update(gsm, gdm, gmm, small_w, small_m, small_v)

    def assemble(small, big_in, big_pw, big_out):
        ln, fg, gn, dwb, clg, clb, pwb, dww, meta = small
        return (meta, ln, big_in[None], gn, dww[:CONV_K][None], dwb, clg, clb, big_pw[None], pwb, big_out[None],
                fg.reshape(d))

    return (loss.reshape(()), grad_x,
            *assemble(sg, grad_w_in, grad_pw, grad_w_out),
            *assemble(sd, d_win, d_pw, d_wo),
            *assemble(snm, nm_win, nm_pw, nm_wo),
            *assemble(snv, nv_win, nv_pw, nv_wo))
```

```python
import functools

import jax
import jax.numpy as jnp
from jax import lax
from jax.experimental import pallas as pl
from jax.experimental.pallas import tpu as pltpu

F32 = jnp.float32
BF16 = jnp.bfloat16
MESH = pl.DeviceIdType.MESH

N_META = 16
CHUNK = 128
LEAD = (-N_META) % CHUNK
RET_HEADS = 4
CONV_K = 31
HALO = 32
ROPE_BASE = 10000.0
EPS = 1e-6
N_CHIPS = 4
N_DEV = 8

ADAM_LR = 0.001
ADAM_B1 = 0.9
ADAM_B2 = 0.999
ADAM_EPS = 1e-08
ADAM_WD = 0.01
ADAM_STEP = 10

MIB = 2 ** 20


def _params(n_grid_axes, vmem_mib):
    return pltpu.CompilerParams(dimension_semantics=("arbitrary",) * n_grid_axes,
                                vmem_limit_bytes=vmem_mib * MIB)


def _row_tile(n, pref):
    for t in (1664, 1280, 1024, 640, 512, 384, 320, 256, 128, 64, 32, 16, 8):
        if t <= pref and n % t == 0:
            return t
    raise ValueError(f"no row tile for {n}")


def _dot(a, b):
    return jnp.dot(a, b, preferred_element_type=F32)


def _dot_nt(a, b):
    return lax.dot_general(a, b, (((1,), (1,)), ((), ())), preferred_element_type=F32)


def _dot_tn(a, b):
    return lax.dot_general(a, b, (((0,), (0,)), ((), ())), preferred_element_type=F32)


def _sigmoid(x):
    return jax.nn.sigmoid(x)


def _dsilu(x, s):
    return s * (1.0 + x * (1.0 - s))


def _mean(x):
    return jnp.mean(x, axis=-1, keepdims=True)


def _colsum(x):
    return jnp.sum(x, axis=0, keepdims=True)


def _rope(x, cos, sin):
    half = x.shape[-1] // 2
    x1, x2 = x[:, :half], x[:, half:]
    return jnp.concatenate([x1 * cos - x2 * sin, x1 * sin + x2 * cos], axis=-1)


def _rope_t(d, cos, sin):
    half = d.shape[-1] // 2
    d1, d2 = d[:, :half], d[:, half:]
    return jnp.concatenate([d1 * cos + d2 * sin, d2 * cos - d1 * sin], axis=-1)


def _adamw(w, g, m, v):
    m = ADAM_B1 * m + (1.0 - ADAM_B1) * g
    v = ADAM_B2 * v + (1.0 - ADAM_B2) * (g * g)
    m_hat = m / (1.0 - ADAM_B1 ** ADAM_STEP)
    v_hat = v / (1.0 - ADAM_B2 ** ADAM_STEP)
    delta = -ADAM_LR * (m_hat / (jnp.sqrt(v_hat) + ADAM_EPS) + ADAM_WD * w)
    return delta, m, v


def _ret_consts():
    h = jnp.arange(RET_HEADS, dtype=F32)
    log_g = jnp.log(1.0 - jnp.exp2(-5.0 - h))
    idx = jnp.arange(CHUNK, dtype=F32)
    rel = idx[:, None] - idx[None, :]
    dmask = jnp.where(rel[None] >= 0, jnp.exp(jnp.maximum(rel, 0.0)[None] * log_g[:, None, None]), 0.0)
    qd = jnp.exp((idx[None, :] + 1.0) * log_g[:, None])[:, :, None]
    kd = jnp.exp((CHUNK - 1.0 - idx[None, :]) * log_g[:, None])[:, :, None]
    cd = jnp.exp(CHUNK * log_g)[:, None, None]
    return dmask, qd, kd, cd


def _rope_tables(nch, half):
    inv_freq = ROPE_BASE ** (-jnp.arange(half, dtype=F32) / half)
    start = (jnp.arange(nch, dtype=F32) * float(CHUNK) - float(LEAD))[:, None] * inv_freq[None, :]
    within = jnp.arange(CHUNK, dtype=F32)[:, None] * inv_freq[None, :]
    return jnp.cos(start)[:, None, :], jnp.sin(start)[:, None, :], jnp.cos(within), jnp.sin(within)


def _rope_of_chunk(ca, sa, cb, sb):
    return ca * cb - sa * sb, sa * cb + ca * sb


def _padded_tile_stream(x_hbm, lead_hbm, buf, sems, tm):
    def start_first(slot):
        pltpu.make_async_copy(lead_hbm, buf.at[slot, pl.ds(0, CHUNK)], sems.at[slot]).start()
        pltpu.make_async_copy(x_hbm.at[pl.ds(0, tm - CHUNK)], buf.at[slot, pl.ds(CHUNK, tm - CHUNK)],
                              sems.at[slot]).start()

    def start(slot, tile):
        pltpu.make_async_copy(x_hbm.at[pl.ds(tile * tm - CHUNK, tm)], buf.at[slot], sems.at[slot]).start()

    def wait(slot):
        pltpu.make_async_copy(x_hbm.at[pl.ds(0, tm)], buf.at[slot], sems.at[slot]).wait()

    return start_first, start, wait


def _cast_into_gathered(a, s_arr, name):
    rows, cols = a.shape
    tr = _row_tile(rows, 256)

    def body(s_ref, a_ref, o_ref):
        del s_ref
        o_ref[0] = a_ref[...].astype(BF16)

    return pl.pallas_call(
        body, name=name,
        grid_spec=pltpu.PrefetchScalarGridSpec(
            num_scalar_prefetch=1, grid=(rows // tr,),
            in_specs=[pl.BlockSpec((tr, cols), lambda i, s_ref: (i, 0))],
            out_specs=pl.BlockSpec((1, tr, cols), lambda i, s_ref: (s_ref[0], i, 0))),
        out_shape=jax.ShapeDtypeStruct((N_CHIPS, rows, cols), BF16),
        compiler_params=_params(1, 32))(s_arr, a)


def _in_proj_gather(order, x, lead, ln_g, w4, pw4, wo4):
    seq, d = x.shape
    lp = CHUNK + seq
    ns = w4.shape[2]
    tm = _row_tile(lp, 640)
    nt = lp // tm
    assert nt >= 2, "the hn write-back of a row tile is waited for one step later, before any pass re-reads it"
    land_step, load_step = max(nt - 3, 0), max(nt - 2, 0)
    gathered = (w4, pw4, wo4)
    halves = [a.shape[1] // 2 for a in gathered]
    n = len(gathered)

    def body(order_ref, x_hbm, lead_hbm, g_ref, w_in, pw_in, wo_in, proj_ref, r_ref, hn_hbm, w_out, pw_out,
             wo_out, wbuf, hnbuf, hbuf, send_sems, recv_sems, hn_out_sems, hn_in_sems, h_sems, w_sem):
        del order_ref, w_in, pw_in, wo_in
        t, i = pl.program_id(0), pl.program_id(1)
        slot = (t * nt + i) % 2
        h_first, h_start, h_wait = _padded_tile_stream(x_hbm, lead_hbm, hbuf, h_sems, tm)
        x, y, c = _mesh_pos()
        me, sibling = (x, y, c), (x, y, 1 - c)
        my_s = 2 * x + y
        chips = _other_chips(x, y)
        outs = (w_out, pw_out, wo_out)

        def half(k, s, cc):
            return outs[k].at[s, pl.ds(cc * halves[k], halves[k])]

        def rcopy(k, j, rows, to):
            return pltpu.make_async_remote_copy(
                src_ref=rows, dst_ref=rows, send_sem=send_sems.at[6 * k + j], recv_sem=recv_sems.at[6 * k + j],
                device_id=to, device_id_type=MESH)

        def send(k, j):
            return rcopy(k, j, half(k, my_s, c), (*chips[j], c))

        def shard_of(j):
            return 2 * chips[j][0] + chips[j][1]

        def forward(k, j):
            return rcopy(k, 3 + j, half(k, shard_of(j), c), sibling)

        def land(k, j):
            rcopy(k, j, half(k, shard_of(j), c), me).wait_recv()
            forward(k, j).start()

        def landed_from_sibling(k, j):
            rcopy(k, 3 + j, half(k, shard_of(j), 1 - c), me).wait_recv()

        def load_w(s, wslot):
            return pltpu.make_async_copy(w_out.at[s], wbuf.at[wslot], w_sem)

        def hn_out(sl, row_tile):
            return pltpu.make_async_copy(hnbuf.at[sl], hn_hbm.at[pl.ds(row_tile * tm, tm)], hn_out_sems.at[sl])

        def hn_in(sl, row_tile):
            return pltpu.make_async_copy(hn_hbm.at[pl.ds(row_tile * tm, tm)], hnbuf.at[sl], hn_in_sems.at[sl])

        @pl.when((t == 0) & (i == 0))
        def _():
            for j in range(2):
                send(0, j).start()
            load_w(my_s, 0).start()
            load_w(my_s, 0).wait()

        @pl.when((t == 1) & (i == 0))
        def _():
            send(0, 2).start()
            for k in range(1, n):
                for jj in range(3):
                    send(k, jj).start()

        for j in range(3):
            @pl.when((t == j) & (i == land_step))
            def _(j=j):
                land(0, j)

            @pl.when((t == j) & (i == load_step))
            def _(j=j):
                landed_from_sibling(0, j)
                load_w(shard_of(j), (j + 1) % 2).start()

            @pl.when((t == j + 1) & (i == 0))
            def _(j=j):
                load_w(shard_of(j), (j + 1) % 2).wait()

        @pl.when((t == 0) & (i == 0))
        def _():
            h_first(0)

        @pl.when((t == 0) & (i + 1 < nt))
        def _():
            h_start((i + 1) % 2, i + 1)

        @pl.when(t == 0)
        def _():
            h_wait(i % 2)
            h = hbuf[i % 2]
            r = lax.rsqrt(_mean(h * h) + EPS)
            hnbuf[slot] = ((h * r) * g_ref[...]).astype(BF16)
            r_ref[...] = r
            hn_out(slot, i).start()

        @pl.when(t > 0)
        def _():
            hn_in(slot, i).wait()

        @pl.when(((t == 0) & (i > 0)) | ((t == 1) & (i == 0)))
        def _():
            hn_out(1 - slot, jnp.where(i > 0, i - 1, nt - 1)).wait()

        last = (t == N_CHIPS - 1) & (i == nt - 1)

        @pl.when(((t > 0) | (i == nt - 1)) & jnp.logical_not(last))
        def _():
            hn_in(1 - slot, jnp.where(i == nt - 1, 0, i + 1)).start()

        proj_ref[...] = _dot(hnbuf[slot], wbuf[t % 2])

        @pl.when(last)
        def _():
            for k in range(1, n):
                for j in range(3):
                    land(k, j)
            for k in range(1, n):
                for j in range(3):
                    landed_from_sibling(k, j)
            for k in range(n):
                for j in range(3):
                    send(k, j).wait_send()
                    forward(k, j).wait_send()

    def frozen(t, i):
        return jnp.where(t == 0, i, nt - 1)

    any_spec = pl.BlockSpec(memory_space=pl.ANY)
    return pl.pallas_call(
        body, name="in_proj_gather",
        grid_spec=pltpu.PrefetchScalarGridSpec(
            num_scalar_prefetch=1, grid=(N_CHIPS, nt),
            in_specs=[any_spec, any_spec, pl.BlockSpec((1, d), lambda t, i, o: (0, 0)),
                      any_spec, any_spec, any_spec],
            out_specs=[pl.BlockSpec((tm, ns), lambda t, i, o: (i, o[t])),
                       pl.BlockSpec((tm, 1), lambda t, i, o: (frozen(t, i), 0)),
                       any_spec, any_spec, any_spec, any_spec],
            scratch_shapes=[pltpu.VMEM((2, d, ns), BF16), pltpu.VMEM((2, tm, d), BF16), pltpu.VMEM((2, tm, d), F32),
                            pltpu.SemaphoreType.DMA((6 * n,)), pltpu.SemaphoreType.DMA((6 * n,)),
                            pltpu.SemaphoreType.DMA((2,)), pltpu.SemaphoreType.DMA((2,)),
                            pltpu.SemaphoreType.DMA((2,)), pltpu.SemaphoreType.DMA]),
        out_shape=[jax.ShapeDtypeStruct((lp, N_CHIPS * ns), F32),
                   jax.ShapeDtypeStruct((lp, 1), F32),
                   jax.ShapeDtypeStruct((lp, d), BF16)]
                  + [jax.ShapeDtypeStruct(a.shape, a.dtype) for a in gathered],
        input_output_aliases={4: 3, 5: 4, 6: 5},
        compiler_params=_params(2, 48))(order, x, lead, ln_g, w4, pw4, wo4)


def _chunks_per_step(nch):
    return next(n for n in (5, 4, 3, 2, 1) if nch % n == 0)


def _ret_fwd(proj, rope, gn_g, consts):
    lp = proj.shape[0]
    w = gn_g.shape[1]
    hd = w // RET_HEADS
    nch = lp // CHUNK
    cps = _chunks_per_step(nch)
    tr = cps * CHUNK
    dmask, qd, kd, cd = consts

    def body(q_ref, k_ref, v_ref, g_ref, ca_ref, sa_ref, cb_ref, sb_ref, gn_ref, m_ref, qd_ref, kd_ref, cd_ref,
             y_ref, ssave_ref, ops_ref, sc_ref, out_ref, s_scr):
        @pl.when(pl.program_id(0) == 0)
        def _():
            s_scr[...] = jnp.zeros_like(s_scr)

        def chunk(cc, carry):
            rows = pl.ds(pl.multiple_of(cc * CHUNK, CHUNK), CHUNK)
            cos_t, sin_t = _rope_of_chunk(ca_ref[cc], sa_ref[cc], cb_ref[...], sb_ref[...])
            for h in range(RET_HEADS):
                sl = slice(h * hd, (h + 1) * hd)
                qr = _rope(q_ref[rows, sl], cos_t, sin_t)
                kr = _rope(k_ref[rows, sl], cos_t, sin_t) * (hd ** -0.5)
                qb, kb = qr.astype(BF16), kr.astype(BF16)
                qdq, kdk = (qr * qd_ref[h]).astype(BF16), (kr * kd_ref[h]).astype(BF16)
                vb = v_ref[rows, sl].astype(BF16)
                for k, operand in enumerate((qb, kb, qdq, kdk, vb)):
                    ops_ref[k, rows, sl] = operand
                scb = (_dot_nt(qb, kb) * m_ref[h]).astype(BF16)
                sc_ref[cc, h] = scb
                state = s_scr[h]
                sb = state.astype(BF16)
                ssave_ref[cc, h] = sb
                out = _dot(scb, vb) + _dot(qdq, sb)
                out_ref[rows, sl] = out
                s_scr[h] = state * cd_ref[h] + _dot_tn(kdk, vb)
                dev = out - _mean(out)
                yn = dev * lax.rsqrt(_mean(dev * dev) + EPS)
                g = g_ref[rows, sl]
                y_ref[rows, sl] = ((yn * gn_ref[:, sl]) * (g * _sigmoid(g))).astype(BF16)
            return carry

        lax.fori_loop(0, cps, chunk, 0)

    def col(j):
        return pl.BlockSpec((tr, w), lambda i: (i, j))

    def whole(a):
        return pl.BlockSpec(a.shape, lambda i: (0,) * a.ndim)

    return pl.pallas_call(
        body, name="ret_fwd", grid=(nch // cps,),
        in_specs=[col(0), col(1), col(2), col(3),
                  pl.BlockSpec((cps, 1, hd // 2), lambda i: (i, 0, 0)),
                  pl.BlockSpec((cps, 1, hd // 2), lambda i: (i, 0, 0)),
                  whole(rope[2]), whole(rope[3]),
                  whole(gn_g), whole(dmask), whole(qd), whole(kd), whole(cd)],
        out_specs=[pl.BlockSpec((tr, w), lambda i: (i, 0)),
                   pl.BlockSpec((cps, RET_HEADS, hd, hd), lambda i: (i, 0, 0, 0)),
                   pl.BlockSpec((5, tr, w), lambda i: (0, i, 0)),
                   pl.BlockSpec((cps, RET_HEADS, CHUNK, CHUNK), lambda i: (i, 0, 0, 0)),
                   pl.BlockSpec((tr, w), lambda i: (i, 0))],
        out_shape=[jax.ShapeDtypeStruct((lp, w), BF16),
                   jax.ShapeDtypeStruct((nch, RET_HEADS, hd, hd), BF16),
                   jax.ShapeDtypeStruct((5, lp, w), BF16),
                   jax.ShapeDtypeStruct((nch, RET_HEADS, CHUNK, CHUNK), BF16),
                   jax.ShapeDtypeStruct((lp, w), F32)],
        scratch_shapes=[pltpu.VMEM((RET_HEADS, hd, hd), F32)],
        compiler_params=_params(1, 56))(proj, proj, proj, proj, *rope, gn_g, dmask, qd, kd, cd)


def _tap_groups(start, flip):
    groups = {}
    for j in range(CONV_K):
        o = start + (CONV_K - 1 - j if flip else j)
        groups.setdefault(o % 8, []).append((o // 8, j))
    return groups


def _shift_up(win, s):
    return win if s == 0 else pltpu.roll(win, win.shape[0] - s, axis=0)


def _dw_taps(src_ref, w_ref, dst_ref, bias, *, rows, start, flip, rb):
    cw = dst_ref.shape[1]
    lb = min(128, cw)
    groups = _tap_groups(start, flip)

    def rb_body(r, carry):
        base = pl.multiple_of(r * rb, rb)
        for cb in range(cw // lb):
            ls = slice(cb * lb, (cb + 1) * lb)
            win = src_ref[pl.ds(base, rb + HALO), ls]
            acc = jnp.zeros((rb, lb), F32) if bias is None else jnp.broadcast_to(bias[:, ls], (rb, lb))
            for s, taps in groups.items():
                ws = _shift_up(win, s)
                for a, j in taps:
                    acc = acc + ws[8 * a:8 * a + rb, :] * w_ref[j:j + 1, ls]
            dst_ref[pl.ds(base, rb), ls] = acc
        return carry

    lax.fori_loop(0, rows // rb, rb_body, 0)


def _conv_fwd(proj, dw_w, dw_b, cln_g, cln_b, pw_w, pw_b):
    lp = proj.shape[0]
    cw = dw_b.shape[1]
    tm = _row_tile(lp, 640)
    rb = _row_tile(tm, 64)

    def body(a_ref, b_ref, gc_ref, w_ref, wb_ref, lg_ref, lb_ref, pw_ref, pb_ref, y_ref, u1_ref, buf):
        @pl.when(pl.program_id(0) == 0)
        def _():
            buf[0:HALO, :] = jnp.zeros((HALO, cw), F32)

        buf[HALO:HALO + tm, :] = a_ref[...] * _sigmoid(b_ref[...])
        _dw_taps(buf, w_ref, u1_ref, wb_ref[...], rows=tm, start=HALO - (CONV_K - 1), flip=False, rb=rb)
        buf[0:HALO, :] = buf[tm:tm + HALO, :]
        u1 = u1_ref[...]
        dev = u1 - _mean(u1)
        z = dev * lax.rsqrt(_mean(dev * dev) + EPS) * lg_ref[...] + lb_ref[...]
        u3 = (z * _sigmoid(z)).astype(BF16)
        u4 = _dot(u3, pw_ref[...]) + pb_ref[...]
        gc = gc_ref[...]
        y_ref[...] = (u4 * (gc * _sigmoid(gc))).astype(BF16)

    def col(j):
        return pl.BlockSpec((tm, cw), lambda i: (i, j))

    def whole(a):
        return pl.BlockSpec(a.shape, lambda i: (0,) * a.ndim)

    return pl.pallas_call(
        body, name="conv_fwd", grid=(lp // tm,),
        in_specs=[col(4), col(5), col(6), whole(dw_w), whole(dw_b), whole(cln_g), whole(cln_b),
                  whole(pw_w), whole(pw_b)],
        out_specs=[pl.BlockSpec((tm, cw), lambda i: (i, 0)), pl.BlockSpec((tm, cw), lambda i: (i, 0))],
        out_shape=[jax.ShapeDtypeStruct((lp, cw), BF16), jax.ShapeDtypeStruct((lp, cw), F32)],
        scratch_shapes=[pltpu.VMEM((tm + HALO, cw), F32)],
        compiler_params=_params(1, 48))(proj, proj, proj, dw_w, dw_b, cln_g, cln_b, pw_w, pw_b)


def _out_proj_loss(yr, yc, x, lead, tgt, zero_lead, w_out, final_g):
    seq, d = x.shape
    lp = CHUNK + seq
    w = yr.shape[1]
    tm = _row_tile(lp, 320)
    nt = lp // tm
    assert tm > CHUNK

    def body(yr_ref, yc_ref, x_hbm, lead_hbm, t_hbm, zlead_hbm, w_ref, fg_ref, dh2_ref, dy_ref, loss_ref, dfg_ref,
             hbuf, tbuf, hsems, tsems):
        i = pl.program_id(0)
        slot = i % 2
        streams = (_padded_tile_stream(x_hbm, lead_hbm, hbuf, hsems, tm),
                   _padded_tile_stream(t_hbm, zlead_hbm, tbuf, tsems, tm))

        @pl.when(i == 0)
        def _():
            loss_ref[...] = jnp.zeros_like(loss_ref)
            dfg_ref[...] = jnp.zeros_like(dfg_ref)
            for start_first, _, _ in streams:
                start_first(0)

        @pl.when(i + 1 < nt)
        def _():
            for _, start, _ in streams:
                start(1 - slot, i + 1)

        for _, _, wait in streams:
            wait(slot)

        h2 = hbuf[slot] + (_dot(yr_ref[...], w_ref[0:w, :]) + _dot(yc_ref[...], w_ref[w:2 * w, :]))
        r2 = lax.rsqrt(_mean(h2 * h2) + EPS)
        h2n = h2 * r2
        fg = fg_ref[...]
        rows = i * tm + lax.broadcasted_iota(jnp.int32, (tm, 1), 0)
        err = jnp.where(rows >= CHUNK, h2n * fg - tbuf[slot], 0.0)
        loss_ref[...] += _colsum(err * err)
        dout = err * (1.0 / d)
        dfg_ref[...] += _colsum(dout * h2n)
        dz = dout * fg
        dh2 = r2 * (dz - h2n * _mean(dz * h2n))
        dh2_ref[...] = dh2
        db = dh2.astype(BF16)
        dy_ref[:, 0:w] = _dot_nt(db, w_ref[0:w, :])
        dy_ref[:, w:2 * w] = _dot_nt(db, w_ref[w:2 * w, :])

    def row(cols):
        return pl.BlockSpec((tm, cols), lambda i: (i, 0))

    any_spec = pl.BlockSpec(memory_space=pl.ANY)
    return pl.pallas_call(
        body, name="out_proj_loss", grid=(nt,),
        in_specs=[row(w), row(w), any_spec, any_spec, any_spec, any_spec,
                  pl.BlockSpec(memory_space=pltpu.VMEM),
                  pl.BlockSpec((1, d), lambda i: (0, 0))],
        out_specs=[row(d), row(2 * w), pl.BlockSpec((1, d), lambda i: (0, 0)),
                   pl.BlockSpec((1, d), lambda i: (0, 0))],
        out_shape=[jax.ShapeDtypeStruct((lp, d), F32), jax.ShapeDtypeStruct((lp, 2 * w), F32),
                   jax.ShapeDtypeStruct((1, d), F32), jax.ShapeDtypeStruct((1, d), F32)],
        scratch_shapes=[pltpu.VMEM((2, tm, d), F32), pltpu.VMEM((2, tm, d), F32),
                        pltpu.SemaphoreType.DMA((2,)), pltpu.SemaphoreType.DMA((2,))],
        compiler_params=_params(1, 56))(yr, yc, x, lead, tgt, zero_lead, w_out, final_g)


def _dw_out(yr, yc, dh2):
    lp, d = dh2.shape
    w = yr.shape[1]
    tm = _row_tile(lp, 1664)
    nb = 2
    dn = d // nb

    def body(yr_ref, yc_ref, d_ref, o_ref):
        @pl.when(pl.program_id(1) == 0)
        def _():
            o_ref[...] = jnp.zeros_like(o_ref)

        db = d_ref[...].astype(BF16)
        o_ref[0:w, :] += _dot_tn(yr_ref[...], db)
        o_ref[w:2 * w, :] += _dot_tn(yc_ref[...], db)

    return pl.pallas_call(
        body, name="dw_out", grid=(nb, lp // tm),
        in_specs=[pl.BlockSpec((tm, w), lambda n, i: (i, 0)),
                  pl.BlockSpec((tm, w), lambda n, i: (i, 0)),
                  pl.BlockSpec((tm, dn), lambda n, i: (i, n))],
        out_specs=pl.BlockSpec((2 * w, dn), lambda n, i: (0, n)),
        out_shape=jax.ShapeDtypeStruct((2 * w, d), F32),
        compiler_params=_params(2, 52))(yr, yc, dh2)


def _ret_bwd(proj, dy, ssave, ops, scores, out_pre, rope, gn_g, consts):
    lp = proj.shape[0]
    w = gn_g.shape[1]
    hd = w // RET_HEADS
    nch = lp // CHUNK
    cps = _chunks_per_step(nch)
    tr = cps * CHUNK
    dmask, qd, kd, cd = consts

    def body(g_ref, dy_ref, ss_ref, ops_ref, sc_ref, out_ref, ca_ref, sa_ref, cb_ref, sb_ref, gn_ref, m_ref, qd_ref,
             kd_ref, cd_ref, dp_ref, dgn_ref, ds_scr):
        @pl.when(pl.program_id(0) == 0)
        def _():
            ds_scr[...] = jnp.zeros_like(ds_scr)
            dgn_ref[...] = jnp.zeros_like(dgn_ref)

        def chunk(n, carry):
            cc = cps - 1 - n
            rows = pl.ds(pl.multiple_of(cc * CHUNK, CHUNK), CHUNK)
            cos_t, sin_t = _rope_of_chunk(ca_ref[cc], sa_ref[cc], cb_ref[...], sb_ref[...])
            for h in range(RET_HEADS):
                sl = slice(h * hd, (h + 1) * hd)
                qb, kb, qdq, kdk, vb = (ops_ref[k, rows, sl] for k in range(5))
                sb = ss_ref[cc, h]
                scb = sc_ref[cc, h]
                mask = m_ref[h]
                qdec, kdec = qd_ref[h], kd_ref[h]
                out = out_ref[rows, sl]
                dev = out - _mean(out)
                rstd = lax.rsqrt(_mean(dev * dev) + EPS)
                yn = dev * rstd
                g = g_ref[rows, sl]
                sg = _sigmoid(g)
                gng = gn_ref[:, sl]
                dyv = dy_ref[rows, sl]
                dgr = dyv * (yn * gng) * _dsilu(g, sg)
                silu_g = g * sg
                dgn_ref[:, sl] += _colsum(dyv * yn * silu_g)
                dyn = dyv * gng * silu_g
                dout = rstd * (dyn - _mean(dyn) - yn * _mean(dyn * yn))
                dob = dout.astype(BF16)
                by_v_and_s = _dot_nt(dob, jnp.concatenate([vb, sb], axis=0))
                dscb = (by_v_and_s[:, 0:CHUNK] * mask).astype(BF16)
                dstate = ds_scr[h]
                dsb = dstate.astype(BF16)
                dq = _dot(dscb, kb) + by_v_and_s[:, CHUNK:] * qdec
                dk = _dot_tn(dscb, qb) + _dot_nt(vb, dsb) * kdec
                onto_do = _dot_tn(jnp.concatenate([scb, qdq], axis=1), dob)
                dv = onto_do[0:CHUNK] + _dot(kdk, dsb)
                ds_scr[h] = dstate * cd_ref[h] + onto_do[CHUNK:]
                dp_ref[rows, 0 * w + h * hd:0 * w + (h + 1) * hd] = _rope_t(dq, cos_t, sin_t).astype(BF16)
                dp_ref[rows, 1 * w + h * hd:1 * w + (h + 1) * hd] = (
                    _rope_t(dk, cos_t, sin_t) * (hd ** -0.5)).astype(BF16)
                dp_ref[rows, 2 * w + h * hd:2 * w + (h + 1) * hd] = dv.astype(BF16)
                dp_ref[rows, 3 * w + h * hd:3 * w + (h + 1) * hd] = dgr.astype(BF16)
            return carry

        lax.fori_loop(0, cps, chunk, 0)

    def rev(i):
        return nch // cps - 1 - i

    def col(j):
        return pl.BlockSpec((tr, w), lambda i: (rev(i), j))

    def whole(a):
        return pl.BlockSpec(a.shape, lambda i: (0,) * a.ndim)

    return pl.pallas_call(
        body, name="ret_bwd", grid=(nch // cps,),
        in_specs=[col(3),
                  pl.BlockSpec((tr, w), lambda i: (rev(i), 0)),
                  pl.BlockSpec((cps, RET_HEADS, hd, hd), lambda i: (rev(i), 0, 0, 0)),
                  pl.BlockSpec((5, tr, w), lambda i: (0, rev(i), 0)),
                  pl.BlockSpec((cps, RET_HEADS, CHUNK, CHUNK), lambda i: (rev(i), 0, 0, 0)),
                  pl.BlockSpec((tr, w), lambda i: (rev(i), 0)),
                  pl.BlockSpec((cps, 1, hd // 2), lambda i: (rev(i), 0, 0)),
                  pl.BlockSpec((cps, 1, hd // 2), lambda i: (rev(i), 0, 0)),
                  whole(rope[2]), whole(rope[3]),
                  whole(gn_g), whole(dmask), whole(qd), whole(kd), whole(cd)],
        out_specs=[pl.BlockSpec((tr, 4 * w), lambda i: (rev(i), 0)),
                   pl.BlockSpec((1, w), lambda i: (0, 0))],
        out_shape=[jax.ShapeDtypeStruct((lp, 7 * w), BF16), jax.ShapeDtypeStruct((1, w), F32)],
        scratch_shapes=[pltpu.VMEM((RET_HEADS, hd, hd), F32)],
        compiler_params=_params(1, 56))(proj, dy, ssave, ops, scores, out_pre, *rope, gn_g, dmask, qd, kd, cd)


def _conv_bwd_pw(dy, proj, u1, cln_g, cln_b, pw_w, pw_b, dproj):
    lp, cw = u1.shape
    tm = _row_tile(lp, 640)

    def body(dy_ref, gc_ref, u1_ref, lg_ref, lb_ref, pw_ref, pb_ref, dp_in, dp_ref, du1_ref, dpw_ref,
             dpb_ref, dlg_ref, dlb_ref):
        del dp_in

        @pl.when(pl.program_id(0) == 0)
        def _():
            dpw_ref[...] = jnp.zeros_like(dpw_ref)
            dpb_ref[...] = jnp.zeros_like(dpb_ref)
            dlg_ref[...] = jnp.zeros_like(dlg_ref)
            dlb_ref[...] = jnp.zeros_like(dlb_ref)

        u1 = u1_ref[...]
        dev = u1 - _mean(u1)
        rstd = lax.rsqrt(_mean(dev * dev) + EPS)
        u1n = dev * rstd
        lg = lg_ref[...]
        z = u1n * lg + lb_ref[...]
        sz = _sigmoid(z)
        u3b = (z * sz).astype(BF16)
        u4 = _dot(u3b, pw_ref[...]) + pb_ref[...]
        gc = gc_ref[...]
        sgc = _sigmoid(gc)
        dyc = dy_ref[...]
        du4 = dyc * (gc * sgc)
        dp_ref[...] = (dyc * u4 * _dsilu(gc, sgc)).astype(BF16)
        du4b = du4.astype(BF16)
        dpb_ref[...] += _colsum(du4)
        dpw_ref[...] += _dot_tn(u3b, du4b)
        dz = _dot_nt(du4b, pw_ref[...]) * _dsilu(z, sz)
        dlg_ref[...] += _colsum(dz * u1n)
        dlb_ref[...] += _colsum(dz)
        dn = dz * lg
        du1_ref[...] = rstd * (dn - _mean(dn) - u1n * _mean(dn * u1n))

    def row(j):
        return pl.BlockSpec((tm, cw), lambda i: (i, j))

    def whole(a):
        return pl.BlockSpec(a.shape, lambda i: (0,) * a.ndim)

    def acc(r):
        return pl.BlockSpec((r, cw), lambda i: (0, 0))

    return pl.pallas_call(
        body, name="conv_bwd_pw", grid=(lp // tm,),
        in_specs=[row(1), row(6), row(0), whole(cln_g), whole(cln_b), whole(pw_w), whole(pw_b),
                  pl.BlockSpec(memory_space=pl.ANY)],
        out_specs=[row(6), row(0), acc(cw), acc(1), acc(1), acc(1)],
        out_shape=[jax.ShapeDtypeStruct(dproj.shape, dproj.dtype), jax.ShapeDtypeStruct((lp, cw), F32),
                   jax.ShapeDtypeStruct((cw, cw), F32), jax.ShapeDtypeStruct((1, cw), F32),
                   jax.ShapeDtypeStruct((1, cw), F32), jax.ShapeDtypeStruct((1, cw), F32)],
        input_output_aliases={7: 0},
        compiler_params=_params(1, 48))(dy, proj, u1, cln_g, cln_b, pw_w, pw_b, dproj)


def _conv_bwd_dw(du1, proj, dw_w, dproj):
    lp, cw = du1.shape
    tm = _row_tile(lp, 640)
    rb = _row_tile(tm, 64)
    nt = lp // tm
    hb = tm // HALO

    def body(a_ref, b_ref, du_ref, nx_ref, w_ref, dp_in, dp_ref, dww_ref, dwb_ref, ubuf, dbuf, du0, acc):
        del dp_in
        i = pl.program_id(0)

        @pl.when(i == 0)
        def _():
            ubuf[0:HALO, :] = jnp.zeros((HALO, cw), F32)
            acc[...] = jnp.zeros_like(acc)
            dwb_ref[...] = jnp.zeros_like(dwb_ref)

        a = a_ref[...]
        sb = _sigmoid(b_ref[...])
        ubuf[HALO:HALO + tm, :] = a * sb
        du = du_ref[...]
        dbuf[0:tm, :] = du
        dbuf[tm:tm + HALO, :] = jnp.where(i == nt - 1, 0.0, nx_ref[...])
        dwb_ref[...] += _colsum(du)
        _dw_taps(dbuf, w_ref, du0, None, rows=tm, start=0, flip=True, rb=rb)
        d0 = du0[...]
        dp_ref[:, 0:cw] = (d0 * sb).astype(BF16)
        dp_ref[:, cw:2 * cw] = (d0 * a * sb * (1.0 - sb)).astype(BF16)

        lb = min(128, cw)
        groups = _tap_groups(HALO - (CONV_K - 1), False)

        def rb_body(r, carry):
            base = pl.multiple_of(r * rb, rb)
            for cb in range(cw // lb):
                ls = slice(cb * lb, (cb + 1) * lb)
                win = ubuf[pl.ds(base, rb + HALO), ls]
                dv = dbuf[pl.ds(base, rb), ls]
                for s, taps in groups.items():
                    ws = _shift_up(win, s)
                    for a, j in taps:
                        prod = dv * ws[8 * a:8 * a + rb, :]
                        acc[8 * j:8 * j + 8, ls] += jnp.sum(prod.reshape(rb // 8, 8, lb), axis=0)
            return carry

        lax.fori_loop(0, tm // rb, rb_body, 0)
        ubuf[0:HALO, :] = ubuf[tm:tm + HALO, :]

        @pl.when(i == nt - 1)
        def _():
            for j in range(CONV_K):
                dww_ref[j:j + 1, :] = _colsum(acc[8 * j:8 * j + 8, :])
            dww_ref[CONV_K:HALO, :] = jnp.zeros((HALO - CONV_K, cw), F32)

    def col(j):
        return pl.BlockSpec((tm, cw), lambda i: (i, j))

    return pl.pallas_call(
        body, name="conv_bwd_dw", grid=(nt,),
        in_specs=[col(4), col(5), col(0),
                  pl.BlockSpec((HALO, cw), lambda i: (jnp.minimum((i + 1) * hb, nt * hb - 1), 0)),
                  pl.BlockSpec(dw_w.shape, lambda i: (0, 0)),
                  pl.BlockSpec(memory_space=pl.ANY)],
        out_specs=[pl.BlockSpec((tm, 2 * cw), lambda i: (i, 2)),
                   pl.BlockSpec((HALO, cw), lambda i: (0, 0)),
                   pl.BlockSpec((1, cw), lambda i: (0, 0))],
        out_shape=[jax.ShapeDtypeStruct(dproj.shape, dproj.dtype), jax.ShapeDtypeStruct((HALO, cw), F32),
                   jax.ShapeDtypeStruct((1, cw), F32)],
        scratch_shapes=[pltpu.VMEM((tm + HALO, cw), F32), pltpu.VMEM((tm + HALO, cw), F32),
                        pltpu.VMEM((tm, cw), F32), pltpu.VMEM((8 * HALO, cw), F32)],
        input_output_aliases={5: 0},
        compiler_params=_params(1, 56))(proj, proj, du1, du1, dw_w, dproj)


def _dw_in(hn, dproj, ns, others):
    lp, d = hn.shape
    tm = _row_tile(lp, 1664)
    nt = lp // tm
    mb = 512 if d % 512 == 0 else d
    n = len(others)
    halves = [d // 2] + [g.shape[1] // 2 for g in others]

    def body(hn_ref, dp_ref, *refs):
        other_refs, o_hbm, recv_refs = refs[:n], refs[n], refs[n + 1:2 * n + 2]
        acc, sem, send_sems, recv_sems = refs[2 * n + 2:]
        s, i = pl.program_id(0), pl.program_id(1)
        x, y, c = _mesh_pos()

        def to_sibling(src, dst, k):
            return pltpu.make_async_remote_copy(
                src_ref=src, dst_ref=dst, send_sem=send_sems.at[k], recv_sem=recv_sems.at[k],
                device_id=(x, y, 1 - c), device_id_type=MESH)

        def shard_half(p):
            return to_sibling(o_hbm.at[p, pl.ds((1 - c) * halves[0], halves[0])], recv_refs[0].at[p], p)

        def other_halves(k):
            return to_sibling(other_refs[k].at[:, pl.ds((1 - c) * halves[1 + k], halves[1 + k])],
                              recv_refs[1 + k], N_CHIPS + k)

        @pl.when((s == 0) & (i == 0))
        def _():
            for k in range(n):
                other_halves(k).start()

        @pl.when(i == 0)
        def _():
            acc[...] = jnp.zeros_like(acc)

        for m in range(d // mb):
            rows = slice(m * mb, (m + 1) * mb)
            acc[rows, :] += _dot_tn(hn_ref[:, rows], dp_ref[...])

        @pl.when(i == nt - 1)
        def _():
            cp = pltpu.make_async_copy(acc, o_hbm.at[s], sem)
            cp.start()
            cp.wait()
            shard_half(s).start()

        @pl.when((s == N_CHIPS - 1) & (i == nt - 1))
        def _():
            for p in range(N_CHIPS):
                shard_half(p).wait()
            for k in range(n):
                other_halves(k).wait()

    any_spec = pl.BlockSpec(memory_space=pl.ANY)
    outs = pl.pallas_call(
        body, name="dw_in", grid=(N_CHIPS, nt),
        in_specs=[pl.BlockSpec((tm, d), lambda s, i: (i, 0)),
                  pl.BlockSpec((tm, ns), lambda s, i: (i, s))] + [any_spec] * n,
        out_specs=[any_spec] * (n + 2),
        out_shape=[jax.ShapeDtypeStruct((N_CHIPS, d, ns), F32), jax.ShapeDtypeStruct((N_CHIPS, d // 2, ns), F32)]
                  + [jax.ShapeDtypeStruct((N_CHIPS, g.shape[1] // 2) + g.shape[2:], g.dtype) for g in others],
        scratch_shapes=[pltpu.VMEM((d, ns), F32), pltpu.SemaphoreType.DMA,
                        pltpu.SemaphoreType.DMA((N_CHIPS + n,)), pltpu.SemaphoreType.DMA((N_CHIPS + n,))],
        compiler_params=_params(2, 56))(hn, dproj, *others)
    return outs[0], outs[1:]


def _in_proj_bwd(dproj, w4, x, lead, r1, dh2, ln_g, cs):
    seq, d = x.shape
    lp = CHUNK + seq
    ns = w4.shape[2]
    tm = _row_tile(lp, 320)
    nt = lp // tm
    n = len(cs)
    assert tm > CHUNK

    def body(dp_ref, w_ref, x_hbm, lead_hbm, r_ref, d2_ref, g_ref, *refs):
        cs_refs, (dh_ref, dlg_ref), rb_refs = refs[:n], refs[n:n + 2], refs[n + 2:2 * n + 2]
        hbuf, h_sems, send_sems, recv_sems = refs[2 * n + 2:]
        i = pl.program_id(0)
        slot = i % 2
        x, y, c = _mesh_pos()
        h_first, h_start, h_wait = _padded_tile_stream(x_hbm, lead_hbm, hbuf, h_sems, tm)

        @pl.when(i == 0)
        def _():
            h_first(0)

        @pl.when(i + 1 < nt)
        def _():
            h_start(1 - slot, i + 1)

        h_wait(slot)

        def exchange():
            return [pltpu.make_async_remote_copy(
                src_ref=cs_refs[k].at[2 * chip[0] + chip[1]], dst_ref=rb_refs[k].at[j],
                send_sem=send_sems.at[3 * k + j], recv_sem=recv_sems.at[3 * k + j],
                device_id=(*chip, c), device_id_type=MESH)
                for k in range(n) for j, chip in enumerate(_other_chips(x, y))]

        @pl.when(i == 0)
        def _():
            dlg_ref[...] = jnp.zeros_like(dlg_ref)
            for cp in exchange():
                cp.start()

        dhn = _dot_nt(dp_ref[:, 0:ns], w_ref[0])
        for s in range(1, N_CHIPS):
            dhn = dhn + _dot_nt(dp_ref[:, s * ns:(s + 1) * ns], w_ref[s])
        r = r_ref[...]
        hn0 = hbuf[slot] * r
        dlg_ref[...] += _colsum(dhn * hn0)
        t = dhn * g_ref[...]
        dh_ref[...] = d2_ref[...] + r * (t - hn0 * _mean(t * hn0))

        @pl.when(i == nt - 1)
        def _():
            for cp in exchange():
                cp.wait()

    def row(cols):
        return pl.BlockSpec((tm, cols), lambda i: (i, 0))

    any_spec = pl.BlockSpec(memory_space=pl.ANY)
    outs = pl.pallas_call(
        body, name="in_proj_bwd", grid=(nt,),
        in_specs=[row(N_CHIPS * ns), pl.BlockSpec(memory_space=pltpu.VMEM),
                  any_spec, any_spec, row(1), row(d), pl.BlockSpec((1, d), lambda i: (0, 0))] + [any_spec] * n,
        out_specs=[row(d), pl.BlockSpec((1, d), lambda i: (0, 0))] + [any_spec] * n,
        out_shape=[jax.ShapeDtypeStruct((lp, d), F32), jax.ShapeDtypeStruct((1, d), F32)]
                  + [jax.ShapeDtypeStruct((3,) + a.shape[1:], a.dtype) for a in cs],
        scratch_shapes=[pltpu.VMEM((2, tm, d), F32), pltpu.SemaphoreType.DMA((2,)),
                        pltpu.SemaphoreType.DMA((3 * n,)), pltpu.SemaphoreType.DMA((3 * n,))],
        compiler_params=_params(1, 58))(dproj, w4, x, lead, r1, dh2, ln_g, *cs)
    return outs[0], outs[1], outs[2:]


def _mesh_pos():
    return lax.axis_index("x"), lax.axis_index("y"), lax.axis_index("c")


def _other_chips(x, y):
    return [(1 - x, y), (x, 1 - y), (1 - x, 1 - y)]


def _gather_shards(shards):
    n = len(shards)
    halves = [a.shape[0] // 2 for a in shards]

    def body(*refs):
        ins, outs = refs[:n], refs[n:2 * n]
        send_sems, recv_sems, loc_sems = refs[2 * n:]
        x, y, c = _mesh_pos()
        me, sibling = (x, y, c), (x, y, 1 - c)
        my_s = 2 * x + y
        chips = _other_chips(x, y)

        def half(k, s, cc):
            return outs[k].at[s, pl.ds(cc * halves[k], halves[k])]

        def rcopy(k, j, src, dst, to):
            return pltpu.make_async_remote_copy(
                src_ref=src, dst_ref=dst, send_sem=send_sems.at[6 * k + j], recv_sem=recv_sems.at[6 * k + j],
                device_id=to, device_id_type=MESH)

        local = [pltpu.make_async_copy(ins[k], outs[k].at[my_s], loc_sems.at[k]) for k in range(n)]
        for cp in local:
            cp.start()
        started = []
        for k in range(n):
            for j, chip in enumerate(chips):
                cp = rcopy(k, j, ins[k].at[pl.ds(c * halves[k], halves[k])], half(k, my_s, c), (*chip, c))
                cp.start()
                started.append(cp)
        for j, chip in enumerate(chips):
            s_j = 2 * chip[0] + chip[1]
            for k in range(n):
                rcopy(k, j, half(k, s_j, c), half(k, s_j, c), me).wait_recv()
                cp = rcopy(k, 3 + j, half(k, s_j, c), half(k, s_j, c), sibling)
                cp.start()
                started.append(cp)
        for j, chip in enumerate(chips):
            s_j = 2 * chip[0] + chip[1]
            for k in range(n):
                rcopy(k, 3 + j, half(k, s_j, 1 - c), half(k, s_j, 1 - c), me).wait_recv()
        for cp in started:
            cp.wait_send()
        for cp in local:
            cp.wait()

    return pl.pallas_call(
        body, name="gather_weights",
        in_specs=[pl.BlockSpec(memory_space=pl.ANY)] * n,
        out_specs=[pl.BlockSpec(memory_space=pl.ANY)] * n,
        out_shape=[jax.ShapeDtypeStruct((N_CHIPS,) + a.shape, a.dtype) for a in shards],
        scratch_shapes=[pltpu.SemaphoreType.DMA((6 * n,)), pltpu.SemaphoreType.DMA((6 * n,)),
                        pltpu.SemaphoreType.DMA((n,))],
    )(*shards)


def _rs_pair_share(fulls):
    n = len(fulls)
    halves = [f.shape[0] // 2 for f in fulls]

    def body(*refs):
        outs = refs[n:2 * n]
        send_sems, recv_sems = refs[2 * n:]
        x, y, c = _mesh_pos()

        def copy(k, cc, to):
            rows = outs[k].at[pl.ds(cc * halves[k], halves[k])]
            return pltpu.make_async_remote_copy(
                src_ref=rows, dst_ref=rows, send_sem=send_sems.at[k], recv_sem=recv_sems.at[k],
                device_id=to, device_id_type=MESH)

        cps = [copy(k, c, (x, y, 1 - c)) for k in range(n)]
        for cp in cps:
            cp.start()
        for k in range(n):
            copy(k, 1 - c, (x, y, c)).wait_recv()
        for cp in cps:
            cp.wait_send()

    return pl.pallas_call(
        body, name="rs_pair_share",
        in_specs=[pl.BlockSpec(memory_space=pl.ANY)] * n,
        out_specs=[pl.BlockSpec(memory_space=pl.ANY)] * n,
        out_shape=[jax.ShapeDtypeStruct(f.shape, f.dtype) for f in fulls],
        input_output_aliases={k: k for k in range(n)},
        scratch_shapes=[pltpu.SemaphoreType.DMA((n,)), pltpu.SemaphoreType.DMA((n,))],
    )(*fulls)


def _pair_sum(g, recv, pos, name):
    _, rows, cols = g.shape
    h = rows // 2
    tr = _row_tile(h, 256)
    nb = h // tr

    def body(pos_ref, g_ref, r_ref, o_ref, own_ref):
        total = g_ref[0] + r_ref[0]
        o_ref[0] = total.astype(BF16)

        @pl.when(pl.program_id(1) == pos_ref[1])
        def _():
            own_ref[...] = total

    return pl.pallas_call(
        body, name=name,
        grid_spec=pltpu.PrefetchScalarGridSpec(
            num_scalar_prefetch=1, grid=(nb, N_CHIPS),
            in_specs=[pl.BlockSpec((1, tr, cols), lambda r, s, pos_ref: (s, pos_ref[0] * nb + r, 0)),
                      pl.BlockSpec((1, tr, cols), lambda r, s, pos_ref: (s, r, 0))],
            out_specs=[pl.BlockSpec((1, tr, cols), lambda r, s, pos_ref: (s, r, 0)),
                       pl.BlockSpec((tr, cols), lambda r, s, pos_ref: (r, 0))]),
        out_shape=[jax.ShapeDtypeStruct((N_CHIPS, h, cols), BF16), jax.ShapeDtypeStruct((h, cols), F32)],
        compiler_params=_params(2, 32))(pos, g, recv)


def _chip_sum(own, rb, pos, name):
    h, cols = own.shape
    tr = _row_tile(h, 256)
    nb = h // tr

    def body(pos_ref, c_ref, r_ref, o_ref):
        del pos_ref
        o_ref[...] = ((c_ref[...] + r_ref[0].astype(F32)) + r_ref[1].astype(F32)) + r_ref[2].astype(F32)

    return pl.pallas_call(
        body, name=name,
        grid_spec=pltpu.PrefetchScalarGridSpec(
            num_scalar_prefetch=1, grid=(nb,),
            in_specs=[pl.BlockSpec((tr, cols), lambda r, pos_ref: (r, 0)),
                      pl.BlockSpec((3, tr, cols), lambda r, pos_ref: (0, r, 0))],
            out_specs=pl.BlockSpec((tr, cols), lambda r, pos_ref: (pos_ref[0] * nb + r, 0))),
        out_shape=jax.ShapeDtypeStruct((2 * h, cols), F32),
        compiler_params=_params(1, 32))(pos, own, rb)


def _adamw_big(w, g, m, v, name):
    rows, cols = w.shape
    tr = _row_tile(rows, 256)

    def body(w_ref, g_ref, m_ref, v_ref, go_ref, d_ref, nm_ref, nv_ref):
        g = g_ref[...]
        go_ref[...] = g
        d_ref[...], nm_ref[...], nv_ref[...] = _adamw(w_ref[...], g, m_ref[...], v_ref[...])

    spec = pl.BlockSpec((tr, cols), lambda i: (i, 0))
    return pl.pallas_call(
        body, name=name, grid=(rows // tr,),
        in_specs=[spec] * 4, out_specs=[spec] * 4,
        out_shape=[jax.ShapeDtypeStruct((rows, cols), F32)] * 4,
        compiler_params=_params(1, 48))(w, g, m, v)


def _gather_small(loss, dfg, dlg, dgn, ddwb, dclg, dclb, dpwb, ddww, dmeta):
    d = loss.shape[1]
    w = dgn.shape[1]
    wc, mc = ddww.shape[1] // N_CHIPS, dmeta.shape[1] // N_CHIPS

    def body(loss_ref, dfg_ref, dlg_ref, dgn_ref, ddwb_ref, dclg_ref, dclb_ref, dpwb_ref, ddww_ref, dmeta_ref,
             gs_ref, gd_ref, gm_ref, send_sems, recv_sems, loc_sems):
        x, y, c = _mesh_pos()
        me = 4 * x + 2 * y + c
        gs_ref[me, 0:1, :] = loss_ref[...]
        gs_ref[me, 1:2, :] = dfg_ref[...]
        gs_ref[me, 2:3, :] = dlg_ref[...]
        gs_ref[me, 3:4, 0:w] = dgn_ref[...]
        gs_ref[me, 3:4, w:2 * w] = ddwb_ref[...]
        gs_ref[me, 4:5, 0:w] = dclg_ref[...]
        gs_ref[me, 4:5, w:2 * w] = dclb_ref[...]
        gs_ref[me, 5:6, 0:w] = dpwb_ref[...]
        gs_ref[me, 5:6, w:2 * w] = jnp.zeros((1, d - w), F32)
        gs_ref[me, 6:8, :] = jnp.zeros((2, d), F32)
        bufs = (gs_ref, gd_ref, gm_ref)

        def mine_for(k, shard):
            if k == 0:
                return gs_ref.at[me]
            ref, width = ((ddww_ref, wc), (dmeta_ref, mc))[k - 1]
            return ref.at[:, pl.ds(pl.multiple_of(shard * width, width), width)]

        def peer(j):
            return (1 - x if j & 4 else x), (1 - y if j & 2 else y), (1 - c if j & 1 else c)

        def copy(k, j, src, slot, to):
            return pltpu.make_async_remote_copy(
                src_ref=src, dst_ref=bufs[k].at[slot],
                send_sem=send_sems.at[7 * k + j - 1], recv_sem=recv_sems.at[7 * k + j - 1],
                device_id=to, device_id_type=MESH)

        own = [pltpu.make_async_copy(mine_for(k, 2 * x + y), bufs[k].at[me], loc_sems.at[k - 1]) for k in (1, 2)]
        for cp in own:
            cp.start()
        cps = []
        for k in range(3):
            for j in range(1, N_DEV):
                px, py, pc = peer(j)
                cp = copy(k, j, mine_for(k, 2 * px + py), me, (px, py, pc))
                cp.start()
                cps.append(cp)
        for k in range(3):
            for j in range(1, N_DEV):
                px, py, pc = peer(j)
                slot = 4 * px + 2 * py + pc
                copy(k, j, bufs[k].at[slot], slot, (x, y, c)).wait_recv()
        for cp in cps:
            cp.wait_send()
        for cp in own:
            cp.wait()

    vm = pl.BlockSpec(memory_space=pltpu.VMEM)
    return pl.pallas_call(
        body, name="gather_small",
        in_specs=[vm] * 10, out_specs=[vm] * 3,
        out_shape=[jax.ShapeDtypeStruct((N_DEV, 8, d), F32),
                   jax.ShapeDtypeStruct((N_DEV, ddww.shape[0], wc), F32),
                   jax.ShapeDtypeStruct((N_DEV, dmeta.shape[0], mc), F32)],
        scratch_shapes=[pltpu.SemaphoreType.DMA((21,)), pltpu.SemaphoreType.DMA((21,)),
                        pltpu.SemaphoreType.DMA((2,))],
    )(loss, dfg, dlg, dgn, ddwb, dclg, dclb, dpwb, ddww, dmeta)


def _small_update(gs, gd, gm, weights, ms, vs):
    d = gs.shape[2]
    w = d // 2
    n = len(weights)

    def body(gs_ref, gd_ref, gm_ref, *refs):
        w_refs, m_refs, v_refs = refs[:n], refs[n:2 * n], refs[2 * n:3 * n]
        loss_ref = refs[3 * n]
        g_refs = refs[3 * n + 1:4 * n + 1]
        d_refs = refs[4 * n + 1:5 * n + 1]
        nm_refs = refs[5 * n + 1:6 * n + 1]
        nv_refs = refs[6 * n + 1:7 * n + 1]

        def total(ref):
            t = ref[0]
            for dev in range(1, N_DEV):
                t = t + ref[dev]
            return t

        packed = total(gs_ref)
        loss_ref[...] = jnp.sum(packed[0:1, :], axis=1, keepdims=True) * (0.5 / d)
        grads = [packed[2:3, :], packed[1:2, :], packed[3:4, 0:w], packed[3:4, w:2 * w], packed[4:5, 0:w],
                 packed[4:5, w:2 * w], packed[5:6, 0:w], total(gd_ref), total(gm_ref)]
        for k in range(n):
            g = grads[k]
            g_refs[k][...] = g
            d_refs[k][...], nm_refs[k][...], nv_refs[k][...] = _adamw(w_refs[k][...], g, m_refs[k][...], v_refs[k][...])

    def whole(shape):
        return pl.BlockSpec(shape, lambda i: (0,) * len(shape))

    shapes = [a.shape for a in weights]
    in_specs = [whole(gs.shape), whole(gd.shape), whole(gm.shape)] + [whole(s) for s in shapes] * 3
    out_specs = [whole((1, 1))] + [whole(s) for s in shapes] * 4
    out_shape = [jax.ShapeDtypeStruct((1, 1), F32)] + [jax.ShapeDtypeStruct(s, F32) for s in shapes] * 4
    outs = pl.pallas_call(
        body, name="small_update", grid=(1,), in_specs=in_specs, out_specs=out_specs, out_shape=out_shape,
        compiler_params=_params(1, 32))(gs, gd, gm, *weights, *ms, *vs)
    loss = outs[0]
    return loss, outs[1:n + 1], outs[n + 1:2 * n + 1], outs[2 * n + 1:3 * n + 1], outs[3 * n + 1:4 * n + 1]


def kernel(x, meta_tokens, ln_g, w_in, ret_gn_g, conv_dw_w, conv_dw_b, conv_ln_g, conv_ln_b, conv_pw_w, conv_pw_b, w_out, final_g, loss_target, m_meta_tokens, m_ln_g, m_w_in, m_ret_gn_g, m_conv_dw_w, m_conv_dw_b, m_conv_ln_g, m_conv_ln_b, m_conv_pw_w, m_conv_pw_b, m_w_out, m_final_g, v_meta_tokens, v_ln_g, v_w_in, v_ret_gn_g, v_conv_dw_w, v_conv_dw_b, v_conv_ln_g, v_conv_ln_b, v_conv_pw_w, v_conv_pw_b, v_w_out, v_final_g):
    seq, d = x.shape[1], x.shape[2]
    w = ret_gn_g.shape[1]
    hd = w // RET_HEADS
    lp = CHUNK + seq
    ns = w_in.shape[2]
    mx, my, mc = lax.axis_index("x"), lax.axis_index("y"), lax.axis_index("c")
    my_s = 2 * mx + my
    s_arr = my_s.astype(jnp.int32).reshape(1)
    pos = jnp.stack([mc, my_s]).astype(jnp.int32)
    order = jnp.stack([my_s, 2 * (1 - mx) + my, 2 * mx + (1 - my), 2 * (1 - mx) + (1 - my)]).astype(jnp.int32)

    dw_pad = jnp.pad(conv_dw_w[0], ((0, HALO - CONV_K), (0, 0)))
    dw4, meta4 = _gather_shards([dw_pad, meta_tokens])
    dw_full = dw4.transpose(1, 0, 2).reshape(HALO, w)
    meta_full = meta4.transpose(1, 0, 2).reshape(N_META, d)

    lead = jnp.concatenate([jnp.zeros((LEAD, d), F32), meta_full], axis=0)
    zero_lead = jnp.zeros((CHUNK, d), F32)
    consts = _ret_consts()
    rope = _rope_tables(lp // CHUNK, hd // 2)
    fg2 = final_g.reshape(1, d)

    proj, r1, hn, w4, pw4, wo4 = _in_proj_gather(
        order, x[0], lead, ln_g, _cast_into_gathered(w_in[0], s_arr, "cast_w_in"),
        _cast_into_gathered(conv_pw_w[0], s_arr, "cast_pw_w"), _cast_into_gathered(w_out[0], s_arr, "cast_w_out"))
    pw_full = pw4.reshape(w, w)
    wo_full = wo4.reshape(2 * w, d)
    y_ret, ssave, ret_ops, ret_scores, ret_out = _ret_fwd(proj, rope, ret_gn_g, consts)
    y_conv, u1 = _conv_fwd(proj, dw_full, conv_dw_b, conv_ln_g, conv_ln_b, pw_full, conv_pw_b)
    dh2, dy, loss_l, dfg = _out_proj_loss(y_ret, y_conv, x[0], lead, loss_target[0], zero_lead, wo_full, fg2)

    g_wo = _dw_out(y_ret, y_conv, dh2)
    dproj, dgn = _ret_bwd(proj, dy, ssave, ret_ops, ret_scores, ret_out, rope, ret_gn_g, consts)
    dproj, du1, g_pw, dpwb, dclg, dclb = _conv_bwd_pw(dy, proj, u1, conv_ln_g, conv_ln_b, pw_full, conv_pw_b, dproj)
    dproj, ddww, ddwb = _conv_bwd_dw(du1, proj, dw_full, dproj)
    g_wo4 = g_wo.reshape(N_CHIPS, (2 * w) // N_CHIPS, d)
    g_pw4 = g_pw.reshape(N_CHIPS, w // N_CHIPS, w)
    g_win, recv = _dw_in(hn, dproj, ns, [g_wo4, g_pw4])
    gs = [g_win, g_wo4, g_pw4]
    names = ("w_in", "w_out", "pw_w")
    sums = [_pair_sum(g, r, pos, "pair_sum_" + nm) for g, r, nm in zip(gs, recv, names)]
    dh, dlg, rb = _in_proj_bwd(dproj, w4, x[0], lead, r1, dh2, ln_g, [cs_ for cs_, _ in sums])
    grad_x = dh[CHUNK:][None]
    dmeta = dh[LEAD:CHUNK]
    fulls = [_chip_sum(own, r, pos, "chip_sum_" + nm) for (_, own), r, nm in zip(sums, rb, names)]
    grad_w_in, grad_w_out, grad_pw = _rs_pair_share(fulls)
    grad_w_in, d_win, nm_win, nv_win = _adamw_big(w_in[0], grad_w_in, m_w_in[0], v_w_in[0], "adamw_w_in")
    grad_w_out, d_wo, nm_wo, nv_wo = _adamw_big(w_out[0], grad_w_out, m_w_out[0], v_w_out[0], "adamw_w_out")
    grad_pw, d_pw, nm_pw, nv_pw = _adamw_big(conv_pw_w[0], grad_pw, m_conv_pw_w[0], v_conv_pw_w[0], "adamw_pw_w")

    gsm, gdm, gmm = _gather_small(loss_l, dfg, dlg, dgn, ddwb, dclg, dclb, dpwb, ddww, dmeta)

    def pad_dw(a):
        return jnp.pad(a[0], ((0, HALO - CONV_K), (0, 0)))

    small_w = [ln_g, fg2, ret_gn_g, conv_dw_b, conv_ln_g, conv_ln_b, conv_pw_b, dw_pad, meta_tokens]
    small_m = [m_ln_g, m_final_g.reshape(1, d), m_ret_gn_g, m_conv_dw_b, m_conv_ln_g, m_conv_ln_b, m_conv_pw_b,
               pad_dw(m_conv_dw_w), m_meta_tokens]
    small_v = [v_ln_g, v_final_g.reshape(1, d), v_ret_gn_g, v_conv_dw_b, v_conv_ln_g, v_conv_ln_b, v_conv_pw_b,
               pad_dw(v_conv_dw_w), v_meta_tokens]
    loss, sg, sd, snm, snv = _small_update(gsm, gdm, gmm, small_w, small_m, small_v)

    def assemble(small, big_in, big_pw, big_out):
        ln, fg, gn, dwb, clg, clb, pwb, dww, meta = small
        return (meta, ln, big_in[None], gn, dww[:CONV_K][None], dwb, clg, clb, big_pw[None], pwb, big_out[None],
                fg.reshape(d))

    return (loss.reshape(()), grad_x,
            *assemble(sg, grad_w_in, grad_pw, grad_w_out),
            *assemble(sd, d_win, d_pw, d_wo),
            *assemble(snm, nm_win, nm_pw, nm_wo),
            *assemble(snv, nv_win, nv_pw, nv_wo))
```

```python
import functools

import jax
import jax.numpy as jnp
from jax import lax
from jax.experimental import pallas as pl
from jax.experimental.pallas import tpu as pltpu

F32 = jnp.float32
BF16 = jnp.bfloat16
MESH = pl.DeviceIdType.MESH

N_META = 16
CHUNK = 128
LEAD = (-N_META) % CHUNK
RET_HEADS = 4
CONV_K = 31
HALO = 32
ROPE_BASE = 10000.0
EPS = 1e-6
N_CHIPS = 4
N_DEV = 8

ADAM_LR = 0.001
ADAM_B1 = 0.9
ADAM_B2 = 0.999
ADAM_EPS = 1e-08
ADAM_WD = 0.01
ADAM_STEP = 10

MIB = 2 ** 20


def _params(n_grid_axes, vmem_mib):
    return pltpu.CompilerParams(dimension_semantics=("arbitrary",) * n_grid_axes,
                                vmem_limit_bytes=vmem_mib * MIB)


def _row_tile(n, pref):
    for t in (1664, 1280, 1024, 640, 512, 384, 320, 256, 128, 64, 32, 16, 8):
        if t <= pref and n % t == 0:
            return t
    raise ValueError(f"no row tile for {n}")


def _dot(a, b):
    return jnp.dot(a, b, preferred_element_type=F32)


def _dot_nt(a, b):
    return lax.dot_general(a, b, (((1,), (1,)), ((), ())), preferred_element_type=F32)


def _dot_tn(a, b):
    return lax.dot_general(a, b, (((0,), (0,)), ((), ())), preferred_element_type=F32)


def _sigmoid(x):
    return jax.nn.sigmoid(x)


def _dsilu(x, s):
    return s * (1.0 + x * (1.0 - s))


def _mean(x):
    return jnp.mean(x, axis=-1, keepdims=True)


def _colsum(x):
    return jnp.sum(x, axis=0, keepdims=True)


def _rope(x, cos, sin):
    half = x.shape[-1] // 2
    x1, x2 = x[:, :half], x[:, half:]
    return jnp.concatenate([x1 * cos - x2 * sin, x1 * sin + x2 * cos], axis=-1)


def _rope_t(d, cos, sin):
    half = d.shape[-1] // 2
    d1, d2 = d[:, :half], d[:, half:]
    return jnp.concatenate([d1 * cos + d2 * sin, d2 * cos - d1 * sin], axis=-1)


def _adamw(w, g, m, v):
    m = ADAM_B1 * m + (1.0 - ADAM_B1) * g
    v = ADAM_B2 * v + (1.0 - ADAM_B2) * (g * g)
    m_hat = m / (1.0 - ADAM_B1 ** ADAM_STEP)
    v_hat = v / (1.0 - ADAM_B2 ** ADAM_STEP)
    delta = -ADAM_LR * (m_hat / (jnp.sqrt(v_hat) + ADAM_EPS) + ADAM_WD * w)
    return delta, m, v


def _ret_consts():
    h = jnp.arange(RET_HEADS, dtype=F32)
    log_g = jnp.log(1.0 - jnp.exp2(-5.0 - h))
    idx = jnp.arange(CHUNK, dtype=F32)
    rel = idx[:, None] - idx[None, :]
    dmask = jnp.where(rel[None] >= 0, jnp.exp(jnp.maximum(rel, 0.0)[None] * log_g[:, None, None]), 0.0)
    qd = jnp.exp((idx[None, :] + 1.0) * log_g[:, None])[:, :, None]
    kd = jnp.exp((CHUNK - 1.0 - idx[None, :]) * log_g[:, None])[:, :, None]
    cd = jnp.exp(CHUNK * log_g)[:, None, None]
    return dmask, qd, kd, cd


def _rope_tables(nch, half):
    inv_freq = ROPE_BASE ** (-jnp.arange(half, dtype=F32) / half)
    start = (jnp.arange(nch, dtype=F32) * float(CHUNK) - float(LEAD))[:, None] * inv_freq[None, :]
    within = jnp.arange(CHUNK, dtype=F32)[:, None] * inv_freq[None, :]
    return jnp.cos(start)[:, None, :], jnp.sin(start)[:, None, :], jnp.cos(within), jnp.sin(within)


def _rope_of_chunk(ca, sa, cb, sb):
    return ca * cb - sa * sb, sa * cb + ca * sb


def _padded_tile_stream(x_hbm, lead_hbm, buf, sems, tm):
    def start_first(slot):
        pltpu.make_async_copy(lead_hbm, buf.at[slot, pl.ds(0, CHUNK)], sems.at[slot]).start()
        pltpu.make_async_copy(x_hbm.at[pl.ds(0, tm - CHUNK)], buf.at[slot, pl.ds(CHUNK, tm - CHUNK)],
                              sems.at[slot]).start()

    def start(slot, tile):
        pltpu.make_async_copy(x_hbm.at[pl.ds(tile * tm - CHUNK, tm)], buf.at[slot], sems.at[slot]).start()

    def wait(slot):
        pltpu.make_async_copy(x_hbm.at[pl.ds(0, tm)], buf.at[slot], sems.at[slot]).wait()

    return start_first, start, wait


def _cast_into_gathered(a, s_arr, name):
    rows, cols = a.shape
    tr = _row_tile(rows, 256)

    def body(s_ref, a_ref, o_ref):
        del s_ref
        o_ref[0] = a_ref[...].astype(BF16)

    return pl.pallas_call(
        body, name=name,
        grid_spec=pltpu.PrefetchScalarGridSpec(
            num_scalar_prefetch=1, grid=(rows // tr,),
            in_specs=[pl.BlockSpec((tr, cols), lambda i, s_ref: (i, 0))],
            out_specs=pl.BlockSpec((1, tr, cols), lambda i, s_ref: (s_ref[0], i, 0))),
        out_shape=jax.ShapeDtypeStruct((N_CHIPS, rows, cols), BF16),
        compiler_params=_params(1, 32))(s_arr, a)


def _in_proj_gather(order, x, lead, ln_g, w4, pw4, wo4):
    seq, d = x.shape
    lp = CHUNK + seq
    ns = w4.shape[2]
    tm = _row_tile(lp, 640)
    nt = lp // tm
    assert nt >= 2, "the hn write-back of a row tile is waited for one step later, before any pass re-reads it"
    land_step, load_step = max(nt - 3, 0), max(nt - 2, 0)
    gathered = (w4, pw4, wo4)
    halves = [a.shape[1] // 2 for a in gathered]
    n = len(gathered)

    def body(order_ref, x_hbm, lead_hbm, g_ref, w_in, pw_in, wo_in, proj_ref, r_ref, hn_hbm, w_out, pw_out,
             wo_out, wbuf, hnbuf, hbuf, send_sems, recv_sems, hn_out_sems, hn_in_sems, h_sems, w_sem):
        del order_ref, w_in, pw_in, wo_in
        t, i = pl.program_id(0), pl.program_id(1)
        slot = (t * nt + i) % 2
        h_first, h_start, h_wait = _padded_tile_stream(x_hbm, lead_hbm, hbuf, h_sems, tm)
        x, y, c = _mesh_pos()
        me, sibling = (x, y, c), (x, y, 1 - c)
        my_s = 2 * x + y
        chips = _other_chips(x, y)
        outs = (w_out, pw_out, wo_out)

        def half(k, s, cc):
            return outs[k].at[s, pl.ds(cc * halves[k], halves[k])]

        def rcopy(k, j, rows, to):
            return pltpu.make_async_remote_copy(
                src_ref=rows, dst_ref=rows, send_sem=send_sems.at[6 * k + j], recv_sem=recv_sems.at[6 * k + j],
                device_id=to, device_id_type=MESH)

        def send(k, j):
            return rcopy(k, j, half(k, my_s, c), (*chips[j], c))

        def shard_of(j):
            return 2 * chips[j][0] + chips[j][1]

        def forward(k, j):
            return rcopy(k, 3 + j, half(k, shard_of(j), c), sibling)

        def land(k, j):
            rcopy(k, j, half(k, shard_of(j), c), me).wait_recv()
            forward(k, j).start()

        def landed_from_sibling(k, j):
            rcopy(k, 3 + j, half(k, shard_of(j), 1 - c), me).wait_recv()

        def load_w(s, wslot):
            return pltpu.make_async_copy(w_out.at[s], wbuf.at[wslot], w_sem)

        def hn_out(sl, row_tile):
            return pltpu.make_async_copy(hnbuf.at[sl], hn_hbm.at[pl.ds(row_tile * tm, tm)], hn_out_sems.at[sl])

        def hn_in(sl, row_tile):
            return pltpu.make_async_copy(hn_hbm.at[pl.ds(row_tile * tm, tm)], hnbuf.at[sl], hn_in_sems.at[sl])

        @pl.when((t == 0) & (i == 0))
        def _():
            for j in range(2):
                send(0, j).start()
            load_w(my_s, 0).start()
            load_w(my_s, 0).wait()

        @pl.when((t == 1) & (i == 0))
        def _():
            send(0, 2).start()
            for k in range(1, n):
                for jj in range(3):
                    send(k, jj).start()

        for j in range(3):
            @pl.when((t == j) & (i == land_step))
            def _(j=j):
                land(0, j)

            @pl.when((t == j) & (i == load_step))
            def _(j=j):
                landed_from_sibling(0, j)
                load_w(shard_of(j), (j + 1) % 2).start()

            @pl.when((t == j + 1) & (i == 0))
            def _(j=j):
                load_w(shard_of(j), (j + 1) % 2).wait()

        @pl.when((t == 0) & (i == 0))
        def _():
            h_first(0)

        @pl.when((t == 0) & (i + 1 < nt))
        def _():
            h_start((i + 1) % 2, i + 1)

        @pl.when(t == 0)
        def _():
            h_wait(i % 2)
            h = hbuf[i % 2]
            r = lax.rsqrt(_mean(h * h) + EPS)
            hnbuf[slot] = ((h * r) * g_ref[...]).astype(BF16)
            r_ref[...] = r
            hn_out(slot, i).start()

        @pl.when(t > 0)
        def _():
            hn_in(slot, i).wait()

        @pl.when(((t == 0) & (i > 0)) | ((t == 1) & (i == 0)))
        def _():
            hn_out(1 - slot, jnp.where(i > 0, i - 1, nt - 1)).wait()

        last = (t == N_CHIPS - 1) & (i == nt - 1)

        @pl.when(((t > 0) | (i == nt - 1)) & jnp.logical_not(last))
        def _():
            hn_in(1 - slot, jnp.where(i == nt - 1, 0, i + 1)).start()

        proj_ref[...] = _dot(hnbuf[slot], wbuf[t % 2])

        @pl.when(last)
        def _():
            for k in range(1, n):
                for j in range(3):
                    land(k, j)
            for k in range(1, n):
                for j in range(3):
                    landed_from_sibling(k, j)
            for k in range(n):
                for j in range(3):
                    send(k, j).wait_send()
                    forward(k, j).wait_send()

    def frozen(t, i):
        return jnp.where(t == 0, i, nt - 1)

    any_spec = pl.BlockSpec(memory_space=pl.ANY)
    return pl.pallas_call(
        body, name="in_proj_gather",
        grid_spec=pltpu.PrefetchScalarGridSpec(
            num_scalar_prefetch=1, grid=(N_CHIPS, nt),
            in_specs=[any_spec, any_spec, pl.BlockSpec((1, d), lambda t, i, o: (0, 0)),
                      any_spec, any_spec, any_spec],
            out_specs=[pl.BlockSpec((tm, ns), lambda t, i, o: (i, o[t])),
                       pl.BlockSpec((tm, 1), lambda t, i, o: (frozen(t, i), 0)),
                       any_spec, any_spec, any_spec, any_spec],
            scratch_shapes=[pltpu.VMEM((2, d, ns), BF16), pltpu.VMEM((2, tm, d), BF16), pltpu.VMEM((2, tm, d), F32),
                            pltpu.SemaphoreType.DMA((6 * n,)), pltpu.SemaphoreType.DMA((6 * n,)),
                            pltpu.SemaphoreType.DMA((2,)), pltpu.SemaphoreType.DMA((2,)),
                            pltpu.SemaphoreType.DMA((2,)), pltpu.SemaphoreType.DMA]),
        out_shape=[jax.ShapeDtypeStruct((lp, N_CHIPS * ns), F32),
                   jax.ShapeDtypeStruct((lp, 1), F32),
                   jax.ShapeDtypeStruct((lp, d), BF16)]
                  + [jax.ShapeDtypeStruct(a.shape, a.dtype) for a in gathered],
        input_output_aliases={4: 3, 5: 4, 6: 5},
        compiler_params=_params(2, 48))(order, x, lead, ln_g, w4, pw4, wo4)


def _chunks_per_step(nch):
    return next(n for n in (5, 4, 3, 2, 1) if nch % n == 0)


def _ret_fwd(proj, rope, gn_g, consts):
    lp = proj.shape[0]
    w = gn_g.shape[1]
    hd = w // RET_HEADS
    nch = lp // CHUNK
    cps = _chunks_per_step(nch)
    tr = cps * CHUNK
    dmask, qd, kd, cd = consts

    def body(q_ref, k_ref, v_ref, g_ref, ca_ref, sa_ref, cb_ref, sb_ref, gn_ref, m_ref, qd_ref, kd_ref, cd_ref,
             y_ref, ssave_ref, ops_ref, sc_ref, out_ref, s_scr):
        @pl.when(pl.program_id(0) == 0)
        def _():
            s_scr[...] = jnp.zeros_like(s_scr)

        def chunk(cc, carry):
            rows = pl.ds(pl.multiple_of(cc * CHUNK, CHUNK), CHUNK)
            cos_t, sin_t = _rope_of_chunk(ca_ref[cc], sa_ref[cc], cb_ref[...], sb_ref[...])
            for h in range(RET_HEADS):
                sl = slice(h * hd, (h + 1) * hd)
                qr = _rope(q_ref[rows, sl], cos_t, sin_t)
                kr = _rope(k_ref[rows, sl], cos_t, sin_t) * (hd ** -0.5)
                qb, kb = qr.astype(BF16), kr.astype(BF16)
                qdq, kdk = (qr * qd_ref[h]).astype(BF16), (kr * kd_ref[h]).astype(BF16)
                vb = v_ref[rows, sl].astype(BF16)
                for k, operand in enumerate((qb, kb, qdq, kdk, vb)):
                    ops_ref[k, rows, sl] = operand
                scb = (_dot_nt(qb, kb) * m_ref[h]).astype(BF16)
                sc_ref[cc, h] = scb
                state = s_scr[h]
                sb = state.astype(BF16)
                ssave_ref[cc, h] = sb
                out = _dot(scb, vb) + _dot(qdq, sb)
                out_ref[rows, sl] = out
                s_scr[h] = state * cd_ref[h] + _dot_tn(kdk, vb)
                dev = out - _mean(out)
                yn = dev * lax.rsqrt(_mean(dev * dev) + EPS)
                g = g_ref[rows, sl]
                y_ref[rows, sl] = ((yn * gn_ref[:, sl]) * (g * _sigmoid(g))).astype(BF16)
            return carry

        lax.fori_loop(0, cps, chunk, 0)

    def col(j):
        return pl.BlockSpec((tr, w), lambda i: (i, j))

    def whole(a):
        return pl.BlockSpec(a.shape, lambda i: (0,) * a.ndim)

    return pl.pallas_call(
        body, name="ret_fwd", grid=(nch // cps,),
        in_specs=[col(0), col(1), col(2), col(3),
                  pl.BlockSpec((cps, 1, hd // 2), lambda i: (i, 0, 0)),
                  pl.BlockSpec((cps, 1, hd // 2), lambda i: (i, 0, 0)),
                  whole(rope[2]), whole(rope[3]),
                  whole(gn_g), whole(dmask), whole(qd), whole(kd), whole(cd)],
        out_specs=[pl.BlockSpec((tr, w), lambda i: (i, 0)),
                   pl.BlockSpec((cps, RET_HEADS, hd, hd), lambda i: (i, 0, 0, 0)),
                   pl.BlockSpec((5, tr, w), lambda i: (0, i, 0)),
                   pl.BlockSpec((cps, RET_HEADS, CHUNK, CHUNK), lambda i: (i, 0, 0, 0)),
                   pl.BlockSpec((tr, w), lambda i: (i, 0))],
        out_shape=[jax.ShapeDtypeStruct((lp, w), BF16),
                   jax.ShapeDtypeStruct((nch, RET_HEADS, hd, hd), BF16),
                   jax.ShapeDtypeStruct((5, lp, w), BF16),
                   jax.ShapeDtypeStruct((nch, RET_HEADS, CHUNK, CHUNK), BF16),
                   jax.ShapeDtypeStruct((lp, w), F32)],
        scratch_shapes=[pltpu.VMEM((RET_HEADS, hd, hd), F32)],
        compiler_params=_params(1, 56))(proj, proj, proj, proj, *rope, gn_g, dmask, qd, kd, cd)


def _tap_groups(start, flip):
    groups = {}
    for j in range(CONV_K):
        o = start + (CONV_K - 1 - j if flip else j)
        groups.setdefault(o % 8, []).append((o // 8, j))
    return groups


def _shift_up(win, s):
    return win if s == 0 else pltpu.roll(win, win.shape[0] - s, axis=0)


def _dw_taps(src_ref, w_ref, dst_ref, bias, *, rows, start, flip, rb):
    cw = dst_ref.shape[1]
    lb = min(128, cw)
    groups = _tap_groups(start, flip)

    def rb_body(r, carry):
        base = pl.multiple_of(r * rb, rb)
        for cb in range(cw // lb):
            ls = slice(cb * lb, (cb + 1) * lb)
            win = src_ref[pl.ds(base, rb + HALO), ls]
            acc = jnp.zeros((rb, lb), F32) if bias is None else jnp.broadcast_to(bias[:, ls], (rb, lb))
            for s, taps in groups.items():
                ws = _shift_up(win, s)
                for a, j in taps:
                    acc = acc + ws[8 * a:8 * a + rb, :] * w_ref[j:j + 1, ls]
            dst_ref[pl.ds(base, rb), ls] = acc
        return carry

    lax.fori_loop(0, rows // rb, rb_body, 0)


def _conv_fwd(proj, dw_w, dw_b, cln_g, cln_b, pw_w, pw_b):
    lp = proj.shape[0]
    cw = dw_b.shape[1]
    tm = _row_tile(lp, 640)
    rb = _row_tile(tm, 128)

    def body(a_ref, b_ref, gc_ref, w_ref, wb_ref, lg_ref, lb_ref, pw_ref, pb_ref, y_ref, u1_ref, buf):
        @pl.when(pl.program_id(0) == 0)
        def _():
            buf[0:HALO, :] = jnp.zeros((HALO, cw), F32)

        buf[HALO:HALO + tm, :] = a_ref[...] * _sigmoid(b_ref[...])
        _dw_taps(buf, w_ref, u1_ref, wb_ref[...], rows=tm, start=HALO - (CONV_K - 1), flip=False, rb=rb)
        buf[0:HALO, :] = buf[tm:tm + HALO, :]
        u1 = u1_ref[...]
        dev = u1 - _mean(u1)
        z = dev * lax.rsqrt(_mean(dev * dev) + EPS) * lg_ref[...] + lb_ref[...]
        u3 = (z * _sigmoid(z)).astype(BF16)
        u4 = _dot(u3, pw_ref[...]) + pb_ref[...]
        gc = gc_ref[...]
        y_ref[...] = (u4 * (gc * _sigmoid(gc))).astype(BF16)

    def col(j):
        return pl.BlockSpec((tm, cw), lambda i: (i, j))

    def whole(a):
        return pl.BlockSpec(a.shape, lambda i: (0,) * a.ndim)

    return pl.pallas_call(
        body, name="conv_fwd", grid=(lp // tm,),
        in_specs=[col(4), col(5), col(6), whole(dw_w), whole(dw_b), whole(cln_g), whole(cln_b),
                  whole(pw_w), whole(pw_b)],
        out_specs=[pl.BlockSpec((tm, cw), lambda i: (i, 0)), pl.BlockSpec((tm, cw), lambda i: (i, 0))],
        out_shape=[jax.ShapeDtypeStruct((lp, cw), BF16), jax.ShapeDtypeStruct((lp, cw), F32)],
        scratch_shapes=[pltpu.VMEM((tm + HALO, cw), F32)],
        compiler_params=_params(1, 48))(proj, proj, proj, dw_w, dw_b, cln_g, cln_b, pw_w, pw_b)


def _out_proj_loss(yr, yc, x, lead, tgt, zero_lead, w_out, final_g):
    seq, d = x.shape
    lp = CHUNK + seq
    w = yr.shape[1]
    tm = _row_tile(lp, 320)
    nt = lp // tm
    assert tm > CHUNK

    def body(yr_ref, yc_ref, x_hbm, lead_hbm, t_hbm, zlead_hbm, w_ref, fg_ref, dh2_ref, dy_ref, loss_ref, dfg_ref,
             hbuf, tbuf, hsems, tsems):
        i = pl.program_id(0)
        slot = i % 2
        streams = (_padded_tile_stream(x_hbm, lead_hbm, hbuf, hsems, tm),
                   _padded_tile_stream(t_hbm, zlead_hbm, tbuf, tsems, tm))

        @pl.when(i == 0)
        def _():
            loss_ref[...] = jnp.zeros_like(loss_ref)
            dfg_ref[...] = jnp.zeros_like(dfg_ref)
            for start_first, _, _ in streams:
                start_first(0)

        @pl.when(i + 1 < nt)
        def _():
            for _, start, _ in streams:
                start(1 - slot, i + 1)

        for _, _, wait in streams:
            wait(slot)

        h2 = hbuf[slot] + (_dot(yr_ref[...], w_ref[0:w, :]) + _dot(yc_ref[...], w_ref[w:2 * w, :]))
        r2 = lax.rsqrt(_mean(h2 * h2) + EPS)
        h2n = h2 * r2
        fg = fg_ref[...]
        rows = i * tm + lax.broadcasted_iota(jnp.int32, (tm, 1), 0)
        err = jnp.where(rows >= CHUNK, h2n * fg - tbuf[slot], 0.0)
        loss_ref[...] += _colsum(err * err)
        dout = err * (1.0 / d)
        dfg_ref[...] += _colsum(dout * h2n)
        dz = dout * fg
        dh2 = r2 * (dz - h2n * _mean(dz * h2n))
        dh2_ref[...] = dh2
        db = dh2.astype(BF16)
        dy_ref[:, 0:w] = _dot_nt(db, w_ref[0:w, :])
        dy_ref[:, w:2 * w] = _dot_nt(db, w_ref[w:2 * w, :])

    def row(cols):
        return pl.BlockSpec((tm, cols), lambda i: (i, 0))

    any_spec = pl.BlockSpec(memory_space=pl.ANY)
    return pl.pallas_call(
        body, name="out_proj_loss", grid=(nt,),
        in_specs=[row(w), row(w), any_spec, any_spec, any_spec, any_spec,
                  pl.BlockSpec(memory_space=pltpu.VMEM),
                  pl.BlockSpec((1, d), lambda i: (0, 0))],
        out_specs=[row(d), row(2 * w), pl.BlockSpec((1, d), lambda i: (0, 0)),
                   pl.BlockSpec((1, d), lambda i: (0, 0))],
        out_shape=[jax.ShapeDtypeStruct((lp, d), F32), jax.ShapeDtypeStruct((lp, 2 * w), F32),
                   jax.ShapeDtypeStruct((1, d), F32), jax.ShapeDtypeStruct((1, d), F32)],
        scratch_shapes=[pltpu.VMEM((2, tm, d), F32), pltpu.VMEM((2, tm, d), F32),
                        pltpu.SemaphoreType.DMA((2,)), pltpu.SemaphoreType.DMA((2,))],
        compiler_params=_params(1, 56))(yr, yc, x, lead, tgt, zero_lead, w_out, final_g)


def _dw_out(yr, yc, dh2):
    lp, d = dh2.shape
    w = yr.shape[1]
    tm = _row_tile(lp, 1664)
    nb = 2
    dn = d // nb

    def body(yr_ref, yc_ref, d_ref, o_ref):
        @pl.when(pl.program_id(1) == 0)
        def _():
            o_ref[...] = jnp.zeros_like(o_ref)

        db = d_ref[...].astype(BF16)
        o_ref[0:w, :] += _dot_tn(yr_ref[...], db)
        o_ref[w:2 * w, :] += _dot_tn(yc_ref[...], db)

    return pl.pallas_call(
        body, name="dw_out", grid=(nb, lp // tm),
        in_specs=[pl.BlockSpec((tm, w), lambda n, i: (i, 0)),
                  pl.BlockSpec((tm, w), lambda n, i: (i, 0)),
                  pl.BlockSpec((tm, dn), lambda n, i: (i, n))],
        out_specs=pl.BlockSpec((2 * w, dn), lambda n, i: (0, n)),
        out_shape=jax.ShapeDtypeStruct((2 * w, d), F32),
        compiler_params=_params(2, 52))(yr, yc, dh2)


def _ret_bwd(proj, dy, ssave, ops, scores, out_pre, rope, gn_g, consts):
    lp = proj.shape[0]
    w = gn_g.shape[1]
    hd = w // RET_HEADS
    nch = lp // CHUNK
    cps = _chunks_per_step(nch)
    tr = cps * CHUNK
    dmask, qd, kd, cd = consts

    def body(g_ref, dy_ref, ss_ref, ops_ref, sc_ref, out_ref, ca_ref, sa_ref, cb_ref, sb_ref, gn_ref, m_ref, qd_ref,
             kd_ref, cd_ref, dp_ref, dgn_ref, ds_scr):
        @pl.when(pl.program_id(0) == 0)
        def _():
            ds_scr[...] = jnp.zeros_like(ds_scr)
            dgn_ref[...] = jnp.zeros_like(dgn_ref)

        def chunk(n, carry):
            cc = cps - 1 - n
            rows = pl.ds(pl.multiple_of(cc * CHUNK, CHUNK), CHUNK)
            cos_t, sin_t = _rope_of_chunk(ca_ref[cc], sa_ref[cc], cb_ref[...], sb_ref[...])
            for h in range(RET_HEADS):
                sl = slice(h * hd, (h + 1) * hd)
                qb, kb, qdq, kdk, vb = (ops_ref[k, rows, sl] for k in range(5))
                sb = ss_ref[cc, h]
                scb = sc_ref[cc, h]
                mask = m_ref[h]
                qdec, kdec = qd_ref[h], kd_ref[h]
                out = out_ref[rows, sl]
                dev = out - _mean(out)
                rstd = lax.rsqrt(_mean(dev * dev) + EPS)
                yn = dev * rstd
                g = g_ref[rows, sl]
                sg = _sigmoid(g)
                gng = gn_ref[:, sl]
                dyv = dy_ref[rows, sl]
                dgr = dyv * (yn * gng) * _dsilu(g, sg)
                silu_g = g * sg
                dgn_ref[:, sl] += _colsum(dyv * yn * silu_g)
                dyn = dyv * gng * silu_g
                dout = rstd * (dyn - _mean(dyn) - yn * _mean(dyn * yn))
                dob = dout.astype(BF16)
                by_v_and_s = _dot_nt(dob, jnp.concatenate([vb, sb], axis=0))
                dscb = (by_v_and_s[:, 0:CHUNK] * mask).astype(BF16)
                dstate = ds_scr[h]
                dsb = dstate.astype(BF16)
                dq = _dot(dscb, kb) + by_v_and_s[:, CHUNK:] * qdec
                dk = _dot_tn(dscb, qb) + _dot_nt(vb, dsb) * kdec
                onto_do = _dot_tn(jnp.concatenate([scb, qdq], axis=1), dob)
                dv = onto_do[0:CHUNK] + _dot(kdk, dsb)
                ds_scr[h] = dstate * cd_ref[h] + onto_do[CHUNK:]
                dp_ref[rows, 0 * w + h * hd:0 * w + (h + 1) * hd] = _rope_t(dq, cos_t, sin_t).astype(BF16)
                dp_ref[rows, 1 * w + h * hd:1 * w + (h + 1) * hd] = (
                    _rope_t(dk, cos_t, sin_t) * (hd ** -0.5)).astype(BF16)
                dp_ref[rows, 2 * w + h * hd:2 * w + (h + 1) * hd] = dv.astype(BF16)
                dp_ref[rows, 3 * w + h * hd:3 * w + (h + 1) * hd] = dgr.astype(BF16)
            return carry

        lax.fori_loop(0, cps, chunk, 0)

    def rev(i):
        return nch // cps - 1 - i

    def col(j):
        return pl.BlockSpec((tr, w), lambda i: (rev(i), j))

    def whole(a):
        return pl.BlockSpec(a.shape, lambda i: (0,) * a.ndim)

    return pl.pallas_call(
        body, name="ret_bwd", grid=(nch // cps,),
        in_specs=[col(3),
                  pl.BlockSpec((tr, w), lambda i: (rev(i), 0)),
                  pl.BlockSpec((cps, RET_HEADS, hd, hd), lambda i: (rev(i), 0, 0, 0)),
                  pl.BlockSpec((5, tr, w), lambda i: (0, rev(i), 0)),
                  pl.BlockSpec((cps, RET_HEADS, CHUNK, CHUNK), lambda i: (rev(i), 0, 0, 0)),
                  pl.BlockSpec((tr, w), lambda i: (rev(i), 0)),
                  pl.BlockSpec((cps, 1, hd // 2), lambda i: (rev(i), 0, 0)),
                  pl.BlockSpec((cps, 1, hd // 2), lambda i: (rev(i), 0, 0)),
                  whole(rope[2]), whole(rope[3]),
                  whole(gn_g), whole(dmask), whole(qd), whole(kd), whole(cd)],
        out_specs=[pl.BlockSpec((tr, 4 * w), lambda i: (rev(i), 0)),
                   pl.BlockSpec((1, w), lambda i: (0, 0))],
        out_shape=[jax.ShapeDtypeStruct((lp, 7 * w), BF16), jax.ShapeDtypeStruct((1, w), F32)],
        scratch_shapes=[pltpu.VMEM((RET_HEADS, hd, hd), F32)],
        compiler_params=_params(1, 56))(proj, dy, ssave, ops, scores, out_pre, *rope, gn_g, dmask, qd, kd, cd)


def _conv_bwd_pw(dy, proj, u1, cln_g, cln_b, pw_w, pw_b, dproj):
    lp, cw = u1.shape
    tm = _row_tile(lp, 640)

    def body(dy_ref, gc_ref, u1_ref, lg_ref, lb_ref, pw_ref, pb_ref, dp_in, dp_ref, du1_ref, dpw_ref,
             dpb_ref, dlg_ref, dlb_ref):
        del dp_in

        @pl.when(pl.program_id(0) == 0)
        def _():
            dpw_ref[...] = jnp.zeros_like(dpw_ref)
            dpb_ref[...] = jnp.zeros_like(dpb_ref)
            dlg_ref[...] = jnp.zeros_like(dlg_ref)
            dlb_ref[...] = jnp.zeros_like(dlb_ref)

        u1 = u1_ref[...]
        dev = u1 - _mean(u1)
        rstd = lax.rsqrt(_mean(dev * dev) + EPS)
        u1n = dev * rstd
        lg = lg_ref[...]
        z = u1n * lg + lb_ref[...]
        sz = _sigmoid(z)
        u3b = (z * sz).astype(BF16)
        u4 = _dot(u3b, pw_ref[...]) + pb_ref[...]
        gc = gc_ref[...]
        sgc = _sigmoid(gc)
        dyc = dy_ref[...]
        du4 = dyc * (gc * sgc)
        dp_ref[...] = (dyc * u4 * _dsilu(gc, sgc)).astype(BF16)
        du4b = du4.astype(BF16)
        dpb_ref[...] += _colsum(du4)
        dpw_ref[...] += _dot_tn(u3b, du4b)
        dz = _dot_nt(du4b, pw_ref[...]) * _dsilu(z, sz)
        dlg_ref[...] += _colsum(dz * u1n)
        dlb_ref[...] += _colsum(dz)
        dn = dz * lg
        du1_ref[...] = rstd * (dn - _mean(dn) - u1n * _mean(dn * u1n))

    def row(j):
        return pl.BlockSpec((tm, cw), lambda i: (i, j))

    def whole(a):
        return pl.BlockSpec(a.shape, lambda i: (0,) * a.ndim)

    def acc(r):
        return pl.BlockSpec((r, cw), lambda i: (0, 0))

    return pl.pallas_call(
        body, name="conv_bwd_pw", grid=(lp // tm,),
        in_specs=[row(1), row(6), row(0), whole(cln_g), whole(cln_b), whole(pw_w), whole(pw_b),
                  pl.BlockSpec(memory_space=pl.ANY)],
        out_specs=[row(6), row(0), acc(cw), acc(1), acc(1), acc(1)],
        out_shape=[jax.ShapeDtypeStruct(dproj.shape, dproj.dtype), jax.ShapeDtypeStruct((lp, cw), F32),
                   jax.ShapeDtypeStruct((cw, cw), F32), jax.ShapeDtypeStruct((1, cw), F32),
                   jax.ShapeDtypeStruct((1, cw), F32), jax.ShapeDtypeStruct((1, cw), F32)],
        input_output_aliases={7: 0},
        compiler_params=_params(1, 48))(dy, proj, u1, cln_g, cln_b, pw_w, pw_b, dproj)


def _conv_bwd_dw(du1, proj, dw_w, dproj):
    lp, cw = du1.shape
    tm = _row_tile(lp, 640)
    rb = _row_tile(tm, 128)
    nt = lp // tm
    hb = tm // HALO

    def body(a_ref, b_ref, du_ref, nx_ref, w_ref, dp_in, dp_ref, dww_ref, dwb_ref, ubuf, dbuf, du0, acc):
        del dp_in
        i = pl.program_id(0)

        @pl.when(i == 0)
        def _():
            ubuf[0:HALO, :] = jnp.zeros((HALO, cw), F32)
            acc[...] = jnp.zeros_like(acc)
            dwb_ref[...] = jnp.zeros_like(dwb_ref)

        a = a_ref[...]
        sb = _sigmoid(b_ref[...])
        ubuf[HALO:HALO + tm, :] = a * sb
        du = du_ref[...]
        dbuf[0:tm, :] = du
        dbuf[tm:tm + HALO, :] = jnp.where(i == nt - 1, 0.0, nx_ref[...])
        dwb_ref[...] += _colsum(du)
        _dw_taps(dbuf, w_ref, du0, None, rows=tm, start=0, flip=True, rb=rb)
        d0 = du0[...]
        dp_ref[:, 0:cw] = (d0 * sb).astype(BF16)
        dp_ref[:, cw:2 * cw] = (d0 * a * sb * (1.0 - sb)).astype(BF16)

        lb = min(128, cw)
        groups = _tap_groups(HALO - (CONV_K - 1), False)

        def rb_body(r, carry):
            base = pl.multiple_of(r * rb, rb)
            for cb in range(cw // lb):
                ls = slice(cb * lb, (cb + 1) * lb)
                win = ubuf[pl.ds(base, rb + HALO), ls]
                dv = dbuf[pl.ds(base, rb), ls]
                for s, taps in groups.items():
                    ws = _shift_up(win, s)
                    for a, j in taps:
                        prod = dv * ws[8 * a:8 * a + rb, :]
                        acc[8 * j:8 * j + 8, ls] += jnp.sum(prod.reshape(rb // 8, 8, lb), axis=0)
            return carry

        lax.fori_loop(0, tm // rb, rb_body, 0)
        ubuf[0:HALO, :] = ubuf[tm:tm + HALO, :]

        @pl.when(i == nt - 1)
        def _():
            for j in range(CONV_K):
                dww_ref[j:j + 1, :] = _colsum(acc[8 * j:8 * j + 8, :])
            dww_ref[CONV_K:HALO, :] = jnp.zeros((HALO - CONV_K, cw), F32)

    def col(j):
        return pl.BlockSpec((tm, cw), lambda i: (i, j))

    return pl.pallas_call(
        body, name="conv_bwd_dw", grid=(nt,),
        in_specs=[col(4), col(5), col(0),
                  pl.BlockSpec((HALO, cw), lambda i: (jnp.minimum((i + 1) * hb, nt * hb - 1), 0)),
                  pl.BlockSpec(dw_w.shape, lambda i: (0, 0)),
                  pl.BlockSpec(memory_space=pl.ANY)],
        out_specs=[pl.BlockSpec((tm, 2 * cw), lambda i: (i, 2)),
                   pl.BlockSpec((HALO, cw), lambda i: (0, 0)),
                   pl.BlockSpec((1, cw), lambda i: (0, 0))],
        out_shape=[jax.ShapeDtypeStruct(dproj.shape, dproj.dtype), jax.ShapeDtypeStruct((HALO, cw), F32),
                   jax.ShapeDtypeStruct((1, cw), F32)],
        scratch_shapes=[pltpu.VMEM((tm + HALO, cw), F32), pltpu.VMEM((tm + HALO, cw), F32),
                        pltpu.VMEM((tm, cw), F32), pltpu.VMEM((8 * HALO, cw), F32)],
        input_output_aliases={5: 0},
        compiler_params=_params(1, 56))(proj, proj, du1, du1, dw_w, dproj)


def _dw_in(hn, dproj, ns, others):
    lp, d = hn.shape
    tm = _row_tile(lp, 1664)
    nt = lp // tm
    mb = 512 if d % 512 == 0 else d
    n = len(others)
    halves = [d // 2] + [g.shape[1] // 2 for g in others]

    def body(hn_ref, dp_ref, *refs):
        other_refs, o_hbm, recv_refs = refs[:n], refs[n], refs[n + 1:2 * n + 2]
        acc, sem, send_sems, recv_sems = refs[2 * n + 2:]
        s, i = pl.program_id(0), pl.program_id(1)
        x, y, c = _mesh_pos()

        def to_sibling(src, dst, k):
            return pltpu.make_async_remote_copy(
                src_ref=src, dst_ref=dst, send_sem=send_sems.at[k], recv_sem=recv_sems.at[k],
                device_id=(x, y, 1 - c), device_id_type=MESH)

        def shard_half(p):
            return to_sibling(o_hbm.at[p, pl.ds((1 - c) * halves[0], halves[0])], recv_refs[0].at[p], p)

        def other_halves(k):
            return to_sibling(other_refs[k].at[:, pl.ds((1 - c) * halves[1 + k], halves[1 + k])],
                              recv_refs[1 + k], N_CHIPS + k)

        @pl.when((s == 0) & (i == 0))
        def _():
            for k in range(n):
                other_halves(k).start()

        @pl.when(i == 0)
        def _():
            acc[...] = jnp.zeros_like(acc)

        for m in range(d // mb):
            rows = slice(m * mb, (m + 1) * mb)
            acc[rows, :] += _dot_tn(hn_ref[:, rows], dp_ref[...])

        @pl.when(i == nt - 1)
        def _():
            cp = pltpu.make_async_copy(acc, o_hbm.at[s], sem)
            cp.start()
            cp.wait()
            shard_half(s).start()

        @pl.when((s == N_CHIPS - 1) & (i == nt - 1))
        def _():
            for p in range(N_CHIPS):
                shard_half(p).wait()
            for k in range(n):
                other_halves(k).wait()

    any_spec = pl.BlockSpec(memory_space=pl.ANY)
    outs = pl.pallas_call(
        body, name="dw_in", grid=(N_CHIPS, nt),
        in_specs=[pl.BlockSpec((tm, d), lambda s, i: (i, 0)),
                  pl.BlockSpec((tm, ns), lambda s, i: (i, s))] + [any_spec] * n,
        out_specs=[any_spec] * (n + 2),
        out_shape=[jax.ShapeDtypeStruct((N_CHIPS, d, ns), F32), jax.ShapeDtypeStruct((N_CHIPS, d // 2, ns), F32)]
                  + [jax.ShapeDtypeStruct((N_CHIPS, g.shape[1] // 2) + g.shape[2:], g.dtype) for g in others],
        scratch_shapes=[pltpu.VMEM((d, ns), F32), pltpu.SemaphoreType.DMA,
                        pltpu.SemaphoreType.DMA((N_CHIPS + n,)), pltpu.SemaphoreType.DMA((N_CHIPS + n,))],
        compiler_params=_params(2, 56))(hn, dproj, *others)
    return outs[0], outs[1:]


def _in_proj_bwd(dproj, w4, x, lead, r1, dh2, ln_g, cs):
    seq, d = x.shape
    lp = CHUNK + seq
    ns = w4.shape[2]
    tm = _row_tile(lp, 320)
    nt = lp // tm
    n = len(cs)
    assert tm > CHUNK

    def body(dp_ref, w_ref, x_hbm, lead_hbm, r_ref, d2_ref, g_ref, *refs):
        cs_refs, (dh_ref, dlg_ref), rb_refs = refs[:n], refs[n:n + 2], refs[n + 2:2 * n + 2]
        hbuf, h_sems, send_sems, recv_sems = refs[2 * n + 2:]
        i = pl.program_id(0)
        slot = i % 2
        x, y, c = _mesh_pos()
        h_first, h_start, h_wait = _padded_tile_stream(x_hbm, lead_hbm, hbuf, h_sems, tm)

        @pl.when(i == 0)
        def _():
            h_first(0)

        @pl.when(i + 1 < nt)
        def _():
            h_start(1 - slot, i + 1)

        h_wait(slot)

        def exchange():
            return [pltpu.make_async_remote_copy(
                src_ref=cs_refs[k].at[2 * chip[0] + chip[1]], dst_ref=rb_refs[k].at[j],
                send_sem=send_sems.at[3 * k + j], recv_sem=recv_sems.at[3 * k + j],
                device_id=(*chip, c), device_id_type=MESH)
                for k in range(n) for j, chip in enumerate(_other_chips(x, y))]

        @pl.when(i == 0)
        def _():
            dlg_ref[...] = jnp.zeros_like(dlg_ref)
            for cp in exchange():
                cp.start()

        dhn = _dot_nt(dp_ref[:, 0:ns], w_ref[0])
        for s in range(1, N_CHIPS):
            dhn = dhn + _dot_nt(dp_ref[:, s * ns:(s + 1) * ns], w_ref[s])
        r = r_ref[...]
        hn0 = hbuf[slot] * r
        dlg_ref[...] += _colsum(dhn * hn0)
        t = dhn * g_ref[...]
        dh_ref[...] = d2_ref[...] + r * (t - hn0 * _mean(t * hn0))

        @pl.when(i == nt - 1)
        def _():
            for cp in exchange():
                cp.wait()

    def row(cols):
        return pl.BlockSpec((tm, cols), lambda i: (i, 0))

    any_spec = pl.BlockSpec(memory_space=pl.ANY)
    outs = pl.pallas_call(
        body, name="in_proj_bwd", grid=(nt,),
        in_specs=[row(N_CHIPS * ns), pl.BlockSpec(memory_space=pltpu.VMEM),
                  any_spec, any_spec, row(1), row(d), pl.BlockSpec((1, d), lambda i: (0, 0))] + [any_spec] * n,
        out_specs=[row(d), pl.BlockSpec((1, d), lambda i: (0, 0))] + [any_spec] * n,
        out_shape=[jax.ShapeDtypeStruct((lp, d), F32), jax.ShapeDtypeStruct((1, d), F32)]
                  + [jax.ShapeDtypeStruct((3,) + a.shape[1:], a.dtype) for a in cs],
        scratch_shapes=[pltpu.VMEM((2, tm, d), F32), pltpu.SemaphoreType.DMA((2,)),
                        pltpu.SemaphoreType.DMA((3 * n,)), pltpu.SemaphoreType.DMA((3 * n,))],
        compiler_params=_params(1, 58))(dproj, w4, x, lead, r1, dh2, ln_g, *cs)
    return outs[0], outs[1], outs[2:]


def _mesh_pos():
    return lax.axis_index("x"), lax.axis_index("y"), lax.axis_index("c")


def _other_chips(x, y):
    return [(1 - x, y), (x, 1 - y), (1 - x, 1 - y)]


def _gather_shards(shards):
    n = len(shards)
    halves = [a.shape[0] // 2 for a in shards]

    def body(*refs):
        ins, outs = refs[:n], refs[n:2 * n]
        send_sems, recv_sems, loc_sems = refs[2 * n:]
        x, y, c = _mesh_pos()
        me, sibling = (x, y, c), (x, y, 1 - c)
        my_s = 2 * x + y
        chips = _other_chips(x, y)

        def half(k, s, cc):
            return outs[k].at[s, pl.ds(cc * halves[k], halves[k])]

        def rcopy(k, j, src, dst, to):
            return pltpu.make_async_remote_copy(
                src_ref=src, dst_ref=dst, send_sem=send_sems.at[6 * k + j], recv_sem=recv_sems.at[6 * k + j],
                device_id=to, device_id_type=MESH)

        local = [pltpu.make_async_copy(ins[k], outs[k].at[my_s], loc_sems.at[k]) for k in range(n)]
        for cp in local:
            cp.start()
        started = []
        for k in range(n):
            for j, chip in enumerate(chips):
                cp = rcopy(k, j, ins[k].at[pl.ds(c * halves[k], halves[k])], half(k, my_s, c), (*chip, c))
                cp.start()
                started.append(cp)
        for j, chip in enumerate(chips):
            s_j = 2 * chip[0] + chip[1]
            for k in range(n):
                rcopy(k, j, half(k, s_j, c), half(k, s_j, c), me).wait_recv()
                cp = rcopy(k, 3 + j, half(k, s_j, c), half(k, s_j, c), sibling)
                cp.start()
                started.append(cp)
        for j, chip in enumerate(chips):
            s_j = 2 * chip[0] + chip[1]
            for k in range(n):
                rcopy(k, 3 + j, half(k, s_j, 1 - c), half(k, s_j, 1 - c), me).wait_recv()
        for cp in started:
            cp.wait_send()
        for cp in local:
            cp.wait()

    return pl.pallas_call(
        body, name="gather_weights",
        in_specs=[pl.BlockSpec(memory_space=pl.ANY)] * n,
        out_specs=[pl.BlockSpec(memory_space=pl.ANY)] * n,
        out_shape=[jax.ShapeDtypeStruct((N_CHIPS,) + a.shape, a.dtype) for a in shards],
        scratch_shapes=[pltpu.SemaphoreType.DMA((6 * n,)), pltpu.SemaphoreType.DMA((6 * n,)),
                        pltpu.SemaphoreType.DMA((n,))],
    )(*shards)


def _rs_pair_share(fulls):
    n = len(fulls)
    halves = [f.shape[0] // 2 for f in fulls]

    def body(*refs):
        outs = refs[n:2 * n]
        send_sems, recv_sems = refs[2 * n:]
        x, y, c = _mesh_pos()

        def copy(k, cc, to):
            rows = outs[k].at[pl.ds(cc * halves[k], halves[k])]
            return pltpu.make_async_remote_copy(
                src_ref=rows, dst_ref=rows, send_sem=send_sems.at[k], recv_sem=recv_sems.at[k],
                device_id=to, device_id_type=MESH)

        cps = [copy(k, c, (x, y, 1 - c)) for k in range(n)]
        for cp in cps:
            cp.start()
        for k in range(n):
            copy(k, 1 - c, (x, y, c)).wait_recv()
        for cp in cps:
            cp.wait_send()

    return pl.pallas_call(
        body, name="rs_pair_share",
        in_specs=[pl.BlockSpec(memory_space=pl.ANY)] * n,
        out_specs=[pl.BlockSpec(memory_space=pl.ANY)] * n,
        out_shape=[jax.ShapeDtypeStruct(f.shape, f.dtype) for f in fulls],
        input_output_aliases={k: k for k in range(n)},
        scratch_shapes=[pltpu.SemaphoreType.DMA((n,)), pltpu.SemaphoreType.DMA((n,))],
    )(*fulls)


def _pair_sum(g, recv, pos, name):
    _, rows, cols = g.shape
    h = rows // 2
    tr = _row_tile(h, 256)
    nb = h // tr

    def body(pos_ref, g_ref, r_ref, o_ref, own_ref):
        total = g_ref[0] + r_ref[0]
        o_ref[0] = total.astype(BF16)

        @pl.when(pl.program_id(1) == pos_ref[1])
        def _():
            own_ref[...] = total

    return pl.pallas_call(
        body, name=name,
        grid_spec=pltpu.PrefetchScalarGridSpec(
            num_scalar_prefetch=1, grid=(nb, N_CHIPS),
            in_specs=[pl.BlockSpec((1, tr, cols), lambda r, s, pos_ref: (s, pos_ref[0] * nb + r, 0)),
                      pl.BlockSpec((1, tr, cols), lambda r, s, pos_ref: (s, r, 0))],
            out_specs=[pl.BlockSpec((1, tr, cols), lambda r, s, pos_ref: (s, r, 0)),
                       pl.BlockSpec((tr, cols), lambda r, s, pos_ref: (r, 0))]),
        out_shape=[jax.ShapeDtypeStruct((N_CHIPS, h, cols), BF16), jax.ShapeDtypeStruct((h, cols), F32)],
        compiler_params=_params(2, 32))(pos, g, recv)


def _chip_sum(own, rb, pos, name):
    h, cols = own.shape
    tr = _row_tile(h, 256)
    nb = h // tr

    def body(pos_ref, c_ref, r_ref, o_ref):
        del pos_ref
        o_ref[...] = ((c_ref[...] + r_ref[0].astype(F32)) + r_ref[1].astype(F32)) + r_ref[2].astype(F32)

    return pl.pallas_call(
        body, name=name,
        grid_spec=pltpu.PrefetchScalarGridSpec(
            num_scalar_prefetch=1, grid=(nb,),
            in_specs=[pl.BlockSpec((tr, cols), lambda r, pos_ref: (r, 0)),
                      pl.BlockSpec((3, tr, cols), lambda r, pos_ref: (0, r, 0))],
            out_specs=pl.BlockSpec((tr, cols), lambda r, pos_ref: (pos_ref[0] * nb + r, 0))),
        out_shape=jax.ShapeDtypeStruct((2 * h, cols), F32),
        compiler_params=_params(1, 32))(pos, own, rb)


def _adamw_big(w, g, m, v, name):
    rows, cols = w.shape
    tr = _row_tile(rows, 256)

    def body(w_ref, g_ref, m_ref, v_ref, go_ref, d_ref, nm_ref, nv_ref):
        g = g_ref[...]
        go_ref[...] = g
        d_ref[...], nm_ref[...], nv_ref[...] = _adamw(w_ref[...], g, m_ref[...], v_ref[...])

    spec = pl.BlockSpec((tr, cols), lambda i: (i, 0))
    return pl.pallas_call(
        body, name=name, grid=(rows // tr,),
        in_specs=[spec] * 4, out_specs=[spec] * 4,
        out_shape=[jax.ShapeDtypeStruct((rows, cols), F32)] * 4,
        compiler_params=_params(1, 48))(w, g, m, v)


def _gather_small(loss, dfg, dlg, dgn, ddwb, dclg, dclb, dpwb, ddww, dmeta):
    d = loss.shape[1]
    w = dgn.shape[1]
    wc, mc = ddww.shape[1] // N_CHIPS, dmeta.shape[1] // N_CHIPS

    def body(loss_ref, dfg_ref, dlg_ref, dgn_ref, ddwb_ref, dclg_ref, dclb_ref, dpwb_ref, ddww_ref, dmeta_ref,
             gs_ref, gd_ref, gm_ref, send_sems, recv_sems, loc_sems):
        x, y, c = _mesh_pos()
        me = 4 * x + 2 * y + c
        gs_ref[me, 0:1, :] = loss_ref[...]
        gs_ref[me, 1:2, :] = dfg_ref[...]
        gs_ref[me, 2:3, :] = dlg_ref[...]
        gs_ref[me, 3:4, 0:w] = dgn_ref[...]
        gs_ref[me, 3:4, w:2 * w] = ddwb_ref[...]
        gs_ref[me, 4:5, 0:w] = dclg_ref[...]
        gs_ref[me, 4:5, w:2 * w] = dclb_ref[...]
        gs_ref[me, 5:6, 0:w] = dpwb_ref[...]
        gs_ref[me, 5:6, w:2 * w] = jnp.zeros((1, d - w), F32)
        gs_ref[me, 6:8, :] = jnp.zeros((2, d), F32)
        bufs = (gs_ref, gd_ref, gm_ref)

        def mine_for(k, shard):
            if k == 0:
                return gs_ref.at[me]
            ref, width = ((ddww_ref, wc), (dmeta_ref, mc))[k - 1]
            return ref.at[:, pl.ds(pl.multiple_of(shard * width, width), width)]

        def peer(j):
            return (1 - x if j & 4 else x), (1 - y if j & 2 else y), (1 - c if j & 1 else c)

        def copy(k, j, src, slot, to):
            return pltpu.make_async_remote_copy(
                src_ref=src, dst_ref=bufs[k].at[slot],
                send_sem=send_sems.at[7 * k + j - 1], recv_sem=recv_sems.at[7 * k + j - 1],
                device_id=to, device_id_type=MESH)

        own = [pltpu.make_async_copy(mine_for(k, 2 * x + y), bufs[k].at[me], loc_sems.at[k - 1]) for k in (1, 2)]
        for cp in own:
            cp.start()
        cps = []
        for k in range(3):
            for j in range(1, N_DEV):
                px, py, pc = peer(j)
                cp = copy(k, j, mine_for(k, 2 * px + py), me, (px, py, pc))
                cp.start()
                cps.append(cp)
        for k in range(3):
            for j in range(1, N_DEV):
                px, py, pc = peer(j)
                slot = 4 * px + 2 * py + pc
                copy(k, j, bufs[k].at[slot], slot, (x, y, c)).wait_recv()
        for cp in cps:
            cp.wait_send()
        for cp in own:
            cp.wait()

    vm = pl.BlockSpec(memory_space=pltpu.VMEM)
    return pl.pallas_call(
        body, name="gather_small",
        in_specs=[vm] * 10, out_specs=[vm] * 3,
        out_shape=[jax.ShapeDtypeStruct((N_DEV, 8, d), F32),
                   jax.ShapeDtypeStruct((N_DEV, ddww.shape[0], wc), F32),
                   jax.ShapeDtypeStruct((N_DEV, dmeta.shape[0], mc), F32)],
        scratch_shapes=[pltpu.SemaphoreType.DMA((21,)), pltpu.SemaphoreType.DMA((21,)),
                        pltpu.SemaphoreType.DMA((2,))],
    )(loss, dfg, dlg, dgn, ddwb, dclg, dclb, dpwb, ddww, dmeta)


def _small_update(gs, gd, gm, weights, ms, vs):
    d = gs.shape[2]
    w = d // 2
    n = len(weights)

    def body(gs_ref, gd_ref, gm_ref, *refs):
        w_refs, m_refs, v_refs = refs[:n], refs[n:2 * n], refs[2 * n:3 * n]
        loss_ref = refs[3 * n]
        g_refs = refs[3 * n + 1:4 * n + 1]
        d_refs = refs[4 * n + 1:5 * n + 1]
        nm_refs = refs[5 * n + 1:6 * n + 1]
        nv_refs = refs[6 * n + 1:7 * n + 1]

        def total(ref):
            t = ref[0]
            for dev in range(1, N_DEV):
                t = t + ref[dev]
            return t

        packed = total(gs_ref)
        loss_ref[...] = jnp.sum(packed[0:1, :], axis=1, keepdims=True) * (0.5 / d)
        grads = [packed[2:3, :], packed[1:2, :], packed[3:4, 0:w], packed[3:4, w:2 * w], packed[4:5, 0:w],
                 packed[4:5, w:2 * w], packed[5:6, 0:w], total(gd_ref), total(gm_ref)]
        for k in range(n):
            g = grads[k]
            g_refs[k][...] = g
            d_refs[k][...], nm_refs[k][...], nv_refs[k][...] = _adamw(w_refs[k][...], g, m_refs[k][...], v_refs[k][...])

    def whole(shape):
        return pl.BlockSpec(shape, lambda i: (0,) * len(shape))

    shapes = [a.shape for a in weights]
    in_specs = [whole(gs.shape), whole(gd.shape), whole(gm.shape)] + [whole(s) for s in shapes] * 3
    out_specs = [whole((1, 1))] + [whole(s) for s in shapes] * 4
    out_shape = [jax.ShapeDtypeStruct((1, 1), F32)] + [jax.ShapeDtypeStruct(s, F32) for s in shapes] * 4
    outs = pl.pallas_call(
        body, name="small_update", grid=(1,), in_specs=in_specs, out_specs=out_specs, out_shape=out_shape,
        compiler_params=_params(1, 32))(gs, gd, gm, *weights, *ms, *vs)
    loss = outs[0]
    return loss, outs[1:n + 1], outs[n + 1:2 * n + 1], outs[2 * n + 1:3 * n + 1], outs[3 * n + 1:4 * n + 1]


def kernel(x, meta_tokens, ln_g, w_in, ret_gn_g, conv_dw_w, conv_dw_b, conv_ln_g, conv_ln_b, conv_pw_w, conv_pw_b, w_out, final_g, loss_target, m_meta_tokens, m_ln_g, m_w_in, m_ret_gn_g, m_conv_dw_w, m_conv_dw_b, m_conv_ln_g, m_conv_ln_b, m_conv_pw_w, m_conv_pw_b, m_w_out, m_final_g, v_meta_tokens, v_ln_g, v_w_in, v_ret_gn_g, v_conv_dw_w, v_conv_dw_b, v_conv_ln_g, v_conv_ln_b, v_conv_pw_w, v_conv_pw_b, v_w_out, v_final_g):
    seq, d = x.shape[1], x.shape[2]
    w = ret_gn_g.shape[1]
    hd = w // RET_HEADS
    lp = CHUNK + seq
    ns = w_in.shape[2]
    mx, my, mc = lax.axis_index("x"), lax.axis_index("y"), lax.axis_index("c")
    my_s = 2 * mx + my
    s_arr = my_s.astype(jnp.int32).reshape(1)
    pos = jnp.stack([mc, my_s]).astype(jnp.int32)
    order = jnp.stack([my_s, 2 * (1 - mx) + my, 2 * mx + (1 - my), 2 * (1 - mx) + (1 - my)]).astype(jnp.int32)

    dw_pad = jnp.pad(conv_dw_w[0], ((0, HALO - CONV_K), (0, 0)))
    dw4, meta4 = _gather_shards([dw_pad, meta_tokens])
    dw_full = dw4.transpose(1, 0, 2).reshape(HALO, w)
    meta_full = meta4.transpose(1, 0, 2).reshape(N_META, d)

    lead = jnp.concatenate([jnp.zeros((LEAD, d), F32), meta_full], axis=0)
    zero_lead = jnp.zeros((CHUNK, d), F32)
    consts = _ret_consts()
    rope = _rope_tables(lp // CHUNK, hd // 2)
    fg2 = final_g.reshape(1, d)

    proj, r1, hn, w4, pw4, wo4 = _in_proj_gather(
        order, x[0], lead, ln_g, _cast_into_gathered(w_in[0], s_arr, "cast_w_in"),
        _cast_into_gathered(conv_pw_w[0], s_arr, "cast_pw_w"), _cast_into_gathered(w_out[0], s_arr, "cast_w_out"))
    pw_full = pw4.reshape(w, w)
    wo_full = wo4.reshape(2 * w, d)
    y_ret, ssave, ret_ops, ret_scores, ret_out = _ret_fwd(proj, rope, ret_gn_g, consts)
    y_conv, u1 = _conv_fwd(proj, dw_full, conv_dw_b, conv_ln_g, conv_ln_b, pw_full, conv_pw_b)
    dh2, dy, loss_l, dfg = _out_proj_loss(y_ret, y_conv, x[0], lead, loss_target[0], zero_lead, wo_full, fg2)

    g_wo = _dw_out(y_ret, y_conv, dh2)
    dproj, dgn = _ret_bwd(proj, dy, ssave, ret_ops, ret_scores, ret_out, rope, ret_gn_g, consts)
    dproj, du1, g_pw, dpwb, dclg, dclb = _conv_bwd_pw(dy, proj, u1, conv_ln_g, conv_ln_b, pw_full, conv_pw_b, dproj)
    dproj, ddww, ddwb = _conv_bwd_dw(du1, proj, dw_full, dproj)
    g_wo4 = g_wo.reshape(N_CHIPS, (2 * w) // N_CHIPS, d)
    g_pw4 = g_pw.reshape(N_CHIPS, w // N_CHIPS, w)
    g_win, recv = _dw_in(hn, dproj, ns, [g_wo4, g_pw4])
    gs = [g_win, g_wo4, g_pw4]
    names = ("w_in", "w_out", "pw_w")
    sums = [_pair_sum(g, r, pos, "pair_sum_" + nm) for g, r, nm in zip(gs, recv, names)]
    dh, dlg, rb = _in_proj_bwd(dproj, w4, x[0], lead, r1, dh2, ln_g, [cs_ for cs_, _ in sums])
    grad_x = dh[CHUNK:][None]
    dmeta = dh[LEAD:CHUNK]
    fulls = [_chip_sum(own, r, pos, "chip_sum_" + nm) for (_, own), r, nm in zip(sums, rb, names)]
    grad_w_in, grad_w_out, grad_pw = _rs_pair_share(fulls)
    grad_w_in, d_win, nm_win, nv_win = _adamw_big(w_in[0], grad_w_in, m_w_in[0], v_w_in[0], "adamw_w_in")
    grad_w_out, d_wo, nm_wo, nv_wo = _adamw_big(w_out[0], grad_w_out, m_w_out[0], v_w_out[0], "adamw_w_out")
    grad_pw, d_pw, nm_pw, nv_pw = _adamw_big(conv_pw_w[0], grad_pw, m_conv_pw_w[0], v_conv_pw_w[0], "adamw_pw_w")

    gsm, gdm, gmm = _gather_small(loss_l, dfg, dlg, dgn, ddwb, dclg, dclb, dpwb, ddww, dmeta)

    def pad_dw(a):
        return jnp.pad(a[0], ((0, HALO - CONV_K), (0, 0)))

    small_w = [ln_g, fg2, ret_gn_g, conv_dw_b, conv_ln_g, conv_ln_b, conv_pw_b, dw_pad, meta_tokens]
    small_m = [m_ln_g, m_final_g.reshape(1, d), m_ret_gn_g, m_conv_dw_b, m_conv_ln_g, m_conv_ln_b, m_conv_pw_b,
               pad_dw(m_conv_dw_w), m_meta_tokens]
    small_v = [v_ln_g, v_final_g.reshape(1, d), v_ret_gn_g, v_conv_dw_b, v_conv_ln_g, v_conv_ln_b, v_conv_pw_b,
               pad_dw(v_conv_dw_w), v_meta_tokens]
    loss, sg, sd, snm, snv = _small_update(gsm, gdm, gmm, small_w, small_m, small_v)

    def assemble(small, big_in, big_pw, big_out):
        ln, fg, gn, dwb, clg, clb, pwb, dww, meta = small
        return (meta, ln, big_in[None], gn, dww[:CONV_K][None], dwb, clg, clb, big_pw[None], pwb, big_out[None],
                fg.reshape(d))

    return (loss.reshape(()), grad_x,
            *assemble(sg, grad_w_in, grad_pw, grad_w_out),
            *assemble(sd, d_win, d_pw, d_wo),
            *assemble(snm, nm_win, nm_pw, nm_wo),
            *assemble(snv, nv_win, nv_pw, nv_wo))
```

```python
import functools

import jax
import jax.numpy as jnp
from jax import lax
from jax.experimental import pallas as pl
from jax.experimental.pallas import tpu as pltpu

F32 = jnp.float32
BF16 = jnp.bfloat16
MESH = pl.DeviceIdType.MESH

N_META = 16
CHUNK = 128
LEAD = (-N_META) % CHUNK
RET_HEADS = 4
CONV_K = 31
HALO = 32
ROPE_BASE = 10000.0
EPS = 1e-6
N_CHIPS = 4
N_DEV = 8

ADAM_LR = 0.001
ADAM_B1 = 0.9
ADAM_B2 = 0.999
ADAM_EPS = 1e-08
ADAM_WD = 0.01
ADAM_STEP = 10

MIB = 2 ** 20


def _params(n_grid_axes, vmem_mib):
    return pltpu.CompilerParams(dimension_semantics=("arbitrary",) * n_grid_axes,
                                vmem_limit_bytes=vmem_mib * MIB)


def _row_tile(n, pref):
    for t in (1664, 1280, 1024, 832, 640, 512, 384, 320, 256, 128, 64, 32, 16, 8):
        if t <= pref and n % t == 0:
            return t
    raise ValueError(f"no row tile for {n}")


def _dot(a, b):
    return jnp.dot(a, b, preferred_element_type=F32)


def _dot_nt(a, b):
    return lax.dot_general(a, b, (((1,), (1,)), ((), ())), preferred_element_type=F32)


def _dot_tn(a, b):
    return lax.dot_general(a, b, (((0,), (0,)), ((), ())), preferred_element_type=F32)


def _sigmoid(x):
    return jax.nn.sigmoid(x)


def _dsilu(x, s):
    return s * (1.0 + x * (1.0 - s))


def _mean(x):
    return jnp.mean(x, axis=-1, keepdims=True)


def _colsum(x):
    return jnp.sum(x, axis=0, keepdims=True)


def _rope(x, cos, sin):
    half = x.shape[-1] // 2
    x1, x2 = x[:, :half], x[:, half:]
    return jnp.concatenate([x1 * cos - x2 * sin, x1 * sin + x2 * cos], axis=-1)


def _rope_t(d, cos, sin):
    half = d.shape[-1] // 2
    d1, d2 = d[:, :half], d[:, half:]
    return jnp.concatenate([d1 * cos + d2 * sin, d2 * cos - d1 * sin], axis=-1)


def _adamw(w, g, m, v):
    m = ADAM_B1 * m + (1.0 - ADAM_B1) * g
    v = ADAM_B2 * v + (1.0 - ADAM_B2) * (g * g)
    m_hat = m / (1.0 - ADAM_B1 ** ADAM_STEP)
    v_hat = v / (1.0 - ADAM_B2 ** ADAM_STEP)
    delta = -ADAM_LR * (m_hat / (jnp.sqrt(v_hat) + ADAM_EPS) + ADAM_WD * w)
    return delta, m, v


def _ret_consts():
    h = jnp.arange(RET_HEADS, dtype=F32)
    log_g = jnp.log(1.0 - jnp.exp2(-5.0 - h))
    idx = jnp.arange(CHUNK, dtype=F32)
    rel = idx[:, None] - idx[None, :]
    dmask = jnp.where(rel[None] >= 0, jnp.exp(jnp.maximum(rel, 0.0)[None] * log_g[:, None, None]), 0.0)
    qd = jnp.exp((idx[None, :] + 1.0) * log_g[:, None])[:, :, None]
    kd = jnp.exp((CHUNK - 1.0 - idx[None, :]) * log_g[:, None])[:, :, None]
    cd = jnp.exp(CHUNK * log_g)[:, None, None]
    return dmask, qd, kd, cd


def _rope_tables(nch, half):
    inv_freq = ROPE_BASE ** (-jnp.arange(half, dtype=F32) / half)
    start = (jnp.arange(nch, dtype=F32) * float(CHUNK) - float(LEAD))[:, None] * inv_freq[None, :]
    within = jnp.arange(CHUNK, dtype=F32)[:, None] * inv_freq[None, :]
    return jnp.cos(start)[:, None, :], jnp.sin(start)[:, None, :], jnp.cos(within), jnp.sin(within)


def _rope_of_chunk(ca, sa, cb, sb):
    return ca * cb - sa * sb, sa * cb + ca * sb


def _padded_tile_stream(x_hbm, lead_hbm, buf, sems, tm):
    def start_first(slot):
        pltpu.make_async_copy(lead_hbm, buf.at[slot, pl.ds(0, CHUNK)], sems.at[slot]).start()
        pltpu.make_async_copy(x_hbm.at[pl.ds(0, tm - CHUNK)], buf.at[slot, pl.ds(CHUNK, tm - CHUNK)],
                              sems.at[slot]).start()

    def start(slot, tile):
        pltpu.make_async_copy(x_hbm.at[pl.ds(tile * tm - CHUNK, tm)], buf.at[slot], sems.at[slot]).start()

    def wait(slot):
        pltpu.make_async_copy(x_hbm.at[pl.ds(0, tm)], buf.at[slot], sems.at[slot]).wait()

    return start_first, start, wait


def _cast_into_gathered(a, s_arr, name):
    rows, cols = a.shape
    tr = _row_tile(rows, 256)

    def body(s_ref, a_ref, o_ref):
        del s_ref
        o_ref[0] = a_ref[...].astype(BF16)

    return pl.pallas_call(
        body, name=name,
        grid_spec=pltpu.PrefetchScalarGridSpec(
            num_scalar_prefetch=1, grid=(rows // tr,),
            in_specs=[pl.BlockSpec((tr, cols), lambda i, s_ref: (i, 0))],
            out_specs=pl.BlockSpec((1, tr, cols), lambda i, s_ref: (s_ref[0], i, 0))),
        out_shape=jax.ShapeDtypeStruct((N_CHIPS, rows, cols), BF16),
        compiler_params=_params(1, 32))(s_arr, a)


def _in_proj_gather(order, x, lead, ln_g, w4, pw4, wo4):
    seq, d = x.shape
    lp = CHUNK + seq
    ns = w4.shape[2]
    tm = _row_tile(lp, 832)
    nt = lp // tm
    assert nt >= 2, "the hn write-back of a row tile is waited for one step later, before any pass re-reads it"
    land_step, load_step = max(nt - 3, 0), max(nt - 2, 0)
    gathered = (w4, pw4, wo4)
    halves = [a.shape[1] // 2 for a in gathered]
    n = len(gathered)

    def body(order_ref, x_hbm, lead_hbm, g_ref, w_in, pw_in, wo_in, proj_ref, r_ref, hn_hbm, w_out, pw_out,
             wo_out, wbuf, hnbuf, hbuf, send_sems, recv_sems, hn_out_sems, hn_in_sems, h_sems, w_sem):
        del order_ref, w_in, pw_in, wo_in
        t, i = pl.program_id(0), pl.program_id(1)
        slot = (t * nt + i) % 2
        h_first, h_start, h_wait = _padded_tile_stream(x_hbm, lead_hbm, hbuf, h_sems, tm)
        x, y, c = _mesh_pos()
        me, sibling = (x, y, c), (x, y, 1 - c)
        my_s = 2 * x + y
        chips = _other_chips(x, y)
        outs = (w_out, pw_out, wo_out)

        def half(k, s, cc):
            return outs[k].at[s, pl.ds(cc * halves[k], halves[k])]

        def rcopy(k, j, rows, to):
            return pltpu.make_async_remote_copy(
                src_ref=rows, dst_ref=rows, send_sem=send_sems.at[6 * k + j], recv_sem=recv_sems.at[6 * k + j],
                device_id=to, device_id_type=MESH)

        def send(k, j):
            return rcopy(k, j, half(k, my_s, c), (*chips[j], c))

        def shard_of(j):
            return 2 * chips[j][0] + chips[j][1]

        def forward(k, j):
            return rcopy(k, 3 + j, half(k, shard_of(j), c), sibling)

        def land(k, j):
            rcopy(k, j, half(k, shard_of(j), c), me).wait_recv()
            forward(k, j).start()

        def landed_from_sibling(k, j):
            rcopy(k, 3 + j, half(k, shard_of(j), 1 - c), me).wait_recv()

        def load_w(s, wslot):
            return pltpu.make_async_copy(w_out.at[s], wbuf.at[wslot], w_sem)

        def hn_out(sl, row_tile):
            return pltpu.make_async_copy(hnbuf.at[sl], hn_hbm.at[pl.ds(row_tile * tm, tm)], hn_out_sems.at[sl])

        def hn_in(sl, row_tile):
            return pltpu.make_async_copy(hn_hbm.at[pl.ds(row_tile * tm, tm)], hnbuf.at[sl], hn_in_sems.at[sl])

        @pl.when((t == 0) & (i == 0))
        def _():
            for j in range(2):
                send(0, j).start()
            load_w(my_s, 0).start()
            load_w(my_s, 0).wait()

        @pl.when((t == 1) & (i == 0))
        def _():
            send(0, 2).start()
            for k in range(1, n):
                for jj in range(3):
                    send(k, jj).start()

        for j in range(3):
            @pl.when((t == j) & (i == land_step))
            def _(j=j):
                land(0, j)

            @pl.when((t == j) & (i == load_step))
            def _(j=j):
                landed_from_sibling(0, j)
                load_w(shard_of(j), (j + 1) % 2).start()

            @pl.when((t == j + 1) & (i == 0))
            def _(j=j):
                load_w(shard_of(j), (j + 1) % 2).wait()

        @pl.when((t == 0) & (i == 0))
        def _():
            h_first(0)

        @pl.when((t == 0) & (i + 1 < nt))
        def _():
            h_start((i + 1) % 2, i + 1)

        @pl.when(t == 0)
        def _():
            h_wait(i % 2)
            h = hbuf[i % 2]
            r = lax.rsqrt(_mean(h * h) + EPS)
            hnbuf[slot] = ((h * r) * g_ref[...]).astype(BF16)
            r_ref[...] = r
            hn_out(slot, i).start()

        @pl.when(t > 0)
        def _():
            hn_in(slot, i).wait()

        @pl.when(((t == 0) & (i > 0)) | ((t == 1) & (i == 0)))
        def _():
            hn_out(1 - slot, jnp.where(i > 0, i - 1, nt - 1)).wait()

        last = (t == N_CHIPS - 1) & (i == nt - 1)

        @pl.when(((t > 0) | (i == nt - 1)) & jnp.logical_not(last))
        def _():
            hn_in(1 - slot, jnp.where(i == nt - 1, 0, i + 1)).start()

        proj_ref[...] = _dot(hnbuf[slot], wbuf[t % 2])

        @pl.when(last)
        def _():
            for k in range(1, n):
                for j in range(3):
                    land(k, j)
            for k in range(1, n):
                for j in range(3):
                    landed_from_sibling(k, j)
            for k in range(n):
                for j in range(3):
                    send(k, j).wait_send()
                    forward(k, j).wait_send()

    def frozen(t, i):
        return jnp.where(t == 0, i, nt - 1)

    any_spec = pl.BlockSpec(memory_space=pl.ANY)
    return pl.pallas_call(
        body, name="in_proj_gather",
        grid_spec=pltpu.PrefetchScalarGridSpec(
            num_scalar_prefetch=1, grid=(N_CHIPS, nt),
            in_specs=[any_spec, any_spec, pl.BlockSpec((1, d), lambda t, i, o: (0, 0)),
                      any_spec, any_spec, any_spec],
            out_specs=[pl.BlockSpec((tm, ns), lambda t, i, o: (i, o[t])),
                       pl.BlockSpec((tm, 1), lambda t, i, o: (frozen(t, i), 0)),
                       any_spec, any_spec, any_spec, any_spec],
            scratch_shapes=[pltpu.VMEM((2, d, ns), BF16), pltpu.VMEM((2, tm, d), BF16), pltpu.VMEM((2, tm, d), F32),
                            pltpu.SemaphoreType.DMA((6 * n,)), pltpu.SemaphoreType.DMA((6 * n,)),
                            pltpu.SemaphoreType.DMA((2,)), pltpu.SemaphoreType.DMA((2,)),
                            pltpu.SemaphoreType.DMA((2,)), pltpu.SemaphoreType.DMA]),
        out_shape=[jax.ShapeDtypeStruct((lp, N_CHIPS * ns), F32),
                   jax.ShapeDtypeStruct((lp, 1), F32),
                   jax.ShapeDtypeStruct((lp, d), BF16)]
                  + [jax.ShapeDtypeStruct(a.shape, a.dtype) for a in gathered],
        input_output_aliases={4: 3, 5: 4, 6: 5},
        compiler_params=_params(2, 56))(order, x, lead, ln_g, w4, pw4, wo4)


def _chunks_per_step(nch):
    return next(n for n in (5, 4, 3, 2, 1) if nch % n == 0)


def _ret_fwd(proj, rope, gn_g, consts):
    lp = proj.shape[0]
    w = gn_g.shape[1]
    hd = w // RET_HEADS
    nch = lp // CHUNK
    cps = _chunks_per_step(nch)
    tr = cps * CHUNK
    dmask, qd, kd, cd = consts

    def body(q_ref, k_ref, v_ref, g_ref, ca_ref, sa_ref, cb_ref, sb_ref, gn_ref, m_ref, qd_ref, kd_ref, cd_ref,
             y_ref, ssave_ref, ops_ref, sc_ref, out_ref, s_scr):
        @pl.when(pl.program_id(0) == 0)
        def _():
            s_scr[...] = jnp.zeros_like(s_scr)

        def chunk(cc, carry):
            rows = pl.ds(pl.multiple_of(cc * CHUNK, CHUNK), CHUNK)
            cos_t, sin_t = _rope_of_chunk(ca_ref[cc], sa_ref[cc], cb_ref[...], sb_ref[...])
            for h in range(RET_HEADS):
                sl = slice(h * hd, (h + 1) * hd)
                qr = _rope(q_ref[rows, sl], cos_t, sin_t)
                kr = _rope(k_ref[rows, sl], cos_t, sin_t) * (hd ** -0.5)
                qb, kb = qr.astype(BF16), kr.astype(BF16)
                qdq, kdk = (qr * qd_ref[h]).astype(BF16), (kr * kd_ref[h]).astype(BF16)
                vb = v_ref[rows, sl].astype(BF16)
                for k, operand in enumerate((qb, kb, qdq, kdk, vb)):
                    ops_ref[k, rows, sl] = operand
                scb = (_dot_nt(qb, kb) * m_ref[h]).astype(BF16)
                sc_ref[cc, h] = scb
                state = s_scr[h]
                sb = state.astype(BF16)
                ssave_ref[cc, h] = sb
                out = _dot(scb, vb) + _dot(qdq, sb)
                out_ref[rows, sl] = out
                s_scr[h] = state * cd_ref[h] + _dot_tn(kdk, vb)
                dev = out - _mean(out)
                yn = dev * lax.rsqrt(_mean(dev * dev) + EPS)
                g = g_ref[rows, sl]
                y_ref[rows, sl] = ((yn * gn_ref[:, sl]) * (g * _sigmoid(g))).astype(BF16)
            return carry

        lax.fori_loop(0, cps, chunk, 0)

    def col(j):
        return pl.BlockSpec((tr, w), lambda i: (i, j))

    def whole(a):
        return pl.BlockSpec(a.shape, lambda i: (0,) * a.ndim)

    return pl.pallas_call(
        body, name="ret_fwd", grid=(nch // cps,),
        in_specs=[col(0), col(1), col(2), col(3),
                  pl.BlockSpec((cps, 1, hd // 2), lambda i: (i, 0, 0)),
                  pl.BlockSpec((cps, 1, hd // 2), lambda i: (i, 0, 0)),
                  whole(rope[2]), whole(rope[3]),
                  whole(gn_g), whole(dmask), whole(qd), whole(kd), whole(cd)],
        out_specs=[pl.BlockSpec((tr, w), lambda i: (i, 0)),
                   pl.BlockSpec((cps, RET_HEADS, hd, hd), lambda i: (i, 0, 0, 0)),
                   pl.BlockSpec((5, tr, w), lambda i: (0, i, 0)),
                   pl.BlockSpec((cps, RET_HEADS, CHUNK, CHUNK), lambda i: (i, 0, 0, 0)),
                   pl.BlockSpec((tr, w), lambda i: (i, 0))],
        out_shape=[jax.ShapeDtypeStruct((lp, w), BF16),
                   jax.ShapeDtypeStruct((nch, RET_HEADS, hd, hd), BF16),
                   jax.ShapeDtypeStruct((5, lp, w), BF16),
                   jax.ShapeDtypeStruct((nch, RET_HEADS, CHUNK, CHUNK), BF16),
                   jax.ShapeDtypeStruct((lp, w), F32)],
        scratch_shapes=[pltpu.VMEM((RET_HEADS, hd, hd), F32)],
        compiler_params=_params(1, 56))(proj, proj, proj, proj, *rope, gn_g, dmask, qd, kd, cd)


def _tap_groups(start, flip):
    groups = {}
    for j in range(CONV_K):
        o = start + (CONV_K - 1 - j if flip else j)
        groups.setdefault(o % 8, []).append((o // 8, j))
    return groups


def _shift_up(win, s):
    return win if s == 0 else pltpu.roll(win, win.shape[0] - s, axis=0)


def _dw_taps(src_ref, w_ref, dst_ref, bias, *, rows, start, flip, rb):
    cw = dst_ref.shape[1]
    lb = min(128, cw)
    groups = _tap_groups(start, flip)

    def rb_body(r, carry):
        base = pl.multiple_of(r * rb, rb)
        for cb in range(cw // lb):
            ls = slice(cb * lb, (cb + 1) * lb)
            win = src_ref[pl.ds(base, rb + HALO), ls]
            acc = jnp.zeros((rb, lb), F32) if bias is None else jnp.broadcast_to(bias[:, ls], (rb, lb))
            for s, taps in groups.items():
                ws = _shift_up(win, s)
                for a, j in taps:
                    acc = acc + ws[8 * a:8 * a + rb, :] * w_ref[j:j + 1, ls]
            dst_ref[pl.ds(base, rb), ls] = acc
        return carry

    lax.fori_loop(0, rows // rb, rb_body, 0)


def _conv_fwd(proj, dw_w, dw_b, cln_g, cln_b, pw_w, pw_b):
    lp = proj.shape[0]
    cw = dw_b.shape[1]
    tm = _row_tile(lp, 640)
    rb = _row_tile(tm, 128)

    def body(a_ref, b_ref, gc_ref, w_ref, wb_ref, lg_ref, lb_ref, pw_ref, pb_ref, y_ref, u1_ref, buf):
        @pl.when(pl.program_id(0) == 0)
        def _():
            buf[0:HALO, :] = jnp.zeros((HALO, cw), F32)

        buf[HALO:HALO + tm, :] = a_ref[...] * _sigmoid(b_ref[...])
        _dw_taps(buf, w_ref, u1_ref, wb_ref[...], rows=tm, start=HALO - (CONV_K - 1), flip=False, rb=rb)
        buf[0:HALO, :] = buf[tm:tm + HALO, :]
        u1 = u1_ref[...]
        dev = u1 - _mean(u1)
        z = dev * lax.rsqrt(_mean(dev * dev) + EPS) * lg_ref[...] + lb_ref[...]
        u3 = (z * _sigmoid(z)).astype(BF16)
        u4 = _dot(u3, pw_ref[...]) + pb_ref[...]
        gc = gc_ref[...]
        y_ref[...] = (u4 * (gc * _sigmoid(gc))).astype(BF16)

    def col(j):
        return pl.BlockSpec((tm, cw), lambda i: (i, j))

    def whole(a):
        return pl.BlockSpec(a.shape, lambda i: (0,) * a.ndim)

    return pl.pallas_call(
        body, name="conv_fwd", grid=(lp // tm,),
        in_specs=[col(4), col(5), col(6), whole(dw_w), whole(dw_b), whole(cln_g), whole(cln_b),
                  whole(pw_w), whole(pw_b)],
        out_specs=[pl.BlockSpec((tm, cw), lambda i: (i, 0)), pl.BlockSpec((tm, cw), lambda i: (i, 0))],
        out_shape=[jax.ShapeDtypeStruct((lp, cw), BF16), jax.ShapeDtypeStruct((lp, cw), F32)],
        scratch_shapes=[pltpu.VMEM((tm + HALO, cw), F32)],
        compiler_params=_params(1, 48))(proj, proj, proj, dw_w, dw_b, cln_g, cln_b, pw_w, pw_b)


def _out_proj_loss(yr, yc, x, lead, tgt, zero_lead, w_out, final_g):
    seq, d = x.shape
    lp = CHUNK + seq
    w = yr.shape[1]
    tm = _row_tile(lp, 320)
    nt = lp // tm
    assert tm > CHUNK

    def body(yr_ref, yc_ref, x_hbm, lead_hbm, t_hbm, zlead_hbm, w_ref, fg_ref, dh2_ref, dy_ref, loss_ref, dfg_ref,
             hbuf, tbuf, hsems, tsems):
        i = pl.program_id(0)
        slot = i % 2
        streams = (_padded_tile_stream(x_hbm, lead_hbm, hbuf, hsems, tm),
                   _padded_tile_stream(t_hbm, zlead_hbm, tbuf, tsems, tm))

        @pl.when(i == 0)
        def _():
            loss_ref[...] = jnp.zeros_like(loss_ref)
            dfg_ref[...] = jnp.zeros_like(dfg_ref)
            for start_first, _, _ in streams:
                start_first(0)

        @pl.when(i + 1 < nt)
        def _():
            for _, start, _ in streams:
                start(1 - slot, i + 1)

        for _, _, wait in streams:
            wait(slot)

        h2 = hbuf[slot] + (_dot(yr_ref[...], w_ref[0:w, :]) + _dot(yc_ref[...], w_ref[w:2 * w, :]))
        r2 = lax.rsqrt(_mean(h2 * h2) + EPS)
        h2n = h2 * r2
        fg = fg_ref[...]
        rows = i * tm + lax.broadcasted_iota(jnp.int32, (tm, 1), 0)
        err = jnp.where(rows >= CHUNK, h2n * fg - tbuf[slot], 0.0)
        loss_ref[...] += _colsum(err * err)
        dout = err * (1.0 / d)
        dfg_ref[...] += _colsum(dout * h2n)
        dz = dout * fg
        dh2 = r2 * (dz - h2n * _mean(dz * h2n))
        dh2_ref[...] = dh2
        db = dh2.astype(BF16)
        dy_ref[:, 0:w] = _dot_nt(db, w_ref[0:w, :])
        dy_ref[:, w:2 * w] = _dot_nt(db, w_ref[w:2 * w, :])

    def row(cols):
        return pl.BlockSpec((tm, cols), lambda i: (i, 0))

    any_spec = pl.BlockSpec(memory_space=pl.ANY)
    return pl.pallas_call(
        body, name="out_proj_loss", grid=(nt,),
        in_specs=[row(w), row(w), any_spec, any_spec, any_spec, any_spec,
                  pl.BlockSpec(memory_space=pltpu.VMEM),
                  pl.BlockSpec((1, d), lambda i: (0, 0))],
        out_specs=[row(d), row(2 * w), pl.BlockSpec((1, d), lambda i: (0, 0)),
                   pl.BlockSpec((1, d), lambda i: (0, 0))],
        out_shape=[jax.ShapeDtypeStruct((lp, d), F32), jax.ShapeDtypeStruct((lp, 2 * w), F32),
                   jax.ShapeDtypeStruct((1, d), F32), jax.ShapeDtypeStruct((1, d), F32)],
        scratch_shapes=[pltpu.VMEM((2, tm, d), F32), pltpu.VMEM((2, tm, d), F32),
                        pltpu.SemaphoreType.DMA((2,)), pltpu.SemaphoreType.DMA((2,))],
        compiler_params=_params(1, 56))(yr, yc, x, lead, tgt, zero_lead, w_out, final_g)


def _dw_out(yr, yc, dh2):
    lp, d = dh2.shape
    w = yr.shape[1]
    tm = _row_tile(lp, 1664)
    nb = 2
    dn = d // nb

    def body(yr_ref, yc_ref, d_ref, o_ref):
        @pl.when(pl.program_id(1) == 0)
        def _():
            o_ref[...] = jnp.zeros_like(o_ref)

        db = d_ref[...].astype(BF16)
        o_ref[0:w, :] += _dot_tn(yr_ref[...], db)
        o_ref[w:2 * w, :] += _dot_tn(yc_ref[...], db)

    return pl.pallas_call(
        body, name="dw_out", grid=(nb, lp // tm),
        in_specs=[pl.BlockSpec((tm, w), lambda n, i: (i, 0)),
                  pl.BlockSpec((tm, w), lambda n, i: (i, 0)),
                  pl.BlockSpec((tm, dn), lambda n, i: (i, n))],
        out_specs=pl.BlockSpec((2 * w, dn), lambda n, i: (0, n)),
        out_shape=jax.ShapeDtypeStruct((2 * w, d), F32),
        compiler_params=_params(2, 52))(yr, yc, dh2)


def _ret_bwd(proj, dy, ssave, ops, scores, out_pre, rope, gn_g, consts):
    lp = proj.shape[0]
    w = gn_g.shape[1]
    hd = w // RET_HEADS
    nch = lp // CHUNK
    cps = _chunks_per_step(nch)
    tr = cps * CHUNK
    dmask, qd, kd, cd = consts

    def body(g_ref, dy_ref, ss_ref, ops_ref, sc_ref, out_ref, ca_ref, sa_ref, cb_ref, sb_ref, gn_ref, m_ref, qd_ref,
             kd_ref, cd_ref, dp_ref, dgn_ref, ds_scr):
        @pl.when(pl.program_id(0) == 0)
        def _():
            ds_scr[...] = jnp.zeros_like(ds_scr)
            dgn_ref[...] = jnp.zeros_like(dgn_ref)

        def chunk(n, carry):
            cc = cps - 1 - n
            rows = pl.ds(pl.multiple_of(cc * CHUNK, CHUNK), CHUNK)
            cos_t, sin_t = _rope_of_chunk(ca_ref[cc], sa_ref[cc], cb_ref[...], sb_ref[...])
            for h in range(RET_HEADS):
                sl = slice(h * hd, (h + 1) * hd)
                qb, kb, qdq, kdk, vb = (ops_ref[k, rows, sl] for k in range(5))
                sb = ss_ref[cc, h]
                scb = sc_ref[cc, h]
                mask = m_ref[h]
                qdec, kdec = qd_ref[h], kd_ref[h]
                out = out_ref[rows, sl]
                dev = out - _mean(out)
                rstd = lax.rsqrt(_mean(dev * dev) + EPS)
                yn = dev * rstd
                g = g_ref[rows, sl]
                sg = _sigmoid(g)
                gng = gn_ref[:, sl]
                dyv = dy_ref[rows, sl]
                dgr = dyv * (yn * gng) * _dsilu(g, sg)
                silu_g = g * sg
                dgn_ref[:, sl] += _colsum(dyv * yn * silu_g)
                dyn = dyv * gng * silu_g
                dout = rstd * (dyn - _mean(dyn) - yn * _mean(dyn * yn))
                dob = dout.astype(BF16)
                by_v_and_s = _dot_nt(dob, jnp.concatenate([vb, sb], axis=0))
                dscb = (by_v_and_s[:, 0:CHUNK] * mask).astype(BF16)
                dstate = ds_scr[h]
                dsb = dstate.astype(BF16)
                dq = _dot(dscb, kb) + by_v_and_s[:, CHUNK:] * qdec
                dk = _dot_tn(dscb, qb) + _dot_nt(vb, dsb) * kdec
                onto_do = _dot_tn(jnp.concatenate([scb, qdq], axis=1), dob)
                dv = onto_do[0:CHUNK] + _dot(kdk, dsb)
                ds_scr[h] = dstate * cd_ref[h] + onto_do[CHUNK:]
                dp_ref[rows, 0 * w + h * hd:0 * w + (h + 1) * hd] = _rope_t(dq, cos_t, sin_t).astype(BF16)
                dp_ref[rows, 1 * w + h * hd:1 * w + (h + 1) * hd] = (
                    _rope_t(dk, cos_t, sin_t) * (hd ** -0.5)).astype(BF16)
                dp_ref[rows, 2 * w + h * hd:2 * w + (h + 1) * hd] = dv.astype(BF16)
                dp_ref[rows, 3 * w + h * hd:3 * w + (h + 1) * hd] = dgr.astype(BF16)
            return carry

        lax.fori_loop(0, cps, chunk, 0)

    def rev(i):
        return nch // cps - 1 - i

    def col(j):
        return pl.BlockSpec((tr, w), lambda i: (rev(i), j))

    def whole(a):
        return pl.BlockSpec(a.shape, lambda i: (0,) * a.ndim)

    return pl.pallas_call(
        body, name="ret_bwd", grid=(nch // cps,),
        in_specs=[col(3),
                  pl.BlockSpec((tr, w), lambda i: (rev(i), 0)),
                  pl.BlockSpec((cps, RET_HEADS, hd, hd), lambda i: (rev(i), 0, 0, 0)),
                  pl.BlockSpec((5, tr, w), lambda i: (0, rev(i), 0)),
                  pl.BlockSpec((cps, RET_HEADS, CHUNK, CHUNK), lambda i: (rev(i), 0, 0, 0)),
                  pl.BlockSpec((tr, w), lambda i: (rev(i), 0)),
                  pl.BlockSpec((cps, 1, hd // 2), lambda i: (rev(i), 0, 0)),
                  pl.BlockSpec((cps, 1, hd // 2), lambda i: (rev(i), 0, 0)),
                  whole(rope[2]), whole(rope[3]),
                  whole(gn_g), whole(dmask), whole(qd), whole(kd), whole(cd)],
        out_specs=[pl.BlockSpec((tr, 4 * w), lambda i: (rev(i), 0)),
                   pl.BlockSpec((1, w), lambda i: (0, 0))],
        out_shape=[jax.ShapeDtypeStruct((lp, 7 * w), BF16), jax.ShapeDtypeStruct((1, w), F32)],
        scratch_shapes=[pltpu.VMEM((RET_HEADS, hd, hd), F32)],
        compiler_params=_params(1, 56))(proj, dy, ssave, ops, scores, out_pre, *rope, gn_g, dmask, qd, kd, cd)


def _conv_bwd_pw(dy, proj, u1, cln_g, cln_b, pw_w, pw_b, dproj):
    lp, cw = u1.shape
    tm = _row_tile(lp, 640)

    def body(dy_ref, gc_ref, u1_ref, lg_ref, lb_ref, pw_ref, pb_ref, dp_in, dp_ref, du1_ref, dpw_ref,
             dpb_ref, dlg_ref, dlb_ref):
        del dp_in

        @pl.when(pl.program_id(0) == 0)
        def _():
            dpw_ref[...] = jnp.zeros_like(dpw_ref)
            dpb_ref[...] = jnp.zeros_like(dpb_ref)
            dlg_ref[...] = jnp.zeros_like(dlg_ref)
            dlb_ref[...] = jnp.zeros_like(dlb_ref)

        u1 = u1_ref[...]
        dev = u1 - _mean(u1)
        rstd = lax.rsqrt(_mean(dev * dev) + EPS)
        u1n = dev * rstd
        lg = lg_ref[...]
        z = u1n * lg + lb_ref[...]
        sz = _sigmoid(z)
        u3b = (z * sz).astype(BF16)
        u4 = _dot(u3b, pw_ref[...]) + pb_ref[...]
        gc = gc_ref[...]
        sgc = _sigmoid(gc)
        dyc = dy_ref[...]
        du4 = dyc * (gc * sgc)
        dp_ref[...] = (dyc * u4 * _dsilu(gc, sgc)).astype(BF16)
        du4b = du4.astype(BF16)
        dpb_ref[...] += _colsum(du4)
        dpw_ref[...] += _dot_tn(u3b, du4b)
        dz = _dot_nt(du4b, pw_ref[...]) * _dsilu(z, sz)
        dlg_ref[...] += _colsum(dz * u1n)
        dlb_ref[...] += _colsum(dz)
        dn = dz * lg
        du1_ref[...] = rstd * (dn - _mean(dn) - u1n * _mean(dn * u1n))

    def row(j):
        return pl.BlockSpec((tm, cw), lambda i: (i, j))

    def whole(a):
        return pl.BlockSpec(a.shape, lambda i: (0,) * a.ndim)

    def acc(r):
        return pl.BlockSpec((r, cw), lambda i: (0, 0))

    return pl.pallas_call(
        body, name="conv_bwd_pw", grid=(lp // tm,),
        in_specs=[row(1), row(6), row(0), whole(cln_g), whole(cln_b), whole(pw_w), whole(pw_b),
                  pl.BlockSpec(memory_space=pl.ANY)],
        out_specs=[row(6), row(0), acc(cw), acc(1), acc(1), acc(1)],
        out_shape=[jax.ShapeDtypeStruct(dproj.shape, dproj.dtype), jax.ShapeDtypeStruct((lp, cw), F32),
                   jax.ShapeDtypeStruct((cw, cw), F32), jax.ShapeDtypeStruct((1, cw), F32),
                   jax.ShapeDtypeStruct((1, cw), F32), jax.ShapeDtypeStruct((1, cw), F32)],
        input_output_aliases={7: 0},
        compiler_params=_params(1, 48))(dy, proj, u1, cln_g, cln_b, pw_w, pw_b, dproj)


def _conv_bwd_dw(du1, proj, dw_w, dproj):
    lp, cw = du1.shape
    tm = _row_tile(lp, 640)
    rb = _row_tile(tm, 128)
    nt = lp // tm
    hb = tm // HALO

    def body(a_ref, b_ref, du_ref, nx_ref, w_ref, dp_in, dp_ref, dww_ref, dwb_ref, ubuf, dbuf, du0, acc):
        del dp_in
        i = pl.program_id(0)

        @pl.when(i == 0)
        def _():
            ubuf[0:HALO, :] = jnp.zeros((HALO, cw), F32)
            acc[...] = jnp.zeros_like(acc)
            dwb_ref[...] = jnp.zeros_like(dwb_ref)

        a = a_ref[...]
        sb = _sigmoid(b_ref[...])
        ubuf[HALO:HALO + tm, :] = a * sb
        du = du_ref[...]
        dbuf[0:tm, :] = du
        dbuf[tm:tm + HALO, :] = jnp.where(i == nt - 1, 0.0, nx_ref[...])
        dwb_ref[...] += _colsum(du)
        _dw_taps(dbuf, w_ref, du0, None, rows=tm, start=0, flip=True, rb=rb)
        d0 = du0[...]
        dp_ref[:, 0:cw] = (d0 * sb).astype(BF16)
        dp_ref[:, cw:2 * cw] = (d0 * a * sb * (1.0 - sb)).astype(BF16)

        lb = min(128, cw)
        groups = _tap_groups(HALO - (CONV_K - 1), False)

        def rb_body(r, carry):
            base = pl.multiple_of(r * rb, rb)
            for cb in range(cw // lb):
                ls = slice(cb * lb, (cb + 1) * lb)
                win = ubuf[pl.ds(base, rb + HALO), ls]
                dv = dbuf[pl.ds(base, rb), ls]
                for s, taps in groups.items():
                    ws = _shift_up(win, s)
                    for a, j in taps:
                        prod = dv * ws[8 * a:8 * a + rb, :]
                        acc[8 * j:8 * j + 8, ls] += jnp.sum(prod.reshape(rb // 8, 8, lb), axis=0)
            return carry

        lax.fori_loop(0, tm // rb, rb_body, 0)
        ubuf[0:HALO, :] = ubuf[tm:tm + HALO, :]

        @pl.when(i == nt - 1)
        def _():
            for j in range(CONV_K):
                dww_ref[j:j + 1, :] = _colsum(acc[8 * j:8 * j + 8, :])
            dww_ref[CONV_K:HALO, :] = jnp.zeros((HALO - CONV_K, cw), F32)

    def col(j):
        return pl.BlockSpec((tm, cw), lambda i: (i, j))

    return pl.pallas_call(
        body, name="conv_bwd_dw", grid=(nt,),
        in_specs=[col(4), col(5), col(0),
                  pl.BlockSpec((HALO, cw), lambda i: (jnp.minimum((i + 1) * hb, nt * hb - 1), 0)),
                  pl.BlockSpec(dw_w.shape, lambda i: (0, 0)),
                  pl.BlockSpec(memory_space=pl.ANY)],
        out_specs=[pl.BlockSpec((tm, 2 * cw), lambda i: (i, 2)),
                   pl.BlockSpec((HALO, cw), lambda i: (0, 0)),
                   pl.BlockSpec((1, cw), lambda i: (0, 0))],
        out_shape=[jax.ShapeDtypeStruct(dproj.shape, dproj.dtype), jax.ShapeDtypeStruct((HALO, cw), F32),
                   jax.ShapeDtypeStruct((1, cw), F32)],
        scratch_shapes=[pltpu.VMEM((tm + HALO, cw), F32), pltpu.VMEM((tm + HALO, cw), F32),
                        pltpu.VMEM((tm, cw), F32), pltpu.VMEM((8 * HALO, cw), F32)],
        input_output_aliases={5: 0},
        compiler_params=_params(1, 56))(proj, proj, du1, du1, dw_w, dproj)


def _dw_in(hn, dproj, ns, others):
    lp, d = hn.shape
    tm = _row_tile(lp, 1664)
    nt = lp // tm
    mb = 512 if d % 512 == 0 else d
    n = len(others)
    halves = [d // 2] + [g.shape[1] // 2 for g in others]

    def body(hn_ref, dp_ref, *refs):
        other_refs, o_hbm, recv_refs = refs[:n], refs[n], refs[n + 1:2 * n + 2]
        acc, sem, send_sems, recv_sems = refs[2 * n + 2:]
        s, i = pl.program_id(0), pl.program_id(1)
        x, y, c = _mesh_pos()

        def to_sibling(src, dst, k):
            return pltpu.make_async_remote_copy(
                src_ref=src, dst_ref=dst, send_sem=send_sems.at[k], recv_sem=recv_sems.at[k],
                device_id=(x, y, 1 - c), device_id_type=MESH)

        def shard_half(p):
            return to_sibling(o_hbm.at[p, pl.ds((1 - c) * halves[0], halves[0])], recv_refs[0].at[p], p)

        def other_halves(k):
            return to_sibling(other_refs[k].at[:, pl.ds((1 - c) * halves[1 + k], halves[1 + k])],
                              recv_refs[1 + k], N_CHIPS + k)

        @pl.when((s == 0) & (i == 0))
        def _():
            for k in range(n):
                other_halves(k).start()

        @pl.when(i == 0)
        def _():
            acc[...] = jnp.zeros_like(acc)

        for m in range(d // mb):
            rows = slice(m * mb, (m + 1) * mb)
            acc[rows, :] += _dot_tn(hn_ref[:, rows], dp_ref[...])

        @pl.when(i == nt - 1)
        def _():
            cp = pltpu.make_async_copy(acc, o_hbm.at[s], sem)
            cp.start()
            cp.wait()
            shard_half(s).start()

        @pl.when((s == N_CHIPS - 1) & (i == nt - 1))
        def _():
            for p in range(N_CHIPS):
                shard_half(p).wait()
            for k in range(n):
                other_halves(k).wait()

    any_spec = pl.BlockSpec(memory_space=pl.ANY)
    outs = pl.pallas_call(
        body, name="dw_in", grid=(N_CHIPS, nt),
        in_specs=[pl.BlockSpec((tm, d), lambda s, i: (i, 0)),
                  pl.BlockSpec((tm, ns), lambda s, i: (i, s))] + [any_spec] * n,
        out_specs=[any_spec] * (n + 2),
        out_shape=[jax.ShapeDtypeStruct((N_CHIPS, d, ns), F32), jax.ShapeDtypeStruct((N_CHIPS, d // 2, ns), F32)]
                  + [jax.ShapeDtypeStruct((N_CHIPS, g.shape[1] // 2) + g.shape[2:], g.dtype) for g in others],
        scratch_shapes=[pltpu.VMEM((d, ns), F32), pltpu.SemaphoreType.DMA,
                        pltpu.SemaphoreType.DMA((N_CHIPS + n,)), pltpu.SemaphoreType.DMA((N_CHIPS + n,))],
        compiler_params=_params(2, 56))(hn, dproj, *others)
    return outs[0], outs[1:]


def _in_proj_bwd(dproj, w4, x, lead, r1, dh2, ln_g, cs):
    seq, d = x.shape
    lp = CHUNK + seq
    ns = w4.shape[2]
    tm = _row_tile(lp, 320)
    nt = lp // tm
    n = len(cs)
    assert tm > CHUNK

    def body(dp_ref, w_ref, x_hbm, lead_hbm, r_ref, d2_ref, g_ref, *refs):
        cs_refs, (dh_ref, dlg_ref), rb_refs = refs[:n], refs[n:n + 2], refs[n + 2:2 * n + 2]
        hbuf, h_sems, send_sems, recv_sems = refs[2 * n + 2:]
        i = pl.program_id(0)
        slot = i % 2
        x, y, c = _mesh_pos()
        h_first, h_start, h_wait = _padded_tile_stream(x_hbm, lead_hbm, hbuf, h_sems, tm)

        @pl.when(i == 0)
        def _():
            h_first(0)

        @pl.when(i + 1 < nt)
        def _():
            h_start(1 - slot, i + 1)

        h_wait(slot)

        def exchange():
            return [pltpu.make_async_remote_copy(
                src_ref=cs_refs[k].at[2 * chip[0] + chip[1]], dst_ref=rb_refs[k].at[j],
                send_sem=send_sems.at[3 * k + j], recv_sem=recv_sems.at[3 * k + j],
                device_id=(*chip, c), device_id_type=MESH)
                for k in range(n) for j, chip in enumerate(_other_chips(x, y))]

        @pl.when(i == 0)
        def _():
            dlg_ref[...] = jnp.zeros_like(dlg_ref)
            for cp in exchange():
                cp.start()

        dhn = _dot_nt(dp_ref[:, 0:ns], w_ref[0])
        for s in range(1, N_CHIPS):
            dhn = dhn + _dot_nt(dp_ref[:, s * ns:(s + 1) * ns], w_ref[s])
        r = r_ref[...]
        hn0 = hbuf[slot] * r
        dlg_ref[...] += _colsum(dhn * hn0)
        t = dhn * g_ref[...]
        dh_ref[...] = d2_ref[...] + r * (t - hn0 * _mean(t * hn0))

        @pl.when(i == nt - 1)
        def _():
            for cp in exchange():
                cp.wait()

    def row(cols):
        return pl.BlockSpec((tm, cols), lambda i: (i, 0))

    any_spec = pl.BlockSpec(memory_space=pl.ANY)
    outs = pl.pallas_call(
        body, name="in_proj_bwd", grid=(nt,),
        in_specs=[row(N_CHIPS * ns), pl.BlockSpec(memory_space=pltpu.VMEM),
                  any_spec, any_spec, row(1), row(d), pl.BlockSpec((1, d), lambda i: (0, 0))] + [any_spec] * n,
        out_specs=[row(d), pl.BlockSpec((1, d), lambda i: (0, 0))] + [any_spec] * n,
        out_shape=[jax.ShapeDtypeStruct((lp, d), F32), jax.ShapeDtypeStruct((1, d), F32)]
                  + [jax.ShapeDtypeStruct((3,) + a.shape[1:], a.dtype) for a in cs],
        scratch_shapes=[pltpu.VMEM((2, tm, d), F32), pltpu.SemaphoreType.DMA((2,)),
                        pltpu.SemaphoreType.DMA((3 * n,)), pltpu.SemaphoreType.DMA((3 * n,))],
        compiler_params=_params(1, 58))(dproj, w4, x, lead, r1, dh2, ln_g, *cs)
    return outs[0], outs[1], outs[2:]


def _mesh_pos():
    return lax.axis_index("x"), lax.axis_index("y"), lax.axis_index("c")


def _other_chips(x, y):
    return [(1 - x, y), (x, 1 - y), (1 - x, 1 - y)]


def _gather_shards(shards):
    n = len(shards)
    halves = [a.shape[0] // 2 for a in shards]

    def body(*refs):
        ins, outs = refs[:n], refs[n:2 * n]
        send_sems, recv_sems, loc_sems = refs[2 * n:]
        x, y, c = _mesh_pos()
        me, sibling = (x, y, c), (x, y, 1 - c)
        my_s = 2 * x + y
        chips = _other_chips(x, y)

        def half(k, s, cc):
            return outs[k].at[s, pl.ds(cc * halves[k], halves[k])]

        def rcopy(k, j, src, dst, to):
            return pltpu.make_async_remote_copy(
                src_ref=src, dst_ref=dst, send_sem=send_sems.at[6 * k + j], recv_sem=recv_sems.at[6 * k + j],
                device_id=to, device_id_type=MESH)

        local = [pltpu.make_async_copy(ins[k], outs[k].at[my_s], loc_sems.at[k]) for k in range(n)]
        for cp in local:
            cp.start()
        started = []
        for k in range(n):
            for j, chip in enumerate(chips):
                cp = rcopy(k, j, ins[k].at[pl.ds(c * halves[k], halves[k])], half(k, my_s, c), (*chip, c))
                cp.start()
                started.append(cp)
        for j, chip in enumerate(chips):
            s_j = 2 * chip[0] + chip[1]
            for k in range(n):
                rcopy(k, j, half(k, s_j, c), half(k, s_j, c), me).wait_recv()
                cp = rcopy(k, 3 + j, half(k, s_j, c), half(k, s_j, c), sibling)
                cp.start()
                started.append(cp)
        for j, chip in enumerate(chips):
            s_j = 2 * chip[0] + chip[1]
            for k in range(n):
                rcopy(k, 3 + j, half(k, s_j, 1 - c), half(k, s_j, 1 - c), me).wait_recv()
        for cp in started:
            cp.wait_send()
        for cp in local:
            cp.wait()

    return pl.pallas_call(
        body, name="gather_weights",
        in_specs=[pl.BlockSpec(memory_space=pl.ANY)] * n,
        out_specs=[pl.BlockSpec(memory_space=pl.ANY)] * n,
        out_shape=[jax.ShapeDtypeStruct((N_CHIPS,) + a.shape, a.dtype) for a in shards],
        scratch_shapes=[pltpu.SemaphoreType.DMA((6 * n,)), pltpu.SemaphoreType.DMA((6 * n,)),
                        pltpu.SemaphoreType.DMA((n,))],
    )(*shards)


def _rs_pair_share(fulls):
    n = len(fulls)
    halves = [f.shape[0] // 2 for f in fulls]

    def body(*refs):
        outs = refs[n:2 * n]
        send_sems, recv_sems = refs[2 * n:]
        x, y, c = _mesh_pos()

        def copy(k, cc, to):
            rows = outs[k].at[pl.ds(cc * halves[k], halves[k])]
            return pltpu.make_async_remote_copy(
                src_ref=rows, dst_ref=rows, send_sem=send_sems.at[k], recv_sem=recv_sems.at[k],
                device_id=to, device_id_type=MESH)

        cps = [copy(k, c, (x, y, 1 - c)) for k in range(n)]
        for cp in cps:
            cp.start()
        for k in range(n):
            copy(k, 1 - c, (x, y, c)).wait_recv()
        for cp in cps:
            cp.wait_send()

    return pl.pallas_call(
        body, name="rs_pair_share",
        in_specs=[pl.BlockSpec(memory_space=pl.ANY)] * n,
        out_specs=[pl.BlockSpec(memory_space=pl.ANY)] * n,
        out_shape=[jax.ShapeDtypeStruct(f.shape, f.dtype) for f in fulls],
        input_output_aliases={k: k for k in range(n)},
        scratch_shapes=[pltpu.SemaphoreType.DMA((n,)), pltpu.SemaphoreType.DMA((n,))],
    )(*fulls)


def _pair_sum(g, recv, pos, name):
    _, rows, cols = g.shape
    h = rows // 2
    tr = _row_tile(h, 256)
    nb = h // tr

    def body(pos_ref, g_ref, r_ref, o_ref, own_ref):
        total = g_ref[0] + r_ref[0]
        o_ref[0] = total.astype(BF16)

        @pl.when(pl.program_id(1) == pos_ref[1])
        def _():
            own_ref[...] = total

    return pl.pallas_call(
        body, name=name,
        grid_spec=pltpu.PrefetchScalarGridSpec(
            num_scalar_prefetch=1, grid=(nb, N_CHIPS),
            in_specs=[pl.BlockSpec((1, tr, cols), lambda r, s, pos_ref: (s, pos_ref[0] * nb + r, 0)),
                      pl.BlockSpec((1, tr, cols), lambda r, s, pos_ref: (s, r, 0))],
            out_specs=[pl.BlockSpec((1, tr, cols), lambda r, s, pos_ref: (s, r, 0)),
                       pl.BlockSpec((tr, cols), lambda r, s, pos_ref: (r, 0))]),
        out_shape=[jax.ShapeDtypeStruct((N_CHIPS, h, cols), BF16), jax.ShapeDtypeStruct((h, cols), F32)],
        compiler_params=_params(2, 32))(pos, g, recv)


def _chip_sum(own, rb, pos, name):
    h, cols = own.shape
    tr = _row_tile(h, 256)
    nb = h // tr

    def body(pos_ref, c_ref, r_ref, o_ref):
        del pos_ref
        o_ref[...] = ((c_ref[...] + r_ref[0].astype(F32)) + r_ref[1].astype(F32)) + r_ref[2].astype(F32)

    return pl.pallas_call(
        body, name=name,
        grid_spec=pltpu.PrefetchScalarGridSpec(
            num_scalar_prefetch=1, grid=(nb,),
            in_specs=[pl.BlockSpec((tr, cols), lambda r, pos_ref: (r, 0)),
                      pl.BlockSpec((3, tr, cols), lambda r, pos_ref: (0, r, 0))],
            out_specs=pl.BlockSpec((tr, cols), lambda r, pos_ref: (pos_ref[0] * nb + r, 0))),
        out_shape=jax.ShapeDtypeStruct((2 * h, cols), F32),
        compiler_params=_params(1, 32))(pos, own, rb)


def _adamw_big(w, g, m, v, name):
    rows, cols = w.shape
    tr = _row_tile(rows, 256)

    def body(w_ref, g_ref, m_ref, v_ref, go_ref, d_ref, nm_ref, nv_ref):
        g = g_ref[...]
        go_ref[...] = g
        d_ref[...], nm_ref[...], nv_ref[...] = _adamw(w_ref[...], g, m_ref[...], v_ref[...])

    spec = pl.BlockSpec((tr, cols), lambda i: (i, 0))
    return pl.pallas_call(
        body, name=name, grid=(rows // tr,),
        in_specs=[spec] * 4, out_specs=[spec] * 4,
        out_shape=[jax.ShapeDtypeStruct((rows, cols), F32)] * 4,
        compiler_params=_params(1, 48))(w, g, m, v)


def _gather_small(loss, dfg, dlg, dgn, ddwb, dclg, dclb, dpwb, ddww, dmeta):
    d = loss.shape[1]
    w = dgn.shape[1]
    wc, mc = ddww.shape[1] // N_CHIPS, dmeta.shape[1] // N_CHIPS

    def body(loss_ref, dfg_ref, dlg_ref, dgn_ref, ddwb_ref, dclg_ref, dclb_ref, dpwb_ref, ddww_ref, dmeta_ref,
             gs_ref, gd_ref, gm_ref, send_sems, recv_sems, loc_sems):
        x, y, c = _mesh_pos()
        me = 4 * x + 2 * y + c
        gs_ref[me, 0:1, :] = loss_ref[...]
        gs_ref[me, 1:2, :] = dfg_ref[...]
        gs_ref[me, 2:3, :] = dlg_ref[...]
        gs_ref[me, 3:4, 0:w] = dgn_ref[...]
        gs_ref[me, 3:4, w:2 * w] = ddwb_ref[...]
        gs_ref[me, 4:5, 0:w] = dclg_ref[...]
        gs_ref[me, 4:5, w:2 * w] = dclb_ref[...]
        gs_ref[me, 5:6, 0:w] = dpwb_ref[...]
        gs_ref[me, 5:6, w:2 * w] = jnp.zeros((1, d - w), F32)
        gs_ref[me, 6:8, :] = jnp.zeros((2, d), F32)
        bufs = (gs_ref, gd_ref, gm_ref)

        def mine_for(k, shard):
            if k == 0:
                return gs_ref.at[me]
            ref, width = ((ddww_ref, wc), (dmeta_ref, mc))[k - 1]
            return ref.at[:, pl.ds(pl.multiple_of(shard * width, width), width)]

        def peer(j):
            return (1 - x if j & 4 else x), (1 - y if j & 2 else y), (1 - c if j & 1 else c)

        def copy(k, j, src, slot, to):
            return pltpu.make_async_remote_copy(
                src_ref=src, dst_ref=bufs[k].at[slot],
                send_sem=send_sems.at[7 * k + j - 1], recv_sem=recv_sems.at[7 * k + j - 1],
                device_id=to, device_id_type=MESH)

        own = [pltpu.make_async_copy(mine_for(k, 2 * x + y), bufs[k].at[me], loc_sems.at[k - 1]) for k in (1, 2)]
        for cp in own:
            cp.start()
        cps = []
        for k in range(3):
            for j in range(1, N_DEV):
                px, py, pc = peer(j)
                cp = copy(k, j, mine_for(k, 2 * px + py), me, (px, py, pc))
                cp.start()
                cps.append(cp)
        for k in range(3):
            for j in range(1, N_DEV):
                px, py, pc = peer(j)
                slot = 4 * px + 2 * py + pc
                copy(k, j, bufs[k].at[slot], slot, (x, y, c)).wait_recv()
        for cp in cps:
            cp.wait_send()
        for cp in own:
            cp.wait()

    vm = pl.BlockSpec(memory_space=pltpu.VMEM)
    return pl.pallas_call(
        body, name="gather_small",
        in_specs=[vm] * 10, out_specs=[vm] * 3,
        out_shape=[jax.ShapeDtypeStruct((N_DEV, 8, d), F32),
                   jax.ShapeDtypeStruct((N_DEV, ddww.shape[0], wc), F32),
                   jax.ShapeDtypeStruct((N_DEV, dmeta.shape[0], mc), F32)],
        scratch_shapes=[pltpu.SemaphoreType.DMA((21,)), pltpu.SemaphoreType.DMA((21,)),
                        pltpu.SemaphoreType.DMA((2,))],
    )(loss, dfg, dlg, dgn, ddwb, dclg, dclb, dpwb, ddww, dmeta)


def _small_update(gs, gd, gm, weights, ms, vs):
    d = gs.shape[2]
    w = d // 2
    n = len(weights)

    def body(gs_ref, gd_ref, gm_ref, *refs):
        w_refs, m_refs, v_refs = refs[:n], refs[n:2 * n], refs[2 * n:3 * n]
        loss_ref = refs[3 * n]
        g_refs = refs[3 * n + 1:4 * n + 1]
        d_refs = refs[4 * n + 1:5 * n + 1]
        nm_refs = refs[5 * n + 1:6 * n + 1]
        nv_refs = refs[6 * n + 1:7 * n + 1]

        def total(ref):
            t = ref[0]
            for dev in range(1, N_DEV):
                t = t + ref[dev]
            return t

        packed = total(gs_ref)
        loss_ref[...] = jnp.sum(packed[0:1, :], axis=1, keepdims=True) * (0.5 / d)
        grads = [packed[2:3, :], packed[1:2, :], packed[3:4, 0:w], packed[3:4, w:2 * w], packed[4:5, 0:w],
                 packed[4:5, w:2 * w], packed[5:6, 0:w], total(gd_ref), total(gm_ref)]
        for k in range(n):
            g = grads[k]
            g_refs[k][...] = g
            d_refs[k][...], nm_refs[k][...], nv_refs[k][...] = _adamw(w_refs[k][...], g, m_refs[k][...], v_refs[k][...])

    def whole(shape):
        return pl.BlockSpec(shape, lambda i: (0,) * len(shape))

    shapes = [a.shape for a in weights]
    in_specs = [whole(gs.shape), whole(gd.shape), whole(gm.shape)] + [whole(s) for s in shapes] * 3
    out_specs = [whole((1, 1))] + [whole(s) for s in shapes] * 4
    out_shape = [jax.ShapeDtypeStruct((1, 1), F32)] + [jax.ShapeDtypeStruct(s, F32) for s in shapes] * 4
    outs = pl.pallas_call(
        body, name="small_update", grid=(1,), in_specs=in_specs, out_specs=out_specs, out_shape=out_shape,
        compiler_params=_params(1, 32))(gs, gd, gm, *weights, *ms, *vs)
    loss = outs[0]
    return loss, outs[1:n + 1], outs[n + 1:2 * n + 1], outs[2 * n + 1:3 * n + 1], outs[3 * n + 1:4 * n + 1]


def kernel(x, meta_tokens, ln_g, w_in, ret_gn_g, conv_dw_w, conv_dw_b, conv_ln_g, conv_ln_b, conv_pw_w, conv_pw_b, w_out, final_g, loss_target, m_meta_tokens, m_ln_g, m_w_in, m_ret_gn_g, m_conv_dw_w, m_conv_dw_b, m_conv_ln_g, m_conv_ln_b, m_conv_pw_w, m_conv_pw_b, m_w_out, m_final_g, v_meta_tokens, v_ln_g, v_w_in, v_ret_gn_g, v_conv_dw_w, v_conv_dw_b, v_conv_ln_g, v_conv_ln_b, v_conv_pw_w, v_conv_pw_b, v_w_out, v_final_g):
    seq, d = x.shape[1], x.shape[2]
    w = ret_gn_g.shape[1]
    hd = w // RET_HEADS
    lp = CHUNK + seq
    ns = w_in.shape[2]
    mx, my, mc = lax.axis_index("x"), lax.axis_index("y"), lax.axis_index("c")
    my_s = 2 * mx + my
    s_arr = my_s.astype(jnp.int32).reshape(1)
    pos = jnp.stack([mc, my_s]).astype(jnp.int32)
    order = jnp.stack([my_s, 2 * (1 - mx) + my, 2 * mx + (1 - my), 2 * (1 - mx) + (1 - my)]).astype(jnp.int32)

    dw_pad = jnp.pad(conv_dw_w[0], ((0, HALO - CONV_K), (0, 0)))
    dw4, meta4 = _gather_shards([dw_pad, meta_tokens])
    dw_full = dw4.transpose(1, 0, 2).reshape(HALO, w)
    meta_full = meta4.transpose(1, 0, 2).reshape(N_META, d)

    lead = jnp.concatenate([jnp.zeros((LEAD, d), F32), meta_full], axis=0)
    zero_lead = jnp.zeros((CHUNK, d), F32)
    consts = _ret_consts()
    rope = _rope_tables(lp // CHUNK, hd // 2)
    fg2 = final_g.reshape(1, d)

    proj, r1, hn, w4, pw4, wo4 = _in_proj_gather(
        order, x[0], lead, ln_g, _cast_into_gathered(w_in[0], s_arr, "cast_w_in"),
        _cast_into_gathered(conv_pw_w[0], s_arr, "cast_pw_w"), _cast_into_gathered(w_out[0], s_arr, "cast_w_out"))
    pw_full = pw4.reshape(w, w)
    wo_full = wo4.reshape(2 * w, d)
    y_ret, ssave, ret_ops, ret_scores, ret_out = _ret_fwd(proj, rope, ret_gn_g, consts)
    y_conv, u1 = _conv_fwd(proj, dw_full, conv_dw_b, conv_ln_g, conv_ln_b, pw_full, conv_pw_b)
    dh2, dy, loss_l, dfg = _out_proj_loss(y_ret, y_conv, x[0], lead, loss_target[0], zero_lead, wo_full, fg2)

    g_wo = _dw_out(y_ret, y_conv, dh2)
    dproj, dgn = _ret_bwd(proj, dy, ssave, ret_ops, ret_scores, ret_out, rope, ret_gn_g, consts)
    dproj, du1, g_pw, dpwb, dclg, dclb = _conv_bwd_pw(dy, proj, u1, conv_ln_g, conv_ln_b, pw_full, conv_pw_b, dproj)
    dproj, ddww, ddwb = _conv_bwd_dw(du1, proj, dw_full, dproj)
    g_wo4 = g_wo.reshape(N_CHIPS, (2 * w) // N_CHIPS, d)
    g_pw4 = g_pw.reshape(N_CHIPS, w // N_CHIPS, w)
    g_win, recv = _dw_in(hn, dproj, ns, [g_wo4, g_pw4])
    gs = [g_win, g_wo4, g_pw4]
    names = ("w_in", "w_out", "pw_w")
    sums = [_pair_sum(g, r, pos, "pair_sum_" + nm) for g, r, nm in zip(gs, recv, names)]
    dh, dlg, rb = _in_proj_bwd(dproj, w4, x[0], lead, r1, dh2, ln_g, [cs_ for cs_, _ in sums])
    grad_x = dh[CHUNK:][None]
    dmeta = dh[LEAD:CHUNK]
    fulls = [_chip_sum(own, r, pos, "chip_sum_" + nm) for (_, own), r, nm in zip(sums, rb, names)]
    grad_w_in, grad_w_out, grad_pw = _rs_pair_share(fulls)
    grad_w_in, d_win, nm_win, nv_win = _adamw_big(w_in[0], grad_w_in, m_w_in[0], v_w_in[0], "adamw_w_in")
    grad_w_out, d_wo, nm_wo, nv_wo = _adamw_big(w_out[0], grad_w_out, m_w_out[0], v_w_out[0], "adamw_w_out")
    grad_pw, d_pw, nm_pw, nv_pw = _adamw_big(conv_pw_w[0], grad_pw, m_conv_pw_w[0], v_conv_pw_w[0], "adamw_pw_w")

    gsm, gdm, gmm = _gather_small(loss_l, dfg, dlg, dgn, ddwb, dclg, dclb, dpwb, ddww, dmeta)

    def pad_dw(a):
        return jnp.pad(a[0], ((0, HALO - CONV_K), (0, 0)))

    small_w = [ln_g, fg2, ret_gn_g, conv_dw_b, conv_ln_g, conv_ln_b, conv_pw_b, dw_pad, meta_tokens]
    small_m = [m_ln_g, m_final_g.reshape(1, d), m_ret_gn_g, m_conv_dw_b, m_conv_ln_g, m_conv_ln_b, m_conv_pw_b,
               pad_dw(m_conv_dw_w), m_meta_tokens]
    small_v = [v_ln_g, v_final_g.reshape(1, d), v_ret_gn_g, v_conv_dw_b, v_conv_ln_g, v_conv_ln_b, v_conv_pw_b,
               pad_dw(v_conv_dw_w), v_meta_tokens]
    loss, sg, sd, snm, snv = _small_update(gsm, gdm, gmm, small_w, small_m, small_v)

    def assemble(small, big_in, big_pw, big_out):
        ln, fg, gn, dwb, clg, clb, pwb, dww, meta = small
        return (meta, ln, big_in[None], gn, dww[:CONV_K][None], dwb, clg, clb, big_pw[None], pwb, big_out[None],
                fg.reshape(d))

    return (loss.reshape(()), grad_x,
            *assemble(sg, grad_w_in, grad_pw, grad_w_out),
            *assemble(sd, d_win, d_pw, d_wo),
            *assemble(snm, nm_win, nm_pw, nm_wo),
            *assemble(snv, nv_win, nv_pw, nv_wo))
```
